```python
import math
import jax, jax.numpy as jnp
from jax import lax
import numpy as np

D_MODEL = 2048
BATCH = 4
SEQ = 2048
DEPTH = 2
DEC_BATCH = 128
DEC_SEQ = 8
PAST_LEN = 16384
PAGE_SIZE = 128

N_META = 16
N_EVEN = (DEPTH + 1) // 2
N_ODD = DEPTH // 2
EPS = 1e-6

S5_WIDTH = D_MODEL // 2
S5_GROUP = 16
S5_GROUPS = S5_WIDTH // S5_GROUP
S5_STATE = 64
HG_WIDTH = D_MODEL - S5_WIDTH
HG_HEADS = 8
HG_V = HG_WIDTH // HG_HEADS
HG_K = 128
HG_KEY = HG_HEADS * HG_K
HG_CHUNK = 16
EVEN_SPLITS = (S5_WIDTH, S5_WIDTH + HG_KEY, S5_WIDTH + 2 * HG_KEY, S5_WIDTH + 2 * HG_KEY + HG_WIDTH)
EVEN_IN = S5_WIDTH + 2 * HG_KEY + 2 * HG_WIDTH

RW_HEAD = 64
RW_HEADS = D_MODEL // RW_HEAD
W_LORA = max(32, int(round(1.8 * D_MODEL ** 0.5 / 32)) * 32)
A_LORA = max(32, int(round(1.8 * D_MODEL ** 0.5 / 32)) * 32)
G_LORA = max(32, int(round(0.6 * D_MODEL ** 0.8 / 32)) * 32)
RW_GN_EPS = 64e-5

D_FF = 11 * D_MODEL // 4
CONV_W = 3

kernel_name = 'hybrid_s5_hgrn2_rwkv7_convglu_step'


def _rmsnorm(x, w):
    xf = x.astype(jnp.float32)
    y = xf * lax.rsqrt(jnp.mean(xf * xf, axis=-1, keepdims=True) + EPS)
    return (y * w.astype(jnp.float32)).astype(x.dtype)


def _cplx_combine(e1, e2):
    a1r, a1i, b1r, b1i = e1
    a2r, a2i, b2r, b2i = e2
    return (a2r * a1r - a2i * a1i, a2r * a1i + a2i * a1r,
            a2r * b1r - a2i * b1i + b2r, a2r * b1i + a2i * b1r + b2i)


def _s5(u, h0_re, h0_im, lam_re, lam_im, log_step, b_re, b_im, c_re, c_im, d, w_glu):
    f32 = jnp.float32
    bsz, seq, _ = u.shape
    uf = u.astype(f32).reshape(bsz, seq, S5_GROUPS, S5_GROUP)
    lr = jnp.minimum(lam_re.astype(f32), -1e-4)
    li = lam_im.astype(f32)
    dt = jnp.exp(log_step.astype(f32))[:, None]
    mag = jnp.exp(lr * dt)
    ar, ai = mag * jnp.cos(li * dt), mag * jnp.sin(li * dt)
    den = lr * lr + li * li
    zr = ((ar - 1.0) * lr + ai * li) / den
    zi = (ai * lr - (ar - 1.0) * li) / den
    br, bi = b_re.astype(f32), b_im.astype(f32)
    bbr = zr[..., None] * br - zi[..., None] * bi
    bbi = zr[..., None] * bi + zi[..., None] * br
    bur = jnp.einsum('blgc,gpc->blgp', uf, bbr)
    bui = jnp.einsum('blgc,gpc->blgp', uf, bbi)
    h0r, h0i = h0_re.astype(f32), h0_im.astype(f32)
    bur = bur.at[:, 0].add(ar * h0r - ai * h0i)
    bui = bui.at[:, 0].add(ar * h0i + ai * h0r)
    shape = bur.shape
    _, _, xr, xi = lax.associative_scan(
        _cplx_combine, (jnp.broadcast_to(ar, shape), jnp.broadcast_to(ai, shape), bur, bui), axis=1)
    y = (jnp.einsum('blgp,gcp->blgc', xr, c_re.astype(f32))
         - jnp.einsum('blgp,gcp->blgc', xi, c_im.astype(f32))
         + d.astype(f32) * uf)
    y = jax.nn.gelu(y.reshape(bsz, seq, S5_WIDTH))
    y = y * jax.nn.sigmoid(y @ w_glu.astype(f32))
    return y, xr[:, -1], xi[:, -1]


def _hgrn2(q, f, i, g, s0, lb, norm_w, front):
    f32 = jnp.float32
    bsz, seq, _ = q.shape
    lbf = lb.astype(f32)
    fg = lbf + (1.0 - lbf) * jax.nn.sigmoid(f.astype(f32))
    qh = jax.nn.silu(q.astype(f32)).reshape(bsz, seq, HG_HEADS, HG_K)
    kh = (1.0 - fg).reshape(bsz, seq, HG_HEADS, HG_K)
    lfh = jnp.log(fg).reshape(bsz, seq, HG_HEADS, HG_K)
    vh = i.astype(f32).reshape(bsz, seq, HG_HEADS, HG_V)
    back = (-(front + seq)) % HG_CHUNK
    n_chunks = (front + seq + back) // HG_CHUNK

    def chunks(t):
        t = jnp.pad(t, ((0, 0), (front, back), (0, 0), (0, 0)))
        t = t.reshape(bsz, n_chunks, HG_CHUNK, HG_HEADS, t.shape[-1])
        return t.transpose(1, 0, 3, 2, 4)

    qc, kc, lfc, vc = chunks(qh), chunks(kh), chunks(lfh), chunks(vh)
    bcum = jnp.cumsum(lfc, axis=3)
    btot = bcum[:, :, :, -1:, :]
    q_in = qc * jnp.exp(bcum)
    k_in = kc * jnp.exp(-bcum)
    k_end = kc * jnp.exp(btot - bcum)
    decay = jnp.exp(btot[:, :, :, 0, :])
    mask = jnp.tril(jnp.ones((HG_CHUNK, HG_CHUNK), f32))

    def step(S, xs):
        qi, ki, ke, vv, dc = xs
        att = jnp.einsum('bhtk,bhsk->bhts', qi, ki) * mask
        o = jnp.einsum('bhtk,bhkv->bhtv', qi, S) + jnp.einsum('bhts,bhsv->bhtv', att, vv)
        S = S * dc[..., None] + jnp.einsum('bhsk,bhsv->bhkv', ke, vv)
        return S, o

    s_fin, o = lax.scan(step, s0.astype(f32), (q_in, k_in, k_end, vc, decay))
    o = o.transpose(1, 0, 3, 2, 4).reshape(bsz, n_chunks * HG_CHUNK, HG_HEADS, HG_V)[:, front:front + seq]
    o = o * lax.rsqrt(jnp.mean(o * o, axis=-1, keepdims=True) + EPS) * norm_w.astype(f32)
    o = o.reshape(bsz, seq, HG_WIDTH) * jax.nn.silu(g.astype(f32))
    return o, s_fin


def _rwkv7(xn, shift0, s0, mu, w0, w1, w2, a0, a1, a2, g1, g2, k_k, k_a, r_k, w_r, w_k, w_v, w_o, ln_w, ln_b):
    f32 = jnp.float32
    bsz, seq, _ = xn.shape
    prev = jnp.concatenate([shift0[:, None].astype(xn.dtype), xn[:, :-1]], axis=1)
    xx = prev - xn
    xr, xw, xk, xv, xa, xg = (xn + xx * mu[j] for j in range(6))

    def heads(t):
        return t.astype(f32).reshape(bsz, seq, RW_HEADS, RW_HEAD)

    r = heads(xr @ w_r)
    wlog = -jax.nn.softplus(-(w0.astype(f32) + (jnp.tanh(xw @ w1) @ w2).astype(f32))) - 0.5
    decay = heads(jnp.exp(-jnp.exp(wlog)))
    k = heads(xk @ w_k)
    v = heads(xv @ w_v)
    a = heads(jax.nn.sigmoid(a0.astype(f32) + ((xa @ a1) @ a2).astype(f32)))
    g = (jax.nn.sigmoid(xg @ g1) @ g2).astype(f32)
    kk = k * k_k.astype(f32).reshape(RW_HEADS, RW_HEAD)
    kk = kk / jnp.maximum(jnp.sqrt(jnp.sum(kk * kk, axis=-1, keepdims=True)), 1e-12)
    k = k * (1.0 + (a - 1.0) * k_a.astype(f32).reshape(RW_HEADS, RW_HEAD))

    def tm(t):
        return jnp.swapaxes(t, 0, 1)

    def step(S, xs):
        rt, wt, kt, vt, at, bt = xs
        sa = jnp.einsum('bhvk,bhk->bhv', S, at)
        S = S * wt[:, :, None, :] + sa[..., None] * bt[:, :, None, :] + vt[..., None] * kt[:, :, None, :]
        return S, jnp.einsum('bhvk,bhk->bhv', S, rt)

    s_fin, y = lax.scan(step, s0.astype(f32), (tm(r), tm(decay), tm(k), tm(v), tm(-kk), tm(kk * a)))
    y = tm(y)
    mean = jnp.mean(y, axis=-1, keepdims=True)
    var = jnp.mean(jnp.square(y - mean), axis=-1, keepdims=True)
    y = ((y - mean) * lax.rsqrt(var + RW_GN_EPS) * ln_w.astype(f32).reshape(RW_HEADS, RW_HEAD)
         + ln_b.astype(f32).reshape(RW_HEADS, RW_HEAD))
    y = y + jnp.sum(r * k * r_k.astype(f32), axis=-1, keepdims=True) * v
    out = (y.reshape(bsz, seq, D_MODEL) * g).astype(xn.dtype) @ w_o
    return out, s_fin, xn[:, -1]


def _conv_ffn(xn, conv0, w_in, conv_w, conv_b, w_down):
    seq = xn.shape[1]
    h = xn @ w_in
    a, v = h[..., :D_FF], h[..., D_FF:]
    full = jnp.concatenate([conv0.astype(a.dtype), a], axis=1)
    c = conv_b + conv_w[0] * full[:, 0:seq]
    for j in range(1, CONV_W):
        c = c + conv_w[j] * full[:, j:j + seq]
    out = (jax.nn.gelu(c) * v) @ w_down
    return out, full[:, seq:]


def setup_inputs(seed: int = 0) -> dict:
    keys = jax.random.split(jax.random.key(seed), 64)
    it = iter(range(64))

    def nrm(shape, scale=1.0):
        return jax.random.normal(keys[next(it)], shape, jnp.float32) * scale

    def uni(shape, lo, hi):
        return jax.random.uniform(keys[next(it)], shape, jnp.float32, lo, hi)

    D = D_MODEL
    ne, no = N_EVEN, N_ODD
    ramp = jnp.arange(D, dtype=jnp.float32) / (D - 1)
    return {
        'x_prompt': nrm((BATCH, SEQ, D)),
        'x_sample': nrm((DEC_BATCH, DEC_SEQ, D)),
        'state_s5_re': nrm((ne, DEC_BATCH, S5_GROUPS, S5_STATE), 0.5),
        'state_s5_im': nrm((ne, DEC_BATCH, S5_GROUPS, S5_STATE), 0.5),
        'state_hgrn': nrm((ne, DEC_BATCH, HG_HEADS, HG_K, HG_V), 0.5),
        'state_rwkv': nrm((no, DEC_BATCH, RW_HEADS, RW_HEAD, RW_HEAD), 0.2),
        'state_shift': nrm((no, DEC_BATCH, D)),
        'state_conv': nrm((DEPTH, DEC_BATCH, CONV_W - 1, D_FF)),
        'meta_tokens': nrm((N_META, D)),
        'ln_mix': 1.0 + nrm((DEPTH, D), 0.02),
        'ln_ffn': 1.0 + nrm((DEPTH, D), 0.02),
        'ln_final': 1.0 + nrm((D,), 0.02),
        'ev_w_in': nrm((ne, D, EVEN_IN), D ** -0.5),
        'ev_w_out': nrm((ne, D, D), D ** -0.5),
        's5_lam_re': -0.5 + nrm((ne, S5_GROUPS, S5_STATE), 0.01),
        's5_lam_im': jnp.pi * jnp.arange(S5_STATE, dtype=jnp.float32) + nrm((ne, S5_GROUPS, S5_STATE), 0.01),
        's5_log_step': uni((ne, S5_GROUPS), math.log(1e-3), math.log(1e-1)),
        's5_b_re': nrm((ne, S5_GROUPS, S5_STATE, S5_GROUP), (2 * S5_GROUP) ** -0.5),
        's5_b_im': nrm((ne, S5_GROUPS, S5_STATE, S5_GROUP), (2 * S5_GROUP) ** -0.5),
        's5_c_re': nrm((ne, S5_GROUPS, S5_GROUP, S5_STATE), S5_STATE ** -0.5),
        's5_c_im': nrm((ne, S5_GROUPS, S5_GROUP, S5_STATE), S5_STATE ** -0.5),
        's5_d': nrm((ne, S5_GROUPS, S5_GROUP)),
        's5_w_glu': nrm((ne, S5_WIDTH, S5_WIDTH), S5_WIDTH ** -0.5),
        'hg_lb': nrm((ne + 1, HG_KEY), 0.1),
        'hg_norm_w': 1.0 + nrm((ne, HG_V), 0.02),
        'rw_mu': uni((no, 6, D), 0.0, 1.0),
        'rw_w0': -5.5 + 5.0 * ramp ** 0.85 + nrm((no, D), 0.01),
        'rw_w1': nrm((no, D, W_LORA), D ** -0.5),
        'rw_w2': nrm((no, W_LORA, D), 0.1 * W_LORA ** -0.5),
        'rw_a0': nrm((no, D), 0.1),
        'rw_a1': nrm((no, D, A_LORA), D ** -0.5),
        'rw_a2': nrm((no, A_LORA, D), A_LORA ** -0.5),
        'rw_g1': nrm((no, D, G_LORA), D ** -0.5),
        'rw_g2': nrm((no, G_LORA, D), G_LORA ** -0.5),
        'rw_k_k': 0.85 + nrm((no, D), 0.02),
        'rw_k_a': 1.0 + nrm((no, D), 0.02),
        'rw_r_k': -0.04 + nrm((no, RW_HEADS, RW_HEAD), 0.1),
        'rw_w_r': nrm((no, D, D), D ** -0.5),
        'rw_w_k': nrm((no, D, D), D ** -0.5),
        'rw_w_v': nrm((no, D, D), D ** -0.5),
        'rw_w_o': nrm((no, D, D), D ** -0.5),
        'rw_ln_w': 1.0 + nrm((no, D), 0.02),
        'rw_ln_b': nrm((no, D), 0.02),
        'ffn_w_in': nrm((DEPTH, D, 2 * D_FF), D ** -0.5),
        'ffn_conv_w': nrm((DEPTH, CONV_W, D_FF), 0.5),
        'ffn_conv_b': nrm((DEPTH, D_FF), 0.02),
        'ffn_w_down': nrm((DEPTH, D_FF, D), D_FF ** -0.5),
    }


def reference(x_prompt, x_sample, state_s5_re, state_s5_im, state_hgrn, state_rwkv, state_shift, state_conv,
              meta_tokens, ln_mix, ln_ffn, ln_final, ev_w_in, ev_w_out,
              s5_lam_re, s5_lam_im, s5_log_step, s5_b_re, s5_b_im, s5_c_re, s5_c_im, s5_d, s5_w_glu,
              hg_lb, hg_norm_w,
              rw_mu, rw_w0, rw_w1, rw_w2, rw_a0, rw_a1, rw_a2, rw_g1, rw_g2, rw_k_k, rw_k_a, rw_r_k,
              rw_w_r, rw_w_k, rw_w_v, rw_w_o, rw_ln_w, rw_ln_b,
              ffn_w_in, ffn_conv_w, ffn_conv_b, ffn_w_down):
    lb_all = jnp.cumsum(jax.nn.softmax(hg_lb.astype(jnp.float32), axis=0), axis=0)

    def trunk(h, s5r, s5i, hg, rw, sh, cv, front):
        n_s5r, n_s5i, n_hg, n_rw, n_sh, n_cv = [], [], [], [], [], []
        for l in range(DEPTH):
            xn = _rmsnorm(h, ln_mix[l])
            if l % 2 == 0:
                e = l // 2
                z = xn @ ev_w_in[e]
                u, q, f, i, g = jnp.split(z, EVEN_SPLITS, axis=-1)
                ya, nr, ni = _s5(u, s5r[e], s5i[e], s5_lam_re[e], s5_lam_im[e], s5_log_step[e],
                                 s5_b_re[e], s5_b_im[e], s5_c_re[e], s5_c_im[e], s5_d[e], s5_w_glu[e])
                yb, nh = _hgrn2(q, f, i, g, hg[e], lb_all[e], hg_norm_w[e], front)
                mix = jnp.concatenate([ya, yb], axis=-1).astype(h.dtype) @ ev_w_out[e]
                n_s5r.append(nr)
                n_s5i.append(ni)
                n_hg.append(nh)
            else:
                o = l // 2
                mix, ns, nshift = _rwkv7(xn, sh[o], rw[o], rw_mu[o], rw_w0[o], rw_w1[o], rw_w2[o],
                                         rw_a0[o], rw_a1[o], rw_a2[o], rw_g1[o], rw_g2[o],
                                         rw_k_k[o], rw_k_a[o], rw_r_k[o], rw_w_r[o], rw_w_k[o],
                                         rw_w_v[o], rw_w_o[o], rw_ln_w[o], rw_ln_b[o])
                n_rw.append(ns)
                n_sh.append(nshift)
            h = h + mix
            xn = _rmsnorm(h, ln_ffn[l])
            ff, nc = _conv_ffn(xn, cv[l], ffn_w_in[l], ffn_conv_w[l], ffn_conv_b[l], ffn_w_down[l])
            n_cv.append(nc)
            h = h + ff
        return (_rmsnorm(h, ln_final), jnp.stack(n_s5r), jnp.stack(n_s5i), jnp.stack(n_hg),
                jnp.stack(n_rw), jnp.stack(n_sh), jnp.stack(n_cv))

    bsz = x_prompt.shape[0]
    f32 = jnp.float32
    hp = jnp.concatenate([jnp.broadcast_to(meta_tokens.astype(x_prompt.dtype)[None], (bsz, N_META, D_MODEL)),
                          x_prompt], axis=1)
    yp, p_s5r, p_s5i, p_hg, p_rw, p_sh, p_cv = trunk(
        hp,
        jnp.zeros((N_EVEN, bsz, S5_GROUPS, S5_STATE), f32),
        jnp.zeros((N_EVEN, bsz, S5_GROUPS, S5_STATE), f32),
        jnp.zeros((N_EVEN, bsz, HG_HEADS, HG_K, HG_V), f32),
        jnp.zeros((N_ODD, bsz, RW_HEADS, RW_HEAD, RW_HEAD), f32),
        jnp.zeros((N_ODD, bsz, D_MODEL), x_prompt.dtype),
        jnp.zeros((DEPTH, bsz, CONV_W - 1, D_FF), x_prompt.dtype),
        (-N_META) % HG_CHUNK)
    y_prompt = yp[:, N_META:]

    y_sample, s_s5r, s_s5i, s_hg, s_rw, s_sh, s_cv = trunk(
        x_sample, state_s5_re, state_s5_im, state_hgrn, state_rwkv, state_shift, state_conv,
        (PAST_LEN + (-N_META) % HG_CHUNK) % HG_CHUNK)

    return (y_prompt, y_sample, p_s5r, p_s5i, p_hg, p_rw, p_sh, p_cv, s_s5r, s_s5i, s_hg, s_rw, s_sh, s_cv)
```

```python
import functools

import jax
import jax.numpy as jnp
from jax import lax
from jax.experimental import pallas as pl
from jax.experimental.pallas import tpu as pltpu

F32 = jnp.float32
BF16 = jnp.bfloat16

D_MODEL = 2048
N_META = 16
EPS = 1e-6
S5_WIDTH = 1024
S5_GROUP = 16
S5_GROUPS = 64
S5_STATE = 64
S5_CH = S5_GROUPS * S5_STATE
HG_HEADS = 8
HG_K = 128
HG_V = 128
HG_CHUNK = 16
EVEN_IN = 5120
RW_HEAD = 64
RW_HEADS = 32
RW_GN_EPS = 64e-5
D_FF = 5632
CONV_W = 3

LANES = 128
SUBLANES = 8
VMEM_LIMIT = 56 * 1024 * 1024


def _params(*sem):
    return pltpu.CompilerParams(dimension_semantics=sem, vmem_limit_bytes=VMEM_LIMIT)


def _row_tile(t, cap=1024):
    best = None
    for d in range(16, min(t, cap) + 1, 16):
        if t % d == 0:
            best = d
    assert best is not None, t
    return best


def _col_tile(n, cap=512):
    if n <= cap:
        return n
    best = None
    for d in range(LANES, cap + 1, LANES):
        if n % d == 0:
            best = d
    assert best is not None, n
    return best


def _rms_kernel(x_ref, w_ref, *o_refs):
    x = x_ref[...]
    y = x * lax.rsqrt(jnp.mean(x * x, axis=-1, keepdims=True) + EPS) * w_ref[...]
    for o_ref in o_refs:
        o_ref[...] = y.astype(o_ref.dtype)


def rmsnorm(x, w, dtypes):
    t, d = x.shape
    tm = _row_tile(t)
    spec = pl.BlockSpec((tm, d), lambda i: (i, 0))
    outs = pl.pallas_call(
        _rms_kernel,
        grid=(t // tm,),
        in_specs=[spec, pl.BlockSpec((1, d), lambda i: (0, 0))],
        out_specs=[spec for _ in dtypes],
        out_shape=[jax.ShapeDtypeStruct((t, d), dt) for dt in dtypes],
        compiler_params=_params("parallel"),
        name="rmsnorm",
    )(x, w.reshape(1, d))
    return outs


def _act(x, act):
    if act == "tanh":
        return jnp.tanh(x)
    if act == "sigmoid":
        return jax.nn.sigmoid(x)
    assert act is None
    return x


def _mm_kernel(*refs, n_a, act, epilogue):
    a_refs = refs[:n_a]
    w_refs = refs[n_a:2 * n_a]
    rest = refs[2 * n_a:]
    o_ref = rest[-1]
    acc = jnp.dot(a_refs[0][...], w_refs[0][...], preferred_element_type=F32)
    for a_ref, w_ref in zip(a_refs[1:], w_refs[1:]):
        acc = acc + jnp.dot(a_ref[...], w_ref[...], preferred_element_type=F32)
    acc = _act(acc, act)
    if epilogue == "residual":
        acc = rest[0][...] + acc
    elif epilogue == "glu":
        acc = rest[0][...] * jax.nn.sigmoid(acc)
    o_ref[...] = acc.astype(o_ref.dtype)


def matmul(a_list, w_list, out_dtype, act=None, epilogue=None, extra=None):
    t = a_list[0].shape[0]
    n = w_list[0].shape[1]
    tm = _row_tile(t)
    tn = _col_tile(n)
    in_specs = [pl.BlockSpec((tm, a.shape[1]), lambda i, j: (i, 0)) for a in a_list]
    in_specs += [pl.BlockSpec((w.shape[0], tn), lambda i, j: (0, j)) for w in w_list]
    args = list(a_list) + list(w_list)
    if epilogue is not None:
        in_specs.append(pl.BlockSpec((tm, tn), lambda i, j: (i, j)))
        args.append(extra)
    return pl.pallas_call(
        functools.partial(_mm_kernel, n_a=len(a_list), act=act, epilogue=epilogue),
        grid=(t // tm, n // tn),
        in_specs=in_specs,
        out_specs=pl.BlockSpec((tm, tn), lambda i, j: (i, j)),
        out_shape=jax.ShapeDtypeStruct((t, n), out_dtype),
        compiler_params=_params("parallel", "parallel"),
        name="matmul",
    )(*args)


def _s5_prep_kernel(lr_ref, li_ref, ls_ref, brt_ref, bit_ref, pwr_ref, pwi_ref, bbr_ref, bbi_ref):
    lr = jnp.minimum(lr_ref[...], -1e-4)
    li = li_ref[...]
    dt = jnp.exp(ls_ref[...])
    n = lax.broadcasted_iota(jnp.int32, (SUBLANES, S5_CH), 0).astype(F32) + 1.0
    mag = jnp.exp(n * (lr * dt))
    ang = n * (li * dt)
    pwr = mag * jnp.cos(ang)
    pwi = mag * jnp.sin(ang)
    pwr_ref[...] = pwr
    pwi_ref[...] = pwi
    ar = pwr[0:1]
    ai = pwi[0:1]
    den = lr * lr + li * li
    zr = ((ar - 1.0) * lr + ai * li) / den
    zi = (ai * lr - (ar - 1.0) * li) / den
    br = brt_ref[...]
    bi = bit_ref[...]
    bbr_ref[...] = zr * br - zi * bi
    bbi_ref[...] = zr * bi + zi * br


def s5_prep(lam_re, lam_im, log_step, b_re, b_im):
    lr = lam_re.reshape(1, S5_CH)
    li = lam_im.reshape(1, S5_CH)
    ls = jnp.broadcast_to(log_step[:, None], (S5_GROUPS, S5_STATE)).reshape(1, S5_CH)
    brt = b_re.reshape(S5_CH, S5_GROUP).T
    bit = b_im.reshape(S5_CH, S5_GROUP).T
    return pl.pallas_call(
        _s5_prep_kernel,
        out_shape=[jax.ShapeDtypeStruct((SUBLANES, S5_CH), F32)] * 2
        + [jax.ShapeDtypeStruct((S5_GROUP, S5_CH), F32)] * 2,
        name="s5_prep",
    )(lr, li, ls, brt, bit)


S5_BLK_GROUPS = LANES // S5_GROUP
S5_BLKS = S5_WIDTH // LANES
S5_BLK_CH = S5_BLK_GROUPS * S5_STATE


def _cmul_add(xr, xi, mr, mi, sr, si):
    return xr + mr * sr - mi * si, xi + mr * si + mi * sr


def _s5_kernel(u_ref, h0r_ref, h0i_ref, pwr_ref, pwi_ref, wbr_ref, wbi_ref, wcr_ref, wci_ref, d_ref,
               y_ref, hr_ref, hi_ref, xr_scr, xi_scr, *, nb, seq):
    u2 = u_ref[...].reshape(nb * seq, LANES)
    ub = u2.astype(BF16)
    xr_scr[...] = jnp.dot(ub, wbr_ref[0], preferred_element_type=F32).reshape(nb, seq, S5_BLK_CH)
    xi_scr[...] = jnp.dot(ub, wbi_ref[0], preferred_element_type=F32).reshape(nb, seq, S5_BLK_CH)

    pwr = pwr_ref[...]
    pwi = pwi_ref[...]
    row = lax.broadcasted_iota(jnp.int32, (SUBLANES, S5_BLK_CH), 0)
    steps = []
    for d in (1, 2, 4):
        keep = row >= d
        steps.append((d, jnp.where(keep, pwr[d - 1:d], 0.0)[None], jnp.where(keep, pwi[d - 1:d], 0.0)[None]))
    pr = pwr[None]
    pi = pwi[None]

    def tile(i, carry):
        cr, ci = carry
        o = pl.multiple_of(i * SUBLANES, SUBLANES)
        xr = xr_scr[:, pl.ds(o, SUBLANES), :]
        xi = xi_scr[:, pl.ds(o, SUBLANES), :]
        for d, mr, mi in steps:
            sr = pltpu.roll(xr, d, axis=1)
            si = pltpu.roll(xi, d, axis=1)
            xr, xi = _cmul_add(xr, xi, mr, mi, sr, si)
        xr, xi = _cmul_add(xr, xi, pr, pi, cr, ci)
        xr_scr[:, pl.ds(o, SUBLANES), :] = xr
        xi_scr[:, pl.ds(o, SUBLANES), :] = xi
        return xr[:, SUBLANES - 1:SUBLANES, :], xi[:, SUBLANES - 1:SUBLANES, :]

    hr, hi = lax.fori_loop(0, seq // SUBLANES, tile, (h0r_ref[...], h0i_ref[...]))
    hr_ref[...] = hr
    hi_ref[...] = hi

    xr = xr_scr[...].reshape(nb * seq, S5_BLK_CH).astype(BF16)
    xi = xi_scr[...].reshape(nb * seq, S5_BLK_CH).astype(BF16)
    y = (jnp.dot(xr, wcr_ref[0], preferred_element_type=F32)
         - jnp.dot(xi, wci_ref[0], preferred_element_type=F32)
         + d_ref[...] * u2)
    y_ref[...] = jax.nn.gelu(y).reshape(nb, seq, LANES)


def s5_scan(z3, h0r, h0i, pwr, pwi, wbr, wbi, wcr, wci, d, nb):
    bsz, seq, _ = z3.shape
    assert seq % SUBLANES == 0 and bsz % nb == 0
    seq_blk = pl.BlockSpec((nb, seq, LANES), lambda b, k: (b, 0, k))
    st_blk = pl.BlockSpec((nb, 1, S5_BLK_CH), lambda b, k: (b, 0, k))
    pw_blk = pl.BlockSpec((SUBLANES, S5_BLK_CH), lambda b, k: (0, k))
    wb_blk = pl.BlockSpec((1, LANES, S5_BLK_CH), lambda b, k: (k, 0, 0))
    wc_blk = pl.BlockSpec((1, S5_BLK_CH, LANES), lambda b, k: (k, 0, 0))
    y, hr, hi = pl.pallas_call(
        functools.partial(_s5_kernel, nb=nb, seq=seq),
        grid=(bsz // nb, S5_BLKS),
        in_specs=[seq_blk, st_blk, st_blk, pw_blk, pw_blk, wb_blk, wb_blk, wc_blk, wc_blk,
                  pl.BlockSpec((1, LANES), lambda b, k: (0, k))],
        out_specs=[seq_blk, st_blk, st_blk],
        out_shape=[jax.ShapeDtypeStruct((bsz, seq, S5_WIDTH), F32),
                   jax.ShapeDtypeStruct((bsz, 1, S5_CH), F32),
                   jax.ShapeDtypeStruct((bsz, 1, S5_CH), F32)],
        scratch_shapes=[pltpu.VMEM((nb, seq, S5_BLK_CH), F32), pltpu.VMEM((nb, seq, S5_BLK_CH), F32)],
        compiler_params=_params("parallel", "parallel"),
        name="s5_scan",
    )(z3, h0r.reshape(bsz, 1, S5_CH), h0i.reshape(bsz, 1, S5_CH), pwr, pwi, wbr, wbi, wcr, wci, d)
    return y, hr.reshape(bsz, S5_GROUPS, S5_STATE), hi.reshape(bsz, S5_GROUPS, S5_STATE)


def _s5_block_weights(bbr_t, bbi_t, c_re, c_im):
    eye = jnp.eye(S5_BLK_GROUPS, dtype=F32)

    def wb(bt):
        b4 = bt.reshape(S5_GROUP, S5_BLKS, S5_BLK_GROUPS, S5_STATE)
        w = jnp.einsum("cbgp,hg->bhcgp", b4, eye)
        return w.reshape(S5_BLKS, LANES, S5_BLK_CH).astype(BF16)

    def wc(c):
        c4 = c.reshape(S5_BLKS, S5_BLK_GROUPS, S5_GROUP, S5_STATE)
        w = jnp.einsum("bgcp,hg->bhpgc", c4, eye)
        return w.reshape(S5_BLKS, S5_BLK_CH, LANES).astype(BF16)

    return wb(bbr_t), wb(bbi_t), wc(c_re), wc(c_im)


def _hg_lb_kernel(x_ref, o_ref):
    x = x_ref[...]
    e = jnp.exp(x - jnp.max(x, axis=0, keepdims=True))
    sm = e / jnp.sum(e, axis=0, keepdims=True)
    acc = sm[0:1]
    o_ref[0:1, :] = acc
    for l in range(1, x.shape[0]):
        acc = acc + sm[l:l + 1]
        o_ref[l:l + 1, :] = acc


def hg_lower_bounds(hg_lb):
    return pl.pallas_call(_hg_lb_kernel, out_shape=jax.ShapeDtypeStruct(hg_lb.shape, F32), name="hg_lb")(hg_lb)


def _cumsum_rows(x, n):
    row = lax.broadcasted_iota(jnp.int32, x.shape, 1)
    d = 1
    while d < n:
        x = x + jnp.where(row >= d, pltpu.roll(x, d, axis=1), 0.0)
        d *= 2
    return x


def _hgrn_kernel(q_ref, f_ref, i_ref, g_ref, s0_ref, lb_ref, nw_ref, o_ref, sf_ref, st_scr, *, nb, seq, chunk):
    lb = lb_ref[...][None]
    nw = nw_ref[...][None]
    for b in range(nb):
        st_scr[b] = s0_ref[b, 0].T
    trow = lax.broadcasted_iota(jnp.int32, (chunk, chunk), 0)
    tcol = lax.broadcasted_iota(jnp.int32, (chunk, chunk), 1)
    causal = (tcol <= trow)[None]

    def body(n, carry):
        o = pl.multiple_of(n * chunk, chunk)
        q = q_ref[:, pl.ds(o, chunk), :]
        f = f_ref[:, pl.ds(o, chunk), :]
        v = i_ref[:, pl.ds(o, chunk), :]
        g = g_ref[:, pl.ds(o, chunk), :]
        fg = lb + (1.0 - lb) * jax.nn.sigmoid(f)
        qh = jax.nn.silu(q)
        kh = 1.0 - fg
        bcum = _cumsum_rows(jnp.log(fg), chunk)
        btot = bcum[:, chunk - 1:chunk, :]
        q_in = (qh * jnp.exp(bcum)).astype(BF16)
        k_in = (kh * jnp.exp(-bcum)).astype(BF16)
        k_end = (kh * jnp.exp(btot - bcum)).astype(BF16)
        decay = jnp.exp(btot)
        vb = v.astype(BF16)
        st = st_scr[...]
        att = jnp.einsum("btk,bsk->bts", q_in, k_in, preferred_element_type=F32)
        att = jnp.where(causal, att, 0.0).astype(BF16)
        out = (jnp.einsum("btk,bvk->btv", q_in, st.astype(BF16), preferred_element_type=F32)
               + jnp.einsum("bts,bsv->btv", att, vb, preferred_element_type=F32))
        st_scr[...] = st * decay + jnp.einsum("bsv,bsk->bvk", vb, k_end, preferred_element_type=F32)
        out = out * lax.rsqrt(jnp.mean(out * out, axis=-1, keepdims=True) + EPS) * nw
        o_ref[:, pl.ds(o, chunk), :] = (out * jax.nn.silu(g)).astype(o_ref.dtype)
        return carry

    lax.fori_loop(0, seq // chunk, body, 0)
    for b in range(nb):
        sf_ref[b, 0] = st_scr[b].T


def hgrn2(z3, s0, lb, norm_w, nb):
    bsz, seq, _ = z3.shape
    chunk = min(HG_CHUNK, seq)
    assert seq % chunk == 0 and bsz % nb == 0

    def col(base):
        return pl.BlockSpec((nb, seq, LANES), lambda b, h, base=base: (b, 0, base + h))

    st_blk = pl.BlockSpec((nb, 1, HG_K, HG_V), lambda b, h: (b, h, 0, 0))
    vec = pl.BlockSpec((1, LANES), lambda b, h: (0, h))
    out, sf = pl.pallas_call(
        functools.partial(_hgrn_kernel, nb=nb, seq=seq, chunk=chunk),
        grid=(bsz // nb, HG_HEADS),
        in_specs=[col(8), col(16), col(24), col(32), st_blk, vec,
                  pl.BlockSpec((1, LANES), lambda b, h: (0, 0))],
        out_specs=[pl.BlockSpec((nb, seq, LANES), lambda b, h: (b, 0, h)), st_blk],
        out_shape=[jax.ShapeDtypeStruct((bsz, seq, HG_HEADS * HG_V), BF16),
                   jax.ShapeDtypeStruct((bsz, HG_HEADS, HG_K, HG_V), F32)],
        scratch_shapes=[pltpu.VMEM((nb, HG_V, HG_K), F32)],
        compiler_params=_params("parallel", "parallel"),
        name="hgrn2",
    )(z3, z3, z3, z3, s0, lb.reshape(1, HG_HEADS * HG_K), norm_w.reshape(1, HG_V))
    return out, sf


def _mix_kernel(x_ref, p_ref, mu_ref, *o_refs):
    x = x_ref[...]
    xx = p_ref[...] - x
    for j, o_ref in enumerate(o_refs):
        o_ref[...] = (x + xx * mu_ref[j:j + 1, :]).astype(o_ref.dtype)


def rwkv_mix(xn, prev, mu):
    t, d = xn.shape
    tm = _row_tile(t, cap=512)
    spec = pl.BlockSpec((tm, d), lambda i: (i, 0))
    return pl.pallas_call(
        _mix_kernel,
        grid=(t // tm,),
        in_specs=[spec, spec, pl.BlockSpec((6, d), lambda i: (0, 0))],
        out_specs=[spec] * 6,
        out_shape=[jax.ShapeDtypeStruct((t, d), BF16)] * 6,
        compiler_params=_params("parallel"),
        name="rwkv_mix",
    )(xn, prev, mu)


def _split_heads(x, nb, hb):
    parts = [x[:, :, h * RW_HEAD:(h + 1) * RW_HEAD] for h in range(hb)]
    c = x.shape[1]
    return jnp.stack(parts, axis=1).reshape(nb * hb, c, RW_HEAD)


def _merge_heads(x, nb, hb):
    c = x.shape[1]
    x4 = x.reshape(nb, hb, c, RW_HEAD)
    return jnp.concatenate([x4[:, h] for h in range(hb)], axis=-1)


def _head_rows(ref, nb, hb):
    p = ref[...]
    parts = jnp.stack([p[:, h * RW_HEAD:(h + 1) * RW_HEAD] for h in range(hb)], axis=0)
    return jnp.broadcast_to(parts[None], (nb, hb, 1, RW_HEAD)).reshape(nb * hb, 1, RW_HEAD)


def _rwkv_kernel(r_ref, k_ref, v_ref, wl_ref, al_ref, g_ref, s0_ref,
                 w0_ref, a0_ref, kk_ref, ka_ref, rk_ref, lnw_ref, lnb_ref,
                 o_ref, sf_ref, s_scr, *, nb, hb, seq, chunk):
    ng = nb * hb
    w0, a0, k_k, k_a, r_k, ln_w, ln_b = (_head_rows(p, nb, hb) for p in
                                         (w0_ref, a0_ref, kk_ref, ka_ref, rk_ref, lnw_ref, lnb_ref))
    s_scr[...] = s0_ref[...].reshape(ng, RW_HEAD, RW_HEAD)
    trow = lax.broadcasted_iota(jnp.int32, (chunk, chunk), 0)
    tcol = lax.broadcasted_iota(jnp.int32, (chunk, chunk), 1)
    strict = (tcol < trow)[None]
    incl = (tcol <= trow)[None]

    def load(ref, o):
        return _split_heads(ref[:, pl.ds(o, chunk), :], nb, hb)

    def bdot(spec, a, b):
        return jnp.einsum(spec, a.astype(BF16), b.astype(BF16), preferred_element_type=F32)

    def body(n, carry):
        o = pl.multiple_of(n * chunk, chunk)
        r, k, v, wl, al, g = (load(ref, o) for ref in (r_ref, k_ref, v_ref, wl_ref, al_ref, g_ref))
        lw = -jnp.exp(-jax.nn.softplus(-(w0 + wl)) - 0.5)
        ag = jax.nn.sigmoid(a0 + al)
        kk = k * k_k
        kk = kk / jnp.maximum(jnp.sqrt(jnp.sum(kk * kk, axis=-1, keepdims=True)), 1e-12)
        k2 = k * (1.0 + (ag - 1.0) * k_a)
        cl = _cumsum_rows(lw, chunk)
        e_pos = jnp.exp(cl)
        e_neg = jnp.exp(-cl)
        at = (-kk) * jnp.exp(cl - lw)
        bt = (kk * ag) * e_neg
        kt = k2 * e_neg
        rt = r * e_pos
        wc = e_pos[:, chunk - 1:chunk, :]
        x2 = jnp.concatenate([at, rt], axis=1)
        pb = bdot("gtk,gsk->gts", x2, bt)
        pk = bdot("gtk,gsk->gts", x2, kt)
        lab = jnp.where(strict, pb[:, :chunk], 0.0)
        lak = jnp.where(strict, pk[:, :chunk], 0.0)
        arb = jnp.where(incl, pb[:, chunk:], 0.0)
        ark = jnp.where(incl, pk[:, chunk:], 0.0)
        x = jnp.concatenate([at, bdot("gts,gsv->gtv", lak, v)], axis=-1)
        for s in range(chunk - 1):
            x = x + lab[:, :, s:s + 1] * x[:, s:s + 1, :]
        ah = x[:, :, :RW_HEAD]
        vh = x[:, :, RW_HEAD:]
        rh = rt + bdot("gts,gsk->gtk", arb, ah)
        yh = bdot("gts,gsv->gtv", arb, vh) + bdot("gts,gsv->gtv", ark, v)
        gp = bdot("gtj,gtk->gjk", ah, bt)
        ht = bdot("gtv,gtk->gvk", vh, bt) + bdot("gtv,gtk->gvk", v, kt)
        st = s_scr[...]
        y = bdot("gtk,gvk->gtv", rh, st) + yh
        s_scr[...] = (st + bdot("gvj,gjk->gvk", st, gp) + ht) * wc
        mean = jnp.mean(y, axis=-1, keepdims=True)
        yc = y - mean
        var = jnp.mean(yc * yc, axis=-1, keepdims=True)
        y = yc * lax.rsqrt(var + RW_GN_EPS) * ln_w + ln_b
        y = y + jnp.sum(r * k2 * r_k, axis=-1, keepdims=True) * v
        o_ref[:, pl.ds(o, chunk), :] = _merge_heads(y * g, nb, hb).astype(o_ref.dtype)
        return carry

    lax.fori_loop(0, seq // chunk, body, 0)
    sf_ref[...] = s_scr[...].reshape(nb, hb, RW_HEAD, RW_HEAD)


def _rwkv_chunk(seq):
    for c in (48, 32, 16, 8):
        if seq % c == 0:
            return c
    raise ValueError(seq)


def rwkv7(r, k, v, wl, al, g, s0, w0, a0, k_k, k_a, r_k, ln_w, ln_b, nb, hb):
    bsz, seq, d = r.shape
    chunk = _rwkv_chunk(seq)
    assert bsz % nb == 0 and RW_HEADS % hb == 0
    wid = hb * RW_HEAD
    seq_blk = pl.BlockSpec((nb, seq, wid), lambda b, h: (b, 0, h))
    st_blk = pl.BlockSpec((nb, hb, RW_HEAD, RW_HEAD), lambda b, h: (b, h, 0, 0))
    vec = pl.BlockSpec((1, wid), lambda b, h: (0, h))
    out, sf = pl.pallas_call(
        functools.partial(_rwkv_kernel, nb=nb, hb=hb, seq=seq, chunk=chunk),
        grid=(bsz // nb, RW_HEADS // hb),
        in_specs=[seq_blk] * 6 + [st_blk] + [vec] * 7,
        out_specs=[seq_blk, st_blk],
        out_shape=[jax.ShapeDtypeStruct((bsz, seq, d), BF16),
                   jax.ShapeDtypeStruct((bsz, RW_HEADS, RW_HEAD, RW_HEAD), F32)],
        scratch_shapes=[pltpu.VMEM((nb * hb, RW_HEAD, RW_HEAD), F32)],
        compiler_params=_params("parallel", "parallel"),
        name="rwkv7",
    )(r, k, v, wl, al, g, s0, *(p.reshape(1, d) for p in (w0, a0, k_k, k_a, r_k, ln_w, ln_b)))
    return out, sf


def _ffn_in_kernel(x_ref, wa_ref, wv_ref, e_ref, cw_ref, cb_ref, o_ref, st_ref, scr, *, nb, seq):
    x = x_ref[...]
    tn = wa_ref.shape[1]
    a = jnp.dot(x, wa_ref[...], preferred_element_type=F32).reshape(nb, seq, tn)
    v = jnp.dot(x, wv_ref[...], preferred_element_type=F32).reshape(nb, seq, tn)
    scr[:, 0:SUBLANES, :] = e_ref[...]
    scr[:, SUBLANES:, :] = a
    cw = cw_ref[...]
    c = cb_ref[...][None] + cw[CONV_W - 1:CONV_W][None] * a
    for j in range(CONV_W - 1):
        back = CONV_W - 1 - j
        c = c + cw[j:j + 1][None] * scr[:, SUBLANES - back:SUBLANES - back + seq, :]
    o_ref[...] = (jax.nn.gelu(c) * v).reshape(nb * seq, tn).astype(o_ref.dtype)
    st_ref[...] = scr[:, SUBLANES + seq - (CONV_W - 1):SUBLANES + seq, :]


def ffn_in(xb, conv0, w_a, w_v, conv_w, conv_b, bsz, seq, nb, tn):
    t, d = xb.shape
    assert t == bsz * seq and bsz % nb == 0 and D_FF % tn == 0
    halo = jnp.pad(conv0, ((0, 0), (SUBLANES - (CONV_W - 1), 0), (0, 0)))
    col = lambda i, j: (0, j)
    out, st = pl.pallas_call(
        functools.partial(_ffn_in_kernel, nb=nb, seq=seq),
        grid=(bsz // nb, D_FF // tn),
        in_specs=[pl.BlockSpec((nb * seq, d), lambda i, j: (i, 0)),
                  pl.BlockSpec((d, tn), col), pl.BlockSpec((d, tn), col),
                  pl.BlockSpec((nb, SUBLANES, tn), lambda i, j: (i, 0, j)),
                  pl.BlockSpec((CONV_W, tn), col), pl.BlockSpec((1, tn), col)],
        out_specs=[pl.BlockSpec((nb * seq, tn), lambda i, j: (i, j)),
                   pl.BlockSpec((nb, CONV_W - 1, tn), lambda i, j: (i, 0, j))],
        out_shape=[jax.ShapeDtypeStruct((t, D_FF), BF16),
                   jax.ShapeDtypeStruct((bsz, CONV_W - 1, D_FF), F32)],
        scratch_shapes=[pltpu.VMEM((nb, SUBLANES + seq, tn), F32)],
        compiler_params=_params("parallel", "parallel"),
        name="ffn_in",
    )(xb, w_a, w_v, halo, conv_w, conv_b.reshape(1, D_FF))
    return out, st


def _trunk(h3, s5r, s5i, hg, rw, sh, cv, p, cfg):
    bsz, seq, d = h3.shape
    t = bsz * seq
    h = h3.reshape(t, d)

    (xb,) = rmsnorm(h, p["ln_mix"][0], (BF16,))
    z = matmul([xb], [p["ev_w_in"]], F32).reshape(bsz, seq, EVEN_IN)
    ys5, n_s5r, n_s5i = s5_scan(z, s5r[0], s5i[0], *p["s5"], nb=cfg["s5_nb"])
    ys5 = ys5.reshape(t, S5_WIDTH)
    ya = matmul([ys5.astype(BF16)], [p["s5_w_glu"]], BF16, epilogue="glu", extra=ys5)
    yb, n_hg = hgrn2(z, hg[0], p["hg_lb"], p["hg_norm_w"], nb=cfg["hg_nb"])
    h = matmul([ya, yb.reshape(t, -1)], p["ev_w_out"], F32, epilogue="residual", extra=h)
    (xb,) = rmsnorm(h, p["ln_ffn"][0], (BF16,))
    gated, n_cv0 = ffn_in(xb, cv[0], *p["ffn_in"][0], bsz, seq, cfg["ffn_nb"], cfg["ffn_tn"])
    h = matmul([gated], [p["ffn_w_down"][0]], F32, epilogue="residual", extra=h)

    (xn,) = rmsnorm(h, p["ln_mix"][1], (F32,))
    xn3 = xn.reshape(bsz, seq, d)
    prev = jnp.concatenate([sh[0][:, None], xn3[:, :-1]], axis=1).reshape(t, d)
    xr, xw, xk, xv, xa, xg = rwkv_mix(xn, prev, p["rw_mu"])
    r = matmul([xr], [p["rw_w_r"]], F32)
    k = matmul([xk], [p["rw_w_k"]], F32)
    v = matmul([xv], [p["rw_w_v"]], F32)
    wl = matmul([matmul([xw], [p["rw_w1"]], BF16, act="tanh")], [p["rw_w2"]], F32)
    al = matmul([matmul([xa], [p["rw_a1"]], BF16)], [p["rw_a2"]], F32)
    g = matmul([matmul([xg], [p["rw_g1"]], BF16, act="sigmoid")], [p["rw_g2"]], F32)
    as3 = lambda a: a.reshape(bsz, seq, d)
    yo, n_rw = rwkv7(as3(r), as3(k), as3(v), as3(wl), as3(al), as3(g), rw[0], *p["rw_vec"],
                     nb=cfg["rw_nb"], hb=cfg["rw_hb"])
    h = matmul([yo.reshape(t, d)], [p["rw_w_o"]], F32, epilogue="residual", extra=h)
    n_sh = xn3[:, -1]
    (xb,) = rmsnorm(h, p["ln_ffn"][1], (BF16,))
    gated, n_cv1 = ffn_in(xb, cv[1], *p["ffn_in"][1], bsz, seq, cfg["ffn_nb"], cfg["ffn_tn"])
    h = matmul([gated], [p["ffn_w_down"][1]], F32, epilogue="residual", extra=h)

    (y,) = rmsnorm(h, p["ln_final"], (F32,))
    return (y.reshape(bsz, seq, d), n_s5r[None], n_s5i[None], n_hg[None], n_rw[None], n_sh[None],
            jnp.stack([n_cv0, n_cv1]))


PROMPT_CFG = dict(s5_nb=1, hg_nb=1, rw_nb=1, rw_hb=2, ffn_nb=1, ffn_tn=256)
SAMPLE_CFG = dict(s5_nb=32, hg_nb=16, rw_nb=4, rw_hb=2, ffn_nb=128, ffn_tn=512)


def kernel(x_prompt, x_sample, state_s5_re, state_s5_im, state_hgrn, state_rwkv, state_shift, state_conv, meta_tokens, ln_mix, ln_ffn, ln_final, ev_w_in, ev_w_out, s5_lam_re, s5_lam_im, s5_log_step, s5_b_re, s5_b_im, s5_c_re, s5_c_im, s5_d, s5_w_glu, hg_lb, hg_norm_w, rw_mu, rw_w0, rw_w1, rw_w2, rw_a0, rw_a1, rw_a2, rw_g1, rw_g2, rw_k_k, rw_k_a, rw_r_k, rw_w_r, rw_w_k, rw_w_v, rw_w_o, rw_ln_w, rw_ln_b, ffn_w_in, ffn_conv_w, ffn_conv_b, ffn_w_down):
    bf = lambda w: w.astype(BF16)
    lb_all = hg_lower_bounds(hg_lb)
    pwr, pwi, bbr_t, bbi_t = s5_prep(s5_lam_re[0], s5_lam_im[0], s5_log_step[0], s5_b_re[0], s5_b_im[0])
    wbr, wbi, wcr, wci = _s5_block_weights(bbr_t, bbi_t, s5_c_re[0], s5_c_im[0])
    p = {
        "ln_mix": ln_mix, "ln_ffn": ln_ffn, "ln_final": ln_final,
        "ev_w_in": bf(ev_w_in[0]),
        "ev_w_out": [bf(ev_w_out[0][:S5_WIDTH]), bf(ev_w_out[0][S5_WIDTH:])],
        "s5": (pwr, pwi, wbr, wbi, wcr, wci, s5_d[0].reshape(1, S5_WIDTH)),
        "s5_w_glu": bf(s5_w_glu[0]),
        "hg_lb": lb_all[0], "hg_norm_w": hg_norm_w[0],
        "rw_mu": rw_mu[0],
        "rw_w1": bf(rw_w1[0]), "rw_w2": bf(rw_w2[0]), "rw_a1": bf(rw_a1[0]), "rw_a2": bf(rw_a2[0]),
        "rw_g1": bf(rw_g1[0]), "rw_g2": bf(rw_g2[0]),
        "rw_w_r": bf(rw_w_r[0]), "rw_w_k": bf(rw_w_k[0]), "rw_w_v": bf(rw_w_v[0]), "rw_w_o": bf(rw_w_o[0]),
        "rw_vec": (rw_w0[0], rw_a0[0], rw_k_k[0], rw_k_a[0], rw_r_k[0].reshape(D_MODEL), rw_ln_w[0], rw_ln_b[0]),
        "ffn_in": [(bf(ffn_w_in[l][:, :D_FF]), bf(ffn_w_in[l][:, D_FF:]), ffn_conv_w[l], ffn_conv_b[l])
                   for l in range(2)],
        "ffn_w_down": [bf(ffn_w_down[l]) for l in range(2)],
    }

    bsz = x_prompt.shape[0]
    hp = jnp.concatenate([jnp.broadcast_to(meta_tokens[None], (bsz, N_META, D_MODEL)), x_prompt], axis=1)
    zeros = lambda *s: jnp.zeros(s, F32)
    outs_p = _trunk(hp,
                    zeros(1, bsz, S5_GROUPS, S5_STATE), zeros(1, bsz, S5_GROUPS, S5_STATE),
                    zeros(1, bsz, HG_HEADS, HG_K, HG_V), zeros(1, bsz, RW_HEADS, RW_HEAD, RW_HEAD),
                    zeros(1, bsz, D_MODEL), zeros(2, bsz, CONV_W - 1, D_FF), p, PROMPT_CFG)
    outs_s = _trunk(x_sample, state_s5_re, state_s5_im, state_hgrn, state_rwkv, state_shift, state_conv,
                    p, SAMPLE_CFG)
    return (outs_p[0][:, N_META:], outs_s[0]) + tuple(outs_p[1:]) + tuple(outs_s[1:])
```

```python
import functools

import jax
import jax.numpy as jnp
from jax import lax
from jax.experimental import pallas as pl
from jax.experimental.pallas import tpu as pltpu

F32 = jnp.float32
BF16 = jnp.bfloat16

D_MODEL = 2048
N_META = 16
EPS = 1e-6
S5_WIDTH = 1024
S5_GROUP = 16
S5_GROUPS = 64
S5_STATE = 64
S5_CH = S5_GROUPS * S5_STATE
HG_HEADS = 8
HG_K = 128
HG_V = 128
HG_CHUNK = 16
EVEN_IN = 5120
RW_HEAD = 64
RW_HEADS = 32
RW_GN_EPS = 64e-5
D_FF = 5632
CONV_W = 3

LANES = 128
SUBLANES = 8
VMEM_LIMIT = 56 * 1024 * 1024


def _params(*sem):
    return pltpu.CompilerParams(dimension_semantics=sem, vmem_limit_bytes=VMEM_LIMIT)


def _row_tile(t, cap=1024):
    best = None
    for d in range(16, min(t, cap) + 1, 16):
        if t % d == 0:
            best = d
    assert best is not None, t
    return best


def _col_tile(n, cap=512):
    if n <= cap:
        return n
    best = None
    for d in range(LANES, cap + 1, LANES):
        if n % d == 0:
            best = d
    assert best is not None, n
    return best


def _rms_kernel(x_ref, w_ref, *o_refs):
    x = x_ref[...]
    y = x * lax.rsqrt(jnp.mean(x * x, axis=-1, keepdims=True) + EPS) * w_ref[...]
    for o_ref in o_refs:
        o_ref[...] = y.astype(o_ref.dtype)


def rmsnorm(x, w, dtypes):
    t, d = x.shape
    tm = _row_tile(t)
    spec = pl.BlockSpec((tm, d), lambda i: (i, 0))
    outs = pl.pallas_call(
        _rms_kernel,
        grid=(t // tm,),
        in_specs=[spec, pl.BlockSpec((1, d), lambda i: (0, 0))],
        out_specs=[spec for _ in dtypes],
        out_shape=[jax.ShapeDtypeStruct((t, d), dt) for dt in dtypes],
        compiler_params=_params("parallel"),
        name="rmsnorm",
    )(x, w.reshape(1, d))
    return outs


def _act(x, act):
    if act == "tanh":
        return jnp.tanh(x)
    if act == "sigmoid":
        return jax.nn.sigmoid(x)
    assert act is None
    return x


def _mm_kernel(*refs, n_a, act, epilogue):
    a_refs = refs[:n_a]
    w_refs = refs[n_a:2 * n_a]
    rest = refs[2 * n_a:]
    o_ref = rest[-1]
    acc = jnp.dot(a_refs[0][...], w_refs[0][...], preferred_element_type=F32)
    for a_ref, w_ref in zip(a_refs[1:], w_refs[1:]):
        acc = acc + jnp.dot(a_ref[...], w_ref[...], preferred_element_type=F32)
    acc = _act(acc, act)
    if epilogue == "residual":
        acc = rest[0][...] + acc
    elif epilogue == "glu":
        acc = rest[0][...] * jax.nn.sigmoid(acc)
    o_ref[...] = acc.astype(o_ref.dtype)


def matmul(a_list, w_list, out_dtype, act=None, epilogue=None, extra=None):
    t = a_list[0].shape[0]
    n = w_list[0].shape[1]
    tm = _row_tile(t)
    tn = _col_tile(n)
    in_specs = [pl.BlockSpec((tm, a.shape[1]), lambda i, j: (i, 0)) for a in a_list]
    in_specs += [pl.BlockSpec((w.shape[0], tn), lambda i, j: (0, j)) for w in w_list]
    args = list(a_list) + list(w_list)
    if epilogue is not None:
        in_specs.append(pl.BlockSpec((tm, tn), lambda i, j: (i, j)))
        args.append(extra)
    return pl.pallas_call(
        functools.partial(_mm_kernel, n_a=len(a_list), act=act, epilogue=epilogue),
        grid=(t // tm, n // tn),
        in_specs=in_specs,
        out_specs=pl.BlockSpec((tm, tn), lambda i, j: (i, j)),
        out_shape=jax.ShapeDtypeStruct((t, n), out_dtype),
        compiler_params=_params("parallel", "parallel"),
        name="matmul",
    )(*args)


def _s5_prep_kernel(lr_ref, li_ref, ls_ref, brt_ref, bit_ref, pwr_ref, pwi_ref, bbr_ref, bbi_ref):
    lr = jnp.minimum(lr_ref[...], -1e-4)
    li = li_ref[...]
    dt = jnp.exp(ls_ref[...])
    n = lax.broadcasted_iota(jnp.int32, (SUBLANES, S5_CH), 0).astype(F32) + 1.0
    mag = jnp.exp(n * (lr * dt))
    ang = n * (li * dt)
    pwr = mag * jnp.cos(ang)
    pwi = mag * jnp.sin(ang)
    pwr_ref[...] = pwr
    pwi_ref[...] = pwi
    ar = pwr[0:1]
    ai = pwi[0:1]
    den = lr * lr + li * li
    zr = ((ar - 1.0) * lr + ai * li) / den
    zi = (ai * lr - (ar - 1.0) * li) / den
    br = brt_ref[...]
    bi = bit_ref[...]
    bbr_ref[...] = zr * br - zi * bi
    bbi_ref[...] = zr * bi + zi * br


def s5_prep(lam_re, lam_im, log_step, b_re, b_im):
    lr = lam_re.reshape(1, S5_CH)
    li = lam_im.reshape(1, S5_CH)
    ls = jnp.broadcast_to(log_step[:, None], (S5_GROUPS, S5_STATE)).reshape(1, S5_CH)
    brt = b_re.reshape(S5_CH, S5_GROUP).T
    bit = b_im.reshape(S5_CH, S5_GROUP).T
    return pl.pallas_call(
        _s5_prep_kernel,
        out_shape=[jax.ShapeDtypeStruct((SUBLANES, S5_CH), F32)] * 2
        + [jax.ShapeDtypeStruct((S5_GROUP, S5_CH), F32)] * 2,
        name="s5_prep",
    )(lr, li, ls, brt, bit)


S5_BLK_GROUPS = LANES // S5_GROUP
S5_BLKS = S5_WIDTH // LANES
S5_BLK_CH = S5_BLK_GROUPS * S5_STATE


def _cmul_add(xr, xi, mr, mi, sr, si):
    return xr + mr * sr - mi * si, xi + mr * si + mi * sr


def _s5_kernel(u_ref, h0r_ref, h0i_ref, pwr_ref, pwi_ref, wbr_ref, wbi_ref, wcr_ref, wci_ref, d_ref,
               y_ref, hr_ref, hi_ref, xr_scr, xi_scr, *, nb, seq):
    u2 = u_ref[...].reshape(nb * seq, LANES)
    ub = u2.astype(BF16)
    xr_scr[...] = jnp.dot(ub, wbr_ref[0], preferred_element_type=F32).reshape(nb, seq, S5_BLK_CH)
    xi_scr[...] = jnp.dot(ub, wbi_ref[0], preferred_element_type=F32).reshape(nb, seq, S5_BLK_CH)

    pwr = pwr_ref[...]
    pwi = pwi_ref[...]
    row = lax.broadcasted_iota(jnp.int32, (SUBLANES, S5_BLK_CH), 0)
    steps = []
    for d in (1, 2, 4):
        keep = row >= d
        steps.append((d, jnp.where(keep, pwr[d - 1:d], 0.0)[None], jnp.where(keep, pwi[d - 1:d], 0.0)[None]))
    pr = pwr[None]
    pi = pwi[None]

    def tile(i, carry):
        cr, ci = carry
        o = pl.multiple_of(i * SUBLANES, SUBLANES)
        xr = xr_scr[:, pl.ds(o, SUBLANES), :]
        xi = xi_scr[:, pl.ds(o, SUBLANES), :]
        for d, mr, mi in steps:
            sr = pltpu.roll(xr, d, axis=1)
            si = pltpu.roll(xi, d, axis=1)
            xr, xi = _cmul_add(xr, xi, mr, mi, sr, si)
        xr, xi = _cmul_add(xr, xi, pr, pi, cr, ci)
        xr_scr[:, pl.ds(o, SUBLANES), :] = xr
        xi_scr[:, pl.ds(o, SUBLANES), :] = xi
        return xr[:, SUBLANES - 1:SUBLANES, :], xi[:, SUBLANES - 1:SUBLANES, :]

    hr, hi = lax.fori_loop(0, seq // SUBLANES, tile, (h0r_ref[...], h0i_ref[...]))
    hr_ref[...] = hr
    hi_ref[...] = hi

    xr = xr_scr[...].reshape(nb * seq, S5_BLK_CH).astype(BF16)
    xi = xi_scr[...].reshape(nb * seq, S5_BLK_CH).astype(BF16)
    y = (jnp.dot(xr, wcr_ref[0], preferred_element_type=F32)
         - jnp.dot(xi, wci_ref[0], preferred_element_type=F32)
         + d_ref[...] * u2)
    y_ref[...] = jax.nn.gelu(y).reshape(nb, seq, LANES)


def s5_scan(z3, h0r, h0i, pwr, pwi, wbr, wbi, wcr, wci, d, nb):
    bsz, seq, _ = z3.shape
    assert seq % SUBLANES == 0 and bsz % nb == 0
    seq_blk = pl.BlockSpec((nb, seq, LANES), lambda b, k: (b, 0, k))
    st_blk = pl.BlockSpec((nb, 1, S5_BLK_CH), lambda b, k: (b, 0, k))
    pw_blk = pl.BlockSpec((SUBLANES, S5_BLK_CH), lambda b, k: (0, k))
    wb_blk = pl.BlockSpec((1, LANES, S5_BLK_CH), lambda b, k: (k, 0, 0))
    wc_blk = pl.BlockSpec((1, S5_BLK_CH, LANES), lambda b, k: (k, 0, 0))
    y, hr, hi = pl.pallas_call(
        functools.partial(_s5_kernel, nb=nb, seq=seq),
        grid=(bsz // nb, S5_BLKS),
        in_specs=[seq_blk, st_blk, st_blk, pw_blk, pw_blk, wb_blk, wb_blk, wc_blk, wc_blk,
                  pl.BlockSpec((1, LANES), lambda b, k: (0, k))],
        out_specs=[seq_blk, st_blk, st_blk],
        out_shape=[jax.ShapeDtypeStruct((bsz, seq, S5_WIDTH), F32),
                   jax.ShapeDtypeStruct((bsz, 1, S5_CH), F32),
                   jax.ShapeDtypeStruct((bsz, 1, S5_CH), F32)],
        scratch_shapes=[pltpu.VMEM((nb, seq, S5_BLK_CH), F32), pltpu.VMEM((nb, seq, S5_BLK_CH), F32)],
        compiler_params=_params("parallel", "parallel"),
        name="s5_scan",
    )(z3, h0r.reshape(bsz, 1, S5_CH), h0i.reshape(bsz, 1, S5_CH), pwr, pwi, wbr, wbi, wcr, wci, d)
    return y, hr.reshape(bsz, S5_GROUPS, S5_STATE), hi.reshape(bsz, S5_GROUPS, S5_STATE)


def _s5_block_weights(bbr_t, bbi_t, c_re, c_im):
    eye = jnp.eye(S5_BLK_GROUPS, dtype=F32)

    def wb(bt):
        b4 = bt.reshape(S5_GROUP, S5_BLKS, S5_BLK_GROUPS, S5_STATE)
        w = jnp.einsum("cbgp,hg->bhcgp", b4, eye)
        return w.reshape(S5_BLKS, LANES, S5_BLK_CH).astype(BF16)

    def wc(c):
        c4 = c.reshape(S5_BLKS, S5_BLK_GROUPS, S5_GROUP, S5_STATE)
        w = jnp.einsum("bgcp,hg->bhpgc", c4, eye)
        return w.reshape(S5_BLKS, S5_BLK_CH, LANES).astype(BF16)

    return wb(bbr_t), wb(bbi_t), wc(c_re), wc(c_im)


def _hg_lb_kernel(x_ref, o_ref):
    x = x_ref[...]
    e = jnp.exp(x - jnp.max(x, axis=0, keepdims=True))
    sm = e / jnp.sum(e, axis=0, keepdims=True)
    acc = sm[0:1]
    o_ref[0:1, :] = acc
    for l in range(1, x.shape[0]):
        acc = acc + sm[l:l + 1]
        o_ref[l:l + 1, :] = acc


def hg_lower_bounds(hg_lb):
    return pl.pallas_call(_hg_lb_kernel, out_shape=jax.ShapeDtypeStruct(hg_lb.shape, F32), name="hg_lb")(hg_lb)


def _cumsum_rows(x, n):
    row = lax.broadcasted_iota(jnp.int32, x.shape, 1)
    d = 1
    while d < n:
        x = x + jnp.where(row >= d, pltpu.roll(x, d, axis=1), 0.0)
        d *= 2
    return x


def _hgrn_kernel(q_ref, f_ref, i_ref, g_ref, s0_ref, lb_ref, nw_ref, o_ref, sf_ref, st_scr, *, nb, seq, chunk):
    lb = lb_ref[...][None]
    nw = nw_ref[...][None]
    for b in range(nb):
        st_scr[b] = s0_ref[b, 0].T
    trow = lax.broadcasted_iota(jnp.int32, (chunk, chunk), 0)
    tcol = lax.broadcasted_iota(jnp.int32, (chunk, chunk), 1)
    causal = (tcol <= trow)[None]

    def body(n, carry):
        o = pl.multiple_of(n * chunk, chunk)
        q = q_ref[:, pl.ds(o, chunk), :]
        f = f_ref[:, pl.ds(o, chunk), :]
        v = i_ref[:, pl.ds(o, chunk), :]
        g = g_ref[:, pl.ds(o, chunk), :]
        fg = lb + (1.0 - lb) * jax.nn.sigmoid(f)
        qh = jax.nn.silu(q)
        kh = 1.0 - fg
        bcum = _cumsum_rows(jnp.log(fg), chunk)
        btot = bcum[:, chunk - 1:chunk, :]
        q_in = (qh * jnp.exp(bcum)).astype(BF16)
        k_in = (kh * jnp.exp(-bcum)).astype(BF16)
        k_end = (kh * jnp.exp(btot - bcum)).astype(BF16)
        decay = jnp.exp(btot)
        vb = v.astype(BF16)
        st = st_scr[...]
        att = jnp.einsum("btk,bsk->bts", q_in, k_in, preferred_element_type=F32)
        att = jnp.where(causal, att, 0.0).astype(BF16)
        out = (jnp.einsum("btk,bvk->btv", q_in, st.astype(BF16), preferred_element_type=F32)
               + jnp.einsum("bts,bsv->btv", att, vb, preferred_element_type=F32))
        st_scr[...] = st * decay + jnp.einsum("bsv,bsk->bvk", vb, k_end, preferred_element_type=F32)
        out = out * lax.rsqrt(jnp.mean(out * out, axis=-1, keepdims=True) + EPS) * nw
        o_ref[:, pl.ds(o, chunk), :] = (out * jax.nn.silu(g)).astype(o_ref.dtype)
        return carry

    n_chunks = seq // chunk
    lax.fori_loop(0, n_chunks, body, 0, unroll=3 if n_chunks % 3 == 0 else 1)
    for b in range(nb):
        sf_ref[b, 0] = st_scr[b].T


def hgrn2(z3, s0, lb, norm_w, nb):
    bsz, seq, _ = z3.shape
    chunk = min(HG_CHUNK, seq)
    assert seq % chunk == 0 and bsz % nb == 0

    def col(base):
        return pl.BlockSpec((nb, seq, LANES), lambda b, h, base=base: (b, 0, base + h))

    st_blk = pl.BlockSpec((nb, 1, HG_K, HG_V), lambda b, h: (b, h, 0, 0))
    vec = pl.BlockSpec((1, LANES), lambda b, h: (0, h))
    out, sf = pl.pallas_call(
        functools.partial(_hgrn_kernel, nb=nb, seq=seq, chunk=chunk),
        grid=(bsz // nb, HG_HEADS),
        in_specs=[col(8), col(16), col(24), col(32), st_blk, vec,
                  pl.BlockSpec((1, LANES), lambda b, h: (0, 0))],
        out_specs=[pl.BlockSpec((nb, seq, LANES), lambda b, h: (b, 0, h)), st_blk],
        out_shape=[jax.ShapeDtypeStruct((bsz, seq, HG_HEADS * HG_V), BF16),
                   jax.ShapeDtypeStruct((bsz, HG_HEADS, HG_K, HG_V), F32)],
        scratch_shapes=[pltpu.VMEM((nb, HG_V, HG_K), F32)],
        compiler_params=_params("parallel", "parallel"),
        name="hgrn2",
    )(z3, z3, z3, z3, s0, lb.reshape(1, HG_HEADS * HG_K), norm_w.reshape(1, HG_V))
    return out, sf


def _mix_kernel(x_ref, p_ref, mu_ref, *o_refs):
    x = x_ref[...]
    xx = p_ref[...] - x
    for j, o_ref in enumerate(o_refs):
        o_ref[...] = (x + xx * mu_ref[j:j + 1, :]).astype(o_ref.dtype)


def rwkv_mix(xn, prev, mu):
    t, d = xn.shape
    tm = _row_tile(t, cap=512)
    spec = pl.BlockSpec((tm, d), lambda i: (i, 0))
    return pl.pallas_call(
        _mix_kernel,
        grid=(t // tm,),
        in_specs=[spec, spec, pl.BlockSpec((6, d), lambda i: (0, 0))],
        out_specs=[spec] * 6,
        out_shape=[jax.ShapeDtypeStruct((t, d), BF16)] * 6,
        compiler_params=_params("parallel"),
        name="rwkv_mix",
    )(xn, prev, mu)


def _split_heads(x, nb, hb):
    parts = [x[:, :, h * RW_HEAD:(h + 1) * RW_HEAD] for h in range(hb)]
    c = x.shape[1]
    return jnp.stack(parts, axis=1).reshape(nb * hb, c, RW_HEAD)


def _merge_heads(x, nb, hb):
    c = x.shape[1]
    x4 = x.reshape(nb, hb, c, RW_HEAD)
    return jnp.concatenate([x4[:, h] for h in range(hb)], axis=-1)


def _head_rows(ref, nb, hb):
    p = ref[...]
    parts = jnp.stack([p[:, h * RW_HEAD:(h + 1) * RW_HEAD] for h in range(hb)], axis=0)
    return jnp.broadcast_to(parts[None], (nb, hb, 1, RW_HEAD)).reshape(nb * hb, 1, RW_HEAD)


RW_SOLVE_BLOCK = 16


def _rwkv_kernel(r_ref, k_ref, v_ref, wl_ref, al_ref, g_ref, s0_ref,
                 w0_ref, a0_ref, kk_ref, ka_ref, rk_ref, lnw_ref, lnb_ref,
                 o_ref, sf_ref, s_scr, *, nb, hb, chunk):
    ng = nb * hb
    step = pl.program_id(2)
    w0, a0, k_k, k_a, r_k, ln_w, ln_b = (_head_rows(p, nb, hb) for p in
                                         (w0_ref, a0_ref, kk_ref, ka_ref, rk_ref, lnw_ref, lnb_ref))

    @pl.when(step == 0)
    def _():
        s_scr[...] = s0_ref[...].reshape(ng, RW_HEAD, RW_HEAD)

    trow = lax.broadcasted_iota(jnp.int32, (chunk, chunk), 0)
    tcol = lax.broadcasted_iota(jnp.int32, (chunk, chunk), 1)
    strict = (tcol < trow)[None]
    incl = (tcol <= trow)[None]

    def bdot(spec, a, b):
        return jnp.einsum(spec, a.astype(BF16), b.astype(BF16), preferred_element_type=F32)

    r, k, v, wl, al, g = (_split_heads(ref[...], nb, hb) for ref in (r_ref, k_ref, v_ref, wl_ref, al_ref, g_ref))
    lw = -jnp.exp(-jax.nn.softplus(-(w0 + wl)) - 0.5)
    ag = jax.nn.sigmoid(a0 + al)
    kk = k * k_k
    kk = kk / jnp.maximum(jnp.sqrt(jnp.sum(kk * kk, axis=-1, keepdims=True)), 1e-12)
    k2 = k * (1.0 + (ag - 1.0) * k_a)
    cl = _cumsum_rows(lw, chunk)
    e_pos = jnp.exp(cl)
    e_neg = jnp.exp(-cl)
    at = (-kk) * jnp.exp(cl - lw)
    bt = (kk * ag) * e_neg
    kt = k2 * e_neg
    rt = r * e_pos
    wc = e_pos[:, chunk - 1:chunk, :]
    x2 = jnp.concatenate([at, rt], axis=1)
    pb = bdot("gtk,gsk->gts", x2, bt)
    pk = bdot("gtk,gsk->gts", x2, kt)
    lab = jnp.where(strict, pb[:, :chunk], 0.0)
    lak = jnp.where(strict, pk[:, :chunk], 0.0)
    arb = jnp.where(incl, pb[:, chunk:], 0.0)
    ark = jnp.where(incl, pk[:, chunk:], 0.0)
    x = jnp.concatenate([at, bdot("gts,gsv->gtv", lak, v)], axis=-1)
    sub = min(RW_SOLVE_BLOCK, chunk)
    solved = []
    for lo in range(0, chunk, sub):
        xi = x[:, lo:lo + sub, :]
        if solved:
            xi = xi + bdot("gts,gsw->gtw", lab[:, lo:lo + sub, :lo], jnp.concatenate(solved, axis=1))
        lii = lab[:, lo:lo + sub, lo:lo + sub]
        for s in range(sub - 1):
            xi = xi + lii[:, :, s:s + 1] * xi[:, s:s + 1, :]
        solved.append(xi)
    x = jnp.concatenate(solved, axis=1)
    ah = x[:, :, :RW_HEAD]
    vh = x[:, :, RW_HEAD:]
    rh = rt + bdot("gts,gsk->gtk", arb, ah)
    yh = bdot("gts,gsv->gtv", arb, vh) + bdot("gts,gsv->gtv", ark, v)
    gp = bdot("gtj,gtk->gjk", ah, bt)
    ht = bdot("gtv,gtk->gvk", vh, bt) + bdot("gtv,gtk->gvk", v, kt)
    st = s_scr[...]
    y = bdot("gtk,gvk->gtv", rh, st) + yh
    st = (st + bdot("gvj,gjk->gvk", st, gp) + ht) * wc
    s_scr[...] = st
    mean = jnp.mean(y, axis=-1, keepdims=True)
    yc = y - mean
    var = jnp.mean(yc * yc, axis=-1, keepdims=True)
    y = yc * lax.rsqrt(var + RW_GN_EPS) * ln_w + ln_b
    y = y + jnp.sum(r * k2 * r_k, axis=-1, keepdims=True) * v
    o_ref[...] = _merge_heads(y * g, nb, hb).astype(o_ref.dtype)

    @pl.when(step == pl.num_programs(2) - 1)
    def _():
        sf_ref[...] = st.reshape(nb, hb, RW_HEAD, RW_HEAD)


def _rwkv_chunk(seq):
    for c in (48, 32, 16, 8):
        if seq % c == 0:
            return c
    raise ValueError(seq)


def rwkv7(r, k, v, wl, al, g, s0, w0, a0, k_k, k_a, r_k, ln_w, ln_b, nb, hb):
    bsz, seq, d = r.shape
    chunk = _rwkv_chunk(seq)
    assert bsz % nb == 0 and RW_HEADS % hb == 0
    wid = hb * RW_HEAD
    seq_blk = pl.BlockSpec((nb, chunk, wid), lambda h, b, t: (b, t, h))
    st_blk = pl.BlockSpec((nb, hb, RW_HEAD, RW_HEAD), lambda h, b, t: (b, h, 0, 0))
    vec = pl.BlockSpec((1, wid), lambda h, b, t: (0, h))
    out, sf = pl.pallas_call(
        functools.partial(_rwkv_kernel, nb=nb, hb=hb, chunk=chunk),
        grid=(RW_HEADS // hb, bsz // nb, seq // chunk),
        in_specs=[seq_blk] * 6 + [st_blk] + [vec] * 7,
        out_specs=[seq_blk, st_blk],
        out_shape=[jax.ShapeDtypeStruct((bsz, seq, d), BF16),
                   jax.ShapeDtypeStruct((bsz, RW_HEADS, RW_HEAD, RW_HEAD), F32)],
        scratch_shapes=[pltpu.VMEM((nb * hb, RW_HEAD, RW_HEAD), F32)],
        compiler_params=_params("parallel", "parallel", "arbitrary"),
        name="rwkv7",
    )(r, k, v, wl, al, g, s0, *(p.reshape(1, d) for p in (w0, a0, k_k, k_a, r_k, ln_w, ln_b)))
    return out, sf


def _ffn_in_kernel(x_ref, wa_ref, wv_ref, e_ref, cw_ref, cb_ref, o_ref, st_ref, scr, *, nb, seq):
    x = x_ref[...]
    tn = wa_ref.shape[1]
    a = jnp.dot(x, wa_ref[...], preferred_element_type=F32).reshape(nb, seq, tn)
    v = jnp.dot(x, wv_ref[...], preferred_element_type=F32).reshape(nb, seq, tn)
    scr[:, 0:SUBLANES, :] = e_ref[...]
    scr[:, SUBLANES:, :] = a
    cw = cw_ref[...]
    c = cb_ref[...][None] + cw[CONV_W - 1:CONV_W][None] * a
    for j in range(CONV_W - 1):
        back = CONV_W - 1 - j
        c = c + cw[j:j + 1][None] * scr[:, SUBLANES - back:SUBLANES - back + seq, :]
    o_ref[...] = (jax.nn.gelu(c) * v).reshape(nb * seq, tn).astype(o_ref.dtype)
    st_ref[...] = scr[:, SUBLANES + seq - (CONV_W - 1):SUBLANES + seq, :]


def ffn_in(xb, conv0, w_a, w_v, conv_w, conv_b, bsz, seq, nb, tn):
    t, d = xb.shape
    assert t == bsz * seq and bsz % nb == 0 and D_FF % tn == 0
    halo = jnp.pad(conv0, ((0, 0), (SUBLANES - (CONV_W - 1), 0), (0, 0)))
    col = lambda i, j: (0, j)
    out, st = pl.pallas_call(
        functools.partial(_ffn_in_kernel, nb=nb, seq=seq),
        grid=(bsz // nb, D_FF // tn),
        in_specs=[pl.BlockSpec((nb * seq, d), lambda i, j: (i, 0)),
                  pl.BlockSpec((d, tn), col), pl.BlockSpec((d, tn), col),
                  pl.BlockSpec((nb, SUBLANES, tn), lambda i, j: (i, 0, j)),
                  pl.BlockSpec((CONV_W, tn), col), pl.BlockSpec((1, tn), col)],
        out_specs=[pl.BlockSpec((nb * seq, tn), lambda i, j: (i, j)),
                   pl.BlockSpec((nb, CONV_W - 1, tn), lambda i, j: (i, 0, j))],
        out_shape=[jax.ShapeDtypeStruct((t, D_FF), BF16),
                   jax.ShapeDtypeStruct((bsz, CONV_W - 1, D_FF), F32)],
        scratch_shapes=[pltpu.VMEM((nb, SUBLANES + seq, tn), F32)],
        compiler_params=_params("parallel", "parallel"),
        name="ffn_in",
    )(xb, w_a, w_v, halo, conv_w, conv_b.reshape(1, D_FF))
    return out, st


def _trunk(h3, s5r, s5i, hg, rw, sh, cv, p, cfg):
    bsz, seq, d = h3.shape
    t = bsz * seq
    h = h3.reshape(t, d)

    (xb,) = rmsnorm(h, p["ln_mix"][0], (BF16,))
    z = matmul([xb], [p["ev_w_in"]], F32).reshape(bsz, seq, EVEN_IN)
    ys5, n_s5r, n_s5i = s5_scan(z, s5r[0], s5i[0], *p["s5"], nb=cfg["s5_nb"])
    ys5 = ys5.reshape(t, S5_WIDTH)
    ya = matmul([ys5.astype(BF16)], [p["s5_w_glu"]], BF16, epilogue="glu", extra=ys5)
    yb, n_hg = hgrn2(z, hg[0], p["hg_lb"], p["hg_norm_w"], nb=cfg["hg_nb"])
    h = matmul([ya, yb.reshape(t, -1)], p["ev_w_out"], F32, epilogue="residual", extra=h)
    (xb,) = rmsnorm(h, p["ln_ffn"][0], (BF16,))
    gated, n_cv0 = ffn_in(xb, cv[0], *p["ffn_in"][0], bsz, seq, cfg["ffn_nb"], cfg["ffn_tn"])
    h = matmul([gated], [p["ffn_w_down"][0]], F32, epilogue="residual", extra=h)

    (xn,) = rmsnorm(h, p["ln_mix"][1], (F32,))
    xn3 = xn.reshape(bsz, seq, d)
    prev = jnp.concatenate([sh[0][:, None], xn3[:, :-1]], axis=1).reshape(t, d)
    xr, xw, xk, xv, xa, xg = rwkv_mix(xn, prev, p["rw_mu"])
    r = matmul([xr], [p["rw_w_r"]], F32)
    k = matmul([xk], [p["rw_w_k"]], F32)
    v = matmul([xv], [p["rw_w_v"]], F32)
    wl = matmul([matmul([xw], [p["rw_w1"]], BF16, act="tanh")], [p["rw_w2"]], F32)
    al = matmul([matmul([xa], [p["rw_a1"]], BF16)], [p["rw_a2"]], F32)
    g = matmul([matmul([xg], [p["rw_g1"]], BF16, act="sigmoid")], [p["rw_g2"]], F32)
    as3 = lambda a: a.reshape(bsz, seq, d)
    yo, n_rw = rwkv7(as3(r), as3(k), as3(v), as3(wl), as3(al), as3(g), rw[0], *p["rw_vec"],
                     nb=cfg["rw_nb"], hb=cfg["rw_hb"])
    h = matmul([yo.reshape(t, d)], [p["rw_w_o"]], F32, epilogue="residual", extra=h)
    n_sh = xn3[:, -1]
    (xb,) = rmsnorm(h, p["ln_ffn"][1], (BF16,))
    gated, n_cv1 = ffn_in(xb, cv[1], *p["ffn_in"][1], bsz, seq, cfg["ffn_nb"], cfg["ffn_tn"])
    h = matmul([gated], [p["ffn_w_down"][1]], F32, epilogue="residual", extra=h)

    (y,) = rmsnorm(h, p["ln_final"], (F32,))
    return (y.reshape(bsz, seq, d), n_s5r[None], n_s5i[None], n_hg[None], n_rw[None], n_sh[None],
            jnp.stack([n_cv0, n_cv1]))


PROMPT_CFG = dict(s5_nb=1, hg_nb=4, rw_nb=4, rw_hb=2, ffn_nb=1, ffn_tn=256)
SAMPLE_CFG = dict(s5_nb=32, hg_nb=16, rw_nb=4, rw_hb=2, ffn_nb=128, ffn_tn=512)


def kernel(x_prompt, x_sample, state_s5_re, state_s5_im, state_hgrn, state_rwkv, state_shift, state_conv, meta_tokens, ln_mix, ln_ffn, ln_final, ev_w_in, ev_w_out, s5_lam_re, s5_lam_im, s5_log_step, s5_b_re, s5_b_im, s5_c_re, s5_c_im, s5_d, s5_w_glu, hg_lb, hg_norm_w, rw_mu, rw_w0, rw_w1, rw_w2, rw_a0, rw_a1, rw_a2, rw_g1, rw_g2, rw_k_k, rw_k_a, rw_r_k, rw_w_r, rw_w_k, rw_w_v, rw_w_o, rw_ln_w, rw_ln_b, ffn_w_in, ffn_conv_w, ffn_conv_b, ffn_w_down):
    bf = lambda w: w.astype(BF16)
    lb_all = hg_lower_bounds(hg_lb)
    pwr, pwi, bbr_t, bbi_t = s5_prep(s5_lam_re[0], s5_lam_im[0], s5_log_step[0], s5_b_re[0], s5_b_im[0])
    wbr, wbi, wcr, wci = _s5_block_weights(bbr_t, bbi_t, s5_c_re[0], s5_c_im[0])
    p = {
        "ln_mix": ln_mix, "ln_ffn": ln_ffn, "ln_final": ln_final,
        "ev_w_in": bf(ev_w_in[0]),
        "ev_w_out": [bf(ev_w_out[0][:S5_WIDTH]), bf(ev_w_out[0][S5_WIDTH:])],
        "s5": (pwr, pwi, wbr, wbi, wcr, wci, s5_d[0].reshape(1, S5_WIDTH)),
        "s5_w_glu": bf(s5_w_glu[0]),
        "hg_lb": lb_all[0], "hg_norm_w": hg_norm_w[0],
        "rw_mu": rw_mu[0],
        "rw_w1": bf(rw_w1[0]), "rw_w2": bf(rw_w2[0]), "rw_a1": bf(rw_a1[0]), "rw_a2": bf(rw_a2[0]),
        "rw_g1": bf(rw_g1[0]), "rw_g2": bf(rw_g2[0]),
        "rw_w_r": bf(rw_w_r[0]), "rw_w_k": bf(rw_w_k[0]), "rw_w_v": bf(rw_w_v[0]), "rw_w_o": bf(rw_w_o[0]),
        "rw_vec": (rw_w0[0], rw_a0[0], rw_k_k[0], rw_k_a[0], rw_r_k[0].reshape(D_MODEL), rw_ln_w[0], rw_ln_b[0]),
        "ffn_in": [(bf(ffn_w_in[l][:, :D_FF]), bf(ffn_w_in[l][:, D_FF:]), ffn_conv_w[l], ffn_conv_b[l])
                   for l in range(2)],
        "ffn_w_down": [bf(ffn_w_down[l]) for l in range(2)],
    }

    bsz = x_prompt.shape[0]
    hp = jnp.concatenate([jnp.broadcast_to(meta_tokens[None], (bsz, N_META, D_MODEL)), x_prompt], axis=1)
    zeros = lambda *s: jnp.zeros(s, F32)
    outs_p = _trunk(hp,
                    zeros(1, bsz, S5_GROUPS, S5_STATE), zeros(1, bsz, S5_GROUPS, S5_STATE),
                    zeros(1, bsz, HG_HEADS, HG_K, HG_V), zeros(1, bsz, RW_HEADS, RW_HEAD, RW_HEAD),
                    zeros(1, bsz, D_MODEL), zeros(2, bsz, CONV_W - 1, D_FF), p, PROMPT_CFG)
    outs_s = _trunk(x_sample, state_s5_re, state_s5_im, state_hgrn, state_rwkv, state_shift, state_conv,
                    p, SAMPLE_CFG)
    return (outs_p[0][:, N_META:], outs_s[0]) + tuple(outs_p[1:]) + tuple(outs_s[1:])
```

```python
import functools

import jax
import jax.numpy as jnp
from jax import lax
from jax.experimental import pallas as pl
from jax.experimental.pallas import tpu as pltpu

F32 = jnp.float32
BF16 = jnp.bfloat16

D_MODEL = 2048
N_META = 16
EPS = 1e-6
S5_WIDTH = 1024
S5_GROUP = 16
S5_GROUPS = 64
S5_STATE = 64
S5_CH = S5_GROUPS * S5_STATE
HG_HEADS = 8
HG_K = 128
HG_V = 128
HG_CHUNK = 16
EVEN_IN = 5120
RW_HEAD = 64
RW_HEADS = 32
RW_GN_EPS = 64e-5
D_FF = 5632
CONV_W = 3

LANES = 128
SUBLANES = 8
VMEM_LIMIT = 56 * 1024 * 1024


def _params(*sem):
    return pltpu.CompilerParams(dimension_semantics=sem, vmem_limit_bytes=VMEM_LIMIT)


def _row_tile(t, cap=1024):
    best = None
    for d in range(16, min(t, cap) + 1, 16):
        if t % d == 0:
            best = d
    assert best is not None, t
    return best


def _rms_kernel(x_ref, w_ref, *o_refs):
    x = x_ref[...]
    y = x * lax.rsqrt(jnp.mean(x * x, axis=-1, keepdims=True) + EPS) * w_ref[...]
    for o_ref in o_refs:
        o_ref[...] = y.astype(o_ref.dtype)


def rmsnorm(x, w, dtypes):
    t, d = x.shape
    tm = _row_tile(t)
    spec = pl.BlockSpec((tm, d), lambda i: (i, 0))
    outs = pl.pallas_call(
        _rms_kernel,
        grid=(t // tm,),
        in_specs=[spec, pl.BlockSpec((1, d), lambda i: (0, 0))],
        out_specs=[spec for _ in dtypes],
        out_shape=[jax.ShapeDtypeStruct((t, d), dt) for dt in dtypes],
        compiler_params=_params("parallel"),
        name="rmsnorm",
    )(x, w.reshape(1, d))
    return outs


def _act(x, act):
    if act == "tanh":
        return jnp.tanh(x)
    if act == "sigmoid":
        return jax.nn.sigmoid(x)
    assert act is None
    return x


def _mm_kernel(*refs, n_a, act, epilogue):
    a_refs = refs[:n_a]
    w_refs = refs[n_a:2 * n_a]
    rest = refs[2 * n_a:]
    o_ref = rest[-1]
    acc = jnp.dot(a_refs[0][...], w_refs[0][...], preferred_element_type=F32)
    for a_ref, w_ref in zip(a_refs[1:], w_refs[1:]):
        acc = acc + jnp.dot(a_ref[...], w_ref[...], preferred_element_type=F32)
    acc = _act(acc, act)
    if epilogue == "residual":
        acc = rest[0][...] + acc
    elif epilogue == "glu":
        acc = rest[0][...] * jax.nn.sigmoid(acc)
    o_ref[...] = acc.astype(o_ref.dtype)


MM_VMEM_BUDGET = 40 * 1024 * 1024
MXU_WIDTH = 256


def _mm_tiles(t, k_total, n, out_bytes, has_extra):
    rows = [d for d in range(16, t + 1, 16) if t % d == 0]
    cols = [d for d in range(LANES, n + 1, LANES) if n % d == 0] or [n]
    best, best_score = None, -1.0
    for tm in rows:
        for tn in cols:
            est = 4 * tm * k_total + 4 * k_total * tn + tm * tn * (2 * out_bytes + 4 + (8 if has_extra else 0))
            if est > MM_VMEM_BUDGET:
                continue
            score = tm * tn * (1.0 if tn % MXU_WIDTH == 0 else 0.8)
            if score > best_score:
                best, best_score = (tm, tn), score
    assert best is not None, (t, k_total, n)
    return best


def matmul(a_list, w_list, out_dtype, act=None, epilogue=None, extra=None):
    t = a_list[0].shape[0]
    n = w_list[0].shape[1]
    tm, tn = _mm_tiles(t, sum(a.shape[1] for a in a_list), n, jnp.dtype(out_dtype).itemsize, epilogue is not None)
    in_specs = [pl.BlockSpec((tm, a.shape[1]), lambda i, j: (i, 0)) for a in a_list]
    in_specs += [pl.BlockSpec((w.shape[0], tn), lambda i, j: (0, j)) for w in w_list]
    args = list(a_list) + list(w_list)
    if epilogue is not None:
        in_specs.append(pl.BlockSpec((tm, tn), lambda i, j: (i, j)))
        args.append(extra)
    return pl.pallas_call(
        functools.partial(_mm_kernel, n_a=len(a_list), act=act, epilogue=epilogue),
        grid=(t // tm, n // tn),
        in_specs=in_specs,
        out_specs=pl.BlockSpec((tm, tn), lambda i, j: (i, j)),
        out_shape=jax.ShapeDtypeStruct((t, n), out_dtype),
        compiler_params=_params("parallel", "parallel"),
        name="matmul",
    )(*args)


def _s5_prep_kernel(lr_ref, li_ref, ls_ref, brt_ref, bit_ref, pwr_ref, pwi_ref, bbr_ref, bbi_ref):
    lr = jnp.minimum(lr_ref[...], -1e-4)
    li = li_ref[...]
    dt = jnp.exp(ls_ref[...])
    n = lax.broadcasted_iota(jnp.int32, (SUBLANES, S5_CH), 0).astype(F32) + 1.0
    mag = jnp.exp(n * (lr * dt))
    ang = n * (li * dt)
    pwr = mag * jnp.cos(ang)
    pwi = mag * jnp.sin(ang)
    pwr_ref[...] = pwr
    pwi_ref[...] = pwi
    ar = pwr[0:1]
    ai = pwi[0:1]
    den = lr * lr + li * li
    zr = ((ar - 1.0) * lr + ai * li) / den
    zi = (ai * lr - (ar - 1.0) * li) / den
    br = brt_ref[...]
    bi = bit_ref[...]
    bbr_ref[...] = zr * br - zi * bi
    bbi_ref[...] = zr * bi + zi * br


def s5_prep(lam_re, lam_im, log_step, b_re, b_im):
    lr = lam_re.reshape(1, S5_CH)
    li = lam_im.reshape(1, S5_CH)
    ls = jnp.broadcast_to(log_step[:, None], (S5_GROUPS, S5_STATE)).reshape(1, S5_CH)
    brt = b_re.reshape(S5_CH, S5_GROUP).T
    bit = b_im.reshape(S5_CH, S5_GROUP).T
    return pl.pallas_call(
        _s5_prep_kernel,
        out_shape=[jax.ShapeDtypeStruct((SUBLANES, S5_CH), F32)] * 2
        + [jax.ShapeDtypeStruct((S5_GROUP, S5_CH), F32)] * 2,
        name="s5_prep",
    )(lr, li, ls, brt, bit)


S5_BLK_GROUPS = LANES // S5_GROUP
S5_BLKS = S5_WIDTH // LANES
S5_BLK_CH = S5_BLK_GROUPS * S5_STATE


def _cmul_add(xr, xi, mr, mi, sr, si):
    return xr + mr * sr - mi * si, xi + mr * si + mi * sr


def _s5_kernel(u_ref, h0r_ref, h0i_ref, pwr_ref, pwi_ref, wbr_ref, wbi_ref, wcr_ref, wci_ref, d_ref,
               y_ref, hr_ref, hi_ref, xr_scr, xi_scr, *, nb, seq):
    u2 = u_ref[...].reshape(nb * seq, LANES)
    ub = u2.astype(BF16)
    xr_scr[...] = jnp.dot(ub, wbr_ref[0], preferred_element_type=F32).reshape(nb, seq, S5_BLK_CH)
    xi_scr[...] = jnp.dot(ub, wbi_ref[0], preferred_element_type=F32).reshape(nb, seq, S5_BLK_CH)

    pwr = pwr_ref[...]
    pwi = pwi_ref[...]
    row = lax.broadcasted_iota(jnp.int32, (SUBLANES, S5_BLK_CH), 0)
    steps = []
    for d in (1, 2, 4):
        keep = row >= d
        steps.append((d, jnp.where(keep, pwr[d - 1:d], 0.0)[None], jnp.where(keep, pwi[d - 1:d], 0.0)[None]))
    pr = pwr[None]
    pi = pwi[None]

    def tile(i, carry):
        cr, ci = carry
        o = pl.multiple_of(i * SUBLANES, SUBLANES)
        xr = xr_scr[:, pl.ds(o, SUBLANES), :]
        xi = xi_scr[:, pl.ds(o, SUBLANES), :]
        for d, mr, mi in steps:
            sr = pltpu.roll(xr, d, axis=1)
            si = pltpu.roll(xi, d, axis=1)
            xr, xi = _cmul_add(xr, xi, mr, mi, sr, si)
        xr, xi = _cmul_add(xr, xi, pr, pi, cr, ci)
        xr_scr[:, pl.ds(o, SUBLANES), :] = xr
        xi_scr[:, pl.ds(o, SUBLANES), :] = xi
        return xr[:, SUBLANES - 1:SUBLANES, :], xi[:, SUBLANES - 1:SUBLANES, :]

    hr, hi = lax.fori_loop(0, seq // SUBLANES, tile, (h0r_ref[...], h0i_ref[...]))
    hr_ref[...] = hr
    hi_ref[...] = hi

    xr = xr_scr[...].reshape(nb * seq, S5_BLK_CH).astype(BF16)
    xi = xi_scr[...].reshape(nb * seq, S5_BLK_CH).astype(BF16)
    y = (jnp.dot(xr, wcr_ref[0], preferred_element_type=F32)
         - jnp.dot(xi, wci_ref[0], preferred_element_type=F32)
         + d_ref[...] * u2)
    y_ref[...] = jax.nn.gelu(y).reshape(nb, seq, LANES)


def s5_scan(z3, h0r, h0i, pwr, pwi, wbr, wbi, wcr, wci, d, nb):
    bsz, seq, _ = z3.shape
    assert seq % SUBLANES == 0 and bsz % nb == 0
    seq_blk = pl.BlockSpec((nb, seq, LANES), lambda b, k: (b, 0, k))
    st_blk = pl.BlockSpec((nb, 1, S5_BLK_CH), lambda b, k: (b, 0, k))
    pw_blk = pl.BlockSpec((SUBLANES, S5_BLK_CH), lambda b, k: (0, k))
    wb_blk = pl.BlockSpec((1, LANES, S5_BLK_CH), lambda b, k: (k, 0, 0))
    wc_blk = pl.BlockSpec((1, S5_BLK_CH, LANES), lambda b, k: (k, 0, 0))
    y, hr, hi = pl.pallas_call(
        functools.partial(_s5_kernel, nb=nb, seq=seq),
        grid=(bsz // nb, S5_BLKS),
        in_specs=[seq_blk, st_blk, st_blk, pw_blk, pw_blk, wb_blk, wb_blk, wc_blk, wc_blk,
                  pl.BlockSpec((1, LANES), lambda b, k: (0, k))],
        out_specs=[seq_blk, st_blk, st_blk],
        out_shape=[jax.ShapeDtypeStruct((bsz, seq, S5_WIDTH), F32),
                   jax.ShapeDtypeStruct((bsz, 1, S5_CH), F32),
                   jax.ShapeDtypeStruct((bsz, 1, S5_CH), F32)],
        scratch_shapes=[pltpu.VMEM((nb, seq, S5_BLK_CH), F32), pltpu.VMEM((nb, seq, S5_BLK_CH), F32)],
        compiler_params=_params("parallel", "parallel"),
        name="s5_scan",
    )(z3, h0r.reshape(bsz, 1, S5_CH), h0i.reshape(bsz, 1, S5_CH), pwr, pwi, wbr, wbi, wcr, wci, d)
    return y, hr.reshape(bsz, S5_GROUPS, S5_STATE), hi.reshape(bsz, S5_GROUPS, S5_STATE)


def _s5_block_weights(bbr_t, bbi_t, c_re, c_im):
    eye = jnp.eye(S5_BLK_GROUPS, dtype=F32)

    def wb(bt):
        b4 = bt.reshape(S5_GROUP, S5_BLKS, S5_BLK_GROUPS, S5_STATE)
        w = jnp.einsum("cbgp,hg->bhcgp", b4, eye)
        return w.reshape(S5_BLKS, LANES, S5_BLK_CH).astype(BF16)

    def wc(c):
        c4 = c.reshape(S5_BLKS, S5_BLK_GROUPS, S5_GROUP, S5_STATE)
        w = jnp.einsum("bgcp,hg->bhpgc", c4, eye)
        return w.reshape(S5_BLKS, S5_BLK_CH, LANES).astype(BF16)

    return wb(bbr_t), wb(bbi_t), wc(c_re), wc(c_im)


def _hg_lb_kernel(x_ref, o_ref):
    x = x_ref[...]
    e = jnp.exp(x - jnp.max(x, axis=0, keepdims=True))
    sm = e / jnp.sum(e, axis=0, keepdims=True)
    acc = sm[0:1]
    o_ref[0:1, :] = acc
    for l in range(1, x.shape[0]):
        acc = acc + sm[l:l + 1]
        o_ref[l:l + 1, :] = acc


def hg_lower_bounds(hg_lb):
    return pl.pallas_call(_hg_lb_kernel, out_shape=jax.ShapeDtypeStruct(hg_lb.shape, F32), name="hg_lb")(hg_lb)


def _cumsum_rows(x, n):
    row = lax.broadcasted_iota(jnp.int32, x.shape, 1)
    d = 1
    while d < n:
        x = x + jnp.where(row >= d, pltpu.roll(x, d, axis=1), 0.0)
        d *= 2
    return x


def _hgrn_kernel(q_ref, f_ref, i_ref, g_ref, s0_ref, lb_ref, nw_ref, o_ref, sf_ref, st_scr, *, nb, seq, chunk):
    lb = lb_ref[...][None]
    nw = nw_ref[...][None]
    for b in range(nb):
        st_scr[b] = s0_ref[b, 0].T
    trow = lax.broadcasted_iota(jnp.int32, (chunk, chunk), 0)
    tcol = lax.broadcasted_iota(jnp.int32, (chunk, chunk), 1)
    causal = (tcol <= trow)[None]

    def body(n, carry):
        o = pl.multiple_of(n * chunk, chunk)
        q = q_ref[:, pl.ds(o, chunk), :]
        f = f_ref[:, pl.ds(o, chunk), :]
        v = i_ref[:, pl.ds(o, chunk), :]
        g = g_ref[:, pl.ds(o, chunk), :]
        fg = lb + (1.0 - lb) * jax.nn.sigmoid(f)
        qh = jax.nn.silu(q)
        kh = 1.0 - fg
        bcum = _cumsum_rows(jnp.log(fg), chunk)
        btot = bcum[:, chunk - 1:chunk, :]
        q_in = (qh * jnp.exp(bcum)).astype(BF16)
        k_in = (kh * jnp.exp(-bcum)).astype(BF16)
        k_end = (kh * jnp.exp(btot - bcum)).astype(BF16)
        decay = jnp.exp(btot)
        vb = v.astype(BF16)
        st = st_scr[...]
        att = jnp.einsum("btk,bsk->bts", q_in, k_in, preferred_element_type=F32)
        att = jnp.where(causal, att, 0.0).astype(BF16)
        out = (jnp.einsum("btk,bvk->btv", q_in, st.astype(BF16), preferred_element_type=F32)
               + jnp.einsum("bts,bsv->btv", att, vb, preferred_element_type=F32))
        st_scr[...] = st * decay + jnp.einsum("bsv,bsk->bvk", vb, k_end, preferred_element_type=F32)
        out = out * lax.rsqrt(jnp.mean(out * out, axis=-1, keepdims=True) + EPS) * nw
        o_ref[:, pl.ds(o, chunk), :] = (out * jax.nn.silu(g)).astype(o_ref.dtype)
        return carry

    n_chunks = seq // chunk
    lax.fori_loop(0, n_chunks, body, 0, unroll=3 if n_chunks % 3 == 0 else 1)
    for b in range(nb):
        sf_ref[b, 0] = st_scr[b].T


def hgrn2(z3, s0, lb, norm_w, nb):
    bsz, seq, _ = z3.shape
    chunk = min(HG_CHUNK, seq)
    assert seq % chunk == 0 and bsz % nb == 0

    def col(base):
        return pl.BlockSpec((nb, seq, LANES), lambda b, h, base=base: (b, 0, base + h))

    st_blk = pl.BlockSpec((nb, 1, HG_K, HG_V), lambda b, h: (b, h, 0, 0))
    vec = pl.BlockSpec((1, LANES), lambda b, h: (0, h))
    out, sf = pl.pallas_call(
        functools.partial(_hgrn_kernel, nb=nb, seq=seq, chunk=chunk),
        grid=(bsz // nb, HG_HEADS),
        in_specs=[col(8), col(16), col(24), col(32), st_blk, vec,
                  pl.BlockSpec((1, LANES), lambda b, h: (0, 0))],
        out_specs=[pl.BlockSpec((nb, seq, LANES), lambda b, h: (b, 0, h)), st_blk],
        out_shape=[jax.ShapeDtypeStruct((bsz, seq, HG_HEADS * HG_V), BF16),
                   jax.ShapeDtypeStruct((bsz, HG_HEADS, HG_K, HG_V), F32)],
        scratch_shapes=[pltpu.VMEM((nb, HG_V, HG_K), F32)],
        compiler_params=_params("parallel", "parallel"),
        name="hgrn2",
    )(z3, z3, z3, z3, s0, lb.reshape(1, HG_HEADS * HG_K), norm_w.reshape(1, HG_V))
    return out, sf


def _mix_kernel(x_ref, p_ref, mu_ref, *o_refs):
    x = x_ref[...]
    xx = p_ref[...] - x
    for j, o_ref in enumerate(o_refs):
        o_ref[...] = (x + xx * mu_ref[j:j + 1, :]).astype(o_ref.dtype)


def rwkv_mix(xn, prev, mu):
    t, d = xn.shape
    tm = _row_tile(t, cap=512)
    spec = pl.BlockSpec((tm, d), lambda i: (i, 0))
    return pl.pallas_call(
        _mix_kernel,
        grid=(t // tm,),
        in_specs=[spec, spec, pl.BlockSpec((6, d), lambda i: (0, 0))],
        out_specs=[spec] * 6,
        out_shape=[jax.ShapeDtypeStruct((t, d), BF16)] * 6,
        compiler_params=_params("parallel"),
        name="rwkv_mix",
    )(xn, prev, mu)


RW_PAIR = LANES // RW_HEAD
RW_SOLVE_BLOCK = 16


def _rwkv_kernel(r_ref, k_ref, v_ref, wl_ref, al_ref, g_ref, s0_ref,
                 w0_ref, a0_ref, kk_ref, ka_ref, rk_ref, lnw_ref, lnb_ref,
                 o_ref, sf_ref, s_scr, *, nb, hp, chunk):
    nu = hp * nb
    step = pl.program_id(2)
    lane = lax.broadcasted_iota(jnp.int32, (1, 1, LANES), 2)
    head1 = lane >= RW_HEAD

    def units(x):
        return jnp.concatenate([x[:, :, p * LANES:(p + 1) * LANES] for p in range(hp)], axis=0)

    def unit_rows(ref):
        return units(jnp.broadcast_to(ref[...][None], (nb, 1, hp * LANES)))

    w0, a0, k_k, k_a, r_k, ln_w, ln_b = (unit_rows(p) for p in
                                         (w0_ref, a0_ref, kk_ref, ka_ref, rk_ref, lnw_ref, lnb_ref))
    sq_row = lax.broadcasted_iota(jnp.int32, (LANES, LANES), 0) >= RW_HEAD
    sq_col = lax.broadcasted_iota(jnp.int32, (LANES, LANES), 1) >= RW_HEAD
    same_head = sq_row == sq_col
    ones_bd = same_head.astype(BF16)

    @pl.when(step == 0)
    def _():
        zero = jnp.zeros((nb, RW_HEAD, RW_HEAD), F32)
        for p in range(hp):
            top = jnp.concatenate([s0_ref[:, RW_PAIR * p], zero], axis=-1)
            bot = jnp.concatenate([zero, s0_ref[:, RW_PAIR * p + 1]], axis=-1)
            s_scr[p * nb:(p + 1) * nb] = jnp.concatenate([top, bot], axis=1)

    def bdot(spec, a, b):
        return jnp.einsum(spec, a.astype(BF16), b.astype(BF16), preferred_element_type=F32)

    def head_sum(x):
        x2 = x.reshape(nu * chunk, LANES)
        hi = x2.astype(BF16)
        lo = (x2 - hi.astype(F32)).astype(BF16)
        s = jnp.dot(hi, ones_bd, preferred_element_type=F32) + jnp.dot(lo, ones_bd, preferred_element_type=F32)
        return s.reshape(nu, chunk, LANES)

    def stack_heads(x):
        return jnp.concatenate([jnp.where(head1, 0.0, x), jnp.where(head1, x, 0.0)], axis=1).astype(BF16)

    r, k, v, wl, al, g = (units(ref[...]) for ref in (r_ref, k_ref, v_ref, wl_ref, al_ref, g_ref))
    lw = -jnp.exp(-jax.nn.softplus(-(w0 + wl)) - 0.5)
    ag = jax.nn.sigmoid(a0 + al)
    kk = k * k_k
    kk = kk * (1.0 / jnp.maximum(jnp.sqrt(head_sum(kk * kk)), 1e-12))
    k2 = k * (1.0 + (ag - 1.0) * k_a)
    cl = _cumsum_rows(lw, chunk)
    e_pos = jnp.exp(cl)
    e_neg = jnp.exp(-cl)
    at = (-kk) * jnp.exp(cl - lw)
    bt = (kk * ag) * e_neg
    kt = k2 * e_neg
    rt = r * e_pos
    wc = e_pos[:, chunk - 1:chunk, :]
    trow = lax.broadcasted_iota(jnp.int32, (chunk, RW_PAIR * chunk), 0)
    tcol = lax.broadcasted_iota(jnp.int32, (chunk, RW_PAIR * chunk), 1)
    tcol = jnp.where(tcol >= chunk, tcol - chunk, tcol)
    strict = (tcol < trow)[None]
    incl = (tcol <= trow)[None]
    x2 = jnp.concatenate([at, rt], axis=1)
    pb = bdot("utc,usc->uts", x2, stack_heads(bt))
    pk = bdot("utc,usc->uts", x2, stack_heads(kt))
    lab = jnp.where(strict, pb[:, :chunk], 0.0)
    lak = jnp.where(strict, pk[:, :chunk], 0.0)
    arb = jnp.where(incl, pb[:, chunk:], 0.0)
    ark = jnp.where(incl, pk[:, chunk:], 0.0)
    v_bd = stack_heads(v)
    xa = at
    xv = bdot("uts,usc->utc", lak, v_bd)
    sub = min(RW_SOLVE_BLOCK, chunk)
    done_a, done_v = [], []
    for lo in range(0, chunk, sub):
        xa_i = xa[:, lo:lo + sub, :]
        xv_i = xv[:, lo:lo + sub, :]
        if lo:
            pad = jnp.zeros((nu, chunk - lo, LANES), F32)
            prev = jnp.concatenate([stack_heads(jnp.concatenate(done_a + [pad], axis=1)),
                                    stack_heads(jnp.concatenate(done_v + [pad], axis=1))], axis=-1)
            upd = bdot("uts,usc->utc", lab[:, lo:lo + sub, :], prev)
            xa_i = xa_i + upd[:, :, :LANES]
            xv_i = xv_i + upd[:, :, LANES:]
        l0 = lab[:, lo:lo + sub, lo:lo + sub]
        l1 = lab[:, lo:lo + sub, chunk + lo:chunk + lo + sub]
        for s in range(sub - 1):
            m = jnp.where(head1, l1[:, :, s:s + 1], l0[:, :, s:s + 1])
            xa_i = xa_i + m * xa_i[:, s:s + 1, :]
            xv_i = xv_i + m * xv_i[:, s:s + 1, :]
        done_a.append(xa_i)
        done_v.append(xv_i)
    ah = jnp.concatenate(done_a, axis=1)
    vh = jnp.concatenate(done_v, axis=1)
    both = bdot("uts,usc->utc", arb, jnp.concatenate([stack_heads(ah), stack_heads(vh)], axis=-1))
    rh = rt + both[:, :, :LANES]
    yh = both[:, :, LANES:] + bdot("uts,usc->utc", ark, v_bd)
    gp = jnp.where(same_head, bdot("utj,utk->ujk", ah, bt), 0.0)
    ht = jnp.where(same_head, bdot("utv,utk->uvk", jnp.concatenate([vh, v], axis=1),
                                   jnp.concatenate([bt, kt], axis=1)), 0.0)
    st = s_scr[...]
    y = bdot("utk,uvk->utv", rh, st) + yh
    st = (st + bdot("uvj,ujk->uvk", st, gp) + ht) * wc
    s_scr[...] = st
    inv_n = 1.0 / RW_HEAD
    yc = y - head_sum(y) * inv_n
    var = head_sum(yc * yc) * inv_n
    y = yc * lax.rsqrt(var + RW_GN_EPS) * ln_w + ln_b
    y = y + head_sum(r * k2 * r_k) * v
    out = (y * g).astype(o_ref.dtype)
    for p in range(hp):
        o_ref[:, :, p * LANES:(p + 1) * LANES] = out[p * nb:(p + 1) * nb]

    @pl.when(step == pl.num_programs(2) - 1)
    def _():
        for p in range(hp):
            sf_ref[:, RW_PAIR * p] = st[p * nb:(p + 1) * nb, :RW_HEAD, :RW_HEAD]
            sf_ref[:, RW_PAIR * p + 1] = st[p * nb:(p + 1) * nb, RW_HEAD:, RW_HEAD:]


def _rwkv_chunk(seq):
    for c in (48, 32, 16, 8):
        if seq % c == 0:
            return c
    raise ValueError(seq)


def rwkv7(r, k, v, wl, al, g, s0, w0, a0, k_k, k_a, r_k, ln_w, ln_b, nb, hp):
    bsz, seq, d = r.shape
    chunk = _rwkv_chunk(seq)
    heads = hp * RW_PAIR
    assert bsz % nb == 0 and RW_HEADS % heads == 0
    seq_blk = pl.BlockSpec((nb, chunk, hp * LANES), lambda h, b, t: (b, t, h))
    st_blk = pl.BlockSpec((nb, heads, RW_HEAD, RW_HEAD), lambda h, b, t: (b, h, 0, 0))
    vec = pl.BlockSpec((1, hp * LANES), lambda h, b, t: (0, h))
    out, sf = pl.pallas_call(
        functools.partial(_rwkv_kernel, nb=nb, hp=hp, chunk=chunk),
        grid=(RW_HEADS // heads, bsz // nb, seq // chunk),
        in_specs=[seq_blk] * 6 + [st_blk] + [vec] * 7,
        out_specs=[seq_blk, st_blk],
        out_shape=[jax.ShapeDtypeStruct((bsz, seq, d), BF16),
                   jax.ShapeDtypeStruct((bsz, RW_HEADS, RW_HEAD, RW_HEAD), F32)],
        scratch_shapes=[pltpu.VMEM((hp * nb, LANES, LANES), F32)],
        compiler_params=_params("parallel", "parallel", "arbitrary"),
        name="rwkv7",
    )(r, k, v, wl, al, g, s0, *(p.reshape(1, d) for p in (w0, a0, k_k, k_a, r_k, ln_w, ln_b)))
    return out, sf


def _ffn_in_kernel(x_ref, wa_ref, wv_ref, e_ref, cw_ref, cb_ref, o_ref, st_ref, scr, *, nb, seq, sb, sr):
    tn = wa_ref.shape[1]
    cw = cw_ref[...]
    cb = cb_ref[...][None]
    scr[:, 0:SUBLANES, :] = e_ref[...]
    for b0 in range(0, nb, sb):
        for r0 in range(0, seq, sr):
            lo = b0 * seq + r0
            x = x_ref[lo:lo + sb * sr, :]
            a = jnp.dot(x, wa_ref[...], preferred_element_type=F32).reshape(sb, sr, tn)
            v = jnp.dot(x, wv_ref[...], preferred_element_type=F32).reshape(sb, sr, tn)
            scr[b0:b0 + sb, SUBLANES + r0:SUBLANES + r0 + sr, :] = a
            c = cb + cw[CONV_W - 1:CONV_W][None] * a
            for j in range(CONV_W - 1):
                first = SUBLANES + r0 - (CONV_W - 1 - j)
                c = c + cw[j:j + 1][None] * scr[b0:b0 + sb, first:first + sr, :]
            o_ref[lo:lo + sb * sr, :] = (jax.nn.gelu(c) * v).reshape(sb * sr, tn).astype(o_ref.dtype)
    st_ref[...] = scr[:, SUBLANES + seq - (CONV_W - 1):SUBLANES + seq, :]


def ffn_in(xb, conv0, w_a, w_v, conv_w, conv_b, bsz, seq, nb, tn, sub):
    t, d = xb.shape
    sb, sr = sub
    assert t == bsz * seq and bsz % nb == 0 and D_FF % tn == 0
    assert nb % sb == 0 and seq % sr == 0 and sr % SUBLANES == 0 and (sr == seq or nb == sb == 1)
    halo = jnp.pad(conv0, ((0, 0), (SUBLANES - (CONV_W - 1), 0), (0, 0)))
    col = lambda i, j: (0, j)
    out, st = pl.pallas_call(
        functools.partial(_ffn_in_kernel, nb=nb, seq=seq, sb=sb, sr=sr),
        grid=(bsz // nb, D_FF // tn),
        in_specs=[pl.BlockSpec((nb * seq, d), lambda i, j: (i, 0)),
                  pl.BlockSpec((d, tn), col), pl.BlockSpec((d, tn), col),
                  pl.BlockSpec((nb, SUBLANES, tn), lambda i, j: (i, 0, j)),
                  pl.BlockSpec((CONV_W, tn), col), pl.BlockSpec((1, tn), col)],
        out_specs=[pl.BlockSpec((nb * seq, tn), lambda i, j: (i, j)),
                   pl.BlockSpec((nb, CONV_W - 1, tn), lambda i, j: (i, 0, j))],
        out_shape=[jax.ShapeDtypeStruct((t, D_FF), BF16),
                   jax.ShapeDtypeStruct((bsz, CONV_W - 1, D_FF), F32)],
        scratch_shapes=[pltpu.VMEM((nb, SUBLANES + seq, tn), F32)],
        compiler_params=_params("parallel", "parallel"),
        name="ffn_in",
    )(xb, w_a, w_v, halo, conv_w, conv_b.reshape(1, D_FF))
    return out, st


def _trunk(h3, s5r, s5i, hg, rw, sh, cv, p, cfg):
    bsz, seq, d = h3.shape
    t = bsz * seq
    h = h3.reshape(t, d)

    (xb,) = rmsnorm(h, p["ln_mix"][0], (BF16,))
    z = matmul([xb], [p["ev_w_in"]], F32).reshape(bsz, seq, EVEN_IN)
    ys5, n_s5r, n_s5i = s5_scan(z, s5r[0], s5i[0], *p["s5"], nb=cfg["s5_nb"])
    ys5 = ys5.reshape(t, S5_WIDTH)
    ya = matmul([ys5.astype(BF16)], [p["s5_w_glu"]], BF16, epilogue="glu", extra=ys5)
    yb, n_hg = hgrn2(z, hg[0], p["hg_lb"], p["hg_norm_w"], nb=cfg["hg_nb"])
    h = matmul([ya, yb.reshape(t, -1)], p["ev_w_out"], F32, epilogue="residual", extra=h)
    (xb,) = rmsnorm(h, p["ln_ffn"][0], (BF16,))
    gated, n_cv0 = ffn_in(xb, cv[0], *p["ffn_in"][0], bsz, seq, cfg["ffn_nb"], cfg["ffn_tn"], cfg["ffn_sub"])
    h = matmul([gated], [p["ffn_w_down"][0]], F32, epilogue="residual", extra=h)

    (xn,) = rmsnorm(h, p["ln_mix"][1], (F32,))
    xn3 = xn.reshape(bsz, seq, d)
    prev = jnp.concatenate([sh[0][:, None], xn3[:, :-1]], axis=1).reshape(t, d)
    xr, xw, xk, xv, xa, xg = rwkv_mix(xn, prev, p["rw_mu"])
    r = matmul([xr], [p["rw_w_r"]], F32)
    k = matmul([xk], [p["rw_w_k"]], F32)
    v = matmul([xv], [p["rw_w_v"]], F32)
    wl = matmul([matmul([xw], [p["rw_w1"]], BF16, act="tanh")], [p["rw_w2"]], F32)
    al = matmul([matmul([xa], [p["rw_a1"]], BF16)], [p["rw_a2"]], F32)
    g = matmul([matmul([xg], [p["rw_g1"]], BF16, act="sigmoid")], [p["rw_g2"]], F32)
    as3 = lambda a: a.reshape(bsz, seq, d)
    yo, n_rw = rwkv7(as3(r), as3(k), as3(v), as3(wl), as3(al), as3(g), rw[0], *p["rw_vec"],
                     nb=cfg["rw_nb"], hp=cfg["rw_hp"])
    h = matmul([yo.reshape(t, d)], [p["rw_w_o"]], F32, epilogue="residual", extra=h)
    n_sh = xn3[:, -1]
    (xb,) = rmsnorm(h, p["ln_ffn"][1], (BF16,))
    gated, n_cv1 = ffn_in(xb, cv[1], *p["ffn_in"][1], bsz, seq, cfg["ffn_nb"], cfg["ffn_tn"], cfg["ffn_sub"])
    h = matmul([gated], [p["ffn_w_down"][1]], F32, epilogue="residual", extra=h)

    (y,) = rmsnorm(h, p["ln_final"], (F32,))
    return (y.reshape(bsz, seq, d), n_s5r[None], n_s5i[None], n_hg[None], n_rw[None], n_sh[None],
            jnp.stack([n_cv0, n_cv1]))


PROMPT_CFG = dict(s5_nb=1, hg_nb=4, rw_nb=4, rw_hp=4, ffn_nb=1, ffn_tn=512, ffn_sub=(1, 688))
SAMPLE_CFG = dict(s5_nb=32, hg_nb=16, rw_nb=16, rw_hp=2, ffn_nb=128, ffn_tn=512, ffn_sub=(32, 8))


def kernel(x_prompt, x_sample, state_s5_re, state_s5_im, state_hgrn, state_rwkv, state_shift, state_conv, meta_tokens, ln_mix, ln_ffn, ln_final, ev_w_in, ev_w_out, s5_lam_re, s5_lam_im, s5_log_step, s5_b_re, s5_b_im, s5_c_re, s5_c_im, s5_d, s5_w_glu, hg_lb, hg_norm_w, rw_mu, rw_w0, rw_w1, rw_w2, rw_a0, rw_a1, rw_a2, rw_g1, rw_g2, rw_k_k, rw_k_a, rw_r_k, rw_w_r, rw_w_k, rw_w_v, rw_w_o, rw_ln_w, rw_ln_b, ffn_w_in, ffn_conv_w, ffn_conv_b, ffn_w_down):
    bf = lambda w: w.astype(BF16)
    lb_all = hg_lower_bounds(hg_lb)
    pwr, pwi, bbr_t, bbi_t = s5_prep(s5_lam_re[0], s5_lam_im[0], s5_log_step[0], s5_b_re[0], s5_b_im[0])
    wbr, wbi, wcr, wci = _s5_block_weights(bbr_t, bbi_t, s5_c_re[0], s5_c_im[0])
    p = {
        "ln_mix": ln_mix, "ln_ffn": ln_ffn, "ln_final": ln_final,
        "ev_w_in": bf(ev_w_in[0]),
        "ev_w_out": [bf(ev_w_out[0][:S5_WIDTH]), bf(ev_w_out[0][S5_WIDTH:])],
        "s5": (pwr, pwi, wbr, wbi, wcr, wci, s5_d[0].reshape(1, S5_WIDTH)),
        "s5_w_glu": bf(s5_w_glu[0]),
        "hg_lb": lb_all[0], "hg_norm_w": hg_norm_w[0],
        "rw_mu": rw_mu[0],
        "rw_w1": bf(rw_w1[0]), "rw_w2": bf(rw_w2[0]), "rw_a1": bf(rw_a1[0]), "rw_a2": bf(rw_a2[0]),
        "rw_g1": bf(rw_g1[0]), "rw_g2": bf(rw_g2[0]),
        "rw_w_r": bf(rw_w_r[0]), "rw_w_k": bf(rw_w_k[0]), "rw_w_v": bf(rw_w_v[0]), "rw_w_o": bf(rw_w_o[0]),
        "rw_vec": (rw_w0[0], rw_a0[0], rw_k_k[0], rw_k_a[0], rw_r_k[0].reshape(D_MODEL), rw_ln_w[0], rw_ln_b[0]),
        "ffn_in": [(bf(ffn_w_in[l][:, :D_FF]), bf(ffn_w_in[l][:, D_FF:]), ffn_conv_w[l], ffn_conv_b[l])
                   for l in range(2)],
        "ffn_w_down": [bf(ffn_w_down[l]) for l in range(2)],
    }

    bsz = x_prompt.shape[0]
    hp = jnp.concatenate([jnp.broadcast_to(meta_tokens[None], (bsz, N_META, D_MODEL)), x_prompt], axis=1)
    zeros = lambda *s: jnp.zeros(s, F32)
    outs_p = _trunk(hp,
                    zeros(1, bsz, S5_GROUPS, S5_STATE), zeros(1, bsz, S5_GROUPS, S5_STATE),
                    zeros(1, bsz, HG_HEADS, HG_K, HG_V), zeros(1, bsz, RW_HEADS, RW_HEAD, RW_HEAD),
                    zeros(1, bsz, D_MODEL), zeros(2, bsz, CONV_W - 1, D_FF), p, PROMPT_CFG)
    outs_s = _trunk(x_sample, state_s5_re, state_s5_im, state_hgrn, state_rwkv, state_shift, state_conv,
                    p, SAMPLE_CFG)
    return (outs_p[0][:, N_META:], outs_s[0]) + tuple(outs_p[1:]) + tuple(outs_s[1:])
```

```python
import functools

import jax
import jax.numpy as jnp
from jax import lax
from jax.experimental import pallas as pl
from jax.experimental.pallas import tpu as pltpu

F32 = jnp.float32
BF16 = jnp.bfloat16

D_MODEL = 2048
N_META = 16
EPS = 1e-6
S5_WIDTH = 1024
S5_GROUP = 16
S5_GROUPS = 64
S5_STATE = 64
S5_CH = S5_GROUPS * S5_STATE
HG_HEADS = 8
HG_K = 128
HG_V = 128
HG_CHUNK = 16
EVEN_IN = 5120
RW_HEAD = 64
RW_HEADS = 32
RW_GN_EPS = 64e-5
D_FF = 5632
CONV_W = 3

LANES = 128
SUBLANES = 8
VMEM_LIMIT = 56 * 1024 * 1024


def _params(*sem):
    return pltpu.CompilerParams(dimension_semantics=sem, vmem_limit_bytes=VMEM_LIMIT)


def _row_tile(t, cap=1024):
    best = None
    for d in range(16, min(t, cap) + 1, 16):
        if t % d == 0:
            best = d
    assert best is not None, t
    return best


def _rms_kernel(x_ref, w_ref, *o_refs):
    x = x_ref[...]
    y = x * lax.rsqrt(jnp.mean(x * x, axis=-1, keepdims=True) + EPS) * w_ref[...]
    for o_ref in o_refs:
        o_ref[...] = y.astype(o_ref.dtype)


def rmsnorm(x, w, dtypes):
    t, d = x.shape
    tm = _row_tile(t)
    spec = pl.BlockSpec((tm, d), lambda i: (i, 0))
    outs = pl.pallas_call(
        _rms_kernel,
        grid=(t // tm,),
        in_specs=[spec, pl.BlockSpec((1, d), lambda i: (0, 0))],
        out_specs=[spec for _ in dtypes],
        out_shape=[jax.ShapeDtypeStruct((t, d), dt) for dt in dtypes],
        compiler_params=_params("parallel"),
        name="rmsnorm",
    )(x, w.reshape(1, d))
    return outs


def _act(x, act):
    if act == "tanh":
        return jnp.tanh(x)
    if act == "sigmoid":
        return jax.nn.sigmoid(x)
    assert act is None
    return x


def _mm_kernel(*refs, n_a, act, epilogue):
    a_refs = refs[:n_a]
    w_refs = refs[n_a:2 * n_a]
    rest = refs[2 * n_a:]
    o_ref = rest[-1]
    acc = jnp.dot(a_refs[0][...], w_refs[0][...], preferred_element_type=F32)
    for a_ref, w_ref in zip(a_refs[1:], w_refs[1:]):
        acc = acc + jnp.dot(a_ref[...], w_ref[...], preferred_element_type=F32)
    acc = _act(acc, act)
    if epilogue == "residual":
        acc = rest[0][...] + acc
    elif epilogue == "glu":
        acc = rest[0][...] * jax.nn.sigmoid(acc)
    o_ref[...] = acc.astype(o_ref.dtype)


MM_VMEM_BUDGET = 40 * 1024 * 1024
MXU_WIDTH = 256


def _mm_tiles(t, k_total, n, out_bytes, has_extra):
    rows = [d for d in range(16, t + 1, 16) if t % d == 0]
    cols = [d for d in range(LANES, n + 1, LANES) if n % d == 0] or [n]
    best, best_score = None, -1.0
    for tm in rows:
        for tn in cols:
            est = 4 * tm * k_total + 4 * k_total * tn + tm * tn * (2 * out_bytes + 4 + (8 if has_extra else 0))
            if est > MM_VMEM_BUDGET:
                continue
            score = tm * tn * (1.0 if tn % MXU_WIDTH == 0 else 0.8)
            if score > best_score:
                best, best_score = (tm, tn), score
    assert best is not None, (t, k_total, n)
    return best


def matmul(a_list, w_list, out_dtype, act=None, epilogue=None, extra=None):
    t = a_list[0].shape[0]
    n = w_list[0][0].shape[2] if isinstance(w_list[0], tuple) else w_list[0].shape[1]
    tm, tn = _mm_tiles(t, sum(a.shape[1] for a in a_list), n, jnp.dtype(out_dtype).itemsize, epilogue is not None)
    in_specs = [pl.BlockSpec((tm, a.shape[1]), lambda i, j: (i, 0)) for a in a_list]
    args = list(a_list)
    for a, w in zip(a_list, w_list):
        if isinstance(w, tuple):
            w, layer, kblk = w
            in_specs.append(pl.BlockSpec((None, a.shape[1], tn), lambda i, j, layer=layer, kblk=kblk: (layer, kblk, j)))
        else:
            in_specs.append(pl.BlockSpec((w.shape[0], tn), lambda i, j: (0, j)))
        args.append(w)
    if epilogue is not None:
        in_specs.append(pl.BlockSpec((tm, tn), lambda i, j: (i, j)))
        args.append(extra)
    return pl.pallas_call(
        functools.partial(_mm_kernel, n_a=len(a_list), act=act, epilogue=epilogue),
        grid=(t // tm, n // tn),
        in_specs=in_specs,
        out_specs=pl.BlockSpec((tm, tn), lambda i, j: (i, j)),
        out_shape=jax.ShapeDtypeStruct((t, n), out_dtype),
        compiler_params=_params("parallel", "parallel"),
        name="matmul",
    )(*args)


def _s5_prep_kernel(lr_ref, li_ref, ls_ref, brt_ref, bit_ref, pwr_ref, pwi_ref, bbr_ref, bbi_ref):
    lr = jnp.minimum(lr_ref[...], -1e-4)
    li = li_ref[...]
    dt = jnp.exp(ls_ref[...])
    n = lax.broadcasted_iota(jnp.int32, (SUBLANES, S5_CH), 0).astype(F32) + 1.0
    mag = jnp.exp(n * (lr * dt))
    ang = n * (li * dt)
    pwr = mag * jnp.cos(ang)
    pwi = mag * jnp.sin(ang)
    pwr_ref[...] = pwr
    pwi_ref[...] = pwi
    ar = pwr[0:1]
    ai = pwi[0:1]
    den = lr * lr + li * li
    zr = ((ar - 1.0) * lr + ai * li) / den
    zi = (ai * lr - (ar - 1.0) * li) / den
    br = brt_ref[...]
    bi = bit_ref[...]
    bbr_ref[...] = zr * br - zi * bi
    bbi_ref[...] = zr * bi + zi * br


def s5_prep(lam_re, lam_im, log_step, b_re, b_im):
    lr = lam_re.reshape(1, S5_CH)
    li = lam_im.reshape(1, S5_CH)
    ls = jnp.broadcast_to(log_step[:, None], (S5_GROUPS, S5_STATE)).reshape(1, S5_CH)
    brt = b_re.reshape(S5_CH, S5_GROUP).T
    bit = b_im.reshape(S5_CH, S5_GROUP).T
    return pl.pallas_call(
        _s5_prep_kernel,
        out_shape=[jax.ShapeDtypeStruct((SUBLANES, S5_CH), F32)] * 2
        + [jax.ShapeDtypeStruct((S5_GROUP, S5_CH), F32)] * 2,
        name="s5_prep",
    )(lr, li, ls, brt, bit)


S5_BLK_GROUPS = LANES // S5_GROUP
S5_BLKS = S5_WIDTH // LANES
S5_BLK_CH = S5_BLK_GROUPS * S5_STATE


def _cmul_add(xr, xi, mr, mi, sr, si):
    return xr + mr * sr - mi * si, xi + mr * si + mi * sr


def _s5_kernel(u_ref, h0r_ref, h0i_ref, pwr_ref, pwi_ref, wbr_ref, wbi_ref, wcr_ref, wci_ref, d_ref,
               y_ref, hr_ref, hi_ref, xr_scr, xi_scr, *, nb, seq):
    u2 = u_ref[...].reshape(nb * seq, LANES)
    ub = u2.astype(BF16)
    xr_scr[...] = jnp.dot(ub, wbr_ref[0], preferred_element_type=F32).reshape(nb, seq, S5_BLK_CH)
    xi_scr[...] = jnp.dot(ub, wbi_ref[0], preferred_element_type=F32).reshape(nb, seq, S5_BLK_CH)

    pwr = pwr_ref[...]
    pwi = pwi_ref[...]
    row = lax.broadcasted_iota(jnp.int32, (SUBLANES, S5_BLK_CH), 0)
    steps = []
    for d in (1, 2, 4):
        keep = row >= d
        steps.append((d, jnp.where(keep, pwr[d - 1:d], 0.0)[None], jnp.where(keep, pwi[d - 1:d], 0.0)[None]))
    pr = pwr[None]
    pi = pwi[None]

    def tile(i, carry):
        cr, ci = carry
        o = pl.multiple_of(i * SUBLANES, SUBLANES)
        xr = xr_scr[:, pl.ds(o, SUBLANES), :]
        xi = xi_scr[:, pl.ds(o, SUBLANES), :]
        for d, mr, mi in steps:
            sr = pltpu.roll(xr, d, axis=1)
            si = pltpu.roll(xi, d, axis=1)
            xr, xi = _cmul_add(xr, xi, mr, mi, sr, si)
        xr, xi = _cmul_add(xr, xi, pr, pi, cr, ci)
        xr_scr[:, pl.ds(o, SUBLANES), :] = xr
        xi_scr[:, pl.ds(o, SUBLANES), :] = xi
        return xr[:, SUBLANES - 1:SUBLANES, :], xi[:, SUBLANES - 1:SUBLANES, :]

    hr, hi = lax.fori_loop(0, seq // SUBLANES, tile, (h0r_ref[...], h0i_ref[...]))
    hr_ref[...] = hr
    hi_ref[...] = hi

    xr = xr_scr[...].reshape(nb * seq, S5_BLK_CH).astype(BF16)
    xi = xi_scr[...].reshape(nb * seq, S5_BLK_CH).astype(BF16)
    y = (jnp.dot(xr, wcr_ref[0], preferred_element_type=F32)
         - jnp.dot(xi, wci_ref[0], preferred_element_type=F32)
         + d_ref[...] * u2)
    y_ref[...] = jax.nn.gelu(y).reshape(nb, seq, LANES)


def s5_scan(z3, h0r, h0i, pwr, pwi, wbr, wbi, wcr, wci, d, nb):
    bsz, seq, _ = z3.shape
    assert seq % SUBLANES == 0 and bsz % nb == 0
    seq_blk = pl.BlockSpec((nb, seq, LANES), lambda b, k: (b, 0, k))
    st_blk = pl.BlockSpec((nb, 1, S5_BLK_CH), lambda b, k: (b, 0, k))
    pw_blk = pl.BlockSpec((SUBLANES, S5_BLK_CH), lambda b, k: (0, k))
    wb_blk = pl.BlockSpec((1, LANES, S5_BLK_CH), lambda b, k: (k, 0, 0))
    wc_blk = pl.BlockSpec((1, S5_BLK_CH, LANES), lambda b, k: (k, 0, 0))
    y, hr, hi = pl.pallas_call(
        functools.partial(_s5_kernel, nb=nb, seq=seq),
        grid=(bsz // nb, S5_BLKS),
        in_specs=[seq_blk, st_blk, st_blk, pw_blk, pw_blk, wb_blk, wb_blk, wc_blk, wc_blk,
                  pl.BlockSpec((1, LANES), lambda b, k: (0, k))],
        out_specs=[seq_blk, st_blk, st_blk],
        out_shape=[jax.ShapeDtypeStruct((bsz, seq, S5_WIDTH), F32),
                   jax.ShapeDtypeStruct((bsz, 1, S5_CH), F32),
                   jax.ShapeDtypeStruct((bsz, 1, S5_CH), F32)],
        scratch_shapes=[pltpu.VMEM((nb, seq, S5_BLK_CH), F32), pltpu.VMEM((nb, seq, S5_BLK_CH), F32)],
        compiler_params=_params("parallel", "parallel"),
        name="s5_scan",
    )(z3, h0r.reshape(bsz, 1, S5_CH), h0i.reshape(bsz, 1, S5_CH), pwr, pwi, wbr, wbi, wcr, wci, d)
    return y, hr.reshape(bsz, S5_GROUPS, S5_STATE), hi.reshape(bsz, S5_GROUPS, S5_STATE)


def _s5_block_weights(bbr_t, bbi_t, c_re, c_im):
    eye = jnp.eye(S5_BLK_GROUPS, dtype=F32)

    def wb(bt):
        b4 = bt.reshape(S5_GROUP, S5_BLKS, S5_BLK_GROUPS, S5_STATE)
        w = jnp.einsum("cbgp,hg->bhcgp", b4, eye)
        return w.reshape(S5_BLKS, LANES, S5_BLK_CH).astype(BF16)

    def wc(c):
        c4 = c.reshape(S5_BLKS, S5_BLK_GROUPS, S5_GROUP, S5_STATE)
        w = jnp.einsum("bgcp,hg->bhpgc", c4, eye)
        return w.reshape(S5_BLKS, S5_BLK_CH, LANES).astype(BF16)

    return wb(bbr_t), wb(bbi_t), wc(c_re), wc(c_im)


def _hg_lb_kernel(x_ref, o_ref):
    x = x_ref[...]
    e = jnp.exp(x - jnp.max(x, axis=0, keepdims=True))
    sm = e / jnp.sum(e, axis=0, keepdims=True)
    acc = sm[0:1]
    o_ref[0:1, :] = acc
    for l in range(1, x.shape[0]):
        acc = acc + sm[l:l + 1]
        o_ref[l:l + 1, :] = acc


def hg_lower_bounds(hg_lb):
    return pl.pallas_call(_hg_lb_kernel, out_shape=jax.ShapeDtypeStruct(hg_lb.shape, F32), name="hg_lb")(hg_lb)


def _cumsum_rows(x, n):
    row = lax.broadcasted_iota(jnp.int32, x.shape, 1)
    d = 1
    while d < n:
        x = x + jnp.where(row >= d, pltpu.roll(x, d, axis=1), 0.0)
        d *= 2
    return x


def _hgrn_kernel(q_ref, f_ref, i_ref, g_ref, s0_ref, lb_ref, nw_ref, o_ref, sf_ref, st_scr, *, nb, seq, chunk):
    lb = lb_ref[...][None]
    nw = nw_ref[...][None]
    for b in range(nb):
        st_scr[b] = s0_ref[b, 0].T
    trow = lax.broadcasted_iota(jnp.int32, (chunk, chunk), 0)
    tcol = lax.broadcasted_iota(jnp.int32, (chunk, chunk), 1)
    causal = (tcol <= trow)[None]

    def body(n, carry):
        o = pl.multiple_of(n * chunk, chunk)
        q = q_ref[:, pl.ds(o, chunk), :]
        f = f_ref[:, pl.ds(o, chunk), :]
        v = i_ref[:, pl.ds(o, chunk), :]
        g = g_ref[:, pl.ds(o, chunk), :]
        fg = lb + (1.0 - lb) * jax.nn.sigmoid(f)
        qh = jax.nn.silu(q)
        kh = 1.0 - fg
        bcum = _cumsum_rows(jnp.log(fg), chunk)
        btot = bcum[:, chunk - 1:chunk, :]
        q_in = (qh * jnp.exp(bcum)).astype(BF16)
        k_in = (kh * jnp.exp(-bcum)).astype(BF16)
        k_end = (kh * jnp.exp(btot - bcum)).astype(BF16)
        decay = jnp.exp(btot)
        vb = v.astype(BF16)
        st = st_scr[...]
        att = jnp.einsum("btk,bsk->bts", q_in, k_in, preferred_element_type=F32)
        att = jnp.where(causal, att, 0.0).astype(BF16)
        out = (jnp.einsum("btk,bvk->btv", q_in, st.astype(BF16), preferred_element_type=F32)
               + jnp.einsum("bts,bsv->btv", att, vb, preferred_element_type=F32))
        st_scr[...] = st * decay + jnp.einsum("bsv,bsk->bvk", vb, k_end, preferred_element_type=F32)
        out = out * lax.rsqrt(jnp.mean(out * out, axis=-1, keepdims=True) + EPS) * nw
        o_ref[:, pl.ds(o, chunk), :] = (out * jax.nn.silu(g)).astype(o_ref.dtype)
        return carry

    n_chunks = seq // chunk
    lax.fori_loop(0, n_chunks, body, 0, unroll=3 if n_chunks % 3 == 0 else 1)
    for b in range(nb):
        sf_ref[b, 0] = st_scr[b].T


def hgrn2(z3, s0, lb, norm_w, nb):
    bsz, seq, _ = z3.shape
    chunk = min(HG_CHUNK, seq)
    assert seq % chunk == 0 and bsz % nb == 0

    def col(base):
        return pl.BlockSpec((nb, seq, LANES), lambda b, h, base=base: (b, 0, base + h))

    st_blk = pl.BlockSpec((nb, 1, HG_K, HG_V), lambda b, h: (b, h, 0, 0))
    vec = pl.BlockSpec((1, LANES), lambda b, h: (0, h))
    out, sf = pl.pallas_call(
        functools.partial(_hgrn_kernel, nb=nb, seq=seq, chunk=chunk),
        grid=(bsz // nb, HG_HEADS),
        in_specs=[col(8), col(16), col(24), col(32), st_blk, vec,
                  pl.BlockSpec((1, LANES), lambda b, h: (0, 0))],
        out_specs=[pl.BlockSpec((nb, seq, LANES), lambda b, h: (b, 0, h)), st_blk],
        out_shape=[jax.ShapeDtypeStruct((bsz, seq, HG_HEADS * HG_V), BF16),
                   jax.ShapeDtypeStruct((bsz, HG_HEADS, HG_K, HG_V), F32)],
        scratch_shapes=[pltpu.VMEM((nb, HG_V, HG_K), F32)],
        compiler_params=_params("parallel", "parallel"),
        name="hgrn2",
    )(z3, z3, z3, z3, s0, lb.reshape(1, HG_HEADS * HG_K), norm_w.reshape(1, HG_V))
    return out, sf


RW_MIXES = 6


def _norm_mix_kernel(h_ref, sh_ref, w_ref, mu_ref, *refs, tl):
    o_refs = refs[:RW_MIXES]
    last_ref, scr = refs[RW_MIXES:]
    x = h_ref[...]
    xn = x * lax.rsqrt(jnp.mean(x * x, axis=-1, keepdims=True) + EPS) * w_ref[...][None]

    @pl.when(pl.program_id(1) == 0)
    def _():
        scr[:, SUBLANES - 1:SUBLANES, :] = sh_ref[...]

    scr[:, SUBLANES:, :] = xn
    xx = scr[:, SUBLANES - 1:SUBLANES - 1 + tl, :] - xn
    for j, o_ref in enumerate(o_refs):
        o_ref[...] = (xn + xx * mu_ref[j:j + 1, :][None]).astype(o_ref.dtype)
    last = xn[:, tl - 1:tl, :]
    scr[:, SUBLANES - 1:SUBLANES, :] = last
    last_ref[...] = last


def norm_mix(h3, shift0, ln_w, mu, nb, tl):
    bsz, seq, d = h3.shape
    assert bsz % nb == 0 and seq % tl == 0 and tl % SUBLANES == 0
    blk = pl.BlockSpec((nb, tl, d), lambda b, t: (b, t, 0))
    row = pl.BlockSpec((nb, 1, d), lambda b, t: (b, 0, 0))
    outs = pl.pallas_call(
        functools.partial(_norm_mix_kernel, tl=tl),
        grid=(bsz // nb, seq // tl),
        in_specs=[blk, row, pl.BlockSpec((1, d), lambda b, t: (0, 0)), pl.BlockSpec((RW_MIXES, d), lambda b, t: (0, 0))],
        out_specs=[blk] * RW_MIXES + [row],
        out_shape=[jax.ShapeDtypeStruct((bsz, seq, d), BF16)] * RW_MIXES + [jax.ShapeDtypeStruct((bsz, 1, d), F32)],
        scratch_shapes=[pltpu.VMEM((nb, SUBLANES + tl, d), F32)],
        compiler_params=_params("parallel", "arbitrary"),
        name="norm_mix",
    )(h3, shift0.reshape(bsz, 1, d), ln_w.reshape(1, d), mu)
    return outs[:RW_MIXES], outs[RW_MIXES].reshape(bsz, d)


RW_PAIR = LANES // RW_HEAD
RW_SOLVE_BLOCK = 8


def _rwkv_kernel(r_ref, k_ref, v_ref, wl_ref, al_ref, g_ref, s0_ref,
                 w0_ref, a0_ref, kk_ref, ka_ref, rk_ref, lnw_ref, lnb_ref,
                 o_ref, sf_ref, s_scr, *, nb, hp, chunk):
    nu = hp * nb
    step = pl.program_id(2)
    lane = lax.broadcasted_iota(jnp.int32, (1, 1, LANES), 2)
    head1 = lane >= RW_HEAD

    def units(x):
        return jnp.concatenate([x[:, :, p * LANES:(p + 1) * LANES] for p in range(hp)], axis=0)

    def unit_rows(ref):
        return units(jnp.broadcast_to(ref[...][None], (nb, 1, hp * LANES)))

    w0, a0, k_k, k_a, r_k, ln_w, ln_b = (unit_rows(p) for p in
                                         (w0_ref, a0_ref, kk_ref, ka_ref, rk_ref, lnw_ref, lnb_ref))
    sq_row = lax.broadcasted_iota(jnp.int32, (LANES, LANES), 0) >= RW_HEAD
    sq_col = lax.broadcasted_iota(jnp.int32, (LANES, LANES), 1) >= RW_HEAD
    same_head = sq_row == sq_col
    ones_bd = same_head.astype(BF16)

    @pl.when(step == 0)
    def _():
        zero = jnp.zeros((nb, RW_HEAD, RW_HEAD), F32)
        for p in range(hp):
            top = jnp.concatenate([s0_ref[:, RW_PAIR * p], zero], axis=-1)
            bot = jnp.concatenate([zero, s0_ref[:, RW_PAIR * p + 1]], axis=-1)
            s_scr[p * nb:(p + 1) * nb] = jnp.concatenate([top, bot], axis=1)

    def bdot(spec, a, b):
        return jnp.einsum(spec, a.astype(BF16), b.astype(BF16), preferred_element_type=F32)

    def head_sum(x):
        x2 = x.reshape(nu * chunk, LANES)
        hi = x2.astype(BF16)
        lo = (x2 - hi.astype(F32)).astype(BF16)
        s = jnp.dot(hi, ones_bd, preferred_element_type=F32) + jnp.dot(lo, ones_bd, preferred_element_type=F32)
        return s.reshape(nu, chunk, LANES)

    def stack_heads(x):
        return jnp.concatenate([jnp.where(head1, 0.0, x), jnp.where(head1, x, 0.0)], axis=1).astype(BF16)

    r, k, v, wl, al, g = (units(ref[...]) for ref in (r_ref, k_ref, v_ref, wl_ref, al_ref, g_ref))
    lw = -jnp.exp(-jax.nn.softplus(-(w0 + wl)) - 0.5)
    ag = jax.nn.sigmoid(a0 + al)
    kk = k * k_k
    kk = kk * (1.0 / jnp.maximum(jnp.sqrt(head_sum(kk * kk)), 1e-12))
    k2 = k * (1.0 + (ag - 1.0) * k_a)
    cl = _cumsum_rows(lw, chunk)
    e_pos = jnp.exp(cl)
    e_neg = jnp.exp(-cl)
    at = (-kk) * jnp.exp(cl - lw)
    bt = (kk * ag) * e_neg
    kt = k2 * e_neg
    rt = r * e_pos
    wc = e_pos[:, chunk - 1:chunk, :]
    trow = lax.broadcasted_iota(jnp.int32, (chunk, RW_PAIR * chunk), 0)
    tcol = lax.broadcasted_iota(jnp.int32, (chunk, RW_PAIR * chunk), 1)
    tcol = jnp.where(tcol >= chunk, tcol - chunk, tcol)
    strict = (tcol < trow)[None]
    incl = (tcol <= trow)[None]
    x2 = jnp.concatenate([at, rt], axis=1)
    pb = bdot("utc,usc->uts", x2, stack_heads(bt))
    pk = bdot("utc,usc->uts", x2, stack_heads(kt))
    lab = jnp.where(strict, pb[:, :chunk], 0.0)
    lak = jnp.where(strict, pk[:, :chunk], 0.0)
    arb = jnp.where(incl, pb[:, chunk:], 0.0)
    ark = jnp.where(incl, pk[:, chunk:], 0.0)
    v_bd = stack_heads(v)
    xa = at
    xv = bdot("uts,usc->utc", lak, v_bd)
    sub = min(RW_SOLVE_BLOCK, chunk)
    done_a, done_v = [], []
    for lo in range(0, chunk, sub):
        xa_i = xa[:, lo:lo + sub, :]
        xv_i = xv[:, lo:lo + sub, :]
        if lo:
            pad = jnp.zeros((nu, chunk - lo, LANES), F32)
            prev = jnp.concatenate([stack_heads(jnp.concatenate(done_a + [pad], axis=1)),
                                    stack_heads(jnp.concatenate(done_v + [pad], axis=1))], axis=-1)
            upd = bdot("uts,usc->utc", lab[:, lo:lo + sub, :], prev)
            xa_i = xa_i + upd[:, :, :LANES]
            xv_i = xv_i + upd[:, :, LANES:]
        l0 = lab[:, lo:lo + sub, lo:lo + sub]
        l1 = lab[:, lo:lo + sub, chunk + lo:chunk + lo + sub]
        for s in range(sub - 1):
            m = jnp.where(head1, l1[:, :, s:s + 1], l0[:, :, s:s + 1])
            xa_i = xa_i + m * xa_i[:, s:s + 1, :]
            xv_i = xv_i + m * xv_i[:, s:s + 1, :]
        done_a.append(xa_i)
        done_v.append(xv_i)
    ah = jnp.concatenate(done_a, axis=1)
    vh = jnp.concatenate(done_v, axis=1)
    both = bdot("uts,usc->utc", arb, jnp.concatenate([stack_heads(ah), stack_heads(vh)], axis=-1))
    rh = rt + both[:, :, :LANES]
    yh = both[:, :, LANES:] + bdot("uts,usc->utc", ark, v_bd)
    gp = jnp.where(same_head, bdot("utj,utk->ujk", ah, bt), 0.0)
    ht = jnp.where(same_head, bdot("utv,utk->uvk", jnp.concatenate([vh, v], axis=1),
                                   jnp.concatenate([bt, kt], axis=1)), 0.0)
    st = s_scr[...]
    y = bdot("utk,uvk->utv", rh, st) + yh
    st = (st + bdot("uvj,ujk->uvk", st, gp) + ht) * wc
    s_scr[...] = st
    inv_n = 1.0 / RW_HEAD
    yc = y - head_sum(y) * inv_n
    var = head_sum(yc * yc) * inv_n
    y = yc * lax.rsqrt(var + RW_GN_EPS) * ln_w + ln_b
    y = y + head_sum(r * k2 * r_k) * v
    out = (y * g).astype(o_ref.dtype)
    for p in range(hp):
        o_ref[:, :, p * LANES:(p + 1) * LANES] = out[p * nb:(p + 1) * nb]

    @pl.when(step == pl.num_programs(2) - 1)
    def _():
        for p in range(hp):
            sf_ref[:, RW_PAIR * p] = st[p * nb:(p + 1) * nb, :RW_HEAD, :RW_HEAD]
            sf_ref[:, RW_PAIR * p + 1] = st[p * nb:(p + 1) * nb, RW_HEAD:, RW_HEAD:]


def _rwkv_chunk(seq):
    for c in (48, 32, 16, 8):
        if seq % c == 0:
            return c
    raise ValueError(seq)


def rwkv7(r, k, v, wl, al, g, s0, w0, a0, k_k, k_a, r_k, ln_w, ln_b, nb, hp):
    bsz, seq, d = r.shape
    chunk = _rwkv_chunk(seq)
    heads = hp * RW_PAIR
    assert bsz % nb == 0 and RW_HEADS % heads == 0
    seq_blk = pl.BlockSpec((nb, chunk, hp * LANES), lambda h, b, t: (b, t, h))
    st_blk = pl.BlockSpec((nb, heads, RW_HEAD, RW_HEAD), lambda h, b, t: (b, h, 0, 0))
    vec = pl.BlockSpec((1, hp * LANES), lambda h, b, t: (0, h))
    out, sf = pl.pallas_call(
        functools.partial(_rwkv_kernel, nb=nb, hp=hp, chunk=chunk),
        grid=(RW_HEADS // heads, bsz // nb, seq // chunk),
        in_specs=[seq_blk] * 6 + [st_blk] + [vec] * 7,
        out_specs=[seq_blk, st_blk],
        out_shape=[jax.ShapeDtypeStruct((bsz, seq, d), BF16),
                   jax.ShapeDtypeStruct((bsz, RW_HEADS, RW_HEAD, RW_HEAD), F32)],
        scratch_shapes=[pltpu.VMEM((hp * nb, LANES, LANES), F32)],
        compiler_params=_params("parallel", "parallel", "arbitrary"),
        name="rwkv7",
    )(r, k, v, wl, al, g, s0, *(p.reshape(1, d) for p in (w0, a0, k_k, k_a, r_k, ln_w, ln_b)))
    return out, sf


def _ffn_in_kernel(x_ref, wa_ref, wv_ref, e_ref, cw_ref, cb_ref, o_ref, st_ref, scr, *, nb, seq, sb, sr):
    tn = wa_ref.shape[1]
    cw = cw_ref[...]
    cb = cb_ref[...][None]
    scr[:, 0:SUBLANES, :] = e_ref[...]
    for b0 in range(0, nb, sb):
        for r0 in range(0, seq, sr):
            lo = b0 * seq + r0
            x = x_ref[lo:lo + sb * sr, :]
            a = jnp.dot(x, wa_ref[...], preferred_element_type=F32).reshape(sb, sr, tn)
            v = jnp.dot(x, wv_ref[...], preferred_element_type=F32).reshape(sb, sr, tn)
            scr[b0:b0 + sb, SUBLANES + r0:SUBLANES + r0 + sr, :] = a
            c = cb + cw[CONV_W - 1:CONV_W][None] * a
            for j in range(CONV_W - 1):
                first = SUBLANES + r0 - (CONV_W - 1 - j)
                c = c + cw[j:j + 1][None] * scr[b0:b0 + sb, first:first + sr, :]
            o_ref[lo:lo + sb * sr, :] = (jax.nn.gelu(c) * v).reshape(sb * sr, tn).astype(o_ref.dtype)
    st_ref[...] = scr[:, SUBLANES + seq - (CONV_W - 1):SUBLANES + seq, :]


def ffn_in(xb, conv0, w_in, layer, conv_w, conv_b, bsz, seq, nb, tn, sub):
    t, d = xb.shape
    sb, sr = sub
    assert t == bsz * seq and bsz % nb == 0 and D_FF % tn == 0
    assert nb % sb == 0 and seq % sr == 0 and sr % SUBLANES == 0 and (sr == seq or nb == sb == 1)
    halo = jnp.pad(conv0, ((0, 0), (SUBLANES - (CONV_W - 1), 0), (0, 0)))
    col = lambda i, j: (0, j)
    n_col = D_FF // tn
    out, st = pl.pallas_call(
        functools.partial(_ffn_in_kernel, nb=nb, seq=seq, sb=sb, sr=sr),
        grid=(bsz // nb, n_col),
        in_specs=[pl.BlockSpec((nb * seq, d), lambda i, j: (i, 0)),
                  pl.BlockSpec((None, d, tn), lambda i, j: (layer, 0, j)),
                  pl.BlockSpec((None, d, tn), lambda i, j: (layer, 0, j + n_col)),
                  pl.BlockSpec((nb, SUBLANES, tn), lambda i, j: (i, 0, j)),
                  pl.BlockSpec((CONV_W, tn), col), pl.BlockSpec((1, tn), col)],
        out_specs=[pl.BlockSpec((nb * seq, tn), lambda i, j: (i, j)),
                   pl.BlockSpec((nb, CONV_W - 1, tn), lambda i, j: (i, 0, j))],
        out_shape=[jax.ShapeDtypeStruct((t, D_FF), BF16),
                   jax.ShapeDtypeStruct((bsz, CONV_W - 1, D_FF), F32)],
        scratch_shapes=[pltpu.VMEM((nb, SUBLANES + seq, tn), F32)],
        compiler_params=_params("parallel", "parallel"),
        name="ffn_in",
    )(xb, w_in, w_in, halo, conv_w, conv_b.reshape(1, D_FF))
    return out, st


def _channel_mixer(h, conv0, layer, p, cfg, bsz, seq):
    (xb,) = rmsnorm(h, p["ln_ffn"][layer], (BF16,))
    gated, n_cv = ffn_in(xb, conv0, p["ffn_w_in"], layer, p["ffn_conv_w"][layer], p["ffn_conv_b"][layer],
                         bsz, seq, cfg["ffn_nb"], cfg["ffn_tn"], cfg["ffn_sub"])
    return matmul([gated], [(p["ffn_w_down"], layer, 0)], F32, epilogue="residual", extra=h), n_cv


def _trunk(h3, s5r, s5i, hg, rw, sh, cv, p, cfg):
    bsz, seq, d = h3.shape
    t = bsz * seq
    h = h3.reshape(t, d)

    (xb,) = rmsnorm(h, p["ln_mix"][0], (BF16,))
    z = matmul([xb], [p["ev_w_in"]], F32).reshape(bsz, seq, EVEN_IN)
    ys5, n_s5r, n_s5i = s5_scan(z, s5r[0], s5i[0], *p["s5"], nb=cfg["s5_nb"])
    ys5 = ys5.reshape(t, S5_WIDTH)
    ya = matmul([ys5.astype(BF16)], [p["s5_w_glu"]], BF16, epilogue="glu", extra=ys5)
    yb, n_hg = hgrn2(z, hg[0], p["hg_lb"], p["hg_norm_w"], nb=cfg["hg_nb"])
    h = matmul([ya, yb.reshape(t, -1)], [(p["ev_w_out"], 0, 0), (p["ev_w_out"], 0, 1)], F32,
               epilogue="residual", extra=h)
    h, n_cv0 = _channel_mixer(h, cv[0], 0, p, cfg, bsz, seq)

    mixes, n_sh = norm_mix(h.reshape(bsz, seq, d), sh[0], p["ln_mix"][1], p["rw_mu"], *cfg["mix_blk"])
    xr, xw, xk, xv, xa, xg = (m.reshape(t, d) for m in mixes)
    r = matmul([xr], [p["rw_w_r"]], F32)
    k = matmul([xk], [p["rw_w_k"]], F32)
    v = matmul([xv], [p["rw_w_v"]], F32)
    wl = matmul([matmul([xw], [p["rw_w1"]], BF16, act="tanh")], [p["rw_w2"]], F32)
    al = matmul([matmul([xa], [p["rw_a1"]], BF16)], [p["rw_a2"]], F32)
    g = matmul([matmul([xg], [p["rw_g1"]], BF16, act="sigmoid")], [p["rw_g2"]], F32)
    as3 = lambda a: a.reshape(bsz, seq, d)
    yo, n_rw = rwkv7(as3(r), as3(k), as3(v), as3(wl), as3(al), as3(g), rw[0], *p["rw_vec"],
                     nb=cfg["rw_nb"], hp=cfg["rw_hp"])
    h = matmul([yo.reshape(t, d)], [p["rw_w_o"]], F32, epilogue="residual", extra=h)
    h, n_cv1 = _channel_mixer(h, cv[1], 1, p, cfg, bsz, seq)

    (y,) = rmsnorm(h, p["ln_final"], (F32,))
    return (y.reshape(bsz, seq, d), n_s5r[None], n_s5i[None], n_hg[None], n_rw[None], n_sh[None],
            jnp.stack([n_cv0, n_cv1]))


PROMPT_CFG = dict(s5_nb=1, hg_nb=4, mix_blk=(1, 344), rw_nb=4, rw_hp=8, ffn_nb=1, ffn_tn=512, ffn_sub=(1, 688))
SAMPLE_CFG = dict(s5_nb=32, hg_nb=16, mix_blk=(32, 8), rw_nb=16, rw_hp=2, ffn_nb=128, ffn_tn=512, ffn_sub=(32, 8))


def kernel(x_prompt, x_sample, state_s5_re, state_s5_im, state_hgrn, state_rwkv, state_shift, state_conv, meta_tokens, ln_mix, ln_ffn, ln_final, ev_w_in, ev_w_out, s5_lam_re, s5_lam_im, s5_log_step, s5_b_re, s5_b_im, s5_c_re, s5_c_im, s5_d, s5_w_glu, hg_lb, hg_norm_w, rw_mu, rw_w0, rw_w1, rw_w2, rw_a0, rw_a1, rw_a2, rw_g1, rw_g2, rw_k_k, rw_k_a, rw_r_k, rw_w_r, rw_w_k, rw_w_v, rw_w_o, rw_ln_w, rw_ln_b, ffn_w_in, ffn_conv_w, ffn_conv_b, ffn_w_down):
    bf = lambda w: w.astype(BF16)
    lb_all = hg_lower_bounds(hg_lb)
    pwr, pwi, bbr_t, bbi_t = s5_prep(s5_lam_re[0], s5_lam_im[0], s5_log_step[0], s5_b_re[0], s5_b_im[0])
    wbr, wbi, wcr, wci = _s5_block_weights(bbr_t, bbi_t, s5_c_re[0], s5_c_im[0])
    p = {
        "ln_mix": ln_mix, "ln_ffn": ln_ffn, "ln_final": ln_final,
        "ev_w_in": bf(ev_w_in[0]),
        "ev_w_out": bf(ev_w_out),
        "s5": (pwr, pwi, wbr, wbi, wcr, wci, s5_d[0].reshape(1, S5_WIDTH)),
        "s5_w_glu": bf(s5_w_glu[0]),
        "hg_lb": lb_all[0], "hg_norm_w": hg_norm_w[0],
        "rw_mu": rw_mu[0],
        "rw_w1": bf(rw_w1[0]), "rw_w2": bf(rw_w2[0]), "rw_a1": bf(rw_a1[0]), "rw_a2": bf(rw_a2[0]),
        "rw_g1": bf(rw_g1[0]), "rw_g2": bf(rw_g2[0]),
        "rw_w_r": bf(rw_w_r[0]), "rw_w_k": bf(rw_w_k[0]), "rw_w_v": bf(rw_w_v[0]), "rw_w_o": bf(rw_w_o[0]),
        "rw_vec": (rw_w0[0], rw_a0[0], rw_k_k[0], rw_k_a[0], rw_r_k[0].reshape(D_MODEL), rw_ln_w[0], rw_ln_b[0]),
        "ffn_w_in": bf(ffn_w_in), "ffn_conv_w": ffn_conv_w, "ffn_conv_b": ffn_conv_b,
        "ffn_w_down": bf(ffn_w_down),
    }

    bsz = x_prompt.shape[0]
    hp = jnp.concatenate([jnp.broadcast_to(meta_tokens[None], (bsz, N_META, D_MODEL)), x_prompt], axis=1)
    zeros = lambda *s: jnp.zeros(s, F32)
    outs_p = _trunk(hp,
                    zeros(1, bsz, S5_GROUPS, S5_STATE), zeros(1, bsz, S5_GROUPS, S5_STATE),
                    zeros(1, bsz, HG_HEADS, HG_K, HG_V), zeros(1, bsz, RW_HEADS, RW_HEAD, RW_HEAD),
                    zeros(1, bsz, D_MODEL), zeros(2, bsz, CONV_W - 1, D_FF), p, PROMPT_CFG)
    outs_s = _trunk(x_sample, state_s5_re, state_s5_im, state_hgrn, state_rwkv, state_shift, state_conv,
                    p, SAMPLE_CFG)
    return (outs_p[0][:, N_META:], outs_s[0]) + tuple(outs_p[1:]) + tuple(outs_s[1:])
```

```python
import functools

import jax
import jax.numpy as jnp
from jax import lax
from jax.experimental import pallas as pl
from jax.experimental.pallas import tpu as pltpu

F32 = jnp.float32
BF16 = jnp.bfloat16

D_MODEL = 2048
N_META = 16
EPS = 1e-6
S5_WIDTH = 1024
S5_GROUP = 16
S5_GROUPS = 64
S5_STATE = 64
S5_CH = S5_GROUPS * S5_STATE
HG_HEADS = 8
HG_K = 128
HG_V = 128
HG_CHUNK = 16
EVEN_IN = 5120
RW_HEAD = 64
RW_HEADS = 32
RW_GN_EPS = 64e-5
D_FF = 5632
CONV_W = 3

LANES = 128
SUBLANES = 8
VMEM_LIMIT = 56 * 1024 * 1024


def _params(*sem):
    return pltpu.CompilerParams(dimension_semantics=sem, vmem_limit_bytes=VMEM_LIMIT)


def _row_tile(t, cap=1024):
    best = None
    for d in range(16, min(t, cap) + 1, 16):
        if t % d == 0:
            best = d
    assert best is not None, t
    return best


def _rms_kernel(x_ref, w_ref, *o_refs):
    x = x_ref[...]
    y = x * lax.rsqrt(jnp.mean(x * x, axis=-1, keepdims=True) + EPS) * w_ref[...]
    for o_ref in o_refs:
        o_ref[...] = y.astype(o_ref.dtype)


def rmsnorm(x, w, dtypes, rows=None):
    t, d = x.shape
    if rows is None:
        tm = _row_tile(t)
        n_out = t
        grid = (t // tm,)
        in_spec = pl.BlockSpec((tm, d), lambda i: (i, 0))
        out_spec = in_spec
        vec = pl.BlockSpec((1, d), lambda i: (0, 0))
    else:
        bsz, seq, front = rows
        keep = seq - front
        assert t == bsz * seq and front % 16 == 0
        tm = _row_tile(keep)
        per = keep // tm
        n_out = bsz * keep
        grid = (bsz, per)
        in_spec = pl.BlockSpec((pl.Element(tm), pl.Element(d)),
                               lambda b, i: (pl.multiple_of(b * seq + front + i * tm, 16), 0))
        out_spec = pl.BlockSpec((tm, d), lambda b, i: (b * per + i, 0))
        vec = pl.BlockSpec((1, d), lambda b, i: (0, 0))
    outs = pl.pallas_call(
        _rms_kernel,
        grid=grid,
        in_specs=[in_spec, vec],
        out_specs=[out_spec for _ in dtypes],
        out_shape=[jax.ShapeDtypeStruct((n_out, d), dt) for dt in dtypes],
        compiler_params=_params(*(["parallel"] * len(grid))),
        name="rmsnorm",
    )(x, w.reshape(1, d))
    return outs


def _embed_norm_kernel(x_ref, m_ref, w_ref, h_ref, xb_ref, *, front):
    def emit(rows):
        h_ref[...] = rows
        y = rows * lax.rsqrt(jnp.mean(rows * rows, axis=-1, keepdims=True) + EPS) * w_ref[...]
        xb_ref[...] = y.astype(xb_ref.dtype)

    @pl.when(pl.program_id(1) == 0)
    def _():
        emit(jnp.concatenate([m_ref[...], x_ref[:x_ref.shape[0] - front, :]], axis=0))

    @pl.when(pl.program_id(1) > 0)
    def _():
        emit(x_ref[...])


def embed_norm(x3, meta, w):
    bsz, seq, d = x3.shape
    front = meta.shape[0]
    total = seq + front
    tm = _row_tile(total)
    per = total // tm
    assert front % 16 == 0 and tm > front
    blk = pl.BlockSpec((tm, d), lambda b, j: (b * per + j, 0))
    return pl.pallas_call(
        functools.partial(_embed_norm_kernel, front=front),
        grid=(bsz, per),
        in_specs=[pl.BlockSpec((pl.Element(tm), pl.Element(d)),
                               lambda b, j: (pl.multiple_of(b * seq + jnp.maximum(j * tm - front, 0), 16), 0)),
                  pl.BlockSpec((front, d), lambda b, j: (0, 0)),
                  pl.BlockSpec((1, d), lambda b, j: (0, 0))],
        out_specs=[blk, blk],
        out_shape=[jax.ShapeDtypeStruct((bsz * total, d), F32), jax.ShapeDtypeStruct((bsz * total, d), BF16)],
        compiler_params=_params("parallel", "parallel"),
        name="embed_norm",
    )(x3.reshape(bsz * seq, d), meta, w.reshape(1, d))


def _act(x, act):
    if act == "tanh":
        return jnp.tanh(x)
    if act == "sigmoid":
        return jax.nn.sigmoid(x)
    assert act is None
    return x


def _mm_kernel(*refs, n_a, act, epilogue):
    a_refs = refs[:n_a]
    w_refs = refs[n_a:2 * n_a]
    rest = refs[2 * n_a:]
    o_ref = rest[-1]
    acc = jnp.dot(a_refs[0][...], w_refs[0][...], preferred_element_type=F32)
    for a_ref, w_ref in zip(a_refs[1:], w_refs[1:]):
        acc = acc + jnp.dot(a_ref[...], w_ref[...], preferred_element_type=F32)
    acc = _act(acc, act)
    if epilogue == "residual":
        acc = rest[0][...] + acc
    elif epilogue == "glu":
        acc = rest[0][...] * jax.nn.sigmoid(acc)
    o_ref[...] = acc.astype(o_ref.dtype)


MM_VMEM_BUDGET = 40 * 1024 * 1024
MXU_WIDTH = 256


def _mm_tiles(t, k_total, n, out_bytes, has_extra):
    rows = [d for d in range(16, t + 1, 16) if t % d == 0]
    cols = [d for d in range(LANES, n + 1, LANES) if n % d == 0] or [n]
    best, best_score = None, -1.0
    for tm in rows:
        for tn in cols:
            est = 4 * tm * k_total + 4 * k_total * tn + tm * tn * (2 * out_bytes + 4 + (8 if has_extra else 0))
            if est > MM_VMEM_BUDGET:
                continue
            score = tm * tn * (1.0 if tn % MXU_WIDTH == 0 else 0.8)
            if score > best_score:
                best, best_score = (tm, tn), score
    assert best is not None, (t, k_total, n)
    return best


def matmul(a_list, w_list, out_dtype, act=None, epilogue=None, extra=None):
    t = a_list[0].shape[0]
    n = w_list[0][0].shape[2] if isinstance(w_list[0], tuple) else w_list[0].shape[1]
    tm, tn = _mm_tiles(t, sum(a.shape[1] for a in a_list), n, jnp.dtype(out_dtype).itemsize, epilogue is not None)
    in_specs = [pl.BlockSpec((tm, a.shape[1]), lambda i, j: (i, 0)) for a in a_list]
    args = list(a_list)
    for a, w in zip(a_list, w_list):
        if isinstance(w, tuple):
            w, layer, kblk = w
            in_specs.append(pl.BlockSpec((None, a.shape[1], tn), lambda i, j, layer=layer, kblk=kblk: (layer, kblk, j)))
        else:
            in_specs.append(pl.BlockSpec((w.shape[0], tn), lambda i, j: (0, j)))
        args.append(w)
    if epilogue is not None:
        in_specs.append(pl.BlockSpec((tm, tn), lambda i, j: (i, j)))
        args.append(extra)
    return pl.pallas_call(
        functools.partial(_mm_kernel, n_a=len(a_list), act=act, epilogue=epilogue),
        grid=(t // tm, n // tn),
        in_specs=in_specs,
        out_specs=pl.BlockSpec((tm, tn), lambda i, j: (i, j)),
        out_shape=jax.ShapeDtypeStruct((t, n), out_dtype),
        compiler_params=_params("parallel", "parallel"),
        name="matmul",
    )(*args)


def _s5_prep_kernel(lr_ref, li_ref, ls_ref, brt_ref, bit_ref, pwr_ref, pwi_ref, bbr_ref, bbi_ref):
    lr = jnp.minimum(lr_ref[...], -1e-4)
    li = li_ref[...]
    dt = jnp.exp(ls_ref[...])
    n = lax.broadcasted_iota(jnp.int32, (SUBLANES, S5_CH), 0).astype(F32) + 1.0
    mag = jnp.exp(n * (lr * dt))
    ang = n * (li * dt)
    pwr = mag * jnp.cos(ang)
    pwi = mag * jnp.sin(ang)
    pwr_ref[...] = pwr
    pwi_ref[...] = pwi
    ar = pwr[0:1]
    ai = pwi[0:1]
    den = lr * lr + li * li
    zr = ((ar - 1.0) * lr + ai * li) / den
    zi = (ai * lr - (ar - 1.0) * li) / den
    br = brt_ref[...]
    bi = bit_ref[...]
    bbr_ref[...] = zr * br - zi * bi
    bbi_ref[...] = zr * bi + zi * br


def s5_prep(lam_re, lam_im, log_step, b_re, b_im):
    lr = lam_re.reshape(1, S5_CH)
    li = lam_im.reshape(1, S5_CH)
    ls = jnp.broadcast_to(log_step[:, None], (S5_GROUPS, S5_STATE)).reshape(1, S5_CH)
    brt = b_re.reshape(S5_CH, S5_GROUP).T
    bit = b_im.reshape(S5_CH, S5_GROUP).T
    return pl.pallas_call(
        _s5_prep_kernel,
        out_shape=[jax.ShapeDtypeStruct((SUBLANES, S5_CH), F32)] * 2
        + [jax.ShapeDtypeStruct((S5_GROUP, S5_CH), F32)] * 2,
        name="s5_prep",
    )(lr, li, ls, brt, bit)


S5_BLK_GROUPS = LANES // S5_GROUP
S5_BLKS = S5_WIDTH // LANES
S5_BLK_CH = S5_BLK_GROUPS * S5_STATE


def _cmul_add(xr, xi, mr, mi, sr, si):
    return xr + mr * sr - mi * si, xi + mr * si + mi * sr


def _s5_kernel(u_ref, h0r_ref, h0i_ref, pwr_ref, pwi_ref, wbr_ref, wbi_ref, wcr_ref, wci_ref, d_ref,
               y_ref, hr_ref, hi_ref, xr_scr, xi_scr, *, nb, seq):
    u2 = u_ref[...].reshape(nb * seq, LANES)
    ub = u2.astype(BF16)
    xr_scr[...] = jnp.dot(ub, wbr_ref[0], preferred_element_type=F32).reshape(nb, seq, S5_BLK_CH)
    xi_scr[...] = jnp.dot(ub, wbi_ref[0], preferred_element_type=F32).reshape(nb, seq, S5_BLK_CH)

    pwr = pwr_ref[...]
    pwi = pwi_ref[...]
    row = lax.broadcasted_iota(jnp.int32, (SUBLANES, S5_BLK_CH), 0)
    steps = []
    for d in (1, 2, 4):
        keep = row >= d
        steps.append((d, jnp.where(keep, pwr[d - 1:d], 0.0)[None], jnp.where(keep, pwi[d - 1:d], 0.0)[None]))
    pr = pwr[None]
    pi = pwi[None]

    def tile(i, carry):
        cr, ci = carry
        o = pl.multiple_of(i * SUBLANES, SUBLANES)
        xr = xr_scr[:, pl.ds(o, SUBLANES), :]
        xi = xi_scr[:, pl.ds(o, SUBLANES), :]
        for d, mr, mi in steps:
            sr = pltpu.roll(xr, d, axis=1)
            si = pltpu.roll(xi, d, axis=1)
            xr, xi = _cmul_add(xr, xi, mr, mi, sr, si)
        xr, xi = _cmul_add(xr, xi, pr, pi, cr, ci)
        xr_scr[:, pl.ds(o, SUBLANES), :] = xr
        xi_scr[:, pl.ds(o, SUBLANES), :] = xi
        return xr[:, SUBLANES - 1:SUBLANES, :], xi[:, SUBLANES - 1:SUBLANES, :]

    hr, hi = lax.fori_loop(0, seq // SUBLANES, tile, (h0r_ref[...], h0i_ref[...]))
    hr_ref[...] = hr
    hi_ref[...] = hi

    xr = xr_scr[...].reshape(nb * seq, S5_BLK_CH).astype(BF16)
    xi = xi_scr[...].reshape(nb * seq, S5_BLK_CH).astype(BF16)
    y = (jnp.dot(xr, wcr_ref[0], preferred_element_type=F32)
         - jnp.dot(xi, wci_ref[0], preferred_element_type=F32)
         + d_ref[...] * u2)
    y_ref[...] = jax.nn.gelu(y).reshape(nb, seq, LANES)


def s5_scan(z3, h0r, h0i, pwr, pwi, wbr, wbi, wcr, wci, d, nb):
    bsz, seq, _ = z3.shape
    assert seq % SUBLANES == 0 and bsz % nb == 0
    seq_blk = pl.BlockSpec((nb, seq, LANES), lambda b, k: (b, 0, k))
    st_blk = pl.BlockSpec((nb, 1, S5_BLK_CH), lambda b, k: (b, 0, k))
    pw_blk = pl.BlockSpec((SUBLANES, S5_BLK_CH), lambda b, k: (0, k))
    wb_blk = pl.BlockSpec((1, LANES, S5_BLK_CH), lambda b, k: (k, 0, 0))
    wc_blk = pl.BlockSpec((1, S5_BLK_CH, LANES), lambda b, k: (k, 0, 0))
    y, hr, hi = pl.pallas_call(
        functools.partial(_s5_kernel, nb=nb, seq=seq),
        grid=(bsz // nb, S5_BLKS),
        in_specs=[seq_blk, st_blk, st_blk, pw_blk, pw_blk, wb_blk, wb_blk, wc_blk, wc_blk,
                  pl.BlockSpec((1, LANES), lambda b, k: (0, k))],
        out_specs=[seq_blk, st_blk, st_blk],
        out_shape=[jax.ShapeDtypeStruct((bsz, seq, S5_WIDTH), F32),
                   jax.ShapeDtypeStruct((bsz, 1, S5_CH), F32),
                   jax.ShapeDtypeStruct((bsz, 1, S5_CH), F32)],
        scratch_shapes=[pltpu.VMEM((nb, seq, S5_BLK_CH), F32), pltpu.VMEM((nb, seq, S5_BLK_CH), F32)],
        compiler_params=_params("parallel", "parallel"),
        name="s5_scan",
    )(z3, h0r.reshape(bsz, 1, S5_CH), h0i.reshape(bsz, 1, S5_CH), pwr, pwi, wbr, wbi, wcr, wci, d)
    return y, hr.reshape(bsz, S5_GROUPS, S5_STATE), hi.reshape(bsz, S5_GROUPS, S5_STATE)


def _s5_block_weights(bbr_t, bbi_t, c_re, c_im):
    eye = jnp.eye(S5_BLK_GROUPS, dtype=F32)

    def wb(bt):
        b4 = bt.reshape(S5_GROUP, S5_BLKS, S5_BLK_GROUPS, S5_STATE)
        w = jnp.einsum("cbgp,hg->bhcgp", b4, eye)
        return w.reshape(S5_BLKS, LANES, S5_BLK_CH).astype(BF16)

    def wc(c):
        c4 = c.reshape(S5_BLKS, S5_BLK_GROUPS, S5_GROUP, S5_STATE)
        w = jnp.einsum("bgcp,hg->bhpgc", c4, eye)
        return w.reshape(S5_BLKS, S5_BLK_CH, LANES).astype(BF16)

    return wb(bbr_t), wb(bbi_t), wc(c_re), wc(c_im)


def _hg_lb_kernel(x_ref, o_ref):
    x = x_ref[...]
    e = jnp.exp(x - jnp.max(x, axis=0, keepdims=True))
    sm = e / jnp.sum(e, axis=0, keepdims=True)
    acc = sm[0:1]
    o_ref[0:1, :] = acc
    for l in range(1, x.shape[0]):
        acc = acc + sm[l:l + 1]
        o_ref[l:l + 1, :] = acc


def hg_lower_bounds(hg_lb):
    return pl.pallas_call(_hg_lb_kernel, out_shape=jax.ShapeDtypeStruct(hg_lb.shape, F32), name="hg_lb")(hg_lb)


def _cumsum_rows(x, n):
    row = lax.broadcasted_iota(jnp.int32, x.shape, 1)
    d = 1
    while d < n:
        x = x + jnp.where(row >= d, pltpu.roll(x, d, axis=1), 0.0)
        d *= 2
    return x


def _hgrn_kernel(q_ref, f_ref, i_ref, g_ref, s0_ref, lb_ref, nw_ref, o_ref, sf_ref, st_scr, *, nb, hh, chunk):
    step = pl.program_id(2)

    def units(x):
        return jnp.concatenate([x[:, :, h * LANES:(h + 1) * LANES] for h in range(hh)], axis=0)

    lb = units(jnp.broadcast_to(lb_ref[...][None], (nb, 1, hh * LANES)))
    nw = nw_ref[...][None]

    @pl.when(step == 0)
    def _():
        for h in range(hh):
            for b in range(nb):
                st_scr[h * nb + b] = s0_ref[b, h].T

    trow = lax.broadcasted_iota(jnp.int32, (chunk, chunk), 0)
    tcol = lax.broadcasted_iota(jnp.int32, (chunk, chunk), 1)
    causal = (tcol <= trow)[None]
    q, f, v, g = (units(ref[...]) for ref in (q_ref, f_ref, i_ref, g_ref))
    fg = lb + (1.0 - lb) * jax.nn.sigmoid(f)
    qh = jax.nn.silu(q)
    kh = 1.0 - fg
    bcum = _cumsum_rows(jnp.log(fg), chunk)
    btot = bcum[:, chunk - 1:chunk, :]
    q_in = (qh * jnp.exp(bcum)).astype(BF16)
    k_in = (kh * jnp.exp(-bcum)).astype(BF16)
    k_end = (kh * jnp.exp(btot - bcum)).astype(BF16)
    decay = jnp.exp(btot)
    vb = v.astype(BF16)
    st = st_scr[...]
    att = jnp.einsum("utk,usk->uts", q_in, k_in, preferred_element_type=F32)
    att = jnp.where(causal, att, 0.0).astype(BF16)
    out = (jnp.einsum("utk,uvk->utv", q_in, st.astype(BF16), preferred_element_type=F32)
           + jnp.einsum("uts,usv->utv", att, vb, preferred_element_type=F32))
    st = st * decay + jnp.einsum("usv,usk->uvk", vb, k_end, preferred_element_type=F32)
    st_scr[...] = st
    out = out * lax.rsqrt(jnp.mean(out * out, axis=-1, keepdims=True) + EPS) * nw
    out = (out * jax.nn.silu(g)).astype(o_ref.dtype)
    for h in range(hh):
        o_ref[:, :, h * LANES:(h + 1) * LANES] = out[h * nb:(h + 1) * nb]

    @pl.when(step == pl.num_programs(2) - 1)
    def _():
        for h in range(hh):
            for b in range(nb):
                sf_ref[b, h] = st[h * nb + b].T


def hgrn2(z3, s0, lb, norm_w, nb, hh):
    bsz, seq, _ = z3.shape
    chunk = min(HG_CHUNK, seq)
    assert seq % chunk == 0 and bsz % nb == 0 and HG_HEADS % hh == 0
    wid = hh * LANES
    n_col = (HG_HEADS * HG_K) // wid

    def col(proj):
        return pl.BlockSpec((nb, chunk, wid), lambda h, b, t, proj=proj: (b, t, proj * n_col + h))

    st_blk = pl.BlockSpec((nb, hh, HG_K, HG_V), lambda h, b, t: (b, h, 0, 0))
    out, sf = pl.pallas_call(
        functools.partial(_hgrn_kernel, nb=nb, hh=hh, chunk=chunk),
        grid=(HG_HEADS // hh, bsz // nb, seq // chunk),
        in_specs=[col(1), col(2), col(3), col(4), st_blk,
                  pl.BlockSpec((1, wid), lambda h, b, t: (0, h)),
                  pl.BlockSpec((1, LANES), lambda h, b, t: (0, 0))],
        out_specs=[pl.BlockSpec((nb, chunk, wid), lambda h, b, t: (b, t, h)), st_blk],
        out_shape=[jax.ShapeDtypeStruct((bsz, seq, HG_HEADS * HG_V), BF16),
                   jax.ShapeDtypeStruct((bsz, HG_HEADS, HG_K, HG_V), F32)],
        scratch_shapes=[pltpu.VMEM((hh * nb, HG_V, HG_K), F32)],
        compiler_params=_params("parallel", "parallel", "arbitrary"),
        name="hgrn2",
    )(z3, z3, z3, z3, s0, lb.reshape(1, HG_HEADS * HG_K), norm_w.reshape(1, HG_V))
    return out, sf


RW_MIXES = 6


def _norm_mix_kernel(h_ref, sh_ref, w_ref, mu_ref, *refs, tl):
    o_refs = refs[:RW_MIXES]
    last_ref, scr = refs[RW_MIXES:]
    x = h_ref[...]
    xn = x * lax.rsqrt(jnp.mean(x * x, axis=-1, keepdims=True) + EPS) * w_ref[...][None]

    @pl.when(pl.program_id(1) == 0)
    def _():
        scr[:, SUBLANES - 1:SUBLANES, :] = sh_ref[...]

    scr[:, SUBLANES:, :] = xn
    xx = scr[:, SUBLANES - 1:SUBLANES - 1 + tl, :] - xn
    for j, o_ref in enumerate(o_refs):
        o_ref[...] = (xn + xx * mu_ref[j:j + 1, :][None]).astype(o_ref.dtype)
    last = xn[:, tl - 1:tl, :]
    scr[:, SUBLANES - 1:SUBLANES, :] = last
    last_ref[...] = last


def norm_mix(h3, shift0, ln_w, mu, nb, tl):
    bsz, seq, d = h3.shape
    assert bsz % nb == 0 and seq % tl == 0 and tl % SUBLANES == 0
    blk = pl.BlockSpec((nb, tl, d), lambda b, t: (b, t, 0))
    row = pl.BlockSpec((nb, 1, d), lambda b, t: (b, 0, 0))
    outs = pl.pallas_call(
        functools.partial(_norm_mix_kernel, tl=tl),
        grid=(bsz // nb, seq // tl),
        in_specs=[blk, row, pl.BlockSpec((1, d), lambda b, t: (0, 0)), pl.BlockSpec((RW_MIXES, d), lambda b, t: (0, 0))],
        out_specs=[blk] * RW_MIXES + [row],
        out_shape=[jax.ShapeDtypeStruct((bsz, seq, d), BF16)] * RW_MIXES + [jax.ShapeDtypeStruct((bsz, 1, d), F32)],
        scratch_shapes=[pltpu.VMEM((nb, SUBLANES + tl, d), F32)],
        compiler_params=_params("parallel", "arbitrary"),
        name="norm_mix",
    )(h3, shift0.reshape(bsz, 1, d), ln_w.reshape(1, d), mu)
    return outs[:RW_MIXES], outs[RW_MIXES].reshape(bsz, d)


RW_PAIR = LANES // RW_HEAD
RW_SOLVE_BLOCK = 8


def _rwkv_kernel(r_ref, k_ref, v_ref, wl_ref, al_ref, g_ref, s0_ref,
                 w0_ref, a0_ref, kk_ref, ka_ref, rk_ref, lnw_ref, lnb_ref,
                 o_ref, sf_ref, s_scr, *, nb, hp, chunk):
    nu = hp * nb
    step = pl.program_id(2)
    lane = lax.broadcasted_iota(jnp.int32, (1, 1, LANES), 2)
    head1 = lane >= RW_HEAD

    def units(x):
        return jnp.concatenate([x[:, :, p * LANES:(p + 1) * LANES] for p in range(hp)], axis=0)

    def unit_rows(ref):
        return units(jnp.broadcast_to(ref[...][None], (nb, 1, hp * LANES)))

    w0, a0, k_k, k_a, r_k, ln_w, ln_b = (unit_rows(p) for p in
                                         (w0_ref, a0_ref, kk_ref, ka_ref, rk_ref, lnw_ref, lnb_ref))
    sq_row = lax.broadcasted_iota(jnp.int32, (LANES, LANES), 0) >= RW_HEAD
    sq_col = lax.broadcasted_iota(jnp.int32, (LANES, LANES), 1) >= RW_HEAD
    same_head = sq_row == sq_col
    ones_bd = same_head.astype(BF16)

    @pl.when(step == 0)
    def _():
        zero = jnp.zeros((nb, RW_HEAD, RW_HEAD), F32)
        for p in range(hp):
            top = jnp.concatenate([s0_ref[:, RW_PAIR * p], zero], axis=-1)
            bot = jnp.concatenate([zero, s0_ref[:, RW_PAIR * p + 1]], axis=-1)
            s_scr[p * nb:(p + 1) * nb] = jnp.concatenate([top, bot], axis=1)

    def bdot(spec, a, b):
        return jnp.einsum(spec, a.astype(BF16), b.astype(BF16), preferred_element_type=F32)

    def head_sum(x):
        x2 = x.reshape(nu * chunk, LANES)
        hi = x2.astype(BF16)
        lo = (x2 - hi.astype(F32)).astype(BF16)
        s = jnp.dot(hi, ones_bd, preferred_element_type=F32) + jnp.dot(lo, ones_bd, preferred_element_type=F32)
        return s.reshape(nu, chunk, LANES)

    def stack_heads(x):
        return jnp.concatenate([jnp.where(head1, 0.0, x), jnp.where(head1, x, 0.0)], axis=1).astype(BF16)

    r, k, v, wl, al, g = (units(ref[...]) for ref in (r_ref, k_ref, v_ref, wl_ref, al_ref, g_ref))
    lw = -jnp.exp(-jax.nn.softplus(-(w0 + wl)) - 0.5)
    ag = jax.nn.sigmoid(a0 + al)
    kk = k * k_k
    kk = kk * (1.0 / jnp.maximum(jnp.sqrt(head_sum(kk * kk)), 1e-12))
    k2 = k * (1.0 + (ag - 1.0) * k_a)
    cl = _cumsum_rows(lw, chunk)
    e_pos = jnp.exp(cl)
    e_neg = jnp.exp(-cl)
    at = (-kk) * jnp.exp(cl - lw)
    bt = (kk * ag) * e_neg
    kt = k2 * e_neg
    rt = r * e_pos
    wc = e_pos[:, chunk - 1:chunk, :]
    trow = lax.broadcasted_iota(jnp.int32, (chunk, RW_PAIR * chunk), 0)
    tcol = lax.broadcasted_iota(jnp.int32, (chunk, RW_PAIR * chunk), 1)
    tcol = jnp.where(tcol >= chunk, tcol - chunk, tcol)
    strict = (tcol < trow)[None]
    incl = (tcol <= trow)[None]
    x2 = jnp.concatenate([at, rt], axis=1)
    pb = bdot("utc,usc->uts", x2, stack_heads(bt))
    pk = bdot("utc,usc->uts", x2, stack_heads(kt))
    lab = jnp.where(strict, pb[:, :chunk], 0.0)
    lak = jnp.where(strict, pk[:, :chunk], 0.0)
    arb = jnp.where(incl, pb[:, chunk:], 0.0)
    ark = jnp.where(incl, pk[:, chunk:], 0.0)
    v_bd = stack_heads(v)
    xa = at
    xv = bdot("uts,usc->utc", lak, v_bd)
    sub = min(RW_SOLVE_BLOCK, chunk)
    done_a, done_v = [], []
    for lo in range(0, chunk, sub):
        xa_i = xa[:, lo:lo + sub, :]
        xv_i = xv[:, lo:lo + sub, :]
        if lo:
            pad = jnp.zeros((nu, chunk - lo, LANES), F32)
            prev = jnp.concatenate([stack_heads(jnp.concatenate(done_a + [pad], axis=1)),
                                    stack_heads(jnp.concatenate(done_v + [pad], axis=1))], axis=-1)
            upd = bdot("uts,usc->utc", lab[:, lo:lo + sub, :], prev)
            xa_i = xa_i + upd[:, :, :LANES]
            xv_i = xv_i + upd[:, :, LANES:]
        l0 = lab[:, lo:lo + sub, lo:lo + sub]
        l1 = lab[:, lo:lo + sub, chunk + lo:chunk + lo + sub]
        for s in range(sub - 1):
            m = jnp.where(head1, l1[:, :, s:s + 1], l0[:, :, s:s + 1])
            xa_i = xa_i + m * xa_i[:, s:s + 1, :]
            xv_i = xv_i + m * xv_i[:, s:s + 1, :]
        done_a.append(xa_i)
        done_v.append(xv_i)
    ah = jnp.concatenate(done_a, axis=1)
    vh = jnp.concatenate(done_v, axis=1)
    both = bdot("uts,usc->utc", arb, jnp.concatenate([stack_heads(ah), stack_heads(vh)], axis=-1))
    rh = rt + both[:, :, :LANES]
    yh = both[:, :, LANES:] + bdot("uts,usc->utc", ark, v_bd)
    gp = jnp.where(same_head, bdot("utj,utk->ujk", ah, bt), 0.0)
    ht = jnp.where(same_head, bdot("utv,utk->uvk", jnp.concatenate([vh, v], axis=1),
                                   jnp.concatenate([bt, kt], axis=1)), 0.0)
    st = s_scr[...]
    y = bdot("utk,uvk->utv", rh, st) + yh
    st = (st + bdot("uvj,ujk->uvk", st, gp) + ht) * wc
    s_scr[...] = st
    inv_n = 1.0 / RW_HEAD
    yc = y - head_sum(y) * inv_n
    var = head_sum(yc * yc) * inv_n
    y = yc * lax.rsqrt(var + RW_GN_EPS) * ln_w + ln_b
    y = y + head_sum(r * k2 * r_k) * v
    out = (y * g).astype(o_ref.dtype)
    for p in range(hp):
        o_ref[:, :, p * LANES:(p + 1) * LANES] = out[p * nb:(p + 1) * nb]

    @pl.when(step == pl.num_programs(2) - 1)
    def _():
        for p in range(hp):
            sf_ref[:, RW_PAIR * p] = st[p * nb:(p + 1) * nb, :RW_HEAD, :RW_HEAD]
            sf_ref[:, RW_PAIR * p + 1] = st[p * nb:(p + 1) * nb, RW_HEAD:, RW_HEAD:]


def _rwkv_chunk(seq):
    for c in (48, 32, 16, 8):
        if seq % c == 0:
            return c
    raise ValueError(seq)


def rwkv7(r, k, v, wl, al, g, s0, w0, a0, k_k, k_a, r_k, ln_w, ln_b, nb, hp):
    bsz, seq, d = r.shape
    chunk = _rwkv_chunk(seq)
    heads = hp * RW_PAIR
    assert bsz % nb == 0 and RW_HEADS % heads == 0
    seq_blk = pl.BlockSpec((nb, chunk, hp * LANES), lambda h, b, t: (b, t, h))
    st_blk = pl.BlockSpec((nb, heads, RW_HEAD, RW_HEAD), lambda h, b, t: (b, h, 0, 0))
    vec = pl.BlockSpec((1, hp * LANES), lambda h, b, t: (0, h))
    out, sf = pl.pallas_call(
        functools.partial(_rwkv_kernel, nb=nb, hp=hp, chunk=chunk),
        grid=(RW_HEADS // heads, bsz // nb, seq // chunk),
        in_specs=[seq_blk] * 6 + [st_blk] + [vec] * 7,
        out_specs=[seq_blk, st_blk],
        out_shape=[jax.ShapeDtypeStruct((bsz, seq, d), BF16),
                   jax.ShapeDtypeStruct((bsz, RW_HEADS, RW_HEAD, RW_HEAD), F32)],
        scratch_shapes=[pltpu.VMEM((hp * nb, LANES, LANES), F32)],
        compiler_params=_params("parallel", "parallel", "arbitrary"),
        name="rwkv7",
    )(r, k, v, wl, al, g, s0, *(p.reshape(1, d) for p in (w0, a0, k_k, k_a, r_k, ln_w, ln_b)))
    return out, sf


def _ffn_in_kernel(x_ref, wa_ref, wv_ref, e_ref, cw_ref, cb_ref, o_ref, st_ref, scr, *, nb, seq, sb, sr):
    tn = wa_ref.shape[1]
    cw = cw_ref[...]
    cb = cb_ref[...][None]
    scr[:, 0:SUBLANES, :] = e_ref[...]
    for b0 in range(0, nb, sb):
        for r0 in range(0, seq, sr):
            lo = b0 * seq + r0
            x = x_ref[lo:lo + sb * sr, :]
            a = jnp.dot(x, wa_ref[...], preferred_element_type=F32).reshape(sb, sr, tn)
            v = jnp.dot(x, wv_ref[...], preferred_element_type=F32).reshape(sb, sr, tn)
            scr[b0:b0 + sb, SUBLANES + r0:SUBLANES + r0 + sr, :] = a
            c = cb + cw[CONV_W - 1:CONV_W][None] * a
            for j in range(CONV_W - 1):
                first = SUBLANES + r0 - (CONV_W - 1 - j)
                c = c + cw[j:j + 1][None] * scr[b0:b0 + sb, first:first + sr, :]
            o_ref[lo:lo + sb * sr, :] = (jax.nn.gelu(c) * v).reshape(sb * sr, tn).astype(o_ref.dtype)
    st_ref[...] = scr[:, SUBLANES + seq - (CONV_W - 1):SUBLANES + seq, :]


def ffn_in(xb, conv0, w_in, layer, conv_w, conv_b, bsz, seq, nb, tn, sub):
    t, d = xb.shape
    sb, sr = sub
    assert t == bsz * seq and bsz % nb == 0 and D_FF % tn == 0
    assert nb % sb == 0 and seq % sr == 0 and sr % SUBLANES == 0 and (sr == seq or nb == sb == 1)
    halo = jnp.pad(conv0, ((0, 0), (SUBLANES - (CONV_W - 1), 0), (0, 0)))
    col = lambda i, j: (0, j)
    n_col = D_FF // tn
    out, st = pl.pallas_call(
        functools.partial(_ffn_in_kernel, nb=nb, seq=seq, sb=sb, sr=sr),
        grid=(bsz // nb, n_col),
        in_specs=[pl.BlockSpec((nb * seq, d), lambda i, j: (i, 0)),
                  pl.BlockSpec((None, d, tn), lambda i, j: (layer, 0, j)),
                  pl.BlockSpec((None, d, tn), lambda i, j: (layer, 0, j + n_col)),
                  pl.BlockSpec((nb, SUBLANES, tn), lambda i, j: (i, 0, j)),
                  pl.BlockSpec((CONV_W, tn), col), pl.BlockSpec((1, tn), col)],
        out_specs=[pl.BlockSpec((nb * seq, tn), lambda i, j: (i, j)),
                   pl.BlockSpec((nb, CONV_W - 1, tn), lambda i, j: (i, 0, j))],
        out_shape=[jax.ShapeDtypeStruct((t, D_FF), BF16),
                   jax.ShapeDtypeStruct((bsz, CONV_W - 1, D_FF), F32)],
        scratch_shapes=[pltpu.VMEM((nb, SUBLANES + seq, tn), F32)],
        compiler_params=_params("parallel", "parallel"),
        name="ffn_in",
    )(xb, w_in, w_in, halo, conv_w, conv_b.reshape(1, D_FF))
    return out, st


def _channel_mixer(h, conv0, layer, p, cfg, bsz, seq):
    (xb,) = rmsnorm(h, p["ln_ffn"][layer], (BF16,))
    gated, n_cv = ffn_in(xb, conv0, p["ffn_w_in"], layer, p["ffn_conv_w"][layer], p["ffn_conv_b"][layer],
                         bsz, seq, cfg["ffn_nb"], cfg["ffn_tn"], cfg["ffn_sub"])
    return matmul([gated], [(p["ffn_w_down"], layer, 0)], F32, epilogue="residual", extra=h), n_cv


def _trunk(x3, s5r, s5i, hg, rw, sh, cv, p, cfg):
    bsz, seq, d = x3.shape
    seq += cfg["front"]
    t = bsz * seq

    if cfg["front"]:
        h, xb = embed_norm(x3, p["meta"], p["ln_mix"][0])
    else:
        h = x3.reshape(t, d)
        (xb,) = rmsnorm(h, p["ln_mix"][0], (BF16,))
    z = matmul([xb], [p["ev_w_in"]], F32).reshape(bsz, seq, EVEN_IN)
    ys5, n_s5r, n_s5i = s5_scan(z, s5r[0], s5i[0], *p["s5"], nb=cfg["s5_nb"])
    ys5 = ys5.reshape(t, S5_WIDTH)
    ya = matmul([ys5.astype(BF16)], [p["s5_w_glu"]], BF16, epilogue="glu", extra=ys5)
    yb, n_hg = hgrn2(z, hg[0], p["hg_lb"], p["hg_norm_w"], nb=cfg["hg_nb"], hh=cfg["hg_hh"])
    h = matmul([ya, yb.reshape(t, -1)], [(p["ev_w_out"], 0, 0), (p["ev_w_out"], 0, 1)], F32,
               epilogue="residual", extra=h)
    h, n_cv0 = _channel_mixer(h, cv[0], 0, p, cfg, bsz, seq)

    mixes, n_sh = norm_mix(h.reshape(bsz, seq, d), sh[0], p["ln_mix"][1], p["rw_mu"], *cfg["mix_blk"])
    xr, xw, xk, xv, xa, xg = (m.reshape(t, d) for m in mixes)
    r = matmul([xr], [p["rw_w_r"]], F32)
    k = matmul([xk], [p["rw_w_k"]], F32)
    v = matmul([xv], [p["rw_w_v"]], F32)
    wl = matmul([matmul([xw], [p["rw_w1"]], BF16, act="tanh")], [p["rw_w2"]], F32)
    al = matmul([matmul([xa], [p["rw_a1"]], BF16)], [p["rw_a2"]], F32)
    g = matmul([matmul([xg], [p["rw_g1"]], BF16, act="sigmoid")], [p["rw_g2"]], F32)
    as3 = lambda a: a.reshape(bsz, seq, d)
    yo, n_rw = rwkv7(as3(r), as3(k), as3(v), as3(wl), as3(al), as3(g), rw[0], *p["rw_vec"],
                     nb=cfg["rw_nb"], hp=cfg["rw_hp"])
    h = matmul([yo.reshape(t, d)], [p["rw_w_o"]], F32, epilogue="residual", extra=h)
    h, n_cv1 = _channel_mixer(h, cv[1], 1, p, cfg, bsz, seq)

    (y,) = rmsnorm(h, p["ln_final"], (F32,), rows=(bsz, seq, cfg["front"]) if cfg["front"] else None)
    return (y.reshape(bsz, seq - cfg["front"], d), n_s5r[None], n_s5i[None], n_hg[None], n_rw[None], n_sh[None],
            jnp.stack([n_cv0, n_cv1]))


PROMPT_CFG = dict(front=N_META, s5_nb=1, hg_nb=4, hg_hh=8, mix_blk=(1, 344), rw_nb=4, rw_hp=8, ffn_nb=1, ffn_tn=512, ffn_sub=(1, 688))
SAMPLE_CFG = dict(front=0, s5_nb=32, hg_nb=4, hg_hh=8, mix_blk=(32, 8), rw_nb=16, rw_hp=2, ffn_nb=128, ffn_tn=512, ffn_sub=(32, 8))


def kernel(x_prompt, x_sample, state_s5_re, state_s5_im, state_hgrn, state_rwkv, state_shift, state_conv, meta_tokens, ln_mix, ln_ffn, ln_final, ev_w_in, ev_w_out, s5_lam_re, s5_lam_im, s5_log_step, s5_b_re, s5_b_im, s5_c_re, s5_c_im, s5_d, s5_w_glu, hg_lb, hg_norm_w, rw_mu, rw_w0, rw_w1, rw_w2, rw_a0, rw_a1, rw_a2, rw_g1, rw_g2, rw_k_k, rw_k_a, rw_r_k, rw_w_r, rw_w_k, rw_w_v, rw_w_o, rw_ln_w, rw_ln_b, ffn_w_in, ffn_conv_w, ffn_conv_b, ffn_w_down):
    bf = lambda w: w.astype(BF16)
    lb_all = hg_lower_bounds(hg_lb)
    pwr, pwi, bbr_t, bbi_t = s5_prep(s5_lam_re[0], s5_lam_im[0], s5_log_step[0], s5_b_re[0], s5_b_im[0])
    wbr, wbi, wcr, wci = _s5_block_weights(bbr_t, bbi_t, s5_c_re[0], s5_c_im[0])
    p = {
        "meta": meta_tokens, "ln_mix": ln_mix, "ln_ffn": ln_ffn, "ln_final": ln_final,
        "ev_w_in": bf(ev_w_in[0]),
        "ev_w_out": bf(ev_w_out),
        "s5": (pwr, pwi, wbr, wbi, wcr, wci, s5_d[0].reshape(1, S5_WIDTH)),
        "s5_w_glu": bf(s5_w_glu[0]),
        "hg_lb": lb_all[0], "hg_norm_w": hg_norm_w[0],
        "rw_mu": rw_mu[0],
        "rw_w1": bf(rw_w1[0]), "rw_w2": bf(rw_w2[0]), "rw_a1": bf(rw_a1[0]), "rw_a2": bf(rw_a2[0]),
        "rw_g1": bf(rw_g1[0]), "rw_g2": bf(rw_g2[0]),
        "rw_w_r": bf(rw_w_r[0]), "rw_w_k": bf(rw_w_k[0]), "rw_w_v": bf(rw_w_v[0]), "rw_w_o": bf(rw_w_o[0]),
        "rw_vec": (rw_w0[0], rw_a0[0], rw_k_k[0], rw_k_a[0], rw_r_k[0].reshape(D_MODEL), rw_ln_w[0], rw_ln_b[0]),
        "ffn_w_in": bf(ffn_w_in), "ffn_conv_w": ffn_conv_w, "ffn_conv_b": ffn_conv_b,
        "ffn_w_down": bf(ffn_w_down),
    }

    bsz = x_prompt.shape[0]
    zeros = lambda *s: jnp.zeros(s, F32)
    outs_p = _trunk(x_prompt,
                    zeros(1, bsz, S5_GROUPS, S5_STATE), zeros(1, bsz, S5_GROUPS, S5_STATE),
                    zeros(1, bsz, HG_HEADS, HG_K, HG_V), zeros(1, bsz, RW_HEADS, RW_HEAD, RW_HEAD),
                    zeros(1, bsz, D_MODEL), zeros(2, bsz, CONV_W - 1, D_FF), p, PROMPT_CFG)
    outs_s = _trunk(x_sample, state_s5_re, state_s5_im, state_hgrn, state_rwkv, state_shift, state_conv,
                    p, SAMPLE_CFG)
    return tuple(outs_p[:1]) + tuple(outs_s[:1]) + tuple(outs_p[1:]) + tuple(outs_s[1:])
```

```python
import functools

import jax
import jax.numpy as jnp
from jax import lax
from jax.experimental import pallas as pl
from jax.experimental.pallas import tpu as pltpu

F32 = jnp.float32
BF16 = jnp.bfloat16

D_MODEL = 2048
N_META = 16
EPS = 1e-6
S5_WIDTH = 1024
S5_GROUP = 16
S5_GROUPS = 64
S5_STATE = 64
S5_CH = S5_GROUPS * S5_STATE
HG_HEADS = 8
HG_K = 128
HG_V = 128
HG_CHUNK = 16
EVEN_IN = 5120
RW_HEAD = 64
RW_HEADS = 32
RW_GN_EPS = 64e-5
D_FF = 5632
CONV_W = 3

LANES = 128
SUBLANES = 8
VMEM_LIMIT = 56 * 1024 * 1024


def _params(*sem):
    return pltpu.CompilerParams(dimension_semantics=sem, vmem_limit_bytes=VMEM_LIMIT)


def _row_tile(t, cap=1024):
    best = None
    for d in range(16, min(t, cap) + 1, 16):
        if t % d == 0:
            best = d
    assert best is not None, t
    return best


def _rms_kernel(x_ref, w_ref, *o_refs):
    x = x_ref[...]
    y = x * lax.rsqrt(jnp.mean(x * x, axis=-1, keepdims=True) + EPS) * w_ref[...]
    for o_ref in o_refs:
        o_ref[...] = y.astype(o_ref.dtype)


def rmsnorm(x, w, dtypes, rows=None):
    t, d = x.shape
    if rows is None:
        tm = _row_tile(t)
        n_out = t
        grid = (t // tm,)
        in_spec = pl.BlockSpec((tm, d), lambda i: (i, 0))
        out_spec = in_spec
        vec = pl.BlockSpec((1, d), lambda i: (0, 0))
    else:
        bsz, seq, front = rows
        keep = seq - front
        assert t == bsz * seq and front % 16 == 0
        tm = _row_tile(keep)
        per = keep // tm
        n_out = bsz * keep
        grid = (bsz, per)
        in_spec = pl.BlockSpec((pl.Element(tm), pl.Element(d)),
                               lambda b, i: (pl.multiple_of(b * seq + front + i * tm, 16), 0))
        out_spec = pl.BlockSpec((tm, d), lambda b, i: (b * per + i, 0))
        vec = pl.BlockSpec((1, d), lambda b, i: (0, 0))
    outs = pl.pallas_call(
        _rms_kernel,
        grid=grid,
        in_specs=[in_spec, vec],
        out_specs=[out_spec for _ in dtypes],
        out_shape=[jax.ShapeDtypeStruct((n_out, d), dt) for dt in dtypes],
        compiler_params=_params(*(["parallel"] * len(grid))),
        name="rmsnorm",
    )(x, w.reshape(1, d))
    return outs


def _embed_norm_kernel(x_ref, m_ref, w_ref, h_ref, xb_ref, *, front):
    def emit(rows):
        h_ref[...] = rows
        y = rows * lax.rsqrt(jnp.mean(rows * rows, axis=-1, keepdims=True) + EPS) * w_ref[...]
        xb_ref[...] = y.astype(xb_ref.dtype)

    @pl.when(pl.program_id(1) == 0)
    def _():
        emit(jnp.concatenate([m_ref[...], x_ref[:x_ref.shape[0] - front, :]], axis=0))

    @pl.when(pl.program_id(1) > 0)
    def _():
        emit(x_ref[...])


def embed_norm(x3, meta, w):
    bsz, seq, d = x3.shape
    front = meta.shape[0]
    total = seq + front
    tm = _row_tile(total)
    per = total // tm
    assert front % 16 == 0 and tm > front
    blk = pl.BlockSpec((tm, d), lambda b, j: (b * per + j, 0))
    return pl.pallas_call(
        functools.partial(_embed_norm_kernel, front=front),
        grid=(bsz, per),
        in_specs=[pl.BlockSpec((pl.Element(tm), pl.Element(d)),
                               lambda b, j: (pl.multiple_of(b * seq + jnp.maximum(j * tm - front, 0), 16), 0)),
                  pl.BlockSpec((front, d), lambda b, j: (0, 0)),
                  pl.BlockSpec((1, d), lambda b, j: (0, 0))],
        out_specs=[blk, blk],
        out_shape=[jax.ShapeDtypeStruct((bsz * total, d), F32), jax.ShapeDtypeStruct((bsz * total, d), BF16)],
        compiler_params=_params("parallel", "parallel"),
        name="embed_norm",
    )(x3.reshape(bsz * seq, d), meta, w.reshape(1, d))


def _act(x, act):
    if act == "tanh":
        return jnp.tanh(x)
    if act == "sigmoid":
        return jax.nn.sigmoid(x)
    assert act is None
    return x


def _mm_kernel(*refs, n_a, act, epilogue):
    a_refs = refs[:n_a]
    w_refs = refs[n_a:2 * n_a]
    rest = refs[2 * n_a:]
    o_ref = rest[-1]
    acc = jnp.dot(a_refs[0][...], w_refs[0][...], preferred_element_type=F32)
    for a_ref, w_ref in zip(a_refs[1:], w_refs[1:]):
        acc = acc + jnp.dot(a_ref[...], w_ref[...], preferred_element_type=F32)
    acc = _act(acc, act)
    if epilogue == "residual":
        acc = rest[0][...] + acc
    elif epilogue == "glu":
        acc = rest[0][...] * jax.nn.sigmoid(acc)
    o_ref[...] = acc.astype(o_ref.dtype)


MM_VMEM_BUDGET = 40 * 1024 * 1024
MXU_WIDTH = 256


def _mm_tiles(t, k_total, n, out_bytes, has_extra):
    rows = [d for d in range(16, t + 1, 16) if t % d == 0]
    cols = [d for d in range(LANES, n + 1, LANES) if n % d == 0] or [n]
    best, best_score = None, -1.0
    for tm in rows:
        for tn in cols:
            est = 4 * tm * k_total + 4 * k_total * tn + tm * tn * (2 * out_bytes + 4 + (8 if has_extra else 0))
            if est > MM_VMEM_BUDGET:
                continue
            score = tm * tn * (1.0 if tn % MXU_WIDTH == 0 else 0.8)
            if score > best_score:
                best, best_score = (tm, tn), score
    assert best is not None, (t, k_total, n)
    return best


def matmul(a_list, w_list, out_dtype, act=None, epilogue=None, extra=None):
    t = a_list[0].shape[0]
    n = w_list[0][0].shape[2] if isinstance(w_list[0], tuple) else w_list[0].shape[1]
    tm, tn = _mm_tiles(t, sum(a.shape[1] for a in a_list), n, jnp.dtype(out_dtype).itemsize, epilogue is not None)
    in_specs = [pl.BlockSpec((tm, a.shape[1]), lambda i, j: (i, 0)) for a in a_list]
    args = list(a_list)
    for a, w in zip(a_list, w_list):
        if isinstance(w, tuple):
            w, layer, kblk = w
            in_specs.append(pl.BlockSpec((None, a.shape[1], tn), lambda i, j, layer=layer, kblk=kblk: (layer, kblk, j)))
        else:
            in_specs.append(pl.BlockSpec((w.shape[0], tn), lambda i, j: (0, j)))
        args.append(w)
    if epilogue is not None:
        in_specs.append(pl.BlockSpec((tm, tn), lambda i, j: (i, j)))
        args.append(extra)
    return pl.pallas_call(
        functools.partial(_mm_kernel, n_a=len(a_list), act=act, epilogue=epilogue),
        grid=(t // tm, n // tn),
        in_specs=in_specs,
        out_specs=pl.BlockSpec((tm, tn), lambda i, j: (i, j)),
        out_shape=jax.ShapeDtypeStruct((t, n), out_dtype),
        compiler_params=_params("parallel", "parallel"),
        name="matmul",
    )(*args)


def _s5_prep_kernel(lr_ref, li_ref, ls_ref, brt_ref, bit_ref, pwr_ref, pwi_ref, bbr_ref, bbi_ref):
    lr = jnp.minimum(lr_ref[...], -1e-4)
    li = li_ref[...]
    dt = jnp.exp(ls_ref[...])
    n = lax.broadcasted_iota(jnp.int32, (SUBLANES, S5_CH), 0).astype(F32) + 1.0
    mag = jnp.exp(n * (lr * dt))
    ang = n * (li * dt)
    pwr = mag * jnp.cos(ang)
    pwi = mag * jnp.sin(ang)
    pwr_ref[...] = pwr
    pwi_ref[...] = pwi
    ar = pwr[0:1]
    ai = pwi[0:1]
    den = lr * lr + li * li
    zr = ((ar - 1.0) * lr + ai * li) / den
    zi = (ai * lr - (ar - 1.0) * li) / den
    br = brt_ref[...]
    bi = bit_ref[...]
    bbr_ref[...] = zr * br - zi * bi
    bbi_ref[...] = zr * bi + zi * br


def s5_prep(lam_re, lam_im, log_step, b_re, b_im):
    lr = lam_re.reshape(1, S5_CH)
    li = lam_im.reshape(1, S5_CH)
    ls = jnp.broadcast_to(log_step[:, None], (S5_GROUPS, S5_STATE)).reshape(1, S5_CH)
    brt = b_re.reshape(S5_CH, S5_GROUP).T
    bit = b_im.reshape(S5_CH, S5_GROUP).T
    return pl.pallas_call(
        _s5_prep_kernel,
        out_shape=[jax.ShapeDtypeStruct((SUBLANES, S5_CH), F32)] * 2
        + [jax.ShapeDtypeStruct((S5_GROUP, S5_CH), F32)] * 2,
        name="s5_prep",
    )(lr, li, ls, brt, bit)


S5_BLK_GROUPS = LANES // S5_GROUP
S5_BLKS = S5_WIDTH // LANES
S5_BLK_CH = S5_BLK_GROUPS * S5_STATE


def _cmul_add(xr, xi, mr, mi, sr, si):
    return xr + mr * sr - mi * si, xi + mr * si + mi * sr


def _s5_kernel(u_ref, h0r_ref, h0i_ref, pwr_ref, pwi_ref, wbr_ref, wbi_ref, wcr_ref, wci_ref, d_ref,
               y_ref, hr_ref, hi_ref, xr_scr, xi_scr, *, nb, seq):
    u2 = u_ref[...].reshape(nb * seq, LANES)
    ub = u2.astype(BF16)
    xr_scr[...] = jnp.dot(ub, wbr_ref[0], preferred_element_type=F32).reshape(nb, seq, S5_BLK_CH)
    xi_scr[...] = jnp.dot(ub, wbi_ref[0], preferred_element_type=F32).reshape(nb, seq, S5_BLK_CH)

    pwr = pwr_ref[...]
    pwi = pwi_ref[...]
    row = lax.broadcasted_iota(jnp.int32, (SUBLANES, S5_BLK_CH), 0)
    steps = []
    for d in (1, 2, 4):
        keep = row >= d
        steps.append((d, jnp.where(keep, pwr[d - 1:d], 0.0)[None], jnp.where(keep, pwi[d - 1:d], 0.0)[None]))
    pr = pwr[None]
    pi = pwi[None]

    def tile(i, carry):
        cr, ci = carry
        o = pl.multiple_of(i * SUBLANES, SUBLANES)
        xr = xr_scr[:, pl.ds(o, SUBLANES), :]
        xi = xi_scr[:, pl.ds(o, SUBLANES), :]
        for d, mr, mi in steps:
            sr = pltpu.roll(xr, d, axis=1)
            si = pltpu.roll(xi, d, axis=1)
            xr, xi = _cmul_add(xr, xi, mr, mi, sr, si)
        xr, xi = _cmul_add(xr, xi, pr, pi, cr, ci)
        xr_scr[:, pl.ds(o, SUBLANES), :] = xr
        xi_scr[:, pl.ds(o, SUBLANES), :] = xi
        return xr[:, SUBLANES - 1:SUBLANES, :], xi[:, SUBLANES - 1:SUBLANES, :]

    hr, hi = lax.fori_loop(0, seq // SUBLANES, tile, (h0r_ref[...], h0i_ref[...]))
    hr_ref[...] = hr
    hi_ref[...] = hi

    xr = xr_scr[...].reshape(nb * seq, S5_BLK_CH).astype(BF16)
    xi = xi_scr[...].reshape(nb * seq, S5_BLK_CH).astype(BF16)
    y = (jnp.dot(xr, wcr_ref[0], preferred_element_type=F32)
         - jnp.dot(xi, wci_ref[0], preferred_element_type=F32)
         + d_ref[...] * u2)
    y_ref[...] = jax.nn.gelu(y).reshape(nb, seq, LANES)


def s5_scan(z3, h0r, h0i, pwr, pwi, wbr, wbi, wcr, wci, d, nb):
    bsz, seq, _ = z3.shape
    assert seq % SUBLANES == 0 and bsz % nb == 0
    seq_blk = pl.BlockSpec((nb, seq, LANES), lambda b, k: (b, 0, k))
    st_blk = pl.BlockSpec((nb, 1, S5_BLK_CH), lambda b, k: (b, 0, k))
    pw_blk = pl.BlockSpec((SUBLANES, S5_BLK_CH), lambda b, k: (0, k))
    wb_blk = pl.BlockSpec((1, LANES, S5_BLK_CH), lambda b, k: (k, 0, 0))
    wc_blk = pl.BlockSpec((1, S5_BLK_CH, LANES), lambda b, k: (k, 0, 0))
    y, hr, hi = pl.pallas_call(
        functools.partial(_s5_kernel, nb=nb, seq=seq),
        grid=(bsz // nb, S5_BLKS),
        in_specs=[seq_blk, st_blk, st_blk, pw_blk, pw_blk, wb_blk, wb_blk, wc_blk, wc_blk,
                  pl.BlockSpec((1, LANES), lambda b, k: (0, k))],
        out_specs=[seq_blk, st_blk, st_blk],
        out_shape=[jax.ShapeDtypeStruct((bsz, seq, S5_WIDTH), F32),
                   jax.ShapeDtypeStruct((bsz, 1, S5_CH), F32),
                   jax.ShapeDtypeStruct((bsz, 1, S5_CH), F32)],
        scratch_shapes=[pltpu.VMEM((nb, seq, S5_BLK_CH), F32), pltpu.VMEM((nb, seq, S5_BLK_CH), F32)],
        compiler_params=_params("parallel", "parallel"),
        name="s5_scan",
    )(z3, h0r.reshape(bsz, 1, S5_CH), h0i.reshape(bsz, 1, S5_CH), pwr, pwi, wbr, wbi, wcr, wci, d)
    return y, hr.reshape(bsz, S5_GROUPS, S5_STATE), hi.reshape(bsz, S5_GROUPS, S5_STATE)


S5_LANE_TILES = S5_BLK_CH // LANES


def _s5_long_kernel(u_ref, h0r_ref, h0i_ref, ar_ref, ai_ref, wbr_ref, wbi_ref, wcr_ref, wci_ref, d_ref,
                    y_ref, hr_ref, hi_ref, xr_scr, xi_scr, cr_scr, ci_scr, *, tl):
    step = pl.program_id(1)

    @pl.when(step == 0)
    def _():
        cr_scr[...] = h0r_ref[0]
        ci_scr[...] = h0i_ref[0]

    for k in range(S5_BLKS):
        ub = u_ref[0, :, k * LANES:(k + 1) * LANES].astype(BF16)
        bur = jnp.dot(ub, wbr_ref[k], preferred_element_type=F32)
        bui = jnp.dot(ub, wbi_ref[k], preferred_element_type=F32)
        for j in range(S5_LANE_TILES):
            xr_scr[j, k * tl:(k + 1) * tl, :] = bur[:, j * LANES:(j + 1) * LANES]
            xi_scr[j, k * tl:(k + 1) * tl, :] = bui[:, j * LANES:(j + 1) * LANES]

    ar = [ar_ref[:, j * LANES:(j + 1) * LANES] for j in range(S5_LANE_TILES)]
    ai = [ai_ref[:, j * LANES:(j + 1) * LANES] for j in range(S5_LANE_TILES)]

    def token(t, carry):
        cr, ci = carry
        nr, ni = [], []
        for j in range(S5_LANE_TILES):
            rows = pl.ds(t, S5_BLKS, stride=tl)
            xr, xi = _cmul_add(xr_scr[j, rows, :], xi_scr[j, rows, :], ar[j], ai[j], cr[j], ci[j])
            xr_scr[j, rows, :] = xr
            xi_scr[j, rows, :] = xi
            nr.append(xr)
            ni.append(xi)
        return tuple(nr), tuple(ni)

    init = (tuple(cr_scr[:, j * LANES:(j + 1) * LANES] for j in range(S5_LANE_TILES)),
            tuple(ci_scr[:, j * LANES:(j + 1) * LANES] for j in range(S5_LANE_TILES)))
    cr, ci = lax.fori_loop(0, tl, token, init, unroll=8)
    cr = jnp.concatenate(cr, axis=-1)
    ci = jnp.concatenate(ci, axis=-1)
    cr_scr[...] = cr
    ci_scr[...] = ci
    hr_ref[0] = cr
    hi_ref[0] = ci

    for k in range(S5_BLKS):
        xr = jnp.concatenate([xr_scr[j, k * tl:(k + 1) * tl, :] for j in range(S5_LANE_TILES)], axis=-1)
        xi = jnp.concatenate([xi_scr[j, k * tl:(k + 1) * tl, :] for j in range(S5_LANE_TILES)], axis=-1)
        u = u_ref[0, :, k * LANES:(k + 1) * LANES]
        y = (jnp.dot(xr.astype(BF16), wcr_ref[k], preferred_element_type=F32)
             - jnp.dot(xi.astype(BF16), wci_ref[k], preferred_element_type=F32)
             + d_ref[:, k * LANES:(k + 1) * LANES] * u)
        y_ref[0, :, k * LANES:(k + 1) * LANES] = jax.nn.gelu(y)


def s5_scan_long(z3, h0r, h0i, pwr, pwi, wbr, wbi, wcr, wci, d, tl):
    bsz, seq, _ = z3.shape
    assert seq % tl == 0 and tl % SUBLANES == 0
    seq_blk = pl.BlockSpec((1, tl, S5_WIDTH), lambda b, t: (b, t, 0))
    st_blk = pl.BlockSpec((1, S5_BLKS, S5_BLK_CH), lambda b, t: (b, 0, 0))
    lam_blk = pl.BlockSpec((S5_BLKS, S5_BLK_CH), lambda b, t: (0, 0))
    wb_blk = pl.BlockSpec((S5_BLKS, LANES, S5_BLK_CH), lambda b, t: (0, 0, 0))
    wc_blk = pl.BlockSpec((S5_BLKS, S5_BLK_CH, LANES), lambda b, t: (0, 0, 0))
    rows = pltpu.VMEM((S5_LANE_TILES, S5_BLKS * tl, LANES), F32)
    carry = pltpu.VMEM((S5_BLKS, S5_BLK_CH), F32)
    y, hr, hi = pl.pallas_call(
        functools.partial(_s5_long_kernel, tl=tl),
        grid=(bsz, seq // tl),
        in_specs=[seq_blk, st_blk, st_blk, lam_blk, lam_blk, wb_blk, wb_blk, wc_blk, wc_blk,
                  pl.BlockSpec((1, S5_WIDTH), lambda b, t: (0, 0))],
        out_specs=[seq_blk, st_blk, st_blk],
        out_shape=[jax.ShapeDtypeStruct((bsz, seq, S5_WIDTH), F32),
                   jax.ShapeDtypeStruct((bsz, S5_BLKS, S5_BLK_CH), F32),
                   jax.ShapeDtypeStruct((bsz, S5_BLKS, S5_BLK_CH), F32)],
        scratch_shapes=[rows, rows, carry, carry],
        compiler_params=_params("parallel", "arbitrary"),
        name="s5_scan_long",
    )(z3, h0r.reshape(bsz, S5_BLKS, S5_BLK_CH), h0i.reshape(bsz, S5_BLKS, S5_BLK_CH),
      pwr[0].reshape(S5_BLKS, S5_BLK_CH), pwi[0].reshape(S5_BLKS, S5_BLK_CH), wbr, wbi, wcr, wci, d)
    return y, hr.reshape(bsz, S5_GROUPS, S5_STATE), hi.reshape(bsz, S5_GROUPS, S5_STATE)


def _s5_block_weights(bbr_t, bbi_t, c_re, c_im):
    eye = jnp.eye(S5_BLK_GROUPS, dtype=F32)

    def wb(bt):
        b4 = bt.reshape(S5_GROUP, S5_BLKS, S5_BLK_GROUPS, S5_STATE)
        w = jnp.einsum("cbgp,hg->bhcgp", b4, eye)
        return w.reshape(S5_BLKS, LANES, S5_BLK_CH).astype(BF16)

    def wc(c):
        c4 = c.reshape(S5_BLKS, S5_BLK_GROUPS, S5_GROUP, S5_STATE)
        w = jnp.einsum("bgcp,hg->bhpgc", c4, eye)
        return w.reshape(S5_BLKS, S5_BLK_CH, LANES).astype(BF16)

    return wb(bbr_t), wb(bbi_t), wc(c_re), wc(c_im)


def _hg_lb_kernel(x_ref, o_ref):
    x = x_ref[...]
    e = jnp.exp(x - jnp.max(x, axis=0, keepdims=True))
    sm = e / jnp.sum(e, axis=0, keepdims=True)
    acc = sm[0:1]
    o_ref[0:1, :] = acc
    for l in range(1, x.shape[0]):
        acc = acc + sm[l:l + 1]
        o_ref[l:l + 1, :] = acc


def hg_lower_bounds(hg_lb):
    return pl.pallas_call(_hg_lb_kernel, out_shape=jax.ShapeDtypeStruct(hg_lb.shape, F32), name="hg_lb")(hg_lb)


def _cumsum_rows(x, n):
    row = lax.broadcasted_iota(jnp.int32, x.shape, 1)
    d = 1
    while d < n:
        x = x + jnp.where(row >= d, pltpu.roll(x, d, axis=1), 0.0)
        d *= 2
    return x


def _hgrn_kernel(q_ref, f_ref, i_ref, g_ref, s0_ref, lb_ref, nw_ref, o_ref, sf_ref, st_scr, *, nb, hh, chunk):
    step = pl.program_id(2)

    def units(x):
        return jnp.concatenate([x[:, :, h * LANES:(h + 1) * LANES] for h in range(hh)], axis=0)

    lb = units(jnp.broadcast_to(lb_ref[...][None], (nb, 1, hh * LANES)))
    nw = nw_ref[...][None]

    @pl.when(step == 0)
    def _():
        for h in range(hh):
            for b in range(nb):
                st_scr[h * nb + b] = s0_ref[b, h].T

    trow = lax.broadcasted_iota(jnp.int32, (chunk, chunk), 0)
    tcol = lax.broadcasted_iota(jnp.int32, (chunk, chunk), 1)
    causal = (tcol <= trow)[None]
    q, f, v, g = (units(ref[...]) for ref in (q_ref, f_ref, i_ref, g_ref))
    fg = lb + (1.0 - lb) * jax.nn.sigmoid(f)
    qh = jax.nn.silu(q)
    kh = 1.0 - fg
    bcum = _cumsum_rows(jnp.log(fg), chunk)
    btot = bcum[:, chunk - 1:chunk, :]
    q_in = (qh * jnp.exp(bcum)).astype(BF16)
    k_in = (kh * jnp.exp(-bcum)).astype(BF16)
    k_end = (kh * jnp.exp(btot - bcum)).astype(BF16)
    decay = jnp.exp(btot)
    vb = v.astype(BF16)
    st = st_scr[...]
    att = jnp.einsum("utk,usk->uts", q_in, k_in, preferred_element_type=F32)
    att = jnp.where(causal, att, 0.0).astype(BF16)
    out = (jnp.einsum("utk,uvk->utv", q_in, st.astype(BF16), preferred_element_type=F32)
           + jnp.einsum("uts,usv->utv", att, vb, preferred_element_type=F32))
    st = st * decay + jnp.einsum("usv,usk->uvk", vb, k_end, preferred_element_type=F32)
    st_scr[...] = st
    out = out * lax.rsqrt(jnp.mean(out * out, axis=-1, keepdims=True) + EPS) * nw
    out = (out * jax.nn.silu(g)).astype(o_ref.dtype)
    for h in range(hh):
        o_ref[:, :, h * LANES:(h + 1) * LANES] = out[h * nb:(h + 1) * nb]

    @pl.when(step == pl.num_programs(2) - 1)
    def _():
        for h in range(hh):
            for b in range(nb):
                sf_ref[b, h] = st[h * nb + b].T


def hgrn2(z3, s0, lb, norm_w, nb, hh):
    bsz, seq, _ = z3.shape
    chunk = min(HG_CHUNK, seq)
    assert seq % chunk == 0 and bsz % nb == 0 and HG_HEADS % hh == 0
    wid = hh * LANES
    n_col = (HG_HEADS * HG_K) // wid

    def col(proj):
        return pl.BlockSpec((nb, chunk, wid), lambda h, b, t, proj=proj: (b, t, proj * n_col + h))

    st_blk = pl.BlockSpec((nb, hh, HG_K, HG_V), lambda h, b, t: (b, h, 0, 0))
    out, sf = pl.pallas_call(
        functools.partial(_hgrn_kernel, nb=nb, hh=hh, chunk=chunk),
        grid=(HG_HEADS // hh, bsz // nb, seq // chunk),
        in_specs=[col(1), col(2), col(3), col(4), st_blk,
                  pl.BlockSpec((1, wid), lambda h, b, t: (0, h)),
                  pl.BlockSpec((1, LANES), lambda h, b, t: (0, 0))],
        out_specs=[pl.BlockSpec((nb, chunk, wid), lambda h, b, t: (b, t, h)), st_blk],
        out_shape=[jax.ShapeDtypeStruct((bsz, seq, HG_HEADS * HG_V), BF16),
                   jax.ShapeDtypeStruct((bsz, HG_HEADS, HG_K, HG_V), F32)],
        scratch_shapes=[pltpu.VMEM((hh * nb, HG_V, HG_K), F32)],
        compiler_params=_params("parallel", "parallel", "arbitrary"),
        name="hgrn2",
    )(z3, z3, z3, z3, s0, lb.reshape(1, HG_HEADS * HG_K), norm_w.reshape(1, HG_V))
    return out, sf


RW_MIXES = 6


def _norm_mix_kernel(h_ref, sh_ref, w_ref, mu_ref, *refs, tl):
    o_refs = refs[:RW_MIXES]
    last_ref, scr = refs[RW_MIXES:]
    x = h_ref[...]
    xn = x * lax.rsqrt(jnp.mean(x * x, axis=-1, keepdims=True) + EPS) * w_ref[...][None]

    @pl.when(pl.program_id(1) == 0)
    def _():
        scr[:, SUBLANES - 1:SUBLANES, :] = sh_ref[...]

    scr[:, SUBLANES:, :] = xn
    xx = scr[:, SUBLANES - 1:SUBLANES - 1 + tl, :] - xn
    for j, o_ref in enumerate(o_refs):
        o_ref[...] = (xn + xx * mu_ref[j:j + 1, :][None]).astype(o_ref.dtype)
    last = xn[:, tl - 1:tl, :]
    scr[:, SUBLANES - 1:SUBLANES, :] = last
    last_ref[...] = last


def norm_mix(h3, shift0, ln_w, mu, nb, tl):
    bsz, seq, d = h3.shape
    assert bsz % nb == 0 and seq % tl == 0 and tl % SUBLANES == 0
    blk = pl.BlockSpec((nb, tl, d), lambda b, t: (b, t, 0))
    row = pl.BlockSpec((nb, 1, d), lambda b, t: (b, 0, 0))
    outs = pl.pallas_call(
        functools.partial(_norm_mix_kernel, tl=tl),
        grid=(bsz // nb, seq // tl),
        in_specs=[blk, row, pl.BlockSpec((1, d), lambda b, t: (0, 0)), pl.BlockSpec((RW_MIXES, d), lambda b, t: (0, 0))],
        out_specs=[blk] * RW_MIXES + [row],
        out_shape=[jax.ShapeDtypeStruct((bsz, seq, d), BF16)] * RW_MIXES + [jax.ShapeDtypeStruct((bsz, 1, d), F32)],
        scratch_shapes=[pltpu.VMEM((nb, SUBLANES + tl, d), F32)],
        compiler_params=_params("parallel", "arbitrary"),
        name="norm_mix",
    )(h3, shift0.reshape(bsz, 1, d), ln_w.reshape(1, d), mu)
    return outs[:RW_MIXES], outs[RW_MIXES].reshape(bsz, d)


RW_PAIR = LANES // RW_HEAD
RW_SOLVE_BLOCK = 8


def _rwkv_kernel(r_ref, k_ref, v_ref, wl_ref, al_ref, g_ref, s0_ref,
                 w0_ref, a0_ref, kk_ref, ka_ref, rk_ref, lnw_ref, lnb_ref,
                 o_ref, sf_ref, s_scr, *, nb, hp, chunk):
    nu = hp * nb
    step = pl.program_id(2)
    lane = lax.broadcasted_iota(jnp.int32, (1, 1, LANES), 2)
    head1 = lane >= RW_HEAD

    def units(x):
        return jnp.concatenate([x[:, :, p * LANES:(p + 1) * LANES] for p in range(hp)], axis=0)

    def unit_rows(ref):
        return units(jnp.broadcast_to(ref[...][None], (nb, 1, hp * LANES)))

    w0, a0, k_k, k_a, r_k, ln_w, ln_b = (unit_rows(p) for p in
                                         (w0_ref, a0_ref, kk_ref, ka_ref, rk_ref, lnw_ref, lnb_ref))
    sq_row = lax.broadcasted_iota(jnp.int32, (LANES, LANES), 0) >= RW_HEAD
    sq_col = lax.broadcasted_iota(jnp.int32, (LANES, LANES), 1) >= RW_HEAD
    same_head = sq_row == sq_col
    ones_bd = same_head.astype(BF16)

    @pl.when(step == 0)
    def _():
        zero = jnp.zeros((nb, RW_HEAD, RW_HEAD), F32)
        for p in range(hp):
            top = jnp.concatenate([s0_ref[:, RW_PAIR * p], zero], axis=-1)
            bot = jnp.concatenate([zero, s0_ref[:, RW_PAIR * p + 1]], axis=-1)
            s_scr[p * nb:(p + 1) * nb] = jnp.concatenate([top, bot], axis=1)

    def bdot(spec, a, b):
        return jnp.einsum(spec, a.astype(BF16), b.astype(BF16), preferred_element_type=F32)

    def head_sum(x):
        x2 = x.reshape(nu * chunk, LANES)
        hi = x2.astype(BF16)
        lo = (x2 - hi.astype(F32)).astype(BF16)
        s = jnp.dot(hi, ones_bd, preferred_element_type=F32) + jnp.dot(lo, ones_bd, preferred_element_type=F32)
        return s.reshape(nu, chunk, LANES)

    def stack_heads(x):
        return jnp.concatenate([jnp.where(head1, 0.0, x), jnp.where(head1, x, 0.0)], axis=1).astype(BF16)

    r, k, v, wl, al, g = (units(ref[...]) for ref in (r_ref, k_ref, v_ref, wl_ref, al_ref, g_ref))
    lw = -jnp.exp(-jax.nn.softplus(-(w0 + wl)) - 0.5)
    ag = jax.nn.sigmoid(a0 + al)
    kk = k * k_k
    kk = kk * (1.0 / jnp.maximum(jnp.sqrt(head_sum(kk * kk)), 1e-12))
    k2 = k * (1.0 + (ag - 1.0) * k_a)
    cl = _cumsum_rows(lw, chunk)
    e_pos = jnp.exp(cl)
    e_neg = jnp.exp(-cl)
    at = (-kk) * jnp.exp(cl - lw)
    bt = (kk * ag) * e_neg
    kt = k2 * e_neg
    rt = r * e_pos
    wc = e_pos[:, chunk - 1:chunk, :]
    trow = lax.broadcasted_iota(jnp.int32, (chunk, RW_PAIR * chunk), 0)
    tcol = lax.broadcasted_iota(jnp.int32, (chunk, RW_PAIR * chunk), 1)
    tcol = jnp.where(tcol >= chunk, tcol - chunk, tcol)
    strict = (tcol < trow)[None]
    incl = (tcol <= trow)[None]
    x2 = jnp.concatenate([at, rt], axis=1)
    pb = bdot("utc,usc->uts", x2, stack_heads(bt))
    pk = bdot("utc,usc->uts", x2, stack_heads(kt))
    lab = jnp.where(strict, pb[:, :chunk], 0.0)
    lak = jnp.where(strict, pk[:, :chunk], 0.0)
    arb = jnp.where(incl, pb[:, chunk:], 0.0)
    ark = jnp.where(incl, pk[:, chunk:], 0.0)
    v_bd = stack_heads(v)
    xa = at
    xv = bdot("uts,usc->utc", lak, v_bd)
    sub = min(RW_SOLVE_BLOCK, chunk)
    done_a, done_v = [], []
    for lo in range(0, chunk, sub):
        xa_i = xa[:, lo:lo + sub, :]
        xv_i = xv[:, lo:lo + sub, :]
        if lo:
            pad = jnp.zeros((nu, chunk - lo, LANES), F32)
            prev = jnp.concatenate([stack_heads(jnp.concatenate(done_a + [pad], axis=1)),
                                    stack_heads(jnp.concatenate(done_v + [pad], axis=1))], axis=-1)
            upd = bdot("uts,usc->utc", lab[:, lo:lo + sub, :], prev)
            xa_i = xa_i + upd[:, :, :LANES]
            xv_i = xv_i + upd[:, :, LANES:]
        l0 = lab[:, lo:lo + sub, lo:lo + sub]
        l1 = lab[:, lo:lo + sub, chunk + lo:chunk + lo + sub]
        for s in range(sub - 1):
            m = jnp.where(head1, l1[:, :, s:s + 1], l0[:, :, s:s + 1])
            xa_i = xa_i + m * xa_i[:, s:s + 1, :]
            xv_i = xv_i + m * xv_i[:, s:s + 1, :]
        done_a.append(xa_i)
        done_v.append(xv_i)
    ah = jnp.concatenate(done_a, axis=1)
    vh = jnp.concatenate(done_v, axis=1)
    both = bdot("uts,usc->utc", arb, jnp.concatenate([stack_heads(ah), stack_heads(vh)], axis=-1))
    rh = rt + both[:, :, :LANES]
    yh = both[:, :, LANES:] + bdot("uts,usc->utc", ark, v_bd)
    gp = jnp.where(same_head, bdot("utj,utk->ujk", ah, bt), 0.0)
    ht = jnp.where(same_head, bdot("utv,utk->uvk", jnp.concatenate([vh, v], axis=1),
                                   jnp.concatenate([bt, kt], axis=1)), 0.0)
    st = s_scr[...]
    y = bdot("utk,uvk->utv", rh, st) + yh
    st = (st + bdot("uvj,ujk->uvk", st, gp) + ht) * wc
    s_scr[...] = st
    inv_n = 1.0 / RW_HEAD
    yc = y - head_sum(y) * inv_n
    var = head_sum(yc * yc) * inv_n
    y = yc * lax.rsqrt(var + RW_GN_EPS) * ln_w + ln_b
    y = y + head_sum(r * k2 * r_k) * v
    out = (y * g).astype(o_ref.dtype)
    for p in range(hp):
        o_ref[:, :, p * LANES:(p + 1) * LANES] = out[p * nb:(p + 1) * nb]

    @pl.when(step == pl.num_programs(2) - 1)
    def _():
        for p in range(hp):
            sf_ref[:, RW_PAIR * p] = st[p * nb:(p + 1) * nb, :RW_HEAD, :RW_HEAD]
            sf_ref[:, RW_PAIR * p + 1] = st[p * nb:(p + 1) * nb, RW_HEAD:, RW_HEAD:]


def _rwkv_chunk(seq):
    for c in (48, 32, 16, 8):
        if seq % c == 0:
            return c
    raise ValueError(seq)


def rwkv7(r, k, v, wl, al, g, s0, w0, a0, k_k, k_a, r_k, ln_w, ln_b, nb, hp):
    bsz, seq, d = r.shape
    chunk = _rwkv_chunk(seq)
    heads = hp * RW_PAIR
    assert bsz % nb == 0 and RW_HEADS % heads == 0
    seq_blk = pl.BlockSpec((nb, chunk, hp * LANES), lambda h, b, t: (b, t, h))
    st_blk = pl.BlockSpec((nb, heads, RW_HEAD, RW_HEAD), lambda h, b, t: (b, h, 0, 0))
    vec = pl.BlockSpec((1, hp * LANES), lambda h, b, t: (0, h))
    out, sf = pl.pallas_call(
        functools.partial(_rwkv_kernel, nb=nb, hp=hp, chunk=chunk),
        grid=(RW_HEADS // heads, bsz // nb, seq // chunk),
        in_specs=[seq_blk] * 6 + [st_blk] + [vec] * 7,
        out_specs=[seq_blk, st_blk],
        out_shape=[jax.ShapeDtypeStruct((bsz, seq, d), BF16),
                   jax.ShapeDtypeStruct((bsz, RW_HEADS, RW_HEAD, RW_HEAD), F32)],
        scratch_shapes=[pltpu.VMEM((hp * nb, LANES, LANES), F32)],
        compiler_params=_params("parallel", "parallel", "arbitrary"),
        name="rwkv7",
    )(r, k, v, wl, al, g, s0, *(p.reshape(1, d) for p in (w0, a0, k_k, k_a, r_k, ln_w, ln_b)))
    return out, sf


def _ffn_in_kernel(x_ref, wa_ref, wv_ref, e_ref, cw_ref, cb_ref, o_ref, st_ref, scr, *, nb, seq, sb, sr):
    tn = wa_ref.shape[1]
    cw = cw_ref[...]
    cb = cb_ref[...][None]
    scr[:, 0:SUBLANES, :] = e_ref[...]
    for b0 in range(0, nb, sb):
        for r0 in range(0, seq, sr):
            lo = b0 * seq + r0
            x = x_ref[lo:lo + sb * sr, :]
            a = jnp.dot(x, wa_ref[...], preferred_element_type=F32).reshape(sb, sr, tn)
            v = jnp.dot(x, wv_ref[...], preferred_element_type=F32).reshape(sb, sr, tn)
            scr[b0:b0 + sb, SUBLANES + r0:SUBLANES + r0 + sr, :] = a
            c = cb + cw[CONV_W - 1:CONV_W][None] * a
            for j in range(CONV_W - 1):
                first = SUBLANES + r0 - (CONV_W - 1 - j)
                c = c + cw[j:j + 1][None] * scr[b0:b0 + sb, first:first + sr, :]
            o_ref[lo:lo + sb * sr, :] = (jax.nn.gelu(c) * v).reshape(sb * sr, tn).astype(o_ref.dtype)
    st_ref[...] = scr[:, SUBLANES + seq - (CONV_W - 1):SUBLANES + seq, :]


def ffn_in(xb, conv0, w_in, layer, conv_w, conv_b, bsz, seq, nb, tn, sub):
    t, d = xb.shape
    sb, sr = sub
    assert t == bsz * seq and bsz % nb == 0 and D_FF % tn == 0
    assert nb % sb == 0 and seq % sr == 0 and sr % SUBLANES == 0 and (sr == seq or nb == sb == 1)
    halo = jnp.pad(conv0, ((0, 0), (SUBLANES - (CONV_W - 1), 0), (0, 0)))
    col = lambda i, j: (0, j)
    n_col = D_FF // tn
    out, st = pl.pallas_call(
        functools.partial(_ffn_in_kernel, nb=nb, seq=seq, sb=sb, sr=sr),
        grid=(bsz // nb, n_col),
        in_specs=[pl.BlockSpec((nb * seq, d), lambda i, j: (i, 0)),
                  pl.BlockSpec((None, d, tn), lambda i, j: (layer, 0, j)),
                  pl.BlockSpec((None, d, tn), lambda i, j: (layer, 0, j + n_col)),
                  pl.BlockSpec((nb, SUBLANES, tn), lambda i, j: (i, 0, j)),
                  pl.BlockSpec((CONV_W, tn), col), pl.BlockSpec((1, tn), col)],
        out_specs=[pl.BlockSpec((nb * seq, tn), lambda i, j: (i, j)),
                   pl.BlockSpec((nb, CONV_W - 1, tn), lambda i, j: (i, 0, j))],
        out_shape=[jax.ShapeDtypeStruct((t, D_FF), BF16),
                   jax.ShapeDtypeStruct((bsz, CONV_W - 1, D_FF), F32)],
        scratch_shapes=[pltpu.VMEM((nb, SUBLANES + seq, tn), F32)],
        compiler_params=_params("parallel", "parallel"),
        name="ffn_in",
    )(xb, w_in, w_in, halo, conv_w, conv_b.reshape(1, D_FF))
    return out, st


def _channel_mixer(h, conv0, layer, p, cfg, bsz, seq):
    (xb,) = rmsnorm(h, p["ln_ffn"][layer], (BF16,))
    gated, n_cv = ffn_in(xb, conv0, p["ffn_w_in"], layer, p["ffn_conv_w"][layer], p["ffn_conv_b"][layer],
                         bsz, seq, cfg["ffn_nb"], cfg["ffn_tn"], cfg["ffn_sub"])
    return matmul([gated], [(p["ffn_w_down"], layer, 0)], F32, epilogue="residual", extra=h), n_cv


def _trunk(x3, s5r, s5i, hg, rw, sh, cv, p, cfg):
    bsz, seq, d = x3.shape
    seq += cfg["front"]
    t = bsz * seq

    if cfg["front"]:
        h, xb = embed_norm(x3, p["meta"], p["ln_mix"][0])
    else:
        h = x3.reshape(t, d)
        (xb,) = rmsnorm(h, p["ln_mix"][0], (BF16,))
    z = matmul([xb], [p["ev_w_in"]], F32).reshape(bsz, seq, EVEN_IN)
    if cfg["s5_tl"]:
        ys5, n_s5r, n_s5i = s5_scan_long(z, s5r[0], s5i[0], *p["s5"], tl=cfg["s5_tl"])
    else:
        ys5, n_s5r, n_s5i = s5_scan(z, s5r[0], s5i[0], *p["s5"], nb=cfg["s5_nb"])
    ys5 = ys5.reshape(t, S5_WIDTH)
    ya = matmul([ys5.astype(BF16)], [p["s5_w_glu"]], BF16, epilogue="glu", extra=ys5)
    yb, n_hg = hgrn2(z, hg[0], p["hg_lb"], p["hg_norm_w"], nb=cfg["hg_nb"], hh=cfg["hg_hh"])
    h = matmul([ya, yb.reshape(t, -1)], [(p["ev_w_out"], 0, 0), (p["ev_w_out"], 0, 1)], F32,
               epilogue="residual", extra=h)
    h, n_cv0 = _channel_mixer(h, cv[0], 0, p, cfg, bsz, seq)

    mixes, n_sh = norm_mix(h.reshape(bsz, seq, d), sh[0], p["ln_mix"][1], p["rw_mu"], *cfg["mix_blk"])
    xr, xw, xk, xv, xa, xg = (m.reshape(t, d) for m in mixes)
    r = matmul([xr], [p["rw_w_r"]], F32)
    k = matmul([xk], [p["rw_w_k"]], F32)
    v = matmul([xv], [p["rw_w_v"]], F32)
    wl = matmul([matmul([xw], [p["rw_w1"]], BF16, act="tanh")], [p["rw_w2"]], F32)
    al = matmul([matmul([xa], [p["rw_a1"]], BF16)], [p["rw_a2"]], F32)
    g = matmul([matmul([xg], [p["rw_g1"]], BF16, act="sigmoid")], [p["rw_g2"]], F32)
    as3 = lambda a: a.reshape(bsz, seq, d)
    yo, n_rw = rwkv7(as3(r), as3(k), as3(v), as3(wl), as3(al), as3(g), rw[0], *p["rw_vec"],
                     nb=cfg["rw_nb"], hp=cfg["rw_hp"])
    h = matmul([yo.reshape(t, d)], [p["rw_w_o"]], F32, epilogue="residual", extra=h)
    h, n_cv1 = _channel_mixer(h, cv[1], 1, p, cfg, bsz, seq)

    (y,) = rmsnorm(h, p["ln_final"], (F32,), rows=(bsz, seq, cfg["front"]) if cfg["front"] else None)
    return (y.reshape(bsz, seq - cfg["front"], d), n_s5r[None], n_s5i[None], n_hg[None], n_rw[None], n_sh[None],
            jnp.stack([n_cv0, n_cv1]))


PROMPT_CFG = dict(front=N_META, s5_tl=344, s5_nb=None, hg_nb=4, hg_hh=8, mix_blk=(1, 344), rw_nb=4, rw_hp=8, ffn_nb=1, ffn_tn=512, ffn_sub=(1, 688))
SAMPLE_CFG = dict(front=0, s5_tl=None, s5_nb=32, hg_nb=4, hg_hh=8, mix_blk=(32, 8), rw_nb=16, rw_hp=2, ffn_nb=128, ffn_tn=512, ffn_sub=(32, 8))


def kernel(x_prompt, x_sample, state_s5_re, state_s5_im, state_hgrn, state_rwkv, state_shift, state_conv, meta_tokens, ln_mix, ln_ffn, ln_final, ev_w_in, ev_w_out, s5_lam_re, s5_lam_im, s5_log_step, s5_b_re, s5_b_im, s5_c_re, s5_c_im, s5_d, s5_w_glu, hg_lb, hg_norm_w, rw_mu, rw_w0, rw_w1, rw_w2, rw_a0, rw_a1, rw_a2, rw_g1, rw_g2, rw_k_k, rw_k_a, rw_r_k, rw_w_r, rw_w_k, rw_w_v, rw_w_o, rw_ln_w, rw_ln_b, ffn_w_in, ffn_conv_w, ffn_conv_b, ffn_w_down):
    bf = lambda w: w.astype(BF16)
    lb_all = hg_lower_bounds(hg_lb)
    pwr, pwi, bbr_t, bbi_t = s5_prep(s5_lam_re[0], s5_lam_im[0], s5_log_step[0], s5_b_re[0], s5_b_im[0])
    wbr, wbi, wcr, wci = _s5_block_weights(bbr_t, bbi_t, s5_c_re[0], s5_c_im[0])
    p = {
        "meta": meta_tokens, "ln_mix": ln_mix, "ln_ffn": ln_ffn, "ln_final": ln_final,
        "ev_w_in": bf(ev_w_in[0]),
        "ev_w_out": bf(ev_w_out),
        "s5": (pwr, pwi, wbr, wbi, wcr, wci, s5_d[0].reshape(1, S5_WIDTH)),
        "s5_w_glu": bf(s5_w_glu[0]),
        "hg_lb": lb_all[0], "hg_norm_w": hg_norm_w[0],
        "rw_mu": rw_mu[0],
        "rw_w1": bf(rw_w1[0]), "rw_w2": bf(rw_w2[0]), "rw_a1": bf(rw_a1[0]), "rw_a2": bf(rw_a2[0]),
        "rw_g1": bf(rw_g1[0]), "rw_g2": bf(rw_g2[0]),
        "rw_w_r": bf(rw_w_r[0]), "rw_w_k": bf(rw_w_k[0]), "rw_w_v": bf(rw_w_v[0]), "rw_w_o": bf(rw_w_o[0]),
        "rw_vec": (rw_w0[0], rw_a0[0], rw_k_k[0], rw_k_a[0], rw_r_k[0].reshape(D_MODEL), rw_ln_w[0], rw_ln_b[0]),
        "ffn_w_in": bf(ffn_w_in), "ffn_conv_w": ffn_conv_w, "ffn_conv_b": ffn_conv_b,
        "ffn_w_down": bf(ffn_w_down),
    }

    bsz = x_prompt.shape[0]
    zeros = lambda *s: jnp.zeros(s, F32)
    outs_p = _trunk(x_prompt,
                    zeros(1, bsz, S5_GROUPS, S5_STATE), zeros(1, bsz, S5_GROUPS, S5_STATE),
                    zeros(1, bsz, HG_HEADS, HG_K, HG_V), zeros(1, bsz, RW_HEADS, RW_HEAD, RW_HEAD),
                    zeros(1, bsz, D_MODEL), zeros(2, bsz, CONV_W - 1, D_FF), p, PROMPT_CFG)
    outs_s = _trunk(x_sample, state_s5_re, state_s5_im, state_hgrn, state_rwkv, state_shift, state_conv,
                    p, SAMPLE_CFG)
    return tuple(outs_p[:1]) + tuple(outs_s[:1]) + tuple(outs_p[1:]) + tuple(outs_s[1:])
```

```python
import functools
import math

import jax
import jax.numpy as jnp
from jax import lax
from jax.experimental import pallas as pl
from jax.experimental.pallas import tpu as pltpu

F32 = jnp.float32
BF16 = jnp.bfloat16

D_MODEL = 2048
N_META = 16
EPS = 1e-6
S5_WIDTH = 1024
S5_GROUP = 16
S5_GROUPS = 64
S5_STATE = 64
S5_CH = S5_GROUPS * S5_STATE
HG_HEADS = 8
HG_K = 128
HG_V = 128
HG_CHUNK = 16
EVEN_IN = 5120
RW_HEAD = 64
RW_HEADS = 32
RW_GN_EPS = 64e-5
D_FF = 5632
CONV_W = 3

LANES = 128
SUBLANES = 8
VMEM_LIMIT = 56 * 1024 * 1024


def _params(*sem):
    return pltpu.CompilerParams(dimension_semantics=sem, vmem_limit_bytes=VMEM_LIMIT)


def _row_tile(t, cap=1024):
    best = None
    for d in range(16, min(t, cap) + 1, 16):
        if t % d == 0:
            best = d
    assert best is not None, t
    return best


def _rms_kernel(x_ref, w_ref, *o_refs):
    x = x_ref[...]
    y = x * lax.rsqrt(jnp.mean(x * x, axis=-1, keepdims=True) + EPS) * w_ref[...]
    for o_ref in o_refs:
        o_ref[...] = y.astype(o_ref.dtype)


def rmsnorm(x, w, dtypes, rows=None):
    t, d = x.shape
    if rows is None:
        tm = _row_tile(t)
        n_out = t
        grid = (t // tm,)
        in_spec = pl.BlockSpec((tm, d), lambda i: (i, 0))
        out_spec = in_spec
        vec = pl.BlockSpec((1, d), lambda i: (0, 0))
    else:
        bsz, seq, front = rows
        keep = seq - front
        assert t == bsz * seq and front % 16 == 0
        tm = _row_tile(keep)
        per = keep // tm
        n_out = bsz * keep
        grid = (bsz, per)
        in_spec = pl.BlockSpec((pl.Element(tm), pl.Element(d)),
                               lambda b, i: (pl.multiple_of(b * seq + front + i * tm, 16), 0))
        out_spec = pl.BlockSpec((tm, d), lambda b, i: (b * per + i, 0))
        vec = pl.BlockSpec((1, d), lambda b, i: (0, 0))
    outs = pl.pallas_call(
        _rms_kernel,
        grid=grid,
        in_specs=[in_spec, vec],
        out_specs=[out_spec for _ in dtypes],
        out_shape=[jax.ShapeDtypeStruct((n_out, d), dt) for dt in dtypes],
        compiler_params=_params(*(["parallel"] * len(grid))),
        name="rmsnorm",
    )(x, w.reshape(1, d))
    return outs


def _embed_norm_kernel(x_ref, m_ref, w_ref, h_ref, xb_ref, *, front):
    def emit(rows):
        h_ref[...] = rows
        y = rows * lax.rsqrt(jnp.mean(rows * rows, axis=-1, keepdims=True) + EPS) * w_ref[...]
        xb_ref[...] = y.astype(xb_ref.dtype)

    @pl.when(pl.program_id(1) == 0)
    def _():
        emit(jnp.concatenate([m_ref[...], x_ref[:x_ref.shape[0] - front, :]], axis=0))

    @pl.when(pl.program_id(1) > 0)
    def _():
        emit(x_ref[...])


def embed_norm(x3, meta, w):
    bsz, seq, d = x3.shape
    front = meta.shape[0]
    total = seq + front
    tm = _row_tile(total)
    per = total // tm
    assert front % 16 == 0 and tm > front
    blk = pl.BlockSpec((tm, d), lambda b, j: (b * per + j, 0))
    return pl.pallas_call(
        functools.partial(_embed_norm_kernel, front=front),
        grid=(bsz, per),
        in_specs=[pl.BlockSpec((pl.Element(tm), pl.Element(d)),
                               lambda b, j: (pl.multiple_of(b * seq + jnp.maximum(j * tm - front, 0), 16), 0)),
                  pl.BlockSpec((front, d), lambda b, j: (0, 0)),
                  pl.BlockSpec((1, d), lambda b, j: (0, 0))],
        out_specs=[blk, blk],
        out_shape=[jax.ShapeDtypeStruct((bsz * total, d), F32), jax.ShapeDtypeStruct((bsz * total, d), BF16)],
        compiler_params=_params("parallel", "parallel"),
        name="embed_norm",
    )(x3.reshape(bsz * seq, d), meta, w.reshape(1, d))


def _act(x, act):
    if act == "tanh":
        return jnp.tanh(x)
    if act == "sigmoid":
        return jax.nn.sigmoid(x)
    assert act is None
    return x


def _mm_kernel(*refs, n_a, act, epilogue):
    a_refs = refs[:n_a]
    w_refs = refs[n_a:2 * n_a]
    rest = refs[2 * n_a:]
    o_ref = rest[-1]
    acc = jnp.dot(a_refs[0][...], w_refs[0][...], preferred_element_type=F32)
    for a_ref, w_ref in zip(a_refs[1:], w_refs[1:]):
        acc = acc + jnp.dot(a_ref[...], w_ref[...], preferred_element_type=F32)
    acc = _act(acc, act)
    if epilogue == "residual":
        acc = rest[0][...] + acc
    elif epilogue == "glu":
        acc = rest[0][...] * jax.nn.sigmoid(acc)
    o_ref[...] = acc.astype(o_ref.dtype)


MM_VMEM_BUDGET = 40 * 1024 * 1024
MXU_WIDTH = 256


def _mm_tiles(t, k_total, n, out_bytes, has_extra):
    rows = [d for d in range(16, t + 1, 16) if t % d == 0]
    cols = [d for d in range(LANES, n + 1, LANES) if n % d == 0] or [n]
    best, best_score = None, -1.0
    for tm in rows:
        for tn in cols:
            est = 4 * tm * k_total + 4 * k_total * tn + tm * tn * (2 * out_bytes + 4 + (8 if has_extra else 0))
            if est > MM_VMEM_BUDGET:
                continue
            score = tm * tn * (1.0 if tn % MXU_WIDTH == 0 else 0.8)
            if score > best_score:
                best, best_score = (tm, tn), score
    assert best is not None, (t, k_total, n)
    return best


def matmul(a_list, w_list, out_dtype, act=None, epilogue=None, extra=None):
    t = a_list[0].shape[0]
    n = w_list[0][0].shape[2] if isinstance(w_list[0], tuple) else w_list[0].shape[1]
    tm, tn = _mm_tiles(t, sum(a.shape[1] for a in a_list), n, jnp.dtype(out_dtype).itemsize, epilogue is not None)
    in_specs = [pl.BlockSpec((tm, a.shape[1]), lambda i, j: (i, 0)) for a in a_list]
    args = list(a_list)
    for a, w in zip(a_list, w_list):
        if isinstance(w, tuple):
            w, layer, kblk = w
            in_specs.append(pl.BlockSpec((None, a.shape[1], tn), lambda i, j, layer=layer, kblk=kblk: (layer, kblk, j)))
        else:
            in_specs.append(pl.BlockSpec((w.shape[0], tn), lambda i, j: (0, j)))
        args.append(w)
    if epilogue is not None:
        in_specs.append(pl.BlockSpec((tm, tn), lambda i, j: (i, j)))
        args.append(extra)
    return pl.pallas_call(
        functools.partial(_mm_kernel, n_a=len(a_list), act=act, epilogue=epilogue),
        grid=(t // tm, n // tn),
        in_specs=in_specs,
        out_specs=pl.BlockSpec((tm, tn), lambda i, j: (i, j)),
        out_shape=jax.ShapeDtypeStruct((t, n), out_dtype),
        compiler_params=_params("parallel", "parallel"),
        name="matmul",
    )(*args)


def _s5_prep_kernel(lr_ref, li_ref, ls_ref, brt_ref, bit_ref, pwr_ref, pwi_ref, bbr_ref, bbi_ref):
    lr = jnp.minimum(lr_ref[...], -1e-4)
    li = li_ref[...]
    dt = jnp.exp(ls_ref[...])
    n = lax.broadcasted_iota(jnp.int32, (SUBLANES, S5_CH), 0).astype(F32) + 1.0
    mag = jnp.exp(n * (lr * dt))
    ang = n * (li * dt)
    pwr = mag * jnp.cos(ang)
    pwi = mag * jnp.sin(ang)
    pwr_ref[...] = pwr
    pwi_ref[...] = pwi
    ar = pwr[0:1]
    ai = pwi[0:1]
    den = lr * lr + li * li
    zr = ((ar - 1.0) * lr + ai * li) / den
    zi = (ai * lr - (ar - 1.0) * li) / den
    br = brt_ref[...]
    bi = bit_ref[...]
    bbr_ref[...] = zr * br - zi * bi
    bbi_ref[...] = zr * bi + zi * br


def s5_prep(lam_re, lam_im, log_step, b_re, b_im):
    lr = lam_re.reshape(1, S5_CH)
    li = lam_im.reshape(1, S5_CH)
    ls = jnp.broadcast_to(log_step[:, None], (S5_GROUPS, S5_STATE)).reshape(1, S5_CH)
    brt = b_re.reshape(S5_CH, S5_GROUP).T
    bit = b_im.reshape(S5_CH, S5_GROUP).T
    return pl.pallas_call(
        _s5_prep_kernel,
        out_shape=[jax.ShapeDtypeStruct((SUBLANES, S5_CH), F32)] * 2
        + [jax.ShapeDtypeStruct((S5_GROUP, S5_CH), F32)] * 2,
        name="s5_prep",
    )(lr, li, ls, brt, bit)


S5_BLK_GROUPS = LANES // S5_GROUP
S5_BLKS = S5_WIDTH // LANES
S5_BLK_CH = S5_BLK_GROUPS * S5_STATE


def _cmul_add(xr, xi, mr, mi, sr, si):
    return xr + mr * sr - mi * si, xi + mr * si + mi * sr


def _s5_kernel(u_ref, h0r_ref, h0i_ref, pwr_ref, pwi_ref, wbr_ref, wbi_ref, wcr_ref, wci_ref, d_ref,
               y_ref, hr_ref, hi_ref, xr_scr, xi_scr, *, nb, seq):
    u2 = u_ref[...].reshape(nb * seq, LANES)
    ub = u2.astype(BF16)
    xr_scr[...] = jnp.dot(ub, wbr_ref[0], preferred_element_type=F32).reshape(nb, seq, S5_BLK_CH)
    xi_scr[...] = jnp.dot(ub, wbi_ref[0], preferred_element_type=F32).reshape(nb, seq, S5_BLK_CH)

    pwr = pwr_ref[...]
    pwi = pwi_ref[...]
    row = lax.broadcasted_iota(jnp.int32, (SUBLANES, S5_BLK_CH), 0)
    steps = []
    for d in (1, 2, 4):
        keep = row >= d
        steps.append((d, jnp.where(keep, pwr[d - 1:d], 0.0)[None], jnp.where(keep, pwi[d - 1:d], 0.0)[None]))
    pr = pwr[None]
    pi = pwi[None]

    def tile(i, carry):
        cr, ci = carry
        o = pl.multiple_of(i * SUBLANES, SUBLANES)
        xr = xr_scr[:, pl.ds(o, SUBLANES), :]
        xi = xi_scr[:, pl.ds(o, SUBLANES), :]
        for d, mr, mi in steps:
            sr = pltpu.roll(xr, d, axis=1)
            si = pltpu.roll(xi, d, axis=1)
            xr, xi = _cmul_add(xr, xi, mr, mi, sr, si)
        xr, xi = _cmul_add(xr, xi, pr, pi, cr, ci)
        xr_scr[:, pl.ds(o, SUBLANES), :] = xr
        xi_scr[:, pl.ds(o, SUBLANES), :] = xi
        return xr[:, SUBLANES - 1:SUBLANES, :], xi[:, SUBLANES - 1:SUBLANES, :]

    hr, hi = lax.fori_loop(0, seq // SUBLANES, tile, (h0r_ref[...], h0i_ref[...]))
    hr_ref[...] = hr
    hi_ref[...] = hi

    xr = xr_scr[...].reshape(nb * seq, S5_BLK_CH).astype(BF16)
    xi = xi_scr[...].reshape(nb * seq, S5_BLK_CH).astype(BF16)
    y = (jnp.dot(xr, wcr_ref[0], preferred_element_type=F32)
         - jnp.dot(xi, wci_ref[0], preferred_element_type=F32)
         + d_ref[...] * u2)
    y_ref[...] = jax.nn.gelu(y).reshape(nb, seq, LANES)


def s5_scan(z3, h0r, h0i, pwr, pwi, wbr, wbi, wcr, wci, d, nb):
    bsz, seq, _ = z3.shape
    assert seq % SUBLANES == 0 and bsz % nb == 0
    seq_blk = pl.BlockSpec((nb, seq, LANES), lambda b, k: (b, 0, k))
    st_blk = pl.BlockSpec((nb, 1, S5_BLK_CH), lambda b, k: (b, 0, k))
    pw_blk = pl.BlockSpec((SUBLANES, S5_BLK_CH), lambda b, k: (0, k))
    wb_blk = pl.BlockSpec((1, LANES, S5_BLK_CH), lambda b, k: (k, 0, 0))
    wc_blk = pl.BlockSpec((1, S5_BLK_CH, LANES), lambda b, k: (k, 0, 0))
    y, hr, hi = pl.pallas_call(
        functools.partial(_s5_kernel, nb=nb, seq=seq),
        grid=(bsz // nb, S5_BLKS),
        in_specs=[seq_blk, st_blk, st_blk, pw_blk, pw_blk, wb_blk, wb_blk, wc_blk, wc_blk,
                  pl.BlockSpec((1, LANES), lambda b, k: (0, k))],
        out_specs=[seq_blk, st_blk, st_blk],
        out_shape=[jax.ShapeDtypeStruct((bsz, seq, S5_WIDTH), F32),
                   jax.ShapeDtypeStruct((bsz, 1, S5_CH), F32),
                   jax.ShapeDtypeStruct((bsz, 1, S5_CH), F32)],
        scratch_shapes=[pltpu.VMEM((nb, seq, S5_BLK_CH), F32), pltpu.VMEM((nb, seq, S5_BLK_CH), F32)],
        compiler_params=_params("parallel", "parallel"),
        name="s5_scan",
    )(z3, h0r.reshape(bsz, 1, S5_CH), h0i.reshape(bsz, 1, S5_CH), pwr, pwi, wbr, wbi, wcr, wci, d)
    return y, hr.reshape(bsz, S5_GROUPS, S5_STATE), hi.reshape(bsz, S5_GROUPS, S5_STATE)


S5_LANE_TILES = S5_BLK_CH // LANES


def _s5_long_kernel(u_ref, h0r_ref, h0i_ref, ar_ref, ai_ref, wbr_ref, wbi_ref, wcr_ref, wci_ref, d_ref,
                    y_ref, hr_ref, hi_ref, xr_scr, xi_scr, cr_scr, ci_scr, *, tl):
    step = pl.program_id(1)

    @pl.when(step == 0)
    def _():
        cr_scr[...] = h0r_ref[0]
        ci_scr[...] = h0i_ref[0]

    for k in range(S5_BLKS):
        ub = u_ref[0, :, k * LANES:(k + 1) * LANES].astype(BF16)
        bur = jnp.dot(ub, wbr_ref[k], preferred_element_type=F32)
        bui = jnp.dot(ub, wbi_ref[k], preferred_element_type=F32)
        for j in range(S5_LANE_TILES):
            xr_scr[j, k * tl:(k + 1) * tl, :] = bur[:, j * LANES:(j + 1) * LANES]
            xi_scr[j, k * tl:(k + 1) * tl, :] = bui[:, j * LANES:(j + 1) * LANES]

    ar = [ar_ref[:, j * LANES:(j + 1) * LANES] for j in range(S5_LANE_TILES)]
    ai = [ai_ref[:, j * LANES:(j + 1) * LANES] for j in range(S5_LANE_TILES)]

    def token(t, carry):
        cr, ci = carry
        nr, ni = [], []
        for j in range(S5_LANE_TILES):
            rows = pl.ds(t, S5_BLKS, stride=tl)
            xr, xi = _cmul_add(xr_scr[j, rows, :], xi_scr[j, rows, :], ar[j], ai[j], cr[j], ci[j])
            xr_scr[j, rows, :] = xr
            xi_scr[j, rows, :] = xi
            nr.append(xr)
            ni.append(xi)
        return tuple(nr), tuple(ni)

    init = (tuple(cr_scr[:, j * LANES:(j + 1) * LANES] for j in range(S5_LANE_TILES)),
            tuple(ci_scr[:, j * LANES:(j + 1) * LANES] for j in range(S5_LANE_TILES)))
    cr, ci = lax.fori_loop(0, tl, token, init, unroll=8)
    cr = jnp.concatenate(cr, axis=-1)
    ci = jnp.concatenate(ci, axis=-1)
    cr_scr[...] = cr
    ci_scr[...] = ci
    hr_ref[0] = cr
    hi_ref[0] = ci

    for k in range(S5_BLKS):
        xr = jnp.concatenate([xr_scr[j, k * tl:(k + 1) * tl, :] for j in range(S5_LANE_TILES)], axis=-1)
        xi = jnp.concatenate([xi_scr[j, k * tl:(k + 1) * tl, :] for j in range(S5_LANE_TILES)], axis=-1)
        u = u_ref[0, :, k * LANES:(k + 1) * LANES]
        y = (jnp.dot(xr.astype(BF16), wcr_ref[k], preferred_element_type=F32)
             - jnp.dot(xi.astype(BF16), wci_ref[k], preferred_element_type=F32)
             + d_ref[:, k * LANES:(k + 1) * LANES] * u)
        y_ref[0, :, k * LANES:(k + 1) * LANES] = jax.nn.gelu(y)


def s5_scan_long(z3, h0r, h0i, pwr, pwi, wbr, wbi, wcr, wci, d, tl):
    bsz, seq, _ = z3.shape
    assert seq % tl == 0 and tl % SUBLANES == 0
    seq_blk = pl.BlockSpec((1, tl, S5_WIDTH), lambda b, t: (b, t, 0))
    st_blk = pl.BlockSpec((1, S5_BLKS, S5_BLK_CH), lambda b, t: (b, 0, 0))
    lam_blk = pl.BlockSpec((S5_BLKS, S5_BLK_CH), lambda b, t: (0, 0))
    wb_blk = pl.BlockSpec((S5_BLKS, LANES, S5_BLK_CH), lambda b, t: (0, 0, 0))
    wc_blk = pl.BlockSpec((S5_BLKS, S5_BLK_CH, LANES), lambda b, t: (0, 0, 0))
    rows = pltpu.VMEM((S5_LANE_TILES, S5_BLKS * tl, LANES), F32)
    carry = pltpu.VMEM((S5_BLKS, S5_BLK_CH), F32)
    y, hr, hi = pl.pallas_call(
        functools.partial(_s5_long_kernel, tl=tl),
        grid=(bsz, seq // tl),
        in_specs=[seq_blk, st_blk, st_blk, lam_blk, lam_blk, wb_blk, wb_blk, wc_blk, wc_blk,
                  pl.BlockSpec((1, S5_WIDTH), lambda b, t: (0, 0))],
        out_specs=[seq_blk, st_blk, st_blk],
        out_shape=[jax.ShapeDtypeStruct((bsz, seq, S5_WIDTH), F32),
                   jax.ShapeDtypeStruct((bsz, S5_BLKS, S5_BLK_CH), F32),
                   jax.ShapeDtypeStruct((bsz, S5_BLKS, S5_BLK_CH), F32)],
        scratch_shapes=[rows, rows, carry, carry],
        compiler_params=_params("parallel", "arbitrary"),
        name="s5_scan_long",
    )(z3, h0r.reshape(bsz, S5_BLKS, S5_BLK_CH), h0i.reshape(bsz, S5_BLKS, S5_BLK_CH),
      pwr[0].reshape(S5_BLKS, S5_BLK_CH), pwi[0].reshape(S5_BLKS, S5_BLK_CH), wbr, wbi, wcr, wci, d)
    return y, hr.reshape(bsz, S5_GROUPS, S5_STATE), hi.reshape(bsz, S5_GROUPS, S5_STATE)


def _s5_block_weights(bbr_t, bbi_t, c_re, c_im):
    eye = jnp.eye(S5_BLK_GROUPS, dtype=F32)

    def wb(bt):
        b4 = bt.reshape(S5_GROUP, S5_BLKS, S5_BLK_GROUPS, S5_STATE)
        w = jnp.einsum("cbgp,hg->bhcgp", b4, eye)
        return w.reshape(S5_BLKS, LANES, S5_BLK_CH).astype(BF16)

    def wc(c):
        c4 = c.reshape(S5_BLKS, S5_BLK_GROUPS, S5_GROUP, S5_STATE)
        w = jnp.einsum("bgcp,hg->bhpgc", c4, eye)
        return w.reshape(S5_BLKS, S5_BLK_CH, LANES).astype(BF16)

    return wb(bbr_t), wb(bbi_t), wc(c_re), wc(c_im)


def _hg_lb_kernel(x_ref, o_ref):
    x = x_ref[...]
    e = jnp.exp(x - jnp.max(x, axis=0, keepdims=True))
    sm = e / jnp.sum(e, axis=0, keepdims=True)
    acc = sm[0:1]
    o_ref[0:1, :] = acc
    for l in range(1, x.shape[0]):
        acc = acc + sm[l:l + 1]
        o_ref[l:l + 1, :] = acc


def hg_lower_bounds(hg_lb):
    return pl.pallas_call(_hg_lb_kernel, out_shape=jax.ShapeDtypeStruct(hg_lb.shape, F32), name="hg_lb")(hg_lb)


def _cumsum_rows(x, n):
    row = lax.broadcasted_iota(jnp.int32, x.shape, 1)
    d = 1
    while d < n:
        x = x + jnp.where(row >= d, pltpu.roll(x, d, axis=1), 0.0)
        d *= 2
    return x


def _hgrn_kernel(q_ref, f_ref, i_ref, g_ref, s0_ref, lb_ref, nw_ref, o_ref, sf_ref, st_scr, *, nb, hh, chunk):
    step = pl.program_id(2)

    def units(x):
        return jnp.concatenate([x[:, :, h * LANES:(h + 1) * LANES] for h in range(hh)], axis=0)

    lb = units(jnp.broadcast_to(lb_ref[...][None], (nb, 1, hh * LANES)))
    nw = nw_ref[...][None]

    @pl.when(step == 0)
    def _():
        for h in range(hh):
            for b in range(nb):
                st_scr[h * nb + b] = s0_ref[b, h].T

    trow = lax.broadcasted_iota(jnp.int32, (chunk, chunk), 0)
    tcol = lax.broadcasted_iota(jnp.int32, (chunk, chunk), 1)
    causal = (tcol <= trow)[None]
    q, f, v, g = (units(ref[...]) for ref in (q_ref, f_ref, i_ref, g_ref))
    fg = lb + (1.0 - lb) * jax.nn.sigmoid(f)
    qh = jax.nn.silu(q)
    kh = 1.0 - fg
    bcum = _cumsum_rows(jnp.log(fg), chunk)
    btot = bcum[:, chunk - 1:chunk, :]
    q_in = (qh * jnp.exp(bcum)).astype(BF16)
    k_in = (kh * jnp.exp(-bcum)).astype(BF16)
    k_end = (kh * jnp.exp(btot - bcum)).astype(BF16)
    decay = jnp.exp(btot)
    vb = v.astype(BF16)
    st = st_scr[...]
    att = jnp.einsum("utk,usk->uts", q_in, k_in, preferred_element_type=F32)
    att = jnp.where(causal, att, 0.0).astype(BF16)
    out = (jnp.einsum("utk,uvk->utv", q_in, st.astype(BF16), preferred_element_type=F32)
           + jnp.einsum("uts,usv->utv", att, vb, preferred_element_type=F32))
    st = st * decay + jnp.einsum("usv,usk->uvk", vb, k_end, preferred_element_type=F32)
    st_scr[...] = st
    out = out * lax.rsqrt(jnp.mean(out * out, axis=-1, keepdims=True) + EPS) * nw
    out = (out * jax.nn.silu(g)).astype(o_ref.dtype)
    for h in range(hh):
        o_ref[:, :, h * LANES:(h + 1) * LANES] = out[h * nb:(h + 1) * nb]

    @pl.when(step == pl.num_programs(2) - 1)
    def _():
        for h in range(hh):
            for b in range(nb):
                sf_ref[b, h] = st[h * nb + b].T


def hgrn2(z3, s0, lb, norm_w, nb, hh):
    bsz, seq, _ = z3.shape
    chunk = min(HG_CHUNK, seq)
    assert seq % chunk == 0 and bsz % nb == 0 and HG_HEADS % hh == 0
    wid = hh * LANES
    n_col = (HG_HEADS * HG_K) // wid

    def col(proj):
        return pl.BlockSpec((nb, chunk, wid), lambda h, b, t, proj=proj: (b, t, proj * n_col + h))

    st_blk = pl.BlockSpec((nb, hh, HG_K, HG_V), lambda h, b, t: (b, h, 0, 0))
    out, sf = pl.pallas_call(
        functools.partial(_hgrn_kernel, nb=nb, hh=hh, chunk=chunk),
        grid=(HG_HEADS // hh, bsz // nb, seq // chunk),
        in_specs=[col(1), col(2), col(3), col(4), st_blk,
                  pl.BlockSpec((1, wid), lambda h, b, t: (0, h)),
                  pl.BlockSpec((1, LANES), lambda h, b, t: (0, 0))],
        out_specs=[pl.BlockSpec((nb, chunk, wid), lambda h, b, t: (b, t, h)), st_blk],
        out_shape=[jax.ShapeDtypeStruct((bsz, seq, HG_HEADS * HG_V), BF16),
                   jax.ShapeDtypeStruct((bsz, HG_HEADS, HG_K, HG_V), F32)],
        scratch_shapes=[pltpu.VMEM((hh * nb, HG_V, HG_K), F32)],
        compiler_params=_params("parallel", "parallel", "arbitrary"),
        name="hgrn2",
    )(z3, z3, z3, z3, s0, lb.reshape(1, HG_HEADS * HG_K), norm_w.reshape(1, HG_V))
    return out, sf


RW_MIXES = 6


def _norm_mix_kernel(h_ref, sh_ref, w_ref, mu_ref, *refs, tl):
    o_refs = refs[:RW_MIXES]
    last_ref, scr = refs[RW_MIXES:]
    x = h_ref[...]
    xn = x * lax.rsqrt(jnp.mean(x * x, axis=-1, keepdims=True) + EPS) * w_ref[...][None]

    @pl.when(pl.program_id(1) == 0)
    def _():
        scr[:, SUBLANES - 1:SUBLANES, :] = sh_ref[...]

    scr[:, SUBLANES:, :] = xn
    xx = scr[:, SUBLANES - 1:SUBLANES - 1 + tl, :] - xn
    for j, o_ref in enumerate(o_refs):
        o_ref[...] = (xn + xx * mu_ref[j:j + 1, :][None]).astype(o_ref.dtype)
    last = xn[:, tl - 1:tl, :]
    scr[:, SUBLANES - 1:SUBLANES, :] = last
    last_ref[...] = last


def norm_mix(h3, shift0, ln_w, mu, nb, tl):
    bsz, seq, d = h3.shape
    assert bsz % nb == 0 and seq % tl == 0 and tl % SUBLANES == 0
    blk = pl.BlockSpec((nb, tl, d), lambda b, t: (b, t, 0))
    row = pl.BlockSpec((nb, 1, d), lambda b, t: (b, 0, 0))
    outs = pl.pallas_call(
        functools.partial(_norm_mix_kernel, tl=tl),
        grid=(bsz // nb, seq // tl),
        in_specs=[blk, row, pl.BlockSpec((1, d), lambda b, t: (0, 0)), pl.BlockSpec((RW_MIXES, d), lambda b, t: (0, 0))],
        out_specs=[blk] * RW_MIXES + [row],
        out_shape=[jax.ShapeDtypeStruct((bsz, seq, d), BF16)] * RW_MIXES + [jax.ShapeDtypeStruct((bsz, 1, d), F32)],
        scratch_shapes=[pltpu.VMEM((nb, SUBLANES + tl, d), F32)],
        compiler_params=_params("parallel", "arbitrary"),
        name="norm_mix",
    )(h3, shift0.reshape(bsz, 1, d), ln_w.reshape(1, d), mu)
    return outs[:RW_MIXES], outs[RW_MIXES].reshape(bsz, d)


RW_PAIR = LANES // RW_HEAD
RW_DECAY_SCALE = math.exp(-0.5)
RW_SOLVE_BLOCK = 8


def _rwkv_kernel(r_ref, k_ref, v_ref, wl_ref, al_ref, g_ref, s0_ref,
                 w0_ref, a0_ref, kk_ref, ka_ref, rk_ref, lnw_ref, lnb_ref,
                 o_ref, sf_ref, s_scr, *, nb, hp, chunk):
    nu = hp * nb
    step = pl.program_id(2)
    lane = lax.broadcasted_iota(jnp.int32, (1, 1, LANES), 2)
    head1 = lane >= RW_HEAD

    def units(x):
        return jnp.concatenate([x[:, :, p * LANES:(p + 1) * LANES] for p in range(hp)], axis=0)

    def unit_rows(ref):
        return units(jnp.broadcast_to(ref[...][None], (nb, 1, hp * LANES)))

    w0, a0, k_k, k_a, r_k, ln_w, ln_b = (unit_rows(p) for p in
                                         (w0_ref, a0_ref, kk_ref, ka_ref, rk_ref, lnw_ref, lnb_ref))
    sq_row = lax.broadcasted_iota(jnp.int32, (LANES, LANES), 0) >= RW_HEAD
    sq_col = lax.broadcasted_iota(jnp.int32, (LANES, LANES), 1) >= RW_HEAD
    same_head = sq_row == sq_col
    ones_bd = same_head.astype(BF16)

    @pl.when(step == 0)
    def _():
        zero = jnp.zeros((nb, RW_HEAD, RW_HEAD), F32)
        for p in range(hp):
            top = jnp.concatenate([s0_ref[:, RW_PAIR * p], zero], axis=-1)
            bot = jnp.concatenate([zero, s0_ref[:, RW_PAIR * p + 1]], axis=-1)
            s_scr[p * nb:(p + 1) * nb] = jnp.concatenate([top, bot], axis=1)

    def bdot(spec, a, b):
        return jnp.einsum(spec, a.astype(BF16), b.astype(BF16), preferred_element_type=F32)

    def head_sum(x):
        x2 = x.reshape(nu * chunk, LANES)
        hi = x2.astype(BF16)
        lo = (x2 - hi.astype(F32)).astype(BF16)
        s = jnp.dot(hi, ones_bd, preferred_element_type=F32) + jnp.dot(lo, ones_bd, preferred_element_type=F32)
        return s.reshape(nu, chunk, LANES)

    srow = lax.broadcasted_iota(jnp.int32, (chunk, chunk), 0)
    scol = lax.broadcasted_iota(jnp.int32, (chunk, chunk), 1)
    tri = jnp.broadcast_to((scol <= srow).astype(BF16)[None], (nu, chunk, chunk))

    def tri_sum(x):
        hi = x.astype(BF16)
        lo = (x - hi.astype(F32)).astype(BF16)
        return (jnp.einsum("uts,usc->utc", tri, hi, preferred_element_type=F32)
                + jnp.einsum("uts,usc->utc", tri, lo, preferred_element_type=F32))

    def stack_heads(x):
        return jnp.concatenate([jnp.where(head1, 0.0, x), jnp.where(head1, x, 0.0)], axis=1).astype(BF16)

    r, k, v, wl, al, g = (units(ref[...]) for ref in (r_ref, k_ref, v_ref, wl_ref, al_ref, g_ref))
    lw = (-RW_DECAY_SCALE) * jax.nn.sigmoid(w0 + wl)
    ag = jax.nn.sigmoid(a0 + al)
    kk = k * k_k
    kk = kk * jnp.minimum(lax.rsqrt(head_sum(kk * kk)), 1e12)
    k2 = k * (1.0 + (ag - 1.0) * k_a)
    cl = tri_sum(lw)
    e_pos = jnp.exp(cl)
    e_neg = jnp.exp(-cl)
    at = (-kk) * jnp.exp(cl - lw)
    bt = (kk * ag) * e_neg
    kt = k2 * e_neg
    rt = r * e_pos
    wc = e_pos[:, chunk - 1:chunk, :]
    trow = lax.broadcasted_iota(jnp.int32, (chunk, RW_PAIR * chunk), 0)
    tcol = lax.broadcasted_iota(jnp.int32, (chunk, RW_PAIR * chunk), 1)
    tcol = jnp.where(tcol >= chunk, tcol - chunk, tcol)
    strict = (tcol < trow)[None]
    incl = (tcol <= trow)[None]
    x2 = jnp.concatenate([at, rt], axis=1)
    pb = bdot("utc,usc->uts", x2, stack_heads(bt))
    pk = bdot("utc,usc->uts", x2, stack_heads(kt))
    lab = jnp.where(strict, pb[:, :chunk], 0.0)
    lak = jnp.where(strict, pk[:, :chunk], 0.0)
    arb = jnp.where(incl, pb[:, chunk:], 0.0)
    ark = jnp.where(incl, pk[:, chunk:], 0.0)
    v_bd = stack_heads(v)
    xa = at
    xv = bdot("uts,usc->utc", lak, v_bd)
    sub = min(RW_SOLVE_BLOCK, chunk)
    done_a, done_v = [], []
    for lo in range(0, chunk, sub):
        xa_i = xa[:, lo:lo + sub, :]
        xv_i = xv[:, lo:lo + sub, :]
        if lo:
            pad = jnp.zeros((nu, chunk - lo, LANES), F32)
            prev = jnp.concatenate([stack_heads(jnp.concatenate(done_a + [pad], axis=1)),
                                    stack_heads(jnp.concatenate(done_v + [pad], axis=1))], axis=-1)
            upd = bdot("uts,usc->utc", lab[:, lo:lo + sub, :], prev)
            xa_i = xa_i + upd[:, :, :LANES]
            xv_i = xv_i + upd[:, :, LANES:]
        l0 = lab[:, lo:lo + sub, lo:lo + sub]
        l1 = lab[:, lo:lo + sub, chunk + lo:chunk + lo + sub]
        for s in range(sub - 1):
            m = jnp.where(head1, l1[:, :, s:s + 1], l0[:, :, s:s + 1])
            xa_i = xa_i + m * xa_i[:, s:s + 1, :]
            xv_i = xv_i + m * xv_i[:, s:s + 1, :]
        done_a.append(xa_i)
        done_v.append(xv_i)
    ah = jnp.concatenate(done_a, axis=1)
    vh = jnp.concatenate(done_v, axis=1)
    both = bdot("uts,usc->utc", arb, jnp.concatenate([stack_heads(ah), stack_heads(vh)], axis=-1))
    rh = rt + both[:, :, :LANES]
    yh = both[:, :, LANES:] + bdot("uts,usc->utc", ark, v_bd)
    gp = jnp.where(same_head, bdot("utj,utk->ujk", ah, bt), 0.0)
    ht = jnp.where(same_head, bdot("utv,utk->uvk", jnp.concatenate([vh, v], axis=1),
                                   jnp.concatenate([bt, kt], axis=1)), 0.0)
    st = s_scr[...]
    y = bdot("utk,uvk->utv", rh, st) + yh
    st = (st + bdot("uvj,ujk->uvk", st, gp) + ht) * wc
    s_scr[...] = st
    inv_n = 1.0 / RW_HEAD
    yc = y - head_sum(y) * inv_n
    var = head_sum(yc * yc) * inv_n
    y = yc * lax.rsqrt(var + RW_GN_EPS) * ln_w + ln_b
    y = y + head_sum(r * k2 * r_k) * v
    out = (y * g).astype(o_ref.dtype)
    for p in range(hp):
        o_ref[:, :, p * LANES:(p + 1) * LANES] = out[p * nb:(p + 1) * nb]

    @pl.when(step == pl.num_programs(2) - 1)
    def _():
        for p in range(hp):
            sf_ref[:, RW_PAIR * p] = st[p * nb:(p + 1) * nb, :RW_HEAD, :RW_HEAD]
            sf_ref[:, RW_PAIR * p + 1] = st[p * nb:(p + 1) * nb, RW_HEAD:, RW_HEAD:]


def _rwkv_chunk(seq):
    for c in (48, 32, 16, 8):
        if seq % c == 0:
            return c
    raise ValueError(seq)


def rwkv7(r, k, v, wl, al, g, s0, w0, a0, k_k, k_a, r_k, ln_w, ln_b, nb, hp):
    bsz, seq, d = r.shape
    chunk = _rwkv_chunk(seq)
    heads = hp * RW_PAIR
    assert bsz % nb == 0 and RW_HEADS % heads == 0
    seq_blk = pl.BlockSpec((nb, chunk, hp * LANES), lambda h, b, t: (b, t, h))
    st_blk = pl.BlockSpec((nb, heads, RW_HEAD, RW_HEAD), lambda h, b, t: (b, h, 0, 0))
    vec = pl.BlockSpec((1, hp * LANES), lambda h, b, t: (0, h))
    out, sf = pl.pallas_call(
        functools.partial(_rwkv_kernel, nb=nb, hp=hp, chunk=chunk),
        grid=(RW_HEADS // heads, bsz // nb, seq // chunk),
        in_specs=[seq_blk] * 6 + [st_blk] + [vec] * 7,
        out_specs=[seq_blk, st_blk],
        out_shape=[jax.ShapeDtypeStruct((bsz, seq, d), BF16),
                   jax.ShapeDtypeStruct((bsz, RW_HEADS, RW_HEAD, RW_HEAD), F32)],
        scratch_shapes=[pltpu.VMEM((hp * nb, LANES, LANES), F32)],
        compiler_params=_params("parallel", "parallel", "arbitrary"),
        name="rwkv7",
    )(r, k, v, wl, al, g, s0, *(p.reshape(1, d) for p in (w0, a0, k_k, k_a, r_k, ln_w, ln_b)))
    return out, sf


def _ffn_in_kernel(x_ref, wa_ref, wv_ref, e_ref, cw_ref, cb_ref, o_ref, st_ref, scr, *, nb, seq, sb, sr):
    tn = wa_ref.shape[1]
    cw = cw_ref[...]
    cb = cb_ref[...][None]
    scr[:, 0:SUBLANES, :] = e_ref[...]
    for b0 in range(0, nb, sb):
        for r0 in range(0, seq, sr):
            lo = b0 * seq + r0
            x = x_ref[lo:lo + sb * sr, :]
            a = jnp.dot(x, wa_ref[...], preferred_element_type=F32).reshape(sb, sr, tn)
            v = jnp.dot(x, wv_ref[...], preferred_element_type=F32).reshape(sb, sr, tn)
            scr[b0:b0 + sb, SUBLANES + r0:SUBLANES + r0 + sr, :] = a
            c = cb + cw[CONV_W - 1:CONV_W][None] * a
            for j in range(CONV_W - 1):
                first = SUBLANES + r0 - (CONV_W - 1 - j)
                c = c + cw[j:j + 1][None] * scr[b0:b0 + sb, first:first + sr, :]
            o_ref[lo:lo + sb * sr, :] = (jax.nn.gelu(c) * v).reshape(sb * sr, tn).astype(o_ref.dtype)
    st_ref[...] = scr[:, SUBLANES + seq - (CONV_W - 1):SUBLANES + seq, :]


def ffn_in(xb, conv0, w_in, layer, conv_w, conv_b, bsz, seq, nb, tn, sub):
    t, d = xb.shape
    sb, sr = sub
    assert t == bsz * seq and bsz % nb == 0 and D_FF % tn == 0
    assert nb % sb == 0 and seq % sr == 0 and sr % SUBLANES == 0 and (sr == seq or nb == sb == 1)
    halo = jnp.pad(conv0, ((0, 0), (SUBLANES - (CONV_W - 1), 0), (0, 0)))
    col = lambda i, j: (0, j)
    n_col = D_FF // tn
    out, st = pl.pallas_call(
        functools.partial(_ffn_in_kernel, nb=nb, seq=seq, sb=sb, sr=sr),
        grid=(bsz // nb, n_col),
        in_specs=[pl.BlockSpec((nb * seq, d), lambda i, j: (i, 0)),
                  pl.BlockSpec((None, d, tn), lambda i, j: (layer, 0, j)),
                  pl.BlockSpec((None, d, tn), lambda i, j: (layer, 0, j + n_col)),
                  pl.BlockSpec((nb, SUBLANES, tn), lambda i, j: (i, 0, j)),
                  pl.BlockSpec((CONV_W, tn), col), pl.BlockSpec((1, tn), col)],
        out_specs=[pl.BlockSpec((nb * seq, tn), lambda i, j: (i, j)),
                   pl.BlockSpec((nb, CONV_W - 1, tn), lambda i, j: (i, 0, j))],
        out_shape=[jax.ShapeDtypeStruct((t, D_FF), BF16),
                   jax.ShapeDtypeStruct((bsz, CONV_W - 1, D_FF), F32)],
        scratch_shapes=[pltpu.VMEM((nb, SUBLANES + seq, tn), F32)],
        compiler_params=_params("parallel", "parallel"),
        name="ffn_in",
    )(xb, w_in, w_in, halo, conv_w, conv_b.reshape(1, D_FF))
    return out, st


def _channel_mixer(h, conv0, layer, p, cfg, bsz, seq):
    (xb,) = rmsnorm(h, p["ln_ffn"][layer], (BF16,))
    gated, n_cv = ffn_in(xb, conv0, p["ffn_w_in"], layer, p["ffn_conv_w"][layer], p["ffn_conv_b"][layer],
                         bsz, seq, cfg["ffn_nb"], cfg["ffn_tn"], cfg["ffn_sub"])
    return matmul([gated], [(p["ffn_w_down"], layer, 0)], F32, epilogue="residual", extra=h), n_cv


def _trunk(x3, s5r, s5i, hg, rw, sh, cv, p, cfg):
    bsz, seq, d = x3.shape
    seq += cfg["front"]
    t = bsz * seq

    if cfg["front"]:
        h, xb = embed_norm(x3, p["meta"], p["ln_mix"][0])
    else:
        h = x3.reshape(t, d)
        (xb,) = rmsnorm(h, p["ln_mix"][0], (BF16,))
    z = matmul([xb], [p["ev_w_in"]], F32).reshape(bsz, seq, EVEN_IN)
    if cfg["s5_tl"]:
        ys5, n_s5r, n_s5i = s5_scan_long(z, s5r[0], s5i[0], *p["s5"], tl=cfg["s5_tl"])
    else:
        ys5, n_s5r, n_s5i = s5_scan(z, s5r[0], s5i[0], *p["s5"], nb=cfg["s5_nb"])
    ys5 = ys5.reshape(t, S5_WIDTH)
    ya = matmul([ys5.astype(BF16)], [p["s5_w_glu"]], BF16, epilogue="glu", extra=ys5)
    yb, n_hg = hgrn2(z, hg[0], p["hg_lb"], p["hg_norm_w"], nb=cfg["hg_nb"], hh=cfg["hg_hh"])
    h = matmul([ya, yb.reshape(t, -1)], [(p["ev_w_out"], 0, 0), (p["ev_w_out"], 0, 1)], F32,
               epilogue="residual", extra=h)
    h, n_cv0 = _channel_mixer(h, cv[0], 0, p, cfg, bsz, seq)

    mixes, n_sh = norm_mix(h.reshape(bsz, seq, d), sh[0], p["ln_mix"][1], p["rw_mu"], *cfg["mix_blk"])
    xr, xw, xk, xv, xa, xg = (m.reshape(t, d) for m in mixes)
    r = matmul([xr], [p["rw_w_r"]], F32)
    k = matmul([xk], [p["rw_w_k"]], F32)
    v = matmul([xv], [p["rw_w_v"]], F32)
    wl = matmul([matmul([xw], [p["rw_w1"]], BF16, act="tanh")], [p["rw_w2"]], F32)
    al = matmul([matmul([xa], [p["rw_a1"]], BF16)], [p["rw_a2"]], F32)
    g = matmul([matmul([xg], [p["rw_g1"]], BF16, act="sigmoid")], [p["rw_g2"]], F32)
    as3 = lambda a: a.reshape(bsz, seq, d)
    yo, n_rw = rwkv7(as3(r), as3(k), as3(v), as3(wl), as3(al), as3(g), rw[0], *p["rw_vec"],
                     nb=cfg["rw_nb"], hp=cfg["rw_hp"])
    h = matmul([yo.reshape(t, d)], [p["rw_w_o"]], F32, epilogue="residual", extra=h)
    h, n_cv1 = _channel_mixer(h, cv[1], 1, p, cfg, bsz, seq)

    (y,) = rmsnorm(h, p["ln_final"], (F32,), rows=(bsz, seq, cfg["front"]) if cfg["front"] else None)
    return (y.reshape(bsz, seq - cfg["front"], d), n_s5r[None], n_s5i[None], n_hg[None], n_rw[None], n_sh[None],
            jnp.stack([n_cv0, n_cv1]))


PROMPT_CFG = dict(front=N_META, s5_tl=344, s5_nb=None, hg_nb=4, hg_hh=8, mix_blk=(1, 344), rw_nb=4, rw_hp=8, ffn_nb=1, ffn_tn=512, ffn_sub=(1, 688))
SAMPLE_CFG = dict(front=0, s5_tl=None, s5_nb=32, hg_nb=4, hg_hh=8, mix_blk=(32, 8), rw_nb=16, rw_hp=4, ffn_nb=128, ffn_tn=512, ffn_sub=(32, 8))


def kernel(x_prompt, x_sample, state_s5_re, state_s5_im, state_hgrn, state_rwkv, state_shift, state_conv, meta_tokens, ln_mix, ln_ffn, ln_final, ev_w_in, ev_w_out, s5_lam_re, s5_lam_im, s5_log_step, s5_b_re, s5_b_im, s5_c_re, s5_c_im, s5_d, s5_w_glu, hg_lb, hg_norm_w, rw_mu, rw_w0, rw_w1, rw_w2, rw_a0, rw_a1, rw_a2, rw_g1, rw_g2, rw_k_k, rw_k_a, rw_r_k, rw_w_r, rw_w_k, rw_w_v, rw_w_o, rw_ln_w, rw_ln_b, ffn_w_in, ffn_conv_w, ffn_conv_b, ffn_w_down):
    bf = lambda w: w.astype(BF16)
    lb_all = hg_lower_bounds(hg_lb)
    pwr, pwi, bbr_t, bbi_t = s5_prep(s5_lam_re[0], s5_lam_im[0], s5_log_step[0], s5_b_re[0], s5_b_im[0])
    wbr, wbi, wcr, wci = _s5_block_weights(bbr_t, bbi_t, s5_c_re[0], s5_c_im[0])
    p = {
        "meta": meta_tokens, "ln_mix": ln_mix, "ln_ffn": ln_ffn, "ln_final": ln_final,
        "ev_w_in": bf(ev_w_in[0]),
        "ev_w_out": bf(ev_w_out),
        "s5": (pwr, pwi, wbr, wbi, wcr, wci, s5_d[0].reshape(1, S5_WIDTH)),
        "s5_w_glu": bf(s5_w_glu[0]),
        "hg_lb": lb_all[0], "hg_norm_w": hg_norm_w[0],
        "rw_mu": rw_mu[0],
        "rw_w1": bf(rw_w1[0]), "rw_w2": bf(rw_w2[0]), "rw_a1": bf(rw_a1[0]), "rw_a2": bf(rw_a2[0]),
        "rw_g1": bf(rw_g1[0]), "rw_g2": bf(rw_g2[0]),
        "rw_w_r": bf(rw_w_r[0]), "rw_w_k": bf(rw_w_k[0]), "rw_w_v": bf(rw_w_v[0]), "rw_w_o": bf(rw_w_o[0]),
        "rw_vec": (rw_w0[0], rw_a0[0], rw_k_k[0], rw_k_a[0], rw_r_k[0].reshape(D_MODEL), rw_ln_w[0], rw_ln_b[0]),
        "ffn_w_in": bf(ffn_w_in), "ffn_conv_w": ffn_conv_w, "ffn_conv_b": ffn_conv_b,
        "ffn_w_down": bf(ffn_w_down),
    }

    bsz = x_prompt.shape[0]
    zeros = lambda *s: jnp.zeros(s, F32)
    outs_p = _trunk(x_prompt,
                    zeros(1, bsz, S5_GROUPS, S5_STATE), zeros(1, bsz, S5_GROUPS, S5_STATE),
                    zeros(1, bsz, HG_HEADS, HG_K, HG_V), zeros(1, bsz, RW_HEADS, RW_HEAD, RW_HEAD),
                    zeros(1, bsz, D_MODEL), zeros(2, bsz, CONV_W - 1, D_FF), p, PROMPT_CFG)
    outs_s = _trunk(x_sample, state_s5_re, state_s5_im, state_hgrn, state_rwkv, state_shift, state_conv,
                    p, SAMPLE_CFG)
    return tuple(outs_p[:1]) + tuple(outs_s[:1]) + tuple(outs_p[1:]) + tuple(outs_s[1:])
```

```python
import functools
import math

import jax
import jax.numpy as jnp
from jax import lax
from jax.experimental import pallas as pl
from jax.experimental.pallas import tpu as pltpu

F32 = jnp.float32
BF16 = jnp.bfloat16

D_MODEL = 2048
N_META = 16
EPS = 1e-6
S5_WIDTH = 1024
S5_GROUP = 16
S5_GROUPS = 64
S5_STATE = 64
S5_CH = S5_GROUPS * S5_STATE
HG_HEADS = 8
HG_K = 128
HG_V = 128
HG_CHUNK = 16
EVEN_IN = 5120
RW_HEAD = 64
RW_HEADS = 32
RW_GN_EPS = 64e-5
D_FF = 5632
CONV_W = 3

LANES = 128
SUBLANES = 8
VMEM_LIMIT = 56 * 1024 * 1024


def _params(*sem):
    return pltpu.CompilerParams(dimension_semantics=sem, vmem_limit_bytes=VMEM_LIMIT)


def _row_tile(t, cap=1024):
    best = None
    for d in range(16, min(t, cap) + 1, 16):
        if t % d == 0:
            best = d
    assert best is not None, t
    return best


def _rms_kernel(x_ref, w_ref, *o_refs):
    x = x_ref[...]
    y = x * lax.rsqrt(jnp.mean(x * x, axis=-1, keepdims=True) + EPS) * w_ref[...]
    for o_ref in o_refs:
        o_ref[...] = y.astype(o_ref.dtype)


def rmsnorm(x, w, dtypes, rows=None):
    t, d = x.shape
    if rows is None:
        tm = _row_tile(t)
        n_out = t
        grid = (t // tm,)
        in_spec = pl.BlockSpec((tm, d), lambda i: (i, 0))
        out_spec = in_spec
        vec = pl.BlockSpec((1, d), lambda i: (0, 0))
    else:
        bsz, seq, front = rows
        keep = seq - front
        assert t == bsz * seq and front % 16 == 0
        tm = _row_tile(keep)
        per = keep // tm
        n_out = bsz * keep
        grid = (bsz, per)
        in_spec = pl.BlockSpec((pl.Element(tm), pl.Element(d)),
                               lambda b, i: (pl.multiple_of(b * seq + front + i * tm, 16), 0))
        out_spec = pl.BlockSpec((tm, d), lambda b, i: (b * per + i, 0))
        vec = pl.BlockSpec((1, d), lambda b, i: (0, 0))
    outs = pl.pallas_call(
        _rms_kernel,
        grid=grid,
        in_specs=[in_spec, vec],
        out_specs=[out_spec for _ in dtypes],
        out_shape=[jax.ShapeDtypeStruct((n_out, d), dt) for dt in dtypes],
        compiler_params=_params(*(["parallel"] * len(grid))),
        name="rmsnorm",
    )(x, w.reshape(1, d))
    return outs


def _embed_norm_kernel(x_ref, m_ref, w_ref, h_ref, xb_ref, *, front):
    def emit(rows):
        h_ref[...] = rows
        y = rows * lax.rsqrt(jnp.mean(rows * rows, axis=-1, keepdims=True) + EPS) * w_ref[...]
        xb_ref[...] = y.astype(xb_ref.dtype)

    @pl.when(pl.program_id(1) == 0)
    def _():
        emit(jnp.concatenate([m_ref[...], x_ref[:x_ref.shape[0] - front, :]], axis=0))

    @pl.when(pl.program_id(1) > 0)
    def _():
        emit(x_ref[...])


def embed_norm(x3, meta, w):
    bsz, seq, d = x3.shape
    front = meta.shape[0]
    total = seq + front
    tm = _row_tile(total)
    per = total // tm
    assert front % 16 == 0 and tm > front
    blk = pl.BlockSpec((tm, d), lambda b, j: (b * per + j, 0))
    return pl.pallas_call(
        functools.partial(_embed_norm_kernel, front=front),
        grid=(bsz, per),
        in_specs=[pl.BlockSpec((pl.Element(tm), pl.Element(d)),
                               lambda b, j: (pl.multiple_of(b * seq + jnp.maximum(j * tm - front, 0), 16), 0)),
                  pl.BlockSpec((front, d), lambda b, j: (0, 0)),
                  pl.BlockSpec((1, d), lambda b, j: (0, 0))],
        out_specs=[blk, blk],
        out_shape=[jax.ShapeDtypeStruct((bsz * total, d), F32), jax.ShapeDtypeStruct((bsz * total, d), BF16)],
        compiler_params=_params("parallel", "parallel"),
        name="embed_norm",
    )(x3.reshape(bsz * seq, d), meta, w.reshape(1, d))


def _act(x, act):
    if act == "tanh":
        return jnp.tanh(x)
    if act == "sigmoid":
        return jax.nn.sigmoid(x)
    assert act is None
    return x


def _mm_kernel(*refs, n_a, act, epilogue):
    a_refs = refs[:n_a]
    w_refs = refs[n_a:2 * n_a]
    rest = refs[2 * n_a:]
    o_ref = rest[-1]
    acc = jnp.dot(a_refs[0][...].astype(BF16), w_refs[0][...], preferred_element_type=F32)
    for a_ref, w_ref in zip(a_refs[1:], w_refs[1:]):
        acc = acc + jnp.dot(a_ref[...].astype(BF16), w_ref[...], preferred_element_type=F32)
    acc = _act(acc, act)
    if epilogue == "residual":
        acc = rest[0][...] + acc
    elif epilogue == "glu":
        acc = rest[0][...] * jax.nn.sigmoid(acc)
    o_ref[...] = acc.astype(o_ref.dtype)


MM_VMEM_BUDGET = 40 * 1024 * 1024
MXU_WIDTH = 256


def _mm_tiles(t, k_total, n, out_bytes, has_extra):
    rows = [d for d in range(16, t + 1, 16) if t % d == 0]
    cols = [d for d in range(LANES, n + 1, LANES) if n % d == 0] or [n]
    best, best_score = None, -1.0
    for tm in rows:
        for tn in cols:
            est = 4 * tm * k_total + 4 * k_total * tn + tm * tn * (2 * out_bytes + 4 + (8 if has_extra else 0))
            if est > MM_VMEM_BUDGET:
                continue
            score = tm * tn * (1.0 if tn % MXU_WIDTH == 0 else 0.8)
            if score > best_score:
                best, best_score = (tm, tn), score
    assert best is not None, (t, k_total, n)
    return best


def matmul(a_list, w_list, out_dtype, act=None, epilogue=None, extra=None):
    t = a_list[0].shape[0]
    n = w_list[0][0].shape[2] if isinstance(w_list[0], tuple) else w_list[0].shape[1]
    k_total = sum(a.shape[1] * a.dtype.itemsize // 2 for a in a_list)
    tm, tn = _mm_tiles(t, k_total, n, jnp.dtype(out_dtype).itemsize, epilogue is not None)
    in_specs = [pl.BlockSpec((tm, a.shape[1]), lambda i, j: (i, 0)) for a in a_list]
    args = list(a_list)
    for a, w in zip(a_list, w_list):
        if isinstance(w, tuple):
            w, layer, kblk = w
            in_specs.append(pl.BlockSpec((None, a.shape[1], tn), lambda i, j, layer=layer, kblk=kblk: (layer, kblk, j)))
        else:
            in_specs.append(pl.BlockSpec((w.shape[0], tn), lambda i, j: (0, j)))
        args.append(w)
    if epilogue is not None:
        in_specs.append(pl.BlockSpec((tm, tn), lambda i, j: (i, j)))
        args.append(extra)
    return pl.pallas_call(
        functools.partial(_mm_kernel, n_a=len(a_list), act=act, epilogue=epilogue),
        grid=(t // tm, n // tn),
        in_specs=in_specs,
        out_specs=pl.BlockSpec((tm, tn), lambda i, j: (i, j)),
        out_shape=jax.ShapeDtypeStruct((t, n), out_dtype),
        compiler_params=_params("parallel", "parallel"),
        name="matmul",
    )(*args)


def _s5_prep_kernel(lr_ref, li_ref, ls_ref, brt_ref, bit_ref, pwr_ref, pwi_ref, bbr_ref, bbi_ref):
    lr = jnp.minimum(lr_ref[...], -1e-4)
    li = li_ref[...]
    dt = jnp.exp(ls_ref[...])
    n = lax.broadcasted_iota(jnp.int32, (SUBLANES, S5_CH), 0).astype(F32) + 1.0
    mag = jnp.exp(n * (lr * dt))
    ang = n * (li * dt)
    pwr = mag * jnp.cos(ang)
    pwi = mag * jnp.sin(ang)
    pwr_ref[...] = pwr
    pwi_ref[...] = pwi
    ar = pwr[0:1]
    ai = pwi[0:1]
    den = lr * lr + li * li
    zr = ((ar - 1.0) * lr + ai * li) / den
    zi = (ai * lr - (ar - 1.0) * li) / den
    br = brt_ref[...]
    bi = bit_ref[...]
    bbr_ref[...] = zr * br - zi * bi
    bbi_ref[...] = zr * bi + zi * br


def s5_prep(lam_re, lam_im, log_step, b_re, b_im):
    lr = lam_re.reshape(1, S5_CH)
    li = lam_im.reshape(1, S5_CH)
    ls = jnp.broadcast_to(log_step[:, None], (S5_GROUPS, S5_STATE)).reshape(1, S5_CH)
    brt = b_re.reshape(S5_CH, S5_GROUP).T
    bit = b_im.reshape(S5_CH, S5_GROUP).T
    return pl.pallas_call(
        _s5_prep_kernel,
        out_shape=[jax.ShapeDtypeStruct((SUBLANES, S5_CH), F32)] * 2
        + [jax.ShapeDtypeStruct((S5_GROUP, S5_CH), F32)] * 2,
        name="s5_prep",
    )(lr, li, ls, brt, bit)


S5_BLK_GROUPS = LANES // S5_GROUP
S5_BLKS = S5_WIDTH // LANES
S5_BLK_CH = S5_BLK_GROUPS * S5_STATE


def _cmul_add(xr, xi, mr, mi, sr, si):
    return xr + mr * sr - mi * si, xi + mr * si + mi * sr


def _s5_kernel(u_ref, h0r_ref, h0i_ref, pwr_ref, pwi_ref, wbr_ref, wbi_ref, wcr_ref, wci_ref, d_ref,
               y_ref, hr_ref, hi_ref, xr_scr, xi_scr, *, nb, seq):
    u2 = u_ref[...].reshape(nb * seq, LANES)
    ub = u2.astype(BF16)
    xr_scr[...] = jnp.dot(ub, wbr_ref[0], preferred_element_type=F32).reshape(nb, seq, S5_BLK_CH)
    xi_scr[...] = jnp.dot(ub, wbi_ref[0], preferred_element_type=F32).reshape(nb, seq, S5_BLK_CH)

    pwr = pwr_ref[...]
    pwi = pwi_ref[...]
    row = lax.broadcasted_iota(jnp.int32, (SUBLANES, S5_BLK_CH), 0)
    steps = []
    for d in (1, 2, 4):
        keep = row >= d
        steps.append((d, jnp.where(keep, pwr[d - 1:d], 0.0)[None], jnp.where(keep, pwi[d - 1:d], 0.0)[None]))
    pr = pwr[None]
    pi = pwi[None]

    def tile(i, carry):
        cr, ci = carry
        o = pl.multiple_of(i * SUBLANES, SUBLANES)
        xr = xr_scr[:, pl.ds(o, SUBLANES), :]
        xi = xi_scr[:, pl.ds(o, SUBLANES), :]
        for d, mr, mi in steps:
            sr = pltpu.roll(xr, d, axis=1)
            si = pltpu.roll(xi, d, axis=1)
            xr, xi = _cmul_add(xr, xi, mr, mi, sr, si)
        xr, xi = _cmul_add(xr, xi, pr, pi, cr, ci)
        xr_scr[:, pl.ds(o, SUBLANES), :] = xr
        xi_scr[:, pl.ds(o, SUBLANES), :] = xi
        return xr[:, SUBLANES - 1:SUBLANES, :], xi[:, SUBLANES - 1:SUBLANES, :]

    hr, hi = lax.fori_loop(0, seq // SUBLANES, tile, (h0r_ref[...], h0i_ref[...]))
    hr_ref[...] = hr
    hi_ref[...] = hi

    xr = xr_scr[...].reshape(nb * seq, S5_BLK_CH).astype(BF16)
    xi = xi_scr[...].reshape(nb * seq, S5_BLK_CH).astype(BF16)
    y = (jnp.dot(xr, wcr_ref[0], preferred_element_type=F32)
         - jnp.dot(xi, wci_ref[0], preferred_element_type=F32)
         + d_ref[...] * u2)
    y_ref[...] = jax.nn.gelu(y).reshape(nb, seq, LANES)


def s5_scan(z3, h0r, h0i, pwr, pwi, wbr, wbi, wcr, wci, d, nb):
    bsz, seq, _ = z3.shape
    assert seq % SUBLANES == 0 and bsz % nb == 0
    seq_blk = pl.BlockSpec((nb, seq, LANES), lambda b, k: (b, 0, k))
    st_blk = pl.BlockSpec((nb, 1, S5_BLK_CH), lambda b, k: (b, 0, k))
    pw_blk = pl.BlockSpec((SUBLANES, S5_BLK_CH), lambda b, k: (0, k))
    wb_blk = pl.BlockSpec((1, LANES, S5_BLK_CH), lambda b, k: (k, 0, 0))
    wc_blk = pl.BlockSpec((1, S5_BLK_CH, LANES), lambda b, k: (k, 0, 0))
    y, hr, hi = pl.pallas_call(
        functools.partial(_s5_kernel, nb=nb, seq=seq),
        grid=(bsz // nb, S5_BLKS),
        in_specs=[seq_blk, st_blk, st_blk, pw_blk, pw_blk, wb_blk, wb_blk, wc_blk, wc_blk,
                  pl.BlockSpec((1, LANES), lambda b, k: (0, k))],
        out_specs=[seq_blk, st_blk, st_blk],
        out_shape=[jax.ShapeDtypeStruct((bsz, seq, S5_WIDTH), F32),
                   jax.ShapeDtypeStruct((bsz, 1, S5_CH), F32),
                   jax.ShapeDtypeStruct((bsz, 1, S5_CH), F32)],
        scratch_shapes=[pltpu.VMEM((nb, seq, S5_BLK_CH), F32), pltpu.VMEM((nb, seq, S5_BLK_CH), F32)],
        compiler_params=_params("parallel", "parallel"),
        name="s5_scan",
    )(z3, h0r.reshape(bsz, 1, S5_CH), h0i.reshape(bsz, 1, S5_CH), pwr, pwi, wbr, wbi, wcr, wci, d)
    return y, hr.reshape(bsz, S5_GROUPS, S5_STATE), hi.reshape(bsz, S5_GROUPS, S5_STATE)


S5_LANE_TILES = S5_BLK_CH // LANES


def _s5_long_kernel(u_ref, h0r_ref, h0i_ref, ar_ref, ai_ref, wbr_ref, wbi_ref, wcr_ref, wci_ref, d_ref,
                    y_ref, hr_ref, hi_ref, xr_scr, xi_scr, cr_scr, ci_scr, *, tl):
    step = pl.program_id(1)

    @pl.when(step == 0)
    def _():
        cr_scr[...] = h0r_ref[0]
        ci_scr[...] = h0i_ref[0]

    for k in range(S5_BLKS):
        ub = u_ref[0, :, k * LANES:(k + 1) * LANES].astype(BF16)
        bur = jnp.dot(ub, wbr_ref[k], preferred_element_type=F32)
        bui = jnp.dot(ub, wbi_ref[k], preferred_element_type=F32)
        for j in range(S5_LANE_TILES):
            xr_scr[j, k * tl:(k + 1) * tl, :] = bur[:, j * LANES:(j + 1) * LANES]
            xi_scr[j, k * tl:(k + 1) * tl, :] = bui[:, j * LANES:(j + 1) * LANES]

    ar = [ar_ref[:, j * LANES:(j + 1) * LANES] for j in range(S5_LANE_TILES)]
    ai = [ai_ref[:, j * LANES:(j + 1) * LANES] for j in range(S5_LANE_TILES)]

    def token(t, carry):
        cr, ci = carry
        nr, ni = [], []
        for j in range(S5_LANE_TILES):
            rows = pl.ds(t, S5_BLKS, stride=tl)
            xr, xi = _cmul_add(xr_scr[j, rows, :], xi_scr[j, rows, :], ar[j], ai[j], cr[j], ci[j])
            xr_scr[j, rows, :] = xr
            xi_scr[j, rows, :] = xi
            nr.append(xr)
            ni.append(xi)
        return tuple(nr), tuple(ni)

    init = (tuple(cr_scr[:, j * LANES:(j + 1) * LANES] for j in range(S5_LANE_TILES)),
            tuple(ci_scr[:, j * LANES:(j + 1) * LANES] for j in range(S5_LANE_TILES)))
    cr, ci = lax.fori_loop(0, tl, token, init, unroll=8)
    cr = jnp.concatenate(cr, axis=-1)
    ci = jnp.concatenate(ci, axis=-1)
    cr_scr[...] = cr
    ci_scr[...] = ci
    hr_ref[0] = cr
    hi_ref[0] = ci

    for k in range(S5_BLKS):
        xr = jnp.concatenate([xr_scr[j, k * tl:(k + 1) * tl, :] for j in range(S5_LANE_TILES)], axis=-1)
        xi = jnp.concatenate([xi_scr[j, k * tl:(k + 1) * tl, :] for j in range(S5_LANE_TILES)], axis=-1)
        u = u_ref[0, :, k * LANES:(k + 1) * LANES]
        y = (jnp.dot(xr.astype(BF16), wcr_ref[k], preferred_element_type=F32)
             - jnp.dot(xi.astype(BF16), wci_ref[k], preferred_element_type=F32)
             + d_ref[:, k * LANES:(k + 1) * LANES] * u)
        y_ref[0, :, k * LANES:(k + 1) * LANES] = jax.nn.gelu(y)


def s5_scan_long(z3, h0r, h0i, pwr, pwi, wbr, wbi, wcr, wci, d, tl):
    bsz, seq, _ = z3.shape
    assert seq % tl == 0 and tl % SUBLANES == 0
    seq_blk = pl.BlockSpec((1, tl, S5_WIDTH), lambda b, t: (b, t, 0))
    st_blk = pl.BlockSpec((1, S5_BLKS, S5_BLK_CH), lambda b, t: (b, 0, 0))
    lam_blk = pl.BlockSpec((S5_BLKS, S5_BLK_CH), lambda b, t: (0, 0))
    wb_blk = pl.BlockSpec((S5_BLKS, LANES, S5_BLK_CH), lambda b, t: (0, 0, 0))
    wc_blk = pl.BlockSpec((S5_BLKS, S5_BLK_CH, LANES), lambda b, t: (0, 0, 0))
    rows = pltpu.VMEM((S5_LANE_TILES, S5_BLKS * tl, LANES), F32)
    carry = pltpu.VMEM((S5_BLKS, S5_BLK_CH), F32)
    y, hr, hi = pl.pallas_call(
        functools.partial(_s5_long_kernel, tl=tl),
        grid=(bsz, seq // tl),
        in_specs=[seq_blk, st_blk, st_blk, lam_blk, lam_blk, wb_blk, wb_blk, wc_blk, wc_blk,
                  pl.BlockSpec((1, S5_WIDTH), lambda b, t: (0, 0))],
        out_specs=[seq_blk, st_blk, st_blk],
        out_shape=[jax.ShapeDtypeStruct((bsz, seq, S5_WIDTH), F32),
                   jax.ShapeDtypeStruct((bsz, S5_BLKS, S5_BLK_CH), F32),
                   jax.ShapeDtypeStruct((bsz, S5_BLKS, S5_BLK_CH), F32)],
        scratch_shapes=[rows, rows, carry, carry],
        compiler_params=_params("parallel", "arbitrary"),
        name="s5_scan_long",
    )(z3, h0r.reshape(bsz, S5_BLKS, S5_BLK_CH), h0i.reshape(bsz, S5_BLKS, S5_BLK_CH),
      pwr[0].reshape(S5_BLKS, S5_BLK_CH), pwi[0].reshape(S5_BLKS, S5_BLK_CH), wbr, wbi, wcr, wci, d)
    return y, hr.reshape(bsz, S5_GROUPS, S5_STATE), hi.reshape(bsz, S5_GROUPS, S5_STATE)


def _s5_block_weights(bbr_t, bbi_t, c_re, c_im):
    eye = jnp.eye(S5_BLK_GROUPS, dtype=F32)

    def wb(bt):
        b4 = bt.reshape(S5_GROUP, S5_BLKS, S5_BLK_GROUPS, S5_STATE)
        w = jnp.einsum("cbgp,hg->bhcgp", b4, eye)
        return w.reshape(S5_BLKS, LANES, S5_BLK_CH).astype(BF16)

    def wc(c):
        c4 = c.reshape(S5_BLKS, S5_BLK_GROUPS, S5_GROUP, S5_STATE)
        w = jnp.einsum("bgcp,hg->bhpgc", c4, eye)
        return w.reshape(S5_BLKS, S5_BLK_CH, LANES).astype(BF16)

    return wb(bbr_t), wb(bbi_t), wc(c_re), wc(c_im)


def _hg_lb_kernel(x_ref, o_ref):
    x = x_ref[...]
    e = jnp.exp(x - jnp.max(x, axis=0, keepdims=True))
    sm = e / jnp.sum(e, axis=0, keepdims=True)
    acc = sm[0:1]
    o_ref[0:1, :] = acc
    for l in range(1, x.shape[0]):
        acc = acc + sm[l:l + 1]
        o_ref[l:l + 1, :] = acc


def hg_lower_bounds(hg_lb):
    return pl.pallas_call(_hg_lb_kernel, out_shape=jax.ShapeDtypeStruct(hg_lb.shape, F32), name="hg_lb")(hg_lb)


def _cumsum_rows(x, n):
    row = lax.broadcasted_iota(jnp.int32, x.shape, 1)
    d = 1
    while d < n:
        x = x + jnp.where(row >= d, pltpu.roll(x, d, axis=1), 0.0)
        d *= 2
    return x


def _hgrn_kernel(q_ref, f_ref, i_ref, g_ref, s0_ref, lb_ref, nw_ref, o_ref, sf_ref, st_scr, *, nb, hh, chunk):
    step = pl.program_id(2)

    def units(x):
        return jnp.concatenate([x[:, :, h * LANES:(h + 1) * LANES] for h in range(hh)], axis=0)

    lb = units(jnp.broadcast_to(lb_ref[...][None], (nb, 1, hh * LANES)))
    nw = nw_ref[...][None]

    @pl.when(step == 0)
    def _():
        for h in range(hh):
            for b in range(nb):
                st_scr[h * nb + b] = s0_ref[b, h].T

    trow = lax.broadcasted_iota(jnp.int32, (chunk, chunk), 0)
    tcol = lax.broadcasted_iota(jnp.int32, (chunk, chunk), 1)
    causal = (tcol <= trow)[None]
    q, f, v, g = (units(ref[...]) for ref in (q_ref, f_ref, i_ref, g_ref))
    fg = lb + (1.0 - lb) * jax.nn.sigmoid(f)
    qh = jax.nn.silu(q)
    kh = 1.0 - fg
    bcum = _cumsum_rows(jnp.log(fg), chunk)
    btot = bcum[:, chunk - 1:chunk, :]
    q_in = (qh * jnp.exp(bcum)).astype(BF16)
    k_in = (kh * jnp.exp(-bcum)).astype(BF16)
    k_end = (kh * jnp.exp(btot - bcum)).astype(BF16)
    decay = jnp.exp(btot)
    vb = v.astype(BF16)
    st = st_scr[...]
    att = jnp.einsum("utk,usk->uts", q_in, k_in, preferred_element_type=F32)
    att = jnp.where(causal, att, 0.0).astype(BF16)
    out = (jnp.einsum("utk,uvk->utv", q_in, st.astype(BF16), preferred_element_type=F32)
           + jnp.einsum("uts,usv->utv", att, vb, preferred_element_type=F32))
    st = st * decay + jnp.einsum("usv,usk->uvk", vb, k_end, preferred_element_type=F32)
    st_scr[...] = st
    out = out * lax.rsqrt(jnp.mean(out * out, axis=-1, keepdims=True) + EPS) * nw
    out = (out * jax.nn.silu(g)).astype(o_ref.dtype)
    for h in range(hh):
        o_ref[:, :, h * LANES:(h + 1) * LANES] = out[h * nb:(h + 1) * nb]

    @pl.when(step == pl.num_programs(2) - 1)
    def _():
        for h in range(hh):
            for b in range(nb):
                sf_ref[b, h] = st[h * nb + b].T


def hgrn2(z3, s0, lb, norm_w, nb, hh):
    bsz, seq, _ = z3.shape
    chunk = min(HG_CHUNK, seq)
    assert seq % chunk == 0 and bsz % nb == 0 and HG_HEADS % hh == 0
    wid = hh * LANES
    n_col = (HG_HEADS * HG_K) // wid

    def col(proj):
        return pl.BlockSpec((nb, chunk, wid), lambda h, b, t, proj=proj: (b, t, proj * n_col + h))

    st_blk = pl.BlockSpec((nb, hh, HG_K, HG_V), lambda h, b, t: (b, h, 0, 0))
    out, sf = pl.pallas_call(
        functools.partial(_hgrn_kernel, nb=nb, hh=hh, chunk=chunk),
        grid=(HG_HEADS // hh, bsz // nb, seq // chunk),
        in_specs=[col(1), col(2), col(3), col(4), st_blk,
                  pl.BlockSpec((1, wid), lambda h, b, t: (0, h)),
                  pl.BlockSpec((1, LANES), lambda h, b, t: (0, 0))],
        out_specs=[pl.BlockSpec((nb, chunk, wid), lambda h, b, t: (b, t, h)), st_blk],
        out_shape=[jax.ShapeDtypeStruct((bsz, seq, HG_HEADS * HG_V), BF16),
                   jax.ShapeDtypeStruct((bsz, HG_HEADS, HG_K, HG_V), F32)],
        scratch_shapes=[pltpu.VMEM((hh * nb, HG_V, HG_K), F32)],
        compiler_params=_params("parallel", "parallel", "arbitrary"),
        name="hgrn2",
    )(z3, z3, z3, z3, s0, lb.reshape(1, HG_HEADS * HG_K), norm_w.reshape(1, HG_V))
    return out, sf


RW_MIXES = 6


def _norm_mix_kernel(h_ref, sh_ref, w_ref, mu_ref, *refs, tl):
    o_refs = refs[:RW_MIXES]
    last_ref, scr = refs[RW_MIXES:]
    x = h_ref[...]
    xn = x * lax.rsqrt(jnp.mean(x * x, axis=-1, keepdims=True) + EPS) * w_ref[...][None]

    @pl.when(pl.program_id(1) == 0)
    def _():
        scr[:, SUBLANES - 1:SUBLANES, :] = sh_ref[...]

    scr[:, SUBLANES:, :] = xn
    xx = scr[:, SUBLANES - 1:SUBLANES - 1 + tl, :] - xn
    for j, o_ref in enumerate(o_refs):
        o_ref[...] = (xn + xx * mu_ref[j:j + 1, :][None]).astype(o_ref.dtype)
    last = xn[:, tl - 1:tl, :]
    scr[:, SUBLANES - 1:SUBLANES, :] = last
    last_ref[...] = last


def norm_mix(h3, shift0, ln_w, mu, nb, tl):
    bsz, seq, d = h3.shape
    assert bsz % nb == 0 and seq % tl == 0 and tl % SUBLANES == 0
    blk = pl.BlockSpec((nb, tl, d), lambda b, t: (b, t, 0))
    row = pl.BlockSpec((nb, 1, d), lambda b, t: (b, 0, 0))
    outs = pl.pallas_call(
        functools.partial(_norm_mix_kernel, tl=tl),
        grid=(bsz // nb, seq // tl),
        in_specs=[blk, row, pl.BlockSpec((1, d), lambda b, t: (0, 0)), pl.BlockSpec((RW_MIXES, d), lambda b, t: (0, 0))],
        out_specs=[blk] * RW_MIXES + [row],
        out_shape=[jax.ShapeDtypeStruct((bsz, seq, d), BF16)] * RW_MIXES + [jax.ShapeDtypeStruct((bsz, 1, d), F32)],
        scratch_shapes=[pltpu.VMEM((nb, SUBLANES + tl, d), F32)],
        compiler_params=_params("parallel", "arbitrary"),
        name="norm_mix",
    )(h3, shift0.reshape(bsz, 1, d), ln_w.reshape(1, d), mu)
    return outs[:RW_MIXES], outs[RW_MIXES].reshape(bsz, d)


RW_PAIR = LANES // RW_HEAD
RW_DECAY_SCALE = math.exp(-0.5)
RW_SOLVE_BLOCK = 8


def _rwkv_kernel(r_ref, k_ref, v_ref, tw_ref, ta_ref, tg_ref, w2_ref, a2_ref, g2_ref, s0_ref,
                 w0_ref, a0_ref, kk_ref, ka_ref, rk_ref, lnw_ref, lnb_ref,
                 o_ref, sf_ref, s_scr, *, nb, hp, chunk):
    nu = hp * nb
    step = pl.program_id(2)
    lane = lax.broadcasted_iota(jnp.int32, (1, 1, LANES), 2)
    head1 = lane >= RW_HEAD

    def units(x):
        return jnp.concatenate([x[:, :, p * LANES:(p + 1) * LANES] for p in range(hp)], axis=0)

    def unit_rows(ref):
        return units(jnp.broadcast_to(ref[...][None], (nb, 1, hp * LANES)))

    w0, a0, k_k, k_a, r_k, ln_w, ln_b = (unit_rows(p) for p in
                                         (w0_ref, a0_ref, kk_ref, ka_ref, rk_ref, lnw_ref, lnb_ref))
    sq_row = lax.broadcasted_iota(jnp.int32, (LANES, LANES), 0) >= RW_HEAD
    sq_col = lax.broadcasted_iota(jnp.int32, (LANES, LANES), 1) >= RW_HEAD
    same_head = sq_row == sq_col
    ones_bd = same_head.astype(BF16)

    @pl.when(step == 0)
    def _():
        zero = jnp.zeros((nb, RW_HEAD, RW_HEAD), F32)
        for p in range(hp):
            top = jnp.concatenate([s0_ref[:, RW_PAIR * p], zero], axis=-1)
            bot = jnp.concatenate([zero, s0_ref[:, RW_PAIR * p + 1]], axis=-1)
            s_scr[p * nb:(p + 1) * nb] = jnp.concatenate([top, bot], axis=1)

    def bdot(spec, a, b):
        return jnp.einsum(spec, a.astype(BF16), b.astype(BF16), preferred_element_type=F32)

    def head_sum(x, two_pass=True):
        x2 = x.reshape(nu * chunk, LANES)
        hi = x2.astype(BF16)
        s = jnp.dot(hi, ones_bd, preferred_element_type=F32)
        if two_pass:
            lo = (x2 - hi.astype(F32)).astype(BF16)
            s = s + jnp.dot(lo, ones_bd, preferred_element_type=F32)
        return s.reshape(nu, chunk, LANES)

    def low_rank(t_ref, w_ref):
        t2 = t_ref[...].reshape(nb * chunk, t_ref.shape[-1])
        return units(jnp.dot(t2, w_ref[...], preferred_element_type=F32).reshape(nb, chunk, hp * LANES))

    srow = lax.broadcasted_iota(jnp.int32, (chunk, chunk), 0)
    scol = lax.broadcasted_iota(jnp.int32, (chunk, chunk), 1)
    tri = jnp.broadcast_to((scol <= srow).astype(BF16)[None], (nu, chunk, chunk))

    def tri_sum(x):
        hi = x.astype(BF16)
        lo = (x - hi.astype(F32)).astype(BF16)
        return (jnp.einsum("uts,usc->utc", tri, hi, preferred_element_type=F32)
                + jnp.einsum("uts,usc->utc", tri, lo, preferred_element_type=F32))

    def stack_heads(x):
        return jnp.concatenate([jnp.where(head1, 0.0, x), jnp.where(head1, x, 0.0)], axis=1).astype(BF16)

    r, k, v = (units(ref[...]) for ref in (r_ref, k_ref, v_ref))
    wl, al, g = low_rank(tw_ref, w2_ref), low_rank(ta_ref, a2_ref), low_rank(tg_ref, g2_ref)
    lw = (-RW_DECAY_SCALE) * jax.nn.sigmoid(w0 + wl)
    ag = jax.nn.sigmoid(a0 + al)
    kk = k * k_k
    kk = kk * jnp.minimum(lax.rsqrt(head_sum(kk * kk)), 1e12)
    k2 = k * (1.0 + (ag - 1.0) * k_a)
    cl = tri_sum(lw)
    e_pos = jnp.exp(cl)
    e_neg = jnp.exp(-cl)
    at = (-kk) * jnp.exp(cl - lw)
    bt = (kk * ag) * e_neg
    kt = k2 * e_neg
    rt = r * e_pos
    wc = e_pos[:, chunk - 1:chunk, :]
    trow = lax.broadcasted_iota(jnp.int32, (chunk, RW_PAIR * chunk), 0)
    tcol = lax.broadcasted_iota(jnp.int32, (chunk, RW_PAIR * chunk), 1)
    tcol = jnp.where(tcol >= chunk, tcol - chunk, tcol)
    strict = (tcol < trow)[None]
    incl = (tcol <= trow)[None]
    x2 = jnp.concatenate([at, rt], axis=1)
    pb = bdot("utc,usc->uts", x2, stack_heads(bt))
    pk = bdot("utc,usc->uts", x2, stack_heads(kt))
    lab = jnp.where(strict, pb[:, :chunk], 0.0)
    lak = jnp.where(strict, pk[:, :chunk], 0.0)
    arb = jnp.where(incl, pb[:, chunk:], 0.0)
    ark = jnp.where(incl, pk[:, chunk:], 0.0)
    v_bd = stack_heads(v)
    xa = at
    xv = bdot("uts,usc->utc", lak, v_bd)
    sub = min(RW_SOLVE_BLOCK, chunk)
    done_a, done_v = [], []
    for lo in range(0, chunk, sub):
        xa_i = xa[:, lo:lo + sub, :]
        xv_i = xv[:, lo:lo + sub, :]
        if lo:
            pad = jnp.zeros((nu, chunk - lo, LANES), F32)
            prev = jnp.concatenate([stack_heads(jnp.concatenate(done_a + [pad], axis=1)),
                                    stack_heads(jnp.concatenate(done_v + [pad], axis=1))], axis=-1)
            upd = bdot("uts,usc->utc", lab[:, lo:lo + sub, :], prev)
            xa_i = xa_i + upd[:, :, :LANES]
            xv_i = xv_i + upd[:, :, LANES:]
        l0 = lab[:, lo:lo + sub, lo:lo + sub]
        l1 = lab[:, lo:lo + sub, chunk + lo:chunk + lo + sub]
        for s in range(sub - 1):
            m = jnp.where(head1, l1[:, :, s:s + 1], l0[:, :, s:s + 1])
            xa_i = xa_i + m * xa_i[:, s:s + 1, :]
            xv_i = xv_i + m * xv_i[:, s:s + 1, :]
        done_a.append(xa_i)
        done_v.append(xv_i)
    ah = jnp.concatenate(done_a, axis=1)
    vh = jnp.concatenate(done_v, axis=1)
    both = bdot("uts,usc->utc", arb, jnp.concatenate([stack_heads(ah), stack_heads(vh)], axis=-1))
    rh = rt + both[:, :, :LANES]
    yh = both[:, :, LANES:] + bdot("uts,usc->utc", ark, v_bd)
    gp = jnp.where(same_head, bdot("utj,utk->ujk", ah, bt), 0.0)
    ht = jnp.where(same_head, bdot("utv,utk->uvk", jnp.concatenate([vh, v], axis=1),
                                   jnp.concatenate([bt, kt], axis=1)), 0.0)
    st = s_scr[...]
    y = bdot("utk,uvk->utv", rh, st) + yh
    st = (st + bdot("uvj,ujk->uvk", st, gp) + ht) * wc
    s_scr[...] = st
    inv_n = 1.0 / RW_HEAD
    yc = y - head_sum(y, two_pass=False) * inv_n
    var = head_sum(yc * yc, two_pass=False) * inv_n
    y = yc * lax.rsqrt(var + RW_GN_EPS) * ln_w + ln_b
    y = y + head_sum(r * k2 * r_k, two_pass=False) * v
    out = (y * g).astype(o_ref.dtype)
    for p in range(hp):
        o_ref[:, :, p * LANES:(p + 1) * LANES] = out[p * nb:(p + 1) * nb]

    @pl.when(step == pl.num_programs(2) - 1)
    def _():
        for p in range(hp):
            sf_ref[:, RW_PAIR * p] = st[p * nb:(p + 1) * nb, :RW_HEAD, :RW_HEAD]
            sf_ref[:, RW_PAIR * p + 1] = st[p * nb:(p + 1) * nb, RW_HEAD:, RW_HEAD:]


def _rwkv_chunk(seq):
    for c in (48, 32, 16, 8):
        if seq % c == 0:
            return c
    raise ValueError(seq)


def rwkv7(r, k, v, low, low_w, s0, w0, a0, k_k, k_a, r_k, ln_w, ln_b, nb, hp):
    bsz, seq, d = r.shape
    chunk = _rwkv_chunk(seq)
    heads = hp * RW_PAIR
    assert bsz % nb == 0 and RW_HEADS % heads == 0
    seq_blk = pl.BlockSpec((nb, chunk, hp * LANES), lambda h, b, t: (b, t, h))
    st_blk = pl.BlockSpec((nb, heads, RW_HEAD, RW_HEAD), lambda h, b, t: (b, h, 0, 0))
    vec = pl.BlockSpec((1, hp * LANES), lambda h, b, t: (0, h))
    low_blk = [pl.BlockSpec((nb, chunk, x.shape[-1]), lambda h, b, t: (b, t, 0)) for x in low]
    low_w_blk = [pl.BlockSpec((w.shape[0], hp * LANES), lambda h, b, t: (0, h)) for w in low_w]
    out, sf = pl.pallas_call(
        functools.partial(_rwkv_kernel, nb=nb, hp=hp, chunk=chunk),
        grid=(RW_HEADS // heads, bsz // nb, seq // chunk),
        in_specs=[seq_blk] * 3 + low_blk + low_w_blk + [st_blk] + [vec] * 7,
        out_specs=[seq_blk, st_blk],
        out_shape=[jax.ShapeDtypeStruct((bsz, seq, d), BF16),
                   jax.ShapeDtypeStruct((bsz, RW_HEADS, RW_HEAD, RW_HEAD), F32)],
        scratch_shapes=[pltpu.VMEM((hp * nb, LANES, LANES), F32)],
        compiler_params=_params("parallel", "parallel", "arbitrary"),
        name="rwkv7",
    )(r, k, v, *low, *low_w, s0, *(p.reshape(1, d) for p in (w0, a0, k_k, k_a, r_k, ln_w, ln_b)))
    return out, sf


def _ffn_in_kernel(x_ref, wa_ref, wv_ref, e_ref, cw_ref, cb_ref, o_ref, st_ref, scr, *, nb, seq, sb, sr):
    tn = wa_ref.shape[1]
    cw = cw_ref[...]
    cb = cb_ref[...][None]
    scr[:, SUBLANES - (CONV_W - 1):SUBLANES, :] = e_ref[...]
    for b0 in range(0, nb, sb):
        for r0 in range(0, seq, sr):
            lo = b0 * seq + r0
            x = x_ref[lo:lo + sb * sr, :]
            a = jnp.dot(x, wa_ref[...], preferred_element_type=F32).reshape(sb, sr, tn)
            v = jnp.dot(x, wv_ref[...], preferred_element_type=F32).reshape(sb, sr, tn)
            scr[b0:b0 + sb, SUBLANES + r0:SUBLANES + r0 + sr, :] = a
            c = cb + cw[CONV_W - 1:CONV_W][None] * a
            for j in range(CONV_W - 1):
                first = SUBLANES + r0 - (CONV_W - 1 - j)
                c = c + cw[j:j + 1][None] * scr[b0:b0 + sb, first:first + sr, :]
            o_ref[lo:lo + sb * sr, :] = (jax.nn.gelu(c) * v).reshape(sb * sr, tn).astype(o_ref.dtype)
    st_ref[...] = scr[:, SUBLANES + seq - (CONV_W - 1):SUBLANES + seq, :]


def ffn_in(xb, conv0, w_in, layer, conv_w, conv_b, bsz, seq, nb, tn, sub):
    t, d = xb.shape
    sb, sr = sub
    assert t == bsz * seq and bsz % nb == 0 and D_FF % tn == 0
    assert nb % sb == 0 and seq % sr == 0 and sr % SUBLANES == 0 and (sr == seq or nb == sb == 1)
    col = lambda i, j: (0, j)
    n_col = D_FF // tn
    out, st = pl.pallas_call(
        functools.partial(_ffn_in_kernel, nb=nb, seq=seq, sb=sb, sr=sr),
        grid=(bsz // nb, n_col),
        in_specs=[pl.BlockSpec((nb * seq, d), lambda i, j: (i, 0)),
                  pl.BlockSpec((None, d, tn), lambda i, j: (layer, 0, j)),
                  pl.BlockSpec((None, d, tn), lambda i, j: (layer, 0, j + n_col)),
                  pl.BlockSpec((nb, CONV_W - 1, tn), lambda i, j: (i, 0, j)),
                  pl.BlockSpec((CONV_W, tn), col), pl.BlockSpec((1, tn), col)],
        out_specs=[pl.BlockSpec((nb * seq, tn), lambda i, j: (i, j)),
                   pl.BlockSpec((nb, CONV_W - 1, tn), lambda i, j: (i, 0, j))],
        out_shape=[jax.ShapeDtypeStruct((t, D_FF), BF16),
                   jax.ShapeDtypeStruct((bsz, CONV_W - 1, D_FF), F32)],
        scratch_shapes=[pltpu.VMEM((nb, SUBLANES + seq, tn), F32)],
        compiler_params=_params("parallel", "parallel"),
        name="ffn_in",
    )(xb, w_in, w_in, conv0, conv_w, conv_b.reshape(1, D_FF))
    return out, st


def _channel_mixer(h, conv0, layer, p, cfg, bsz, seq):
    (xb,) = rmsnorm(h, p["ln_ffn"][layer], (BF16,))
    gated, n_cv = ffn_in(xb, conv0, p["ffn_w_in"], layer, p["ffn_conv_w"][layer], p["ffn_conv_b"][layer],
                         bsz, seq, cfg["ffn_nb"], cfg["ffn_tn"], cfg["ffn_sub"])
    return matmul([gated], [(p["ffn_w_down"], layer, 0)], F32, epilogue="residual", extra=h), n_cv


def _trunk(x3, s5r, s5i, hg, rw, sh, cv, p, cfg):
    bsz, seq, d = x3.shape
    seq += cfg["front"]
    t = bsz * seq

    if cfg["front"]:
        h, xb = embed_norm(x3, p["meta"], p["ln_mix"][0])
    else:
        h = x3.reshape(t, d)
        (xb,) = rmsnorm(h, p["ln_mix"][0], (BF16,))
    z = matmul([xb], [p["ev_w_in"]], F32).reshape(bsz, seq, EVEN_IN)
    if cfg["s5_tl"]:
        ys5, n_s5r, n_s5i = s5_scan_long(z, s5r[0], s5i[0], *p["s5"], tl=cfg["s5_tl"])
    else:
        ys5, n_s5r, n_s5i = s5_scan(z, s5r[0], s5i[0], *p["s5"], nb=cfg["s5_nb"])
    ys5 = ys5.reshape(t, S5_WIDTH)
    ya = matmul([ys5], [p["s5_w_glu"]], BF16, epilogue="glu", extra=ys5)
    yb, n_hg = hgrn2(z, hg[0], p["hg_lb"], p["hg_norm_w"], nb=cfg["hg_nb"], hh=cfg["hg_hh"])
    h = matmul([ya, yb.reshape(t, -1)], [(p["ev_w_out"], 0, 0), (p["ev_w_out"], 0, 1)], F32,
               epilogue="residual", extra=h)
    h, n_cv0 = _channel_mixer(h, cv[0], 0, p, cfg, bsz, seq)

    mixes, n_sh = norm_mix(h.reshape(bsz, seq, d), sh[0], p["ln_mix"][1], p["rw_mu"], *cfg["mix_blk"])
    xr, xw, xk, xv, xa, xg = (m.reshape(t, d) for m in mixes)
    r = matmul([xr], [p["rw_w_r"]], F32)
    k = matmul([xk], [p["rw_w_k"]], F32)
    v = matmul([xv], [p["rw_w_v"]], F32)
    low = (matmul([xw], [p["rw_w1"]], BF16, act="tanh"), matmul([xa], [p["rw_a1"]], BF16),
           matmul([xg], [p["rw_g1"]], BF16, act="sigmoid"))
    as3 = lambda a: a.reshape(bsz, seq, a.shape[-1])
    yo, n_rw = rwkv7(as3(r), as3(k), as3(v), [as3(x) for x in low], (p["rw_w2"], p["rw_a2"], p["rw_g2"]),
                     rw[0], *p["rw_vec"], nb=cfg["rw_nb"], hp=cfg["rw_hp"])
    h = matmul([yo.reshape(t, d)], [p["rw_w_o"]], F32, epilogue="residual", extra=h)
    h, n_cv1 = _channel_mixer(h, cv[1], 1, p, cfg, bsz, seq)

    (y,) = rmsnorm(h, p["ln_final"], (F32,), rows=(bsz, seq, cfg["front"]) if cfg["front"] else None)
    return (y.reshape(bsz, seq - cfg["front"], d), n_s5r[None], n_s5i[None], n_hg[None], n_rw[None], n_sh[None],
            jnp.stack([n_cv0, n_cv1]))


PROMPT_CFG = dict(front=N_META, s5_tl=344, s5_nb=None, hg_nb=4, hg_hh=8, mix_blk=(1, 344), rw_nb=4, rw_hp=8, ffn_nb=1, ffn_tn=512, ffn_sub=(1, 688))
SAMPLE_CFG = dict(front=0, s5_tl=None, s5_nb=32, hg_nb=4, hg_hh=8, mix_blk=(32, 8), rw_nb=16, rw_hp=4, ffn_nb=128, ffn_tn=512, ffn_sub=(32, 8))


def kernel(x_prompt, x_sample, state_s5_re, state_s5_im, state_hgrn, state_rwkv, state_shift, state_conv, meta_tokens, ln_mix, ln_ffn, ln_final, ev_w_in, ev_w_out, s5_lam_re, s5_lam_im, s5_log_step, s5_b_re, s5_b_im, s5_c_re, s5_c_im, s5_d, s5_w_glu, hg_lb, hg_norm_w, rw_mu, rw_w0, rw_w1, rw_w2, rw_a0, rw_a1, rw_a2, rw_g1, rw_g2, rw_k_k, rw_k_a, rw_r_k, rw_w_r, rw_w_k, rw_w_v, rw_w_o, rw_ln_w, rw_ln_b, ffn_w_in, ffn_conv_w, ffn_conv_b, ffn_w_down):
    bf = lambda w: w.astype(BF16)
    lb_all = hg_lower_bounds(hg_lb)
    pwr, pwi, bbr_t, bbi_t = s5_prep(s5_lam_re[0], s5_lam_im[0], s5_log_step[0], s5_b_re[0], s5_b_im[0])
    wbr, wbi, wcr, wci = _s5_block_weights(bbr_t, bbi_t, s5_c_re[0], s5_c_im[0])
    p = {
        "meta": meta_tokens, "ln_mix": ln_mix, "ln_ffn": ln_ffn, "ln_final": ln_final,
        "ev_w_in": bf(ev_w_in[0]),
        "ev_w_out": bf(ev_w_out),
        "s5": (pwr, pwi, wbr, wbi, wcr, wci, s5_d[0].reshape(1, S5_WIDTH)),
        "s5_w_glu": bf(s5_w_glu[0]),
        "hg_lb": lb_all[0], "hg_norm_w": hg_norm_w[0],
        "rw_mu": rw_mu[0],
        "rw_w1": bf(rw_w1[0]), "rw_w2": bf(rw_w2[0]), "rw_a1": bf(rw_a1[0]), "rw_a2": bf(rw_a2[0]),
        "rw_g1": bf(rw_g1[0]), "rw_g2": bf(rw_g2[0]),
        "rw_w_r": bf(rw_w_r[0]), "rw_w_k": bf(rw_w_k[0]), "rw_w_v": bf(rw_w_v[0]), "rw_w_o": bf(rw_w_o[0]),
        "rw_vec": (rw_w0[0], rw_a0[0], rw_k_k[0], rw_k_a[0], rw_r_k[0].reshape(D_MODEL), rw_ln_w[0], rw_ln_b[0]),
        "ffn_w_in": bf(ffn_w_in), "ffn_conv_w": ffn_conv_w, "ffn_conv_b": ffn_conv_b,
        "ffn_w_down": bf(ffn_w_down),
    }

    bsz = x_prompt.shape[0]
    zeros = lambda *s: jnp.zeros(s, F32)
    outs_p = _trunk(x_prompt,
                    zeros(1, bsz, S5_GROUPS, S5_STATE), zeros(1, bsz, S5_GROUPS, S5_STATE),
                    zeros(1, bsz, HG_HEADS, HG_K, HG_V), zeros(1, bsz, RW_HEADS, RW_HEAD, RW_HEAD),
                    zeros(1, bsz, D_MODEL), zeros(2, bsz, CONV_W - 1, D_FF), p, PROMPT_CFG)
    outs_s = _trunk(x_sample, state_s5_re, state_s5_im, state_hgrn, state_rwkv, state_shift, state_conv,
                    p, SAMPLE_CFG)
    return tuple(outs_p[:1]) + tuple(outs_s[:1]) + tuple(outs_p[1:]) + tuple(outs_s[1:])
```

```python
import functools
import math

import jax
import jax.numpy as jnp
from jax import lax
from jax.experimental import pallas as pl
from jax.experimental.pallas import tpu as pltpu

F32 = jnp.float32
BF16 = jnp.bfloat16

D_MODEL = 2048
N_META = 16
EPS = 1e-6
S5_WIDTH = 1024
S5_GROUP = 16
S5_GROUPS = 64
S5_STATE = 64
S5_CH = S5_GROUPS * S5_STATE
HG_HEADS = 8
HG_K = 128
HG_V = 128
HG_CHUNK = 16
EVEN_IN = 5120
RW_HEAD = 64
RW_HEADS = 32
RW_GN_EPS = 64e-5
D_FF = 5632
CONV_W = 3

LANES = 128
SUBLANES = 8
VMEM_LIMIT = 56 * 1024 * 1024


def _params(*sem):
    return pltpu.CompilerParams(dimension_semantics=sem, vmem_limit_bytes=VMEM_LIMIT)


def _row_tile(t, cap=1024):
    best = None
    for d in range(16, min(t, cap) + 1, 16):
        if t % d == 0:
            best = d
    assert best is not None, t
    return best


def _rms_kernel(x_ref, w_ref, *o_refs):
    x = x_ref[...]
    y = x * lax.rsqrt(jnp.mean(x * x, axis=-1, keepdims=True) + EPS) * w_ref[...]
    for o_ref in o_refs:
        o_ref[...] = y.astype(o_ref.dtype)


def rmsnorm(x, w, dtypes, rows=None):
    t, d = x.shape
    if rows is None:
        tm = _row_tile(t)
        n_out = t
        grid = (t // tm,)
        in_spec = pl.BlockSpec((tm, d), lambda i: (i, 0))
        out_spec = in_spec
        vec = pl.BlockSpec((1, d), lambda i: (0, 0))
    else:
        bsz, seq, front = rows
        keep = seq - front
        assert t == bsz * seq and front % 16 == 0
        tm = _row_tile(keep)
        per = keep // tm
        n_out = bsz * keep
        grid = (bsz, per)
        in_spec = pl.BlockSpec((pl.Element(tm), pl.Element(d)),
                               lambda b, i: (pl.multiple_of(b * seq + front + i * tm, 16), 0))
        out_spec = pl.BlockSpec((tm, d), lambda b, i: (b * per + i, 0))
        vec = pl.BlockSpec((1, d), lambda b, i: (0, 0))
    outs = pl.pallas_call(
        _rms_kernel,
        grid=grid,
        in_specs=[in_spec, vec],
        out_specs=[out_spec for _ in dtypes],
        out_shape=[jax.ShapeDtypeStruct((n_out, d), dt) for dt in dtypes],
        compiler_params=_params(*(["parallel"] * len(grid))),
        name="rmsnorm",
    )(x, w.reshape(1, d))
    return outs


def _embed_norm_kernel(x_ref, m_ref, w_ref, h_ref, xb_ref, *, front):
    x = x_ref[...]
    first = pl.program_id(1) == 0
    body = jnp.where(first, pltpu.roll(x, front, axis=0), x)
    head = jnp.where(first, m_ref[...], x[:front])
    rows = jnp.concatenate([head, body[front:]], axis=0)
    h_ref[...] = rows
    y = rows * lax.rsqrt(jnp.mean(rows * rows, axis=-1, keepdims=True) + EPS) * w_ref[...]
    xb_ref[...] = y.astype(xb_ref.dtype)


def embed_norm(x3, meta, w):
    bsz, seq, d = x3.shape
    front = meta.shape[0]
    total = seq + front
    tm = _row_tile(total)
    per = total // tm
    assert front % 16 == 0 and tm > front
    blk = pl.BlockSpec((tm, d), lambda b, j: (b * per + j, 0))
    return pl.pallas_call(
        functools.partial(_embed_norm_kernel, front=front),
        grid=(bsz, per),
        in_specs=[pl.BlockSpec((pl.Element(tm), pl.Element(d)),
                               lambda b, j: (pl.multiple_of(b * seq + jnp.maximum(j * tm - front, 0), 16), 0)),
                  pl.BlockSpec((front, d), lambda b, j: (0, 0)),
                  pl.BlockSpec((1, d), lambda b, j: (0, 0))],
        out_specs=[blk, blk],
        out_shape=[jax.ShapeDtypeStruct((bsz * total, d), F32), jax.ShapeDtypeStruct((bsz * total, d), BF16)],
        compiler_params=_params("parallel", "parallel"),
        name="embed_norm",
    )(x3.reshape(bsz * seq, d), meta, w.reshape(1, d))


def _act(x, act):
    if act == "tanh":
        return jnp.tanh(x)
    if act == "sigmoid":
        return jax.nn.sigmoid(x)
    assert act is None
    return x


def _mm_kernel(*refs, n_a, act, epilogue):
    a_refs = refs[:n_a]
    w_refs = refs[n_a:2 * n_a]
    rest = refs[2 * n_a:]
    o_ref = rest[-1]
    acc = jnp.dot(a_refs[0][...].astype(BF16), w_refs[0][...], preferred_element_type=F32)
    for a_ref, w_ref in zip(a_refs[1:], w_refs[1:]):
        acc = acc + jnp.dot(a_ref[...].astype(BF16), w_ref[...], preferred_element_type=F32)
    acc = _act(acc, act)
    if epilogue == "residual":
        acc = rest[0][...] + acc
    elif epilogue == "glu":
        acc = rest[0][...] * jax.nn.sigmoid(acc)
    o_ref[...] = acc.astype(o_ref.dtype)


MM_VMEM_BUDGET = 40 * 1024 * 1024
MXU_WIDTH = 256


def _mm_tiles(t, k_total, n, out_bytes, has_extra):
    rows = [d for d in range(16, t + 1, 16) if t % d == 0]
    cols = [d for d in range(LANES, n + 1, LANES) if n % d == 0] or [n]
    best, best_score = None, -1.0
    for tm in rows:
        for tn in cols:
            est = 4 * tm * k_total + 4 * k_total * tn + tm * tn * (2 * out_bytes + 4 + (8 if has_extra else 0))
            if est > MM_VMEM_BUDGET:
                continue
            score = tm * tn * (1.0 if tn % MXU_WIDTH == 0 else 0.8)
            if score > best_score:
                best, best_score = (tm, tn), score
    assert best is not None, (t, k_total, n)
    return best


def matmul(a_list, w_list, out_dtype, act=None, epilogue=None, extra=None):
    t = a_list[0].shape[0]
    n = w_list[0][0].shape[2] if isinstance(w_list[0], tuple) else w_list[0].shape[1]
    k_total = sum(a.shape[1] * a.dtype.itemsize // 2 for a in a_list)
    tm, tn = _mm_tiles(t, k_total, n, jnp.dtype(out_dtype).itemsize, epilogue is not None)
    in_specs = [pl.BlockSpec((tm, a.shape[1]), lambda i, j: (i, 0)) for a in a_list]
    args = list(a_list)
    for a, w in zip(a_list, w_list):
        if isinstance(w, tuple):
            w, layer, kblk = w
            in_specs.append(pl.BlockSpec((None, a.shape[1], tn), lambda i, j, layer=layer, kblk=kblk: (layer, kblk, j)))
        else:
            in_specs.append(pl.BlockSpec((w.shape[0], tn), lambda i, j: (0, j)))
        args.append(w)
    if epilogue is not None:
        in_specs.append(pl.BlockSpec((tm, tn), lambda i, j: (i, j)))
        args.append(extra)
    return pl.pallas_call(
        functools.partial(_mm_kernel, n_a=len(a_list), act=act, epilogue=epilogue),
        grid=(t // tm, n // tn),
        in_specs=in_specs,
        out_specs=pl.BlockSpec((tm, tn), lambda i, j: (i, j)),
        out_shape=jax.ShapeDtypeStruct((t, n), out_dtype),
        compiler_params=_params("parallel", "parallel"),
        name="matmul",
    )(*args)


def _mm_resid_split_kernel(a_ref, w_ref, res_ref, o_ref):
    part = jnp.dot(a_ref[...], w_ref[...], preferred_element_type=F32)

    @pl.when(pl.program_id(2) == 0)
    def _():
        o_ref[...] = res_ref[...] + part

    @pl.when(pl.program_id(2) > 0)
    def _():
        o_ref[...] += part


def matmul_residual_split(a, w, layer, res, k_steps):
    t, k = a.shape
    n = w.shape[2]
    tk = k // k_steps
    assert k % k_steps == 0 and tk % LANES == 0
    tm, tn = _mm_tiles(t, tk, n, 4, True)
    return pl.pallas_call(
        _mm_resid_split_kernel,
        grid=(t // tm, n // tn, k_steps),
        in_specs=[pl.BlockSpec((tm, tk), lambda i, j, s: (i, s)),
                  pl.BlockSpec((None, tk, tn), lambda i, j, s: (layer, s, j)),
                  pl.BlockSpec((tm, tn), lambda i, j, s: (i, j))],
        out_specs=pl.BlockSpec((tm, tn), lambda i, j, s: (i, j)),
        out_shape=jax.ShapeDtypeStruct((t, n), F32),
        compiler_params=_params("parallel", "parallel", "arbitrary"),
        name="matmul_split",
    )(a, w, res)


def _s5_prep_kernel(lr_ref, li_ref, ls_ref, brt_ref, bit_ref, pwr_ref, pwi_ref, bbr_ref, bbi_ref):
    lr = jnp.minimum(lr_ref[...], -1e-4)
    li = li_ref[...]
    dt = jnp.exp(ls_ref[...])
    n = lax.broadcasted_iota(jnp.int32, (SUBLANES, S5_CH), 0).astype(F32) + 1.0
    mag = jnp.exp(n * (lr * dt))
    ang = n * (li * dt)
    pwr = mag * jnp.cos(ang)
    pwi = mag * jnp.sin(ang)
    pwr_ref[...] = pwr
    pwi_ref[...] = pwi
    ar = pwr[0:1]
    ai = pwi[0:1]
    den = lr * lr + li * li
    zr = ((ar - 1.0) * lr + ai * li) / den
    zi = (ai * lr - (ar - 1.0) * li) / den
    br = brt_ref[...]
    bi = bit_ref[...]
    bbr_ref[...] = zr * br - zi * bi
    bbi_ref[...] = zr * bi + zi * br


def s5_prep(lam_re, lam_im, log_step, b_re, b_im):
    lr = lam_re.reshape(1, S5_CH)
    li = lam_im.reshape(1, S5_CH)
    ls = jnp.broadcast_to(log_step[:, None], (S5_GROUPS, S5_STATE)).reshape(1, S5_CH)
    brt = b_re.reshape(S5_CH, S5_GROUP).T
    bit = b_im.reshape(S5_CH, S5_GROUP).T
    return pl.pallas_call(
        _s5_prep_kernel,
        out_shape=[jax.ShapeDtypeStruct((SUBLANES, S5_CH), F32)] * 2
        + [jax.ShapeDtypeStruct((S5_GROUP, S5_CH), F32)] * 2,
        name="s5_prep",
    )(lr, li, ls, brt, bit)


S5_BLK_GROUPS = LANES // S5_GROUP
S5_BLKS = S5_WIDTH // LANES
S5_BLK_CH = S5_BLK_GROUPS * S5_STATE


def _cmul_add(xr, xi, mr, mi, sr, si):
    return xr + mr * sr - mi * si, xi + mr * si + mi * sr


def _s5_kernel(u_ref, h0r_ref, h0i_ref, pwr_ref, pwi_ref, wbr_ref, wbi_ref, wcr_ref, wci_ref, d_ref,
               y_ref, hr_ref, hi_ref, xr_scr, xi_scr, *, nb, seq):
    u2 = u_ref[...].reshape(nb * seq, LANES)
    ub = u2.astype(BF16)
    xr_scr[...] = jnp.dot(ub, wbr_ref[0], preferred_element_type=F32).reshape(nb, seq, S5_BLK_CH)
    xi_scr[...] = jnp.dot(ub, wbi_ref[0], preferred_element_type=F32).reshape(nb, seq, S5_BLK_CH)

    pwr = pwr_ref[...]
    pwi = pwi_ref[...]
    row = lax.broadcasted_iota(jnp.int32, (SUBLANES, S5_BLK_CH), 0)
    steps = []
    for d in (1, 2, 4):
        keep = row >= d
        steps.append((d, jnp.where(keep, pwr[d - 1:d], 0.0)[None], jnp.where(keep, pwi[d - 1:d], 0.0)[None]))
    pr = pwr[None]
    pi = pwi[None]

    def tile(i, carry):
        cr, ci = carry
        o = pl.multiple_of(i * SUBLANES, SUBLANES)
        xr = xr_scr[:, pl.ds(o, SUBLANES), :]
        xi = xi_scr[:, pl.ds(o, SUBLANES), :]
        for d, mr, mi in steps:
            sr = pltpu.roll(xr, d, axis=1)
            si = pltpu.roll(xi, d, axis=1)
            xr, xi = _cmul_add(xr, xi, mr, mi, sr, si)
        xr, xi = _cmul_add(xr, xi, pr, pi, cr, ci)
        xr_scr[:, pl.ds(o, SUBLANES), :] = xr
        xi_scr[:, pl.ds(o, SUBLANES), :] = xi
        return xr[:, SUBLANES - 1:SUBLANES, :], xi[:, SUBLANES - 1:SUBLANES, :]

    hr, hi = lax.fori_loop(0, seq // SUBLANES, tile, (h0r_ref[...], h0i_ref[...]))
    hr_ref[...] = hr
    hi_ref[...] = hi

    xr = xr_scr[...].reshape(nb * seq, S5_BLK_CH).astype(BF16)
    xi = xi_scr[...].reshape(nb * seq, S5_BLK_CH).astype(BF16)
    y = (jnp.dot(xr, wcr_ref[0], preferred_element_type=F32)
         - jnp.dot(xi, wci_ref[0], preferred_element_type=F32)
         + d_ref[...] * u2)
    y_ref[...] = jax.nn.gelu(y).reshape(nb, seq, LANES)


def s5_scan(z3, h0r, h0i, pwr, pwi, wbr, wbi, wcr, wci, d, nb):
    bsz, seq, _ = z3.shape
    assert seq % SUBLANES == 0 and bsz % nb == 0
    seq_blk = pl.BlockSpec((nb, seq, LANES), lambda b, k: (b, 0, k))
    st_blk = pl.BlockSpec((nb, 1, S5_BLK_CH), lambda b, k: (b, 0, k))
    pw_blk = pl.BlockSpec((SUBLANES, S5_BLK_CH), lambda b, k: (0, k))
    wb_blk = pl.BlockSpec((1, LANES, S5_BLK_CH), lambda b, k: (k, 0, 0))
    wc_blk = pl.BlockSpec((1, S5_BLK_CH, LANES), lambda b, k: (k, 0, 0))
    y, hr, hi = pl.pallas_call(
        functools.partial(_s5_kernel, nb=nb, seq=seq),
        grid=(bsz // nb, S5_BLKS),
        in_specs=[seq_blk, st_blk, st_blk, pw_blk, pw_blk, wb_blk, wb_blk, wc_blk, wc_blk,
                  pl.BlockSpec((1, LANES), lambda b, k: (0, k))],
        out_specs=[seq_blk, st_blk, st_blk],
        out_shape=[jax.ShapeDtypeStruct((bsz, seq, S5_WIDTH), F32),
                   jax.ShapeDtypeStruct((bsz, 1, S5_CH), F32),
                   jax.ShapeDtypeStruct((bsz, 1, S5_CH), F32)],
        scratch_shapes=[pltpu.VMEM((nb, seq, S5_BLK_CH), F32), pltpu.VMEM((nb, seq, S5_BLK_CH), F32)],
        compiler_params=_params("parallel", "parallel"),
        name="s5_scan",
    )(z3, h0r.reshape(bsz, 1, S5_CH), h0i.reshape(bsz, 1, S5_CH), pwr, pwi, wbr, wbi, wcr, wci, d)
    return y, hr.reshape(bsz, S5_GROUPS, S5_STATE), hi.reshape(bsz, S5_GROUPS, S5_STATE)


S5_LANE_TILES = S5_BLK_CH // LANES


def _s5_long_kernel(u_ref, h0r_ref, h0i_ref, ar_ref, ai_ref, wbr_ref, wbi_ref, wcr_ref, wci_ref, d_ref,
                    y_ref, hr_ref, hi_ref, xr_scr, xi_scr, cr_scr, ci_scr, *, tl):
    step = pl.program_id(1)

    @pl.when(step == 0)
    def _():
        cr_scr[...] = h0r_ref[0]
        ci_scr[...] = h0i_ref[0]

    for k in range(S5_BLKS):
        ub = u_ref[0, :, k * LANES:(k + 1) * LANES].astype(BF16)
        bur = jnp.dot(ub, wbr_ref[k], preferred_element_type=F32)
        bui = jnp.dot(ub, wbi_ref[k], preferred_element_type=F32)
        for j in range(S5_LANE_TILES):
            xr_scr[j, k * tl:(k + 1) * tl, :] = bur[:, j * LANES:(j + 1) * LANES]
            xi_scr[j, k * tl:(k + 1) * tl, :] = bui[:, j * LANES:(j + 1) * LANES]

    ar = [ar_ref[:, j * LANES:(j + 1) * LANES] for j in range(S5_LANE_TILES)]
    ai = [ai_ref[:, j * LANES:(j + 1) * LANES] for j in range(S5_LANE_TILES)]

    def token(t, carry):
        cr, ci = carry
        nr, ni = [], []
        for j in range(S5_LANE_TILES):
            rows = pl.ds(t, S5_BLKS, stride=tl)
            xr, xi = _cmul_add(xr_scr[j, rows, :], xi_scr[j, rows, :], ar[j], ai[j], cr[j], ci[j])
            xr_scr[j, rows, :] = xr
            xi_scr[j, rows, :] = xi
            nr.append(xr)
            ni.append(xi)
        return tuple(nr), tuple(ni)

    init = (tuple(cr_scr[:, j * LANES:(j + 1) * LANES] for j in range(S5_LANE_TILES)),
            tuple(ci_scr[:, j * LANES:(j + 1) * LANES] for j in range(S5_LANE_TILES)))
    cr, ci = lax.fori_loop(0, tl, token, init, unroll=8)
    cr = jnp.concatenate(cr, axis=-1)
    ci = jnp.concatenate(ci, axis=-1)
    cr_scr[...] = cr
    ci_scr[...] = ci
    hr_ref[0] = cr
    hi_ref[0] = ci

    for k in range(S5_BLKS):
        xr = jnp.concatenate([xr_scr[j, k * tl:(k + 1) * tl, :] for j in range(S5_LANE_TILES)], axis=-1)
        xi = jnp.concatenate([xi_scr[j, k * tl:(k + 1) * tl, :] for j in range(S5_LANE_TILES)], axis=-1)
        u = u_ref[0, :, k * LANES:(k + 1) * LANES]
        y = (jnp.dot(xr.astype(BF16), wcr_ref[k], preferred_element_type=F32)
             - jnp.dot(xi.astype(BF16), wci_ref[k], preferred_element_type=F32)
             + d_ref[:, k * LANES:(k + 1) * LANES] * u)
        y_ref[0, :, k * LANES:(k + 1) * LANES] = jax.nn.gelu(y)


def s5_scan_long(z3, h0r, h0i, pwr, pwi, wbr, wbi, wcr, wci, d, tl):
    bsz, seq, _ = z3.shape
    assert seq % tl == 0 and tl % SUBLANES == 0
    seq_blk = pl.BlockSpec((1, tl, S5_WIDTH), lambda b, t: (b, t, 0))
    st_blk = pl.BlockSpec((1, S5_BLKS, S5_BLK_CH), lambda b, t: (b, 0, 0))
    lam_blk = pl.BlockSpec((S5_BLKS, S5_BLK_CH), lambda b, t: (0, 0))
    wb_blk = pl.BlockSpec((S5_BLKS, LANES, S5_BLK_CH), lambda b, t: (0, 0, 0))
    wc_blk = pl.BlockSpec((S5_BLKS, S5_BLK_CH, LANES), lambda b, t: (0, 0, 0))
    rows = pltpu.VMEM((S5_LANE_TILES, S5_BLKS * tl, LANES), F32)
    carry = pltpu.VMEM((S5_BLKS, S5_BLK_CH), F32)
    y, hr, hi = pl.pallas_call(
        functools.partial(_s5_long_kernel, tl=tl),
        grid=(bsz, seq // tl),
        in_specs=[seq_blk, st_blk, st_blk, lam_blk, lam_blk, wb_blk, wb_blk, wc_blk, wc_blk,
                  pl.BlockSpec((1, S5_WIDTH), lambda b, t: (0, 0))],
        out_specs=[seq_blk, st_blk, st_blk],
        out_shape=[jax.ShapeDtypeStruct((bsz, seq, S5_WIDTH), F32),
                   jax.ShapeDtypeStruct((bsz, S5_BLKS, S5_BLK_CH), F32),
                   jax.ShapeDtypeStruct((bsz, S5_BLKS, S5_BLK_CH), F32)],
        scratch_shapes=[rows, rows, carry, carry],
        compiler_params=_params("parallel", "arbitrary"),
        name="s5_scan_long",
    )(z3, h0r.reshape(bsz, S5_BLKS, S5_BLK_CH), h0i.reshape(bsz, S5_BLKS, S5_BLK_CH),
      pwr[0].reshape(S5_BLKS, S5_BLK_CH), pwi[0].reshape(S5_BLKS, S5_BLK_CH), wbr, wbi, wcr, wci, d)
    return y, hr.reshape(bsz, S5_GROUPS, S5_STATE), hi.reshape(bsz, S5_GROUPS, S5_STATE)


def _s5_block_weights(bbr_t, bbi_t, c_re, c_im):
    eye = jnp.eye(S5_BLK_GROUPS, dtype=F32)

    def wb(bt):
        b4 = bt.reshape(S5_GROUP, S5_BLKS, S5_BLK_GROUPS, S5_STATE)
        w = jnp.einsum("cbgp,hg->bhcgp", b4, eye)
        return w.reshape(S5_BLKS, LANES, S5_BLK_CH).astype(BF16)

    def wc(c):
        c4 = c.reshape(S5_BLKS, S5_BLK_GROUPS, S5_GROUP, S5_STATE)
        w = jnp.einsum("bgcp,hg->bhpgc", c4, eye)
        return w.reshape(S5_BLKS, S5_BLK_CH, LANES).astype(BF16)

    return wb(bbr_t), wb(bbi_t), wc(c_re), wc(c_im)


def _hg_lb_kernel(x_ref, o_ref):
    x = x_ref[...]
    e = jnp.exp(x - jnp.max(x, axis=0, keepdims=True))
    sm = e / jnp.sum(e, axis=0, keepdims=True)
    acc = sm[0:1]
    o_ref[0:1, :] = acc
    for l in range(1, x.shape[0]):
        acc = acc + sm[l:l + 1]
        o_ref[l:l + 1, :] = acc


def hg_lower_bounds(hg_lb):
    return pl.pallas_call(_hg_lb_kernel, out_shape=jax.ShapeDtypeStruct(hg_lb.shape, F32), name="hg_lb")(hg_lb)


def _cumsum_rows(x, n):
    row = lax.broadcasted_iota(jnp.int32, x.shape, 1)
    d = 1
    while d < n:
        x = x + jnp.where(row >= d, pltpu.roll(x, d, axis=1), 0.0)
        d *= 2
    return x


def _hgrn_kernel(q_ref, f_ref, i_ref, g_ref, s0_ref, lb_ref, nw_ref, o_ref, sf_ref, st_scr, *, nb, hh, chunk):
    step = pl.program_id(2)

    def units(x):
        return jnp.concatenate([x[:, :, h * LANES:(h + 1) * LANES] for h in range(hh)], axis=0)

    lb = units(jnp.broadcast_to(lb_ref[...][None], (nb, 1, hh * LANES)))
    nw = nw_ref[...][None]

    @pl.when(step == 0)
    def _():
        for h in range(hh):
            for b in range(nb):
                st_scr[h * nb + b] = s0_ref[b, h].T

    trow = lax.broadcasted_iota(jnp.int32, (chunk, chunk), 0)
    tcol = lax.broadcasted_iota(jnp.int32, (chunk, chunk), 1)
    causal = (tcol <= trow)[None]
    q, f, v, g = (units(ref[...]) for ref in (q_ref, f_ref, i_ref, g_ref))
    fg = lb + (1.0 - lb) * jax.nn.sigmoid(f)
    qh = jax.nn.silu(q)
    kh = 1.0 - fg
    bcum = _cumsum_rows(jnp.log(fg), chunk)
    btot = bcum[:, chunk - 1:chunk, :]
    q_in = (qh * jnp.exp(bcum)).astype(BF16)
    k_in = (kh * jnp.exp(-bcum)).astype(BF16)
    k_end = (kh * jnp.exp(btot - bcum)).astype(BF16)
    decay = jnp.exp(btot)
    vb = v.astype(BF16)
    st = st_scr[...]
    att = jnp.einsum("utk,usk->uts", q_in, k_in, preferred_element_type=F32)
    att = jnp.where(causal, att, 0.0).astype(BF16)
    out = (jnp.einsum("utk,uvk->utv", q_in, st.astype(BF16), preferred_element_type=F32)
           + jnp.einsum("uts,usv->utv", att, vb, preferred_element_type=F32))
    st = st * decay + jnp.einsum("usv,usk->uvk", vb, k_end, preferred_element_type=F32)
    st_scr[...] = st
    out = out * lax.rsqrt(jnp.mean(out * out, axis=-1, keepdims=True) + EPS) * nw
    out = (out * jax.nn.silu(g)).astype(o_ref.dtype)
    for h in range(hh):
        o_ref[:, :, h * LANES:(h + 1) * LANES] = out[h * nb:(h + 1) * nb]

    @pl.when(step == pl.num_programs(2) - 1)
    def _():
        for h in range(hh):
            for b in range(nb):
                sf_ref[b, h] = st[h * nb + b].T


def hgrn2(z3, s0, lb, norm_w, nb, hh):
    bsz, seq, _ = z3.shape
    chunk = min(HG_CHUNK, seq)
    assert seq % chunk == 0 and bsz % nb == 0 and HG_HEADS % hh == 0
    wid = hh * LANES
    n_col = (HG_HEADS * HG_K) // wid

    def col(proj):
        return pl.BlockSpec((nb, chunk, wid), lambda h, b, t, proj=proj: (b, t, proj * n_col + h))

    st_blk = pl.BlockSpec((nb, hh, HG_K, HG_V), lambda h, b, t: (b, h, 0, 0))
    out, sf = pl.pallas_call(
        functools.partial(_hgrn_kernel, nb=nb, hh=hh, chunk=chunk),
        grid=(HG_HEADS // hh, bsz // nb, seq // chunk),
        in_specs=[col(1), col(2), col(3), col(4), st_blk,
                  pl.BlockSpec((1, wid), lambda h, b, t: (0, h)),
                  pl.BlockSpec((1, LANES), lambda h, b, t: (0, 0))],
        out_specs=[pl.BlockSpec((nb, chunk, wid), lambda h, b, t: (b, t, h)), st_blk],
        out_shape=[jax.ShapeDtypeStruct((bsz, seq, HG_HEADS * HG_V), BF16),
                   jax.ShapeDtypeStruct((bsz, HG_HEADS, HG_K, HG_V), F32)],
        scratch_shapes=[pltpu.VMEM((hh * nb, HG_V, HG_K), F32)],
        compiler_params=_params("parallel", "parallel", "arbitrary"),
        name="hgrn2",
    )(z3, z3, z3, z3, s0, lb.reshape(1, HG_HEADS * HG_K), norm_w.reshape(1, HG_V))
    return out, sf


RW_MIXES = 6


def _norm_mix_kernel(h_ref, sh_ref, w_ref, mu_ref, *refs, tl):
    o_refs = refs[:RW_MIXES]
    last_ref, scr = refs[RW_MIXES:]
    x = h_ref[...]
    xn = x * lax.rsqrt(jnp.mean(x * x, axis=-1, keepdims=True) + EPS) * w_ref[...][None]

    @pl.when(pl.program_id(1) == 0)
    def _():
        scr[:, SUBLANES - 1:SUBLANES, :] = sh_ref[...]

    scr[:, SUBLANES:, :] = xn
    xx = scr[:, SUBLANES - 1:SUBLANES - 1 + tl, :] - xn
    for j, o_ref in enumerate(o_refs):
        o_ref[...] = (xn + xx * mu_ref[j:j + 1, :][None]).astype(o_ref.dtype)
    last = xn[:, tl - 1:tl, :]
    scr[:, SUBLANES - 1:SUBLANES, :] = last
    last_ref[...] = last


def norm_mix(h3, shift0, ln_w, mu, nb, tl):
    bsz, seq, d = h3.shape
    assert bsz % nb == 0 and seq % tl == 0 and tl % SUBLANES == 0
    blk = pl.BlockSpec((nb, tl, d), lambda b, t: (b, t, 0))
    row = pl.BlockSpec((nb, 1, d), lambda b, t: (b, 0, 0))
    outs = pl.pallas_call(
        functools.partial(_norm_mix_kernel, tl=tl),
        grid=(bsz // nb, seq // tl),
        in_specs=[blk, row, pl.BlockSpec((1, d), lambda b, t: (0, 0)), pl.BlockSpec((RW_MIXES, d), lambda b, t: (0, 0))],
        out_specs=[blk] * RW_MIXES + [row],
        out_shape=[jax.ShapeDtypeStruct((bsz, seq, d), BF16)] * RW_MIXES + [jax.ShapeDtypeStruct((bsz, 1, d), F32)],
        scratch_shapes=[pltpu.VMEM((nb, SUBLANES + tl, d), F32)],
        compiler_params=_params("parallel", "arbitrary"),
        name="norm_mix",
    )(h3, shift0.reshape(bsz, 1, d), ln_w.reshape(1, d), mu)
    return outs[:RW_MIXES], outs[RW_MIXES].reshape(bsz, d)


RW_PAIR = LANES // RW_HEAD
RW_DECAY_SCALE = math.exp(-0.5)
RW_SOLVE_BLOCK = 8


def _rwkv_kernel(r_ref, k_ref, v_ref, tw_ref, ta_ref, tg_ref, w2_ref, a2_ref, g2_ref, s0_ref,
                 w0_ref, a0_ref, kk_ref, ka_ref, rk_ref, lnw_ref, lnb_ref,
                 o_ref, sf_ref, s_scr, *, nb, hp, chunk):
    nu = hp * nb
    step = pl.program_id(2)
    lane = lax.broadcasted_iota(jnp.int32, (1, 1, LANES), 2)
    head1 = lane >= RW_HEAD

    def units(x):
        return jnp.concatenate([x[:, :, p * LANES:(p + 1) * LANES] for p in range(hp)], axis=0)

    def unit_rows(ref):
        return units(jnp.broadcast_to(ref[...][None], (nb, 1, hp * LANES)))

    w0, a0, k_k, k_a, r_k, ln_w, ln_b = (unit_rows(p) for p in
                                         (w0_ref, a0_ref, kk_ref, ka_ref, rk_ref, lnw_ref, lnb_ref))
    sq_row = lax.broadcasted_iota(jnp.int32, (LANES, LANES), 0) >= RW_HEAD
    sq_col = lax.broadcasted_iota(jnp.int32, (LANES, LANES), 1) >= RW_HEAD
    same_head = sq_row == sq_col
    ones_bd = same_head.astype(BF16)

    @pl.when(step == 0)
    def _():
        zero = jnp.zeros((nb, RW_HEAD, RW_HEAD), F32)
        for p in range(hp):
            top = jnp.concatenate([s0_ref[:, RW_PAIR * p], zero], axis=-1)
            bot = jnp.concatenate([zero, s0_ref[:, RW_PAIR * p + 1]], axis=-1)
            s_scr[p * nb:(p + 1) * nb] = jnp.concatenate([top, bot], axis=1)

    def bdot(spec, a, b):
        return jnp.einsum(spec, a.astype(BF16), b.astype(BF16), preferred_element_type=F32)

    def head_sum(x, two_pass=True):
        x2 = x.reshape(nu * chunk, LANES)
        hi = x2.astype(BF16)
        s = jnp.dot(hi, ones_bd, preferred_element_type=F32)
        if two_pass:
            lo = (x2 - hi.astype(F32)).astype(BF16)
            s = s + jnp.dot(lo, ones_bd, preferred_element_type=F32)
        return s.reshape(nu, chunk, LANES)

    def low_rank(t_ref, w_ref):
        t2 = t_ref[...].reshape(nb * chunk, t_ref.shape[-1])
        return units(jnp.dot(t2, w_ref[...], preferred_element_type=F32).reshape(nb, chunk, hp * LANES))

    srow = lax.broadcasted_iota(jnp.int32, (chunk, chunk), 0)
    scol = lax.broadcasted_iota(jnp.int32, (chunk, chunk), 1)
    tri = jnp.broadcast_to((scol <= srow).astype(BF16)[None], (nu, chunk, chunk))

    def tri_sum(x):
        hi = x.astype(BF16)
        lo = (x - hi.astype(F32)).astype(BF16)
        return (jnp.einsum("uts,usc->utc", tri, hi, preferred_element_type=F32)
                + jnp.einsum("uts,usc->utc", tri, lo, preferred_element_type=F32))

    def stack_heads(x):
        return jnp.concatenate([jnp.where(head1, 0.0, x), jnp.where(head1, x, 0.0)], axis=1).astype(BF16)

    r, k, v = (units(ref[...]) for ref in (r_ref, k_ref, v_ref))
    wl, al, g = low_rank(tw_ref, w2_ref), low_rank(ta_ref, a2_ref), low_rank(tg_ref, g2_ref)
    lw = (-RW_DECAY_SCALE) * jax.nn.sigmoid(w0 + wl)
    ag = jax.nn.sigmoid(a0 + al)
    kk = k * k_k
    kk = kk * jnp.minimum(lax.rsqrt(head_sum(kk * kk)), 1e12)
    k2 = k * (1.0 + (ag - 1.0) * k_a)
    cl = tri_sum(lw)
    e_pos = jnp.exp(cl)
    e_neg = jnp.exp(-cl)
    at = (-kk) * jnp.exp(cl - lw)
    bt = (kk * ag) * e_neg
    kt = k2 * e_neg
    rt = r * e_pos
    wc = e_pos[:, chunk - 1:chunk, :]
    trow = lax.broadcasted_iota(jnp.int32, (chunk, RW_PAIR * chunk), 0)
    tcol = lax.broadcasted_iota(jnp.int32, (chunk, RW_PAIR * chunk), 1)
    tcol = jnp.where(tcol >= chunk, tcol - chunk, tcol)
    strict = (tcol < trow)[None]
    incl = (tcol <= trow)[None]
    x2 = jnp.concatenate([at, rt], axis=1)
    pb = bdot("utc,usc->uts", x2, stack_heads(bt))
    pk = bdot("utc,usc->uts", x2, stack_heads(kt))
    lab = jnp.where(strict, pb[:, :chunk], 0.0)
    lak = jnp.where(strict, pk[:, :chunk], 0.0)
    arb = jnp.where(incl, pb[:, chunk:], 0.0)
    ark = jnp.where(incl, pk[:, chunk:], 0.0)
    v_bd = stack_heads(v)
    xa = at
    xv = bdot("uts,usc->utc", lak, v_bd)
    sub = min(RW_SOLVE_BLOCK, chunk)
    done_a, done_v = [], []
    for lo in range(0, chunk, sub):
        xa_i = xa[:, lo:lo + sub, :]
        xv_i = xv[:, lo:lo + sub, :]
        if lo:
            pad = jnp.zeros((nu, chunk - lo, LANES), F32)
            prev = jnp.concatenate([stack_heads(jnp.concatenate(done_a + [pad], axis=1)),
                                    stack_heads(jnp.concatenate(done_v + [pad], axis=1))], axis=-1)
            upd = bdot("uts,usc->utc", lab[:, lo:lo + sub, :], prev)
            xa_i = xa_i + upd[:, :, :LANES]
            xv_i = xv_i + upd[:, :, LANES:]
        l0 = lab[:, lo:lo + sub, lo:lo + sub]
        l1 = lab[:, lo:lo + sub, chunk + lo:chunk + lo + sub]
        for s in range(sub - 1):
            m = jnp.where(head1, l1[:, :, s:s + 1], l0[:, :, s:s + 1])
            xa_i = xa_i + m * xa_i[:, s:s + 1, :]
            xv_i = xv_i + m * xv_i[:, s:s + 1, :]
        done_a.append(xa_i)
        done_v.append(xv_i)
    ah = jnp.concatenate(done_a, axis=1)
    vh = jnp.concatenate(done_v, axis=1)
    both = bdot("uts,usc->utc", arb, jnp.concatenate([stack_heads(ah), stack_heads(vh)], axis=-1))
    rh = rt + both[:, :, :LANES]
    yh = both[:, :, LANES:] + bdot("uts,usc->utc", ark, v_bd)
    gp = jnp.where(same_head, bdot("utj,utk->ujk", ah, bt), 0.0)
    ht = jnp.where(same_head, bdot("utv,utk->uvk", jnp.concatenate([vh, v], axis=1),
                                   jnp.concatenate([bt, kt], axis=1)), 0.0)
    st = s_scr[...]
    y = bdot("utk,uvk->utv", rh, st) + yh
    st = (st + bdot("uvj,ujk->uvk", st, gp) + ht) * wc
    s_scr[...] = st
    inv_n = 1.0 / RW_HEAD
    yc = y - head_sum(y, two_pass=False) * inv_n
    var = head_sum(yc * yc, two_pass=False) * inv_n
    y = yc * lax.rsqrt(var + RW_GN_EPS) * ln_w + ln_b
    y = y + head_sum(r * k2 * r_k, two_pass=False) * v
    out = (y * g).astype(o_ref.dtype)
    for p in range(hp):
        o_ref[:, :, p * LANES:(p + 1) * LANES] = out[p * nb:(p + 1) * nb]

    @pl.when(step == pl.num_programs(2) - 1)
    def _():
        for p in range(hp):
            sf_ref[:, RW_PAIR * p] = st[p * nb:(p + 1) * nb, :RW_HEAD, :RW_HEAD]
            sf_ref[:, RW_PAIR * p + 1] = st[p * nb:(p + 1) * nb, RW_HEAD:, RW_HEAD:]


def _rwkv_chunk(seq):
    for c in (48, 32, 16, 8):
        if seq % c == 0:
            return c
    raise ValueError(seq)


def rwkv7(r, k, v, low, low_w, s0, w0, a0, k_k, k_a, r_k, ln_w, ln_b, nb, hp):
    bsz, seq, d = r.shape
    chunk = _rwkv_chunk(seq)
    heads = hp * RW_PAIR
    assert bsz % nb == 0 and RW_HEADS % heads == 0
    seq_blk = pl.BlockSpec((nb, chunk, hp * LANES), lambda h, b, t: (b, t, h))
    st_blk = pl.BlockSpec((nb, heads, RW_HEAD, RW_HEAD), lambda h, b, t: (b, h, 0, 0))
    vec = pl.BlockSpec((1, hp * LANES), lambda h, b, t: (0, h))
    low_blk = [pl.BlockSpec((nb, chunk, x.shape[-1]), lambda h, b, t: (b, t, 0)) for x in low]
    low_w_blk = [pl.BlockSpec((w.shape[0], hp * LANES), lambda h, b, t: (0, h)) for w in low_w]
    out, sf = pl.pallas_call(
        functools.partial(_rwkv_kernel, nb=nb, hp=hp, chunk=chunk),
        grid=(RW_HEADS // heads, bsz // nb, seq // chunk),
        in_specs=[seq_blk] * 3 + low_blk + low_w_blk + [st_blk] + [vec] * 7,
        out_specs=[seq_blk, st_blk],
        out_shape=[jax.ShapeDtypeStruct((bsz, seq, d), BF16),
                   jax.ShapeDtypeStruct((bsz, RW_HEADS, RW_HEAD, RW_HEAD), F32)],
        scratch_shapes=[pltpu.VMEM((hp * nb, LANES, LANES), F32)],
        compiler_params=_params("parallel", "parallel", "arbitrary"),
        name="rwkv7",
    )(r, k, v, *low, *low_w, s0, *(p.reshape(1, d) for p in (w0, a0, k_k, k_a, r_k, ln_w, ln_b)))
    return out, sf


def _ffn_in_kernel(x_ref, wa_ref, wv_ref, e_ref, cw_ref, cb_ref, o_ref, st_ref, scr, *, nb, seq, sb, sr):
    tn = wa_ref.shape[1]
    cw = cw_ref[...]
    cb = cb_ref[...][None]
    scr[:, SUBLANES - (CONV_W - 1):SUBLANES, :] = e_ref[...]
    for b0 in range(0, nb, sb):
        for r0 in range(0, seq, sr):
            lo = b0 * seq + r0
            x = x_ref[lo:lo + sb * sr, :]
            a = jnp.dot(x, wa_ref[...], preferred_element_type=F32).reshape(sb, sr, tn)
            v = jnp.dot(x, wv_ref[...], preferred_element_type=F32).reshape(sb, sr, tn)
            scr[b0:b0 + sb, SUBLANES + r0:SUBLANES + r0 + sr, :] = a
            c = cb + cw[CONV_W - 1:CONV_W][None] * a
            for j in range(CONV_W - 1):
                first = SUBLANES + r0 - (CONV_W - 1 - j)
                c = c + cw[j:j + 1][None] * scr[b0:b0 + sb, first:first + sr, :]
            o_ref[lo:lo + sb * sr, :] = (jax.nn.gelu(c) * v).reshape(sb * sr, tn).astype(o_ref.dtype)
    st_ref[...] = scr[:, SUBLANES + seq - (CONV_W - 1):SUBLANES + seq, :]


def ffn_in(xb, conv0, w_in, layer, conv_w, conv_b, bsz, seq, nb, tn, sub):
    t, d = xb.shape
    sb, sr = sub
    assert t == bsz * seq and bsz % nb == 0 and D_FF % tn == 0
    assert nb % sb == 0 and seq % sr == 0 and sr % SUBLANES == 0 and (sr == seq or nb == sb == 1)
    col = lambda i, j: (0, j)
    n_col = D_FF // tn
    out, st = pl.pallas_call(
        functools.partial(_ffn_in_kernel, nb=nb, seq=seq, sb=sb, sr=sr),
        grid=(bsz // nb, n_col),
        in_specs=[pl.BlockSpec((nb * seq, d), lambda i, j: (i, 0)),
                  pl.BlockSpec((None, d, tn), lambda i, j: (layer, 0, j)),
                  pl.BlockSpec((None, d, tn), lambda i, j: (layer, 0, j + n_col)),
                  pl.BlockSpec((nb, CONV_W - 1, tn), lambda i, j: (i, 0, j)),
                  pl.BlockSpec((CONV_W, tn), col), pl.BlockSpec((1, tn), col)],
        out_specs=[pl.BlockSpec((nb * seq, tn), lambda i, j: (i, j)),
                   pl.BlockSpec((nb, CONV_W - 1, tn), lambda i, j: (i, 0, j))],
        out_shape=[jax.ShapeDtypeStruct((t, D_FF), BF16),
                   jax.ShapeDtypeStruct((bsz, CONV_W - 1, D_FF), F32)],
        scratch_shapes=[pltpu.VMEM((nb, SUBLANES + seq, tn), F32)],
        compiler_params=_params("parallel", "parallel"),
        name="ffn_in",
    )(xb, w_in, w_in, conv0, conv_w, conv_b.reshape(1, D_FF))
    return out, st


FFN_DOWN_K_STEPS = 2


def _channel_mixer(h, conv0, layer, p, cfg, bsz, seq):
    (xb,) = rmsnorm(h, p["ln_ffn"][layer], (BF16,))
    gated, n_cv = ffn_in(xb, conv0, p["ffn_w_in"], layer, p["ffn_conv_w"][layer], p["ffn_conv_b"][layer],
                         bsz, seq, cfg["ffn_nb"], cfg["ffn_tn"], cfg["ffn_sub"])
    return matmul_residual_split(gated, p["ffn_w_down"], layer, h, FFN_DOWN_K_STEPS), n_cv


def _trunk(x3, s5r, s5i, hg, rw, sh, cv, p, cfg):
    bsz, seq, d = x3.shape
    seq += cfg["front"]
    t = bsz * seq

    if cfg["front"]:
        h, xb = embed_norm(x3, p["meta"], p["ln_mix"][0])
    else:
        h = x3.reshape(t, d)
        (xb,) = rmsnorm(h, p["ln_mix"][0], (BF16,))
    z = matmul([xb], [p["ev_w_in"]], F32).reshape(bsz, seq, EVEN_IN)
    if cfg["s5_tl"]:
        ys5, n_s5r, n_s5i = s5_scan_long(z, s5r[0], s5i[0], *p["s5"], tl=cfg["s5_tl"])
    else:
        ys5, n_s5r, n_s5i = s5_scan(z, s5r[0], s5i[0], *p["s5"], nb=cfg["s5_nb"])
    ys5 = ys5.reshape(t, S5_WIDTH)
    ya = matmul([ys5], [p["s5_w_glu"]], BF16, epilogue="glu", extra=ys5)
    yb, n_hg = hgrn2(z, hg[0], p["hg_lb"], p["hg_norm_w"], nb=cfg["hg_nb"], hh=cfg["hg_hh"])
    h = matmul([ya, yb.reshape(t, -1)], [(p["ev_w_out"], 0, 0), (p["ev_w_out"], 0, 1)], F32,
               epilogue="residual", extra=h)
    h, n_cv0 = _channel_mixer(h, cv[0], 0, p, cfg, bsz, seq)

    mixes, n_sh = norm_mix(h.reshape(bsz, seq, d), sh[0], p["ln_mix"][1], p["rw_mu"], *cfg["mix_blk"])
    xr, xw, xk, xv, xa, xg = (m.reshape(t, d) for m in mixes)
    r = matmul([xr], [p["rw_w_r"]], F32)
    k = matmul([xk], [p["rw_w_k"]], F32)
    v = matmul([xv], [p["rw_w_v"]], F32)
    low = (matmul([xw], [p["rw_w1"]], BF16, act="tanh"), matmul([xa], [p["rw_a1"]], BF16),
           matmul([xg], [p["rw_g1"]], BF16, act="sigmoid"))
    as3 = lambda a: a.reshape(bsz, seq, a.shape[-1])
    yo, n_rw = rwkv7(as3(r), as3(k), as3(v), [as3(x) for x in low], (p["rw_w2"], p["rw_a2"], p["rw_g2"]),
                     rw[0], *p["rw_vec"], nb=cfg["rw_nb"], hp=cfg["rw_hp"])
    h = matmul([yo.reshape(t, d)], [p["rw_w_o"]], F32, epilogue="residual", extra=h)
    h, n_cv1 = _channel_mixer(h, cv[1], 1, p, cfg, bsz, seq)

    (y,) = rmsnorm(h, p["ln_final"], (F32,), rows=(bsz, seq, cfg["front"]) if cfg["front"] else None)
    return (y.reshape(bsz, seq - cfg["front"], d), n_s5r[None], n_s5i[None], n_hg[None], n_rw[None], n_sh[None],
            jnp.stack([n_cv0, n_cv1]))


PROMPT_CFG = dict(front=N_META, s5_tl=344, s5_nb=None, hg_nb=4, hg_hh=8, mix_blk=(1, 344), rw_nb=4, rw_hp=16, ffn_nb=1, ffn_tn=512, ffn_sub=(1, 688))
SAMPLE_CFG = dict(front=0, s5_tl=None, s5_nb=32, hg_nb=4, hg_hh=8, mix_blk=(32, 8), rw_nb=16, rw_hp=4, ffn_nb=128, ffn_tn=512, ffn_sub=(32, 8))


def kernel(x_prompt, x_sample, state_s5_re, state_s5_im, state_hgrn, state_rwkv, state_shift, state_conv, meta_tokens, ln_mix, ln_ffn, ln_final, ev_w_in, ev_w_out, s5_lam_re, s5_lam_im, s5_log_step, s5_b_re, s5_b_im, s5_c_re, s5_c_im, s5_d, s5_w_glu, hg_lb, hg_norm_w, rw_mu, rw_w0, rw_w1, rw_w2, rw_a0, rw_a1, rw_a2, rw_g1, rw_g2, rw_k_k, rw_k_a, rw_r_k, rw_w_r, rw_w_k, rw_w_v, rw_w_o, rw_ln_w, rw_ln_b, ffn_w_in, ffn_conv_w, ffn_conv_b, ffn_w_down):
    bf = lambda w: w.astype(BF16)
    lb_all = hg_lower_bounds(hg_lb)
    pwr, pwi, bbr_t, bbi_t = s5_prep(s5_lam_re[0], s5_lam_im[0], s5_log_step[0], s5_b_re[0], s5_b_im[0])
    wbr, wbi, wcr, wci = _s5_block_weights(bbr_t, bbi_t, s5_c_re[0], s5_c_im[0])
    p = {
        "meta": meta_tokens, "ln_mix": ln_mix, "ln_ffn": ln_ffn, "ln_final": ln_final,
        "ev_w_in": bf(ev_w_in[0]),
        "ev_w_out": bf(ev_w_out),
        "s5": (pwr, pwi, wbr, wbi, wcr, wci, s5_d[0].reshape(1, S5_WIDTH)),
        "s5_w_glu": bf(s5_w_glu[0]),
        "hg_lb": lb_all[0], "hg_norm_w": hg_norm_w[0],
        "rw_mu": rw_mu[0],
        "rw_w1": bf(rw_w1[0]), "rw_w2": bf(rw_w2[0]), "rw_a1": bf(rw_a1[0]), "rw_a2": bf(rw_a2[0]),
        "rw_g1": bf(rw_g1[0]), "rw_g2": bf(rw_g2[0]),
        "rw_w_r": bf(rw_w_r[0]), "rw_w_k": bf(rw_w_k[0]), "rw_w_v": bf(rw_w_v[0]), "rw_w_o": bf(rw_w_o[0]),
        "rw_vec": (rw_w0[0], rw_a0[0], rw_k_k[0], rw_k_a[0], rw_r_k[0].reshape(D_MODEL), rw_ln_w[0], rw_ln_b[0]),
        "ffn_w_in": bf(ffn_w_in), "ffn_conv_w": ffn_conv_w, "ffn_conv_b": ffn_conv_b,
        "ffn_w_down": bf(ffn_w_down),
    }

    bsz = x_prompt.shape[0]
    zeros = lambda *s: jnp.zeros(s, F32)
    outs_p = _trunk(x_prompt,
                    zeros(1, bsz, S5_GROUPS, S5_STATE), zeros(1, bsz, S5_GROUPS, S5_STATE),
                    zeros(1, bsz, HG_HEADS, HG_K, HG_V), zeros(1, bsz, RW_HEADS, RW_HEAD, RW_HEAD),
                    zeros(1, bsz, D_MODEL), zeros(2, bsz, CONV_W - 1, D_FF), p, PROMPT_CFG)
    outs_s = _trunk(x_sample, state_s5_re, state_s5_im, state_hgrn, state_rwkv, state_shift, state_conv,
                    p, SAMPLE_CFG)
    return tuple(outs_p[:1]) + tuple(outs_s[:1]) + tuple(outs_p[1:]) + tuple(outs_s[1:])
```

```python
import functools
import math

import jax
import jax.numpy as jnp
from jax import lax
from jax.experimental import pallas as pl
from jax.experimental.pallas import tpu as pltpu

F32 = jnp.float32
BF16 = jnp.bfloat16

D_MODEL = 2048
N_META = 16
EPS = 1e-6
S5_WIDTH = 1024
S5_GROUP = 16
S5_GROUPS = 64
S5_STATE = 64
S5_CH = S5_GROUPS * S5_STATE
HG_HEADS = 8
HG_K = 128
HG_V = 128
HG_CHUNK = 16
EVEN_IN = 5120
RW_HEAD = 64
RW_HEADS = 32
RW_GN_EPS = 64e-5
D_FF = 5632
CONV_W = 3

LANES = 128
SUBLANES = 8
VMEM_LIMIT = 56 * 1024 * 1024


def _params(*sem):
    return pltpu.CompilerParams(dimension_semantics=sem, vmem_limit_bytes=VMEM_LIMIT)


def _row_tile(t, cap=1024):
    best = None
    for d in range(16, min(t, cap) + 1, 16):
        if t % d == 0:
            best = d
    assert best is not None, t
    return best


def _rms_kernel(x_ref, w_ref, *o_refs):
    x = x_ref[...]
    y = x * lax.rsqrt(jnp.mean(x * x, axis=-1, keepdims=True) + EPS) * w_ref[...]
    for o_ref in o_refs:
        o_ref[...] = y.astype(o_ref.dtype)


def rmsnorm(x, w, dtypes, rows=None):
    t, d = x.shape
    if rows is None:
        tm = _row_tile(t)
        n_out = t
        grid = (t // tm,)
        in_spec = pl.BlockSpec((tm, d), lambda i: (i, 0))
        out_spec = in_spec
        vec = pl.BlockSpec((1, d), lambda i: (0, 0))
    else:
        bsz, seq, front = rows
        keep = seq - front
        assert t == bsz * seq and front % 16 == 0
        tm = _row_tile(keep)
        per = keep // tm
        n_out = bsz * keep
        grid = (bsz, per)
        in_spec = pl.BlockSpec((pl.Element(tm), pl.Element(d)),
                               lambda b, i: (pl.multiple_of(b * seq + front + i * tm, 16), 0))
        out_spec = pl.BlockSpec((tm, d), lambda b, i: (b * per + i, 0))
        vec = pl.BlockSpec((1, d), lambda b, i: (0, 0))
    outs = pl.pallas_call(
        _rms_kernel,
        grid=grid,
        in_specs=[in_spec, vec],
        out_specs=[out_spec for _ in dtypes],
        out_shape=[jax.ShapeDtypeStruct((n_out, d), dt) for dt in dtypes],
        compiler_params=_params(*(["parallel"] * len(grid))),
        name="rmsnorm",
    )(x, w.reshape(1, d))
    return outs


def _embed_norm_kernel(x_ref, m_ref, w_ref, h_ref, xb_ref, *, front):
    x = x_ref[...]
    first = pl.program_id(1) == 0
    body = jnp.where(first, pltpu.roll(x, front, axis=0), x)
    head = jnp.where(first, m_ref[...], x[:front])
    rows = jnp.concatenate([head, body[front:]], axis=0)
    h_ref[...] = rows
    y = rows * lax.rsqrt(jnp.mean(rows * rows, axis=-1, keepdims=True) + EPS) * w_ref[...]
    xb_ref[...] = y.astype(xb_ref.dtype)


def embed_norm(x3, meta, w):
    bsz, seq, d = x3.shape
    front = meta.shape[0]
    total = seq + front
    tm = _row_tile(total)
    per = total // tm
    assert front % 16 == 0 and tm > front
    blk = pl.BlockSpec((tm, d), lambda b, j: (b * per + j, 0))
    return pl.pallas_call(
        functools.partial(_embed_norm_kernel, front=front),
        grid=(bsz, per),
        in_specs=[pl.BlockSpec((pl.Element(tm), pl.Element(d)),
                               lambda b, j: (pl.multiple_of(b * seq + jnp.maximum(j * tm - front, 0), 16), 0)),
                  pl.BlockSpec((front, d), lambda b, j: (0, 0)),
                  pl.BlockSpec((1, d), lambda b, j: (0, 0))],
        out_specs=[blk, blk],
        out_shape=[jax.ShapeDtypeStruct((bsz * total, d), F32), jax.ShapeDtypeStruct((bsz * total, d), BF16)],
        compiler_params=_params("parallel", "parallel"),
        name="embed_norm",
    )(x3.reshape(bsz * seq, d), meta, w.reshape(1, d))


def _act(x, act):
    if act == "tanh":
        return jnp.tanh(x)
    if act == "sigmoid":
        return jax.nn.sigmoid(x)
    assert act is None
    return x


def _mm_kernel(*refs, n_a, act, epilogue):
    a_refs = refs[:n_a]
    w_refs = refs[n_a:2 * n_a]
    rest = refs[2 * n_a:]
    o_ref = rest[-1]
    acc = jnp.dot(a_refs[0][...].astype(BF16), w_refs[0][...], preferred_element_type=F32)
    for a_ref, w_ref in zip(a_refs[1:], w_refs[1:]):
        acc = acc + jnp.dot(a_ref[...].astype(BF16), w_ref[...], preferred_element_type=F32)
    acc = _act(acc, act)
    if epilogue == "residual":
        acc = rest[0][...] + acc
    elif epilogue == "glu":
        acc = rest[0][...] * jax.nn.sigmoid(acc)
    o_ref[...] = acc.astype(o_ref.dtype)


MM_VMEM_BUDGET = 40 * 1024 * 1024
MXU_WIDTH = 256


def _mm_tiles(t, k_total, n, out_bytes, has_extra):
    rows = [d for d in range(16, t + 1, 16) if t % d == 0]
    cols = [d for d in range(LANES, n + 1, LANES) if n % d == 0] or [n]
    best, best_score = None, -1.0
    for tm in rows:
        for tn in cols:
            est = 4 * tm * k_total + 4 * k_total * tn + tm * tn * (2 * out_bytes + 4 + (8 if has_extra else 0))
            if est > MM_VMEM_BUDGET:
                continue
            score = tm * tn * (1.0 if tn % MXU_WIDTH == 0 else 0.8)
            if score > best_score:
                best, best_score = (tm, tn), score
    assert best is not None, (t, k_total, n)
    return best


def matmul(a_list, w_list, out_dtype, act=None, epilogue=None, extra=None):
    t = a_list[0].shape[0]
    n = w_list[0][0].shape[2] if isinstance(w_list[0], tuple) else w_list[0].shape[1]
    k_total = sum(a.shape[1] * a.dtype.itemsize // 2 for a in a_list)
    tm, tn = _mm_tiles(t, k_total, n, jnp.dtype(out_dtype).itemsize, epilogue is not None)
    in_specs = [pl.BlockSpec((tm, a.shape[1]), lambda i, j: (i, 0)) for a in a_list]
    args = list(a_list)
    for a, w in zip(a_list, w_list):
        if isinstance(w, tuple):
            w, layer, kblk = w
            in_specs.append(pl.BlockSpec((None, a.shape[1], tn), lambda i, j, layer=layer, kblk=kblk: (layer, kblk, j)))
        else:
            in_specs.append(pl.BlockSpec((w.shape[0], tn), lambda i, j: (0, j)))
        args.append(w)
    if epilogue is not None:
        in_specs.append(pl.BlockSpec((tm, tn), lambda i, j: (i, j)))
        args.append(extra)
    return pl.pallas_call(
        functools.partial(_mm_kernel, n_a=len(a_list), act=act, epilogue=epilogue),
        grid=(t // tm, n // tn),
        in_specs=in_specs,
        out_specs=pl.BlockSpec((tm, tn), lambda i, j: (i, j)),
        out_shape=jax.ShapeDtypeStruct((t, n), out_dtype),
        compiler_params=_params("parallel", "parallel"),
        name="matmul",
    )(*args)


def _mm_resid_split_kernel(a_ref, w_ref, res_ref, o_ref):
    part = jnp.dot(a_ref[...], w_ref[...], preferred_element_type=F32)

    @pl.when(pl.program_id(2) == 0)
    def _():
        o_ref[...] = res_ref[...] + part

    @pl.when(pl.program_id(2) > 0)
    def _():
        o_ref[...] += part


def matmul_residual_split(a, w, layer, res, k_steps):
    t, k = a.shape
    n = w.shape[2]
    tk = k // k_steps
    assert k % k_steps == 0 and tk % LANES == 0
    tm, tn = _mm_tiles(t, tk, n, 4, True)
    return pl.pallas_call(
        _mm_resid_split_kernel,
        grid=(t // tm, n // tn, k_steps),
        in_specs=[pl.BlockSpec((tm, tk), lambda i, j, s: (i, s)),
                  pl.BlockSpec((None, tk, tn), lambda i, j, s: (layer, s, j)),
                  pl.BlockSpec((tm, tn), lambda i, j, s: (i, j))],
        out_specs=pl.BlockSpec((tm, tn), lambda i, j, s: (i, j)),
        out_shape=jax.ShapeDtypeStruct((t, n), F32),
        compiler_params=_params("parallel", "parallel", "arbitrary"),
        name="matmul_split",
    )(a, w, res)


def _s5_prep_kernel(lr_ref, li_ref, ls_ref, brt_ref, bit_ref, pwr_ref, pwi_ref, bbr_ref, bbi_ref):
    lr = jnp.minimum(lr_ref[...], -1e-4)
    li = li_ref[...]
    dt = jnp.exp(ls_ref[...])
    n = lax.broadcasted_iota(jnp.int32, (SUBLANES, S5_CH), 0).astype(F32) + 1.0
    mag = jnp.exp(n * (lr * dt))
    ang = n * (li * dt)
    pwr = mag * jnp.cos(ang)
    pwi = mag * jnp.sin(ang)
    pwr_ref[...] = pwr
    pwi_ref[...] = pwi
    ar = pwr[0:1]
    ai = pwi[0:1]
    den = lr * lr + li * li
    zr = ((ar - 1.0) * lr + ai * li) / den
    zi = (ai * lr - (ar - 1.0) * li) / den
    br = brt_ref[...]
    bi = bit_ref[...]
    bbr_ref[...] = zr * br - zi * bi
    bbi_ref[...] = zr * bi + zi * br


def s5_prep(lam_re, lam_im, log_step, b_re, b_im):
    lr = lam_re.reshape(1, S5_CH)
    li = lam_im.reshape(1, S5_CH)
    ls = jnp.broadcast_to(log_step[:, None], (S5_GROUPS, S5_STATE)).reshape(1, S5_CH)
    brt = b_re.reshape(S5_CH, S5_GROUP).T
    bit = b_im.reshape(S5_CH, S5_GROUP).T
    return pl.pallas_call(
        _s5_prep_kernel,
        out_shape=[jax.ShapeDtypeStruct((SUBLANES, S5_CH), F32)] * 2
        + [jax.ShapeDtypeStruct((S5_GROUP, S5_CH), F32)] * 2,
        name="s5_prep",
    )(lr, li, ls, brt, bit)


S5_BLK_GROUPS = LANES // S5_GROUP
S5_BLKS = S5_WIDTH // LANES
S5_BLK_CH = S5_BLK_GROUPS * S5_STATE


def _cmul_add(xr, xi, mr, mi, sr, si):
    return xr + mr * sr - mi * si, xi + mr * si + mi * sr


def _s5_kernel(u_ref, h0r_ref, h0i_ref, pwr_ref, pwi_ref, wbr_ref, wbi_ref, wcr_ref, wci_ref, d_ref,
               y_ref, hr_ref, hi_ref, xr_scr, xi_scr, *, nb, seq):
    u2 = u_ref[...].reshape(nb * seq, LANES)
    ub = u2.astype(BF16)
    xr_scr[...] = jnp.dot(ub, wbr_ref[0], preferred_element_type=F32).reshape(nb, seq, S5_BLK_CH)
    xi_scr[...] = jnp.dot(ub, wbi_ref[0], preferred_element_type=F32).reshape(nb, seq, S5_BLK_CH)

    pwr = pwr_ref[...]
    pwi = pwi_ref[...]
    row = lax.broadcasted_iota(jnp.int32, (SUBLANES, S5_BLK_CH), 0)
    steps = []
    for d in (1, 2, 4):
        keep = row >= d
        steps.append((d, jnp.where(keep, pwr[d - 1:d], 0.0)[None], jnp.where(keep, pwi[d - 1:d], 0.0)[None]))
    pr = pwr[None]
    pi = pwi[None]

    def tile(i, carry):
        cr, ci = carry
        o = pl.multiple_of(i * SUBLANES, SUBLANES)
        xr = xr_scr[:, pl.ds(o, SUBLANES), :]
        xi = xi_scr[:, pl.ds(o, SUBLANES), :]
        for d, mr, mi in steps:
            sr = pltpu.roll(xr, d, axis=1)
            si = pltpu.roll(xi, d, axis=1)
            xr, xi = _cmul_add(xr, xi, mr, mi, sr, si)
        xr, xi = _cmul_add(xr, xi, pr, pi, cr, ci)
        xr_scr[:, pl.ds(o, SUBLANES), :] = xr
        xi_scr[:, pl.ds(o, SUBLANES), :] = xi
        return xr[:, SUBLANES - 1:SUBLANES, :], xi[:, SUBLANES - 1:SUBLANES, :]

    hr, hi = lax.fori_loop(0, seq // SUBLANES, tile, (h0r_ref[...], h0i_ref[...]))
    hr_ref[...] = hr
    hi_ref[...] = hi

    xr = xr_scr[...].reshape(nb * seq, S5_BLK_CH).astype(BF16)
    xi = xi_scr[...].reshape(nb * seq, S5_BLK_CH).astype(BF16)
    y = (jnp.dot(xr, wcr_ref[0], preferred_element_type=F32)
         - jnp.dot(xi, wci_ref[0], preferred_element_type=F32)
         + d_ref[...] * u2)
    y_ref[...] = jax.nn.gelu(y).reshape(nb, seq, LANES)


def s5_scan(z3, h0r, h0i, pwr, pwi, wbr, wbi, wcr, wci, d, nb):
    bsz, seq, _ = z3.shape
    assert seq % SUBLANES == 0 and bsz % nb == 0
    seq_blk = pl.BlockSpec((nb, seq, LANES), lambda b, k: (b, 0, k))
    st_blk = pl.BlockSpec((nb, 1, S5_BLK_CH), lambda b, k: (b, 0, k))
    pw_blk = pl.BlockSpec((SUBLANES, S5_BLK_CH), lambda b, k: (0, k))
    wb_blk = pl.BlockSpec((1, LANES, S5_BLK_CH), lambda b, k: (k, 0, 0))
    wc_blk = pl.BlockSpec((1, S5_BLK_CH, LANES), lambda b, k: (k, 0, 0))
    y, hr, hi = pl.pallas_call(
        functools.partial(_s5_kernel, nb=nb, seq=seq),
        grid=(bsz // nb, S5_BLKS),
        in_specs=[seq_blk, st_blk, st_blk, pw_blk, pw_blk, wb_blk, wb_blk, wc_blk, wc_blk,
                  pl.BlockSpec((1, LANES), lambda b, k: (0, k))],
        out_specs=[seq_blk, st_blk, st_blk],
        out_shape=[jax.ShapeDtypeStruct((bsz, seq, S5_WIDTH), F32),
                   jax.ShapeDtypeStruct((bsz, 1, S5_CH), F32),
                   jax.ShapeDtypeStruct((bsz, 1, S5_CH), F32)],
        scratch_shapes=[pltpu.VMEM((nb, seq, S5_BLK_CH), F32), pltpu.VMEM((nb, seq, S5_BLK_CH), F32)],
        compiler_params=_params("parallel", "parallel"),
        name="s5_scan",
    )(z3, h0r.reshape(bsz, 1, S5_CH), h0i.reshape(bsz, 1, S5_CH), pwr, pwi, wbr, wbi, wcr, wci, d)
    return y, hr.reshape(bsz, S5_GROUPS, S5_STATE), hi.reshape(bsz, S5_GROUPS, S5_STATE)


S5_LANE_TILES = S5_BLK_CH // LANES


def _s5_long_kernel(u_ref, h0r_ref, h0i_ref, ar_ref, ai_ref, wbr_ref, wbi_ref, wcr_ref, wci_ref, d_ref,
                    y_ref, hr_ref, hi_ref, xr_scr, xi_scr, cr_scr, ci_scr, *, tl):
    step = pl.program_id(1)

    @pl.when(step == 0)
    def _():
        cr_scr[...] = h0r_ref[0]
        ci_scr[...] = h0i_ref[0]

    for k in range(S5_BLKS):
        ub = u_ref[0, :, k * LANES:(k + 1) * LANES].astype(BF16)
        bur = jnp.dot(ub, wbr_ref[k], preferred_element_type=F32)
        bui = jnp.dot(ub, wbi_ref[k], preferred_element_type=F32)
        for j in range(S5_LANE_TILES):
            xr_scr[j, k * tl:(k + 1) * tl, :] = bur[:, j * LANES:(j + 1) * LANES]
            xi_scr[j, k * tl:(k + 1) * tl, :] = bui[:, j * LANES:(j + 1) * LANES]

    ar = [ar_ref[:, j * LANES:(j + 1) * LANES] for j in range(S5_LANE_TILES)]
    ai = [ai_ref[:, j * LANES:(j + 1) * LANES] for j in range(S5_LANE_TILES)]

    def token(t, carry):
        cr, ci = carry
        nr, ni = [], []
        for j in range(S5_LANE_TILES):
            rows = pl.ds(t, S5_BLKS, stride=tl)
            xr, xi = _cmul_add(xr_scr[j, rows, :], xi_scr[j, rows, :], ar[j], ai[j], cr[j], ci[j])
            xr_scr[j, rows, :] = xr
            xi_scr[j, rows, :] = xi
            nr.append(xr)
            ni.append(xi)
        return tuple(nr), tuple(ni)

    init = (tuple(cr_scr[:, j * LANES:(j + 1) * LANES] for j in range(S5_LANE_TILES)),
            tuple(ci_scr[:, j * LANES:(j + 1) * LANES] for j in range(S5_LANE_TILES)))
    cr, ci = lax.fori_loop(0, tl, token, init, unroll=8)
    cr = jnp.concatenate(cr, axis=-1)
    ci = jnp.concatenate(ci, axis=-1)
    cr_scr[...] = cr
    ci_scr[...] = ci
    hr_ref[0] = cr
    hi_ref[0] = ci

    for k in range(S5_BLKS):
        xr = jnp.concatenate([xr_scr[j, k * tl:(k + 1) * tl, :] for j in range(S5_LANE_TILES)], axis=-1)
        xi = jnp.concatenate([xi_scr[j, k * tl:(k + 1) * tl, :] for j in range(S5_LANE_TILES)], axis=-1)
        u = u_ref[0, :, k * LANES:(k + 1) * LANES]
        y = (jnp.dot(xr.astype(BF16), wcr_ref[k], preferred_element_type=F32)
             - jnp.dot(xi.astype(BF16), wci_ref[k], preferred_element_type=F32)
             + d_ref[:, k * LANES:(k + 1) * LANES] * u)
        y_ref[0, :, k * LANES:(k + 1) * LANES] = jax.nn.gelu(y)


def s5_scan_long(z3, h0r, h0i, pwr, pwi, wbr, wbi, wcr, wci, d, tl):
    bsz, seq, _ = z3.shape
    assert seq % tl == 0 and tl % SUBLANES == 0
    seq_blk = pl.BlockSpec((1, tl, S5_WIDTH), lambda b, t: (b, t, 0))
    st_blk = pl.BlockSpec((1, S5_BLKS, S5_BLK_CH), lambda b, t: (b, 0, 0))
    lam_blk = pl.BlockSpec((S5_BLKS, S5_BLK_CH), lambda b, t: (0, 0))
    wb_blk = pl.BlockSpec((S5_BLKS, LANES, S5_BLK_CH), lambda b, t: (0, 0, 0))
    wc_blk = pl.BlockSpec((S5_BLKS, S5_BLK_CH, LANES), lambda b, t: (0, 0, 0))
    rows = pltpu.VMEM((S5_LANE_TILES, S5_BLKS * tl, LANES), F32)
    carry = pltpu.VMEM((S5_BLKS, S5_BLK_CH), F32)
    y, hr, hi = pl.pallas_call(
        functools.partial(_s5_long_kernel, tl=tl),
        grid=(bsz, seq // tl),
        in_specs=[seq_blk, st_blk, st_blk, lam_blk, lam_blk, wb_blk, wb_blk, wc_blk, wc_blk,
                  pl.BlockSpec((1, S5_WIDTH), lambda b, t: (0, 0))],
        out_specs=[seq_blk, st_blk, st_blk],
        out_shape=[jax.ShapeDtypeStruct((bsz, seq, S5_WIDTH), F32),
                   jax.ShapeDtypeStruct((bsz, S5_BLKS, S5_BLK_CH), F32),
                   jax.ShapeDtypeStruct((bsz, S5_BLKS, S5_BLK_CH), F32)],
        scratch_shapes=[rows, rows, carry, carry],
        compiler_params=_params("parallel", "arbitrary"),
        name="s5_scan_long",
    )(z3, h0r.reshape(bsz, S5_BLKS, S5_BLK_CH), h0i.reshape(bsz, S5_BLKS, S5_BLK_CH),
      pwr[0].reshape(S5_BLKS, S5_BLK_CH), pwi[0].reshape(S5_BLKS, S5_BLK_CH), wbr, wbi, wcr, wci, d)
    return y, hr.reshape(bsz, S5_GROUPS, S5_STATE), hi.reshape(bsz, S5_GROUPS, S5_STATE)


def _s5_block_weights(bbr_t, bbi_t, c_re, c_im):
    eye = jnp.eye(S5_BLK_GROUPS, dtype=F32)

    def wb(bt):
        b4 = bt.reshape(S5_GROUP, S5_BLKS, S5_BLK_GROUPS, S5_STATE)
        w = jnp.einsum("cbgp,hg->bhcgp", b4, eye)
        return w.reshape(S5_BLKS, LANES, S5_BLK_CH).astype(BF16)

    def wc(c):
        c4 = c.reshape(S5_BLKS, S5_BLK_GROUPS, S5_GROUP, S5_STATE)
        w = jnp.einsum("bgcp,hg->bhpgc", c4, eye)
        return w.reshape(S5_BLKS, S5_BLK_CH, LANES).astype(BF16)

    return wb(bbr_t), wb(bbi_t), wc(c_re), wc(c_im)


def _hg_lb_kernel(x_ref, o_ref):
    x = x_ref[...]
    e = jnp.exp(x - jnp.max(x, axis=0, keepdims=True))
    sm = e / jnp.sum(e, axis=0, keepdims=True)
    acc = sm[0:1]
    o_ref[0:1, :] = acc
    for l in range(1, x.shape[0]):
        acc = acc + sm[l:l + 1]
        o_ref[l:l + 1, :] = acc


def hg_lower_bounds(hg_lb):
    return pl.pallas_call(_hg_lb_kernel, out_shape=jax.ShapeDtypeStruct(hg_lb.shape, F32), name="hg_lb")(hg_lb)


def _cumsum_rows(x, n):
    row = lax.broadcasted_iota(jnp.int32, x.shape, 1)
    d = 1
    while d < n:
        x = x + jnp.where(row >= d, pltpu.roll(x, d, axis=1), 0.0)
        d *= 2
    return x


def _hgrn_kernel(q_ref, f_ref, i_ref, g_ref, s0_ref, lb_ref, nw_ref, o_ref, sf_ref, st_scr, *, nb, hh, chunk, n_sub):
    step = pl.program_id(2)

    def units(x):
        return jnp.concatenate([x[:, :, h * LANES:(h + 1) * LANES] for h in range(hh)], axis=0)

    lb = units(jnp.broadcast_to(lb_ref[...][None], (nb, 1, hh * LANES)))
    nw = nw_ref[...][None]

    @pl.when(step == 0)
    def _():
        for h in range(hh):
            for b in range(nb):
                st_scr[h * nb + b] = s0_ref[b, h].T

    trow = lax.broadcasted_iota(jnp.int32, (chunk, chunk), 0)
    tcol = lax.broadcasted_iota(jnp.int32, (chunk, chunk), 1)
    causal = (tcol <= trow)[None]
    st = st_scr[...]
    for c in range(n_sub):
        rows = slice(c * chunk, (c + 1) * chunk)
        q, f, v, g = (units(ref[:, rows, :]) for ref in (q_ref, f_ref, i_ref, g_ref))
        fg = lb + (1.0 - lb) * jax.nn.sigmoid(f)
        qh = jax.nn.silu(q)
        kh = 1.0 - fg
        bcum = _cumsum_rows(jnp.log(fg), chunk)
        btot = bcum[:, chunk - 1:chunk, :]
        q_in = (qh * jnp.exp(bcum)).astype(BF16)
        k_in = (kh * jnp.exp(-bcum)).astype(BF16)
        k_end = (kh * jnp.exp(btot - bcum)).astype(BF16)
        decay = jnp.exp(btot)
        vb = v.astype(BF16)
        att = jnp.einsum("utk,usk->uts", q_in, k_in, preferred_element_type=F32)
        att = jnp.where(causal, att, 0.0).astype(BF16)
        out = (jnp.einsum("utk,uvk->utv", q_in, st.astype(BF16), preferred_element_type=F32)
               + jnp.einsum("uts,usv->utv", att, vb, preferred_element_type=F32))
        st = st * decay + jnp.einsum("usv,usk->uvk", vb, k_end, preferred_element_type=F32)
        out = out * lax.rsqrt(jnp.mean(out * out, axis=-1, keepdims=True) + EPS) * nw
        out = (out * jax.nn.silu(g)).astype(o_ref.dtype)
        for h in range(hh):
            o_ref[:, rows, h * LANES:(h + 1) * LANES] = out[h * nb:(h + 1) * nb]
    st_scr[...] = st

    @pl.when(step == pl.num_programs(2) - 1)
    def _():
        for h in range(hh):
            for b in range(nb):
                sf_ref[b, h] = st[h * nb + b].T


def hgrn2(z3, s0, lb, norm_w, nb, hh):
    bsz, seq, _ = z3.shape
    chunk = min(HG_CHUNK, seq)
    n_sub = next(n for n in (3, 2, 1) if seq % (n * chunk) == 0)
    rows = n_sub * chunk
    assert bsz % nb == 0 and HG_HEADS % hh == 0
    wid = hh * LANES
    n_col = (HG_HEADS * HG_K) // wid

    def col(proj):
        return pl.BlockSpec((nb, rows, wid), lambda h, b, t, proj=proj: (b, t, proj * n_col + h))

    st_blk = pl.BlockSpec((nb, hh, HG_K, HG_V), lambda h, b, t: (b, h, 0, 0))
    out, sf = pl.pallas_call(
        functools.partial(_hgrn_kernel, nb=nb, hh=hh, chunk=chunk, n_sub=n_sub),
        grid=(HG_HEADS // hh, bsz // nb, seq // rows),
        in_specs=[col(1), col(2), col(3), col(4), st_blk,
                  pl.BlockSpec((1, wid), lambda h, b, t: (0, h)),
                  pl.BlockSpec((1, LANES), lambda h, b, t: (0, 0))],
        out_specs=[pl.BlockSpec((nb, rows, wid), lambda h, b, t: (b, t, h)), st_blk],
        out_shape=[jax.ShapeDtypeStruct((bsz, seq, HG_HEADS * HG_V), BF16),
                   jax.ShapeDtypeStruct((bsz, HG_HEADS, HG_K, HG_V), F32)],
        scratch_shapes=[pltpu.VMEM((hh * nb, HG_V, HG_K), F32)],
        compiler_params=_params("parallel", "parallel", "arbitrary"),
        name="hgrn2",
    )(z3, z3, z3, z3, s0, lb.reshape(1, HG_HEADS * HG_K), norm_w.reshape(1, HG_V))
    return out, sf


RW_MIXES = 6


def _norm_mix_kernel(h_ref, sh_ref, w_ref, mu_ref, *refs, tl):
    o_refs = refs[:RW_MIXES]
    last_ref, scr = refs[RW_MIXES:]
    x = h_ref[...]
    xn = x * lax.rsqrt(jnp.mean(x * x, axis=-1, keepdims=True) + EPS) * w_ref[...][None]

    @pl.when(pl.program_id(1) == 0)
    def _():
        scr[:, SUBLANES - 1:SUBLANES, :] = sh_ref[...]

    scr[:, SUBLANES:, :] = xn
    xx = scr[:, SUBLANES - 1:SUBLANES - 1 + tl, :] - xn
    for j, o_ref in enumerate(o_refs):
        o_ref[...] = (xn + xx * mu_ref[j:j + 1, :][None]).astype(o_ref.dtype)
    last = xn[:, tl - 1:tl, :]
    scr[:, SUBLANES - 1:SUBLANES, :] = last
    last_ref[...] = last


def norm_mix(h3, shift0, ln_w, mu, nb, tl):
    bsz, seq, d = h3.shape
    assert bsz % nb == 0 and seq % tl == 0 and tl % SUBLANES == 0
    blk = pl.BlockSpec((nb, tl, d), lambda b, t: (b, t, 0))
    row = pl.BlockSpec((nb, 1, d), lambda b, t: (b, 0, 0))
    outs = pl.pallas_call(
        functools.partial(_norm_mix_kernel, tl=tl),
        grid=(bsz // nb, seq // tl),
        in_specs=[blk, row, pl.BlockSpec((1, d), lambda b, t: (0, 0)), pl.BlockSpec((RW_MIXES, d), lambda b, t: (0, 0))],
        out_specs=[blk] * RW_MIXES + [row],
        out_shape=[jax.ShapeDtypeStruct((bsz, seq, d), BF16)] * RW_MIXES + [jax.ShapeDtypeStruct((bsz, 1, d), F32)],
        scratch_shapes=[pltpu.VMEM((nb, SUBLANES + tl, d), F32)],
        compiler_params=_params("parallel", "arbitrary"),
        name="norm_mix",
    )(h3, shift0.reshape(bsz, 1, d), ln_w.reshape(1, d), mu)
    return outs[:RW_MIXES], outs[RW_MIXES].reshape(bsz, d)


RW_PAIR = LANES // RW_HEAD
RW_DECAY_SCALE = math.exp(-0.5)
RW_SOLVE_BLOCK = 8


def _rwkv_kernel(r_ref, k_ref, v_ref, tw_ref, ta_ref, tg_ref, w2_ref, a2_ref, g2_ref, s0_ref,
                 w0_ref, a0_ref, kk_ref, ka_ref, rk_ref, lnw_ref, lnb_ref,
                 o_ref, sf_ref, s_scr, *, nb, hp, chunk):
    nu = hp * nb
    step = pl.program_id(2)
    lane = lax.broadcasted_iota(jnp.int32, (1, 1, LANES), 2)
    head1 = lane >= RW_HEAD

    def units(x):
        return jnp.concatenate([x[:, :, p * LANES:(p + 1) * LANES] for p in range(hp)], axis=0)

    def unit_rows(ref):
        return units(jnp.broadcast_to(ref[...][None], (nb, 1, hp * LANES)))

    w0, a0, k_k, k_a, r_k, ln_w, ln_b = (unit_rows(p) for p in
                                         (w0_ref, a0_ref, kk_ref, ka_ref, rk_ref, lnw_ref, lnb_ref))
    sq_row = lax.broadcasted_iota(jnp.int32, (LANES, LANES), 0) >= RW_HEAD
    sq_col = lax.broadcasted_iota(jnp.int32, (LANES, LANES), 1) >= RW_HEAD
    same_head = sq_row == sq_col
    ones_bd = same_head.astype(BF16)

    @pl.when(step == 0)
    def _():
        zero = jnp.zeros((nb, RW_HEAD, RW_HEAD), F32)
        for p in range(hp):
            top = jnp.concatenate([s0_ref[:, RW_PAIR * p], zero], axis=-1)
            bot = jnp.concatenate([zero, s0_ref[:, RW_PAIR * p + 1]], axis=-1)
            s_scr[p * nb:(p + 1) * nb] = jnp.concatenate([top, bot], axis=1)

    def bdot(spec, a, b):
        return jnp.einsum(spec, a.astype(BF16), b.astype(BF16), preferred_element_type=F32)

    def head_sum(x, two_pass=True):
        x2 = x.reshape(nu * chunk, LANES)
        hi = x2.astype(BF16)
        s = jnp.dot(hi, ones_bd, preferred_element_type=F32)
        if two_pass:
            lo = (x2 - hi.astype(F32)).astype(BF16)
            s = s + jnp.dot(lo, ones_bd, preferred_element_type=F32)
        return s.reshape(nu, chunk, LANES)

    def low_rank(t_ref, w_ref):
        t2 = t_ref[...].reshape(nb * chunk, t_ref.shape[-1])
        return units(jnp.dot(t2, w_ref[...], preferred_element_type=F32).reshape(nb, chunk, hp * LANES))

    srow = lax.broadcasted_iota(jnp.int32, (chunk, chunk), 0)
    scol = lax.broadcasted_iota(jnp.int32, (chunk, chunk), 1)
    tri = jnp.broadcast_to((scol <= srow).astype(BF16)[None], (nu, chunk, chunk))

    def tri_sum(x):
        hi = x.astype(BF16)
        lo = (x - hi.astype(F32)).astype(BF16)
        return (jnp.einsum("uts,usc->utc", tri, hi, preferred_element_type=F32)
                + jnp.einsum("uts,usc->utc", tri, lo, preferred_element_type=F32))

    def stack_heads(x):
        return jnp.concatenate([jnp.where(head1, 0.0, x), jnp.where(head1, x, 0.0)], axis=1).astype(BF16)

    r, k, v = (units(ref[...]) for ref in (r_ref, k_ref, v_ref))
    wl, al, g = low_rank(tw_ref, w2_ref), low_rank(ta_ref, a2_ref), low_rank(tg_ref, g2_ref)
    lw = (-RW_DECAY_SCALE) * jax.nn.sigmoid(w0 + wl)
    ag = jax.nn.sigmoid(a0 + al)
    kk = k * k_k
    kk = kk * jnp.minimum(lax.rsqrt(head_sum(kk * kk)), 1e12)
    k2 = k * (1.0 + (ag - 1.0) * k_a)
    cl = tri_sum(lw)
    e_pos = jnp.exp(cl)
    e_neg = jnp.exp(-cl)
    at = (-kk) * jnp.exp(cl - lw)
    bt = (kk * ag) * e_neg
    kt = k2 * e_neg
    rt = r * e_pos
    wc = e_pos[:, chunk - 1:chunk, :]
    trow = lax.broadcasted_iota(jnp.int32, (chunk, RW_PAIR * chunk), 0)
    tcol = lax.broadcasted_iota(jnp.int32, (chunk, RW_PAIR * chunk), 1)
    tcol = jnp.where(tcol >= chunk, tcol - chunk, tcol)
    strict = (tcol < trow)[None]
    incl = (tcol <= trow)[None]
    x2 = jnp.concatenate([at, rt], axis=1)
    pb = bdot("utc,usc->uts", x2, stack_heads(bt))
    pk = bdot("utc,usc->uts", x2, stack_heads(kt))
    lab = jnp.where(strict, pb[:, :chunk], 0.0)
    lak = jnp.where(strict, pk[:, :chunk], 0.0)
    arb = jnp.where(incl, pb[:, chunk:], 0.0)
    ark = jnp.where(incl, pk[:, chunk:], 0.0)
    v_bd = stack_heads(v)
    xa = at
    xv = bdot("uts,usc->utc", lak, v_bd)
    sub = min(RW_SOLVE_BLOCK, chunk)
    done_a, done_v = [], []
    for lo in range(0, chunk, sub):
        xa_i = xa[:, lo:lo + sub, :]
        xv_i = xv[:, lo:lo + sub, :]
        if lo:
            pad = jnp.zeros((nu, chunk - lo, LANES), F32)
            prev = jnp.concatenate([stack_heads(jnp.concatenate(done_a + [pad], axis=1)),
                                    stack_heads(jnp.concatenate(done_v + [pad], axis=1))], axis=-1)
            upd = bdot("uts,usc->utc", lab[:, lo:lo + sub, :], prev)
            xa_i = xa_i + upd[:, :, :LANES]
            xv_i = xv_i + upd[:, :, LANES:]
        l0 = lab[:, lo:lo + sub, lo:lo + sub]
        l1 = lab[:, lo:lo + sub, chunk + lo:chunk + lo + sub]
        for s in range(sub - 1):
            m = jnp.where(head1, l1[:, :, s:s + 1], l0[:, :, s:s + 1])
            xa_i = xa_i + m * xa_i[:, s:s + 1, :]
            xv_i = xv_i + m * xv_i[:, s:s + 1, :]
        done_a.append(xa_i)
        done_v.append(xv_i)
    ah = jnp.concatenate(done_a, axis=1)
    vh = jnp.concatenate(done_v, axis=1)
    both = bdot("uts,usc->utc", arb, jnp.concatenate([stack_heads(ah), stack_heads(vh)], axis=-1))
    rh = rt + both[:, :, :LANES]
    yh = both[:, :, LANES:] + bdot("uts,usc->utc", ark, v_bd)
    gp = jnp.where(same_head, bdot("utj,utk->ujk", ah, bt), 0.0)
    ht = jnp.where(same_head, bdot("utv,utk->uvk", jnp.concatenate([vh, v], axis=1),
                                   jnp.concatenate([bt, kt], axis=1)), 0.0)
    st = s_scr[...]
    y = bdot("utk,uvk->utv", rh, st) + yh
    st = (st + bdot("uvj,ujk->uvk", st, gp) + ht) * wc
    s_scr[...] = st
    inv_n = 1.0 / RW_HEAD
    yc = y - head_sum(y, two_pass=False) * inv_n
    var = head_sum(yc * yc, two_pass=False) * inv_n
    y = yc * lax.rsqrt(var + RW_GN_EPS) * ln_w + ln_b
    y = y + head_sum(r * k2 * r_k, two_pass=False) * v
    out = (y * g).astype(o_ref.dtype)
    for p in range(hp):
        o_ref[:, :, p * LANES:(p + 1) * LANES] = out[p * nb:(p + 1) * nb]

    @pl.when(step == pl.num_programs(2) - 1)
    def _():
        for p in range(hp):
            sf_ref[:, RW_PAIR * p] = st[p * nb:(p + 1) * nb, :RW_HEAD, :RW_HEAD]
            sf_ref[:, RW_PAIR * p + 1] = st[p * nb:(p + 1) * nb, RW_HEAD:, RW_HEAD:]


def _rwkv_chunk(seq):
    for c in (48, 32, 16, 8):
        if seq % c == 0:
            return c
    raise ValueError(seq)


def rwkv7(r, k, v, low, low_w, s0, w0, a0, k_k, k_a, r_k, ln_w, ln_b, nb, hp):
    bsz, seq, d = r.shape
    chunk = _rwkv_chunk(seq)
    heads = hp * RW_PAIR
    assert bsz % nb == 0 and RW_HEADS % heads == 0
    seq_blk = pl.BlockSpec((nb, chunk, hp * LANES), lambda h, b, t: (b, t, h))
    st_blk = pl.BlockSpec((nb, heads, RW_HEAD, RW_HEAD), lambda h, b, t: (b, h, 0, 0))
    vec = pl.BlockSpec((1, hp * LANES), lambda h, b, t: (0, h))
    low_blk = [pl.BlockSpec((nb, chunk, x.shape[-1]), lambda h, b, t: (b, t, 0)) for x in low]
    low_w_blk = [pl.BlockSpec((w.shape[0], hp * LANES), lambda h, b, t: (0, h)) for w in low_w]
    out, sf = pl.pallas_call(
        functools.partial(_rwkv_kernel, nb=nb, hp=hp, chunk=chunk),
        grid=(RW_HEADS // heads, bsz // nb, seq // chunk),
        in_specs=[seq_blk] * 3 + low_blk + low_w_blk + [st_blk] + [vec] * 7,
        out_specs=[seq_blk, st_blk],
        out_shape=[jax.ShapeDtypeStruct((bsz, seq, d), BF16),
                   jax.ShapeDtypeStruct((bsz, RW_HEADS, RW_HEAD, RW_HEAD), F32)],
        scratch_shapes=[pltpu.VMEM((hp * nb, LANES, LANES), F32)],
        compiler_params=_params("parallel", "parallel", "arbitrary"),
        name="rwkv7",
    )(r, k, v, *low, *low_w, s0, *(p.reshape(1, d) for p in (w0, a0, k_k, k_a, r_k, ln_w, ln_b)))
    return out, sf


def _ffn_in_kernel(x_ref, wa_ref, wv_ref, e_ref, cw_ref, cb_ref, o_ref, st_ref, scr, *, nb, seq, sb, sr):
    tn = wa_ref.shape[1]
    cw = cw_ref[...]
    cb = cb_ref[...][None]
    scr[:, SUBLANES - (CONV_W - 1):SUBLANES, :] = e_ref[...]
    for b0 in range(0, nb, sb):
        for r0 in range(0, seq, sr):
            lo = b0 * seq + r0
            x = x_ref[lo:lo + sb * sr, :]
            a = jnp.dot(x, wa_ref[...], preferred_element_type=F32).reshape(sb, sr, tn)
            v = jnp.dot(x, wv_ref[...], preferred_element_type=F32).reshape(sb, sr, tn)
            scr[b0:b0 + sb, SUBLANES + r0:SUBLANES + r0 + sr, :] = a
            c = cb + cw[CONV_W - 1:CONV_W][None] * a
            for j in range(CONV_W - 1):
                first = SUBLANES + r0 - (CONV_W - 1 - j)
                c = c + cw[j:j + 1][None] * scr[b0:b0 + sb, first:first + sr, :]
            o_ref[lo:lo + sb * sr, :] = (jax.nn.gelu(c) * v).reshape(sb * sr, tn).astype(o_ref.dtype)
    st_ref[...] = scr[:, SUBLANES + seq - (CONV_W - 1):SUBLANES + seq, :]


def ffn_in(xb, conv0, w_in, layer, conv_w, conv_b, bsz, seq, nb, tn, sub):
    t, d = xb.shape
    sb, sr = sub
    assert t == bsz * seq and bsz % nb == 0 and D_FF % tn == 0
    assert nb % sb == 0 and seq % sr == 0 and sr % SUBLANES == 0 and (sr == seq or nb == sb == 1)
    col = lambda i, j: (0, j)
    n_col = D_FF // tn
    out, st = pl.pallas_call(
        functools.partial(_ffn_in_kernel, nb=nb, seq=seq, sb=sb, sr=sr),
        grid=(bsz // nb, n_col),
        in_specs=[pl.BlockSpec((nb * seq, d), lambda i, j: (i, 0)),
                  pl.BlockSpec((None, d, tn), lambda i, j: (layer, 0, j)),
                  pl.BlockSpec((None, d, tn), lambda i, j: (layer, 0, j + n_col)),
                  pl.BlockSpec((nb, CONV_W - 1, tn), lambda i, j: (i, 0, j)),
                  pl.BlockSpec((CONV_W, tn), col), pl.BlockSpec((1, tn), col)],
        out_specs=[pl.BlockSpec((nb * seq, tn), lambda i, j: (i, j)),
                   pl.BlockSpec((nb, CONV_W - 1, tn), lambda i, j: (i, 0, j))],
        out_shape=[jax.ShapeDtypeStruct((t, D_FF), BF16),
                   jax.ShapeDtypeStruct((bsz, CONV_W - 1, D_FF), F32)],
        scratch_shapes=[pltpu.VMEM((nb, SUBLANES + seq, tn), F32)],
        compiler_params=_params("parallel", "parallel"),
        name="ffn_in",
    )(xb, w_in, w_in, conv0, conv_w, conv_b.reshape(1, D_FF))
    return out, st


FFN_DOWN_K_STEPS = 2


def _channel_mixer(h, conv0, layer, p, cfg, bsz, seq):
    (xb,) = rmsnorm(h, p["ln_ffn"][layer], (BF16,))
    gated, n_cv = ffn_in(xb, conv0, p["ffn_w_in"], layer, p["ffn_conv_w"][layer], p["ffn_conv_b"][layer],
                         bsz, seq, cfg["ffn_nb"], cfg["ffn_tn"], cfg["ffn_sub"])
    return matmul_residual_split(gated, p["ffn_w_down"], layer, h, FFN_DOWN_K_STEPS), n_cv


def _trunk(x3, s5r, s5i, hg, rw, sh, cv, p, cfg):
    bsz, seq, d = x3.shape
    seq += cfg["front"]
    t = bsz * seq

    if cfg["front"]:
        h, xb = embed_norm(x3, p["meta"], p["ln_mix"][0])
    else:
        h = x3.reshape(t, d)
        (xb,) = rmsnorm(h, p["ln_mix"][0], (BF16,))
    z = matmul([xb], [p["ev_w_in"]], F32).reshape(bsz, seq, EVEN_IN)
    if cfg["s5_tl"]:
        ys5, n_s5r, n_s5i = s5_scan_long(z, s5r[0], s5i[0], *p["s5"], tl=cfg["s5_tl"])
    else:
        ys5, n_s5r, n_s5i = s5_scan(z, s5r[0], s5i[0], *p["s5"], nb=cfg["s5_nb"])
    ys5 = ys5.reshape(t, S5_WIDTH)
    ya = matmul([ys5], [p["s5_w_glu"]], BF16, epilogue="glu", extra=ys5)
    yb, n_hg = hgrn2(z, hg[0], p["hg_lb"], p["hg_norm_w"], nb=cfg["hg_nb"], hh=cfg["hg_hh"])
    h = matmul([ya, yb.reshape(t, -1)], [(p["ev_w_out"], 0, 0), (p["ev_w_out"], 0, 1)], F32,
               epilogue="residual", extra=h)
    h, n_cv0 = _channel_mixer(h, cv[0], 0, p, cfg, bsz, seq)

    mixes, n_sh = norm_mix(h.reshape(bsz, seq, d), sh[0], p["ln_mix"][1], p["rw_mu"], *cfg["mix_blk"])
    xr, xw, xk, xv, xa, xg = (m.reshape(t, d) for m in mixes)
    r = matmul([xr], [p["rw_w_r"]], F32)
    k = matmul([xk], [p["rw_w_k"]], F32)
    v = matmul([xv], [p["rw_w_v"]], F32)
    low = (matmul([xw], [p["rw_w1"]], BF16, act="tanh"), matmul([xa], [p["rw_a1"]], BF16),
           matmul([xg], [p["rw_g1"]], BF16, act="sigmoid"))
    as3 = lambda a: a.reshape(bsz, seq, a.shape[-1])
    yo, n_rw = rwkv7(as3(r), as3(k), as3(v), [as3(x) for x in low], (p["rw_w2"], p["rw_a2"], p["rw_g2"]),
                     rw[0], *p["rw_vec"], nb=cfg["rw_nb"], hp=cfg["rw_hp"])
    h = matmul([yo.reshape(t, d)], [p["rw_w_o"]], F32, epilogue="residual", extra=h)
    h, n_cv1 = _channel_mixer(h, cv[1], 1, p, cfg, bsz, seq)

    (y,) = rmsnorm(h, p["ln_final"], (F32,), rows=(bsz, seq, cfg["front"]) if cfg["front"] else None)
    return (y.reshape(bsz, seq - cfg["front"], d), n_s5r[None], n_s5i[None], n_hg[None], n_rw[None], n_sh[None],
            jnp.stack([n_cv0, n_cv1]))


PROMPT_CFG = dict(front=N_META, s5_tl=344, s5_nb=None, hg_nb=4, hg_hh=8, mix_blk=(1, 344), rw_nb=4, rw_hp=16, ffn_nb=1, ffn_tn=512, ffn_sub=(1, 688))
SAMPLE_CFG = dict(front=0, s5_tl=None, s5_nb=32, hg_nb=4, hg_hh=8, mix_blk=(32, 8), rw_nb=16, rw_hp=4, ffn_nb=128, ffn_tn=512, ffn_sub=(32, 8))


def kernel(x_prompt, x_sample, state_s5_re, state_s5_im, state_hgrn, state_rwkv, state_shift, state_conv, meta_tokens, ln_mix, ln_ffn, ln_final, ev_w_in, ev_w_out, s5_lam_re, s5_lam_im, s5_log_step, s5_b_re, s5_b_im, s5_c_re, s5_c_im, s5_d, s5_w_glu, hg_lb, hg_norm_w, rw_mu, rw_w0, rw_w1, rw_w2, rw_a0, rw_a1, rw_a2, rw_g1, rw_g2, rw_k_k, rw_k_a, rw_r_k, rw_w_r, rw_w_k, rw_w_v, rw_w_o, rw_ln_w, rw_ln_b, ffn_w_in, ffn_conv_w, ffn_conv_b, ffn_w_down):
    bf = lambda w: w.astype(BF16)
    lb_all = hg_lower_bounds(hg_lb)
    pwr, pwi, bbr_t, bbi_t = s5_prep(s5_lam_re[0], s5_lam_im[0], s5_log_step[0], s5_b_re[0], s5_b_im[0])
    wbr, wbi, wcr, wci = _s5_block_weights(bbr_t, bbi_t, s5_c_re[0], s5_c_im[0])
    p = {
        "meta": meta_tokens, "ln_mix": ln_mix, "ln_ffn": ln_ffn, "ln_final": ln_final,
        "ev_w_in": bf(ev_w_in[0]),
        "ev_w_out": bf(ev_w_out),
        "s5": (pwr, pwi, wbr, wbi, wcr, wci, s5_d[0].reshape(1, S5_WIDTH)),
        "s5_w_glu": bf(s5_w_glu[0]),
        "hg_lb": lb_all[0], "hg_norm_w": hg_norm_w[0],
        "rw_mu": rw_mu[0],
        "rw_w1": bf(rw_w1[0]), "rw_w2": bf(rw_w2[0]), "rw_a1": bf(rw_a1[0]), "rw_a2": bf(rw_a2[0]),
        "rw_g1": bf(rw_g1[0]), "rw_g2": bf(rw_g2[0]),
        "rw_w_r": bf(rw_w_r[0]), "rw_w_k": bf(rw_w_k[0]), "rw_w_v": bf(rw_w_v[0]), "rw_w_o": bf(rw_w_o[0]),
        "rw_vec": (rw_w0[0], rw_a0[0], rw_k_k[0], rw_k_a[0], rw_r_k[0].reshape(D_MODEL), rw_ln_w[0], rw_ln_b[0]),
        "ffn_w_in": bf(ffn_w_in), "ffn_conv_w": ffn_conv_w, "ffn_conv_b": ffn_conv_b,
        "ffn_w_down": bf(ffn_w_down),
    }

    bsz = x_prompt.shape[0]
    zeros = lambda *s: jnp.zeros(s, F32)
    outs_p = _trunk(x_prompt,
                    zeros(1, bsz, S5_GROUPS, S5_STATE), zeros(1, bsz, S5_GROUPS, S5_STATE),
                    zeros(1, bsz, HG_HEADS, HG_K, HG_V), zeros(1, bsz, RW_HEADS, RW_HEAD, RW_HEAD),
                    zeros(1, bsz, D_MODEL), zeros(2, bsz, CONV_W - 1, D_FF), p, PROMPT_CFG)
    outs_s = _trunk(x_sample, state_s5_re, state_s5_im, state_hgrn, state_rwkv, state_shift, state_conv,
                    p, SAMPLE_CFG)
    return tuple(outs_p[:1]) + tuple(outs_s[:1]) + tuple(outs_p[1:]) + tuple(outs_s[1:])
```

```python
import functools
import math

import jax
import jax.numpy as jnp
from jax import lax
from jax.experimental import pallas as pl
from jax.experimental.pallas import tpu as pltpu

F32 = jnp.float32
BF16 = jnp.bfloat16

D_MODEL = 2048
N_META = 16
EPS = 1e-6
S5_WIDTH = 1024
S5_GROUP = 16
S5_GROUPS = 64
S5_STATE = 64
S5_CH = S5_GROUPS * S5_STATE
HG_HEADS = 8
HG_K = 128
HG_V = 128
HG_CHUNK = 16
EVEN_IN = 5120
RW_HEAD = 64
RW_HEADS = 32
RW_GN_EPS = 64e-5
D_FF = 5632
CONV_W = 3

LANES = 128
SUBLANES = 8
VMEM_LIMIT = 56 * 1024 * 1024


def _params(*sem):
    return pltpu.CompilerParams(dimension_semantics=sem, vmem_limit_bytes=VMEM_LIMIT)


def _row_tile(t, cap=1024):
    best = None
    for d in range(16, min(t, cap) + 1, 16):
        if t % d == 0:
            best = d
    assert best is not None, t
    return best


def _rms_kernel(x_ref, w_ref, *o_refs):
    x = x_ref[...]
    y = x * lax.rsqrt(jnp.mean(x * x, axis=-1, keepdims=True) + EPS) * w_ref[...]
    for o_ref in o_refs:
        o_ref[...] = y.astype(o_ref.dtype)


def rmsnorm(x, w, dtypes, rows=None):
    t, d = x.shape
    if rows is None:
        tm = _row_tile(t)
        n_out = t
        grid = (t // tm,)
        in_spec = pl.BlockSpec((tm, d), lambda i: (i, 0))
        out_spec = in_spec
        vec = pl.BlockSpec((1, d), lambda i: (0, 0))
    else:
        bsz, seq, front = rows
        keep = seq - front
        assert t == bsz * seq and front % 16 == 0
        tm = _row_tile(keep)
        per = keep // tm
        n_out = bsz * keep
        grid = (bsz, per)
        in_spec = pl.BlockSpec((pl.Element(tm), pl.Element(d)),
                               lambda b, i: (pl.multiple_of(b * seq + front + i * tm, 16), 0))
        out_spec = pl.BlockSpec((tm, d), lambda b, i: (b * per + i, 0))
        vec = pl.BlockSpec((1, d), lambda b, i: (0, 0))
    outs = pl.pallas_call(
        _rms_kernel,
        grid=grid,
        in_specs=[in_spec, vec],
        out_specs=[out_spec for _ in dtypes],
        out_shape=[jax.ShapeDtypeStruct((n_out, d), dt) for dt in dtypes],
        compiler_params=_params(*(["parallel"] * len(grid))),
        name="rmsnorm",
    )(x, w.reshape(1, d))
    return outs


def _embed_norm_kernel(x_ref, m_ref, w_ref, h_ref, xb_ref, *, front):
    x = x_ref[...]
    first = pl.program_id(1) == 0
    body = jnp.where(first, pltpu.roll(x, front, axis=0), x)
    head = jnp.where(first, m_ref[...], x[:front])
    rows = jnp.concatenate([head, body[front:]], axis=0)
    h_ref[...] = rows
    y = rows * lax.rsqrt(jnp.mean(rows * rows, axis=-1, keepdims=True) + EPS) * w_ref[...]
    xb_ref[...] = y.astype(xb_ref.dtype)


def embed_norm(x3, meta, w):
    bsz, seq, d = x3.shape
    front = meta.shape[0]
    total = seq + front
    tm = _row_tile(total)
    per = total // tm
    assert front % 16 == 0 and tm > front
    blk = pl.BlockSpec((tm, d), lambda b, j: (b * per + j, 0))
    return pl.pallas_call(
        functools.partial(_embed_norm_kernel, front=front),
        grid=(bsz, per),
        in_specs=[pl.BlockSpec((pl.Element(tm), pl.Element(d)),
                               lambda b, j: (pl.multiple_of(b * seq + jnp.maximum(j * tm - front, 0), 16), 0)),
                  pl.BlockSpec((front, d), lambda b, j: (0, 0)),
                  pl.BlockSpec((1, d), lambda b, j: (0, 0))],
        out_specs=[blk, blk],
        out_shape=[jax.ShapeDtypeStruct((bsz * total, d), F32), jax.ShapeDtypeStruct((bsz * total, d), BF16)],
        compiler_params=_params("parallel", "parallel"),
        name="embed_norm",
    )(x3.reshape(bsz * seq, d), meta, w.reshape(1, d))


def _act(x, act):
    if act == "tanh":
        return jnp.tanh(x)
    if act == "sigmoid":
        return jax.nn.sigmoid(x)
    assert act is None
    return x


def _mm_kernel(*refs, n_a, act, epilogue, cast_w):
    a_refs = refs[:n_a]
    w_refs = refs[n_a:2 * n_a]
    rest = refs[2 * n_a:]
    if cast_w:
        rest, wb_refs = rest[:-n_a], rest[-n_a:]

        @pl.when(pl.program_id(1) == 0)
        def _():
            for w_ref, wb_ref in zip(w_refs, wb_refs):
                wb_ref[...] = w_ref[...].astype(BF16)

        w_refs = wb_refs
    o_ref = rest[-1]
    acc = jnp.dot(a_refs[0][...].astype(BF16), w_refs[0][...], preferred_element_type=F32)
    for a_ref, w_ref in zip(a_refs[1:], w_refs[1:]):
        acc = acc + jnp.dot(a_ref[...].astype(BF16), w_ref[...], preferred_element_type=F32)
    acc = _act(acc, act)
    if epilogue == "residual":
        acc = rest[0][...] + acc
    elif epilogue == "glu":
        acc = rest[0][...] * jax.nn.sigmoid(acc)
    o_ref[...] = acc.astype(o_ref.dtype)


MM_VMEM_BUDGET = 40 * 1024 * 1024
MXU_WIDTH = 256


def _mm_tiles(t, a_row_bytes, w_col_bytes, n, out_bytes, has_extra):
    rows = [d for d in range(16, t + 1, 16) if t % d == 0]
    cols = [d for d in range(LANES, n + 1, LANES) if n % d == 0] or [n]
    best, best_score = None, -1.0
    for tm in rows:
        for tn in cols:
            est = 2 * tm * a_row_bytes + w_col_bytes * tn + tm * tn * (2 * out_bytes + 4 + (8 if has_extra else 0))
            if est > MM_VMEM_BUDGET:
                continue
            score = tm * tn * (1.0 if tn % MXU_WIDTH == 0 else 0.8)
            if score > best_score:
                best, best_score = (tm, tn), score
    assert best is not None, (t, a_row_bytes, w_col_bytes, n)
    return best


def matmul(a_list, w_list, out_dtype, act=None, epilogue=None, extra=None):
    t = a_list[0].shape[0]
    w0 = w_list[0][0] if isinstance(w_list[0], tuple) else w_list[0]
    n = w0.shape[-1]
    cast_w = w0.dtype == F32
    k_rows = sum(a.shape[1] for a in a_list)
    a_row_bytes = sum(a.shape[1] * a.dtype.itemsize for a in a_list)
    w_col_bytes = k_rows * (2 * 4 + 2 if cast_w else 2 * 2)
    tm, tn = _mm_tiles(t, a_row_bytes, w_col_bytes, n, jnp.dtype(out_dtype).itemsize, epilogue is not None)
    rc = (lambda f: lambda j, i: f(i, j)) if cast_w else (lambda f: f)
    in_specs = [pl.BlockSpec((tm, a.shape[1]), rc(lambda i, j: (i, 0))) for a in a_list]
    args = list(a_list)
    for a, w in zip(a_list, w_list):
        if isinstance(w, tuple):
            w, layer, kblk = w
            in_specs.append(pl.BlockSpec((None, a.shape[1], tn),
                                         rc(lambda i, j, layer=layer, kblk=kblk: (layer, kblk, j))))
        else:
            in_specs.append(pl.BlockSpec((w.shape[0], tn), rc(lambda i, j: (0, j))))
        args.append(w)
    if epilogue is not None:
        in_specs.append(pl.BlockSpec((tm, tn), rc(lambda i, j: (i, j))))
        args.append(extra)
    return pl.pallas_call(
        functools.partial(_mm_kernel, n_a=len(a_list), act=act, epilogue=epilogue, cast_w=cast_w),
        grid=(n // tn, t // tm) if cast_w else (t // tm, n // tn),
        in_specs=in_specs,
        out_specs=pl.BlockSpec((tm, tn), rc(lambda i, j: (i, j))),
        out_shape=jax.ShapeDtypeStruct((t, n), out_dtype),
        scratch_shapes=[pltpu.VMEM((a.shape[1], tn), BF16) for a in a_list] if cast_w else [],
        compiler_params=_params("parallel", "arbitrary" if cast_w else "parallel"),
        name="matmul",
    )(*args)


def _mm_resid_split_kernel(a_ref, w_ref, res_ref, o_ref):
    part = jnp.dot(a_ref[...], w_ref[...], preferred_element_type=F32)

    @pl.when(pl.program_id(2) == 0)
    def _():
        o_ref[...] = res_ref[...] + part

    @pl.when(pl.program_id(2) > 0)
    def _():
        o_ref[...] += part


def matmul_residual_split(a, w, layer, res, k_steps):
    t, k = a.shape
    n = w.shape[2]
    tk = k // k_steps
    assert k % k_steps == 0 and tk % LANES == 0
    tm, tn = _mm_tiles(t, 2 * tk, 4 * tk, n, 4, True)
    return pl.pallas_call(
        _mm_resid_split_kernel,
        grid=(t // tm, n // tn, k_steps),
        in_specs=[pl.BlockSpec((tm, tk), lambda i, j, s: (i, s)),
                  pl.BlockSpec((None, tk, tn), lambda i, j, s: (layer, s, j)),
                  pl.BlockSpec((tm, tn), lambda i, j, s: (i, j))],
        out_specs=pl.BlockSpec((tm, tn), lambda i, j, s: (i, j)),
        out_shape=jax.ShapeDtypeStruct((t, n), F32),
        compiler_params=_params("parallel", "parallel", "arbitrary"),
        name="matmul_split",
    )(a, w, res)


def _s5_prep_kernel(lr_ref, li_ref, ls_ref, brt_ref, bit_ref, pwr_ref, pwi_ref, bbr_ref, bbi_ref):
    lr = jnp.minimum(lr_ref[...], -1e-4)
    li = li_ref[...]
    dt = jnp.exp(ls_ref[...])
    n = lax.broadcasted_iota(jnp.int32, (SUBLANES, S5_CH), 0).astype(F32) + 1.0
    mag = jnp.exp(n * (lr * dt))
    ang = n * (li * dt)
    pwr = mag * jnp.cos(ang)
    pwi = mag * jnp.sin(ang)
    pwr_ref[...] = pwr
    pwi_ref[...] = pwi
    ar = pwr[0:1]
    ai = pwi[0:1]
    den = lr * lr + li * li
    zr = ((ar - 1.0) * lr + ai * li) / den
    zi = (ai * lr - (ar - 1.0) * li) / den
    br = brt_ref[...]
    bi = bit_ref[...]
    bbr_ref[...] = zr * br - zi * bi
    bbi_ref[...] = zr * bi + zi * br


def s5_prep(lam_re, lam_im, log_step, b_re, b_im):
    lr = lam_re.reshape(1, S5_CH)
    li = lam_im.reshape(1, S5_CH)
    ls = jnp.broadcast_to(log_step[:, None], (S5_GROUPS, S5_STATE)).reshape(1, S5_CH)
    brt = b_re.reshape(S5_CH, S5_GROUP).T
    bit = b_im.reshape(S5_CH, S5_GROUP).T
    return pl.pallas_call(
        _s5_prep_kernel,
        out_shape=[jax.ShapeDtypeStruct((SUBLANES, S5_CH), F32)] * 2
        + [jax.ShapeDtypeStruct((S5_GROUP, S5_CH), F32)] * 2,
        name="s5_prep",
    )(lr, li, ls, brt, bit)


S5_BLK_GROUPS = LANES // S5_GROUP
S5_BLKS = S5_WIDTH // LANES
S5_BLK_CH = S5_BLK_GROUPS * S5_STATE


def _cmul_add(xr, xi, mr, mi, sr, si):
    return xr + mr * sr - mi * si, xi + mr * si + mi * sr


def _s5_kernel(u_ref, h0r_ref, h0i_ref, pwr_ref, pwi_ref, wbr_ref, wbi_ref, wcr_ref, wci_ref, d_ref,
               y_ref, hr_ref, hi_ref, xr_scr, xi_scr, *, nb, seq):
    u2 = u_ref[...].reshape(nb * seq, LANES)
    ub = u2.astype(BF16)
    xr_scr[...] = jnp.dot(ub, wbr_ref[0], preferred_element_type=F32).reshape(nb, seq, S5_BLK_CH)
    xi_scr[...] = jnp.dot(ub, wbi_ref[0], preferred_element_type=F32).reshape(nb, seq, S5_BLK_CH)

    pwr = pwr_ref[...]
    pwi = pwi_ref[...]
    row = lax.broadcasted_iota(jnp.int32, (SUBLANES, S5_BLK_CH), 0)
    steps = []
    for d in (1, 2, 4):
        keep = row >= d
        steps.append((d, jnp.where(keep, pwr[d - 1:d], 0.0)[None], jnp.where(keep, pwi[d - 1:d], 0.0)[None]))
    pr = pwr[None]
    pi = pwi[None]

    def tile(i, carry):
        cr, ci = carry
        o = pl.multiple_of(i * SUBLANES, SUBLANES)
        xr = xr_scr[:, pl.ds(o, SUBLANES), :]
        xi = xi_scr[:, pl.ds(o, SUBLANES), :]
        for d, mr, mi in steps:
            sr = pltpu.roll(xr, d, axis=1)
            si = pltpu.roll(xi, d, axis=1)
            xr, xi = _cmul_add(xr, xi, mr, mi, sr, si)
        xr, xi = _cmul_add(xr, xi, pr, pi, cr, ci)
        xr_scr[:, pl.ds(o, SUBLANES), :] = xr
        xi_scr[:, pl.ds(o, SUBLANES), :] = xi
        return xr[:, SUBLANES - 1:SUBLANES, :], xi[:, SUBLANES - 1:SUBLANES, :]

    hr, hi = lax.fori_loop(0, seq // SUBLANES, tile, (h0r_ref[...], h0i_ref[...]))
    hr_ref[...] = hr
    hi_ref[...] = hi

    xr = xr_scr[...].reshape(nb * seq, S5_BLK_CH).astype(BF16)
    xi = xi_scr[...].reshape(nb * seq, S5_BLK_CH).astype(BF16)
    y = (jnp.dot(xr, wcr_ref[0], preferred_element_type=F32)
         - jnp.dot(xi, wci_ref[0], preferred_element_type=F32)
         + d_ref[...] * u2)
    y_ref[...] = jax.nn.gelu(y).reshape(nb, seq, LANES)


def s5_scan(z3, h0r, h0i, pwr, pwi, wbr, wbi, wcr, wci, d, nb):
    bsz, seq, _ = z3.shape
    assert seq % SUBLANES == 0 and bsz % nb == 0
    seq_blk = pl.BlockSpec((nb, seq, LANES), lambda b, k: (b, 0, k))
    st_blk = pl.BlockSpec((nb, 1, S5_BLK_CH), lambda b, k: (b, 0, k))
    pw_blk = pl.BlockSpec((SUBLANES, S5_BLK_CH), lambda b, k: (0, k))
    wb_blk = pl.BlockSpec((1, LANES, S5_BLK_CH), lambda b, k: (k, 0, 0))
    wc_blk = pl.BlockSpec((1, S5_BLK_CH, LANES), lambda b, k: (k, 0, 0))
    y, hr, hi = pl.pallas_call(
        functools.partial(_s5_kernel, nb=nb, seq=seq),
        grid=(bsz // nb, S5_BLKS),
        in_specs=[seq_blk, st_blk, st_blk, pw_blk, pw_blk, wb_blk, wb_blk, wc_blk, wc_blk,
                  pl.BlockSpec((1, LANES), lambda b, k: (0, k))],
        out_specs=[seq_blk, st_blk, st_blk],
        out_shape=[jax.ShapeDtypeStruct((bsz, seq, S5_WIDTH), F32),
                   jax.ShapeDtypeStruct((bsz, 1, S5_CH), F32),
                   jax.ShapeDtypeStruct((bsz, 1, S5_CH), F32)],
        scratch_shapes=[pltpu.VMEM((nb, seq, S5_BLK_CH), F32), pltpu.VMEM((nb, seq, S5_BLK_CH), F32)],
        compiler_params=_params("parallel", "parallel"),
        name="s5_scan",
    )(z3, h0r.reshape(bsz, 1, S5_CH), h0i.reshape(bsz, 1, S5_CH), pwr, pwi, wbr, wbi, wcr, wci, d)
    return y, hr.reshape(bsz, S5_GROUPS, S5_STATE), hi.reshape(bsz, S5_GROUPS, S5_STATE)


S5_LANE_TILES = S5_BLK_CH // LANES


def _s5_long_kernel(u_ref, h0r_ref, h0i_ref, ar_ref, ai_ref, wbr_ref, wbi_ref, wcr_ref, wci_ref, d_ref,
                    y_ref, hr_ref, hi_ref, xr_scr, xi_scr, cr_scr, ci_scr, *, tl):
    step = pl.program_id(1)

    @pl.when(step == 0)
    def _():
        cr_scr[...] = h0r_ref[0]
        ci_scr[...] = h0i_ref[0]

    for k in range(S5_BLKS):
        ub = u_ref[0, :, k * LANES:(k + 1) * LANES].astype(BF16)
        bur = jnp.dot(ub, wbr_ref[k], preferred_element_type=F32)
        bui = jnp.dot(ub, wbi_ref[k], preferred_element_type=F32)
        for j in range(S5_LANE_TILES):
            xr_scr[j, k * tl:(k + 1) * tl, :] = bur[:, j * LANES:(j + 1) * LANES]
            xi_scr[j, k * tl:(k + 1) * tl, :] = bui[:, j * LANES:(j + 1) * LANES]

    ar = [ar_ref[:, j * LANES:(j + 1) * LANES] for j in range(S5_LANE_TILES)]
    ai = [ai_ref[:, j * LANES:(j + 1) * LANES] for j in range(S5_LANE_TILES)]

    def token(t, carry):
        cr, ci = carry
        nr, ni = [], []
        for j in range(S5_LANE_TILES):
            rows = pl.ds(t, S5_BLKS, stride=tl)
            xr, xi = _cmul_add(xr_scr[j, rows, :], xi_scr[j, rows, :], ar[j], ai[j], cr[j], ci[j])
            xr_scr[j, rows, :] = xr
            xi_scr[j, rows, :] = xi
            nr.append(xr)
            ni.append(xi)
        return tuple(nr), tuple(ni)

    init = (tuple(cr_scr[:, j * LANES:(j + 1) * LANES] for j in range(S5_LANE_TILES)),
            tuple(ci_scr[:, j * LANES:(j + 1) * LANES] for j in range(S5_LANE_TILES)))
    cr, ci = lax.fori_loop(0, tl, token, init, unroll=8)
    cr = jnp.concatenate(cr, axis=-1)
    ci = jnp.concatenate(ci, axis=-1)
    cr_scr[...] = cr
    ci_scr[...] = ci
    hr_ref[0] = cr
    hi_ref[0] = ci

    for k in range(S5_BLKS):
        xr = jnp.concatenate([xr_scr[j, k * tl:(k + 1) * tl, :] for j in range(S5_LANE_TILES)], axis=-1)
        xi = jnp.concatenate([xi_scr[j, k * tl:(k + 1) * tl, :] for j in range(S5_LANE_TILES)], axis=-1)
        u = u_ref[0, :, k * LANES:(k + 1) * LANES]
        y = (jnp.dot(xr.astype(BF16), wcr_ref[k], preferred_element_type=F32)
             - jnp.dot(xi.astype(BF16), wci_ref[k], preferred_element_type=F32)
             + d_ref[:, k * LANES:(k + 1) * LANES] * u)
        y_ref[0, :, k * LANES:(k + 1) * LANES] = jax.nn.gelu(y)


def s5_scan_long(z3, h0r, h0i, pwr, pwi, wbr, wbi, wcr, wci, d, tl):
    bsz, seq, _ = z3.shape
    assert seq % tl == 0 and tl % SUBLANES == 0
    seq_blk = pl.BlockSpec((1, tl, S5_WIDTH), lambda b, t: (b, t, 0))
    st_blk = pl.BlockSpec((1, S5_BLKS, S5_BLK_CH), lambda b, t: (b, 0, 0))
    lam_blk = pl.BlockSpec((S5_BLKS, S5_BLK_CH), lambda b, t: (0, 0))
    wb_blk = pl.BlockSpec((S5_BLKS, LANES, S5_BLK_CH), lambda b, t: (0, 0, 0))
    wc_blk = pl.BlockSpec((S5_BLKS, S5_BLK_CH, LANES), lambda b, t: (0, 0, 0))
    rows = pltpu.VMEM((S5_LANE_TILES, S5_BLKS * tl, LANES), F32)
    carry = pltpu.VMEM((S5_BLKS, S5_BLK_CH), F32)
    y, hr, hi = pl.pallas_call(
        functools.partial(_s5_long_kernel, tl=tl),
        grid=(bsz, seq // tl),
        in_specs=[seq_blk, st_blk, st_blk, lam_blk, lam_blk, wb_blk, wb_blk, wc_blk, wc_blk,
                  pl.BlockSpec((1, S5_WIDTH), lambda b, t: (0, 0))],
        out_specs=[seq_blk, st_blk, st_blk],
        out_shape=[jax.ShapeDtypeStruct((bsz, seq, S5_WIDTH), F32),
                   jax.ShapeDtypeStruct((bsz, S5_BLKS, S5_BLK_CH), F32),
                   jax.ShapeDtypeStruct((bsz, S5_BLKS, S5_BLK_CH), F32)],
        scratch_shapes=[rows, rows, carry, carry],
        compiler_params=_params("parallel", "arbitrary"),
        name="s5_scan_long",
    )(z3, h0r.reshape(bsz, S5_BLKS, S5_BLK_CH), h0i.reshape(bsz, S5_BLKS, S5_BLK_CH),
      pwr[0].reshape(S5_BLKS, S5_BLK_CH), pwi[0].reshape(S5_BLKS, S5_BLK_CH), wbr, wbi, wcr, wci, d)
    return y, hr.reshape(bsz, S5_GROUPS, S5_STATE), hi.reshape(bsz, S5_GROUPS, S5_STATE)


def _s5_block_weights(bbr_t, bbi_t, c_re, c_im):
    eye = jnp.eye(S5_BLK_GROUPS, dtype=F32)

    def wb(bt):
        b4 = bt.reshape(S5_GROUP, S5_BLKS, S5_BLK_GROUPS, S5_STATE)
        w = jnp.einsum("cbgp,hg->bhcgp", b4, eye)
        return w.reshape(S5_BLKS, LANES, S5_BLK_CH).astype(BF16)

    def wc(c):
        c4 = c.reshape(S5_BLKS, S5_BLK_GROUPS, S5_GROUP, S5_STATE)
        w = jnp.einsum("bgcp,hg->bhpgc", c4, eye)
        return w.reshape(S5_BLKS, S5_BLK_CH, LANES).astype(BF16)

    return wb(bbr_t), wb(bbi_t), wc(c_re), wc(c_im)


def _hg_lb_kernel(x_ref, o_ref):
    x = x_ref[...]
    e = jnp.exp(x - jnp.max(x, axis=0, keepdims=True))
    sm = e / jnp.sum(e, axis=0, keepdims=True)
    acc = sm[0:1]
    o_ref[0:1, :] = acc
    for l in range(1, x.shape[0]):
        acc = acc + sm[l:l + 1]
        o_ref[l:l + 1, :] = acc


def hg_lower_bounds(hg_lb):
    return pl.pallas_call(_hg_lb_kernel, out_shape=jax.ShapeDtypeStruct(hg_lb.shape, F32), name="hg_lb")(hg_lb)


def _cumsum_rows(x, n):
    row = lax.broadcasted_iota(jnp.int32, x.shape, 1)
    d = 1
    while d < n:
        x = x + jnp.where(row >= d, pltpu.roll(x, d, axis=1), 0.0)
        d *= 2
    return x


def _hgrn_kernel(q_ref, f_ref, i_ref, g_ref, s0_ref, lb_ref, nw_ref, o_ref, sf_ref, st_scr, *, nb, hh, chunk, n_sub):
    step = pl.program_id(2)

    def units(x):
        return jnp.concatenate([x[:, :, h * LANES:(h + 1) * LANES] for h in range(hh)], axis=0)

    lb = units(jnp.broadcast_to(lb_ref[...][None], (nb, 1, hh * LANES)))
    nw = nw_ref[...][None]

    @pl.when(step == 0)
    def _():
        for h in range(hh):
            for b in range(nb):
                st_scr[h * nb + b] = s0_ref[b, h].T

    trow = lax.broadcasted_iota(jnp.int32, (chunk, chunk), 0)
    tcol = lax.broadcasted_iota(jnp.int32, (chunk, chunk), 1)
    causal = (tcol <= trow)[None]
    st = st_scr[...]
    for c in range(n_sub):
        rows = slice(c * chunk, (c + 1) * chunk)
        q, f, v, g = (units(ref[:, rows, :]) for ref in (q_ref, f_ref, i_ref, g_ref))
        fg = lb + (1.0 - lb) * jax.nn.sigmoid(f)
        qh = jax.nn.silu(q)
        kh = 1.0 - fg
        bcum = _cumsum_rows(jnp.log(fg), chunk)
        btot = bcum[:, chunk - 1:chunk, :]
        q_in = (qh * jnp.exp(bcum)).astype(BF16)
        k_in = (kh * jnp.exp(-bcum)).astype(BF16)
        k_end = (kh * jnp.exp(btot - bcum)).astype(BF16)
        decay = jnp.exp(btot)
        vb = v.astype(BF16)
        att = jnp.einsum("utk,usk->uts", q_in, k_in, preferred_element_type=F32)
        att = jnp.where(causal, att, 0.0).astype(BF16)
        out = (jnp.einsum("utk,uvk->utv", q_in, st.astype(BF16), preferred_element_type=F32)
               + jnp.einsum("uts,usv->utv", att, vb, preferred_element_type=F32))
        st = st * decay + jnp.einsum("usv,usk->uvk", vb, k_end, preferred_element_type=F32)
        out = out * lax.rsqrt(jnp.mean(out * out, axis=-1, keepdims=True) + EPS) * nw
        out = (out * jax.nn.silu(g)).astype(o_ref.dtype)
        for h in range(hh):
            o_ref[:, rows, h * LANES:(h + 1) * LANES] = out[h * nb:(h + 1) * nb]
    st_scr[...] = st

    @pl.when(step == pl.num_programs(2) - 1)
    def _():
        for h in range(hh):
            for b in range(nb):
                sf_ref[b, h] = st[h * nb + b].T


def hgrn2(z3, s0, lb, norm_w, nb, hh):
    bsz, seq, _ = z3.shape
    chunk = min(HG_CHUNK, seq)
    n_sub = next(n for n in (3, 2, 1) if seq % (n * chunk) == 0)
    rows = n_sub * chunk
    assert bsz % nb == 0 and HG_HEADS % hh == 0
    wid = hh * LANES
    n_col = (HG_HEADS * HG_K) // wid

    def col(proj):
        return pl.BlockSpec((nb, rows, wid), lambda h, b, t, proj=proj: (b, t, proj * n_col + h))

    st_blk = pl.BlockSpec((nb, hh, HG_K, HG_V), lambda h, b, t: (b, h, 0, 0))
    out, sf = pl.pallas_call(
        functools.partial(_hgrn_kernel, nb=nb, hh=hh, chunk=chunk, n_sub=n_sub),
        grid=(HG_HEADS // hh, bsz // nb, seq // rows),
        in_specs=[col(1), col(2), col(3), col(4), st_blk,
                  pl.BlockSpec((1, wid), lambda h, b, t: (0, h)),
                  pl.BlockSpec((1, LANES), lambda h, b, t: (0, 0))],
        out_specs=[pl.BlockSpec((nb, rows, wid), lambda h, b, t: (b, t, h)), st_blk],
        out_shape=[jax.ShapeDtypeStruct((bsz, seq, HG_HEADS * HG_V), BF16),
                   jax.ShapeDtypeStruct((bsz, HG_HEADS, HG_K, HG_V), F32)],
        scratch_shapes=[pltpu.VMEM((hh * nb, HG_V, HG_K), F32)],
        compiler_params=_params("parallel", "parallel", "arbitrary"),
        name="hgrn2",
    )(z3, z3, z3, z3, s0, lb.reshape(1, HG_HEADS * HG_K), norm_w.reshape(1, HG_V))
    return out, sf


RW_MIXES = 6


def _norm_mix_kernel(h_ref, sh_ref, w_ref, mu_ref, *refs, tl):
    o_refs = refs[:RW_MIXES]
    last_ref, scr = refs[RW_MIXES:]
    x = h_ref[...]
    xn = x * lax.rsqrt(jnp.mean(x * x, axis=-1, keepdims=True) + EPS) * w_ref[...][None]

    @pl.when(pl.program_id(1) == 0)
    def _():
        scr[:, SUBLANES - 1:SUBLANES, :] = sh_ref[...]

    scr[:, SUBLANES:, :] = xn
    xx = scr[:, SUBLANES - 1:SUBLANES - 1 + tl, :] - xn
    for j, o_ref in enumerate(o_refs):
        o_ref[...] = (xn + xx * mu_ref[j:j + 1, :][None]).astype(o_ref.dtype)
    last = xn[:, tl - 1:tl, :]
    scr[:, SUBLANES - 1:SUBLANES, :] = last
    last_ref[...] = last


def norm_mix(h3, shift0, ln_w, mu, nb, tl):
    bsz, seq, d = h3.shape
    assert bsz % nb == 0 and seq % tl == 0 and tl % SUBLANES == 0
    blk = pl.BlockSpec((nb, tl, d), lambda b, t: (b, t, 0))
    row = pl.BlockSpec((nb, 1, d), lambda b, t: (b, 0, 0))
    outs = pl.pallas_call(
        functools.partial(_norm_mix_kernel, tl=tl),
        grid=(bsz // nb, seq // tl),
        in_specs=[blk, row, pl.BlockSpec((1, d), lambda b, t: (0, 0)), pl.BlockSpec((RW_MIXES, d), lambda b, t: (0, 0))],
        out_specs=[blk] * RW_MIXES + [row],
        out_shape=[jax.ShapeDtypeStruct((bsz, seq, d), BF16)] * RW_MIXES + [jax.ShapeDtypeStruct((bsz, 1, d), F32)],
        scratch_shapes=[pltpu.VMEM((nb, SUBLANES + tl, d), F32)],
        compiler_params=_params("parallel", "arbitrary"),
        name="norm_mix",
    )(h3, shift0.reshape(bsz, 1, d), ln_w.reshape(1, d), mu)
    return outs[:RW_MIXES], outs[RW_MIXES].reshape(bsz, d)


RW_PAIR = LANES // RW_HEAD
RW_DECAY_SCALE = math.exp(-0.5)
RW_SOLVE_BLOCK = 8


def _rwkv_kernel(r_ref, k_ref, v_ref, tw_ref, ta_ref, tg_ref, w2_ref, a2_ref, g2_ref, s0_ref,
                 w0_ref, a0_ref, kk_ref, ka_ref, rk_ref, lnw_ref, lnb_ref,
                 o_ref, sf_ref, s_scr, *, nb, hp, chunk):
    nu = hp * nb
    step = pl.program_id(2)
    lane = lax.broadcasted_iota(jnp.int32, (1, 1, LANES), 2)
    head1 = lane >= RW_HEAD

    def units(x):
        return jnp.concatenate([x[:, :, p * LANES:(p + 1) * LANES] for p in range(hp)], axis=0)

    def unit_rows(ref):
        return units(jnp.broadcast_to(ref[...][None], (nb, 1, hp * LANES)))

    w0, a0, k_k, k_a, r_k, ln_w, ln_b = (unit_rows(p) for p in
                                         (w0_ref, a0_ref, kk_ref, ka_ref, rk_ref, lnw_ref, lnb_ref))
    sq_row = lax.broadcasted_iota(jnp.int32, (LANES, LANES), 0) >= RW_HEAD
    sq_col = lax.broadcasted_iota(jnp.int32, (LANES, LANES), 1) >= RW_HEAD
    same_head = sq_row == sq_col
    ones_bd = same_head.astype(BF16)

    @pl.when(step == 0)
    def _():
        zero = jnp.zeros((nb, RW_HEAD, RW_HEAD), F32)
        for p in range(hp):
            top = jnp.concatenate([s0_ref[:, RW_PAIR * p], zero], axis=-1)
            bot = jnp.concatenate([zero, s0_ref[:, RW_PAIR * p + 1]], axis=-1)
            s_scr[p * nb:(p + 1) * nb] = jnp.concatenate([top, bot], axis=1)

    def bdot(spec, a, b):
        return jnp.einsum(spec, a.astype(BF16), b.astype(BF16), preferred_element_type=F32)

    def head_sum(x, two_pass=True):
        x2 = x.reshape(nu * chunk, LANES)
        hi = x2.astype(BF16)
        s = jnp.dot(hi, ones_bd, preferred_element_type=F32)
        if two_pass:
            lo = (x2 - hi.astype(F32)).astype(BF16)
            s = s + jnp.dot(lo, ones_bd, preferred_element_type=F32)
        return s.reshape(nu, chunk, LANES)

    def low_rank(t_ref, w_ref):
        t2 = t_ref[...].reshape(nb * chunk, t_ref.shape[-1])
        return units(jnp.dot(t2, w_ref[...], preferred_element_type=F32).reshape(nb, chunk, hp * LANES))

    srow = lax.broadcasted_iota(jnp.int32, (chunk, chunk), 0)
    scol = lax.broadcasted_iota(jnp.int32, (chunk, chunk), 1)
    tri = jnp.broadcast_to((scol <= srow).astype(BF16)[None], (nu, chunk, chunk))

    def tri_sum(x):
        hi = x.astype(BF16)
        lo = (x - hi.astype(F32)).astype(BF16)
        return (jnp.einsum("uts,usc->utc", tri, hi, preferred_element_type=F32)
                + jnp.einsum("uts,usc->utc", tri, lo, preferred_element_type=F32))

    def stack_heads(x):
        return jnp.concatenate([jnp.where(head1, 0.0, x), jnp.where(head1, x, 0.0)], axis=1).astype(BF16)

    r, k, v = (units(ref[...]) for ref in (r_ref, k_ref, v_ref))
    wl, al, g = low_rank(tw_ref, w2_ref), low_rank(ta_ref, a2_ref), low_rank(tg_ref, g2_ref)
    lw = (-RW_DECAY_SCALE) * jax.nn.sigmoid(w0 + wl)
    ag = jax.nn.sigmoid(a0 + al)
    kk = k * k_k
    kk = kk * jnp.minimum(lax.rsqrt(head_sum(kk * kk)), 1e12)
    k2 = k * (1.0 + (ag - 1.0) * k_a)
    cl = tri_sum(lw)
    e_pos = jnp.exp(cl)
    e_neg = jnp.exp(-cl)
    at = (-kk) * jnp.exp(cl - lw)
    bt = (kk * ag) * e_neg
    kt = k2 * e_neg
    rt = r * e_pos
    wc = e_pos[:, chunk - 1:chunk, :]
    trow = lax.broadcasted_iota(jnp.int32, (chunk, RW_PAIR * chunk), 0)
    tcol = lax.broadcasted_iota(jnp.int32, (chunk, RW_PAIR * chunk), 1)
    tcol = jnp.where(tcol >= chunk, tcol - chunk, tcol)
    strict = (tcol < trow)[None]
    incl = (tcol <= trow)[None]
    x2 = jnp.concatenate([at, rt], axis=1)
    pb = bdot("utc,usc->uts", x2, stack_heads(bt))
    pk = bdot("utc,usc->uts", x2, stack_heads(kt))
    lab = jnp.where(strict, pb[:, :chunk], 0.0)
    lak = jnp.where(strict, pk[:, :chunk], 0.0)
    arb = jnp.where(incl, pb[:, chunk:], 0.0)
    ark = jnp.where(incl, pk[:, chunk:], 0.0)
    v_bd = stack_heads(v)
    xa = at
    xv = bdot("uts,usc->utc", lak, v_bd)
    sub = min(RW_SOLVE_BLOCK, chunk)
    done_a, done_v = [], []
    for lo in range(0, chunk, sub):
        xa_i = xa[:, lo:lo + sub, :]
        xv_i = xv[:, lo:lo + sub, :]
        if lo:
            pad = jnp.zeros((nu, chunk - lo, LANES), F32)
            prev = jnp.concatenate([stack_heads(jnp.concatenate(done_a + [pad], axis=1)),
                                    stack_heads(jnp.concatenate(done_v + [pad], axis=1))], axis=-1)
            upd = bdot("uts,usc->utc", lab[:, lo:lo + sub, :], prev)
            xa_i = xa_i + upd[:, :, :LANES]
            xv_i = xv_i + upd[:, :, LANES:]
        l0 = lab[:, lo:lo + sub, lo:lo + sub]
        l1 = lab[:, lo:lo + sub, chunk + lo:chunk + lo + sub]
        for s in range(sub - 1):
            m = jnp.where(head1, l1[:, :, s:s + 1], l0[:, :, s:s + 1])
            xa_i = xa_i + m * xa_i[:, s:s + 1, :]
            xv_i = xv_i + m * xv_i[:, s:s + 1, :]
        done_a.append(xa_i)
        done_v.append(xv_i)
    ah = jnp.concatenate(done_a, axis=1)
    vh = jnp.concatenate(done_v, axis=1)
    both = bdot("uts,usc->utc", arb, jnp.concatenate([stack_heads(ah), stack_heads(vh)], axis=-1))
    rh = rt + both[:, :, :LANES]
    yh = both[:, :, LANES:] + bdot("uts,usc->utc", ark, v_bd)
    gp = jnp.where(same_head, bdot("utj,utk->ujk", ah, bt), 0.0)
    ht = jnp.where(same_head, bdot("utv,utk->uvk", jnp.concatenate([vh, v], axis=1),
                                   jnp.concatenate([bt, kt], axis=1)), 0.0)
    st = s_scr[...]
    y = bdot("utk,uvk->utv", rh, st) + yh
    st = (st + bdot("uvj,ujk->uvk", st, gp) + ht) * wc
    s_scr[...] = st
    inv_n = 1.0 / RW_HEAD
    yc = y - head_sum(y, two_pass=False) * inv_n
    var = head_sum(yc * yc, two_pass=False) * inv_n
    y = yc * lax.rsqrt(var + RW_GN_EPS) * ln_w + ln_b
    y = y + head_sum(r * k2 * r_k, two_pass=False) * v
    out = (y * g).astype(o_ref.dtype)
    for p in range(hp):
        o_ref[:, :, p * LANES:(p + 1) * LANES] = out[p * nb:(p + 1) * nb]

    @pl.when(step == pl.num_programs(2) - 1)
    def _():
        for p in range(hp):
            sf_ref[:, RW_PAIR * p] = st[p * nb:(p + 1) * nb, :RW_HEAD, :RW_HEAD]
            sf_ref[:, RW_PAIR * p + 1] = st[p * nb:(p + 1) * nb, RW_HEAD:, RW_HEAD:]


def _rwkv_chunk(seq):
    for c in (48, 32, 16, 8):
        if seq % c == 0:
            return c
    raise ValueError(seq)


def rwkv7(r, k, v, low, low_w, s0, w0, a0, k_k, k_a, r_k, ln_w, ln_b, nb, hp):
    bsz, seq, d = r.shape
    chunk = _rwkv_chunk(seq)
    heads = hp * RW_PAIR
    assert bsz % nb == 0 and RW_HEADS % heads == 0
    seq_blk = pl.BlockSpec((nb, chunk, hp * LANES), lambda h, b, t: (b, t, h))
    st_blk = pl.BlockSpec((nb, heads, RW_HEAD, RW_HEAD), lambda h, b, t: (b, h, 0, 0))
    vec = pl.BlockSpec((1, hp * LANES), lambda h, b, t: (0, h))
    low_blk = [pl.BlockSpec((nb, chunk, x.shape[-1]), lambda h, b, t: (b, t, 0)) for x in low]
    low_w_blk = [pl.BlockSpec((w.shape[0], hp * LANES), lambda h, b, t: (0, h)) for w in low_w]
    out, sf = pl.pallas_call(
        functools.partial(_rwkv_kernel, nb=nb, hp=hp, chunk=chunk),
        grid=(RW_HEADS // heads, bsz // nb, seq // chunk),
        in_specs=[seq_blk] * 3 + low_blk + low_w_blk + [st_blk] + [vec] * 7,
        out_specs=[seq_blk, st_blk],
        out_shape=[jax.ShapeDtypeStruct((bsz, seq, d), BF16),
                   jax.ShapeDtypeStruct((bsz, RW_HEADS, RW_HEAD, RW_HEAD), F32)],
        scratch_shapes=[pltpu.VMEM((hp * nb, LANES, LANES), F32)],
        compiler_params=_params("parallel", "parallel", "arbitrary"),
        name="rwkv7",
    )(r, k, v, *low, *low_w, s0, *(p.reshape(1, d) for p in (w0, a0, k_k, k_a, r_k, ln_w, ln_b)))
    return out, sf


def _ffn_in_kernel(x_ref, wa_ref, wv_ref, e_ref, cw_ref, cb_ref, o_ref, st_ref, scr, *, nb, seq, sb, sr):
    tn = wa_ref.shape[1]
    cw = cw_ref[...]
    cb = cb_ref[...][None]
    scr[:, SUBLANES - (CONV_W - 1):SUBLANES, :] = e_ref[...]
    for b0 in range(0, nb, sb):
        for r0 in range(0, seq, sr):
            lo = b0 * seq + r0
            x = x_ref[lo:lo + sb * sr, :]
            a = jnp.dot(x, wa_ref[...], preferred_element_type=F32).reshape(sb, sr, tn)
            v = jnp.dot(x, wv_ref[...], preferred_element_type=F32).reshape(sb, sr, tn)
            scr[b0:b0 + sb, SUBLANES + r0:SUBLANES + r0 + sr, :] = a
            c = cb + cw[CONV_W - 1:CONV_W][None] * a
            for j in range(CONV_W - 1):
                first = SUBLANES + r0 - (CONV_W - 1 - j)
                c = c + cw[j:j + 1][None] * scr[b0:b0 + sb, first:first + sr, :]
            o_ref[lo:lo + sb * sr, :] = (jax.nn.gelu(c) * v).reshape(sb * sr, tn).astype(o_ref.dtype)
    st_ref[...] = scr[:, SUBLANES + seq - (CONV_W - 1):SUBLANES + seq, :]


def ffn_in(xb, conv0, w_in, layer, conv_w, conv_b, bsz, seq, nb, tn, sub):
    t, d = xb.shape
    sb, sr = sub
    assert t == bsz * seq and bsz % nb == 0 and D_FF % tn == 0
    assert nb % sb == 0 and seq % sr == 0 and sr % SUBLANES == 0 and (sr == seq or nb == sb == 1)
    col = lambda i, j: (0, j)
    n_col = D_FF // tn
    out, st = pl.pallas_call(
        functools.partial(_ffn_in_kernel, nb=nb, seq=seq, sb=sb, sr=sr),
        grid=(bsz // nb, n_col),
        in_specs=[pl.BlockSpec((nb * seq, d), lambda i, j: (i, 0)),
                  pl.BlockSpec((None, d, tn), lambda i, j: (layer, 0, j)),
                  pl.BlockSpec((None, d, tn), lambda i, j: (layer, 0, j + n_col)),
                  pl.BlockSpec((nb, CONV_W - 1, tn), lambda i, j: (i, 0, j)),
                  pl.BlockSpec((CONV_W, tn), col), pl.BlockSpec((1, tn), col)],
        out_specs=[pl.BlockSpec((nb * seq, tn), lambda i, j: (i, j)),
                   pl.BlockSpec((nb, CONV_W - 1, tn), lambda i, j: (i, 0, j))],
        out_shape=[jax.ShapeDtypeStruct((t, D_FF), BF16),
                   jax.ShapeDtypeStruct((bsz, CONV_W - 1, D_FF), F32)],
        scratch_shapes=[pltpu.VMEM((nb, SUBLANES + seq, tn), F32)],
        compiler_params=_params("parallel", "parallel"),
        name="ffn_in",
    )(xb, w_in, w_in, conv0, conv_w, conv_b.reshape(1, D_FF))
    return out, st


FFN_DOWN_K_STEPS = 2


def _channel_mixer(h, conv0, layer, p, cfg, bsz, seq):
    (xb,) = rmsnorm(h, p["ln_ffn"][layer], (BF16,))
    gated, n_cv = ffn_in(xb, conv0, p["ffn_w_in"], layer, p["ffn_conv_w"][layer], p["ffn_conv_b"][layer],
                         bsz, seq, cfg["ffn_nb"], cfg["ffn_tn"], cfg["ffn_sub"])
    return matmul_residual_split(gated, p["ffn_w_down"], layer, h, FFN_DOWN_K_STEPS), n_cv


def _trunk(x3, s5r, s5i, hg, rw, sh, cv, p, cfg):
    bsz, seq, d = x3.shape
    seq += cfg["front"]
    t = bsz * seq

    if cfg["front"]:
        h, xb = embed_norm(x3, p["meta"], p["ln_mix"][0])
    else:
        h = x3.reshape(t, d)
        (xb,) = rmsnorm(h, p["ln_mix"][0], (BF16,))
    z = matmul([xb], [p["ev_w_in"]], F32).reshape(bsz, seq, EVEN_IN)
    if cfg["s5_tl"]:
        ys5, n_s5r, n_s5i = s5_scan_long(z, s5r[0], s5i[0], *p["s5"], tl=cfg["s5_tl"])
    else:
        ys5, n_s5r, n_s5i = s5_scan(z, s5r[0], s5i[0], *p["s5"], nb=cfg["s5_nb"])
    ys5 = ys5.reshape(t, S5_WIDTH)
    ya = matmul([ys5], [p["s5_w_glu"]], BF16, epilogue="glu", extra=ys5)
    yb, n_hg = hgrn2(z, hg[0], p["hg_lb"], p["hg_norm_w"], nb=cfg["hg_nb"], hh=cfg["hg_hh"])
    h = matmul([ya, yb.reshape(t, -1)], [(p["ev_w_out"], 0, 0), (p["ev_w_out"], 0, 1)], F32,
               epilogue="residual", extra=h)
    h, n_cv0 = _channel_mixer(h, cv[0], 0, p, cfg, bsz, seq)

    mixes, n_sh = norm_mix(h.reshape(bsz, seq, d), sh[0], p["ln_mix"][1], p["rw_mu"], *cfg["mix_blk"])
    xr, xw, xk, xv, xa, xg = (m.reshape(t, d) for m in mixes)
    r = matmul([xr], [p["rw_w_r"]], F32)
    k = matmul([xk], [p["rw_w_k"]], F32)
    v = matmul([xv], [p["rw_w_v"]], F32)
    low = (matmul([xw], [p["rw_w1"]], BF16, act="tanh"), matmul([xa], [p["rw_a1"]], BF16),
           matmul([xg], [p["rw_g1"]], BF16, act="sigmoid"))
    as3 = lambda a: a.reshape(bsz, seq, a.shape[-1])
    yo, n_rw = rwkv7(as3(r), as3(k), as3(v), [as3(x) for x in low], (p["rw_w2"], p["rw_a2"], p["rw_g2"]),
                     rw[0], *p["rw_vec"], nb=cfg["rw_nb"], hp=cfg["rw_hp"])
    h = matmul([yo.reshape(t, d)], [p["rw_w_o"]], F32, epilogue="residual", extra=h)
    h, n_cv1 = _channel_mixer(h, cv[1], 1, p, cfg, bsz, seq)

    (y,) = rmsnorm(h, p["ln_final"], (F32,), rows=(bsz, seq, cfg["front"]) if cfg["front"] else None)
    return (y.reshape(bsz, seq - cfg["front"], d), n_s5r[None], n_s5i[None], n_hg[None], n_rw[None], n_sh[None],
            jnp.stack([n_cv0, n_cv1]))


PROMPT_CFG = dict(front=N_META, s5_tl=344, s5_nb=None, hg_nb=4, hg_hh=8, mix_blk=(1, 344), rw_nb=4, rw_hp=16, ffn_nb=1, ffn_tn=512, ffn_sub=(1, 688))
SAMPLE_CFG = dict(front=0, s5_tl=None, s5_nb=32, hg_nb=4, hg_hh=8, mix_blk=(32, 8), rw_nb=16, rw_hp=4, ffn_nb=128, ffn_tn=512, ffn_sub=(32, 8))


def kernel(x_prompt, x_sample, state_s5_re, state_s5_im, state_hgrn, state_rwkv, state_shift, state_conv, meta_tokens, ln_mix, ln_ffn, ln_final, ev_w_in, ev_w_out, s5_lam_re, s5_lam_im, s5_log_step, s5_b_re, s5_b_im, s5_c_re, s5_c_im, s5_d, s5_w_glu, hg_lb, hg_norm_w, rw_mu, rw_w0, rw_w1, rw_w2, rw_a0, rw_a1, rw_a2, rw_g1, rw_g2, rw_k_k, rw_k_a, rw_r_k, rw_w_r, rw_w_k, rw_w_v, rw_w_o, rw_ln_w, rw_ln_b, ffn_w_in, ffn_conv_w, ffn_conv_b, ffn_w_down):
    bf = lambda w: w.astype(BF16)
    lb_all = hg_lower_bounds(hg_lb)
    pwr, pwi, bbr_t, bbi_t = s5_prep(s5_lam_re[0], s5_lam_im[0], s5_log_step[0], s5_b_re[0], s5_b_im[0])
    wbr, wbi, wcr, wci = _s5_block_weights(bbr_t, bbi_t, s5_c_re[0], s5_c_im[0])
    p = {
        "meta": meta_tokens, "ln_mix": ln_mix, "ln_ffn": ln_ffn, "ln_final": ln_final,
        "ev_w_in": ev_w_in[0],
        "ev_w_out": ev_w_out,
        "s5": (pwr, pwi, wbr, wbi, wcr, wci, s5_d[0].reshape(1, S5_WIDTH)),
        "s5_w_glu": s5_w_glu[0],
        "hg_lb": lb_all[0], "hg_norm_w": hg_norm_w[0],
        "rw_mu": rw_mu[0],
        "rw_w1": rw_w1[0], "rw_w2": bf(rw_w2[0]), "rw_a1": rw_a1[0], "rw_a2": bf(rw_a2[0]),
        "rw_g1": rw_g1[0], "rw_g2": bf(rw_g2[0]),
        "rw_w_r": rw_w_r[0], "rw_w_k": rw_w_k[0], "rw_w_v": rw_w_v[0], "rw_w_o": rw_w_o[0],
        "rw_vec": (rw_w0[0], rw_a0[0], rw_k_k[0], rw_k_a[0], rw_r_k[0].reshape(D_MODEL), rw_ln_w[0], rw_ln_b[0]),
        "ffn_w_in": bf(ffn_w_in), "ffn_conv_w": ffn_conv_w, "ffn_conv_b": ffn_conv_b,
        "ffn_w_down": bf(ffn_w_down),
    }

    bsz = x_prompt.shape[0]
    zeros = lambda *s: jnp.zeros(s, F32)
    outs_p = _trunk(x_prompt,
                    zeros(1, bsz, S5_GROUPS, S5_STATE), zeros(1, bsz, S5_GROUPS, S5_STATE),
                    zeros(1, bsz, HG_HEADS, HG_K, HG_V), zeros(1, bsz, RW_HEADS, RW_HEAD, RW_HEAD),
                    zeros(1, bsz, D_MODEL), zeros(2, bsz, CONV_W - 1, D_FF), p, PROMPT_CFG)
    outs_s = _trunk(x_sample, state_s5_re, state_s5_im, state_hgrn, state_rwkv, state_shift, state_conv,
                    p, SAMPLE_CFG)
    return tuple(outs_p[:1]) + tuple(outs_s[:1]) + tuple(outs_p[1:]) + tuple(outs_s[1:])
```

```python
import functools
import math

import jax
import jax.numpy as jnp
from jax import lax
from jax.experimental import pallas as pl
from jax.experimental.pallas import tpu as pltpu

F32 = jnp.float32
BF16 = jnp.bfloat16

D_MODEL = 2048
N_META = 16
EPS = 1e-6
S5_WIDTH = 1024
S5_GROUP = 16
S5_GROUPS = 64
S5_STATE = 64
S5_CH = S5_GROUPS * S5_STATE
HG_HEADS = 8
HG_K = 128
HG_V = 128
HG_CHUNK = 16
EVEN_IN = 5120
RW_HEAD = 64
RW_HEADS = 32
RW_GN_EPS = 64e-5
D_FF = 5632
CONV_W = 3

LANES = 128
SUBLANES = 8
VMEM_LIMIT = 56 * 1024 * 1024


def _params(*sem):
    return pltpu.CompilerParams(dimension_semantics=sem, vmem_limit_bytes=VMEM_LIMIT)


def _row_tile(t, cap=1024):
    best = None
    for d in range(16, min(t, cap) + 1, 16):
        if t % d == 0:
            best = d
    assert best is not None, t
    return best


def _rms_kernel(x_ref, w_ref, *o_refs):
    x = x_ref[...]
    y = x * lax.rsqrt(jnp.mean(x * x, axis=-1, keepdims=True) + EPS) * w_ref[...]
    for o_ref in o_refs:
        o_ref[...] = y.astype(o_ref.dtype)


def rmsnorm(x, w, dtypes, rows=None):
    t, d = x.shape
    if rows is None:
        tm = _row_tile(t)
        n_out = t
        grid = (t // tm,)
        in_spec = pl.BlockSpec((tm, d), lambda i: (i, 0))
        out_spec = in_spec
        vec = pl.BlockSpec((1, d), lambda i: (0, 0))
    else:
        bsz, seq, front = rows
        keep = seq - front
        assert t == bsz * seq and front % 16 == 0
        tm = _row_tile(keep)
        per = keep // tm
        n_out = bsz * keep
        grid = (bsz, per)
        in_spec = pl.BlockSpec((pl.Element(tm), pl.Element(d)),
                               lambda b, i: (pl.multiple_of(b * seq + front + i * tm, 16), 0))
        out_spec = pl.BlockSpec((tm, d), lambda b, i: (b * per + i, 0))
        vec = pl.BlockSpec((1, d), lambda b, i: (0, 0))
    outs = pl.pallas_call(
        _rms_kernel,
        grid=grid,
        in_specs=[in_spec, vec],
        out_specs=[out_spec for _ in dtypes],
        out_shape=[jax.ShapeDtypeStruct((n_out, d), dt) for dt in dtypes],
        compiler_params=_params(*(["parallel"] * len(grid))),
        name="rmsnorm",
    )(x, w.reshape(1, d))
    return outs


def _embed_norm_kernel(x_ref, m_ref, w_ref, h_ref, xb_ref, *, front):
    x = x_ref[...]
    first = pl.program_id(1) == 0
    body = jnp.where(first, pltpu.roll(x, front, axis=0), x)
    head = jnp.where(first, m_ref[...], x[:front])
    rows = jnp.concatenate([head, body[front:]], axis=0)
    h_ref[...] = rows
    y = rows * lax.rsqrt(jnp.mean(rows * rows, axis=-1, keepdims=True) + EPS) * w_ref[...]
    xb_ref[...] = y.astype(xb_ref.dtype)


def embed_norm(x3, meta, w):
    bsz, seq, d = x3.shape
    front = meta.shape[0]
    total = seq + front
    tm = _row_tile(total)
    per = total // tm
    assert front % 16 == 0 and tm > front
    blk = pl.BlockSpec((tm, d), lambda b, j: (b * per + j, 0))
    return pl.pallas_call(
        functools.partial(_embed_norm_kernel, front=front),
        grid=(bsz, per),
        in_specs=[pl.BlockSpec((pl.Element(tm), pl.Element(d)),
                               lambda b, j: (pl.multiple_of(b * seq + jnp.maximum(j * tm - front, 0), 16), 0)),
                  pl.BlockSpec((front, d), lambda b, j: (0, 0)),
                  pl.BlockSpec((1, d), lambda b, j: (0, 0))],
        out_specs=[blk, blk],
        out_shape=[jax.ShapeDtypeStruct((bsz * total, d), F32), jax.ShapeDtypeStruct((bsz * total, d), BF16)],
        compiler_params=_params("parallel", "parallel"),
        name="embed_norm",
    )(x3.reshape(bsz * seq, d), meta, w.reshape(1, d))


def _act(x, act):
    if act == "tanh":
        return jnp.tanh(x)
    if act == "sigmoid":
        return jax.nn.sigmoid(x)
    assert act is None
    return x


def _mm_kernel(*refs, n_a, act, epilogue, cast_w):
    a_refs = refs[:n_a]
    w_refs = refs[n_a:2 * n_a]
    rest = refs[2 * n_a:]
    if cast_w:
        rest, wb_refs = rest[:-n_a], rest[-n_a:]

        @pl.when(pl.program_id(1) == 0)
        def _():
            for w_ref, wb_ref in zip(w_refs, wb_refs):
                wb_ref[...] = w_ref[...].astype(BF16)

        w_refs = wb_refs
    o_ref = rest[-1]
    acc = jnp.dot(a_refs[0][...].astype(BF16), w_refs[0][...], preferred_element_type=F32)
    for a_ref, w_ref in zip(a_refs[1:], w_refs[1:]):
        acc = acc + jnp.dot(a_ref[...].astype(BF16), w_ref[...], preferred_element_type=F32)
    acc = _act(acc, act)
    if epilogue == "residual":
        acc = rest[0][...] + acc
    elif epilogue == "glu":
        acc = rest[0][...] * jax.nn.sigmoid(acc)
    o_ref[...] = acc.astype(o_ref.dtype)


MM_VMEM_BUDGET = 40 * 1024 * 1024
MM_CAST_VMEM_BUDGET = 48 * 1024 * 1024
MXU_WIDTH = 256


def _mm_tiles(t, a_row_bytes, w_col_bytes, n, out_bytes, has_extra, budget=MM_VMEM_BUDGET):
    rows = [d for d in range(16, t + 1, 16) if t % d == 0]
    cols = [d for d in range(LANES, n + 1, LANES) if n % d == 0] or [n]
    best, best_score = None, -1.0
    for tm in rows:
        for tn in cols:
            est = 2 * tm * a_row_bytes + w_col_bytes * tn + tm * tn * (2 * out_bytes + 4 + (8 if has_extra else 0))
            if est > budget:
                continue
            score = tm * tn * (1.0 if tn % MXU_WIDTH == 0 else 0.8)
            if score > best_score:
                best, best_score = (tm, tn), score
    assert best is not None, (t, a_row_bytes, w_col_bytes, n)
    return best


def matmul(a_list, w_list, out_dtype, act=None, epilogue=None, extra=None):
    t = a_list[0].shape[0]
    w0 = w_list[0][0] if isinstance(w_list[0], tuple) else w_list[0]
    n = w0.shape[-1]
    cast_w = w0.dtype == F32
    k_rows = sum(a.shape[1] for a in a_list)
    a_row_bytes = sum(a.shape[1] * a.dtype.itemsize for a in a_list)
    w_col_bytes = k_rows * (2 * 4 + 2 if cast_w else 2 * 2)
    tm, tn = _mm_tiles(t, a_row_bytes, w_col_bytes, n, jnp.dtype(out_dtype).itemsize, epilogue is not None,
                       budget=MM_CAST_VMEM_BUDGET if cast_w else MM_VMEM_BUDGET)
    rc = (lambda f: lambda j, i: f(i, j)) if cast_w else (lambda f: f)
    in_specs = [pl.BlockSpec((tm, a.shape[1]), rc(lambda i, j: (i, 0))) for a in a_list]
    args = list(a_list)
    for a, w in zip(a_list, w_list):
        if isinstance(w, tuple):
            w, layer, kblk = w
            in_specs.append(pl.BlockSpec((None, a.shape[1], tn),
                                         rc(lambda i, j, layer=layer, kblk=kblk: (layer, kblk, j))))
        else:
            in_specs.append(pl.BlockSpec((w.shape[0], tn), rc(lambda i, j: (0, j))))
        args.append(w)
    if epilogue is not None:
        in_specs.append(pl.BlockSpec((tm, tn), rc(lambda i, j: (i, j))))
        args.append(extra)
    return pl.pallas_call(
        functools.partial(_mm_kernel, n_a=len(a_list), act=act, epilogue=epilogue, cast_w=cast_w),
        grid=(n // tn, t // tm) if cast_w else (t // tm, n // tn),
        in_specs=in_specs,
        out_specs=pl.BlockSpec((tm, tn), rc(lambda i, j: (i, j))),
        out_shape=jax.ShapeDtypeStruct((t, n), out_dtype),
        scratch_shapes=[pltpu.VMEM((a.shape[1], tn), BF16) for a in a_list] if cast_w else [],
        compiler_params=_params("parallel", "arbitrary" if cast_w else "parallel"),
        name="matmul",
    )(*args)


def _mm_resid_split_kernel(a_ref, w_ref, res_ref, o_ref):
    part = jnp.dot(a_ref[...], w_ref[...], preferred_element_type=F32)

    @pl.when(pl.program_id(2) == 0)
    def _():
        o_ref[...] = res_ref[...] + part

    @pl.when(pl.program_id(2) > 0)
    def _():
        o_ref[...] += part


def matmul_residual_split(a, w, layer, res, k_steps):
    t, k = a.shape
    n = w.shape[2]
    tk = k // k_steps
    assert k % k_steps == 0 and tk % LANES == 0
    tm, tn = _mm_tiles(t, 2 * tk, 4 * tk, n, 4, True)
    return pl.pallas_call(
        _mm_resid_split_kernel,
        grid=(t // tm, n // tn, k_steps),
        in_specs=[pl.BlockSpec((tm, tk), lambda i, j, s: (i, s)),
                  pl.BlockSpec((None, tk, tn), lambda i, j, s: (layer, s, j)),
                  pl.BlockSpec((tm, tn), lambda i, j, s: (i, j))],
        out_specs=pl.BlockSpec((tm, tn), lambda i, j, s: (i, j)),
        out_shape=jax.ShapeDtypeStruct((t, n), F32),
        compiler_params=_params("parallel", "parallel", "arbitrary"),
        name="matmul_split",
    )(a, w, res)


def _s5_prep_kernel(lr_ref, li_ref, ls_ref, brt_ref, bit_ref, pwr_ref, pwi_ref, bbr_ref, bbi_ref):
    lr = jnp.minimum(lr_ref[...], -1e-4)
    li = li_ref[...]
    dt = jnp.exp(ls_ref[...])
    n = lax.broadcasted_iota(jnp.int32, (SUBLANES, S5_CH), 0).astype(F32) + 1.0
    mag = jnp.exp(n * (lr * dt))
    ang = n * (li * dt)
    pwr = mag * jnp.cos(ang)
    pwi = mag * jnp.sin(ang)
    pwr_ref[...] = pwr
    pwi_ref[...] = pwi
    ar = pwr[0:1]
    ai = pwi[0:1]
    den = lr * lr + li * li
    zr = ((ar - 1.0) * lr + ai * li) / den
    zi = (ai * lr - (ar - 1.0) * li) / den
    br = brt_ref[...]
    bi = bit_ref[...]
    bbr_ref[...] = zr * br - zi * bi
    bbi_ref[...] = zr * bi + zi * br


def s5_prep(lam_re, lam_im, log_step, b_re, b_im):
    lr = lam_re.reshape(1, S5_CH)
    li = lam_im.reshape(1, S5_CH)
    ls = jnp.broadcast_to(log_step[:, None], (S5_GROUPS, S5_STATE)).reshape(1, S5_CH)
    brt = b_re.reshape(S5_CH, S5_GROUP).T
    bit = b_im.reshape(S5_CH, S5_GROUP).T
    return pl.pallas_call(
        _s5_prep_kernel,
        out_shape=[jax.ShapeDtypeStruct((SUBLANES, S5_CH), F32)] * 2
        + [jax.ShapeDtypeStruct((S5_GROUP, S5_CH), F32)] * 2,
        name="s5_prep",
    )(lr, li, ls, brt, bit)


S5_BLK_GROUPS = LANES // S5_GROUP
S5_BLKS = S5_WIDTH // LANES
S5_BLK_CH = S5_BLK_GROUPS * S5_STATE


def _cmul_add(xr, xi, mr, mi, sr, si):
    return xr + mr * sr - mi * si, xi + mr * si + mi * sr


def _s5_kernel(u_ref, h0r_ref, h0i_ref, pwr_ref, pwi_ref, wbr_ref, wbi_ref, wcr_ref, wci_ref, d_ref,
               y_ref, hr_ref, hi_ref, xr_scr, xi_scr, *, nb, seq):
    u2 = u_ref[...].reshape(nb * seq, LANES)
    ub = u2.astype(BF16)
    xr_scr[...] = jnp.dot(ub, wbr_ref[0], preferred_element_type=F32).reshape(nb, seq, S5_BLK_CH)
    xi_scr[...] = jnp.dot(ub, wbi_ref[0], preferred_element_type=F32).reshape(nb, seq, S5_BLK_CH)

    pwr = pwr_ref[...]
    pwi = pwi_ref[...]
    row = lax.broadcasted_iota(jnp.int32, (SUBLANES, S5_BLK_CH), 0)
    steps = []
    for d in (1, 2, 4):
        keep = row >= d
        steps.append((d, jnp.where(keep, pwr[d - 1:d], 0.0)[None], jnp.where(keep, pwi[d - 1:d], 0.0)[None]))
    pr = pwr[None]
    pi = pwi[None]

    def tile(i, carry):
        cr, ci = carry
        o = pl.multiple_of(i * SUBLANES, SUBLANES)
        xr = xr_scr[:, pl.ds(o, SUBLANES), :]
        xi = xi_scr[:, pl.ds(o, SUBLANES), :]
        for d, mr, mi in steps:
            sr = pltpu.roll(xr, d, axis=1)
            si = pltpu.roll(xi, d, axis=1)
            xr, xi = _cmul_add(xr, xi, mr, mi, sr, si)
        xr, xi = _cmul_add(xr, xi, pr, pi, cr, ci)
        xr_scr[:, pl.ds(o, SUBLANES), :] = xr
        xi_scr[:, pl.ds(o, SUBLANES), :] = xi
        return xr[:, SUBLANES - 1:SUBLANES, :], xi[:, SUBLANES - 1:SUBLANES, :]

    hr, hi = lax.fori_loop(0, seq // SUBLANES, tile, (h0r_ref[...], h0i_ref[...]))
    hr_ref[...] = hr
    hi_ref[...] = hi

    xr = xr_scr[...].reshape(nb * seq, S5_BLK_CH).astype(BF16)
    xi = xi_scr[...].reshape(nb * seq, S5_BLK_CH).astype(BF16)
    y = (jnp.dot(xr, wcr_ref[0], preferred_element_type=F32)
         - jnp.dot(xi, wci_ref[0], preferred_element_type=F32)
         + d_ref[...] * u2)
    y_ref[...] = jax.nn.gelu(y).reshape(nb, seq, LANES)


def s5_scan(z3, h0r, h0i, pwr, pwi, wbr, wbi, wcr, wci, d, nb):
    bsz, seq, _ = z3.shape
    assert seq % SUBLANES == 0 and bsz % nb == 0
    seq_blk = pl.BlockSpec((nb, seq, LANES), lambda b, k: (b, 0, k))
    st_blk = pl.BlockSpec((nb, 1, S5_BLK_CH), lambda b, k: (b, 0, k))
    pw_blk = pl.BlockSpec((SUBLANES, S5_BLK_CH), lambda b, k: (0, k))
    wb_blk = pl.BlockSpec((1, LANES, S5_BLK_CH), lambda b, k: (k, 0, 0))
    wc_blk = pl.BlockSpec((1, S5_BLK_CH, LANES), lambda b, k: (k, 0, 0))
    y, hr, hi = pl.pallas_call(
        functools.partial(_s5_kernel, nb=nb, seq=seq),
        grid=(bsz // nb, S5_BLKS),
        in_specs=[seq_blk, st_blk, st_blk, pw_blk, pw_blk, wb_blk, wb_blk, wc_blk, wc_blk,
                  pl.BlockSpec((1, LANES), lambda b, k: (0, k))],
        out_specs=[seq_blk, st_blk, st_blk],
        out_shape=[jax.ShapeDtypeStruct((bsz, seq, S5_WIDTH), F32),
                   jax.ShapeDtypeStruct((bsz, 1, S5_CH), F32),
                   jax.ShapeDtypeStruct((bsz, 1, S5_CH), F32)],
        scratch_shapes=[pltpu.VMEM((nb, seq, S5_BLK_CH), F32), pltpu.VMEM((nb, seq, S5_BLK_CH), F32)],
        compiler_params=_params("parallel", "parallel"),
        name="s5_scan",
    )(z3, h0r.reshape(bsz, 1, S5_CH), h0i.reshape(bsz, 1, S5_CH), pwr, pwi, wbr, wbi, wcr, wci, d)
    return y, hr.reshape(bsz, S5_GROUPS, S5_STATE), hi.reshape(bsz, S5_GROUPS, S5_STATE)


S5_LANE_TILES = S5_BLK_CH // LANES


def _s5_long_kernel(u_ref, h0r_ref, h0i_ref, ar_ref, ai_ref, wbr_ref, wbi_ref, wcr_ref, wci_ref, d_ref,
                    y_ref, hr_ref, hi_ref, xr_scr, xi_scr, cr_scr, ci_scr, *, tl):
    step = pl.program_id(1)

    @pl.when(step == 0)
    def _():
        cr_scr[...] = h0r_ref[0]
        ci_scr[...] = h0i_ref[0]

    for k in range(S5_BLKS):
        ub = u_ref[0, :, k * LANES:(k + 1) * LANES].astype(BF16)
        bur = jnp.dot(ub, wbr_ref[k], preferred_element_type=F32)
        bui = jnp.dot(ub, wbi_ref[k], preferred_element_type=F32)
        for j in range(S5_LANE_TILES):
            xr_scr[j, k * tl:(k + 1) * tl, :] = bur[:, j * LANES:(j + 1) * LANES]
            xi_scr[j, k * tl:(k + 1) * tl, :] = bui[:, j * LANES:(j + 1) * LANES]

    ar = [ar_ref[:, j * LANES:(j + 1) * LANES] for j in range(S5_LANE_TILES)]
    ai = [ai_ref[:, j * LANES:(j + 1) * LANES] for j in range(S5_LANE_TILES)]

    def token(t, carry):
        cr, ci = carry
        nr, ni = [], []
        for j in range(S5_LANE_TILES):
            rows = pl.ds(t, S5_BLKS, stride=tl)
            xr, xi = _cmul_add(xr_scr[j, rows, :], xi_scr[j, rows, :], ar[j], ai[j], cr[j], ci[j])
            xr_scr[j, rows, :] = xr
            xi_scr[j, rows, :] = xi
            nr.append(xr)
            ni.append(xi)
        return tuple(nr), tuple(ni)

    init = (tuple(cr_scr[:, j * LANES:(j + 1) * LANES] for j in range(S5_LANE_TILES)),
            tuple(ci_scr[:, j * LANES:(j + 1) * LANES] for j in range(S5_LANE_TILES)))
    cr, ci = lax.fori_loop(0, tl, token, init, unroll=8)
    cr = jnp.concatenate(cr, axis=-1)
    ci = jnp.concatenate(ci, axis=-1)
    cr_scr[...] = cr
    ci_scr[...] = ci
    hr_ref[0] = cr
    hi_ref[0] = ci

    for k in range(S5_BLKS):
        xr = jnp.concatenate([xr_scr[j, k * tl:(k + 1) * tl, :] for j in range(S5_LANE_TILES)], axis=-1)
        xi = jnp.concatenate([xi_scr[j, k * tl:(k + 1) * tl, :] for j in range(S5_LANE_TILES)], axis=-1)
        u = u_ref[0, :, k * LANES:(k + 1) * LANES]
        y = (jnp.dot(xr.astype(BF16), wcr_ref[k], preferred_element_type=F32)
             - jnp.dot(xi.astype(BF16), wci_ref[k], preferred_element_type=F32)
             + d_ref[:, k * LANES:(k + 1) * LANES] * u)
        y_ref[0, :, k * LANES:(k + 1) * LANES] = jax.nn.gelu(y)


def s5_scan_long(z3, h0r, h0i, pwr, pwi, wbr, wbi, wcr, wci, d, tl):
    bsz, seq, _ = z3.shape
    assert seq % tl == 0 and tl % SUBLANES == 0
    seq_blk = pl.BlockSpec((1, tl, S5_WIDTH), lambda b, t: (b, t, 0))
    st_blk = pl.BlockSpec((1, S5_BLKS, S5_BLK_CH), lambda b, t: (b, 0, 0))
    lam_blk = pl.BlockSpec((S5_BLKS, S5_BLK_CH), lambda b, t: (0, 0))
    wb_blk = pl.BlockSpec((S5_BLKS, LANES, S5_BLK_CH), lambda b, t: (0, 0, 0))
    wc_blk = pl.BlockSpec((S5_BLKS, S5_BLK_CH, LANES), lambda b, t: (0, 0, 0))
    rows = pltpu.VMEM((S5_LANE_TILES, S5_BLKS * tl, LANES), F32)
    carry = pltpu.VMEM((S5_BLKS, S5_BLK_CH), F32)
    y, hr, hi = pl.pallas_call(
        functools.partial(_s5_long_kernel, tl=tl),
        grid=(bsz, seq // tl),
        in_specs=[seq_blk, st_blk, st_blk, lam_blk, lam_blk, wb_blk, wb_blk, wc_blk, wc_blk,
                  pl.BlockSpec((1, S5_WIDTH), lambda b, t: (0, 0))],
        out_specs=[seq_blk, st_blk, st_blk],
        out_shape=[jax.ShapeDtypeStruct((bsz, seq, S5_WIDTH), F32),
                   jax.ShapeDtypeStruct((bsz, S5_BLKS, S5_BLK_CH), F32),
                   jax.ShapeDtypeStruct((bsz, S5_BLKS, S5_BLK_CH), F32)],
        scratch_shapes=[rows, rows, carry, carry],
        compiler_params=_params("parallel", "arbitrary"),
        name="s5_scan_long",
    )(z3, h0r.reshape(bsz, S5_BLKS, S5_BLK_CH), h0i.reshape(bsz, S5_BLKS, S5_BLK_CH),
      pwr[0].reshape(S5_BLKS, S5_BLK_CH), pwi[0].reshape(S5_BLKS, S5_BLK_CH), wbr, wbi, wcr, wci, d)
    return y, hr.reshape(bsz, S5_GROUPS, S5_STATE), hi.reshape(bsz, S5_GROUPS, S5_STATE)


def _s5_block_weights(bbr_t, bbi_t, c_re, c_im):
    eye = jnp.eye(S5_BLK_GROUPS, dtype=F32)

    def wb(bt):
        b4 = bt.reshape(S5_GROUP, S5_BLKS, S5_BLK_GROUPS, S5_STATE)
        w = jnp.einsum("cbgp,hg->bhcgp", b4, eye)
        return w.reshape(S5_BLKS, LANES, S5_BLK_CH).astype(BF16)

    def wc(c):
        c4 = c.reshape(S5_BLKS, S5_BLK_GROUPS, S5_GROUP, S5_STATE)
        w = jnp.einsum("bgcp,hg->bhpgc", c4, eye)
        return w.reshape(S5_BLKS, S5_BLK_CH, LANES).astype(BF16)

    return wb(bbr_t), wb(bbi_t), wc(c_re), wc(c_im)


def _hg_lb_kernel(x_ref, o_ref):
    x = x_ref[...]
    e = jnp.exp(x - jnp.max(x, axis=0, keepdims=True))
    sm = e / jnp.sum(e, axis=0, keepdims=True)
    acc = sm[0:1]
    o_ref[0:1, :] = acc
    for l in range(1, x.shape[0]):
        acc = acc + sm[l:l + 1]
        o_ref[l:l + 1, :] = acc


def hg_lower_bounds(hg_lb):
    return pl.pallas_call(_hg_lb_kernel, out_shape=jax.ShapeDtypeStruct(hg_lb.shape, F32), name="hg_lb")(hg_lb)


def _cumsum_rows(x, n):
    row = lax.broadcasted_iota(jnp.int32, x.shape, 1)
    d = 1
    while d < n:
        x = x + jnp.where(row >= d, pltpu.roll(x, d, axis=1), 0.0)
        d *= 2
    return x


def _hgrn_kernel(q_ref, f_ref, i_ref, g_ref, s0_ref, lb_ref, nw_ref, o_ref, sf_ref, st_scr, *, nb, hh, chunk, n_sub):
    step = pl.program_id(2)

    def units(x):
        return jnp.concatenate([x[:, :, h * LANES:(h + 1) * LANES] for h in range(hh)], axis=0)

    lb = units(jnp.broadcast_to(lb_ref[...][None], (nb, 1, hh * LANES)))
    nw = nw_ref[...][None]

    @pl.when(step == 0)
    def _():
        for h in range(hh):
            for b in range(nb):
                st_scr[h * nb + b] = s0_ref[b, h].T

    trow = lax.broadcasted_iota(jnp.int32, (chunk, chunk), 0)
    tcol = lax.broadcasted_iota(jnp.int32, (chunk, chunk), 1)
    causal = (tcol <= trow)[None]
    st = st_scr[...]
    for c in range(n_sub):
        rows = slice(c * chunk, (c + 1) * chunk)
        q, f, v, g = (units(ref[:, rows, :]) for ref in (q_ref, f_ref, i_ref, g_ref))
        fg = lb + (1.0 - lb) * jax.nn.sigmoid(f)
        qh = jax.nn.silu(q)
        kh = 1.0 - fg
        bcum = _cumsum_rows(jnp.log(fg), chunk)
        btot = bcum[:, chunk - 1:chunk, :]
        q_in = (qh * jnp.exp(bcum)).astype(BF16)
        k_in = (kh * jnp.exp(-bcum)).astype(BF16)
        k_end = (kh * jnp.exp(btot - bcum)).astype(BF16)
        decay = jnp.exp(btot)
        vb = v.astype(BF16)
        att = jnp.einsum("utk,usk->uts", q_in, k_in, preferred_element_type=F32)
        att = jnp.where(causal, att, 0.0).astype(BF16)
        out = (jnp.einsum("utk,uvk->utv", q_in, st.astype(BF16), preferred_element_type=F32)
               + jnp.einsum("uts,usv->utv", att, vb, preferred_element_type=F32))
        st = st * decay + jnp.einsum("usv,usk->uvk", vb, k_end, preferred_element_type=F32)
        out = out * lax.rsqrt(jnp.mean(out * out, axis=-1, keepdims=True) + EPS) * nw
        out = (out * jax.nn.silu(g)).astype(o_ref.dtype)
        for h in range(hh):
            o_ref[:, rows, h * LANES:(h + 1) * LANES] = out[h * nb:(h + 1) * nb]
    st_scr[...] = st

    @pl.when(step == pl.num_programs(2) - 1)
    def _():
        for h in range(hh):
            for b in range(nb):
                sf_ref[b, h] = st[h * nb + b].T


def hgrn2(z3, s0, lb, norm_w, nb, hh):
    bsz, seq, _ = z3.shape
    chunk = min(HG_CHUNK, seq)
    n_sub = next(n for n in (3, 2, 1) if seq % (n * chunk) == 0)
    rows = n_sub * chunk
    assert bsz % nb == 0 and HG_HEADS % hh == 0
    wid = hh * LANES
    n_col = (HG_HEADS * HG_K) // wid

    def col(proj):
        return pl.BlockSpec((nb, rows, wid), lambda h, b, t, proj=proj: (b, t, proj * n_col + h))

    st_blk = pl.BlockSpec((nb, hh, HG_K, HG_V), lambda h, b, t: (b, h, 0, 0))
    out, sf = pl.pallas_call(
        functools.partial(_hgrn_kernel, nb=nb, hh=hh, chunk=chunk, n_sub=n_sub),
        grid=(HG_HEADS // hh, bsz // nb, seq // rows),
        in_specs=[col(1), col(2), col(3), col(4), st_blk,
                  pl.BlockSpec((1, wid), lambda h, b, t: (0, h)),
                  pl.BlockSpec((1, LANES), lambda h, b, t: (0, 0))],
        out_specs=[pl.BlockSpec((nb, rows, wid), lambda h, b, t: (b, t, h)), st_blk],
        out_shape=[jax.ShapeDtypeStruct((bsz, seq, HG_HEADS * HG_V), BF16),
                   jax.ShapeDtypeStruct((bsz, HG_HEADS, HG_K, HG_V), F32)],
        scratch_shapes=[pltpu.VMEM((hh * nb, HG_V, HG_K), F32)],
        compiler_params=_params("parallel", "parallel", "arbitrary"),
        name="hgrn2",
    )(z3, z3, z3, z3, s0, lb.reshape(1, HG_HEADS * HG_K), norm_w.reshape(1, HG_V))
    return out, sf


RW_MIXES = 6


def _norm_mix_kernel(h_ref, sh_ref, w_ref, mu_ref, *refs, tl):
    o_refs = refs[:RW_MIXES]
    last_ref, scr = refs[RW_MIXES:]
    x = h_ref[...]
    xn = x * lax.rsqrt(jnp.mean(x * x, axis=-1, keepdims=True) + EPS) * w_ref[...][None]

    @pl.when(pl.program_id(1) == 0)
    def _():
        scr[:, SUBLANES - 1:SUBLANES, :] = sh_ref[...]

    scr[:, SUBLANES:, :] = xn
    xx = scr[:, SUBLANES - 1:SUBLANES - 1 + tl, :] - xn
    for j, o_ref in enumerate(o_refs):
        o_ref[...] = (xn + xx * mu_ref[j:j + 1, :][None]).astype(o_ref.dtype)
    last = xn[:, tl - 1:tl, :]
    scr[:, SUBLANES - 1:SUBLANES, :] = last
    last_ref[...] = last


def norm_mix(h3, shift0, ln_w, mu, nb, tl):
    bsz, seq, d = h3.shape
    assert bsz % nb == 0 and seq % tl == 0 and tl % SUBLANES == 0
    blk = pl.BlockSpec((nb, tl, d), lambda b, t: (b, t, 0))
    row = pl.BlockSpec((nb, 1, d), lambda b, t: (b, 0, 0))
    outs = pl.pallas_call(
        functools.partial(_norm_mix_kernel, tl=tl),
        grid=(bsz // nb, seq // tl),
        in_specs=[blk, row, pl.BlockSpec((1, d), lambda b, t: (0, 0)), pl.BlockSpec((RW_MIXES, d), lambda b, t: (0, 0))],
        out_specs=[blk] * RW_MIXES + [row],
        out_shape=[jax.ShapeDtypeStruct((bsz, seq, d), BF16)] * RW_MIXES + [jax.ShapeDtypeStruct((bsz, 1, d), F32)],
        scratch_shapes=[pltpu.VMEM((nb, SUBLANES + tl, d), F32)],
        compiler_params=_params("parallel", "arbitrary"),
        name="norm_mix",
    )(h3, shift0.reshape(bsz, 1, d), ln_w.reshape(1, d), mu)
    return outs[:RW_MIXES], outs[RW_MIXES].reshape(bsz, d)


RW_PAIR = LANES // RW_HEAD
RW_DECAY_SCALE = math.exp(-0.5)
RW_SOLVE_BLOCK = 8


def _rwkv_kernel(r_ref, k_ref, v_ref, tw_ref, ta_ref, tg_ref, w2_ref, a2_ref, g2_ref, s0_ref,
                 w0_ref, a0_ref, kk_ref, ka_ref, rk_ref, lnw_ref, lnb_ref,
                 o_ref, sf_ref, s_scr, *, nb, hp, chunk):
    nu = hp * nb
    step = pl.program_id(2)
    lane = lax.broadcasted_iota(jnp.int32, (1, 1, LANES), 2)
    head1 = lane >= RW_HEAD

    def units(x):
        return jnp.concatenate([x[:, :, p * LANES:(p + 1) * LANES] for p in range(hp)], axis=0)

    def unit_rows(ref):
        return units(jnp.broadcast_to(ref[...][None], (nb, 1, hp * LANES)))

    w0, a0, k_k, k_a, r_k, ln_w, ln_b = (unit_rows(p) for p in
                                         (w0_ref, a0_ref, kk_ref, ka_ref, rk_ref, lnw_ref, lnb_ref))
    sq_row = lax.broadcasted_iota(jnp.int32, (LANES, LANES), 0) >= RW_HEAD
    sq_col = lax.broadcasted_iota(jnp.int32, (LANES, LANES), 1) >= RW_HEAD
    same_head = sq_row == sq_col
    ones_bd = same_head.astype(BF16)

    @pl.when(step == 0)
    def _():
        zero = jnp.zeros((nb, RW_HEAD, RW_HEAD), F32)
        for p in range(hp):
            top = jnp.concatenate([s0_ref[:, RW_PAIR * p], zero], axis=-1)
            bot = jnp.concatenate([zero, s0_ref[:, RW_PAIR * p + 1]], axis=-1)
            s_scr[p * nb:(p + 1) * nb] = jnp.concatenate([top, bot], axis=1)

    def bdot(spec, a, b):
        return jnp.einsum(spec, a.astype(BF16), b.astype(BF16), preferred_element_type=F32)

    def head_sum(x, two_pass=True):
        x2 = x.reshape(nu * chunk, LANES)
        hi = x2.astype(BF16)
        s = jnp.dot(hi, ones_bd, preferred_element_type=F32)
        if two_pass:
            lo = (x2 - hi.astype(F32)).astype(BF16)
            s = s + jnp.dot(lo, ones_bd, preferred_element_type=F32)
        return s.reshape(nu, chunk, LANES)

    def low_rank(t_ref, w_ref):
        t2 = t_ref[...].reshape(nb * chunk, t_ref.shape[-1])
        return units(jnp.dot(t2, w_ref[...], preferred_element_type=F32).reshape(nb, chunk, hp * LANES))

    srow = lax.broadcasted_iota(jnp.int32, (chunk, chunk), 0)
    scol = lax.broadcasted_iota(jnp.int32, (chunk, chunk), 1)
    tri = jnp.broadcast_to((scol <= srow).astype(BF16)[None], (nu, chunk, chunk))

    def tri_sum(x):
        hi = x.astype(BF16)
        lo = (x - hi.astype(F32)).astype(BF16)
        return (jnp.einsum("uts,usc->utc", tri, hi, preferred_element_type=F32)
                + jnp.einsum("uts,usc->utc", tri, lo, preferred_element_type=F32))

    def stack_heads(x):
        return jnp.concatenate([jnp.where(head1, 0.0, x), jnp.where(head1, x, 0.0)], axis=1).astype(BF16)

    r, k, v = (units(ref[...]) for ref in (r_ref, k_ref, v_ref))
    wl, al, g = low_rank(tw_ref, w2_ref), low_rank(ta_ref, a2_ref), low_rank(tg_ref, g2_ref)
    lw = (-RW_DECAY_SCALE) * jax.nn.sigmoid(w0 + wl)
    ag = jax.nn.sigmoid(a0 + al)
    kk = k * k_k
    kk = kk * jnp.minimum(lax.rsqrt(head_sum(kk * kk)), 1e12)
    k2 = k * (1.0 + (ag - 1.0) * k_a)
    cl = tri_sum(lw)
    e_pos = jnp.exp(cl)
    e_neg = jnp.exp(-cl)
    at = (-kk) * jnp.exp(cl - lw)
    bt = (kk * ag) * e_neg
    kt = k2 * e_neg
    rt = r * e_pos
    wc = e_pos[:, chunk - 1:chunk, :]
    trow = lax.broadcasted_iota(jnp.int32, (chunk, RW_PAIR * chunk), 0)
    tcol = lax.broadcasted_iota(jnp.int32, (chunk, RW_PAIR * chunk), 1)
    tcol = jnp.where(tcol >= chunk, tcol - chunk, tcol)
    strict = (tcol < trow)[None]
    incl = (tcol <= trow)[None]
    x2 = jnp.concatenate([at, rt], axis=1)
    pb = bdot("utc,usc->uts", x2, stack_heads(bt))
    pk = bdot("utc,usc->uts", x2, stack_heads(kt))
    lab = jnp.where(strict, pb[:, :chunk], 0.0)
    lak = jnp.where(strict, pk[:, :chunk], 0.0)
    arb = jnp.where(incl, pb[:, chunk:], 0.0)
    ark = jnp.where(incl, pk[:, chunk:], 0.0)
    v_bd = stack_heads(v)
    xa = at
    xv = bdot("uts,usc->utc", lak, v_bd)
    sub = min(RW_SOLVE_BLOCK, chunk)
    done_a, done_v = [], []
    for lo in range(0, chunk, sub):
        xa_i = xa[:, lo:lo + sub, :]
        xv_i = xv[:, lo:lo + sub, :]
        if lo:
            pad = jnp.zeros((nu, chunk - lo, LANES), F32)
            prev = jnp.concatenate([stack_heads(jnp.concatenate(done_a + [pad], axis=1)),
                                    stack_heads(jnp.concatenate(done_v + [pad], axis=1))], axis=-1)
            upd = bdot("uts,usc->utc", lab[:, lo:lo + sub, :], prev)
            xa_i = xa_i + upd[:, :, :LANES]
            xv_i = xv_i + upd[:, :, LANES:]
        l0 = lab[:, lo:lo + sub, lo:lo + sub]
        l1 = lab[:, lo:lo + sub, chunk + lo:chunk + lo + sub]
        for s in range(sub - 1):
            m = jnp.where(head1, l1[:, :, s:s + 1], l0[:, :, s:s + 1])
            xa_i = xa_i + m * xa_i[:, s:s + 1, :]
            xv_i = xv_i + m * xv_i[:, s:s + 1, :]
        done_a.append(xa_i)
        done_v.append(xv_i)
    ah = jnp.concatenate(done_a, axis=1)
    vh = jnp.concatenate(done_v, axis=1)
    both = bdot("uts,usc->utc", arb, jnp.concatenate([stack_heads(ah), stack_heads(vh)], axis=-1))
    rh = rt + both[:, :, :LANES]
    yh = both[:, :, LANES:] + bdot("uts,usc->utc", ark, v_bd)
    gp = jnp.where(same_head, bdot("utj,utk->ujk", ah, bt), 0.0)
    ht = jnp.where(same_head, bdot("utv,utk->uvk", jnp.concatenate([vh, v], axis=1),
                                   jnp.concatenate([bt, kt], axis=1)), 0.0)
    st = s_scr[...]
    y = bdot("utk,uvk->utv", rh, st) + yh
    st = (st + bdot("uvj,ujk->uvk", st, gp) + ht) * wc
    s_scr[...] = st
    inv_n = 1.0 / RW_HEAD
    yc = y - head_sum(y, two_pass=False) * inv_n
    var = head_sum(yc * yc, two_pass=False) * inv_n
    y = yc * lax.rsqrt(var + RW_GN_EPS) * ln_w + ln_b
    y = y + head_sum(r * k2 * r_k, two_pass=False) * v
    out = (y * g).astype(o_ref.dtype)
    for p in range(hp):
        o_ref[:, :, p * LANES:(p + 1) * LANES] = out[p * nb:(p + 1) * nb]

    @pl.when(step == pl.num_programs(2) - 1)
    def _():
        for p in range(hp):
            sf_ref[:, RW_PAIR * p] = st[p * nb:(p + 1) * nb, :RW_HEAD, :RW_HEAD]
            sf_ref[:, RW_PAIR * p + 1] = st[p * nb:(p + 1) * nb, RW_HEAD:, RW_HEAD:]


def _rwkv_chunk(seq):
    for c in (48, 32, 16, 8):
        if seq % c == 0:
            return c
    raise ValueError(seq)


def rwkv7(r, k, v, low, low_w, s0, w0, a0, k_k, k_a, r_k, ln_w, ln_b, nb, hp):
    bsz, seq, d = r.shape
    chunk = _rwkv_chunk(seq)
    heads = hp * RW_PAIR
    assert bsz % nb == 0 and RW_HEADS % heads == 0
    seq_blk = pl.BlockSpec((nb, chunk, hp * LANES), lambda h, b, t: (b, t, h))
    st_blk = pl.BlockSpec((nb, heads, RW_HEAD, RW_HEAD), lambda h, b, t: (b, h, 0, 0))
    vec = pl.BlockSpec((1, hp * LANES), lambda h, b, t: (0, h))
    low_blk = [pl.BlockSpec((nb, chunk, x.shape[-1]), lambda h, b, t: (b, t, 0)) for x in low]
    low_w_blk = [pl.BlockSpec((w.shape[0], hp * LANES), lambda h, b, t: (0, h)) for w in low_w]
    out, sf = pl.pallas_call(
        functools.partial(_rwkv_kernel, nb=nb, hp=hp, chunk=chunk),
        grid=(RW_HEADS // heads, bsz // nb, seq // chunk),
        in_specs=[seq_blk] * 3 + low_blk + low_w_blk + [st_blk] + [vec] * 7,
        out_specs=[seq_blk, st_blk],
        out_shape=[jax.ShapeDtypeStruct((bsz, seq, d), BF16),
                   jax.ShapeDtypeStruct((bsz, RW_HEADS, RW_HEAD, RW_HEAD), F32)],
        scratch_shapes=[pltpu.VMEM((hp * nb, LANES, LANES), F32)],
        compiler_params=_params("parallel", "parallel", "arbitrary"),
        name="rwkv7",
    )(r, k, v, *low, *low_w, s0, *(p.reshape(1, d) for p in (w0, a0, k_k, k_a, r_k, ln_w, ln_b)))
    return out, sf


def _ffn_in_kernel(x_ref, wa_ref, wv_ref, e_ref, cw_ref, cb_ref, o_ref, st_ref, scr, *, nb, seq, sb, sr):
    tn = wa_ref.shape[1]
    cw = cw_ref[...]
    cb = cb_ref[...][None]
    scr[:, SUBLANES - (CONV_W - 1):SUBLANES, :] = e_ref[...]
    for b0 in range(0, nb, sb):
        for r0 in range(0, seq, sr):
            lo = b0 * seq + r0
            x = x_ref[lo:lo + sb * sr, :]
            a = jnp.dot(x, wa_ref[...], preferred_element_type=F32).reshape(sb, sr, tn)
            v = jnp.dot(x, wv_ref[...], preferred_element_type=F32).reshape(sb, sr, tn)
            scr[b0:b0 + sb, SUBLANES + r0:SUBLANES + r0 + sr, :] = a
            c = cb + cw[CONV_W - 1:CONV_W][None] * a
            for j in range(CONV_W - 1):
                first = SUBLANES + r0 - (CONV_W - 1 - j)
                c = c + cw[j:j + 1][None] * scr[b0:b0 + sb, first:first + sr, :]
            o_ref[lo:lo + sb * sr, :] = (jax.nn.gelu(c) * v).reshape(sb * sr, tn).astype(o_ref.dtype)
    st_ref[...] = scr[:, SUBLANES + seq - (CONV_W - 1):SUBLANES + seq, :]


def ffn_in(xb, conv0, w_in, layer, conv_w, conv_b, bsz, seq, nb, tn, sub):
    t, d = xb.shape
    sb, sr = sub
    assert t == bsz * seq and bsz % nb == 0 and D_FF % tn == 0
    assert nb % sb == 0 and seq % sr == 0 and sr % SUBLANES == 0 and (sr == seq or nb == sb == 1)
    col = lambda i, j: (0, j)
    n_col = D_FF // tn
    out, st = pl.pallas_call(
        functools.partial(_ffn_in_kernel, nb=nb, seq=seq, sb=sb, sr=sr),
        grid=(bsz // nb, n_col),
        in_specs=[pl.BlockSpec((nb * seq, d), lambda i, j: (i, 0)),
                  pl.BlockSpec((None, d, tn), lambda i, j: (layer, 0, j)),
                  pl.BlockSpec((None, d, tn), lambda i, j: (layer, 0, j + n_col)),
                  pl.BlockSpec((nb, CONV_W - 1, tn), lambda i, j: (i, 0, j)),
                  pl.BlockSpec((CONV_W, tn), col), pl.BlockSpec((1, tn), col)],
        out_specs=[pl.BlockSpec((nb * seq, tn), lambda i, j: (i, j)),
                   pl.BlockSpec((nb, CONV_W - 1, tn), lambda i, j: (i, 0, j))],
        out_shape=[jax.ShapeDtypeStruct((t, D_FF), BF16),
                   jax.ShapeDtypeStruct((bsz, CONV_W - 1, D_FF), F32)],
        scratch_shapes=[pltpu.VMEM((nb, SUBLANES + seq, tn), F32)],
        compiler_params=_params("parallel", "parallel"),
        name="ffn_in",
    )(xb, w_in, w_in, conv0, conv_w, conv_b.reshape(1, D_FF))
    return out, st


FFN_DOWN_K_STEPS = 2


def _channel_mixer(h, conv0, layer, p, cfg, bsz, seq):
    (xb,) = rmsnorm(h, p["ln_ffn"][layer], (BF16,))
    gated, n_cv = ffn_in(xb, conv0, p["ffn_w_in"], layer, p["ffn_conv_w"][layer], p["ffn_conv_b"][layer],
                         bsz, seq, cfg["ffn_nb"], cfg["ffn_tn"], cfg["ffn_sub"])
    return matmul_residual_split(gated, p["ffn_w_down"], layer, h, FFN_DOWN_K_STEPS), n_cv


def _trunk(x3, s5r, s5i, hg, rw, sh, cv, p, cfg):
    bsz, seq, d = x3.shape
    seq += cfg["front"]
    t = bsz * seq

    if cfg["front"]:
        h, xb = embed_norm(x3, p["meta"], p["ln_mix"][0])
    else:
        h = x3.reshape(t, d)
        (xb,) = rmsnorm(h, p["ln_mix"][0], (BF16,))
    z = matmul([xb], [p["ev_w_in"]], F32).reshape(bsz, seq, EVEN_IN)
    if cfg["s5_tl"]:
        ys5, n_s5r, n_s5i = s5_scan_long(z, s5r[0], s5i[0], *p["s5"], tl=cfg["s5_tl"])
    else:
        ys5, n_s5r, n_s5i = s5_scan(z, s5r[0], s5i[0], *p["s5"], nb=cfg["s5_nb"])
    ys5 = ys5.reshape(t, S5_WIDTH)
    ya = matmul([ys5], [p["s5_w_glu"]], BF16, epilogue="glu", extra=ys5)
    yb, n_hg = hgrn2(z, hg[0], p["hg_lb"], p["hg_norm_w"], nb=cfg["hg_nb"], hh=cfg["hg_hh"])
    h = matmul([ya, yb.reshape(t, -1)], [(p["ev_w_out"], 0, 0), (p["ev_w_out"], 0, 1)], F32,
               epilogue="residual", extra=h)
    h, n_cv0 = _channel_mixer(h, cv[0], 0, p, cfg, bsz, seq)

    mixes, n_sh = norm_mix(h.reshape(bsz, seq, d), sh[0], p["ln_mix"][1], p["rw_mu"], *cfg["mix_blk"])
    xr, xw, xk, xv, xa, xg = (m.reshape(t, d) for m in mixes)
    r = matmul([xr], [p["rw_w_r"]], F32)
    k = matmul([xk], [p["rw_w_k"]], F32)
    v = matmul([xv], [p["rw_w_v"]], F32)
    low = (matmul([xw], [p["rw_w1"]], BF16, act="tanh"), matmul([xa], [p["rw_a1"]], BF16),
           matmul([xg], [p["rw_g1"]], BF16, act="sigmoid"))
    as3 = lambda a: a.reshape(bsz, seq, a.shape[-1])
    yo, n_rw = rwkv7(as3(r), as3(k), as3(v), [as3(x) for x in low], (p["rw_w2"], p["rw_a2"], p["rw_g2"]),
                     rw[0], *p["rw_vec"], nb=cfg["rw_nb"], hp=cfg["rw_hp"])
    h = matmul([yo.reshape(t, d)], [p["rw_w_o"]], F32, epilogue="residual", extra=h)
    h, n_cv1 = _channel_mixer(h, cv[1], 1, p, cfg, bsz, seq)

    (y,) = rmsnorm(h, p["ln_final"], (F32,), rows=(bsz, seq, cfg["front"]) if cfg["front"] else None)
    return (y.reshape(bsz, seq - cfg["front"], d), n_s5r[None], n_s5i[None], n_hg[None], n_rw[None], n_sh[None],
            jnp.stack([n_cv0, n_cv1]))


PROMPT_CFG = dict(front=N_META, s5_tl=344, s5_nb=None, hg_nb=4, hg_hh=8, mix_blk=(1, 344), rw_nb=4, rw_hp=16, ffn_nb=1, ffn_tn=512, ffn_sub=(1, 688))
SAMPLE_CFG = dict(front=0, s5_tl=None, s5_nb=32, hg_nb=4, hg_hh=8, mix_blk=(32, 8), rw_nb=16, rw_hp=4, ffn_nb=128, ffn_tn=512, ffn_sub=(32, 8))


def kernel(x_prompt, x_sample, state_s5_re, state_s5_im, state_hgrn, state_rwkv, state_shift, state_conv, meta_tokens, ln_mix, ln_ffn, ln_final, ev_w_in, ev_w_out, s5_lam_re, s5_lam_im, s5_log_step, s5_b_re, s5_b_im, s5_c_re, s5_c_im, s5_d, s5_w_glu, hg_lb, hg_norm_w, rw_mu, rw_w0, rw_w1, rw_w2, rw_a0, rw_a1, rw_a2, rw_g1, rw_g2, rw_k_k, rw_k_a, rw_r_k, rw_w_r, rw_w_k, rw_w_v, rw_w_o, rw_ln_w, rw_ln_b, ffn_w_in, ffn_conv_w, ffn_conv_b, ffn_w_down):
    bf = lambda w: w.astype(BF16)
    lb_all = hg_lower_bounds(hg_lb)
    pwr, pwi, bbr_t, bbi_t = s5_prep(s5_lam_re[0], s5_lam_im[0], s5_log_step[0], s5_b_re[0], s5_b_im[0])
    wbr, wbi, wcr, wci = _s5_block_weights(bbr_t, bbi_t, s5_c_re[0], s5_c_im[0])
    p = {
        "meta": meta_tokens, "ln_mix": ln_mix, "ln_ffn": ln_ffn, "ln_final": ln_final,
        "ev_w_in": ev_w_in[0],
        "ev_w_out": ev_w_out,
        "s5": (pwr, pwi, wbr, wbi, wcr, wci, s5_d[0].reshape(1, S5_WIDTH)),
        "s5_w_glu": s5_w_glu[0],
        "hg_lb": lb_all[0], "hg_norm_w": hg_norm_w[0],
        "rw_mu": rw_mu[0],
        "rw_w1": rw_w1[0], "rw_w2": bf(rw_w2[0]), "rw_a1": rw_a1[0], "rw_a2": bf(rw_a2[0]),
        "rw_g1": rw_g1[0], "rw_g2": bf(rw_g2[0]),
        "rw_w_r": rw_w_r[0], "rw_w_k": rw_w_k[0], "rw_w_v": rw_w_v[0], "rw_w_o": rw_w_o[0],
        "rw_vec": (rw_w0[0], rw_a0[0], rw_k_k[0], rw_k_a[0], rw_r_k[0].reshape(D_MODEL), rw_ln_w[0], rw_ln_b[0]),
        "ffn_w_in": bf(ffn_w_in), "ffn_conv_w": ffn_conv_w, "ffn_conv_b": ffn_conv_b,
        "ffn_w_down": bf(ffn_w_down),
    }

    bsz = x_prompt.shape[0]
    zeros = lambda *s: jnp.zeros(s, F32)
    outs_p = _trunk(x_prompt,
                    zeros(1, bsz, S5_GROUPS, S5_STATE), zeros(1, bsz, S5_GROUPS, S5_STATE),
                    zeros(1, bsz, HG_HEADS, HG_K, HG_V), zeros(1, bsz, RW_HEADS, RW_HEAD, RW_HEAD),
                    zeros(1, bsz, D_MODEL), zeros(2, bsz, CONV_W - 1, D_FF), p, PROMPT_CFG)
    outs_s = _trunk(x_sample, state_s5_re, state_s5_im, state_hgrn, state_rwkv, state_shift, state_conv,
                    p, SAMPLE_CFG)
    return tuple(outs_p[:1]) + tuple(outs_s[:1]) + tuple(outs_p[1:]) + tuple(outs_s[1:])
```

```python
import functools
import math

import jax
import jax.numpy as jnp
from jax import lax
from jax.experimental import pallas as pl
from jax.experimental.pallas import tpu as pltpu

F32 = jnp.float32
BF16 = jnp.bfloat16

D_MODEL = 2048
N_META = 16
EPS = 1e-6
S5_WIDTH = 1024
S5_GROUP = 16
S5_GROUPS = 64
S5_STATE = 64
S5_CH = S5_GROUPS * S5_STATE
HG_HEADS = 8
HG_K = 128
HG_V = 128
HG_CHUNK = 16
EVEN_IN = 5120
RW_HEAD = 64
RW_HEADS = 32
RW_GN_EPS = 64e-5
D_FF = 5632
CONV_W = 3

LANES = 128
SUBLANES = 8
VMEM_LIMIT = 56 * 1024 * 1024


def _params(*sem):
    return pltpu.CompilerParams(dimension_semantics=sem, vmem_limit_bytes=VMEM_LIMIT)


def _row_tile(t, cap=1024):
    best = None
    for d in range(16, min(t, cap) + 1, 16):
        if t % d == 0:
            best = d
    assert best is not None, t
    return best


def _rms_kernel(x_ref, w_ref, *o_refs):
    x = x_ref[...]
    y = x * lax.rsqrt(jnp.mean(x * x, axis=-1, keepdims=True) + EPS) * w_ref[...]
    for o_ref in o_refs:
        o_ref[...] = y.astype(o_ref.dtype)


def rmsnorm(x, w, dtypes, rows=None):
    t, d = x.shape
    if rows is None:
        tm = _row_tile(t)
        n_out = t
        grid = (t // tm,)
        in_spec = pl.BlockSpec((tm, d), lambda i: (i, 0))
        out_spec = in_spec
        vec = pl.BlockSpec((1, d), lambda i: (0, 0))
    else:
        bsz, seq, front = rows
        keep = seq - front
        assert t == bsz * seq and front % 16 == 0
        tm = _row_tile(keep)
        per = keep // tm
        n_out = bsz * keep
        grid = (bsz, per)
        in_spec = pl.BlockSpec((pl.Element(tm), pl.Element(d)),
                               lambda b, i: (pl.multiple_of(b * seq + front + i * tm, 16), 0))
        out_spec = pl.BlockSpec((tm, d), lambda b, i: (b * per + i, 0))
        vec = pl.BlockSpec((1, d), lambda b, i: (0, 0))
    outs = pl.pallas_call(
        _rms_kernel,
        grid=grid,
        in_specs=[in_spec, vec],
        out_specs=[out_spec for _ in dtypes],
        out_shape=[jax.ShapeDtypeStruct((n_out, d), dt) for dt in dtypes],
        compiler_params=_params(*(["parallel"] * len(grid))),
        name="rmsnorm",
    )(x, w.reshape(1, d))
    return outs


def _embed_norm_kernel(x_ref, m_ref, w_ref, h_ref, xb_ref, *, front):
    x = x_ref[...]
    first = pl.program_id(1) == 0
    body = jnp.where(first, pltpu.roll(x, front, axis=0), x)
    head = jnp.where(first, m_ref[...], x[:front])
    rows = jnp.concatenate([head, body[front:]], axis=0)
    h_ref[...] = rows
    y = rows * lax.rsqrt(jnp.mean(rows * rows, axis=-1, keepdims=True) + EPS) * w_ref[...]
    xb_ref[...] = y.astype(xb_ref.dtype)


def embed_norm(x3, meta, w):
    bsz, seq, d = x3.shape
    front = meta.shape[0]
    total = seq + front
    tm = _row_tile(total)
    per = total // tm
    assert front % 16 == 0 and tm > front
    blk = pl.BlockSpec((tm, d), lambda b, j: (b * per + j, 0))
    return pl.pallas_call(
        functools.partial(_embed_norm_kernel, front=front),
        grid=(bsz, per),
        in_specs=[pl.BlockSpec((pl.Element(tm), pl.Element(d)),
                               lambda b, j: (pl.multiple_of(b * seq + jnp.maximum(j * tm - front, 0), 16), 0)),
                  pl.BlockSpec((front, d), lambda b, j: (0, 0)),
                  pl.BlockSpec((1, d), lambda b, j: (0, 0))],
        out_specs=[blk, blk],
        out_shape=[jax.ShapeDtypeStruct((bsz * total, d), F32), jax.ShapeDtypeStruct((bsz * total, d), BF16)],
        compiler_params=_params("parallel", "parallel"),
        name="embed_norm",
    )(x3.reshape(bsz * seq, d), meta, w.reshape(1, d))


def _act(x, act):
    if act == "tanh":
        return jnp.tanh(x)
    if act == "sigmoid":
        return jax.nn.sigmoid(x)
    assert act is None
    return x


def _mm_kernel(*refs, n_a, act, epilogue, cast_w):
    a_refs = refs[:n_a]
    w_refs = refs[n_a:2 * n_a]
    rest = refs[2 * n_a:]
    if cast_w:
        rest, wb_refs = rest[:-n_a], rest[-n_a:]

        @pl.when(pl.program_id(1) == 0)
        def _():
            for w_ref, wb_ref in zip(w_refs, wb_refs):
                wb_ref[...] = w_ref[...].astype(BF16)

        w_refs = wb_refs
    o_ref = rest[-1]
    acc = jnp.dot(a_refs[0][...].astype(BF16), w_refs[0][...], preferred_element_type=F32)
    for a_ref, w_ref in zip(a_refs[1:], w_refs[1:]):
        acc = acc + jnp.dot(a_ref[...].astype(BF16), w_ref[...], preferred_element_type=F32)
    acc = _act(acc, act)
    if epilogue == "residual":
        acc = rest[0][...] + acc
    elif epilogue == "glu":
        acc = rest[0][...] * jax.nn.sigmoid(acc)
    o_ref[...] = acc.astype(o_ref.dtype)


MM_VMEM_BUDGET = 40 * 1024 * 1024
MM_CAST_VMEM_BUDGET = 48 * 1024 * 1024
MM_MAX_A_BLOCK = 6 * 1024 * 1024
MXU_WIDTH = 256


def _mm_tiles(t, a_row_bytes, w_col_bytes, n, out_bytes, has_extra, budget=MM_VMEM_BUDGET, tall=False):
    rows = [d for d in range(16, t + 1, 16) if t % d == 0 and d * a_row_bytes <= MM_MAX_A_BLOCK]
    cols = [d for d in range(LANES, n + 1, LANES) if n % d == 0] or [n]
    best, best_score = None, -1.0
    for tm in rows:
        for tn in cols:
            est = 2 * tm * a_row_bytes + w_col_bytes * tn + tm * tn * (2 * out_bytes + 4 + (8 if has_extra else 0))
            if est > budget:
                continue
            score = tm * tn * (1.0 if tn % MXU_WIDTH == 0 else 0.8)
            if score > best_score or (tall and score == best_score):
                best, best_score = (tm, tn), score
    assert best is not None, (t, a_row_bytes, w_col_bytes, n)
    return best


def matmul(a_list, w_list, out_dtype, act=None, epilogue=None, extra=None):
    t = a_list[0].shape[0]
    w0 = w_list[0][0] if isinstance(w_list[0], tuple) else w_list[0]
    n = w0.shape[-1]
    cast_w = w0.dtype == F32
    k_rows = sum(a.shape[1] for a in a_list)
    a_row_bytes = sum(a.shape[1] * a.dtype.itemsize for a in a_list)
    w_col_bytes = k_rows * (2 * 4 + 2 if cast_w else 2 * 2)
    tm, tn = _mm_tiles(t, a_row_bytes, w_col_bytes, n, jnp.dtype(out_dtype).itemsize, epilogue is not None,
                       budget=MM_CAST_VMEM_BUDGET if cast_w else MM_VMEM_BUDGET, tall=cast_w)
    rc = (lambda f: lambda j, i: f(i, j)) if cast_w else (lambda f: f)
    in_specs = [pl.BlockSpec((tm, a.shape[1]), rc(lambda i, j: (i, 0))) for a in a_list]
    args = list(a_list)
    for a, w in zip(a_list, w_list):
        if isinstance(w, tuple):
            w, layer, kblk = w
            in_specs.append(pl.BlockSpec((None, a.shape[1], tn),
                                         rc(lambda i, j, layer=layer, kblk=kblk: (layer, kblk, j))))
        else:
            in_specs.append(pl.BlockSpec((w.shape[0], tn), rc(lambda i, j: (0, j))))
        args.append(w)
    if epilogue is not None:
        in_specs.append(pl.BlockSpec((tm, tn), rc(lambda i, j: (i, j))))
        args.append(extra)
    return pl.pallas_call(
        functools.partial(_mm_kernel, n_a=len(a_list), act=act, epilogue=epilogue, cast_w=cast_w),
        grid=(n // tn, t // tm) if cast_w else (t // tm, n // tn),
        in_specs=in_specs,
        out_specs=pl.BlockSpec((tm, tn), rc(lambda i, j: (i, j))),
        out_shape=jax.ShapeDtypeStruct((t, n), out_dtype),
        scratch_shapes=[pltpu.VMEM((a.shape[1], tn), BF16) for a in a_list] if cast_w else [],
        compiler_params=_params("parallel", "arbitrary" if cast_w else "parallel"),
        name="matmul",
    )(*args)


def _mm_resid_split_kernel(a_ref, w_ref, res_ref, o_ref):
    part = jnp.dot(a_ref[...], w_ref[...], preferred_element_type=F32)

    @pl.when(pl.program_id(2) == 0)
    def _():
        o_ref[...] = res_ref[...] + part

    @pl.when(pl.program_id(2) > 0)
    def _():
        o_ref[...] += part


def matmul_residual_split(a, w, layer, res, k_steps):
    t, k = a.shape
    n = w.shape[2]
    tk = k // k_steps
    assert k % k_steps == 0 and tk % LANES == 0
    tm, tn = _mm_tiles(t, 2 * tk, 4 * tk, n, 4, True)
    return pl.pallas_call(
        _mm_resid_split_kernel,
        grid=(t // tm, n // tn, k_steps),
        in_specs=[pl.BlockSpec((tm, tk), lambda i, j, s: (i, s)),
                  pl.BlockSpec((None, tk, tn), lambda i, j, s: (layer, s, j)),
                  pl.BlockSpec((tm, tn), lambda i, j, s: (i, j))],
        out_specs=pl.BlockSpec((tm, tn), lambda i, j, s: (i, j)),
        out_shape=jax.ShapeDtypeStruct((t, n), F32),
        compiler_params=_params("parallel", "parallel", "arbitrary"),
        name="matmul_split",
    )(a, w, res)


def _s5_prep_kernel(lr_ref, li_ref, ls_ref, brt_ref, bit_ref, pwr_ref, pwi_ref, bbr_ref, bbi_ref):
    lr = jnp.minimum(lr_ref[...], -1e-4)
    li = li_ref[...]
    dt = jnp.exp(ls_ref[...])
    n = lax.broadcasted_iota(jnp.int32, (SUBLANES, S5_CH), 0).astype(F32) + 1.0
    mag = jnp.exp(n * (lr * dt))
    ang = n * (li * dt)
    pwr = mag * jnp.cos(ang)
    pwi = mag * jnp.sin(ang)
    pwr_ref[...] = pwr
    pwi_ref[...] = pwi
    ar = pwr[0:1]
    ai = pwi[0:1]
    den = lr * lr + li * li
    zr = ((ar - 1.0) * lr + ai * li) / den
    zi = (ai * lr - (ar - 1.0) * li) / den
    br = brt_ref[...]
    bi = bit_ref[...]
    bbr_ref[...] = zr * br - zi * bi
    bbi_ref[...] = zr * bi + zi * br


def s5_prep(lam_re, lam_im, log_step, b_re, b_im):
    lr = lam_re.reshape(1, S5_CH)
    li = lam_im.reshape(1, S5_CH)
    ls = jnp.broadcast_to(log_step[:, None], (S5_GROUPS, S5_STATE)).reshape(1, S5_CH)
    brt = b_re.reshape(S5_CH, S5_GROUP).T
    bit = b_im.reshape(S5_CH, S5_GROUP).T
    return pl.pallas_call(
        _s5_prep_kernel,
        out_shape=[jax.ShapeDtypeStruct((SUBLANES, S5_CH), F32)] * 2
        + [jax.ShapeDtypeStruct((S5_GROUP, S5_CH), F32)] * 2,
        name="s5_prep",
    )(lr, li, ls, brt, bit)


S5_BLK_GROUPS = LANES // S5_GROUP
S5_BLKS = S5_WIDTH // LANES
S5_BLK_CH = S5_BLK_GROUPS * S5_STATE


def _cmul_add(xr, xi, mr, mi, sr, si):
    return xr + mr * sr - mi * si, xi + mr * si + mi * sr


def _s5_kernel(u_ref, h0r_ref, h0i_ref, pwr_ref, pwi_ref, wbr_ref, wbi_ref, wcr_ref, wci_ref, d_ref,
               y_ref, hr_ref, hi_ref, xr_scr, xi_scr, *, nb, seq):
    u2 = u_ref[...].reshape(nb * seq, LANES)
    ub = u2.astype(BF16)
    xr_scr[...] = jnp.dot(ub, wbr_ref[0], preferred_element_type=F32).reshape(nb, seq, S5_BLK_CH)
    xi_scr[...] = jnp.dot(ub, wbi_ref[0], preferred_element_type=F32).reshape(nb, seq, S5_BLK_CH)

    pwr = pwr_ref[...]
    pwi = pwi_ref[...]
    row = lax.broadcasted_iota(jnp.int32, (SUBLANES, S5_BLK_CH), 0)
    steps = []
    for d in (1, 2, 4):
        keep = row >= d
        steps.append((d, jnp.where(keep, pwr[d - 1:d], 0.0)[None], jnp.where(keep, pwi[d - 1:d], 0.0)[None]))
    pr = pwr[None]
    pi = pwi[None]

    def tile(i, carry):
        cr, ci = carry
        o = pl.multiple_of(i * SUBLANES, SUBLANES)
        xr = xr_scr[:, pl.ds(o, SUBLANES), :]
        xi = xi_scr[:, pl.ds(o, SUBLANES), :]
        for d, mr, mi in steps:
            sr = pltpu.roll(xr, d, axis=1)
            si = pltpu.roll(xi, d, axis=1)
            xr, xi = _cmul_add(xr, xi, mr, mi, sr, si)
        xr, xi = _cmul_add(xr, xi, pr, pi, cr, ci)
        xr_scr[:, pl.ds(o, SUBLANES), :] = xr
        xi_scr[:, pl.ds(o, SUBLANES), :] = xi
        return xr[:, SUBLANES - 1:SUBLANES, :], xi[:, SUBLANES - 1:SUBLANES, :]

    hr, hi = lax.fori_loop(0, seq // SUBLANES, tile, (h0r_ref[...], h0i_ref[...]))
    hr_ref[...] = hr
    hi_ref[...] = hi

    xr = xr_scr[...].reshape(nb * seq, S5_BLK_CH).astype(BF16)
    xi = xi_scr[...].reshape(nb * seq, S5_BLK_CH).astype(BF16)
    y = (jnp.dot(xr, wcr_ref[0], preferred_element_type=F32)
         - jnp.dot(xi, wci_ref[0], preferred_element_type=F32)
         + d_ref[...] * u2)
    y_ref[...] = jax.nn.gelu(y).reshape(nb, seq, LANES)


def s5_scan(z3, h0r, h0i, pwr, pwi, wbr, wbi, wcr, wci, d, nb):
    bsz, seq, _ = z3.shape
    assert seq % SUBLANES == 0 and bsz % nb == 0
    seq_blk = pl.BlockSpec((nb, seq, LANES), lambda b, k: (b, 0, k))
    st_blk = pl.BlockSpec((nb, 1, S5_BLK_CH), lambda b, k: (b, 0, k))
    pw_blk = pl.BlockSpec((SUBLANES, S5_BLK_CH), lambda b, k: (0, k))
    wb_blk = pl.BlockSpec((1, LANES, S5_BLK_CH), lambda b, k: (k, 0, 0))
    wc_blk = pl.BlockSpec((1, S5_BLK_CH, LANES), lambda b, k: (k, 0, 0))
    y, hr, hi = pl.pallas_call(
        functools.partial(_s5_kernel, nb=nb, seq=seq),
        grid=(bsz // nb, S5_BLKS),
        in_specs=[seq_blk, st_blk, st_blk, pw_blk, pw_blk, wb_blk, wb_blk, wc_blk, wc_blk,
                  pl.BlockSpec((1, LANES), lambda b, k: (0, k))],
        out_specs=[seq_blk, st_blk, st_blk],
        out_shape=[jax.ShapeDtypeStruct((bsz, seq, S5_WIDTH), F32),
                   jax.ShapeDtypeStruct((bsz, 1, S5_CH), F32),
                   jax.ShapeDtypeStruct((bsz, 1, S5_CH), F32)],
        scratch_shapes=[pltpu.VMEM((nb, seq, S5_BLK_CH), F32), pltpu.VMEM((nb, seq, S5_BLK_CH), F32)],
        compiler_params=_params("parallel", "parallel"),
        name="s5_scan",
    )(z3, h0r.reshape(bsz, 1, S5_CH), h0i.reshape(bsz, 1, S5_CH), pwr, pwi, wbr, wbi, wcr, wci, d)
    return y, hr.reshape(bsz, S5_GROUPS, S5_STATE), hi.reshape(bsz, S5_GROUPS, S5_STATE)


S5_LANE_TILES = S5_BLK_CH // LANES


def _s5_long_kernel(u_ref, h0r_ref, h0i_ref, ar_ref, ai_ref, wbr_ref, wbi_ref, wcr_ref, wci_ref, d_ref,
                    y_ref, hr_ref, hi_ref, xr_scr, xi_scr, cr_scr, ci_scr, *, tl):
    step = pl.program_id(1)

    @pl.when(step == 0)
    def _():
        cr_scr[...] = h0r_ref[0]
        ci_scr[...] = h0i_ref[0]

    for k in range(S5_BLKS):
        ub = u_ref[0, :, k * LANES:(k + 1) * LANES].astype(BF16)
        bur = jnp.dot(ub, wbr_ref[k], preferred_element_type=F32)
        bui = jnp.dot(ub, wbi_ref[k], preferred_element_type=F32)
        for j in range(S5_LANE_TILES):
            xr_scr[j, k * tl:(k + 1) * tl, :] = bur[:, j * LANES:(j + 1) * LANES]
            xi_scr[j, k * tl:(k + 1) * tl, :] = bui[:, j * LANES:(j + 1) * LANES]

    ar = [ar_ref[:, j * LANES:(j + 1) * LANES] for j in range(S5_LANE_TILES)]
    ai = [ai_ref[:, j * LANES:(j + 1) * LANES] for j in range(S5_LANE_TILES)]

    def token(t, carry):
        cr, ci = carry
        nr, ni = [], []
        for j in range(S5_LANE_TILES):
            rows = pl.ds(t, S5_BLKS, stride=tl)
            xr, xi = _cmul_add(xr_scr[j, rows, :], xi_scr[j, rows, :], ar[j], ai[j], cr[j], ci[j])
            xr_scr[j, rows, :] = xr
            xi_scr[j, rows, :] = xi
            nr.append(xr)
            ni.append(xi)
        return tuple(nr), tuple(ni)

    init = (tuple(cr_scr[:, j * LANES:(j + 1) * LANES] for j in range(S5_LANE_TILES)),
            tuple(ci_scr[:, j * LANES:(j + 1) * LANES] for j in range(S5_LANE_TILES)))
    cr, ci = lax.fori_loop(0, tl, token, init, unroll=8)
    cr = jnp.concatenate(cr, axis=-1)
    ci = jnp.concatenate(ci, axis=-1)
    cr_scr[...] = cr
    ci_scr[...] = ci
    hr_ref[0] = cr
    hi_ref[0] = ci

    for k in range(S5_BLKS):
        xr = jnp.concatenate([xr_scr[j, k * tl:(k + 1) * tl, :] for j in range(S5_LANE_TILES)], axis=-1)
        xi = jnp.concatenate([xi_scr[j, k * tl:(k + 1) * tl, :] for j in range(S5_LANE_TILES)], axis=-1)
        u = u_ref[0, :, k * LANES:(k + 1) * LANES]
        y = (jnp.dot(xr.astype(BF16), wcr_ref[k], preferred_element_type=F32)
             - jnp.dot(xi.astype(BF16), wci_ref[k], preferred_element_type=F32)
             + d_ref[:, k * LANES:(k + 1) * LANES] * u)
        y_ref[0, :, k * LANES:(k + 1) * LANES] = jax.nn.gelu(y)


def s5_scan_long(z3, h0r, h0i, pwr, pwi, wbr, wbi, wcr, wci, d, tl):
    bsz, seq, _ = z3.shape
    assert seq % tl == 0 and tl % SUBLANES == 0
    seq_blk = pl.BlockSpec((1, tl, S5_WIDTH), lambda b, t: (b, t, 0))
    st_blk = pl.BlockSpec((1, S5_BLKS, S5_BLK_CH), lambda b, t: (b, 0, 0))
    lam_blk = pl.BlockSpec((S5_BLKS, S5_BLK_CH), lambda b, t: (0, 0))
    wb_blk = pl.BlockSpec((S5_BLKS, LANES, S5_BLK_CH), lambda b, t: (0, 0, 0))
    wc_blk = pl.BlockSpec((S5_BLKS, S5_BLK_CH, LANES), lambda b, t: (0, 0, 0))
    rows = pltpu.VMEM((S5_LANE_TILES, S5_BLKS * tl, LANES), F32)
    carry = pltpu.VMEM((S5_BLKS, S5_BLK_CH), F32)
    y, hr, hi = pl.pallas_call(
        functools.partial(_s5_long_kernel, tl=tl),
        grid=(bsz, seq // tl),
        in_specs=[seq_blk, st_blk, st_blk, lam_blk, lam_blk, wb_blk, wb_blk, wc_blk, wc_blk,
                  pl.BlockSpec((1, S5_WIDTH), lambda b, t: (0, 0))],
        out_specs=[seq_blk, st_blk, st_blk],
        out_shape=[jax.ShapeDtypeStruct((bsz, seq, S5_WIDTH), F32),
                   jax.ShapeDtypeStruct((bsz, S5_BLKS, S5_BLK_CH), F32),
                   jax.ShapeDtypeStruct((bsz, S5_BLKS, S5_BLK_CH), F32)],
        scratch_shapes=[rows, rows, carry, carry],
        compiler_params=_params("parallel", "arbitrary"),
        name="s5_scan_long",
    )(z3, h0r.reshape(bsz, S5_BLKS, S5_BLK_CH), h0i.reshape(bsz, S5_BLKS, S5_BLK_CH),
      pwr[0].reshape(S5_BLKS, S5_BLK_CH), pwi[0].reshape(S5_BLKS, S5_BLK_CH), wbr, wbi, wcr, wci, d)
    return y, hr.reshape(bsz, S5_GROUPS, S5_STATE), hi.reshape(bsz, S5_GROUPS, S5_STATE)


def _s5_block_weights(bbr_t, bbi_t, c_re, c_im):
    eye = jnp.eye(S5_BLK_GROUPS, dtype=F32)

    def wb(bt):
        b4 = bt.reshape(S5_GROUP, S5_BLKS, S5_BLK_GROUPS, S5_STATE)
        w = jnp.einsum("cbgp,hg->bhcgp", b4, eye)
        return w.reshape(S5_BLKS, LANES, S5_BLK_CH).astype(BF16)

    def wc(c):
        c4 = c.reshape(S5_BLKS, S5_BLK_GROUPS, S5_GROUP, S5_STATE)
        w = jnp.einsum("bgcp,hg->bhpgc", c4, eye)
        return w.reshape(S5_BLKS, S5_BLK_CH, LANES).astype(BF16)

    return wb(bbr_t), wb(bbi_t), wc(c_re), wc(c_im)


def _hg_lb_kernel(x_ref, o_ref):
    x = x_ref[...]
    e = jnp.exp(x - jnp.max(x, axis=0, keepdims=True))
    sm = e / jnp.sum(e, axis=0, keepdims=True)
    acc = sm[0:1]
    o_ref[0:1, :] = acc
    for l in range(1, x.shape[0]):
        acc = acc + sm[l:l + 1]
        o_ref[l:l + 1, :] = acc


def hg_lower_bounds(hg_lb):
    return pl.pallas_call(_hg_lb_kernel, out_shape=jax.ShapeDtypeStruct(hg_lb.shape, F32), name="hg_lb")(hg_lb)


def _cumsum_rows(x, n):
    row = lax.broadcasted_iota(jnp.int32, x.shape, 1)
    d = 1
    while d < n:
        x = x + jnp.where(row >= d, pltpu.roll(x, d, axis=1), 0.0)
        d *= 2
    return x


def _hgrn_kernel(q_ref, f_ref, i_ref, g_ref, s0_ref, lb_ref, nw_ref, o_ref, sf_ref, st_scr, *, nb, hh, chunk, n_sub):
    step = pl.program_id(2)

    def units(x):
        return jnp.concatenate([x[:, :, h * LANES:(h + 1) * LANES] for h in range(hh)], axis=0)

    lb = units(jnp.broadcast_to(lb_ref[...][None], (nb, 1, hh * LANES)))
    nw = nw_ref[...][None]

    @pl.when(step == 0)
    def _():
        for h in range(hh):
            for b in range(nb):
                st_scr[h * nb + b] = s0_ref[b, h].T

    trow = lax.broadcasted_iota(jnp.int32, (chunk, chunk), 0)
    tcol = lax.broadcasted_iota(jnp.int32, (chunk, chunk), 1)
    causal = (tcol <= trow)[None]
    st = st_scr[...]
    for c in range(n_sub):
        rows = slice(c * chunk, (c + 1) * chunk)
        q, f, v, g = (units(ref[:, rows, :]) for ref in (q_ref, f_ref, i_ref, g_ref))
        fg = lb + (1.0 - lb) * jax.nn.sigmoid(f)
        qh = jax.nn.silu(q)
        kh = 1.0 - fg
        bcum = _cumsum_rows(jnp.log(fg), chunk)
        btot = bcum[:, chunk - 1:chunk, :]
        q_in = (qh * jnp.exp(bcum)).astype(BF16)
        k_in = (kh * jnp.exp(-bcum)).astype(BF16)
        k_end = (kh * jnp.exp(btot - bcum)).astype(BF16)
        decay = jnp.exp(btot)
        vb = v.astype(BF16)
        att = jnp.einsum("utk,usk->uts", q_in, k_in, preferred_element_type=F32)
        att = jnp.where(causal, att, 0.0).astype(BF16)
        out = (jnp.einsum("utk,uvk->utv", q_in, st.astype(BF16), preferred_element_type=F32)
               + jnp.einsum("uts,usv->utv", att, vb, preferred_element_type=F32))
        st = st * decay + jnp.einsum("usv,usk->uvk", vb, k_end, preferred_element_type=F32)
        out = out * lax.rsqrt(jnp.mean(out * out, axis=-1, keepdims=True) + EPS) * nw
        out = (out * jax.nn.silu(g)).astype(o_ref.dtype)
        for h in range(hh):
            o_ref[:, rows, h * LANES:(h + 1) * LANES] = out[h * nb:(h + 1) * nb]
    st_scr[...] = st

    @pl.when(step == pl.num_programs(2) - 1)
    def _():
        for h in range(hh):
            for b in range(nb):
                sf_ref[b, h] = st[h * nb + b].T


def hgrn2(z3, s0, lb, norm_w, nb, hh):
    bsz, seq, _ = z3.shape
    chunk = min(HG_CHUNK, seq)
    n_sub = next(n for n in (3, 2, 1) if seq % (n * chunk) == 0)
    rows = n_sub * chunk
    assert bsz % nb == 0 and HG_HEADS % hh == 0
    wid = hh * LANES
    n_col = (HG_HEADS * HG_K) // wid

    def col(proj):
        return pl.BlockSpec((nb, rows, wid), lambda h, b, t, proj=proj: (b, t, proj * n_col + h))

    st_blk = pl.BlockSpec((nb, hh, HG_K, HG_V), lambda h, b, t: (b, h, 0, 0))
    out, sf = pl.pallas_call(
        functools.partial(_hgrn_kernel, nb=nb, hh=hh, chunk=chunk, n_sub=n_sub),
        grid=(HG_HEADS // hh, bsz // nb, seq // rows),
        in_specs=[col(1), col(2), col(3), col(4), st_blk,
                  pl.BlockSpec((1, wid), lambda h, b, t: (0, h)),
                  pl.BlockSpec((1, LANES), lambda h, b, t: (0, 0))],
        out_specs=[pl.BlockSpec((nb, rows, wid), lambda h, b, t: (b, t, h)), st_blk],
        out_shape=[jax.ShapeDtypeStruct((bsz, seq, HG_HEADS * HG_V), BF16),
                   jax.ShapeDtypeStruct((bsz, HG_HEADS, HG_K, HG_V), F32)],
        scratch_shapes=[pltpu.VMEM((hh * nb, HG_V, HG_K), F32)],
        compiler_params=_params("parallel", "parallel", "arbitrary"),
        name="hgrn2",
    )(z3, z3, z3, z3, s0, lb.reshape(1, HG_HEADS * HG_K), norm_w.reshape(1, HG_V))
    return out, sf


RW_MIXES = 6


def _norm_mix_kernel(h_ref, sh_ref, w_ref, mu_ref, *refs, tl):
    o_refs = refs[:RW_MIXES]
    last_ref, scr = refs[RW_MIXES:]
    x = h_ref[...]
    xn = x * lax.rsqrt(jnp.mean(x * x, axis=-1, keepdims=True) + EPS) * w_ref[...][None]

    @pl.when(pl.program_id(1) == 0)
    def _():
        scr[:, SUBLANES - 1:SUBLANES, :] = sh_ref[...]

    scr[:, SUBLANES:, :] = xn
    xx = scr[:, SUBLANES - 1:SUBLANES - 1 + tl, :] - xn
    for j, o_ref in enumerate(o_refs):
        o_ref[...] = (xn + xx * mu_ref[j:j + 1, :][None]).astype(o_ref.dtype)
    last = xn[:, tl - 1:tl, :]
    scr[:, SUBLANES - 1:SUBLANES, :] = last
    last_ref[...] = last


def norm_mix(h3, shift0, ln_w, mu, nb, tl):
    bsz, seq, d = h3.shape
    assert bsz % nb == 0 and seq % tl == 0 and tl % SUBLANES == 0
    blk = pl.BlockSpec((nb, tl, d), lambda b, t: (b, t, 0))
    row = pl.BlockSpec((nb, 1, d), lambda b, t: (b, 0, 0))
    outs = pl.pallas_call(
        functools.partial(_norm_mix_kernel, tl=tl),
        grid=(bsz // nb, seq // tl),
        in_specs=[blk, row, pl.BlockSpec((1, d), lambda b, t: (0, 0)), pl.BlockSpec((RW_MIXES, d), lambda b, t: (0, 0))],
        out_specs=[blk] * RW_MIXES + [row],
        out_shape=[jax.ShapeDtypeStruct((bsz, seq, d), BF16)] * RW_MIXES + [jax.ShapeDtypeStruct((bsz, 1, d), F32)],
        scratch_shapes=[pltpu.VMEM((nb, SUBLANES + tl, d), F32)],
        compiler_params=_params("parallel", "arbitrary"),
        name="norm_mix",
    )(h3, shift0.reshape(bsz, 1, d), ln_w.reshape(1, d), mu)
    return outs[:RW_MIXES], outs[RW_MIXES].reshape(bsz, d)


RW_PAIR = LANES // RW_HEAD
RW_DECAY_SCALE = math.exp(-0.5)
RW_SOLVE_BLOCK = 8


def _rwkv_kernel(r_ref, k_ref, v_ref, tw_ref, ta_ref, tg_ref, w2_ref, a2_ref, g2_ref, s0_ref,
                 w0_ref, a0_ref, kk_ref, ka_ref, rk_ref, lnw_ref, lnb_ref,
                 o_ref, sf_ref, s_scr, *, nb, hp, chunk):
    nu = hp * nb
    step = pl.program_id(2)
    lane = lax.broadcasted_iota(jnp.int32, (1, 1, LANES), 2)
    head1 = lane >= RW_HEAD

    def units(x):
        return jnp.concatenate([x[:, :, p * LANES:(p + 1) * LANES] for p in range(hp)], axis=0)

    def unit_rows(ref):
        return units(jnp.broadcast_to(ref[...][None], (nb, 1, hp * LANES)))

    w0, a0, k_k, k_a, r_k, ln_w, ln_b = (unit_rows(p) for p in
                                         (w0_ref, a0_ref, kk_ref, ka_ref, rk_ref, lnw_ref, lnb_ref))
    sq_row = lax.broadcasted_iota(jnp.int32, (LANES, LANES), 0) >= RW_HEAD
    sq_col = lax.broadcasted_iota(jnp.int32, (LANES, LANES), 1) >= RW_HEAD
    same_head = sq_row == sq_col
    ones_bd = same_head.astype(BF16)

    @pl.when(step == 0)
    def _():
        zero = jnp.zeros((nb, RW_HEAD, RW_HEAD), F32)
        for p in range(hp):
            top = jnp.concatenate([s0_ref[:, RW_PAIR * p], zero], axis=-1)
            bot = jnp.concatenate([zero, s0_ref[:, RW_PAIR * p + 1]], axis=-1)
            s_scr[p * nb:(p + 1) * nb] = jnp.concatenate([top, bot], axis=1)

    def bdot(spec, a, b):
        return jnp.einsum(spec, a.astype(BF16), b.astype(BF16), preferred_element_type=F32)

    def head_sum(x, two_pass=True):
        x2 = x.reshape(nu * chunk, LANES)
        hi = x2.astype(BF16)
        s = jnp.dot(hi, ones_bd, preferred_element_type=F32)
        if two_pass:
            lo = (x2 - hi.astype(F32)).astype(BF16)
            s = s + jnp.dot(lo, ones_bd, preferred_element_type=F32)
        return s.reshape(nu, chunk, LANES)

    def low_rank(t_ref, w_ref):
        t2 = t_ref[...].reshape(nb * chunk, t_ref.shape[-1])
        return units(jnp.dot(t2, w_ref[...], preferred_element_type=F32).reshape(nb, chunk, hp * LANES))

    srow = lax.broadcasted_iota(jnp.int32, (chunk, chunk), 0)
    scol = lax.broadcasted_iota(jnp.int32, (chunk, chunk), 1)
    tri = jnp.broadcast_to((scol <= srow).astype(BF16)[None], (nu, chunk, chunk))

    def tri_sum(x):
        hi = x.astype(BF16)
        lo = (x - hi.astype(F32)).astype(BF16)
        return (jnp.einsum("uts,usc->utc", tri, hi, preferred_element_type=F32)
                + jnp.einsum("uts,usc->utc", tri, lo, preferred_element_type=F32))

    def stack_heads(x):
        return jnp.concatenate([jnp.where(head1, 0.0, x), jnp.where(head1, x, 0.0)], axis=1).astype(BF16)

    r, k, v = (units(ref[...]) for ref in (r_ref, k_ref, v_ref))
    wl, al, g = low_rank(tw_ref, w2_ref), low_rank(ta_ref, a2_ref), low_rank(tg_ref, g2_ref)
    lw = (-RW_DECAY_SCALE) * jax.nn.sigmoid(w0 + wl)
    ag = jax.nn.sigmoid(a0 + al)
    kk = k * k_k
    kk = kk * jnp.minimum(lax.rsqrt(head_sum(kk * kk)), 1e12)
    k2 = k * (1.0 + (ag - 1.0) * k_a)
    cl = tri_sum(lw)
    e_pos = jnp.exp(cl)
    e_neg = jnp.exp(-cl)
    at = (-kk) * jnp.exp(cl - lw)
    bt = (kk * ag) * e_neg
    kt = k2 * e_neg
    rt = r * e_pos
    wc = e_pos[:, chunk - 1:chunk, :]
    trow = lax.broadcasted_iota(jnp.int32, (chunk, RW_PAIR * chunk), 0)
    tcol = lax.broadcasted_iota(jnp.int32, (chunk, RW_PAIR * chunk), 1)
    tcol = jnp.where(tcol >= chunk, tcol - chunk, tcol)
    strict = (tcol < trow)[None]
    incl = (tcol <= trow)[None]
    x2 = jnp.concatenate([at, rt], axis=1)
    pb = bdot("utc,usc->uts", x2, stack_heads(bt))
    pk = bdot("utc,usc->uts", x2, stack_heads(kt))
    lab = jnp.where(strict, pb[:, :chunk], 0.0)
    lak = jnp.where(strict, pk[:, :chunk], 0.0)
    arb = jnp.where(incl, pb[:, chunk:], 0.0)
    ark = jnp.where(incl, pk[:, chunk:], 0.0)
    v_bd = stack_heads(v)
    xa = at
    xv = bdot("uts,usc->utc", lak, v_bd)
    sub = min(RW_SOLVE_BLOCK, chunk)
    done_a, done_v = [], []
    for lo in range(0, chunk, sub):
        xa_i = xa[:, lo:lo + sub, :]
        xv_i = xv[:, lo:lo + sub, :]
        if lo:
            pad = jnp.zeros((nu, chunk - lo, LANES), F32)
            prev = jnp.concatenate([stack_heads(jnp.concatenate(done_a + [pad], axis=1)),
                                    stack_heads(jnp.concatenate(done_v + [pad], axis=1))], axis=-1)
            upd = bdot("uts,usc->utc", lab[:, lo:lo + sub, :], prev)
            xa_i = xa_i + upd[:, :, :LANES]
            xv_i = xv_i + upd[:, :, LANES:]
        l0 = lab[:, lo:lo + sub, lo:lo + sub]
        l1 = lab[:, lo:lo + sub, chunk + lo:chunk + lo + sub]
        for s in range(sub - 1):
            m = jnp.where(head1, l1[:, :, s:s + 1], l0[:, :, s:s + 1])
            xa_i = xa_i + m * xa_i[:, s:s + 1, :]
            xv_i = xv_i + m * xv_i[:, s:s + 1, :]
        done_a.append(xa_i)
        done_v.append(xv_i)
    ah = jnp.concatenate(done_a, axis=1)
    vh = jnp.concatenate(done_v, axis=1)
    both = bdot("uts,usc->utc", arb, jnp.concatenate([stack_heads(ah), stack_heads(vh)], axis=-1))
    rh = rt + both[:, :, :LANES]
    yh = both[:, :, LANES:] + bdot("uts,usc->utc", ark, v_bd)
    gp = jnp.where(same_head, bdot("utj,utk->ujk", ah, bt), 0.0)
    ht = jnp.where(same_head, bdot("utv,utk->uvk", jnp.concatenate([vh, v], axis=1),
                                   jnp.concatenate([bt, kt], axis=1)), 0.0)
    st = s_scr[...]
    y = bdot("utk,uvk->utv", rh, st) + yh
    st = (st + bdot("uvj,ujk->uvk", st, gp) + ht) * wc
    s_scr[...] = st
    inv_n = 1.0 / RW_HEAD
    yc = y - head_sum(y, two_pass=False) * inv_n
    var = head_sum(yc * yc, two_pass=False) * inv_n
    y = yc * lax.rsqrt(var + RW_GN_EPS) * ln_w + ln_b
    y = y + head_sum(r * k2 * r_k, two_pass=False) * v
    out = (y * g).astype(o_ref.dtype)
    for p in range(hp):
        o_ref[:, :, p * LANES:(p + 1) * LANES] = out[p * nb:(p + 1) * nb]

    @pl.when(step == pl.num_programs(2) - 1)
    def _():
        for p in range(hp):
            sf_ref[:, RW_PAIR * p] = st[p * nb:(p + 1) * nb, :RW_HEAD, :RW_HEAD]
            sf_ref[:, RW_PAIR * p + 1] = st[p * nb:(p + 1) * nb, RW_HEAD:, RW_HEAD:]


def _rwkv_chunk(seq):
    for c in (48, 32, 16, 8):
        if seq % c == 0:
            return c
    raise ValueError(seq)


def rwkv7(r, k, v, low, low_w, s0, w0, a0, k_k, k_a, r_k, ln_w, ln_b, nb, hp):
    bsz, seq, d = r.shape
    chunk = _rwkv_chunk(seq)
    heads = hp * RW_PAIR
    assert bsz % nb == 0 and RW_HEADS % heads == 0
    seq_blk = pl.BlockSpec((nb, chunk, hp * LANES), lambda h, b, t: (b, t, h))
    st_blk = pl.BlockSpec((nb, heads, RW_HEAD, RW_HEAD), lambda h, b, t: (b, h, 0, 0))
    vec = pl.BlockSpec((1, hp * LANES), lambda h, b, t: (0, h))
    low_blk = [pl.BlockSpec((nb, chunk, x.shape[-1]), lambda h, b, t: (b, t, 0)) for x in low]
    low_w_blk = [pl.BlockSpec((w.shape[0], hp * LANES), lambda h, b, t: (0, h)) for w in low_w]
    out, sf = pl.pallas_call(
        functools.partial(_rwkv_kernel, nb=nb, hp=hp, chunk=chunk),
        grid=(RW_HEADS // heads, bsz // nb, seq // chunk),
        in_specs=[seq_blk] * 3 + low_blk + low_w_blk + [st_blk] + [vec] * 7,
        out_specs=[seq_blk, st_blk],
        out_shape=[jax.ShapeDtypeStruct((bsz, seq, d), BF16),
                   jax.ShapeDtypeStruct((bsz, RW_HEADS, RW_HEAD, RW_HEAD), F32)],
        scratch_shapes=[pltpu.VMEM((hp * nb, LANES, LANES), F32)],
        compiler_params=_params("parallel", "parallel", "arbitrary"),
        name="rwkv7",
    )(r, k, v, *low, *low_w, s0, *(p.reshape(1, d) for p in (w0, a0, k_k, k_a, r_k, ln_w, ln_b)))
    return out, sf


def _ffn_in_kernel(x_ref, wa_ref, wv_ref, e_ref, cw_ref, cb_ref, o_ref, st_ref, scr, *, nb, seq, sb, sr):
    tn = wa_ref.shape[1]
    cw = cw_ref[...]
    cb = cb_ref[...][None]
    scr[:, SUBLANES - (CONV_W - 1):SUBLANES, :] = e_ref[...]
    for b0 in range(0, nb, sb):
        for r0 in range(0, seq, sr):
            lo = b0 * seq + r0
            x = x_ref[lo:lo + sb * sr, :]
            a = jnp.dot(x, wa_ref[...], preferred_element_type=F32).reshape(sb, sr, tn)
            v = jnp.dot(x, wv_ref[...], preferred_element_type=F32).reshape(sb, sr, tn)
            scr[b0:b0 + sb, SUBLANES + r0:SUBLANES + r0 + sr, :] = a
            c = cb + cw[CONV_W - 1:CONV_W][None] * a
            for j in range(CONV_W - 1):
                first = SUBLANES + r0 - (CONV_W - 1 - j)
                c = c + cw[j:j + 1][None] * scr[b0:b0 + sb, first:first + sr, :]
            o_ref[lo:lo + sb * sr, :] = (jax.nn.gelu(c) * v).reshape(sb * sr, tn).astype(o_ref.dtype)
    st_ref[...] = scr[:, SUBLANES + seq - (CONV_W - 1):SUBLANES + seq, :]


def ffn_in(xb, conv0, w_in, layer, conv_w, conv_b, bsz, seq, nb, tn, sub):
    t, d = xb.shape
    sb, sr = sub
    assert t == bsz * seq and bsz % nb == 0 and D_FF % tn == 0
    assert nb % sb == 0 and seq % sr == 0 and sr % SUBLANES == 0 and (sr == seq or nb == sb == 1)
    col = lambda i, j: (0, j)
    n_col = D_FF // tn
    out, st = pl.pallas_call(
        functools.partial(_ffn_in_kernel, nb=nb, seq=seq, sb=sb, sr=sr),
        grid=(bsz // nb, n_col),
        in_specs=[pl.BlockSpec((nb * seq, d), lambda i, j: (i, 0)),
                  pl.BlockSpec((None, d, tn), lambda i, j: (layer, 0, j)),
                  pl.BlockSpec((None, d, tn), lambda i, j: (layer, 0, j + n_col)),
                  pl.BlockSpec((nb, CONV_W - 1, tn), lambda i, j: (i, 0, j)),
                  pl.BlockSpec((CONV_W, tn), col), pl.BlockSpec((1, tn), col)],
        out_specs=[pl.BlockSpec((nb * seq, tn), lambda i, j: (i, j)),
                   pl.BlockSpec((nb, CONV_W - 1, tn), lambda i, j: (i, 0, j))],
        out_shape=[jax.ShapeDtypeStruct((t, D_FF), BF16),
                   jax.ShapeDtypeStruct((bsz, CONV_W - 1, D_FF), F32)],
        scratch_shapes=[pltpu.VMEM((nb, SUBLANES + seq, tn), F32)],
        compiler_params=_params("parallel", "parallel"),
        name="ffn_in",
    )(xb, w_in, w_in, conv0, conv_w, conv_b.reshape(1, D_FF))
    return out, st


FFN_DOWN_K_STEPS = 2


def _channel_mixer(h, conv0, layer, p, cfg, bsz, seq):
    (xb,) = rmsnorm(h, p["ln_ffn"][layer], (BF16,))
    gated, n_cv = ffn_in(xb, conv0, p["ffn_w_in"], layer, p["ffn_conv_w"][layer], p["ffn_conv_b"][layer],
                         bsz, seq, cfg["ffn_nb"], cfg["ffn_tn"], cfg["ffn_sub"])
    return matmul_residual_split(gated, p["ffn_w_down"], layer, h, FFN_DOWN_K_STEPS), n_cv


def _trunk(x3, s5r, s5i, hg, rw, sh, cv, p, cfg):
    bsz, seq, d = x3.shape
    seq += cfg["front"]
    t = bsz * seq

    if cfg["front"]:
        h, xb = embed_norm(x3, p["meta"], p["ln_mix"][0])
    else:
        h = x3.reshape(t, d)
        (xb,) = rmsnorm(h, p["ln_mix"][0], (BF16,))
    z = matmul([xb], [p["ev_w_in"]], F32).reshape(bsz, seq, EVEN_IN)
    if cfg["s5_tl"]:
        ys5, n_s5r, n_s5i = s5_scan_long(z, s5r[0], s5i[0], *p["s5"], tl=cfg["s5_tl"])
    else:
        ys5, n_s5r, n_s5i = s5_scan(z, s5r[0], s5i[0], *p["s5"], nb=cfg["s5_nb"])
    ys5 = ys5.reshape(t, S5_WIDTH)
    ya = matmul([ys5], [p["s5_w_glu"]], BF16, epilogue="glu", extra=ys5)
    yb, n_hg = hgrn2(z, hg[0], p["hg_lb"], p["hg_norm_w"], nb=cfg["hg_nb"], hh=cfg["hg_hh"])
    h = matmul([ya, yb.reshape(t, -1)], [(p["ev_w_out"], 0, 0), (p["ev_w_out"], 0, 1)], F32,
               epilogue="residual", extra=h)
    h, n_cv0 = _channel_mixer(h, cv[0], 0, p, cfg, bsz, seq)

    mixes, n_sh = norm_mix(h.reshape(bsz, seq, d), sh[0], p["ln_mix"][1], p["rw_mu"], *cfg["mix_blk"])
    xr, xw, xk, xv, xa, xg = (m.reshape(t, d) for m in mixes)
    r = matmul([xr], [p["rw_w_r"]], F32)
    k = matmul([xk], [p["rw_w_k"]], F32)
    v = matmul([xv], [p["rw_w_v"]], F32)
    low = (matmul([xw], [p["rw_w1"]], BF16, act="tanh"), matmul([xa], [p["rw_a1"]], BF16),
           matmul([xg], [p["rw_g1"]], BF16, act="sigmoid"))
    as3 = lambda a: a.reshape(bsz, seq, a.shape[-1])
    yo, n_rw = rwkv7(as3(r), as3(k), as3(v), [as3(x) for x in low], (p["rw_w2"], p["rw_a2"], p["rw_g2"]),
                     rw[0], *p["rw_vec"], nb=cfg["rw_nb"], hp=cfg["rw_hp"])
    h = matmul([yo.reshape(t, d)], [p["rw_w_o"]], F32, epilogue="residual", extra=h)
    h, n_cv1 = _channel_mixer(h, cv[1], 1, p, cfg, bsz, seq)

    (y,) = rmsnorm(h, p["ln_final"], (F32,), rows=(bsz, seq, cfg["front"]) if cfg["front"] else None)
    return (y.reshape(bsz, seq - cfg["front"], d), n_s5r[None], n_s5i[None], n_hg[None], n_rw[None], n_sh[None],
            jnp.stack([n_cv0, n_cv1]))


PROMPT_CFG = dict(front=N_META, s5_tl=344, s5_nb=None, hg_nb=4, hg_hh=8, mix_blk=(1, 344), rw_nb=4, rw_hp=16, ffn_nb=1, ffn_tn=512, ffn_sub=(1, 688))
SAMPLE_CFG = dict(front=0, s5_tl=None, s5_nb=32, hg_nb=4, hg_hh=8, mix_blk=(32, 8), rw_nb=16, rw_hp=4, ffn_nb=128, ffn_tn=512, ffn_sub=(32, 8))


def kernel(x_prompt, x_sample, state_s5_re, state_s5_im, state_hgrn, state_rwkv, state_shift, state_conv, meta_tokens, ln_mix, ln_ffn, ln_final, ev_w_in, ev_w_out, s5_lam_re, s5_lam_im, s5_log_step, s5_b_re, s5_b_im, s5_c_re, s5_c_im, s5_d, s5_w_glu, hg_lb, hg_norm_w, rw_mu, rw_w0, rw_w1, rw_w2, rw_a0, rw_a1, rw_a2, rw_g1, rw_g2, rw_k_k, rw_k_a, rw_r_k, rw_w_r, rw_w_k, rw_w_v, rw_w_o, rw_ln_w, rw_ln_b, ffn_w_in, ffn_conv_w, ffn_conv_b, ffn_w_down):
    bf = lambda w: w.astype(BF16)
    lb_all = hg_lower_bounds(hg_lb)
    pwr, pwi, bbr_t, bbi_t = s5_prep(s5_lam_re[0], s5_lam_im[0], s5_log_step[0], s5_b_re[0], s5_b_im[0])
    wbr, wbi, wcr, wci = _s5_block_weights(bbr_t, bbi_t, s5_c_re[0], s5_c_im[0])
    p = {
        "meta": meta_tokens, "ln_mix": ln_mix, "ln_ffn": ln_ffn, "ln_final": ln_final,
        "ev_w_in": ev_w_in[0],
        "ev_w_out": ev_w_out,
        "s5": (pwr, pwi, wbr, wbi, wcr, wci, s5_d[0].reshape(1, S5_WIDTH)),
        "s5_w_glu": s5_w_glu[0],
        "hg_lb": lb_all[0], "hg_norm_w": hg_norm_w[0],
        "rw_mu": rw_mu[0],
        "rw_w1": rw_w1[0], "rw_w2": bf(rw_w2[0]), "rw_a1": rw_a1[0], "rw_a2": bf(rw_a2[0]),
        "rw_g1": rw_g1[0], "rw_g2": bf(rw_g2[0]),
        "rw_w_r": rw_w_r[0], "rw_w_k": rw_w_k[0], "rw_w_v": rw_w_v[0], "rw_w_o": rw_w_o[0],
        "rw_vec": (rw_w0[0], rw_a0[0], rw_k_k[0], rw_k_a[0], rw_r_k[0].reshape(D_MODEL), rw_ln_w[0], rw_ln_b[0]),
        "ffn_w_in": bf(ffn_w_in), "ffn_conv_w": ffn_conv_w, "ffn_conv_b": ffn_conv_b,
        "ffn_w_down": bf(ffn_w_down),
    }

    bsz = x_prompt.shape[0]
    zeros = lambda *s: jnp.zeros(s, F32)
    outs_p = _trunk(x_prompt,
                    zeros(1, bsz, S5_GROUPS, S5_STATE), zeros(1, bsz, S5_GROUPS, S5_STATE),
                    zeros(1, bsz, HG_HEADS, HG_K, HG_V), zeros(1, bsz, RW_HEADS, RW_HEAD, RW_HEAD),
                    zeros(1, bsz, D_MODEL), zeros(2, bsz, CONV_W - 1, D_FF), p, PROMPT_CFG)
    outs_s = _trunk(x_sample, state_s5_re, state_s5_im, state_hgrn, state_rwkv, state_shift, state_conv,
                    p, SAMPLE_CFG)
    return tuple(outs_p[:1]) + tuple(outs_s[:1]) + tuple(outs_p[1:]) + tuple(outs_s[1:])
```

```python
import functools
import math

import jax
import jax.numpy as jnp
from jax import lax
from jax.experimental import pallas as pl
from jax.experimental.pallas import tpu as pltpu

F32 = jnp.float32
BF16 = jnp.bfloat16

D_MODEL = 2048
N_META = 16
EPS = 1e-6
S5_WIDTH = 1024
S5_GROUP = 16
S5_GROUPS = 64
S5_STATE = 64
S5_CH = S5_GROUPS * S5_STATE
HG_HEADS = 8
HG_K = 128
HG_V = 128
HG_CHUNK = 16
EVEN_IN = 5120
RW_HEAD = 64
RW_HEADS = 32
RW_GN_EPS = 64e-5
D_FF = 5632
CONV_W = 3

LANES = 128
SUBLANES = 8
VMEM_LIMIT = 56 * 1024 * 1024


def _params(*sem):
    return pltpu.CompilerParams(dimension_semantics=sem, vmem_limit_bytes=VMEM_LIMIT)


def _row_tile(t, cap=1024):
    best = None
    for d in range(16, min(t, cap) + 1, 16):
        if t % d == 0:
            best = d
    assert best is not None, t
    return best


def _rms_kernel(x_ref, w_ref, *o_refs):
    x = x_ref[...]
    y = x * lax.rsqrt(jnp.mean(x * x, axis=-1, keepdims=True) + EPS) * w_ref[...]
    for o_ref in o_refs:
        o_ref[...] = y.astype(o_ref.dtype)


def rmsnorm(x, w, dtypes, rows=None):
    t, d = x.shape
    if rows is None:
        tm = _row_tile(t)
        n_out = t
        grid = (t // tm,)
        in_spec = pl.BlockSpec((tm, d), lambda i: (i, 0))
        out_spec = in_spec
        vec = pl.BlockSpec((1, d), lambda i: (0, 0))
    else:
        bsz, seq, front = rows
        keep = seq - front
        assert t == bsz * seq and front % 16 == 0
        tm = _row_tile(keep)
        per = keep // tm
        n_out = bsz * keep
        grid = (bsz, per)
        in_spec = pl.BlockSpec((pl.Element(tm), pl.Element(d)),
                               lambda b, i: (pl.multiple_of(b * seq + front + i * tm, 16), 0))
        out_spec = pl.BlockSpec((tm, d), lambda b, i: (b * per + i, 0))
        vec = pl.BlockSpec((1, d), lambda b, i: (0, 0))
    outs = pl.pallas_call(
        _rms_kernel,
        grid=grid,
        in_specs=[in_spec, vec],
        out_specs=[out_spec for _ in dtypes],
        out_shape=[jax.ShapeDtypeStruct((n_out, d), dt) for dt in dtypes],
        compiler_params=_params(*(["parallel"] * len(grid))),
        name="rmsnorm",
    )(x, w.reshape(1, d))
    return outs


def _embed_norm_kernel(x_ref, m_ref, w_ref, h_ref, xb_ref, *, front):
    x = x_ref[...]
    first = pl.program_id(1) == 0
    body = jnp.where(first, pltpu.roll(x, front, axis=0), x)
    head = jnp.where(first, m_ref[...], x[:front])
    rows = jnp.concatenate([head, body[front:]], axis=0)
    h_ref[...] = rows
    y = rows * lax.rsqrt(jnp.mean(rows * rows, axis=-1, keepdims=True) + EPS) * w_ref[...]
    xb_ref[...] = y.astype(xb_ref.dtype)


def embed_norm(x3, meta, w):
    bsz, seq, d = x3.shape
    front = meta.shape[0]
    total = seq + front
    tm = _row_tile(total)
    per = total // tm
    assert front % 16 == 0 and tm > front
    blk = pl.BlockSpec((tm, d), lambda b, j: (b * per + j, 0))
    return pl.pallas_call(
        functools.partial(_embed_norm_kernel, front=front),
        grid=(bsz, per),
        in_specs=[pl.BlockSpec((pl.Element(tm), pl.Element(d)),
                               lambda b, j: (pl.multiple_of(b * seq + jnp.maximum(j * tm - front, 0), 16), 0)),
                  pl.BlockSpec((front, d), lambda b, j: (0, 0)),
                  pl.BlockSpec((1, d), lambda b, j: (0, 0))],
        out_specs=[blk, blk],
        out_shape=[jax.ShapeDtypeStruct((bsz * total, d), F32), jax.ShapeDtypeStruct((bsz * total, d), BF16)],
        compiler_params=_params("parallel", "parallel"),
        name="embed_norm",
    )(x3.reshape(bsz * seq, d), meta, w.reshape(1, d))


def _act(x, act):
    if act == "tanh":
        return jnp.tanh(x)
    if act == "sigmoid":
        return jax.nn.sigmoid(x)
    assert act is None
    return x


def _mm_kernel(*refs, n_a, act, epilogue, cast_w):
    a_refs = refs[:n_a]
    w_refs = refs[n_a:2 * n_a]
    rest = refs[2 * n_a:]
    if cast_w:
        rest, wb_refs = rest[:-n_a], rest[-n_a:]

        @pl.when(pl.program_id(1) == 0)
        def _():
            for w_ref, wb_ref in zip(w_refs, wb_refs):
                wb_ref[...] = w_ref[...].astype(BF16)

        w_refs = wb_refs
    o_ref = rest[-1]
    acc = jnp.dot(a_refs[0][...].astype(BF16), w_refs[0][...], preferred_element_type=F32)
    for a_ref, w_ref in zip(a_refs[1:], w_refs[1:]):
        acc = acc + jnp.dot(a_ref[...].astype(BF16), w_ref[...], preferred_element_type=F32)
    acc = _act(acc, act)
    if epilogue == "residual":
        acc = rest[0][...] + acc
    elif epilogue == "glu":
        acc = rest[0][...] * jax.nn.sigmoid(acc)
    o_ref[...] = acc.astype(o_ref.dtype)


MM_VMEM_BUDGET = 40 * 1024 * 1024
MM_CAST_VMEM_BUDGET = 48 * 1024 * 1024
MM_MAX_A_BLOCK = 6 * 1024 * 1024
MXU_WIDTH = 256


def _mm_tiles(t, a_row_bytes, w_col_bytes, n, out_bytes, has_extra, budget=MM_VMEM_BUDGET):
    rows = [d for d in range(16, t + 1, 16) if t % d == 0 and d * a_row_bytes <= MM_MAX_A_BLOCK]
    cols = [d for d in range(LANES, n + 1, LANES) if n % d == 0] or [n]
    best, best_score = None, -1.0
    for tm in rows:
        for tn in cols:
            est = 2 * tm * a_row_bytes + w_col_bytes * tn + tm * tn * (2 * out_bytes + 4 + (8 if has_extra else 0))
            if est > budget:
                continue
            score = tm * tn * (1.0 if tn % MXU_WIDTH == 0 else 0.8)
            if score > best_score:
                best, best_score = (tm, tn), score
    assert best is not None, (t, a_row_bytes, w_col_bytes, n)
    return best


def matmul(a_list, w_list, out_dtype, act=None, epilogue=None, extra=None):
    t = a_list[0].shape[0]
    w0 = w_list[0][0] if isinstance(w_list[0], tuple) else w_list[0]
    n = w0.shape[-1]
    cast_w = w0.dtype == F32
    k_rows = sum(a.shape[1] for a in a_list)
    a_row_bytes = sum(a.shape[1] * a.dtype.itemsize for a in a_list)
    w_col_bytes = k_rows * (2 * 4 + 2 if cast_w else 2 * 2)
    tm, tn = _mm_tiles(t, a_row_bytes, w_col_bytes, n, jnp.dtype(out_dtype).itemsize, epilogue is not None,
                       budget=MM_CAST_VMEM_BUDGET if cast_w else MM_VMEM_BUDGET)
    rc = (lambda f: lambda j, i: f(i, j)) if cast_w else (lambda f: f)
    in_specs = [pl.BlockSpec((tm, a.shape[1]), rc(lambda i, j: (i, 0))) for a in a_list]
    args = list(a_list)
    for a, w in zip(a_list, w_list):
        if isinstance(w, tuple):
            w, layer, kblk = w
            in_specs.append(pl.BlockSpec((None, a.shape[1], tn),
                                         rc(lambda i, j, layer=layer, kblk=kblk: (layer, kblk, j))))
        else:
            in_specs.append(pl.BlockSpec((w.shape[0], tn), rc(lambda i, j: (0, j))))
        args.append(w)
    if epilogue is not None:
        in_specs.append(pl.BlockSpec((tm, tn), rc(lambda i, j: (i, j))))
        args.append(extra)
    return pl.pallas_call(
        functools.partial(_mm_kernel, n_a=len(a_list), act=act, epilogue=epilogue, cast_w=cast_w),
        grid=(n // tn, t // tm) if cast_w else (t // tm, n // tn),
        in_specs=in_specs,
        out_specs=pl.BlockSpec((tm, tn), rc(lambda i, j: (i, j))),
        out_shape=jax.ShapeDtypeStruct((t, n), out_dtype),
        scratch_shapes=[pltpu.VMEM((a.shape[1], tn), BF16) for a in a_list] if cast_w else [],
        compiler_params=_params("parallel", "arbitrary" if cast_w else "parallel"),
        name="matmul",
    )(*args)


def _mm_resid_split_kernel(a_ref, w_ref, res_ref, o_ref):
    part = jnp.dot(a_ref[...], w_ref[...], preferred_element_type=F32)

    @pl.when(pl.program_id(2) == 0)
    def _():
        o_ref[...] = res_ref[...] + part

    @pl.when(pl.program_id(2) > 0)
    def _():
        o_ref[...] += part


def matmul_residual_split(a, w, layer, res, k_steps):
    t, k = a.shape
    n = w.shape[2]
    tk = k // k_steps
    assert k % k_steps == 0 and tk % LANES == 0
    tm, tn = _mm_tiles(t, 2 * tk, 4 * tk, n, 4, True)
    return pl.pallas_call(
        _mm_resid_split_kernel,
        grid=(t // tm, n // tn, k_steps),
        in_specs=[pl.BlockSpec((tm, tk), lambda i, j, s: (i, s)),
                  pl.BlockSpec((None, tk, tn), lambda i, j, s: (layer, s, j)),
                  pl.BlockSpec((tm, tn), lambda i, j, s: (i, j))],
        out_specs=pl.BlockSpec((tm, tn), lambda i, j, s: (i, j)),
        out_shape=jax.ShapeDtypeStruct((t, n), F32),
        compiler_params=_params("parallel", "parallel", "arbitrary"),
        name="matmul_split",
    )(a, w, res)


def _s5_prep_kernel(lr_ref, li_ref, ls_ref, brt_ref, bit_ref, pwr_ref, pwi_ref, bbr_ref, bbi_ref):
    lr = jnp.minimum(lr_ref[...], -1e-4)
    li = li_ref[...]
    dt = jnp.exp(ls_ref[...])
    n = lax.broadcasted_iota(jnp.int32, (SUBLANES, S5_CH), 0).astype(F32) + 1.0
    mag = jnp.exp(n * (lr * dt))
    ang = n * (li * dt)
    pwr = mag * jnp.cos(ang)
    pwi = mag * jnp.sin(ang)
    pwr_ref[...] = pwr
    pwi_ref[...] = pwi
    ar = pwr[0:1]
    ai = pwi[0:1]
    den = lr * lr + li * li
    zr = ((ar - 1.0) * lr + ai * li) / den
    zi = (ai * lr - (ar - 1.0) * li) / den
    br = brt_ref[...]
    bi = bit_ref[...]
    bbr_ref[...] = zr * br - zi * bi
    bbi_ref[...] = zr * bi + zi * br


def s5_prep(lam_re, lam_im, log_step, b_re, b_im):
    lr = lam_re.reshape(1, S5_CH)
    li = lam_im.reshape(1, S5_CH)
    ls = jnp.broadcast_to(log_step[:, None], (S5_GROUPS, S5_STATE)).reshape(1, S5_CH)
    brt = b_re.reshape(S5_CH, S5_GROUP).T
    bit = b_im.reshape(S5_CH, S5_GROUP).T
    return pl.pallas_call(
        _s5_prep_kernel,
        out_shape=[jax.ShapeDtypeStruct((SUBLANES, S5_CH), F32)] * 2
        + [jax.ShapeDtypeStruct((S5_GROUP, S5_CH), F32)] * 2,
        name="s5_prep",
    )(lr, li, ls, brt, bit)


S5_BLK_GROUPS = LANES // S5_GROUP
S5_BLKS = S5_WIDTH // LANES
S5_BLK_CH = S5_BLK_GROUPS * S5_STATE


def _cmul_add(xr, xi, mr, mi, sr, si):
    return xr + mr * sr - mi * si, xi + mr * si + mi * sr


def _s5_kernel(u_ref, h0r_ref, h0i_ref, pwr_ref, pwi_ref, wbr_ref, wbi_ref, wcr_ref, wci_ref, d_ref,
               y_ref, hr_ref, hi_ref, xr_scr, xi_scr, *, nb, seq):
    u2 = u_ref[...].reshape(nb * seq, LANES)
    ub = u2.astype(BF16)
    xr_scr[...] = jnp.dot(ub, wbr_ref[0], preferred_element_type=F32).reshape(nb, seq, S5_BLK_CH)
    xi_scr[...] = jnp.dot(ub, wbi_ref[0], preferred_element_type=F32).reshape(nb, seq, S5_BLK_CH)

    pwr = pwr_ref[...]
    pwi = pwi_ref[...]
    row = lax.broadcasted_iota(jnp.int32, (SUBLANES, S5_BLK_CH), 0)
    steps = []
    for d in (1, 2, 4):
        keep = row >= d
        steps.append((d, jnp.where(keep, pwr[d - 1:d], 0.0)[None], jnp.where(keep, pwi[d - 1:d], 0.0)[None]))
    pr = pwr[None]
    pi = pwi[None]

    def tile(i, carry):
        cr, ci = carry
        o = pl.multiple_of(i * SUBLANES, SUBLANES)
        xr = xr_scr[:, pl.ds(o, SUBLANES), :]
        xi = xi_scr[:, pl.ds(o, SUBLANES), :]
        for d, mr, mi in steps:
            sr = pltpu.roll(xr, d, axis=1)
            si = pltpu.roll(xi, d, axis=1)
            xr, xi = _cmul_add(xr, xi, mr, mi, sr, si)
        xr, xi = _cmul_add(xr, xi, pr, pi, cr, ci)
        xr_scr[:, pl.ds(o, SUBLANES), :] = xr
        xi_scr[:, pl.ds(o, SUBLANES), :] = xi
        return xr[:, SUBLANES - 1:SUBLANES, :], xi[:, SUBLANES - 1:SUBLANES, :]

    hr, hi = lax.fori_loop(0, seq // SUBLANES, tile, (h0r_ref[...], h0i_ref[...]))
    hr_ref[...] = hr
    hi_ref[...] = hi

    xr = xr_scr[...].reshape(nb * seq, S5_BLK_CH).astype(BF16)
    xi = xi_scr[...].reshape(nb * seq, S5_BLK_CH).astype(BF16)
    y = (jnp.dot(xr, wcr_ref[0], preferred_element_type=F32)
         - jnp.dot(xi, wci_ref[0], preferred_element_type=F32)
         + d_ref[...] * u2)
    y_ref[...] = jax.nn.gelu(y).reshape(nb, seq, LANES)


def s5_scan(z3, h0r, h0i, pwr, pwi, wbr, wbi, wcr, wci, d, nb):
    bsz, seq, _ = z3.shape
    assert seq % SUBLANES == 0 and bsz % nb == 0
    seq_blk = pl.BlockSpec((nb, seq, LANES), lambda b, k: (b, 0, k))
    st_blk = pl.BlockSpec((nb, 1, S5_BLK_CH), lambda b, k: (b, 0, k))
    pw_blk = pl.BlockSpec((SUBLANES, S5_BLK_CH), lambda b, k: (0, k))
    wb_blk = pl.BlockSpec((1, LANES, S5_BLK_CH), lambda b, k: (k, 0, 0))
    wc_blk = pl.BlockSpec((1, S5_BLK_CH, LANES), lambda b, k: (k, 0, 0))
    y, hr, hi = pl.pallas_call(
        functools.partial(_s5_kernel, nb=nb, seq=seq),
        grid=(bsz // nb, S5_BLKS),
        in_specs=[seq_blk, st_blk, st_blk, pw_blk, pw_blk, wb_blk, wb_blk, wc_blk, wc_blk,
                  pl.BlockSpec((1, LANES), lambda b, k: (0, k))],
        out_specs=[seq_blk, st_blk, st_blk],
        out_shape=[jax.ShapeDtypeStruct((bsz, seq, S5_WIDTH), F32),
                   jax.ShapeDtypeStruct((bsz, 1, S5_CH), F32),
                   jax.ShapeDtypeStruct((bsz, 1, S5_CH), F32)],
        scratch_shapes=[pltpu.VMEM((nb, seq, S5_BLK_CH), F32), pltpu.VMEM((nb, seq, S5_BLK_CH), F32)],
        compiler_params=_params("parallel", "parallel"),
        name="s5_scan",
    )(z3, h0r.reshape(bsz, 1, S5_CH), h0i.reshape(bsz, 1, S5_CH), pwr, pwi, wbr, wbi, wcr, wci, d)
    return y, hr.reshape(bsz, S5_GROUPS, S5_STATE), hi.reshape(bsz, S5_GROUPS, S5_STATE)


S5_LANE_TILES = S5_BLK_CH // LANES


def _s5_long_kernel(u_ref, h0r_ref, h0i_ref, ar_ref, ai_ref, wbr_ref, wbi_ref, wcr_ref, wci_ref, d_ref,
                    y_ref, hr_ref, hi_ref, xr_scr, xi_scr, cr_scr, ci_scr, *, tl):
    step = pl.program_id(1)

    @pl.when(step == 0)
    def _():
        cr_scr[...] = h0r_ref[0]
        ci_scr[...] = h0i_ref[0]

    for k in range(S5_BLKS):
        ub = u_ref[0, :, k * LANES:(k + 1) * LANES].astype(BF16)
        bur = jnp.dot(ub, wbr_ref[k], preferred_element_type=F32)
        bui = jnp.dot(ub, wbi_ref[k], preferred_element_type=F32)
        for j in range(S5_LANE_TILES):
            xr_scr[j, k * tl:(k + 1) * tl, :] = bur[:, j * LANES:(j + 1) * LANES]
            xi_scr[j, k * tl:(k + 1) * tl, :] = bui[:, j * LANES:(j + 1) * LANES]

    ar = [ar_ref[:, j * LANES:(j + 1) * LANES] for j in range(S5_LANE_TILES)]
    ai = [ai_ref[:, j * LANES:(j + 1) * LANES] for j in range(S5_LANE_TILES)]

    def token(t, carry):
        cr, ci = carry
        nr, ni = [], []
        for j in range(S5_LANE_TILES):
            rows = pl.ds(t, S5_BLKS, stride=tl)
            xr, xi = _cmul_add(xr_scr[j, rows, :], xi_scr[j, rows, :], ar[j], ai[j], cr[j], ci[j])
            xr_scr[j, rows, :] = xr
            xi_scr[j, rows, :] = xi
            nr.append(xr)
            ni.append(xi)
        return tuple(nr), tuple(ni)

    init = (tuple(cr_scr[:, j * LANES:(j + 1) * LANES] for j in range(S5_LANE_TILES)),
            tuple(ci_scr[:, j * LANES:(j + 1) * LANES] for j in range(S5_LANE_TILES)))
    cr, ci = lax.fori_loop(0, tl, token, init, unroll=8)
    cr = jnp.concatenate(cr, axis=-1)
    ci = jnp.concatenate(ci, axis=-1)
    cr_scr[...] = cr
    ci_scr[...] = ci
    hr_ref[0] = cr
    hi_ref[0] = ci

    for k in range(S5_BLKS):
        xr = jnp.concatenate([xr_scr[j, k * tl:(k + 1) * tl, :] for j in range(S5_LANE_TILES)], axis=-1)
        xi = jnp.concatenate([xi_scr[j, k * tl:(k + 1) * tl, :] for j in range(S5_LANE_TILES)], axis=-1)
        u = u_ref[0, :, k * LANES:(k + 1) * LANES]
        y = (jnp.dot(xr.astype(BF16), wcr_ref[k], preferred_element_type=F32)
             - jnp.dot(xi.astype(BF16), wci_ref[k], preferred_element_type=F32)
             + d_ref[:, k * LANES:(k + 1) * LANES] * u)
        y_ref[0, :, k * LANES:(k + 1) * LANES] = jax.nn.gelu(y)


def s5_scan_long(z3, h0r, h0i, pwr, pwi, wbr, wbi, wcr, wci, d, tl):
    bsz, seq, _ = z3.shape
    assert seq % tl == 0 and tl % SUBLANES == 0
    seq_blk = pl.BlockSpec((1, tl, S5_WIDTH), lambda b, t: (b, t, 0))
    st_blk = pl.BlockSpec((1, S5_BLKS, S5_BLK_CH), lambda b, t: (b, 0, 0))
    lam_blk = pl.BlockSpec((S5_BLKS, S5_BLK_CH), lambda b, t: (0, 0))
    wb_blk = pl.BlockSpec((S5_BLKS, LANES, S5_BLK_CH), lambda b, t: (0, 0, 0))
    wc_blk = pl.BlockSpec((S5_BLKS, S5_BLK_CH, LANES), lambda b, t: (0, 0, 0))
    rows = pltpu.VMEM((S5_LANE_TILES, S5_BLKS * tl, LANES), F32)
    carry = pltpu.VMEM((S5_BLKS, S5_BLK_CH), F32)
    y, hr, hi = pl.pallas_call(
        functools.partial(_s5_long_kernel, tl=tl),
        grid=(bsz, seq // tl),
        in_specs=[seq_blk, st_blk, st_blk, lam_blk, lam_blk, wb_blk, wb_blk, wc_blk, wc_blk,
                  pl.BlockSpec((1, S5_WIDTH), lambda b, t: (0, 0))],
        out_specs=[seq_blk, st_blk, st_blk],
        out_shape=[jax.ShapeDtypeStruct((bsz, seq, S5_WIDTH), F32),
                   jax.ShapeDtypeStruct((bsz, S5_BLKS, S5_BLK_CH), F32),
                   jax.ShapeDtypeStruct((bsz, S5_BLKS, S5_BLK_CH), F32)],
        scratch_shapes=[rows, rows, carry, carry],
        compiler_params=_params("parallel", "arbitrary"),
        name="s5_scan_long",
    )(z3, h0r.reshape(bsz, S5_BLKS, S5_BLK_CH), h0i.reshape(bsz, S5_BLKS, S5_BLK_CH),
      pwr[0].reshape(S5_BLKS, S5_BLK_CH), pwi[0].reshape(S5_BLKS, S5_BLK_CH), wbr, wbi, wcr, wci, d)
    return y, hr.reshape(bsz, S5_GROUPS, S5_STATE), hi.reshape(bsz, S5_GROUPS, S5_STATE)


def _s5_block_weights(bbr_t, bbi_t, c_re, c_im):
    eye = jnp.eye(S5_BLK_GROUPS, dtype=F32)

    def wb(bt):
        b4 = bt.reshape(S5_GROUP, S5_BLKS, S5_BLK_GROUPS, S5_STATE)
        w = jnp.einsum("cbgp,hg->bhcgp", b4, eye)
        return w.reshape(S5_BLKS, LANES, S5_BLK_CH).astype(BF16)

    def wc(c):
        c4 = c.reshape(S5_BLKS, S5_BLK_GROUPS, S5_GROUP, S5_STATE)
        w = jnp.einsum("bgcp,hg->bhpgc", c4, eye)
        return w.reshape(S5_BLKS, S5_BLK_CH, LANES).astype(BF16)

    return wb(bbr_t), wb(bbi_t), wc(c_re), wc(c_im)


def _hg_lb_kernel(x_ref, o_ref):
    x = x_ref[...]
    e = jnp.exp(x - jnp.max(x, axis=0, keepdims=True))
    sm = e / jnp.sum(e, axis=0, keepdims=True)
    acc = sm[0:1]
    o_ref[0:1, :] = acc
    for l in range(1, x.shape[0]):
        acc = acc + sm[l:l + 1]
        o_ref[l:l + 1, :] = acc


def hg_lower_bounds(hg_lb):
    return pl.pallas_call(_hg_lb_kernel, out_shape=jax.ShapeDtypeStruct(hg_lb.shape, F32), name="hg_lb")(hg_lb)


def _cumsum_rows(x, n):
    row = lax.broadcasted_iota(jnp.int32, x.shape, 1)
    d = 1
    while d < n:
        x = x + jnp.where(row >= d, pltpu.roll(x, d, axis=1), 0.0)
        d *= 2
    return x


def _hgrn_kernel(q_ref, f_ref, i_ref, g_ref, s0_ref, lb_ref, nw_ref, o_ref, sf_ref, st_scr, *, nb, hh, chunk, n_sub):
    step = pl.program_id(2)

    def units(x):
        return jnp.concatenate([x[:, :, h * LANES:(h + 1) * LANES] for h in range(hh)], axis=0)

    lb = units(jnp.broadcast_to(lb_ref[...][None], (nb, 1, hh * LANES)))
    nw = nw_ref[...][None]

    @pl.when(step == 0)
    def _():
        for h in range(hh):
            for b in range(nb):
                st_scr[h * nb + b] = s0_ref[b, h].T

    trow = lax.broadcasted_iota(jnp.int32, (chunk, chunk), 0)
    tcol = lax.broadcasted_iota(jnp.int32, (chunk, chunk), 1)
    causal = (tcol <= trow)[None]
    st = st_scr[...]
    for c in range(n_sub):
        rows = slice(c * chunk, (c + 1) * chunk)
        q, f, v, g = (units(ref[:, rows, :]) for ref in (q_ref, f_ref, i_ref, g_ref))
        fg = lb + (1.0 - lb) * jax.nn.sigmoid(f)
        qh = jax.nn.silu(q)
        kh = 1.0 - fg
        bcum = _cumsum_rows(jnp.log(fg), chunk)
        btot = bcum[:, chunk - 1:chunk, :]
        q_in = (qh * jnp.exp(bcum)).astype(BF16)
        k_in = (kh * jnp.exp(-bcum)).astype(BF16)
        k_end = (kh * jnp.exp(btot - bcum)).astype(BF16)
        decay = jnp.exp(btot)
        vb = v.astype(BF16)
        att = jnp.einsum("utk,usk->uts", q_in, k_in, preferred_element_type=F32)
        att = jnp.where(causal, att, 0.0).astype(BF16)
        out = (jnp.einsum("utk,uvk->utv", q_in, st.astype(BF16), preferred_element_type=F32)
               + jnp.einsum("uts,usv->utv", att, vb, preferred_element_type=F32))
        st = st * decay + jnp.einsum("usv,usk->uvk", vb, k_end, preferred_element_type=F32)
        out = out * lax.rsqrt(jnp.mean(out * out, axis=-1, keepdims=True) + EPS) * nw
        out = (out * jax.nn.silu(g)).astype(o_ref.dtype)
        for h in range(hh):
            o_ref[:, rows, h * LANES:(h + 1) * LANES] = out[h * nb:(h + 1) * nb]
    st_scr[...] = st

    @pl.when(step == pl.num_programs(2) - 1)
    def _():
        for h in range(hh):
            for b in range(nb):
                sf_ref[b, h] = st[h * nb + b].T


def hgrn2(z3, s0, lb, norm_w, nb, hh):
    bsz, seq, _ = z3.shape
    chunk = min(HG_CHUNK, seq)
    n_sub = next(n for n in (3, 2, 1) if seq % (n * chunk) == 0)
    rows = n_sub * chunk
    assert bsz % nb == 0 and HG_HEADS % hh == 0
    wid = hh * LANES
    n_col = (HG_HEADS * HG_K) // wid

    def col(proj):
        return pl.BlockSpec((nb, rows, wid), lambda h, b, t, proj=proj: (b, t, proj * n_col + h))

    st_blk = pl.BlockSpec((nb, hh, HG_K, HG_V), lambda h, b, t: (b, h, 0, 0))
    out, sf = pl.pallas_call(
        functools.partial(_hgrn_kernel, nb=nb, hh=hh, chunk=chunk, n_sub=n_sub),
        grid=(HG_HEADS // hh, bsz // nb, seq // rows),
        in_specs=[col(1), col(2), col(3), col(4), st_blk,
                  pl.BlockSpec((1, wid), lambda h, b, t: (0, h)),
                  pl.BlockSpec((1, LANES), lambda h, b, t: (0, 0))],
        out_specs=[pl.BlockSpec((nb, rows, wid), lambda h, b, t: (b, t, h)), st_blk],
        out_shape=[jax.ShapeDtypeStruct((bsz, seq, HG_HEADS * HG_V), BF16),
                   jax.ShapeDtypeStruct((bsz, HG_HEADS, HG_K, HG_V), F32)],
        scratch_shapes=[pltpu.VMEM((hh * nb, HG_V, HG_K), F32)],
        compiler_params=_params("parallel", "parallel", "arbitrary"),
        name="hgrn2",
    )(z3, z3, z3, z3, s0, lb.reshape(1, HG_HEADS * HG_K), norm_w.reshape(1, HG_V))
    return out, sf


RW_MIXES = 6


def _norm_mix_kernel(h_ref, sh_ref, w_ref, mu_ref, *refs, tl):
    o_refs = refs[:RW_MIXES]
    last_ref, scr = refs[RW_MIXES:]
    x = h_ref[...]
    xn = x * lax.rsqrt(jnp.mean(x * x, axis=-1, keepdims=True) + EPS) * w_ref[...][None]

    @pl.when(pl.program_id(1) == 0)
    def _():
        scr[:, SUBLANES - 1:SUBLANES, :] = sh_ref[...]

    scr[:, SUBLANES:, :] = xn
    xx = scr[:, SUBLANES - 1:SUBLANES - 1 + tl, :] - xn
    for j, o_ref in enumerate(o_refs):
        o_ref[...] = (xn + xx * mu_ref[j:j + 1, :][None]).astype(o_ref.dtype)
    last = xn[:, tl - 1:tl, :]
    scr[:, SUBLANES - 1:SUBLANES, :] = last
    last_ref[...] = last


def norm_mix(h3, shift0, ln_w, mu, nb, tl):
    bsz, seq, d = h3.shape
    assert bsz % nb == 0 and seq % tl == 0 and tl % SUBLANES == 0
    blk = pl.BlockSpec((nb, tl, d), lambda b, t: (b, t, 0))
    row = pl.BlockSpec((nb, 1, d), lambda b, t: (b, 0, 0))
    outs = pl.pallas_call(
        functools.partial(_norm_mix_kernel, tl=tl),
        grid=(bsz // nb, seq // tl),
        in_specs=[blk, row, pl.BlockSpec((1, d), lambda b, t: (0, 0)), pl.BlockSpec((RW_MIXES, d), lambda b, t: (0, 0))],
        out_specs=[blk] * RW_MIXES + [row],
        out_shape=[jax.ShapeDtypeStruct((bsz, seq, d), BF16)] * RW_MIXES + [jax.ShapeDtypeStruct((bsz, 1, d), F32)],
        scratch_shapes=[pltpu.VMEM((nb, SUBLANES + tl, d), F32)],
        compiler_params=_params("parallel", "arbitrary"),
        name="norm_mix",
    )(h3, shift0.reshape(bsz, 1, d), ln_w.reshape(1, d), mu)
    return outs[:RW_MIXES], outs[RW_MIXES].reshape(bsz, d)


RW_PAIR = LANES // RW_HEAD
RW_DECAY_SCALE = math.exp(-0.5)
RW_SOLVE_BLOCK = 8


def _rwkv_kernel(r_ref, k_ref, v_ref, tw_ref, ta_ref, tg_ref, w2_ref, a2_ref, g2_ref, s0_ref,
                 w0_ref, a0_ref, kk_ref, ka_ref, rk_ref, lnw_ref, lnb_ref,
                 o_ref, sf_ref, s_scr, *, nb, hp, chunk):
    nu = hp * nb
    step = pl.program_id(2)
    lane = lax.broadcasted_iota(jnp.int32, (1, 1, LANES), 2)
    head1 = lane >= RW_HEAD

    def units(x):
        return jnp.concatenate([x[:, :, p * LANES:(p + 1) * LANES] for p in range(hp)], axis=0)

    def unit_rows(ref):
        return units(jnp.broadcast_to(ref[...][None], (nb, 1, hp * LANES)))

    w0, a0, k_k, k_a, r_k, ln_w, ln_b = (unit_rows(p) for p in
                                         (w0_ref, a0_ref, kk_ref, ka_ref, rk_ref, lnw_ref, lnb_ref))
    sq_row = lax.broadcasted_iota(jnp.int32, (LANES, LANES), 0) >= RW_HEAD
    sq_col = lax.broadcasted_iota(jnp.int32, (LANES, LANES), 1) >= RW_HEAD
    same_head = sq_row == sq_col
    ones_bd = same_head.astype(BF16)

    @pl.when(step == 0)
    def _():
        zero = jnp.zeros((nb, RW_HEAD, RW_HEAD), F32)
        for p in range(hp):
            top = jnp.concatenate([s0_ref[:, RW_PAIR * p], zero], axis=-1)
            bot = jnp.concatenate([zero, s0_ref[:, RW_PAIR * p + 1]], axis=-1)
            s_scr[p * nb:(p + 1) * nb] = jnp.concatenate([top, bot], axis=1)

    def bdot(spec, a, b):
        return jnp.einsum(spec, a.astype(BF16), b.astype(BF16), preferred_element_type=F32)

    def head_sum(x, two_pass=True):
        x2 = x.reshape(nu * chunk, LANES)
        hi = x2.astype(BF16)
        s = jnp.dot(hi, ones_bd, preferred_element_type=F32)
        if two_pass:
            lo = (x2 - hi.astype(F32)).astype(BF16)
            s = s + jnp.dot(lo, ones_bd, preferred_element_type=F32)
        return s.reshape(nu, chunk, LANES)

    def low_rank(t_ref, w_ref):
        t2 = t_ref[...].reshape(nb * chunk, t_ref.shape[-1])
        return units(jnp.dot(t2, w_ref[...], preferred_element_type=F32).reshape(nb, chunk, hp * LANES))

    srow = lax.broadcasted_iota(jnp.int32, (chunk, chunk), 0)
    scol = lax.broadcasted_iota(jnp.int32, (chunk, chunk), 1)
    tri = jnp.broadcast_to((scol <= srow).astype(BF16)[None], (nu, chunk, chunk))

    def tri_sum(x):
        hi = x.astype(BF16)
        lo = (x - hi.astype(F32)).astype(BF16)
        return (jnp.einsum("uts,usc->utc", tri, hi, preferred_element_type=F32)
                + jnp.einsum("uts,usc->utc", tri, lo, preferred_element_type=F32))

    def stack_heads(x):
        return jnp.concatenate([jnp.where(head1, 0.0, x), jnp.where(head1, x, 0.0)], axis=1).astype(BF16)

    r, k, v = (units(ref[...]) for ref in (r_ref, k_ref, v_ref))
    wl, al, g = low_rank(tw_ref, w2_ref), low_rank(ta_ref, a2_ref), low_rank(tg_ref, g2_ref)
    lw = (-RW_DECAY_SCALE) * jax.nn.sigmoid(w0 + wl)
    ag = jax.nn.sigmoid(a0 + al)
    kk = k * k_k
    kk = kk * jnp.minimum(lax.rsqrt(head_sum(kk * kk)), 1e12)
    k2 = k * (1.0 + (ag - 1.0) * k_a)
    cl = tri_sum(lw)
    e_pos = jnp.exp(cl)
    e_neg = jnp.exp(-cl)
    at = (-kk) * jnp.exp(cl - lw)
    bt = (kk * ag) * e_neg
    kt = k2 * e_neg
    rt = r * e_pos
    wc = e_pos[:, chunk - 1:chunk, :]
    trow = lax.broadcasted_iota(jnp.int32, (chunk, RW_PAIR * chunk), 0)
    tcol = lax.broadcasted_iota(jnp.int32, (chunk, RW_PAIR * chunk), 1)
    tcol = jnp.where(tcol >= chunk, tcol - chunk, tcol)
    strict = (tcol < trow)[None]
    incl = (tcol <= trow)[None]
    x2 = jnp.concatenate([at, rt], axis=1)
    pb = bdot("utc,usc->uts", x2, stack_heads(bt))
    pk = bdot("utc,usc->uts", x2, stack_heads(kt))
    lab = jnp.where(strict, pb[:, :chunk], 0.0)
    lak = jnp.where(strict, pk[:, :chunk], 0.0)
    arb = jnp.where(incl, pb[:, chunk:], 0.0)
    ark = jnp.where(incl, pk[:, chunk:], 0.0)
    v_bd = stack_heads(v)
    xa = at
    xv = bdot("uts,usc->utc", lak, v_bd)
    sub = min(RW_SOLVE_BLOCK, chunk)
    done_a, done_v = [], []
    for lo in range(0, chunk, sub):
        xa_i = xa[:, lo:lo + sub, :]
        xv_i = xv[:, lo:lo + sub, :]
        if lo:
            pad = jnp.zeros((nu, chunk - lo, LANES), F32)
            prev = jnp.concatenate([stack_heads(jnp.concatenate(done_a + [pad], axis=1)),
                                    stack_heads(jnp.concatenate(done_v + [pad], axis=1))], axis=-1)
            upd = bdot("uts,usc->utc", lab[:, lo:lo + sub, :], prev)
            xa_i = xa_i + upd[:, :, :LANES]
            xv_i = xv_i + upd[:, :, LANES:]
        l0 = lab[:, lo:lo + sub, lo:lo + sub]
        l1 = lab[:, lo:lo + sub, chunk + lo:chunk + lo + sub]
        for s in range(sub - 1):
            m = jnp.where(head1, l1[:, :, s:s + 1], l0[:, :, s:s + 1])
            xa_i = xa_i + m * xa_i[:, s:s + 1, :]
            xv_i = xv_i + m * xv_i[:, s:s + 1, :]
        done_a.append(xa_i)
        done_v.append(xv_i)
    ah = jnp.concatenate(done_a, axis=1)
    vh = jnp.concatenate(done_v, axis=1)
    both = bdot("uts,usc->utc", arb, jnp.concatenate([stack_heads(ah), stack_heads(vh)], axis=-1))
    rh = rt + both[:, :, :LANES]
    yh = both[:, :, LANES:] + bdot("uts,usc->utc", ark, v_bd)
    gp = jnp.where(same_head, bdot("utj,utk->ujk", ah, bt), 0.0)
    ht = jnp.where(same_head, bdot("utv,utk->uvk", jnp.concatenate([vh, v], axis=1),
                                   jnp.concatenate([bt, kt], axis=1)), 0.0)
    st = s_scr[...]
    y = bdot("utk,uvk->utv", rh, st) + yh
    st = (st + bdot("uvj,ujk->uvk", st, gp) + ht) * wc
    s_scr[...] = st
    inv_n = 1.0 / RW_HEAD
    yc = y - head_sum(y, two_pass=False) * inv_n
    var = head_sum(yc * yc, two_pass=False) * inv_n
    y = yc * lax.rsqrt(var + RW_GN_EPS) * ln_w + ln_b
    y = y + head_sum(r * k2 * r_k, two_pass=False) * v
    out = (y * g).astype(o_ref.dtype)
    for p in range(hp):
        o_ref[:, :, p * LANES:(p + 1) * LANES] = out[p * nb:(p + 1) * nb]

    @pl.when(step == pl.num_programs(2) - 1)
    def _():
        for p in range(hp):
            sf_ref[:, RW_PAIR * p] = st[p * nb:(p + 1) * nb, :RW_HEAD, :RW_HEAD]
            sf_ref[:, RW_PAIR * p + 1] = st[p * nb:(p + 1) * nb, RW_HEAD:, RW_HEAD:]


def _rwkv_chunk(seq):
    for c in (48, 32, 16, 8):
        if seq % c == 0:
            return c
    raise ValueError(seq)


def rwkv7(r, k, v, low, low_w, s0, w0, a0, k_k, k_a, r_k, ln_w, ln_b, nb, hp):
    bsz, seq, d = r.shape
    chunk = _rwkv_chunk(seq)
    heads = hp * RW_PAIR
    assert bsz % nb == 0 and RW_HEADS % heads == 0
    seq_blk = pl.BlockSpec((nb, chunk, hp * LANES), lambda h, b, t: (b, t, h))
    st_blk = pl.BlockSpec((nb, heads, RW_HEAD, RW_HEAD), lambda h, b, t: (b, h, 0, 0))
    vec = pl.BlockSpec((1, hp * LANES), lambda h, b, t: (0, h))
    low_blk = [pl.BlockSpec((nb, chunk, x.shape[-1]), lambda h, b, t: (b, t, 0)) for x in low]
    low_w_blk = [pl.BlockSpec((w.shape[0], hp * LANES), lambda h, b, t: (0, h)) for w in low_w]
    out, sf = pl.pallas_call(
        functools.partial(_rwkv_kernel, nb=nb, hp=hp, chunk=chunk),
        grid=(RW_HEADS // heads, bsz // nb, seq // chunk),
        in_specs=[seq_blk] * 3 + low_blk + low_w_blk + [st_blk] + [vec] * 7,
        out_specs=[seq_blk, st_blk],
        out_shape=[jax.ShapeDtypeStruct((bsz, seq, d), BF16),
                   jax.ShapeDtypeStruct((bsz, RW_HEADS, RW_HEAD, RW_HEAD), F32)],
        scratch_shapes=[pltpu.VMEM((hp * nb, LANES, LANES), F32)],
        compiler_params=_params("parallel", "parallel", "arbitrary"),
        name="rwkv7",
    )(r, k, v, *low, *low_w, s0, *(p.reshape(1, d) for p in (w0, a0, k_k, k_a, r_k, ln_w, ln_b)))
    return out, sf


def _ffn_in_kernel(x_ref, wa_ref, wv_ref, e_ref, cw_ref, cb_ref, o_ref, st_ref, scr, *, nb, seq, sb, sr):
    tn = wa_ref.shape[1]
    cw = cw_ref[...]
    cb = cb_ref[...][None]
    scr[:, SUBLANES - (CONV_W - 1):SUBLANES, :] = e_ref[...]
    for b0 in range(0, nb, sb):
        for r0 in range(0, seq, sr):
            lo = b0 * seq + r0
            x = x_ref[lo:lo + sb * sr, :]
            a = jnp.dot(x, wa_ref[...], preferred_element_type=F32).reshape(sb, sr, tn)
            v = jnp.dot(x, wv_ref[...], preferred_element_type=F32).reshape(sb, sr, tn)
            scr[b0:b0 + sb, SUBLANES + r0:SUBLANES + r0 + sr, :] = a
            c = cb + cw[CONV_W - 1:CONV_W][None] * a
            for j in range(CONV_W - 1):
                first = SUBLANES + r0 - (CONV_W - 1 - j)
                c = c + cw[j:j + 1][None] * scr[b0:b0 + sb, first:first + sr, :]
            o_ref[lo:lo + sb * sr, :] = (jax.nn.gelu(c) * v).reshape(sb * sr, tn).astype(o_ref.dtype)
    st_ref[...] = scr[:, SUBLANES + seq - (CONV_W - 1):SUBLANES + seq, :]


def ffn_in(xb, conv0, w_in, layer, conv_w, conv_b, bsz, seq, nb, tn, sub):
    t, d = xb.shape
    sb, sr = sub
    assert t == bsz * seq and bsz % nb == 0 and D_FF % tn == 0
    assert nb % sb == 0 and seq % sr == 0 and sr % SUBLANES == 0 and (sr == seq or nb == sb == 1)
    col = lambda i, j: (0, j)
    n_col = D_FF // tn
    out, st = pl.pallas_call(
        functools.partial(_ffn_in_kernel, nb=nb, seq=seq, sb=sb, sr=sr),
        grid=(bsz // nb, n_col),
        in_specs=[pl.BlockSpec((nb * seq, d), lambda i, j: (i, 0)),
                  pl.BlockSpec((None, d, tn), lambda i, j: (layer, 0, j)),
                  pl.BlockSpec((None, d, tn), lambda i, j: (layer, 0, j + n_col)),
                  pl.BlockSpec((nb, CONV_W - 1, tn), lambda i, j: (i, 0, j)),
                  pl.BlockSpec((CONV_W, tn), col), pl.BlockSpec((1, tn), col)],
        out_specs=[pl.BlockSpec((nb * seq, tn), lambda i, j: (i, j)),
                   pl.BlockSpec((nb, CONV_W - 1, tn), lambda i, j: (i, 0, j))],
        out_shape=[jax.ShapeDtypeStruct((t, D_FF), BF16),
                   jax.ShapeDtypeStruct((bsz, CONV_W - 1, D_FF), F32)],
        scratch_shapes=[pltpu.VMEM((nb, SUBLANES + seq, tn), F32)],
        compiler_params=_params("parallel", "parallel"),
        name="ffn_in",
    )(xb, w_in, w_in, conv0, conv_w, conv_b.reshape(1, D_FF))
    return out, st


FFN_DOWN_K_STEPS = 2


def _channel_mixer(h, conv0, layer, p, cfg, bsz, seq):
    (xb,) = rmsnorm(h, p["ln_ffn"][layer], (BF16,))
    gated, n_cv = ffn_in(xb, conv0, p["ffn_w_in"], layer, p["ffn_conv_w"][layer], p["ffn_conv_b"][layer],
                         bsz, seq, cfg["ffn_nb"], cfg["ffn_tn"], cfg["ffn_sub"])
    return matmul_residual_split(gated, p["ffn_w_down"], layer, h, FFN_DOWN_K_STEPS), n_cv


def _trunk(x3, s5r, s5i, hg, rw, sh, cv, p, cfg):
    bsz, seq, d = x3.shape
    seq += cfg["front"]
    t = bsz * seq

    if cfg["front"]:
        h, xb = embed_norm(x3, p["meta"], p["ln_mix"][0])
    else:
        h = x3.reshape(t, d)
        (xb,) = rmsnorm(h, p["ln_mix"][0], (BF16,))
    z = matmul([xb], [p["ev_w_in"]], F32).reshape(bsz, seq, EVEN_IN)
    if cfg["s5_tl"]:
        ys5, n_s5r, n_s5i = s5_scan_long(z, s5r[0], s5i[0], *p["s5"], tl=cfg["s5_tl"])
    else:
        ys5, n_s5r, n_s5i = s5_scan(z, s5r[0], s5i[0], *p["s5"], nb=cfg["s5_nb"])
    ys5 = ys5.reshape(t, S5_WIDTH)
    ya = matmul([ys5], [p["s5_w_glu"]], BF16, epilogue="glu", extra=ys5)
    yb, n_hg = hgrn2(z, hg[0], p["hg_lb"], p["hg_norm_w"], nb=cfg["hg_nb"], hh=cfg["hg_hh"])
    h = matmul([ya, yb.reshape(t, -1)], [(p["ev_w_out"], 0, 0), (p["ev_w_out"], 0, 1)], F32,
               epilogue="residual", extra=h)
    h, n_cv0 = _channel_mixer(h, cv[0], 0, p, cfg, bsz, seq)

    mixes, n_sh = norm_mix(h.reshape(bsz, seq, d), sh[0], p["ln_mix"][1], p["rw_mu"], *cfg["mix_blk"])
    xr, xw, xk, xv, xa, xg = (m.reshape(t, d) for m in mixes)
    r = matmul([xr], [p["rw_w_r"]], F32)
    k = matmul([xk], [p["rw_w_k"]], F32)
    v = matmul([xv], [p["rw_w_v"]], F32)
    low = (matmul([xw], [p["rw_w1"]], BF16, act="tanh"), matmul([xa], [p["rw_a1"]], BF16),
           matmul([xg], [p["rw_g1"]], BF16, act="sigmoid"))
    as3 = lambda a: a.reshape(bsz, seq, a.shape[-1])
    yo, n_rw = rwkv7(as3(r), as3(k), as3(v), [as3(x) for x in low], (p["rw_w2"], p["rw_a2"], p["rw_g2"]),
                     rw[0], *p["rw_vec"], nb=cfg["rw_nb"], hp=cfg["rw_hp"])
    h = matmul([yo.reshape(t, d)], [p["rw_w_o"]], F32, epilogue="residual", extra=h)
    h, n_cv1 = _channel_mixer(h, cv[1], 1, p, cfg, bsz, seq)

    (y,) = rmsnorm(h, p["ln_final"], (F32,), rows=(bsz, seq, cfg["front"]) if cfg["front"] else None)
    return (y.reshape(bsz, seq - cfg["front"], d), n_s5r[None], n_s5i[None], n_hg[None], n_rw[None], n_sh[None],
            jnp.stack([n_cv0, n_cv1]))


PROMPT_CFG = dict(front=N_META, s5_tl=344, s5_nb=None, hg_nb=4, hg_hh=8, mix_blk=(1, 344), rw_nb=4, rw_hp=16, ffn_nb=1, ffn_tn=512, ffn_sub=(1, 688))
SAMPLE_CFG = dict(front=0, s5_tl=None, s5_nb=32, hg_nb=4, hg_hh=8, mix_blk=(32, 8), rw_nb=16, rw_hp=4, ffn_nb=128, ffn_tn=512, ffn_sub=(32, 8))


def kernel(x_prompt, x_sample, state_s5_re, state_s5_im, state_hgrn, state_rwkv, state_shift, state_conv, meta_tokens, ln_mix, ln_ffn, ln_final, ev_w_in, ev_w_out, s5_lam_re, s5_lam_im, s5_log_step, s5_b_re, s5_b_im, s5_c_re, s5_c_im, s5_d, s5_w_glu, hg_lb, hg_norm_w, rw_mu, rw_w0, rw_w1, rw_w2, rw_a0, rw_a1, rw_a2, rw_g1, rw_g2, rw_k_k, rw_k_a, rw_r_k, rw_w_r, rw_w_k, rw_w_v, rw_w_o, rw_ln_w, rw_ln_b, ffn_w_in, ffn_conv_w, ffn_conv_b, ffn_w_down):
    bf = lambda w: w.astype(BF16)
    lb_all = hg_lower_bounds(hg_lb)
    pwr, pwi, bbr_t, bbi_t = s5_prep(s5_lam_re[0], s5_lam_im[0], s5_log_step[0], s5_b_re[0], s5_b_im[0])
    wbr, wbi, wcr, wci = _s5_block_weights(bbr_t, bbi_t, s5_c_re[0], s5_c_im[0])
    p = {
        "meta": meta_tokens, "ln_mix": ln_mix, "ln_ffn": ln_ffn, "ln_final": ln_final,
        "ev_w_in": ev_w_in[0],
        "ev_w_out": ev_w_out,
        "s5": (pwr, pwi, wbr, wbi, wcr, wci, s5_d[0].reshape(1, S5_WIDTH)),
        "s5_w_glu": s5_w_glu[0],
        "hg_lb": lb_all[0], "hg_norm_w": hg_norm_w[0],
        "rw_mu": rw_mu[0],
        "rw_w1": rw_w1[0], "rw_w2": bf(rw_w2[0]), "rw_a1": rw_a1[0], "rw_a2": bf(rw_a2[0]),
        "rw_g1": rw_g1[0], "rw_g2": bf(rw_g2[0]),
        "rw_w_r": rw_w_r[0], "rw_w_k": rw_w_k[0], "rw_w_v": rw_w_v[0], "rw_w_o": rw_w_o[0],
        "rw_vec": (rw_w0[0], rw_a0[0], rw_k_k[0], rw_k_a[0], rw_r_k[0].reshape(D_MODEL), rw_ln_w[0], rw_ln_b[0]),
        "ffn_w_in": bf(ffn_w_in), "ffn_conv_w": ffn_conv_w, "ffn_conv_b": ffn_conv_b,
        "ffn_w_down": bf(ffn_w_down),
    }

    bsz = x_prompt.shape[0]
    zeros = lambda *s: jnp.zeros(s, F32)
    outs_p = _trunk(x_prompt,
                    zeros(1, bsz, S5_GROUPS, S5_STATE), zeros(1, bsz, S5_GROUPS, S5_STATE),
                    zeros(1, bsz, HG_HEADS, HG_K, HG_V), zeros(1, bsz, RW_HEADS, RW_HEAD, RW_HEAD),
                    zeros(1, bsz, D_MODEL), zeros(2, bsz, CONV_W - 1, D_FF), p, PROMPT_CFG)
    outs_s = _trunk(x_sample, state_s5_re, state_s5_im, state_hgrn, state_rwkv, state_shift, state_conv,
                    p, SAMPLE_CFG)
    return tuple(outs_p[:1]) + tuple(outs_s[:1]) + tuple(outs_p[1:]) + tuple(outs_s[1:])
```

```python
import functools
import math

import jax
import jax.numpy as jnp
from jax import lax
from jax.experimental import pallas as pl
from jax.experimental.pallas import tpu as pltpu

F32 = jnp.float32
BF16 = jnp.bfloat16

D_MODEL = 2048
N_META = 16
EPS = 1e-6
S5_WIDTH = 1024
S5_GROUP = 16
S5_GROUPS = 64
S5_STATE = 64
S5_CH = S5_GROUPS * S5_STATE
HG_HEADS = 8
HG_K = 128
HG_V = 128
HG_CHUNK = 16
EVEN_IN = 5120
RW_HEAD = 64
RW_HEADS = 32
RW_GN_EPS = 64e-5
D_FF = 5632
CONV_W = 3

LANES = 128
SUBLANES = 8
VMEM_LIMIT = 56 * 1024 * 1024


def _params(*sem):
    return pltpu.CompilerParams(dimension_semantics=sem, vmem_limit_bytes=VMEM_LIMIT)


def _row_tile(t, cap=1024):
    best = None
    for d in range(16, min(t, cap) + 1, 16):
        if t % d == 0:
            best = d
    assert best is not None, t
    return best


def _rms_kernel(x_ref, w_ref, *o_refs):
    x = x_ref[...]
    y = x * lax.rsqrt(jnp.mean(x * x, axis=-1, keepdims=True) + EPS) * w_ref[...]
    for o_ref in o_refs:
        o_ref[...] = y.astype(o_ref.dtype)


def rmsnorm(x, w, dtypes, rows=None):
    t, d = x.shape
    if rows is None:
        tm = _row_tile(t)
        n_out = t
        grid = (t // tm,)
        in_spec = pl.BlockSpec((tm, d), lambda i: (i, 0))
        out_spec = in_spec
        vec = pl.BlockSpec((1, d), lambda i: (0, 0))
    else:
        bsz, seq, front = rows
        keep = seq - front
        assert t == bsz * seq and front % 16 == 0
        tm = _row_tile(keep)
        per = keep // tm
        n_out = bsz * keep
        grid = (bsz, per)
        in_spec = pl.BlockSpec((pl.Element(tm), pl.Element(d)),
                               lambda b, i: (pl.multiple_of(b * seq + front + i * tm, 16), 0))
        out_spec = pl.BlockSpec((tm, d), lambda b, i: (b * per + i, 0))
        vec = pl.BlockSpec((1, d), lambda b, i: (0, 0))
    outs = pl.pallas_call(
        _rms_kernel,
        grid=grid,
        in_specs=[in_spec, vec],
        out_specs=[out_spec for _ in dtypes],
        out_shape=[jax.ShapeDtypeStruct((n_out, d), dt) for dt in dtypes],
        compiler_params=_params(*(["parallel"] * len(grid))),
        name="rmsnorm",
    )(x, w.reshape(1, d))
    return outs


def _embed_norm_kernel(x_ref, m_ref, w_ref, h_ref, xb_ref, *, front):
    x = x_ref[...]
    first = pl.program_id(1) == 0
    body = jnp.where(first, pltpu.roll(x, front, axis=0), x)
    head = jnp.where(first, m_ref[...], x[:front])
    rows = jnp.concatenate([head, body[front:]], axis=0)
    h_ref[...] = rows
    y = rows * lax.rsqrt(jnp.mean(rows * rows, axis=-1, keepdims=True) + EPS) * w_ref[...]
    xb_ref[...] = y.astype(xb_ref.dtype)


def embed_norm(x3, meta, w):
    bsz, seq, d = x3.shape
    front = meta.shape[0]
    total = seq + front
    tm = _row_tile(total)
    per = total // tm
    assert front % 16 == 0 and tm > front
    blk = pl.BlockSpec((tm, d), lambda b, j: (b * per + j, 0))
    return pl.pallas_call(
        functools.partial(_embed_norm_kernel, front=front),
        grid=(bsz, per),
        in_specs=[pl.BlockSpec((pl.Element(tm), pl.Element(d)),
                               lambda b, j: (pl.multiple_of(b * seq + jnp.maximum(j * tm - front, 0), 16), 0)),
                  pl.BlockSpec((front, d), lambda b, j: (0, 0)),
                  pl.BlockSpec((1, d), lambda b, j: (0, 0))],
        out_specs=[blk, blk],
        out_shape=[jax.ShapeDtypeStruct((bsz * total, d), F32), jax.ShapeDtypeStruct((bsz * total, d), BF16)],
        compiler_params=_params("parallel", "parallel"),
        name="embed_norm",
    )(x3.reshape(bsz * seq, d), meta, w.reshape(1, d))


def _act(x, act):
    if act == "tanh":
        return jnp.tanh(x)
    if act == "sigmoid":
        return jax.nn.sigmoid(x)
    assert act is None
    return x


def _mm_kernel(*refs, n_a, act, epilogue, cast_w):
    a_refs = refs[:n_a]
    w_refs = refs[n_a:2 * n_a]
    rest = refs[2 * n_a:]
    if cast_w:
        rest, wb_refs = rest[:-n_a], rest[-n_a:]

        @pl.when(pl.program_id(1) == 0)
        def _():
            for w_ref, wb_ref in zip(w_refs, wb_refs):
                wb_ref[...] = w_ref[...].astype(BF16)

        w_refs = wb_refs
    o_ref = rest[-1]
    acc = jnp.dot(a_refs[0][...].astype(BF16), w_refs[0][...], preferred_element_type=F32)
    for a_ref, w_ref in zip(a_refs[1:], w_refs[1:]):
        acc = acc + jnp.dot(a_ref[...].astype(BF16), w_ref[...], preferred_element_type=F32)
    acc = _act(acc, act)
    if epilogue == "residual":
        acc = rest[0][...] + acc
    elif epilogue == "glu":
        acc = rest[0][...] * jax.nn.sigmoid(acc)
    o_ref[...] = acc.astype(o_ref.dtype)


MM_VMEM_BUDGET = 40 * 1024 * 1024
MM_CAST_VMEM_BUDGET = 48 * 1024 * 1024
MM_MAX_A_BLOCK = 6 * 1024 * 1024
MXU_WIDTH = 256


def _mm_tiles(t, a_row_bytes, w_col_bytes, n, out_bytes, has_extra, budget=MM_VMEM_BUDGET):
    rows = [d for d in range(16, t + 1, 16) if t % d == 0 and d * a_row_bytes <= MM_MAX_A_BLOCK]
    cols = [d for d in range(LANES, n + 1, LANES) if n % d == 0] or [n]
    best, best_score = None, -1.0
    for tm in rows:
        for tn in cols:
            est = 2 * tm * a_row_bytes + w_col_bytes * tn + tm * tn * (2 * out_bytes + 4 + (8 if has_extra else 0))
            if est > budget:
                continue
            score = tm * tn * (1.0 if tn % MXU_WIDTH == 0 else 0.8)
            if score > best_score:
                best, best_score = (tm, tn), score
    assert best is not None, (t, a_row_bytes, w_col_bytes, n)
    return best


def matmul(a_list, w_list, out_dtype, act=None, epilogue=None, extra=None):
    t = a_list[0].shape[0]
    w0 = w_list[0][0] if isinstance(w_list[0], tuple) else w_list[0]
    n = w0.shape[-1]
    cast_w = w0.dtype == F32
    k_rows = sum(a.shape[1] for a in a_list)
    a_row_bytes = sum(a.shape[1] * a.dtype.itemsize for a in a_list)
    w_col_bytes = k_rows * (2 * 4 + 2 if cast_w else 2 * 2)
    tm, tn = _mm_tiles(t, a_row_bytes, w_col_bytes, n, jnp.dtype(out_dtype).itemsize, epilogue is not None,
                       budget=MM_CAST_VMEM_BUDGET if cast_w else MM_VMEM_BUDGET)
    rc = (lambda f: lambda j, i: f(i, j)) if cast_w else (lambda f: f)
    in_specs = [pl.BlockSpec((tm, a.shape[1]), rc(lambda i, j: (i, 0))) for a in a_list]
    args = list(a_list)
    for a, w in zip(a_list, w_list):
        if isinstance(w, tuple):
            w, layer, kblk = w
            in_specs.append(pl.BlockSpec((None, a.shape[1], tn),
                                         rc(lambda i, j, layer=layer, kblk=kblk: (layer, kblk, j))))
        else:
            in_specs.append(pl.BlockSpec((w.shape[0], tn), rc(lambda i, j: (0, j))))
        args.append(w)
    if epilogue is not None:
        in_specs.append(pl.BlockSpec((tm, tn), rc(lambda i, j: (i, j))))
        args.append(extra)
    return pl.pallas_call(
        functools.partial(_mm_kernel, n_a=len(a_list), act=act, epilogue=epilogue, cast_w=cast_w),
        grid=(n // tn, t // tm) if cast_w else (t // tm, n // tn),
        in_specs=in_specs,
        out_specs=pl.BlockSpec((tm, tn), rc(lambda i, j: (i, j))),
        out_shape=jax.ShapeDtypeStruct((t, n), out_dtype),
        scratch_shapes=[pltpu.VMEM((a.shape[1], tn), BF16) for a in a_list] if cast_w else [],
        compiler_params=_params("parallel", "arbitrary" if cast_w else "parallel"),
        name="matmul",
    )(*args)


def _mm_resid_split_kernel(a_ref, w_ref, res_ref, o_ref):
    part = jnp.dot(a_ref[...], w_ref[...], preferred_element_type=F32)

    @pl.when(pl.program_id(2) == 0)
    def _():
        o_ref[...] = res_ref[...] + part

    @pl.when(pl.program_id(2) > 0)
    def _():
        o_ref[...] += part


def matmul_residual_split(a, w, layer, res, k_steps):
    t, k = a.shape
    n = w.shape[2]
    tk = k // k_steps
    assert k % k_steps == 0 and tk % LANES == 0
    tm, tn = _mm_tiles(t, 2 * tk, 4 * tk, n, 4, True)
    return pl.pallas_call(
        _mm_resid_split_kernel,
        grid=(t // tm, n // tn, k_steps),
        in_specs=[pl.BlockSpec((tm, tk), lambda i, j, s: (i, s)),
                  pl.BlockSpec((None, tk, tn), lambda i, j, s: (layer, s, j)),
                  pl.BlockSpec((tm, tn), lambda i, j, s: (i, j))],
        out_specs=pl.BlockSpec((tm, tn), lambda i, j, s: (i, j)),
        out_shape=jax.ShapeDtypeStruct((t, n), F32),
        compiler_params=_params("parallel", "parallel", "arbitrary"),
        name="matmul_split",
    )(a, w, res)


def _s5_prep_kernel(lr_ref, li_ref, ls_ref, brt_ref, bit_ref, pwr_ref, pwi_ref, bbr_ref, bbi_ref):
    lr = jnp.minimum(lr_ref[...], -1e-4)
    li = li_ref[...]
    dt = jnp.exp(ls_ref[...])
    n = lax.broadcasted_iota(jnp.int32, (SUBLANES, S5_CH), 0).astype(F32) + 1.0
    mag = jnp.exp(n * (lr * dt))
    ang = n * (li * dt)
    pwr = mag * jnp.cos(ang)
    pwi = mag * jnp.sin(ang)
    pwr_ref[...] = pwr
    pwi_ref[...] = pwi
    ar = pwr[0:1]
    ai = pwi[0:1]
    den = lr * lr + li * li
    zr = ((ar - 1.0) * lr + ai * li) / den
    zi = (ai * lr - (ar - 1.0) * li) / den
    br = brt_ref[...]
    bi = bit_ref[...]
    bbr_ref[...] = zr * br - zi * bi
    bbi_ref[...] = zr * bi + zi * br


def s5_prep(lam_re, lam_im, log_step, b_re, b_im):
    lr = lam_re.reshape(1, S5_CH)
    li = lam_im.reshape(1, S5_CH)
    ls = jnp.broadcast_to(log_step[:, None], (S5_GROUPS, S5_STATE)).reshape(1, S5_CH)
    brt = b_re.reshape(S5_CH, S5_GROUP).T
    bit = b_im.reshape(S5_CH, S5_GROUP).T
    return pl.pallas_call(
        _s5_prep_kernel,
        out_shape=[jax.ShapeDtypeStruct((SUBLANES, S5_CH), F32)] * 2
        + [jax.ShapeDtypeStruct((S5_GROUP, S5_CH), F32)] * 2,
        name="s5_prep",
    )(lr, li, ls, brt, bit)


S5_BLK_GROUPS = LANES // S5_GROUP
S5_BLKS = S5_WIDTH // LANES
S5_BLK_CH = S5_BLK_GROUPS * S5_STATE


def _cmul_add(xr, xi, mr, mi, sr, si):
    return xr + mr * sr - mi * si, xi + mr * si + mi * sr


def _s5_kernel(u_ref, h0r_ref, h0i_ref, pwr_ref, pwi_ref, wbr_ref, wbi_ref, wcr_ref, wci_ref, d_ref,
               y_ref, hr_ref, hi_ref, xr_scr, xi_scr, *, nb, seq):
    u2 = u_ref[...].reshape(nb * seq, LANES)
    ub = u2.astype(BF16)
    xr_scr[...] = jnp.dot(ub, wbr_ref[0], preferred_element_type=F32).reshape(nb, seq, S5_BLK_CH)
    xi_scr[...] = jnp.dot(ub, wbi_ref[0], preferred_element_type=F32).reshape(nb, seq, S5_BLK_CH)

    pwr = pwr_ref[...]
    pwi = pwi_ref[...]
    row = lax.broadcasted_iota(jnp.int32, (SUBLANES, S5_BLK_CH), 0)
    steps = []
    for d in (1, 2, 4):
        keep = row >= d
        steps.append((d, jnp.where(keep, pwr[d - 1:d], 0.0)[None], jnp.where(keep, pwi[d - 1:d], 0.0)[None]))
    pr = pwr[None]
    pi = pwi[None]

    def tile(i, carry):
        cr, ci = carry
        o = pl.multiple_of(i * SUBLANES, SUBLANES)
        xr = xr_scr[:, pl.ds(o, SUBLANES), :]
        xi = xi_scr[:, pl.ds(o, SUBLANES), :]
        for d, mr, mi in steps:
            sr = pltpu.roll(xr, d, axis=1)
            si = pltpu.roll(xi, d, axis=1)
            xr, xi = _cmul_add(xr, xi, mr, mi, sr, si)
        xr, xi = _cmul_add(xr, xi, pr, pi, cr, ci)
        xr_scr[:, pl.ds(o, SUBLANES), :] = xr
        xi_scr[:, pl.ds(o, SUBLANES), :] = xi
        return xr[:, SUBLANES - 1:SUBLANES, :], xi[:, SUBLANES - 1:SUBLANES, :]

    hr, hi = lax.fori_loop(0, seq // SUBLANES, tile, (h0r_ref[...], h0i_ref[...]))
    hr_ref[...] = hr
    hi_ref[...] = hi

    xr = xr_scr[...].reshape(nb * seq, S5_BLK_CH).astype(BF16)
    xi = xi_scr[...].reshape(nb * seq, S5_BLK_CH).astype(BF16)
    y = (jnp.dot(xr, wcr_ref[0], preferred_element_type=F32)
         - jnp.dot(xi, wci_ref[0], preferred_element_type=F32)
         + d_ref[...] * u2)
    y_ref[...] = jax.nn.gelu(y).reshape(nb, seq, LANES)


def s5_scan(z3, h0r, h0i, pwr, pwi, wbr, wbi, wcr, wci, d, nb):
    bsz, seq, _ = z3.shape
    assert seq % SUBLANES == 0 and bsz % nb == 0
    seq_blk = pl.BlockSpec((nb, seq, LANES), lambda b, k: (b, 0, k))
    st_blk = pl.BlockSpec((nb, 1, S5_BLK_CH), lambda b, k: (b, 0, k))
    pw_blk = pl.BlockSpec((SUBLANES, S5_BLK_CH), lambda b, k: (0, k))
    wb_blk = pl.BlockSpec((1, LANES, S5_BLK_CH), lambda b, k: (k, 0, 0))
    wc_blk = pl.BlockSpec((1, S5_BLK_CH, LANES), lambda b, k: (k, 0, 0))
    y, hr, hi = pl.pallas_call(
        functools.partial(_s5_kernel, nb=nb, seq=seq),
        grid=(bsz // nb, S5_BLKS),
        in_specs=[seq_blk, st_blk, st_blk, pw_blk, pw_blk, wb_blk, wb_blk, wc_blk, wc_blk,
                  pl.BlockSpec((1, LANES), lambda b, k: (0, k))],
        out_specs=[seq_blk, st_blk, st_blk],
        out_shape=[jax.ShapeDtypeStruct((bsz, seq, S5_WIDTH), F32),
                   jax.ShapeDtypeStruct((bsz, 1, S5_CH), F32),
                   jax.ShapeDtypeStruct((bsz, 1, S5_CH), F32)],
        scratch_shapes=[pltpu.VMEM((nb, seq, S5_BLK_CH), F32), pltpu.VMEM((nb, seq, S5_BLK_CH), F32)],
        compiler_params=_params("parallel", "parallel"),
        name="s5_scan",
    )(z3, h0r.reshape(bsz, 1, S5_CH), h0i.reshape(bsz, 1, S5_CH), pwr, pwi, wbr, wbi, wcr, wci, d)
    return y, hr.reshape(bsz, S5_GROUPS, S5_STATE), hi.reshape(bsz, S5_GROUPS, S5_STATE)


S5_LANE_TILES = S5_BLK_CH // LANES


def _s5_long_kernel(u_ref, h0r_ref, h0i_ref, ar_ref, ai_ref, wbr_ref, wbi_ref, wcr_ref, wci_ref, d_ref,
                    y_ref, hr_ref, hi_ref, xr_scr, xi_scr, cr_scr, ci_scr, *, tl):
    step = pl.program_id(1)

    @pl.when(step == 0)
    def _():
        cr_scr[...] = h0r_ref[0]
        ci_scr[...] = h0i_ref[0]

    for k in range(S5_BLKS):
        ub = u_ref[0, :, k * LANES:(k + 1) * LANES].astype(BF16)
        bur = jnp.dot(ub, wbr_ref[k], preferred_element_type=F32)
        bui = jnp.dot(ub, wbi_ref[k], preferred_element_type=F32)
        for j in range(S5_LANE_TILES):
            xr_scr[j, k * tl:(k + 1) * tl, :] = bur[:, j * LANES:(j + 1) * LANES]
            xi_scr[j, k * tl:(k + 1) * tl, :] = bui[:, j * LANES:(j + 1) * LANES]

    ar = [ar_ref[:, j * LANES:(j + 1) * LANES] for j in range(S5_LANE_TILES)]
    ai = [ai_ref[:, j * LANES:(j + 1) * LANES] for j in range(S5_LANE_TILES)]

    def token(t, carry):
        cr, ci = carry
        nr, ni = [], []
        for j in range(S5_LANE_TILES):
            rows = pl.ds(t, S5_BLKS, stride=tl)
            xr, xi = _cmul_add(xr_scr[j, rows, :], xi_scr[j, rows, :], ar[j], ai[j], cr[j], ci[j])
            xr_scr[j, rows, :] = xr
            xi_scr[j, rows, :] = xi
            nr.append(xr)
            ni.append(xi)
        return tuple(nr), tuple(ni)

    init = (tuple(cr_scr[:, j * LANES:(j + 1) * LANES] for j in range(S5_LANE_TILES)),
            tuple(ci_scr[:, j * LANES:(j + 1) * LANES] for j in range(S5_LANE_TILES)))
    cr, ci = lax.fori_loop(0, tl, token, init, unroll=8)
    cr = jnp.concatenate(cr, axis=-1)
    ci = jnp.concatenate(ci, axis=-1)
    cr_scr[...] = cr
    ci_scr[...] = ci
    hr_ref[0] = cr
    hi_ref[0] = ci

    for k in range(S5_BLKS):
        xr = jnp.concatenate([xr_scr[j, k * tl:(k + 1) * tl, :] for j in range(S5_LANE_TILES)], axis=-1)
        xi = jnp.concatenate([xi_scr[j, k * tl:(k + 1) * tl, :] for j in range(S5_LANE_TILES)], axis=-1)
        u = u_ref[0, :, k * LANES:(k + 1) * LANES]
        y = (jnp.dot(xr.astype(BF16), wcr_ref[k], preferred_element_type=F32)
             - jnp.dot(xi.astype(BF16), wci_ref[k], preferred_element_type=F32)
             + d_ref[:, k * LANES:(k + 1) * LANES] * u)
        y_ref[0, :, k * LANES:(k + 1) * LANES] = jax.nn.gelu(y)


def s5_scan_long(z3, h0r, h0i, pwr, pwi, wbr, wbi, wcr, wci, d, tl):
    bsz, seq, _ = z3.shape
    assert seq % tl == 0 and tl % SUBLANES == 0
    seq_blk = pl.BlockSpec((1, tl, S5_WIDTH), lambda b, t: (b, t, 0))
    st_blk = pl.BlockSpec((1, S5_BLKS, S5_BLK_CH), lambda b, t: (b, 0, 0))
    lam_blk = pl.BlockSpec((S5_BLKS, S5_BLK_CH), lambda b, t: (0, 0))
    wb_blk = pl.BlockSpec((S5_BLKS, LANES, S5_BLK_CH), lambda b, t: (0, 0, 0))
    wc_blk = pl.BlockSpec((S5_BLKS, S5_BLK_CH, LANES), lambda b, t: (0, 0, 0))
    rows = pltpu.VMEM((S5_LANE_TILES, S5_BLKS * tl, LANES), F32)
    carry = pltpu.VMEM((S5_BLKS, S5_BLK_CH), F32)
    y, hr, hi = pl.pallas_call(
        functools.partial(_s5_long_kernel, tl=tl),
        grid=(bsz, seq // tl),
        in_specs=[seq_blk, st_blk, st_blk, lam_blk, lam_blk, wb_blk, wb_blk, wc_blk, wc_blk,
                  pl.BlockSpec((1, S5_WIDTH), lambda b, t: (0, 0))],
        out_specs=[seq_blk, st_blk, st_blk],
        out_shape=[jax.ShapeDtypeStruct((bsz, seq, S5_WIDTH), F32),
                   jax.ShapeDtypeStruct((bsz, S5_BLKS, S5_BLK_CH), F32),
                   jax.ShapeDtypeStruct((bsz, S5_BLKS, S5_BLK_CH), F32)],
        scratch_shapes=[rows, rows, carry, carry],
        compiler_params=_params("parallel", "arbitrary"),
        name="s5_scan_long",
    )(z3, h0r.reshape(bsz, S5_BLKS, S5_BLK_CH), h0i.reshape(bsz, S5_BLKS, S5_BLK_CH),
      pwr[0].reshape(S5_BLKS, S5_BLK_CH), pwi[0].reshape(S5_BLKS, S5_BLK_CH), wbr, wbi, wcr, wci, d)
    return y, hr.reshape(bsz, S5_GROUPS, S5_STATE), hi.reshape(bsz, S5_GROUPS, S5_STATE)


def _s5_block_weights(bbr_t, bbi_t, c_re, c_im):
    eye = jnp.eye(S5_BLK_GROUPS, dtype=F32)

    def wb(bt):
        b4 = bt.reshape(S5_GROUP, S5_BLKS, S5_BLK_GROUPS, S5_STATE)
        w = jnp.einsum("cbgp,hg->bhcgp", b4, eye)
        return w.reshape(S5_BLKS, LANES, S5_BLK_CH).astype(BF16)

    def wc(c):
        c4 = c.reshape(S5_BLKS, S5_BLK_GROUPS, S5_GROUP, S5_STATE)
        w = jnp.einsum("bgcp,hg->bhpgc", c4, eye)
        return w.reshape(S5_BLKS, S5_BLK_CH, LANES).astype(BF16)

    return wb(bbr_t), wb(bbi_t), wc(c_re), wc(c_im)


def _hg_lb_kernel(x_ref, o_ref):
    x = x_ref[...]
    e = jnp.exp(x - jnp.max(x, axis=0, keepdims=True))
    sm = e / jnp.sum(e, axis=0, keepdims=True)
    acc = sm[0:1]
    o_ref[0:1, :] = acc
    for l in range(1, x.shape[0]):
        acc = acc + sm[l:l + 1]
        o_ref[l:l + 1, :] = acc


def hg_lower_bounds(hg_lb):
    return pl.pallas_call(_hg_lb_kernel, out_shape=jax.ShapeDtypeStruct(hg_lb.shape, F32), name="hg_lb")(hg_lb)


def _cumsum_rows(x, n):
    row = lax.broadcasted_iota(jnp.int32, x.shape, 1)
    d = 1
    while d < n:
        x = x + jnp.where(row >= d, pltpu.roll(x, d, axis=1), 0.0)
        d *= 2
    return x


def _hgrn_kernel(q_ref, f_ref, i_ref, g_ref, s0_ref, lb_ref, nw_ref, o_ref, sf_ref, st_scr, *, nb, hh, chunk, n_sub):
    step = pl.program_id(2)

    def units(x):
        return jnp.concatenate([x[:, :, h * LANES:(h + 1) * LANES] for h in range(hh)], axis=0)

    lb = units(jnp.broadcast_to(lb_ref[...][None], (nb, 1, hh * LANES)))
    nw = nw_ref[...][None]

    @pl.when(step == 0)
    def _():
        for h in range(hh):
            for b in range(nb):
                st_scr[h * nb + b] = s0_ref[b, h].T

    trow = lax.broadcasted_iota(jnp.int32, (chunk, chunk), 0)
    tcol = lax.broadcasted_iota(jnp.int32, (chunk, chunk), 1)
    causal = (tcol <= trow)[None]
    st = st_scr[...]
    for c in range(n_sub):
        rows = slice(c * chunk, (c + 1) * chunk)
        q, f, v, g = (units(ref[:, rows, :]) for ref in (q_ref, f_ref, i_ref, g_ref))
        fg = lb + (1.0 - lb) * jax.nn.sigmoid(f)
        qh = jax.nn.silu(q)
        kh = 1.0 - fg
        bcum = _cumsum_rows(jnp.log(fg), chunk)
        btot = bcum[:, chunk - 1:chunk, :]
        q_in = (qh * jnp.exp(bcum)).astype(BF16)
        k_in = (kh * jnp.exp(-bcum)).astype(BF16)
        k_end = (kh * jnp.exp(btot - bcum)).astype(BF16)
        decay = jnp.exp(btot)
        vb = v.astype(BF16)
        att = jnp.einsum("utk,usk->uts", q_in, k_in, preferred_element_type=F32)
        att = jnp.where(causal, att, 0.0).astype(BF16)
        out = (jnp.einsum("utk,uvk->utv", q_in, st.astype(BF16), preferred_element_type=F32)
               + jnp.einsum("uts,usv->utv", att, vb, preferred_element_type=F32))
        st = st * decay + jnp.einsum("usv,usk->uvk", vb, k_end, preferred_element_type=F32)
        out = out * lax.rsqrt(jnp.mean(out * out, axis=-1, keepdims=True) + EPS) * nw
        out = (out * jax.nn.silu(g)).astype(o_ref.dtype)
        for h in range(hh):
            o_ref[:, rows, h * LANES:(h + 1) * LANES] = out[h * nb:(h + 1) * nb]
    st_scr[...] = st

    @pl.when(step == pl.num_programs(2) - 1)
    def _():
        for h in range(hh):
            for b in range(nb):
                sf_ref[b, h] = st[h * nb + b].T


def hgrn2(z3, s0, lb, norm_w, nb, hh):
    bsz, seq, _ = z3.shape
    chunk = min(HG_CHUNK, seq)
    n_sub = next(n for n in (3, 2, 1) if seq % (n * chunk) == 0)
    rows = n_sub * chunk
    assert bsz % nb == 0 and HG_HEADS % hh == 0
    wid = hh * LANES
    n_col = (HG_HEADS * HG_K) // wid

    def col(proj):
        return pl.BlockSpec((nb, rows, wid), lambda h, b, t, proj=proj: (b, t, proj * n_col + h))

    st_blk = pl.BlockSpec((nb, hh, HG_K, HG_V), lambda h, b, t: (b, h, 0, 0))
    out, sf = pl.pallas_call(
        functools.partial(_hgrn_kernel, nb=nb, hh=hh, chunk=chunk, n_sub=n_sub),
        grid=(HG_HEADS // hh, bsz // nb, seq // rows),
        in_specs=[col(1), col(2), col(3), col(4), st_blk,
                  pl.BlockSpec((1, wid), lambda h, b, t: (0, h)),
                  pl.BlockSpec((1, LANES), lambda h, b, t: (0, 0))],
        out_specs=[pl.BlockSpec((nb, rows, wid), lambda h, b, t: (b, t, h)), st_blk],
        out_shape=[jax.ShapeDtypeStruct((bsz, seq, HG_HEADS * HG_V), BF16),
                   jax.ShapeDtypeStruct((bsz, HG_HEADS, HG_K, HG_V), F32)],
        scratch_shapes=[pltpu.VMEM((hh * nb, HG_V, HG_K), F32)],
        compiler_params=_params("parallel", "parallel", "arbitrary"),
        name="hgrn2",
    )(z3, z3, z3, z3, s0, lb.reshape(1, HG_HEADS * HG_K), norm_w.reshape(1, HG_V))
    return out, sf


RW_MIXES = 6


def _norm_mix_kernel(h_ref, sh_ref, w_ref, mu_ref, *refs, tl):
    o_refs = refs[:RW_MIXES]
    last_ref, scr = refs[RW_MIXES:]
    x = h_ref[...]
    xn = x * lax.rsqrt(jnp.mean(x * x, axis=-1, keepdims=True) + EPS) * w_ref[...][None]

    @pl.when(pl.program_id(1) == 0)
    def _():
        scr[:, SUBLANES - 1:SUBLANES, :] = sh_ref[...]

    scr[:, SUBLANES:, :] = xn
    xx = scr[:, SUBLANES - 1:SUBLANES - 1 + tl, :] - xn
    for j, o_ref in enumerate(o_refs):
        o_ref[...] = (xn + xx * mu_ref[j:j + 1, :][None]).astype(o_ref.dtype)
    last = xn[:, tl - 1:tl, :]
    scr[:, SUBLANES - 1:SUBLANES, :] = last
    last_ref[...] = last


def norm_mix(h3, shift0, ln_w, mu, nb, tl):
    bsz, seq, d = h3.shape
    assert bsz % nb == 0 and seq % tl == 0 and tl % SUBLANES == 0
    blk = pl.BlockSpec((nb, tl, d), lambda b, t: (b, t, 0))
    row = pl.BlockSpec((nb, 1, d), lambda b, t: (b, 0, 0))
    outs = pl.pallas_call(
        functools.partial(_norm_mix_kernel, tl=tl),
        grid=(bsz // nb, seq // tl),
        in_specs=[blk, row, pl.BlockSpec((1, d), lambda b, t: (0, 0)), pl.BlockSpec((RW_MIXES, d), lambda b, t: (0, 0))],
        out_specs=[blk] * RW_MIXES + [row],
        out_shape=[jax.ShapeDtypeStruct((bsz, seq, d), BF16)] * RW_MIXES + [jax.ShapeDtypeStruct((bsz, 1, d), F32)],
        scratch_shapes=[pltpu.VMEM((nb, SUBLANES + tl, d), F32)],
        compiler_params=_params("parallel", "arbitrary"),
        name="norm_mix",
    )(h3, shift0.reshape(bsz, 1, d), ln_w.reshape(1, d), mu)
    return outs[:RW_MIXES], outs[RW_MIXES].reshape(bsz, d)


RW_PAIR = LANES // RW_HEAD
RW_DECAY_SCALE = math.exp(-0.5)
RW_SOLVE_BLOCK = 8


def _rwkv_kernel(r_ref, k_ref, v_ref, tw_ref, ta_ref, tg_ref, w2_ref, a2_ref, g2_ref, s0_ref,
                 w0_ref, a0_ref, kk_ref, ka_ref, rk_ref, lnw_ref, lnb_ref,
                 o_ref, sf_ref, s_scr, *, nb, hp, chunk):
    nu = hp * nb
    step = pl.program_id(2)
    lane = lax.broadcasted_iota(jnp.int32, (1, 1, LANES), 2)
    head1 = lane >= RW_HEAD

    def units(x):
        return jnp.concatenate([x[:, :, p * LANES:(p + 1) * LANES] for p in range(hp)], axis=0)

    def unit_rows(ref):
        return units(jnp.broadcast_to(ref[...][None], (nb, 1, hp * LANES)))

    w0, a0, k_k, k_a, r_k, ln_w, ln_b = (unit_rows(p) for p in
                                         (w0_ref, a0_ref, kk_ref, ka_ref, rk_ref, lnw_ref, lnb_ref))
    sq_row = lax.broadcasted_iota(jnp.int32, (LANES, LANES), 0) >= RW_HEAD
    sq_col = lax.broadcasted_iota(jnp.int32, (LANES, LANES), 1) >= RW_HEAD
    same_head = sq_row == sq_col
    ones_bd = same_head.astype(BF16)

    @pl.when(step == 0)
    def _():
        zero = jnp.zeros((nb, RW_HEAD, RW_HEAD), F32)
        for p in range(hp):
            top = jnp.concatenate([s0_ref[:, RW_PAIR * p], zero], axis=-1)
            bot = jnp.concatenate([zero, s0_ref[:, RW_PAIR * p + 1]], axis=-1)
            s_scr[p * nb:(p + 1) * nb] = jnp.concatenate([top, bot], axis=1)

    def bdot(spec, a, b):
        return jnp.einsum(spec, a.astype(BF16), b.astype(BF16), preferred_element_type=F32)

    def head_sum(x, two_pass=True):
        x2 = x.reshape(nu * chunk, LANES)
        hi = x2.astype(BF16)
        s = jnp.dot(hi, ones_bd, preferred_element_type=F32)
        if two_pass:
            lo = (x2 - hi.astype(F32)).astype(BF16)
            s = s + jnp.dot(lo, ones_bd, preferred_element_type=F32)
        return s.reshape(nu, chunk, LANES)

    def low_rank(t_ref, w_ref):
        t2 = t_ref[...].reshape(nb * chunk, t_ref.shape[-1])
        return units(jnp.dot(t2, w_ref[...], preferred_element_type=F32).reshape(nb, chunk, hp * LANES))

    srow = lax.broadcasted_iota(jnp.int32, (chunk, chunk), 0)
    scol = lax.broadcasted_iota(jnp.int32, (chunk, chunk), 1)
    tri = jnp.broadcast_to((scol <= srow).astype(BF16)[None], (nu, chunk, chunk))

    def tri_sum(x):
        hi = x.astype(BF16)
        lo = (x - hi.astype(F32)).astype(BF16)
        return (jnp.einsum("uts,usc->utc", tri, hi, preferred_element_type=F32)
                + jnp.einsum("uts,usc->utc", tri, lo, preferred_element_type=F32))

    def stack_heads(x):
        return jnp.concatenate([jnp.where(head1, 0.0, x), jnp.where(head1, x, 0.0)], axis=1).astype(BF16)

    r, k, v = (units(ref[...]) for ref in (r_ref, k_ref, v_ref))
    wl, al, g = low_rank(tw_ref, w2_ref), low_rank(ta_ref, a2_ref), low_rank(tg_ref, g2_ref)
    lw = (-RW_DECAY_SCALE) * jax.nn.sigmoid(w0 + wl)
    ag = jax.nn.sigmoid(a0 + al)
    kk = k * k_k
    kk = kk * jnp.minimum(lax.rsqrt(head_sum(kk * kk)), 1e12)
    k2 = k * (1.0 + (ag - 1.0) * k_a)
    cl = tri_sum(lw)
    e_pos = jnp.exp(cl)
    e_neg = jnp.exp(-cl)
    at = (-kk) * jnp.exp(cl - lw)
    bt = (kk * ag) * e_neg
    kt = k2 * e_neg
    rt = r * e_pos
    wc = e_pos[:, chunk - 1:chunk, :]
    trow = lax.broadcasted_iota(jnp.int32, (chunk, RW_PAIR * chunk), 0)
    tcol = lax.broadcasted_iota(jnp.int32, (chunk, RW_PAIR * chunk), 1)
    tcol = jnp.where(tcol >= chunk, tcol - chunk, tcol)
    strict = (tcol < trow)[None]
    incl = (tcol <= trow)[None]
    x2 = jnp.concatenate([at, rt], axis=1)
    pb = bdot("utc,usc->uts", x2, stack_heads(bt))
    pk = bdot("utc,usc->uts", x2, stack_heads(kt))
    lab = jnp.where(strict, pb[:, :chunk], 0.0)
    lak = jnp.where(strict, pk[:, :chunk], 0.0)
    arb = jnp.where(incl, pb[:, chunk:], 0.0)
    ark = jnp.where(incl, pk[:, chunk:], 0.0)
    v_bd = stack_heads(v)
    xa = at
    xv = bdot("uts,usc->utc", lak, v_bd)
    sub = min(RW_SOLVE_BLOCK, chunk)
    done_a, done_v = [], []
    for lo in range(0, chunk, sub):
        xa_i = xa[:, lo:lo + sub, :]
        xv_i = xv[:, lo:lo + sub, :]
        if lo:
            pad = jnp.zeros((nu, chunk - lo, LANES), F32)
            prev = jnp.concatenate([stack_heads(jnp.concatenate(done_a + [pad], axis=1)),
                                    stack_heads(jnp.concatenate(done_v + [pad], axis=1))], axis=-1)
            upd = bdot("uts,usc->utc", lab[:, lo:lo + sub, :], prev)
            xa_i = xa_i + upd[:, :, :LANES]
            xv_i = xv_i + upd[:, :, LANES:]
        l0 = lab[:, lo:lo + sub, lo:lo + sub]
        l1 = lab[:, lo:lo + sub, chunk + lo:chunk + lo + sub]
        for s in range(sub - 1):
            m = jnp.where(head1, l1[:, :, s:s + 1], l0[:, :, s:s + 1])
            xa_i = xa_i + m * xa_i[:, s:s + 1, :]
            xv_i = xv_i + m * xv_i[:, s:s + 1, :]
        done_a.append(xa_i)
        done_v.append(xv_i)
    ah = jnp.concatenate(done_a, axis=1)
    vh = jnp.concatenate(done_v, axis=1)
    both = bdot("uts,usc->utc", arb, jnp.concatenate([stack_heads(ah), stack_heads(vh)], axis=-1))
    rh = rt + both[:, :, :LANES]
    yh = both[:, :, LANES:] + bdot("uts,usc->utc", ark, v_bd)
    gp = jnp.where(same_head, bdot("utj,utk->ujk", ah, bt), 0.0)
    ht = jnp.where(same_head, bdot("utv,utk->uvk", jnp.concatenate([vh, v], axis=1),
                                   jnp.concatenate([bt, kt], axis=1)), 0.0)
    st = s_scr[...]
    y = bdot("utk,uvk->utv", rh, st) + yh
    st = (st + bdot("uvj,ujk->uvk", st, gp) + ht) * wc
    s_scr[...] = st
    inv_n = 1.0 / RW_HEAD
    yc = y - head_sum(y, two_pass=False) * inv_n
    var = head_sum(yc * yc, two_pass=False) * inv_n
    y = yc * lax.rsqrt(var + RW_GN_EPS) * ln_w + ln_b
    y = y + head_sum(r * k2 * r_k, two_pass=False) * v
    out = (y * g).astype(o_ref.dtype)
    for p in range(hp):
        o_ref[:, :, p * LANES:(p + 1) * LANES] = out[p * nb:(p + 1) * nb]

    @pl.when(step == pl.num_programs(2) - 1)
    def _():
        for p in range(hp):
            sf_ref[:, RW_PAIR * p] = st[p * nb:(p + 1) * nb, :RW_HEAD, :RW_HEAD]
            sf_ref[:, RW_PAIR * p + 1] = st[p * nb:(p + 1) * nb, RW_HEAD:, RW_HEAD:]


def _rwkv_chunk(seq):
    for c in (48, 32, 16, 8):
        if seq % c == 0:
            return c
    raise ValueError(seq)


def rwkv7(r, k, v, low, low_w, s0, w0, a0, k_k, k_a, r_k, ln_w, ln_b, nb, hp):
    bsz, seq, d = r.shape
    chunk = _rwkv_chunk(seq)
    heads = hp * RW_PAIR
    assert bsz % nb == 0 and RW_HEADS % heads == 0
    seq_blk = pl.BlockSpec((nb, chunk, hp * LANES), lambda h, b, t: (b, t, h))
    st_blk = pl.BlockSpec((nb, heads, RW_HEAD, RW_HEAD), lambda h, b, t: (b, h, 0, 0))
    vec = pl.BlockSpec((1, hp * LANES), lambda h, b, t: (0, h))
    low_blk = [pl.BlockSpec((nb, chunk, x.shape[-1]), lambda h, b, t: (b, t, 0)) for x in low]
    low_w_blk = [pl.BlockSpec((w.shape[0], hp * LANES), lambda h, b, t: (0, h)) for w in low_w]
    out, sf = pl.pallas_call(
        functools.partial(_rwkv_kernel, nb=nb, hp=hp, chunk=chunk),
        grid=(RW_HEADS // heads, bsz // nb, seq // chunk),
        in_specs=[seq_blk] * 3 + low_blk + low_w_blk + [st_blk] + [vec] * 7,
        out_specs=[seq_blk, st_blk],
        out_shape=[jax.ShapeDtypeStruct((bsz, seq, d), BF16),
                   jax.ShapeDtypeStruct((bsz, RW_HEADS, RW_HEAD, RW_HEAD), F32)],
        scratch_shapes=[pltpu.VMEM((hp * nb, LANES, LANES), F32)],
        compiler_params=_params("parallel", "parallel", "arbitrary"),
        name="rwkv7",
    )(r, k, v, *low, *low_w, s0, *(p.reshape(1, d) for p in (w0, a0, k_k, k_a, r_k, ln_w, ln_b)))
    return out, sf


def _ffn_in_kernel(x_ref, wa_ref, wv_ref, e_ref, cw_ref, cb_ref, o_ref, st_ref, scr, *wb_refs, nb, seq, sb, sr):
    tn = wa_ref.shape[1]
    if wb_refs:
        @pl.when(pl.program_id(1) == 0)
        def _():
            wb_refs[0][...] = wa_ref[...].astype(BF16)
            wb_refs[1][...] = wv_ref[...].astype(BF16)

        wa_ref, wv_ref = wb_refs
    cw = cw_ref[...]
    cb = cb_ref[...][None]
    scr[:, SUBLANES - (CONV_W - 1):SUBLANES, :] = e_ref[...]
    for b0 in range(0, nb, sb):
        for r0 in range(0, seq, sr):
            lo = b0 * seq + r0
            x = x_ref[lo:lo + sb * sr, :]
            a = jnp.dot(x, wa_ref[...], preferred_element_type=F32).reshape(sb, sr, tn)
            v = jnp.dot(x, wv_ref[...], preferred_element_type=F32).reshape(sb, sr, tn)
            scr[b0:b0 + sb, SUBLANES + r0:SUBLANES + r0 + sr, :] = a
            c = cb + cw[CONV_W - 1:CONV_W][None] * a
            for j in range(CONV_W - 1):
                first = SUBLANES + r0 - (CONV_W - 1 - j)
                c = c + cw[j:j + 1][None] * scr[b0:b0 + sb, first:first + sr, :]
            o_ref[lo:lo + sb * sr, :] = (jax.nn.gelu(c) * v).reshape(sb * sr, tn).astype(o_ref.dtype)
    st_ref[...] = scr[:, SUBLANES + seq - (CONV_W - 1):SUBLANES + seq, :]


def ffn_in(xb, conv0, w_in, layer, conv_w, conv_b, bsz, seq, nb, tn, sub):
    t, d = xb.shape
    sb, sr = sub
    assert t == bsz * seq and bsz % nb == 0 and D_FF % tn == 0
    assert nb % sb == 0 and seq % sr == 0 and sr % SUBLANES == 0 and (sr == seq or nb == sb == 1)
    n_col = D_FF // tn
    cast_w = w_in.dtype == F32
    rc = (lambda f: lambda j, i: f(i, j)) if cast_w else (lambda f: f)
    col = rc(lambda i, j: (0, j))
    out, st = pl.pallas_call(
        functools.partial(_ffn_in_kernel, nb=nb, seq=seq, sb=sb, sr=sr),
        grid=(n_col, bsz // nb) if cast_w else (bsz // nb, n_col),
        in_specs=[pl.BlockSpec((nb * seq, d), rc(lambda i, j: (i, 0))),
                  pl.BlockSpec((None, d, tn), rc(lambda i, j: (layer, 0, j))),
                  pl.BlockSpec((None, d, tn), rc(lambda i, j: (layer, 0, j + n_col))),
                  pl.BlockSpec((nb, CONV_W - 1, tn), rc(lambda i, j: (i, 0, j))),
                  pl.BlockSpec((CONV_W, tn), col), pl.BlockSpec((1, tn), col)],
        out_specs=[pl.BlockSpec((nb * seq, tn), rc(lambda i, j: (i, j))),
                   pl.BlockSpec((nb, CONV_W - 1, tn), rc(lambda i, j: (i, 0, j)))],
        out_shape=[jax.ShapeDtypeStruct((t, D_FF), BF16),
                   jax.ShapeDtypeStruct((bsz, CONV_W - 1, D_FF), F32)],
        scratch_shapes=[pltpu.VMEM((nb, SUBLANES + seq, tn), F32)] + [pltpu.VMEM((d, tn), BF16)] * (2 * cast_w),
        compiler_params=_params("parallel", "arbitrary" if cast_w else "parallel"),
        name="ffn_in",
    )(xb, w_in, w_in, conv0, conv_w, conv_b.reshape(1, D_FF))
    return out, st


FFN_DOWN_K_STEPS = 2


def _channel_mixer(h, conv0, layer, p, cfg, bsz, seq):
    (xb,) = rmsnorm(h, p["ln_ffn"][layer], (BF16,))
    gated, n_cv = ffn_in(xb, conv0, p["ffn_w_in"], layer, p["ffn_conv_w"][layer], p["ffn_conv_b"][layer],
                         bsz, seq, cfg["ffn_nb"], cfg["ffn_tn"], cfg["ffn_sub"])
    return matmul_residual_split(gated, p["ffn_w_down"], layer, h, FFN_DOWN_K_STEPS), n_cv


def _trunk(x3, s5r, s5i, hg, rw, sh, cv, p, cfg):
    bsz, seq, d = x3.shape
    seq += cfg["front"]
    t = bsz * seq

    if cfg["front"]:
        h, xb = embed_norm(x3, p["meta"], p["ln_mix"][0])
    else:
        h = x3.reshape(t, d)
        (xb,) = rmsnorm(h, p["ln_mix"][0], (BF16,))
    z = matmul([xb], [p["ev_w_in"]], F32).reshape(bsz, seq, EVEN_IN)
    if cfg["s5_tl"]:
        ys5, n_s5r, n_s5i = s5_scan_long(z, s5r[0], s5i[0], *p["s5"], tl=cfg["s5_tl"])
    else:
        ys5, n_s5r, n_s5i = s5_scan(z, s5r[0], s5i[0], *p["s5"], nb=cfg["s5_nb"])
    ys5 = ys5.reshape(t, S5_WIDTH)
    ya = matmul([ys5], [p["s5_w_glu"]], BF16, epilogue="glu", extra=ys5)
    yb, n_hg = hgrn2(z, hg[0], p["hg_lb"], p["hg_norm_w"], nb=cfg["hg_nb"], hh=cfg["hg_hh"])
    h = matmul([ya, yb.reshape(t, -1)], [(p["ev_w_out"], 0, 0), (p["ev_w_out"], 0, 1)], F32,
               epilogue="residual", extra=h)
    h, n_cv0 = _channel_mixer(h, cv[0], 0, p, cfg, bsz, seq)

    mixes, n_sh = norm_mix(h.reshape(bsz, seq, d), sh[0], p["ln_mix"][1], p["rw_mu"], *cfg["mix_blk"])
    xr, xw, xk, xv, xa, xg = (m.reshape(t, d) for m in mixes)
    r = matmul([xr], [p["rw_w_r"]], F32)
    k = matmul([xk], [p["rw_w_k"]], F32)
    v = matmul([xv], [p["rw_w_v"]], F32)
    low = (matmul([xw], [p["rw_w1"]], BF16, act="tanh"), matmul([xa], [p["rw_a1"]], BF16),
           matmul([xg], [p["rw_g1"]], BF16, act="sigmoid"))
    as3 = lambda a: a.reshape(bsz, seq, a.shape[-1])
    yo, n_rw = rwkv7(as3(r), as3(k), as3(v), [as3(x) for x in low], (p["rw_w2"], p["rw_a2"], p["rw_g2"]),
                     rw[0], *p["rw_vec"], nb=cfg["rw_nb"], hp=cfg["rw_hp"])
    h = matmul([yo.reshape(t, d)], [p["rw_w_o"]], F32, epilogue="residual", extra=h)
    h, n_cv1 = _channel_mixer(h, cv[1], 1, p, cfg, bsz, seq)

    (y,) = rmsnorm(h, p["ln_final"], (F32,), rows=(bsz, seq, cfg["front"]) if cfg["front"] else None)
    return (y.reshape(bsz, seq - cfg["front"], d), n_s5r[None], n_s5i[None], n_hg[None], n_rw[None], n_sh[None],
            jnp.stack([n_cv0, n_cv1]))


PROMPT_CFG = dict(front=N_META, s5_tl=344, s5_nb=None, hg_nb=4, hg_hh=8, mix_blk=(1, 344), rw_nb=4, rw_hp=16, ffn_nb=1, ffn_tn=512, ffn_sub=(1, 688))
SAMPLE_CFG = dict(front=0, s5_tl=None, s5_nb=32, hg_nb=4, hg_hh=8, mix_blk=(32, 8), rw_nb=16, rw_hp=4, ffn_nb=128, ffn_tn=512, ffn_sub=(32, 8))


def kernel(x_prompt, x_sample, state_s5_re, state_s5_im, state_hgrn, state_rwkv, state_shift, state_conv, meta_tokens, ln_mix, ln_ffn, ln_final, ev_w_in, ev_w_out, s5_lam_re, s5_lam_im, s5_log_step, s5_b_re, s5_b_im, s5_c_re, s5_c_im, s5_d, s5_w_glu, hg_lb, hg_norm_w, rw_mu, rw_w0, rw_w1, rw_w2, rw_a0, rw_a1, rw_a2, rw_g1, rw_g2, rw_k_k, rw_k_a, rw_r_k, rw_w_r, rw_w_k, rw_w_v, rw_w_o, rw_ln_w, rw_ln_b, ffn_w_in, ffn_conv_w, ffn_conv_b, ffn_w_down):
    bf = lambda w: w.astype(BF16)
    lb_all = hg_lower_bounds(hg_lb)
    pwr, pwi, bbr_t, bbi_t = s5_prep(s5_lam_re[0], s5_lam_im[0], s5_log_step[0], s5_b_re[0], s5_b_im[0])
    wbr, wbi, wcr, wci = _s5_block_weights(bbr_t, bbi_t, s5_c_re[0], s5_c_im[0])
    p = {
        "meta": meta_tokens, "ln_mix": ln_mix, "ln_ffn": ln_ffn, "ln_final": ln_final,
        "ev_w_in": ev_w_in[0],
        "ev_w_out": ev_w_out,
        "s5": (pwr, pwi, wbr, wbi, wcr, wci, s5_d[0].reshape(1, S5_WIDTH)),
        "s5_w_glu": s5_w_glu[0],
        "hg_lb": lb_all[0], "hg_norm_w": hg_norm_w[0],
        "rw_mu": rw_mu[0],
        "rw_w1": rw_w1[0], "rw_w2": bf(rw_w2[0]), "rw_a1": rw_a1[0], "rw_a2": bf(rw_a2[0]),
        "rw_g1": rw_g1[0], "rw_g2": bf(rw_g2[0]),
        "rw_w_r": rw_w_r[0], "rw_w_k": rw_w_k[0], "rw_w_v": rw_w_v[0], "rw_w_o": rw_w_o[0],
        "rw_vec": (rw_w0[0], rw_a0[0], rw_k_k[0], rw_k_a[0], rw_r_k[0].reshape(D_MODEL), rw_ln_w[0], rw_ln_b[0]),
        "ffn_w_in": ffn_w_in, "ffn_conv_w": ffn_conv_w, "ffn_conv_b": ffn_conv_b,
        "ffn_w_down": bf(ffn_w_down),
    }

    bsz = x_prompt.shape[0]
    zeros = lambda *s: jnp.zeros(s, F32)
    outs_p = _trunk(x_prompt,
                    zeros(1, bsz, S5_GROUPS, S5_STATE), zeros(1, bsz, S5_GROUPS, S5_STATE),
                    zeros(1, bsz, HG_HEADS, HG_K, HG_V), zeros(1, bsz, RW_HEADS, RW_HEAD, RW_HEAD),
                    zeros(1, bsz, D_MODEL), zeros(2, bsz, CONV_W - 1, D_FF), p, PROMPT_CFG)
    outs_s = _trunk(x_sample, state_s5_re, state_s5_im, state_hgrn, state_rwkv, state_shift, state_conv,
                    p, SAMPLE_CFG)
    return tuple(outs_p[:1]) + tuple(outs_s[:1]) + tuple(outs_p[1:]) + tuple(outs_s[1:])
```

```python
import functools
import math

import jax
import jax.numpy as jnp
from jax import lax
from jax.experimental import pallas as pl
from jax.experimental.pallas import tpu as pltpu

F32 = jnp.float32
BF16 = jnp.bfloat16

D_MODEL = 2048
N_META = 16
EPS = 1e-6
S5_WIDTH = 1024
S5_GROUP = 16
S5_GROUPS = 64
S5_STATE = 64
S5_CH = S5_GROUPS * S5_STATE
HG_HEADS = 8
HG_K = 128
HG_V = 128
HG_CHUNK = 16
EVEN_IN = 5120
RW_HEAD = 64
RW_HEADS = 32
RW_GN_EPS = 64e-5
D_FF = 5632
CONV_W = 3

LANES = 128
SUBLANES = 8
VMEM_LIMIT = 56 * 1024 * 1024


def _params(*sem):
    return pltpu.CompilerParams(dimension_semantics=sem, vmem_limit_bytes=VMEM_LIMIT)


def _row_tile(t, cap=1024):
    best = None
    for d in range(16, min(t, cap) + 1, 16):
        if t % d == 0:
            best = d
    assert best is not None, t
    return best


def _rms_kernel(x_ref, w_ref, *o_refs):
    x = x_ref[...]
    y = x * lax.rsqrt(jnp.mean(x * x, axis=-1, keepdims=True) + EPS) * w_ref[...]
    for o_ref in o_refs:
        o_ref[...] = y.astype(o_ref.dtype)


def rmsnorm(x, w, dtypes, rows=None):
    t, d = x.shape
    if rows is None:
        tm = _row_tile(t)
        n_out = t
        grid = (t // tm,)
        in_spec = pl.BlockSpec((tm, d), lambda i: (i, 0))
        out_spec = in_spec
        vec = pl.BlockSpec((1, d), lambda i: (0, 0))
    else:
        bsz, seq, front = rows
        keep = seq - front
        assert t == bsz * seq and front % 16 == 0
        tm = _row_tile(keep)
        per = keep // tm
        n_out = bsz * keep
        grid = (bsz, per)
        in_spec = pl.BlockSpec((pl.Element(tm), pl.Element(d)),
                               lambda b, i: (pl.multiple_of(b * seq + front + i * tm, 16), 0))
        out_spec = pl.BlockSpec((tm, d), lambda b, i: (b * per + i, 0))
        vec = pl.BlockSpec((1, d), lambda b, i: (0, 0))
    outs = pl.pallas_call(
        _rms_kernel,
        grid=grid,
        in_specs=[in_spec, vec],
        out_specs=[out_spec for _ in dtypes],
        out_shape=[jax.ShapeDtypeStruct((n_out, d), dt) for dt in dtypes],
        compiler_params=_params(*(["parallel"] * len(grid))),
        name="rmsnorm",
    )(x, w.reshape(1, d))
    return outs


def _embed_norm_kernel(x_ref, m_ref, w_ref, h_ref, xb_ref, *, front):
    x = x_ref[...]
    first = pl.program_id(1) == 0
    body = jnp.where(first, pltpu.roll(x, front, axis=0), x)
    head = jnp.where(first, m_ref[...], x[:front])
    rows = jnp.concatenate([head, body[front:]], axis=0)
    h_ref[...] = rows
    y = rows * lax.rsqrt(jnp.mean(rows * rows, axis=-1, keepdims=True) + EPS) * w_ref[...]
    xb_ref[...] = y.astype(xb_ref.dtype)


def embed_norm(x3, meta, w):
    bsz, seq, d = x3.shape
    front = meta.shape[0]
    total = seq + front
    tm = _row_tile(total)
    per = total // tm
    assert front % 16 == 0 and tm > front
    blk = pl.BlockSpec((tm, d), lambda b, j: (b * per + j, 0))
    return pl.pallas_call(
        functools.partial(_embed_norm_kernel, front=front),
        grid=(bsz, per),
        in_specs=[pl.BlockSpec((pl.Element(tm), pl.Element(d)),
                               lambda b, j: (pl.multiple_of(b * seq + jnp.maximum(j * tm - front, 0), 16), 0)),
                  pl.BlockSpec((front, d), lambda b, j: (0, 0)),
                  pl.BlockSpec((1, d), lambda b, j: (0, 0))],
        out_specs=[blk, blk],
        out_shape=[jax.ShapeDtypeStruct((bsz * total, d), F32), jax.ShapeDtypeStruct((bsz * total, d), BF16)],
        compiler_params=_params("parallel", "parallel"),
        name="embed_norm",
    )(x3.reshape(bsz * seq, d), meta, w.reshape(1, d))


def _act(x, act):
    if act == "tanh":
        return jnp.tanh(x)
    if act == "sigmoid":
        return jax.nn.sigmoid(x)
    assert act is None
    return x


def _mm_kernel(*refs, n_a, act, epilogue, cast_w):
    a_refs = refs[:n_a]
    w_refs = refs[n_a:2 * n_a]
    rest = refs[2 * n_a:]
    if cast_w:
        rest, wb_refs = rest[:-n_a], rest[-n_a:]

        @pl.when(pl.program_id(1) == 0)
        def _():
            for w_ref, wb_ref in zip(w_refs, wb_refs):
                wb_ref[...] = w_ref[...].astype(BF16)

        w_refs = wb_refs
    o_ref = rest[-1]
    acc = jnp.dot(a_refs[0][...].astype(BF16), w_refs[0][...], preferred_element_type=F32)
    for a_ref, w_ref in zip(a_refs[1:], w_refs[1:]):
        acc = acc + jnp.dot(a_ref[...].astype(BF16), w_ref[...], preferred_element_type=F32)
    acc = _act(acc, act)
    if epilogue == "residual":
        acc = rest[0][...] + acc
    elif epilogue == "glu":
        acc = rest[0][...] * jax.nn.sigmoid(acc)
    o_ref[...] = acc.astype(o_ref.dtype)


MM_VMEM_BUDGET = 40 * 1024 * 1024
MM_CAST_VMEM_BUDGET = 48 * 1024 * 1024
MM_MAX_A_BLOCK = 6 * 1024 * 1024
MXU_WIDTH = 256


def _mm_tiles(t, a_row_bytes, w_col_bytes, n, out_bytes, has_extra, budget=MM_VMEM_BUDGET):
    rows = [d for d in range(16, t + 1, 16) if t % d == 0 and d * a_row_bytes <= MM_MAX_A_BLOCK]
    cols = [d for d in range(LANES, n + 1, LANES) if n % d == 0] or [n]
    best, best_score = None, -1.0
    for tm in rows:
        for tn in cols:
            est = 2 * tm * a_row_bytes + w_col_bytes * tn + tm * tn * (2 * out_bytes + 4 + (8 if has_extra else 0))
            if est > budget:
                continue
            score = tm * tn * (1.0 if tn % MXU_WIDTH == 0 else 0.8)
            if score > best_score:
                best, best_score = (tm, tn), score
    assert best is not None, (t, a_row_bytes, w_col_bytes, n)
    return best


def matmul(a_list, w_list, out_dtype, act=None, epilogue=None, extra=None):
    t = a_list[0].shape[0]
    w0 = w_list[0][0] if isinstance(w_list[0], tuple) else w_list[0]
    n = w0.shape[-1]
    cast_w = w0.dtype == F32
    k_rows = sum(a.shape[1] for a in a_list)
    a_row_bytes = sum(a.shape[1] * a.dtype.itemsize for a in a_list)
    w_col_bytes = k_rows * (2 * 4 + 2 if cast_w else 2 * 2)
    tm, tn = _mm_tiles(t, a_row_bytes, w_col_bytes, n, jnp.dtype(out_dtype).itemsize, epilogue is not None,
                       budget=MM_CAST_VMEM_BUDGET if cast_w else MM_VMEM_BUDGET)
    rc = (lambda f: lambda j, i: f(i, j)) if cast_w else (lambda f: f)
    in_specs = [pl.BlockSpec((tm, a.shape[1]), rc(lambda i, j: (i, 0))) for a in a_list]
    args = list(a_list)
    for a, w in zip(a_list, w_list):
        if isinstance(w, tuple):
            w, layer, kblk = w
            in_specs.append(pl.BlockSpec((None, a.shape[1], tn),
                                         rc(lambda i, j, layer=layer, kblk=kblk: (layer, kblk, j))))
        else:
            in_specs.append(pl.BlockSpec((w.shape[0], tn), rc(lambda i, j: (0, j))))
        args.append(w)
    if epilogue is not None:
        in_specs.append(pl.BlockSpec((tm, tn), rc(lambda i, j: (i, j))))
        args.append(extra)
    return pl.pallas_call(
        functools.partial(_mm_kernel, n_a=len(a_list), act=act, epilogue=epilogue, cast_w=cast_w),
        grid=(n // tn, t // tm) if cast_w else (t // tm, n // tn),
        in_specs=in_specs,
        out_specs=pl.BlockSpec((tm, tn), rc(lambda i, j: (i, j))),
        out_shape=jax.ShapeDtypeStruct((t, n), out_dtype),
        scratch_shapes=[pltpu.VMEM((a.shape[1], tn), BF16) for a in a_list] if cast_w else [],
        compiler_params=_params("parallel", "arbitrary" if cast_w else "parallel"),
        name="matmul",
    )(*args)


def _mm_resid_split_kernel(a_ref, w_ref, res_ref, o_ref):
    part = jnp.dot(a_ref[...], w_ref[...], preferred_element_type=F32)

    @pl.when(pl.program_id(2) == 0)
    def _():
        o_ref[...] = res_ref[...] + part

    @pl.when(pl.program_id(2) > 0)
    def _():
        o_ref[...] += part


def matmul_residual_split(a, w, layer, res, k_steps):
    t, k = a.shape
    n = w.shape[2]
    tk = k // k_steps
    assert k % k_steps == 0 and tk % LANES == 0
    tm, tn = _mm_tiles(t, 2 * tk, 4 * tk, n, 4, True)
    return pl.pallas_call(
        _mm_resid_split_kernel,
        grid=(t // tm, n // tn, k_steps),
        in_specs=[pl.BlockSpec((tm, tk), lambda i, j, s: (i, s)),
                  pl.BlockSpec((None, tk, tn), lambda i, j, s: (layer, s, j)),
                  pl.BlockSpec((tm, tn), lambda i, j, s: (i, j))],
        out_specs=pl.BlockSpec((tm, tn), lambda i, j, s: (i, j)),
        out_shape=jax.ShapeDtypeStruct((t, n), F32),
        compiler_params=_params("parallel", "parallel", "arbitrary"),
        name="matmul_split",
    )(a, w, res)


def _s5_prep_kernel(lr_ref, li_ref, ls_ref, brt_ref, bit_ref, pwr_ref, pwi_ref, bbr_ref, bbi_ref):
    lr = jnp.minimum(lr_ref[...], -1e-4)
    li = li_ref[...]
    dt = jnp.exp(ls_ref[...])
    n = lax.broadcasted_iota(jnp.int32, (SUBLANES, S5_CH), 0).astype(F32) + 1.0
    mag = jnp.exp(n * (lr * dt))
    ang = n * (li * dt)
    pwr = mag * jnp.cos(ang)
    pwi = mag * jnp.sin(ang)
    pwr_ref[...] = pwr
    pwi_ref[...] = pwi
    ar = pwr[0:1]
    ai = pwi[0:1]
    den = lr * lr + li * li
    zr = ((ar - 1.0) * lr + ai * li) / den
    zi = (ai * lr - (ar - 1.0) * li) / den
    br = brt_ref[...]
    bi = bit_ref[...]
    bbr_ref[...] = zr * br - zi * bi
    bbi_ref[...] = zr * bi + zi * br


def s5_prep(lam_re, lam_im, log_step, b_re, b_im):
    lr = lam_re.reshape(1, S5_CH)
    li = lam_im.reshape(1, S5_CH)
    ls = jnp.broadcast_to(log_step[:, None], (S5_GROUPS, S5_STATE)).reshape(1, S5_CH)
    brt = b_re.reshape(S5_CH, S5_GROUP).T
    bit = b_im.reshape(S5_CH, S5_GROUP).T
    return pl.pallas_call(
        _s5_prep_kernel,
        out_shape=[jax.ShapeDtypeStruct((SUBLANES, S5_CH), F32)] * 2
        + [jax.ShapeDtypeStruct((S5_GROUP, S5_CH), F32)] * 2,
        name="s5_prep",
    )(lr, li, ls, brt, bit)


S5_BLK_GROUPS = LANES // S5_GROUP
S5_BLKS = S5_WIDTH // LANES
S5_BLK_CH = S5_BLK_GROUPS * S5_STATE


def _cmul_add(xr, xi, mr, mi, sr, si):
    return xr + mr * sr - mi * si, xi + mr * si + mi * sr


def _s5_kernel(u_ref, h0r_ref, h0i_ref, pwr_ref, pwi_ref, wbr_ref, wbi_ref, wcr_ref, wci_ref, d_ref,
               y_ref, hr_ref, hi_ref, xr_scr, xi_scr, *, nb, seq):
    u2 = u_ref[...].reshape(nb * seq, LANES)
    ub = u2.astype(BF16)
    xr_scr[...] = jnp.dot(ub, wbr_ref[0], preferred_element_type=F32).reshape(nb, seq, S5_BLK_CH)
    xi_scr[...] = jnp.dot(ub, wbi_ref[0], preferred_element_type=F32).reshape(nb, seq, S5_BLK_CH)

    pwr = pwr_ref[...]
    pwi = pwi_ref[...]
    row = lax.broadcasted_iota(jnp.int32, (SUBLANES, S5_BLK_CH), 0)
    steps = []
    for d in (1, 2, 4):
        keep = row >= d
        steps.append((d, jnp.where(keep, pwr[d - 1:d], 0.0)[None], jnp.where(keep, pwi[d - 1:d], 0.0)[None]))
    pr = pwr[None]
    pi = pwi[None]

    def tile(i, carry):
        cr, ci = carry
        o = pl.multiple_of(i * SUBLANES, SUBLANES)
        xr = xr_scr[:, pl.ds(o, SUBLANES), :]
        xi = xi_scr[:, pl.ds(o, SUBLANES), :]
        for d, mr, mi in steps:
            sr = pltpu.roll(xr, d, axis=1)
            si = pltpu.roll(xi, d, axis=1)
            xr, xi = _cmul_add(xr, xi, mr, mi, sr, si)
        xr, xi = _cmul_add(xr, xi, pr, pi, cr, ci)
        xr_scr[:, pl.ds(o, SUBLANES), :] = xr
        xi_scr[:, pl.ds(o, SUBLANES), :] = xi
        return xr[:, SUBLANES - 1:SUBLANES, :], xi[:, SUBLANES - 1:SUBLANES, :]

    hr, hi = lax.fori_loop(0, seq // SUBLANES, tile, (h0r_ref[...], h0i_ref[...]))
    hr_ref[...] = hr
    hi_ref[...] = hi

    xr = xr_scr[...].reshape(nb * seq, S5_BLK_CH).astype(BF16)
    xi = xi_scr[...].reshape(nb * seq, S5_BLK_CH).astype(BF16)
    y = (jnp.dot(xr, wcr_ref[0], preferred_element_type=F32)
         - jnp.dot(xi, wci_ref[0], preferred_element_type=F32)
         + d_ref[...] * u2)
    y_ref[...] = jax.nn.gelu(y).reshape(nb, seq, LANES)


def s5_scan(z3, h0r, h0i, pwr, pwi, wbr, wbi, wcr, wci, d, nb):
    bsz, seq, _ = z3.shape
    assert seq % SUBLANES == 0 and bsz % nb == 0
    seq_blk = pl.BlockSpec((nb, seq, LANES), lambda b, k: (b, 0, k))
    st_blk = pl.BlockSpec((nb, 1, S5_BLK_CH), lambda b, k: (b, 0, k))
    pw_blk = pl.BlockSpec((SUBLANES, S5_BLK_CH), lambda b, k: (0, k))
    wb_blk = pl.BlockSpec((1, LANES, S5_BLK_CH), lambda b, k: (k, 0, 0))
    wc_blk = pl.BlockSpec((1, S5_BLK_CH, LANES), lambda b, k: (k, 0, 0))
    y, hr, hi = pl.pallas_call(
        functools.partial(_s5_kernel, nb=nb, seq=seq),
        grid=(bsz // nb, S5_BLKS),
        in_specs=[seq_blk, st_blk, st_blk, pw_blk, pw_blk, wb_blk, wb_blk, wc_blk, wc_blk,
                  pl.BlockSpec((1, LANES), lambda b, k: (0, k))],
        out_specs=[seq_blk, st_blk, st_blk],
        out_shape=[jax.ShapeDtypeStruct((bsz, seq, S5_WIDTH), F32),
                   jax.ShapeDtypeStruct((bsz, 1, S5_CH), F32),
                   jax.ShapeDtypeStruct((bsz, 1, S5_CH), F32)],
        scratch_shapes=[pltpu.VMEM((nb, seq, S5_BLK_CH), F32), pltpu.VMEM((nb, seq, S5_BLK_CH), F32)],
        compiler_params=_params("parallel", "parallel"),
        name="s5_scan",
    )(z3, h0r.reshape(bsz, 1, S5_CH), h0i.reshape(bsz, 1, S5_CH), pwr, pwi, wbr, wbi, wcr, wci, d)
    return y, hr.reshape(bsz, S5_GROUPS, S5_STATE), hi.reshape(bsz, S5_GROUPS, S5_STATE)


S5_LANE_TILES = S5_BLK_CH // LANES


def _s5_long_kernel(u_ref, h0r_ref, h0i_ref, ar_ref, ai_ref, wbr_ref, wbi_ref, wcr_ref, wci_ref, d_ref,
                    y_ref, hr_ref, hi_ref, xr_scr, xi_scr, cr_scr, ci_scr, *, tl):
    step = pl.program_id(1)

    @pl.when(step == 0)
    def _():
        cr_scr[...] = h0r_ref[0]
        ci_scr[...] = h0i_ref[0]

    for k in range(S5_BLKS):
        ub = u_ref[0, :, k * LANES:(k + 1) * LANES].astype(BF16)
        bur = jnp.dot(ub, wbr_ref[k], preferred_element_type=F32)
        bui = jnp.dot(ub, wbi_ref[k], preferred_element_type=F32)
        for j in range(S5_LANE_TILES):
            xr_scr[j, k * tl:(k + 1) * tl, :] = bur[:, j * LANES:(j + 1) * LANES]
            xi_scr[j, k * tl:(k + 1) * tl, :] = bui[:, j * LANES:(j + 1) * LANES]

    ar = [ar_ref[:, j * LANES:(j + 1) * LANES] for j in range(S5_LANE_TILES)]
    ai = [ai_ref[:, j * LANES:(j + 1) * LANES] for j in range(S5_LANE_TILES)]

    def token(t, carry):
        cr, ci = carry
        nr, ni = [], []
        for j in range(S5_LANE_TILES):
            rows = pl.ds(t, S5_BLKS, stride=tl)
            xr, xi = _cmul_add(xr_scr[j, rows, :], xi_scr[j, rows, :], ar[j], ai[j], cr[j], ci[j])
            xr_scr[j, rows, :] = xr
            xi_scr[j, rows, :] = xi
            nr.append(xr)
            ni.append(xi)
        return tuple(nr), tuple(ni)

    init = (tuple(cr_scr[:, j * LANES:(j + 1) * LANES] for j in range(S5_LANE_TILES)),
            tuple(ci_scr[:, j * LANES:(j + 1) * LANES] for j in range(S5_LANE_TILES)))
    cr, ci = lax.fori_loop(0, tl, token, init, unroll=8)
    cr = jnp.concatenate(cr, axis=-1)
    ci = jnp.concatenate(ci, axis=-1)
    cr_scr[...] = cr
    ci_scr[...] = ci
    hr_ref[0] = cr
    hi_ref[0] = ci

    for k in range(S5_BLKS):
        xr = jnp.concatenate([xr_scr[j, k * tl:(k + 1) * tl, :] for j in range(S5_LANE_TILES)], axis=-1)
        xi = jnp.concatenate([xi_scr[j, k * tl:(k + 1) * tl, :] for j in range(S5_LANE_TILES)], axis=-1)
        u = u_ref[0, :, k * LANES:(k + 1) * LANES]
        y = (jnp.dot(xr.astype(BF16), wcr_ref[k], preferred_element_type=F32)
             - jnp.dot(xi.astype(BF16), wci_ref[k], preferred_element_type=F32)
             + d_ref[:, k * LANES:(k + 1) * LANES] * u)
        y_ref[0, :, k * LANES:(k + 1) * LANES] = jax.nn.gelu(y)


def s5_scan_long(z3, h0r, h0i, pwr, pwi, wbr, wbi, wcr, wci, d, tl):
    bsz, seq, _ = z3.shape
    assert seq % tl == 0 and tl % SUBLANES == 0
    seq_blk = pl.BlockSpec((1, tl, S5_WIDTH), lambda b, t: (b, t, 0))
    st_blk = pl.BlockSpec((1, S5_BLKS, S5_BLK_CH), lambda b, t: (b, 0, 0))
    lam_blk = pl.BlockSpec((S5_BLKS, S5_BLK_CH), lambda b, t: (0, 0))
    wb_blk = pl.BlockSpec((S5_BLKS, LANES, S5_BLK_CH), lambda b, t: (0, 0, 0))
    wc_blk = pl.BlockSpec((S5_BLKS, S5_BLK_CH, LANES), lambda b, t: (0, 0, 0))
    rows = pltpu.VMEM((S5_LANE_TILES, S5_BLKS * tl, LANES), F32)
    carry = pltpu.VMEM((S5_BLKS, S5_BLK_CH), F32)
    y, hr, hi = pl.pallas_call(
        functools.partial(_s5_long_kernel, tl=tl),
        grid=(bsz, seq // tl),
        in_specs=[seq_blk, st_blk, st_blk, lam_blk, lam_blk, wb_blk, wb_blk, wc_blk, wc_blk,
                  pl.BlockSpec((1, S5_WIDTH), lambda b, t: (0, 0))],
        out_specs=[seq_blk, st_blk, st_blk],
        out_shape=[jax.ShapeDtypeStruct((bsz, seq, S5_WIDTH), F32),
                   jax.ShapeDtypeStruct((bsz, S5_BLKS, S5_BLK_CH), F32),
                   jax.ShapeDtypeStruct((bsz, S5_BLKS, S5_BLK_CH), F32)],
        scratch_shapes=[rows, rows, carry, carry],
        compiler_params=_params("parallel", "arbitrary"),
        name="s5_scan_long",
    )(z3, h0r.reshape(bsz, S5_BLKS, S5_BLK_CH), h0i.reshape(bsz, S5_BLKS, S5_BLK_CH),
      pwr[0].reshape(S5_BLKS, S5_BLK_CH), pwi[0].reshape(S5_BLKS, S5_BLK_CH), wbr, wbi, wcr, wci, d)
    return y, hr.reshape(bsz, S5_GROUPS, S5_STATE), hi.reshape(bsz, S5_GROUPS, S5_STATE)


def _s5_block_weights(bbr_t, bbi_t, c_re, c_im):
    eye = jnp.eye(S5_BLK_GROUPS, dtype=F32)

    def wb(bt):
        b4 = bt.reshape(S5_GROUP, S5_BLKS, S5_BLK_GROUPS, S5_STATE)
        w = jnp.einsum("cbgp,hg->bhcgp", b4, eye)
        return w.reshape(S5_BLKS, LANES, S5_BLK_CH).astype(BF16)

    def wc(c):
        c4 = c.reshape(S5_BLKS, S5_BLK_GROUPS, S5_GROUP, S5_STATE)
        w = jnp.einsum("bgcp,hg->bhpgc", c4, eye)
        return w.reshape(S5_BLKS, S5_BLK_CH, LANES).astype(BF16)

    return wb(bbr_t), wb(bbi_t), wc(c_re), wc(c_im)


def _hg_lb_kernel(x_ref, o_ref):
    x = x_ref[...]
    e = jnp.exp(x - jnp.max(x, axis=0, keepdims=True))
    sm = e / jnp.sum(e, axis=0, keepdims=True)
    acc = sm[0:1]
    o_ref[0:1, :] = acc
    for l in range(1, x.shape[0]):
        acc = acc + sm[l:l + 1]
        o_ref[l:l + 1, :] = acc


def hg_lower_bounds(hg_lb):
    return pl.pallas_call(_hg_lb_kernel, out_shape=jax.ShapeDtypeStruct(hg_lb.shape, F32), name="hg_lb")(hg_lb)


def _cumsum_rows(x, n):
    row = lax.broadcasted_iota(jnp.int32, x.shape, 1)
    d = 1
    while d < n:
        x = x + jnp.where(row >= d, pltpu.roll(x, d, axis=1), 0.0)
        d *= 2
    return x


def _hgrn_kernel(q_ref, f_ref, i_ref, g_ref, s0_ref, lb_ref, nw_ref, o_ref, sf_ref, st_scr, *, nb, hh, chunk, n_sub):
    step = pl.program_id(2)

    def units(x):
        return jnp.concatenate([x[:, :, h * LANES:(h + 1) * LANES] for h in range(hh)], axis=0)

    lb = units(jnp.broadcast_to(lb_ref[...][None], (nb, 1, hh * LANES)))
    nw = nw_ref[...][None]

    @pl.when(step == 0)
    def _():
        for h in range(hh):
            for b in range(nb):
                st_scr[h * nb + b] = s0_ref[b, h].T

    trow = lax.broadcasted_iota(jnp.int32, (chunk, chunk), 0)
    tcol = lax.broadcasted_iota(jnp.int32, (chunk, chunk), 1)
    causal = (tcol <= trow)[None]
    st = st_scr[...]
    for c in range(n_sub):
        rows = slice(c * chunk, (c + 1) * chunk)
        q, f, v, g = (units(ref[:, rows, :]) for ref in (q_ref, f_ref, i_ref, g_ref))
        fg = lb + (1.0 - lb) * jax.nn.sigmoid(f)
        qh = jax.nn.silu(q)
        kh = 1.0 - fg
        bcum = _cumsum_rows(jnp.log(fg), chunk)
        btot = bcum[:, chunk - 1:chunk, :]
        q_in = (qh * jnp.exp(bcum)).astype(BF16)
        k_in = (kh * jnp.exp(-bcum)).astype(BF16)
        k_end = (kh * jnp.exp(btot - bcum)).astype(BF16)
        decay = jnp.exp(btot)
        vb = v.astype(BF16)
        att = jnp.einsum("utk,usk->uts", q_in, k_in, preferred_element_type=F32)
        att = jnp.where(causal, att, 0.0).astype(BF16)
        out = (jnp.einsum("utk,uvk->utv", q_in, st.astype(BF16), preferred_element_type=F32)
               + jnp.einsum("uts,usv->utv", att, vb, preferred_element_type=F32))
        st = st * decay + jnp.einsum("usv,usk->uvk", vb, k_end, preferred_element_type=F32)
        out = out * lax.rsqrt(jnp.mean(out * out, axis=-1, keepdims=True) + EPS) * nw
        out = (out * jax.nn.silu(g)).astype(o_ref.dtype)
        for h in range(hh):
            o_ref[:, rows, h * LANES:(h + 1) * LANES] = out[h * nb:(h + 1) * nb]
    st_scr[...] = st

    @pl.when(step == pl.num_programs(2) - 1)
    def _():
        for h in range(hh):
            for b in range(nb):
                sf_ref[b, h] = st[h * nb + b].T


def hgrn2(z3, s0, lb, norm_w, nb, hh):
    bsz, seq, _ = z3.shape
    chunk = min(HG_CHUNK, seq)
    n_sub = next(n for n in (3, 2, 1) if seq % (n * chunk) == 0)
    rows = n_sub * chunk
    assert bsz % nb == 0 and HG_HEADS % hh == 0
    wid = hh * LANES
    n_col = (HG_HEADS * HG_K) // wid

    def col(proj):
        return pl.BlockSpec((nb, rows, wid), lambda h, b, t, proj=proj: (b, t, proj * n_col + h))

    st_blk = pl.BlockSpec((nb, hh, HG_K, HG_V), lambda h, b, t: (b, h, 0, 0))
    out, sf = pl.pallas_call(
        functools.partial(_hgrn_kernel, nb=nb, hh=hh, chunk=chunk, n_sub=n_sub),
        grid=(HG_HEADS // hh, bsz // nb, seq // rows),
        in_specs=[col(1), col(2), col(3), col(4), st_blk,
                  pl.BlockSpec((1, wid), lambda h, b, t: (0, h)),
                  pl.BlockSpec((1, LANES), lambda h, b, t: (0, 0))],
        out_specs=[pl.BlockSpec((nb, rows, wid), lambda h, b, t: (b, t, h)), st_blk],
        out_shape=[jax.ShapeDtypeStruct((bsz, seq, HG_HEADS * HG_V), BF16),
                   jax.ShapeDtypeStruct((bsz, HG_HEADS, HG_K, HG_V), F32)],
        scratch_shapes=[pltpu.VMEM((hh * nb, HG_V, HG_K), F32)],
        compiler_params=_params("parallel", "parallel", "arbitrary"),
        name="hgrn2",
    )(z3, z3, z3, z3, s0, lb.reshape(1, HG_HEADS * HG_K), norm_w.reshape(1, HG_V))
    return out, sf


RW_MIXES = 6


def _norm_mix_kernel(h_ref, sh_ref, w_ref, mu_ref, *refs, tl):
    o_refs = refs[:RW_MIXES]
    last_ref, scr = refs[RW_MIXES:]
    x = h_ref[...]
    xn = x * lax.rsqrt(jnp.mean(x * x, axis=-1, keepdims=True) + EPS) * w_ref[...][None]

    @pl.when(pl.program_id(1) == 0)
    def _():
        scr[:, SUBLANES - 1:SUBLANES, :] = sh_ref[...]

    scr[:, SUBLANES:, :] = xn
    xx = scr[:, SUBLANES - 1:SUBLANES - 1 + tl, :] - xn
    for j, o_ref in enumerate(o_refs):
        o_ref[...] = (xn + xx * mu_ref[j:j + 1, :][None]).astype(o_ref.dtype)
    last = xn[:, tl - 1:tl, :]
    scr[:, SUBLANES - 1:SUBLANES, :] = last
    last_ref[...] = last


def norm_mix(h3, shift0, ln_w, mu, nb, tl):
    bsz, seq, d = h3.shape
    assert bsz % nb == 0 and seq % tl == 0 and tl % SUBLANES == 0
    blk = pl.BlockSpec((nb, tl, d), lambda b, t: (b, t, 0))
    row = pl.BlockSpec((nb, 1, d), lambda b, t: (b, 0, 0))
    outs = pl.pallas_call(
        functools.partial(_norm_mix_kernel, tl=tl),
        grid=(bsz // nb, seq // tl),
        in_specs=[blk, row, pl.BlockSpec((1, d), lambda b, t: (0, 0)), pl.BlockSpec((RW_MIXES, d), lambda b, t: (0, 0))],
        out_specs=[blk] * RW_MIXES + [row],
        out_shape=[jax.ShapeDtypeStruct((bsz, seq, d), BF16)] * RW_MIXES + [jax.ShapeDtypeStruct((bsz, 1, d), F32)],
        scratch_shapes=[pltpu.VMEM((nb, SUBLANES + tl, d), F32)],
        compiler_params=_params("parallel", "arbitrary"),
        name="norm_mix",
    )(h3, shift0.reshape(bsz, 1, d), ln_w.reshape(1, d), mu)
    return outs[:RW_MIXES], outs[RW_MIXES].reshape(bsz, d)


RW_PAIR = LANES // RW_HEAD
RW_DECAY_SCALE = math.exp(-0.5)
RW_SOLVE_BLOCK = 8


def _rwkv_kernel(r_ref, k_ref, v_ref, tw_ref, ta_ref, tg_ref, w2_ref, a2_ref, g2_ref, s0_ref,
                 w0_ref, a0_ref, kk_ref, ka_ref, rk_ref, lnw_ref, lnb_ref,
                 o_ref, sf_ref, s_scr, *, nb, hp, chunk):
    nu = hp * nb
    step = pl.program_id(2)
    lane = lax.broadcasted_iota(jnp.int32, (1, 1, LANES), 2)
    head1 = lane >= RW_HEAD

    def units(x):
        return jnp.concatenate([x[:, :, p * LANES:(p + 1) * LANES] for p in range(hp)], axis=0)

    def unit_rows(ref):
        return units(jnp.broadcast_to(ref[...][None], (nb, 1, hp * LANES)))

    w0, a0, k_k, k_a, r_k, ln_w, ln_b = (unit_rows(p) for p in
                                         (w0_ref, a0_ref, kk_ref, ka_ref, rk_ref, lnw_ref, lnb_ref))
    sq_row = lax.broadcasted_iota(jnp.int32, (LANES, LANES), 0) >= RW_HEAD
    sq_col = lax.broadcasted_iota(jnp.int32, (LANES, LANES), 1) >= RW_HEAD
    same_head = sq_row == sq_col
    ones_bd = same_head.astype(BF16)

    @pl.when(step == 0)
    def _():
        zero = jnp.zeros((nb, RW_HEAD, RW_HEAD), F32)
        for p in range(hp):
            top = jnp.concatenate([s0_ref[:, RW_PAIR * p], zero], axis=-1)
            bot = jnp.concatenate([zero, s0_ref[:, RW_PAIR * p + 1]], axis=-1)
            s_scr[p * nb:(p + 1) * nb] = jnp.concatenate([top, bot], axis=1)

    def bdot(spec, a, b):
        return jnp.einsum(spec, a.astype(BF16), b.astype(BF16), preferred_element_type=F32)

    def head_sum(x, two_pass=True):
        x2 = x.reshape(nu * chunk, LANES)
        hi = x2.astype(BF16)
        s = jnp.dot(hi, ones_bd, preferred_element_type=F32)
        if two_pass:
            lo = (x2 - hi.astype(F32)).astype(BF16)
            s = s + jnp.dot(lo, ones_bd, preferred_element_type=F32)
        return s.reshape(nu, chunk, LANES)

    def low_rank(t_ref, w_ref):
        t2 = t_ref[...].reshape(nb * chunk, t_ref.shape[-1])
        return units(jnp.dot(t2, w_ref[...], preferred_element_type=F32).reshape(nb, chunk, hp * LANES))

    srow = lax.broadcasted_iota(jnp.int32, (chunk, chunk), 0)
    scol = lax.broadcasted_iota(jnp.int32, (chunk, chunk), 1)
    tri = jnp.broadcast_to((scol <= srow).astype(BF16)[None], (nu, chunk, chunk))

    def tri_sum(x):
        hi = x.astype(BF16)
        lo = (x - hi.astype(F32)).astype(BF16)
        return (jnp.einsum("uts,usc->utc", tri, hi, preferred_element_type=F32)
                + jnp.einsum("uts,usc->utc", tri, lo, preferred_element_type=F32))

    def stack_heads(x):
        return jnp.concatenate([jnp.where(head1, 0.0, x), jnp.where(head1, x, 0.0)], axis=1).astype(BF16)

    r, k, v = (units(ref[...]) for ref in (r_ref, k_ref, v_ref))
    wl, al, g = low_rank(tw_ref, w2_ref), low_rank(ta_ref, a2_ref), low_rank(tg_ref, g2_ref)
    lw = (-RW_DECAY_SCALE) * jax.nn.sigmoid(w0 + wl)
    ag = jax.nn.sigmoid(a0 + al)
    kk = k * k_k
    kk = kk * jnp.minimum(lax.rsqrt(head_sum(kk * kk)), 1e12)
    k2 = k * (1.0 + (ag - 1.0) * k_a)
    cl = tri_sum(lw)
    e_pos = jnp.exp(cl)
    e_neg = jnp.exp(-cl)
    at = (-kk) * jnp.exp(cl - lw)
    bt = (kk * ag) * e_neg
    kt = k2 * e_neg
    rt = r * e_pos
    wc = e_pos[:, chunk - 1:chunk, :]
    trow = lax.broadcasted_iota(jnp.int32, (chunk, RW_PAIR * chunk), 0)
    tcol = lax.broadcasted_iota(jnp.int32, (chunk, RW_PAIR * chunk), 1)
    tcol = jnp.where(tcol >= chunk, tcol - chunk, tcol)
    strict = (tcol < trow)[None]
    incl = (tcol <= trow)[None]
    x2 = jnp.concatenate([at, rt], axis=1)
    pb = bdot("utc,usc->uts", x2, stack_heads(bt))
    pk = bdot("utc,usc->uts", x2, stack_heads(kt))
    lab = jnp.where(strict, pb[:, :chunk], 0.0)
    lak = jnp.where(strict, pk[:, :chunk], 0.0)
    arb = jnp.where(incl, pb[:, chunk:], 0.0)
    ark = jnp.where(incl, pk[:, chunk:], 0.0)
    v_bd = stack_heads(v)
    xa = at
    xv = bdot("uts,usc->utc", lak, v_bd)
    sub = min(RW_SOLVE_BLOCK, chunk)
    done_a, done_v = [], []
    for lo in range(0, chunk, sub):
        xa_i = xa[:, lo:lo + sub, :]
        xv_i = xv[:, lo:lo + sub, :]
        if lo:
            pad = jnp.zeros((nu, chunk - lo, LANES), F32)
            prev = jnp.concatenate([stack_heads(jnp.concatenate(done_a + [pad], axis=1)),
                                    stack_heads(jnp.concatenate(done_v + [pad], axis=1))], axis=-1)
            upd = bdot("uts,usc->utc", lab[:, lo:lo + sub, :], prev)
            xa_i = xa_i + upd[:, :, :LANES]
            xv_i = xv_i + upd[:, :, LANES:]
        l0 = lab[:, lo:lo + sub, lo:lo + sub]
        l1 = lab[:, lo:lo + sub, chunk + lo:chunk + lo + sub]
        for s in range(sub - 1):
            m = jnp.where(head1, l1[:, :, s:s + 1], l0[:, :, s:s + 1])
            xa_i = xa_i + m * xa_i[:, s:s + 1, :]
            xv_i = xv_i + m * xv_i[:, s:s + 1, :]
        done_a.append(xa_i)
        done_v.append(xv_i)
    ah = jnp.concatenate(done_a, axis=1)
    vh = jnp.concatenate(done_v, axis=1)
    both = bdot("uts,usc->utc", arb, jnp.concatenate([stack_heads(ah), stack_heads(vh)], axis=-1))
    rh = rt + both[:, :, :LANES]
    yh = both[:, :, LANES:] + bdot("uts,usc->utc", ark, v_bd)
    gp = jnp.where(same_head, bdot("utj,utk->ujk", ah, bt), 0.0)
    ht = jnp.where(same_head, bdot("utv,utk->uvk", jnp.concatenate([vh, v], axis=1),
                                   jnp.concatenate([bt, kt], axis=1)), 0.0)
    st = s_scr[...]
    y = bdot("utk,uvk->utv", rh, st) + yh
    st = (st + bdot("uvj,ujk->uvk", st, gp) + ht) * wc
    s_scr[...] = st
    inv_n = 1.0 / RW_HEAD
    yc = y - head_sum(y, two_pass=False) * inv_n
    var = head_sum(yc * yc, two_pass=False) * inv_n
    y = yc * lax.rsqrt(var + RW_GN_EPS) * ln_w + ln_b
    y = y + head_sum(r * k2 * r_k, two_pass=False) * v
    out = (y * g).astype(o_ref.dtype)
    for p in range(hp):
        o_ref[:, :, p * LANES:(p + 1) * LANES] = out[p * nb:(p + 1) * nb]

    @pl.when(step == pl.num_programs(2) - 1)
    def _():
        for p in range(hp):
            sf_ref[:, RW_PAIR * p] = st[p * nb:(p + 1) * nb, :RW_HEAD, :RW_HEAD]
            sf_ref[:, RW_PAIR * p + 1] = st[p * nb:(p + 1) * nb, RW_HEAD:, RW_HEAD:]


def _rwkv_chunk(seq):
    for c in (48, 32, 16, 8):
        if seq % c == 0:
            return c
    raise ValueError(seq)


def rwkv7(r, k, v, low, low_w, s0, w0, a0, k_k, k_a, r_k, ln_w, ln_b, nb, hp):
    bsz, seq, d = r.shape
    chunk = _rwkv_chunk(seq)
    heads = hp * RW_PAIR
    assert bsz % nb == 0 and RW_HEADS % heads == 0
    seq_blk = pl.BlockSpec((nb, chunk, hp * LANES), lambda h, b, t: (b, t, h))
    st_blk = pl.BlockSpec((nb, heads, RW_HEAD, RW_HEAD), lambda h, b, t: (b, h, 0, 0))
    vec = pl.BlockSpec((1, hp * LANES), lambda h, b, t: (0, h))
    low_blk = [pl.BlockSpec((nb, chunk, x.shape[-1]), lambda h, b, t: (b, t, 0)) for x in low]
    low_w_blk = [pl.BlockSpec((w.shape[0], hp * LANES), lambda h, b, t: (0, h)) for w in low_w]
    out, sf = pl.pallas_call(
        functools.partial(_rwkv_kernel, nb=nb, hp=hp, chunk=chunk),
        grid=(RW_HEADS // heads, bsz // nb, seq // chunk),
        in_specs=[seq_blk] * 3 + low_blk + low_w_blk + [st_blk] + [vec] * 7,
        out_specs=[seq_blk, st_blk],
        out_shape=[jax.ShapeDtypeStruct((bsz, seq, d), BF16),
                   jax.ShapeDtypeStruct((bsz, RW_HEADS, RW_HEAD, RW_HEAD), F32)],
        scratch_shapes=[pltpu.VMEM((hp * nb, LANES, LANES), F32)],
        compiler_params=_params("parallel", "parallel", "arbitrary"),
        name="rwkv7",
    )(r, k, v, *low, *low_w, s0, *(p.reshape(1, d) for p in (w0, a0, k_k, k_a, r_k, ln_w, ln_b)))
    return out, sf


def _ffn_in_kernel(x_ref, wa_ref, wv_ref, e_ref, cw_ref, cb_ref, o_ref, st_ref, scr, *wb_refs, nb, seq, sb, sr):
    tn = wa_ref.shape[1]
    if wb_refs:
        @pl.when(pl.program_id(1) == 0)
        def _():
            wb_refs[0][...] = wa_ref[...].astype(BF16)
            wb_refs[1][...] = wv_ref[...].astype(BF16)

        wa_ref, wv_ref = wb_refs
    cw = cw_ref[...]
    cb = cb_ref[...][None]
    scr[:, SUBLANES - (CONV_W - 1):SUBLANES, :] = e_ref[...]
    for b0 in range(0, nb, sb):
        for r0 in range(0, seq, sr):
            lo = b0 * seq + r0
            x = x_ref[lo:lo + sb * sr, :]
            a = jnp.dot(x, wa_ref[...], preferred_element_type=F32).reshape(sb, sr, tn)
            v = jnp.dot(x, wv_ref[...], preferred_element_type=F32).reshape(sb, sr, tn)
            scr[b0:b0 + sb, SUBLANES + r0:SUBLANES + r0 + sr, :] = a
            c = cb + cw[CONV_W - 1:CONV_W][None] * a
            for j in range(CONV_W - 1):
                first = SUBLANES + r0 - (CONV_W - 1 - j)
                c = c + cw[j:j + 1][None] * scr[b0:b0 + sb, first:first + sr, :]
            o_ref[lo:lo + sb * sr, :] = (jax.nn.gelu(c) * v).reshape(sb * sr, tn).astype(o_ref.dtype)
    st_ref[...] = scr[:, SUBLANES + seq - (CONV_W - 1):SUBLANES + seq, :]


def ffn_in(xb, conv0, w_in, layer, conv_w, conv_b, bsz, seq, nb, tn, sub):
    t, d = xb.shape
    sb, sr = sub
    assert t == bsz * seq and bsz % nb == 0 and D_FF % tn == 0
    assert nb % sb == 0 and seq % sr == 0 and sr % SUBLANES == 0 and (sr == seq or nb == sb == 1)
    n_col = D_FF // tn
    cast_w = w_in.dtype == F32
    rc = (lambda f: lambda j, i: f(i, j)) if cast_w else (lambda f: f)
    col = rc(lambda i, j: (0, j))
    out, st = pl.pallas_call(
        functools.partial(_ffn_in_kernel, nb=nb, seq=seq, sb=sb, sr=sr),
        grid=(n_col, bsz // nb) if cast_w else (bsz // nb, n_col),
        in_specs=[pl.BlockSpec((nb * seq, d), rc(lambda i, j: (i, 0))),
                  pl.BlockSpec((None, d, tn), rc(lambda i, j: (layer, 0, j))),
                  pl.BlockSpec((None, d, tn), rc(lambda i, j: (layer, 0, j + n_col))),
                  pl.BlockSpec((nb, CONV_W - 1, tn), rc(lambda i, j: (i, 0, j))),
                  pl.BlockSpec((CONV_W, tn), col), pl.BlockSpec((1, tn), col)],
        out_specs=[pl.BlockSpec((nb * seq, tn), rc(lambda i, j: (i, j))),
                   pl.BlockSpec((nb, CONV_W - 1, tn), rc(lambda i, j: (i, 0, j)))],
        out_shape=[jax.ShapeDtypeStruct((t, D_FF), BF16),
                   jax.ShapeDtypeStruct((bsz, CONV_W - 1, D_FF), F32)],
        scratch_shapes=[pltpu.VMEM((nb, SUBLANES + seq, tn), F32)] + [pltpu.VMEM((d, tn), BF16)] * (2 * cast_w),
        compiler_params=_params("parallel", "arbitrary" if cast_w else "parallel"),
        name="ffn_in",
    )(xb, w_in, w_in, conv0, conv_w, conv_b.reshape(1, D_FF))
    return out, st


FFN_DOWN_K_STEPS = 2


def _channel_mixer(h, conv0, layer, p, cfg, bsz, seq):
    (xb,) = rmsnorm(h, p["ln_ffn"][layer], (BF16,))
    gated, n_cv = ffn_in(xb, conv0, p["ffn_w_in"], layer, p["ffn_conv_w"][layer], p["ffn_conv_b"][layer],
                         bsz, seq, cfg["ffn_nb"], cfg["ffn_tn"], cfg["ffn_sub"])
    return matmul_residual_split(gated, p["ffn_w_down"], layer, h, FFN_DOWN_K_STEPS), n_cv


def _trunk(x3, s5r, s5i, hg, rw, sh, cv, p, cfg):
    bsz, seq, d = x3.shape
    seq += cfg["front"]
    t = bsz * seq

    if cfg["front"]:
        h, xb = embed_norm(x3, p["meta"], p["ln_mix"][0])
    else:
        h = x3.reshape(t, d)
        (xb,) = rmsnorm(h, p["ln_mix"][0], (BF16,))
    z = matmul([xb], [p["ev_w_in"]], F32).reshape(bsz, seq, EVEN_IN)
    if cfg["s5_tl"]:
        ys5, n_s5r, n_s5i = s5_scan_long(z, s5r[0], s5i[0], *p["s5"], tl=cfg["s5_tl"])
    else:
        ys5, n_s5r, n_s5i = s5_scan(z, s5r[0], s5i[0], *p["s5"], nb=cfg["s5_nb"])
    ys5 = ys5.reshape(t, S5_WIDTH)
    ya = matmul([ys5], [p["s5_w_glu"]], BF16, epilogue="glu", extra=ys5)
    yb, n_hg = hgrn2(z, hg[0], p["hg_lb"], p["hg_norm_w"], nb=cfg["hg_nb"], hh=cfg["hg_hh"])
    h = matmul([ya, yb.reshape(t, -1)], [(p["ev_w_out"], 0, 0), (p["ev_w_out"], 0, 1)], F32,
               epilogue="residual", extra=h)
    h, n_cv0 = _channel_mixer(h, cv[0], 0, p, cfg, bsz, seq)

    mixes, n_sh = norm_mix(h.reshape(bsz, seq, d), sh[0], p["ln_mix"][1], p["rw_mu"], *cfg["mix_blk"])
    xr, xw, xk, xv, xa, xg = (m.reshape(t, d) for m in mixes)
    r = matmul([xr], [p["rw_w_r"]], F32)
    k = matmul([xk], [p["rw_w_k"]], F32)
    v = matmul([xv], [p["rw_w_v"]], F32)
    low = (matmul([xw], [p["rw_w1"]], BF16, act="tanh"), matmul([xa], [p["rw_a1"]], BF16),
           matmul([xg], [p["rw_g1"]], BF16, act="sigmoid"))
    as3 = lambda a: a.reshape(bsz, seq, a.shape[-1])
    yo, n_rw = rwkv7(as3(r), as3(k), as3(v), [as3(x) for x in low], (p["rw_w2"], p["rw_a2"], p["rw_g2"]),
                     rw[0], *p["rw_vec"], nb=cfg["rw_nb"], hp=cfg["rw_hp"])
    h = matmul([yo.reshape(t, d)], [p["rw_w_o"]], F32, epilogue="residual", extra=h)
    h, n_cv1 = _channel_mixer(h, cv[1], 1, p, cfg, bsz, seq)

    (y,) = rmsnorm(h, p["ln_final"], (F32,), rows=(bsz, seq, cfg["front"]) if cfg["front"] else None)
    return (y.reshape(bsz, seq - cfg["front"], d), n_s5r[None], n_s5i[None], n_hg[None], n_rw[None], n_sh[None],
            jnp.stack([n_cv0, n_cv1]))


PROMPT_CFG = dict(front=N_META, s5_tl=344, s5_nb=None, hg_nb=4, hg_hh=8, mix_blk=(1, 344), rw_nb=4, rw_hp=16, ffn_nb=1, ffn_tn=512, ffn_sub=(1, 688))
SAMPLE_CFG = dict(front=0, s5_tl=None, s5_nb=128, hg_nb=8, hg_hh=8, mix_blk=(32, 8), rw_nb=16, rw_hp=4, ffn_nb=128, ffn_tn=512, ffn_sub=(32, 8))


def kernel(x_prompt, x_sample, state_s5_re, state_s5_im, state_hgrn, state_rwkv, state_shift, state_conv, meta_tokens, ln_mix, ln_ffn, ln_final, ev_w_in, ev_w_out, s5_lam_re, s5_lam_im, s5_log_step, s5_b_re, s5_b_im, s5_c_re, s5_c_im, s5_d, s5_w_glu, hg_lb, hg_norm_w, rw_mu, rw_w0, rw_w1, rw_w2, rw_a0, rw_a1, rw_a2, rw_g1, rw_g2, rw_k_k, rw_k_a, rw_r_k, rw_w_r, rw_w_k, rw_w_v, rw_w_o, rw_ln_w, rw_ln_b, ffn_w_in, ffn_conv_w, ffn_conv_b, ffn_w_down):
    bf = lambda w: w.astype(BF16)
    lb_all = hg_lower_bounds(hg_lb)
    pwr, pwi, bbr_t, bbi_t = s5_prep(s5_lam_re[0], s5_lam_im[0], s5_log_step[0], s5_b_re[0], s5_b_im[0])
    wbr, wbi, wcr, wci = _s5_block_weights(bbr_t, bbi_t, s5_c_re[0], s5_c_im[0])
    p = {
        "meta": meta_tokens, "ln_mix": ln_mix, "ln_ffn": ln_ffn, "ln_final": ln_final,
        "ev_w_in": ev_w_in[0],
        "ev_w_out": ev_w_out,
        "s5": (pwr, pwi, wbr, wbi, wcr, wci, s5_d[0].reshape(1, S5_WIDTH)),
        "s5_w_glu": s5_w_glu[0],
        "hg_lb": lb_all[0], "hg_norm_w": hg_norm_w[0],
        "rw_mu": rw_mu[0],
        "rw_w1": rw_w1[0], "rw_w2": bf(rw_w2[0]), "rw_a1": rw_a1[0], "rw_a2": bf(rw_a2[0]),
        "rw_g1": rw_g1[0], "rw_g2": bf(rw_g2[0]),
        "rw_w_r": rw_w_r[0], "rw_w_k": rw_w_k[0], "rw_w_v": rw_w_v[0], "rw_w_o": rw_w_o[0],
        "rw_vec": (rw_w0[0], rw_a0[0], rw_k_k[0], rw_k_a[0], rw_r_k[0].reshape(D_MODEL), rw_ln_w[0], rw_ln_b[0]),
        "ffn_w_in": ffn_w_in, "ffn_conv_w": ffn_conv_w, "ffn_conv_b": ffn_conv_b,
        "ffn_w_down": bf(ffn_w_down),
    }

    bsz = x_prompt.shape[0]
    zeros = lambda *s: jnp.zeros(s, F32)
    outs_p = _trunk(x_prompt,
                    zeros(1, bsz, S5_GROUPS, S5_STATE), zeros(1, bsz, S5_GROUPS, S5_STATE),
                    zeros(1, bsz, HG_HEADS, HG_K, HG_V), zeros(1, bsz, RW_HEADS, RW_HEAD, RW_HEAD),
                    zeros(1, bsz, D_MODEL), zeros(2, bsz, CONV_W - 1, D_FF), p, PROMPT_CFG)
    outs_s = _trunk(x_sample, state_s5_re, state_s5_im, state_hgrn, state_rwkv, state_shift, state_conv,
                    p, SAMPLE_CFG)
    return tuple(outs_p[:1]) + tuple(outs_s[:1]) + tuple(outs_p[1:]) + tuple(outs_s[1:])
```

```python
import functools
import math

import jax
import jax.numpy as jnp
from jax import lax
from jax.experimental import pallas as pl
from jax.experimental.pallas import tpu as pltpu

F32 = jnp.float32
BF16 = jnp.bfloat16

D_MODEL = 2048
N_META = 16
EPS = 1e-6
S5_WIDTH = 1024
S5_GROUP = 16
S5_GROUPS = 64
S5_STATE = 64
S5_CH = S5_GROUPS * S5_STATE
HG_HEADS = 8
HG_K = 128
HG_V = 128
HG_CHUNK = 16
EVEN_IN = 5120
RW_HEAD = 64
RW_HEADS = 32
RW_GN_EPS = 64e-5
D_FF = 5632
CONV_W = 3

LANES = 128
SUBLANES = 8
VMEM_LIMIT = 56 * 1024 * 1024


def _params(*sem):
    return pltpu.CompilerParams(dimension_semantics=sem, vmem_limit_bytes=VMEM_LIMIT)


def _row_tile(t, cap=1024):
    best = None
    for d in range(16, min(t, cap) + 1, 16):
        if t % d == 0:
            best = d
    assert best is not None, t
    return best


def _rms_kernel(x_ref, w_ref, *o_refs):
    x = x_ref[...]
    y = x * lax.rsqrt(jnp.mean(x * x, axis=-1, keepdims=True) + EPS) * w_ref[...]
    for o_ref in o_refs:
        o_ref[...] = y.astype(o_ref.dtype)


def rmsnorm(x, w, dtypes, rows=None):
    t, d = x.shape
    if rows is None:
        tm = _row_tile(t)
        n_out = t
        grid = (t // tm,)
        in_spec = pl.BlockSpec((tm, d), lambda i: (i, 0))
        out_spec = in_spec
        vec = pl.BlockSpec((1, d), lambda i: (0, 0))
    else:
        bsz, seq, front = rows
        keep = seq - front
        assert t == bsz * seq and front % 16 == 0
        tm = _row_tile(keep)
        per = keep // tm
        n_out = bsz * keep
        grid = (bsz, per)
        in_spec = pl.BlockSpec((pl.Element(tm), pl.Element(d)),
                               lambda b, i: (pl.multiple_of(b * seq + front + i * tm, 16), 0))
        out_spec = pl.BlockSpec((tm, d), lambda b, i: (b * per + i, 0))
        vec = pl.BlockSpec((1, d), lambda b, i: (0, 0))
    outs = pl.pallas_call(
        _rms_kernel,
        grid=grid,
        in_specs=[in_spec, vec],
        out_specs=[out_spec for _ in dtypes],
        out_shape=[jax.ShapeDtypeStruct((n_out, d), dt) for dt in dtypes],
        compiler_params=_params(*(["parallel"] * len(grid))),
        name="rmsnorm",
    )(x, w.reshape(1, d))
    return outs


def _embed_norm_kernel(x_ref, m_ref, w_ref, h_ref, xb_ref, *, front):
    x = x_ref[...]
    first = pl.program_id(1) == 0
    body = jnp.where(first, pltpu.roll(x, front, axis=0), x)
    head = jnp.where(first, m_ref[...], x[:front])
    rows = jnp.concatenate([head, body[front:]], axis=0)
    h_ref[...] = rows
    y = rows * lax.rsqrt(jnp.mean(rows * rows, axis=-1, keepdims=True) + EPS) * w_ref[...]
    xb_ref[...] = y.astype(xb_ref.dtype)


def embed_norm(x3, meta, w):
    bsz, seq, d = x3.shape
    front = meta.shape[0]
    total = seq + front
    tm = _row_tile(total, cap=512)
    per = total // tm
    assert front % 16 == 0 and tm > front
    blk = pl.BlockSpec((tm, d), lambda b, j: (b * per + j, 0))
    return pl.pallas_call(
        functools.partial(_embed_norm_kernel, front=front),
        grid=(bsz, per),
        in_specs=[pl.BlockSpec((pl.Element(tm), pl.Element(d)),
                               lambda b, j: (pl.multiple_of(b * seq + jnp.maximum(j * tm - front, 0), 16), 0)),
                  pl.BlockSpec((front, d), lambda b, j: (0, 0)),
                  pl.BlockSpec((1, d), lambda b, j: (0, 0))],
        out_specs=[blk, blk],
        out_shape=[jax.ShapeDtypeStruct((bsz * total, d), F32), jax.ShapeDtypeStruct((bsz * total, d), BF16)],
        compiler_params=_params("parallel", "parallel"),
        name="embed_norm",
    )(x3.reshape(bsz * seq, d), meta, w.reshape(1, d))


def _act(x, act):
    if act == "tanh":
        return jnp.tanh(x)
    if act == "sigmoid":
        return jax.nn.sigmoid(x)
    assert act is None
    return x


def _mm_kernel(*refs, n_a, act, epilogue, cast_w):
    a_refs = refs[:n_a]
    w_refs = refs[n_a:2 * n_a]
    rest = refs[2 * n_a:]
    if cast_w:
        rest, wb_refs = rest[:-n_a], rest[-n_a:]

        @pl.when(pl.program_id(1) == 0)
        def _():
            for w_ref, wb_ref in zip(w_refs, wb_refs):
                wb_ref[...] = w_ref[...].astype(BF16)

        w_refs = wb_refs
    o_ref = rest[-1]
    acc = jnp.dot(a_refs[0][...].astype(BF16), w_refs[0][...], preferred_element_type=F32)
    for a_ref, w_ref in zip(a_refs[1:], w_refs[1:]):
        acc = acc + jnp.dot(a_ref[...].astype(BF16), w_ref[...], preferred_element_type=F32)
    acc = _act(acc, act)
    if epilogue == "residual":
        acc = rest[0][...] + acc
    elif epilogue == "glu":
        acc = rest[0][...] * jax.nn.sigmoid(acc)
    o_ref[...] = acc.astype(o_ref.dtype)


MM_VMEM_BUDGET = 40 * 1024 * 1024
MM_CAST_VMEM_BUDGET = 48 * 1024 * 1024
MM_MAX_A_BLOCK = 6 * 1024 * 1024
MXU_WIDTH = 256


def _mm_tiles(t, a_row_bytes, w_col_bytes, n, out_bytes, has_extra, budget=MM_VMEM_BUDGET):
    rows = [d for d in range(16, t + 1, 16) if t % d == 0 and d * a_row_bytes <= MM_MAX_A_BLOCK]
    cols = [d for d in range(LANES, n + 1, LANES) if n % d == 0] or [n]
    best, best_score = None, -1.0
    for tm in rows:
        for tn in cols:
            est = 2 * tm * a_row_bytes + w_col_bytes * tn + tm * tn * (2 * out_bytes + 4 + (8 if has_extra else 0))
            if est > budget:
                continue
            score = tm * tn * (1.0 if tn % MXU_WIDTH == 0 else 0.8)
            if score > best_score:
                best, best_score = (tm, tn), score
    assert best is not None, (t, a_row_bytes, w_col_bytes, n)
    return best


def matmul(a_list, w_list, out_dtype, act=None, epilogue=None, extra=None):
    t = a_list[0].shape[0]
    w0 = w_list[0][0] if isinstance(w_list[0], tuple) else w_list[0]
    n = w0.shape[-1]
    cast_w = w0.dtype == F32
    k_rows = sum(a.shape[1] for a in a_list)
    a_row_bytes = sum(a.shape[1] * a.dtype.itemsize for a in a_list)
    w_col_bytes = k_rows * (2 * 4 + 2 if cast_w else 2 * 2)
    tm, tn = _mm_tiles(t, a_row_bytes, w_col_bytes, n, jnp.dtype(out_dtype).itemsize, epilogue is not None,
                       budget=MM_CAST_VMEM_BUDGET if cast_w else MM_VMEM_BUDGET)
    rc = (lambda f: lambda j, i: f(i, j)) if cast_w else (lambda f: f)
    in_specs = [pl.BlockSpec((tm, a.shape[1]), rc(lambda i, j: (i, 0))) for a in a_list]
    args = list(a_list)
    for a, w in zip(a_list, w_list):
        if isinstance(w, tuple):
            w, layer, kblk = w
            in_specs.append(pl.BlockSpec((None, a.shape[1], tn),
                                         rc(lambda i, j, layer=layer, kblk=kblk: (layer, kblk, j))))
        else:
            in_specs.append(pl.BlockSpec((w.shape[0], tn), rc(lambda i, j: (0, j))))
        args.append(w)
    if epilogue is not None:
        in_specs.append(pl.BlockSpec((tm, tn), rc(lambda i, j: (i, j))))
        args.append(extra)
    return pl.pallas_call(
        functools.partial(_mm_kernel, n_a=len(a_list), act=act, epilogue=epilogue, cast_w=cast_w),
        grid=(n // tn, t // tm) if cast_w else (t // tm, n // tn),
        in_specs=in_specs,
        out_specs=pl.BlockSpec((tm, tn), rc(lambda i, j: (i, j))),
        out_shape=jax.ShapeDtypeStruct((t, n), out_dtype),
        scratch_shapes=[pltpu.VMEM((a.shape[1], tn), BF16) for a in a_list] if cast_w else [],
        compiler_params=_params("parallel", "arbitrary" if cast_w else "parallel"),
        name="matmul",
    )(*args)


def _mm_resid_split_kernel(a_ref, w_ref, res_ref, o_ref):
    part = jnp.dot(a_ref[...], w_ref[...], preferred_element_type=F32)

    @pl.when(pl.program_id(2) == 0)
    def _():
        o_ref[...] = res_ref[...] + part

    @pl.when(pl.program_id(2) > 0)
    def _():
        o_ref[...] += part


def matmul_residual_split(a, w, layer, res, k_steps):
    t, k = a.shape
    n = w.shape[2]
    tk = k // k_steps
    assert k % k_steps == 0 and tk % LANES == 0
    tm, tn = _mm_tiles(t, 2 * tk, 4 * tk, n, 4, True)
    return pl.pallas_call(
        _mm_resid_split_kernel,
        grid=(t // tm, n // tn, k_steps),
        in_specs=[pl.BlockSpec((tm, tk), lambda i, j, s: (i, s)),
                  pl.BlockSpec((None, tk, tn), lambda i, j, s: (layer, s, j)),
                  pl.BlockSpec((tm, tn), lambda i, j, s: (i, j))],
        out_specs=pl.BlockSpec((tm, tn), lambda i, j, s: (i, j)),
        out_shape=jax.ShapeDtypeStruct((t, n), F32),
        compiler_params=_params("parallel", "parallel", "arbitrary"),
        name="matmul_split",
    )(a, w, res)


def _s5_prep_kernel(lr_ref, li_ref, ls_ref, brt_ref, bit_ref, pwr_ref, pwi_ref, bbr_ref, bbi_ref):
    lr = jnp.minimum(lr_ref[...], -1e-4)
    li = li_ref[...]
    dt = jnp.exp(ls_ref[...])
    n = lax.broadcasted_iota(jnp.int32, (SUBLANES, S5_CH), 0).astype(F32) + 1.0
    mag = jnp.exp(n * (lr * dt))
    ang = n * (li * dt)
    pwr = mag * jnp.cos(ang)
    pwi = mag * jnp.sin(ang)
    pwr_ref[...] = pwr
    pwi_ref[...] = pwi
    ar = pwr[0:1]
    ai = pwi[0:1]
    den = lr * lr + li * li
    zr = ((ar - 1.0) * lr + ai * li) / den
    zi = (ai * lr - (ar - 1.0) * li) / den
    br = brt_ref[...]
    bi = bit_ref[...]
    bbr_ref[...] = zr * br - zi * bi
    bbi_ref[...] = zr * bi + zi * br


def s5_prep(lam_re, lam_im, log_step, b_re, b_im):
    lr = lam_re.reshape(1, S5_CH)
    li = lam_im.reshape(1, S5_CH)
    ls = jnp.broadcast_to(log_step[:, None], (S5_GROUPS, S5_STATE)).reshape(1, S5_CH)
    brt = b_re.reshape(S5_CH, S5_GROUP).T
    bit = b_im.reshape(S5_CH, S5_GROUP).T
    return pl.pallas_call(
        _s5_prep_kernel,
        out_shape=[jax.ShapeDtypeStruct((SUBLANES, S5_CH), F32)] * 2
        + [jax.ShapeDtypeStruct((S5_GROUP, S5_CH), F32)] * 2,
        name="s5_prep",
    )(lr, li, ls, brt, bit)


S5_BLK_GROUPS = LANES // S5_GROUP
S5_BLKS = S5_WIDTH // LANES
S5_BLK_CH = S5_BLK_GROUPS * S5_STATE


def _cmul_add(xr, xi, mr, mi, sr, si):
    return xr + mr * sr - mi * si, xi + mr * si + mi * sr


def _s5_kernel(u_ref, h0r_ref, h0i_ref, pwr_ref, pwi_ref, wbr_ref, wbi_ref, wcr_ref, wci_ref, d_ref,
               y_ref, hr_ref, hi_ref, xr_scr, xi_scr, *, nb, seq):
    u2 = u_ref[...].reshape(nb * seq, LANES)
    ub = u2.astype(BF16)
    xr_scr[...] = jnp.dot(ub, wbr_ref[0], preferred_element_type=F32).reshape(nb, seq, S5_BLK_CH)
    xi_scr[...] = jnp.dot(ub, wbi_ref[0], preferred_element_type=F32).reshape(nb, seq, S5_BLK_CH)

    pwr = pwr_ref[...]
    pwi = pwi_ref[...]
    row = lax.broadcasted_iota(jnp.int32, (SUBLANES, S5_BLK_CH), 0)
    steps = []
    for d in (1, 2, 4):
        keep = row >= d
        steps.append((d, jnp.where(keep, pwr[d - 1:d], 0.0)[None], jnp.where(keep, pwi[d - 1:d], 0.0)[None]))
    pr = pwr[None]
    pi = pwi[None]

    def tile(i, carry):
        cr, ci = carry
        o = pl.multiple_of(i * SUBLANES, SUBLANES)
        xr = xr_scr[:, pl.ds(o, SUBLANES), :]
        xi = xi_scr[:, pl.ds(o, SUBLANES), :]
        for d, mr, mi in steps:
            sr = pltpu.roll(xr, d, axis=1)
            si = pltpu.roll(xi, d, axis=1)
            xr, xi = _cmul_add(xr, xi, mr, mi, sr, si)
        xr, xi = _cmul_add(xr, xi, pr, pi, cr, ci)
        xr_scr[:, pl.ds(o, SUBLANES), :] = xr
        xi_scr[:, pl.ds(o, SUBLANES), :] = xi
        return xr[:, SUBLANES - 1:SUBLANES, :], xi[:, SUBLANES - 1:SUBLANES, :]

    hr, hi = lax.fori_loop(0, seq // SUBLANES, tile, (h0r_ref[...], h0i_ref[...]))
    hr_ref[...] = hr
    hi_ref[...] = hi

    xr = xr_scr[...].reshape(nb * seq, S5_BLK_CH).astype(BF16)
    xi = xi_scr[...].reshape(nb * seq, S5_BLK_CH).astype(BF16)
    y = (jnp.dot(xr, wcr_ref[0], preferred_element_type=F32)
         - jnp.dot(xi, wci_ref[0], preferred_element_type=F32)
         + d_ref[...] * u2)
    y_ref[...] = jax.nn.gelu(y).reshape(nb, seq, LANES)


def s5_scan(z3, h0r, h0i, pwr, pwi, wbr, wbi, wcr, wci, d, nb):
    bsz, seq, _ = z3.shape
    assert seq % SUBLANES == 0 and bsz % nb == 0
    seq_blk = pl.BlockSpec((nb, seq, LANES), lambda b, k: (b, 0, k))
    st_blk = pl.BlockSpec((nb, 1, S5_BLK_CH), lambda b, k: (b, 0, k))
    pw_blk = pl.BlockSpec((SUBLANES, S5_BLK_CH), lambda b, k: (0, k))
    wb_blk = pl.BlockSpec((1, LANES, S5_BLK_CH), lambda b, k: (k, 0, 0))
    wc_blk = pl.BlockSpec((1, S5_BLK_CH, LANES), lambda b, k: (k, 0, 0))
    y, hr, hi = pl.pallas_call(
        functools.partial(_s5_kernel, nb=nb, seq=seq),
        grid=(bsz // nb, S5_BLKS),
        in_specs=[seq_blk, st_blk, st_blk, pw_blk, pw_blk, wb_blk, wb_blk, wc_blk, wc_blk,
                  pl.BlockSpec((1, LANES), lambda b, k: (0, k))],
        out_specs=[seq_blk, st_blk, st_blk],
        out_shape=[jax.ShapeDtypeStruct((bsz, seq, S5_WIDTH), F32),
                   jax.ShapeDtypeStruct((bsz, 1, S5_CH), F32),
                   jax.ShapeDtypeStruct((bsz, 1, S5_CH), F32)],
        scratch_shapes=[pltpu.VMEM((nb, seq, S5_BLK_CH), F32), pltpu.VMEM((nb, seq, S5_BLK_CH), F32)],
        compiler_params=_params("parallel", "parallel"),
        name="s5_scan",
    )(z3, h0r.reshape(bsz, 1, S5_CH), h0i.reshape(bsz, 1, S5_CH), pwr, pwi, wbr, wbi, wcr, wci, d)
    return y, hr.reshape(bsz, S5_GROUPS, S5_STATE), hi.reshape(bsz, S5_GROUPS, S5_STATE)


S5_LANE_TILES = S5_BLK_CH // LANES


def _s5_long_kernel(u_ref, h0r_ref, h0i_ref, ar_ref, ai_ref, wbr_ref, wbi_ref, wcr_ref, wci_ref, d_ref,
                    y_ref, hr_ref, hi_ref, xr_scr, xi_scr, cr_scr, ci_scr, *, tl):
    step = pl.program_id(1)

    @pl.when(step == 0)
    def _():
        cr_scr[...] = h0r_ref[0]
        ci_scr[...] = h0i_ref[0]

    for k in range(S5_BLKS):
        ub = u_ref[0, :, k * LANES:(k + 1) * LANES].astype(BF16)
        bur = jnp.dot(ub, wbr_ref[k], preferred_element_type=F32)
        bui = jnp.dot(ub, wbi_ref[k], preferred_element_type=F32)
        for j in range(S5_LANE_TILES):
            xr_scr[j, k * tl:(k + 1) * tl, :] = bur[:, j * LANES:(j + 1) * LANES]
            xi_scr[j, k * tl:(k + 1) * tl, :] = bui[:, j * LANES:(j + 1) * LANES]

    ar = [ar_ref[:, j * LANES:(j + 1) * LANES] for j in range(S5_LANE_TILES)]
    ai = [ai_ref[:, j * LANES:(j + 1) * LANES] for j in range(S5_LANE_TILES)]

    def token(t, carry):
        cr, ci = carry
        nr, ni = [], []
        for j in range(S5_LANE_TILES):
            rows = pl.ds(t, S5_BLKS, stride=tl)
            xr, xi = _cmul_add(xr_scr[j, rows, :], xi_scr[j, rows, :], ar[j], ai[j], cr[j], ci[j])
            xr_scr[j, rows, :] = xr
            xi_scr[j, rows, :] = xi
            nr.append(xr)
            ni.append(xi)
        return tuple(nr), tuple(ni)

    init = (tuple(cr_scr[:, j * LANES:(j + 1) * LANES] for j in range(S5_LANE_TILES)),
            tuple(ci_scr[:, j * LANES:(j + 1) * LANES] for j in range(S5_LANE_TILES)))
    cr, ci = lax.fori_loop(0, tl, token, init, unroll=8)
    cr = jnp.concatenate(cr, axis=-1)
    ci = jnp.concatenate(ci, axis=-1)
    cr_scr[...] = cr
    ci_scr[...] = ci
    hr_ref[0] = cr
    hi_ref[0] = ci

    for k in range(S5_BLKS):
        xr = jnp.concatenate([xr_scr[j, k * tl:(k + 1) * tl, :] for j in range(S5_LANE_TILES)], axis=-1)
        xi = jnp.concatenate([xi_scr[j, k * tl:(k + 1) * tl, :] for j in range(S5_LANE_TILES)], axis=-1)
        u = u_ref[0, :, k * LANES:(k + 1) * LANES]
        y = (jnp.dot(xr.astype(BF16), wcr_ref[k], preferred_element_type=F32)
             - jnp.dot(xi.astype(BF16), wci_ref[k], preferred_element_type=F32)
             + d_ref[:, k * LANES:(k + 1) * LANES] * u)
        y_ref[0, :, k * LANES:(k + 1) * LANES] = jax.nn.gelu(y)


def s5_scan_long(z3, h0r, h0i, pwr, pwi, wbr, wbi, wcr, wci, d, tl):
    bsz, seq, _ = z3.shape
    assert seq % tl == 0 and tl % SUBLANES == 0
    seq_blk = pl.BlockSpec((1, tl, S5_WIDTH), lambda b, t: (b, t, 0))
    st_blk = pl.BlockSpec((1, S5_BLKS, S5_BLK_CH), lambda b, t: (b, 0, 0))
    lam_blk = pl.BlockSpec((S5_BLKS, S5_BLK_CH), lambda b, t: (0, 0))
    wb_blk = pl.BlockSpec((S5_BLKS, LANES, S5_BLK_CH), lambda b, t: (0, 0, 0))
    wc_blk = pl.BlockSpec((S5_BLKS, S5_BLK_CH, LANES), lambda b, t: (0, 0, 0))
    rows = pltpu.VMEM((S5_LANE_TILES, S5_BLKS * tl, LANES), F32)
    carry = pltpu.VMEM((S5_BLKS, S5_BLK_CH), F32)
    y, hr, hi = pl.pallas_call(
        functools.partial(_s5_long_kernel, tl=tl),
        grid=(bsz, seq // tl),
        in_specs=[seq_blk, st_blk, st_blk, lam_blk, lam_blk, wb_blk, wb_blk, wc_blk, wc_blk,
                  pl.BlockSpec((1, S5_WIDTH), lambda b, t: (0, 0))],
        out_specs=[seq_blk, st_blk, st_blk],
        out_shape=[jax.ShapeDtypeStruct((bsz, seq, S5_WIDTH), F32),
                   jax.ShapeDtypeStruct((bsz, S5_BLKS, S5_BLK_CH), F32),
                   jax.ShapeDtypeStruct((bsz, S5_BLKS, S5_BLK_CH), F32)],
        scratch_shapes=[rows, rows, carry, carry],
        compiler_params=_params("parallel", "arbitrary"),
        name="s5_scan_long",
    )(z3, h0r.reshape(bsz, S5_BLKS, S5_BLK_CH), h0i.reshape(bsz, S5_BLKS, S5_BLK_CH),
      pwr[0].reshape(S5_BLKS, S5_BLK_CH), pwi[0].reshape(S5_BLKS, S5_BLK_CH), wbr, wbi, wcr, wci, d)
    return y, hr.reshape(bsz, S5_GROUPS, S5_STATE), hi.reshape(bsz, S5_GROUPS, S5_STATE)


def _s5_block_weights(bbr_t, bbi_t, c_re, c_im):
    eye = jnp.eye(S5_BLK_GROUPS, dtype=F32)

    def wb(bt):
        b4 = bt.reshape(S5_GROUP, S5_BLKS, S5_BLK_GROUPS, S5_STATE)
        w = jnp.einsum("cbgp,hg->bhcgp", b4, eye)
        return w.reshape(S5_BLKS, LANES, S5_BLK_CH).astype(BF16)

    def wc(c):
        c4 = c.reshape(S5_BLKS, S5_BLK_GROUPS, S5_GROUP, S5_STATE)
        w = jnp.einsum("bgcp,hg->bhpgc", c4, eye)
        return w.reshape(S5_BLKS, S5_BLK_CH, LANES).astype(BF16)

    return wb(bbr_t), wb(bbi_t), wc(c_re), wc(c_im)


def _hg_lb_kernel(x_ref, o_ref):
    x = x_ref[...]
    e = jnp.exp(x - jnp.max(x, axis=0, keepdims=True))
    sm = e / jnp.sum(e, axis=0, keepdims=True)
    acc = sm[0:1]
    o_ref[0:1, :] = acc
    for l in range(1, x.shape[0]):
        acc = acc + sm[l:l + 1]
        o_ref[l:l + 1, :] = acc


def hg_lower_bounds(hg_lb):
    return pl.pallas_call(_hg_lb_kernel, out_shape=jax.ShapeDtypeStruct(hg_lb.shape, F32), name="hg_lb")(hg_lb)


def _cumsum_rows(x, n):
    row = lax.broadcasted_iota(jnp.int32, x.shape, 1)
    d = 1
    while d < n:
        x = x + jnp.where(row >= d, pltpu.roll(x, d, axis=1), 0.0)
        d *= 2
    return x


def _hgrn_kernel(q_ref, f_ref, i_ref, g_ref, s0_ref, lb_ref, nw_ref, o_ref, sf_ref, st_scr, *, nb, hh, chunk, n_sub):
    step = pl.program_id(2)

    def units(x):
        return jnp.concatenate([x[:, :, h * LANES:(h + 1) * LANES] for h in range(hh)], axis=0)

    lb = units(jnp.broadcast_to(lb_ref[...][None], (nb, 1, hh * LANES)))
    nw = nw_ref[...][None]

    @pl.when(step == 0)
    def _():
        for h in range(hh):
            for b in range(nb):
                st_scr[h * nb + b] = s0_ref[b, h].T

    trow = lax.broadcasted_iota(jnp.int32, (chunk, chunk), 0)
    tcol = lax.broadcasted_iota(jnp.int32, (chunk, chunk), 1)
    causal = (tcol <= trow)[None]
    st = st_scr[...]
    for c in range(n_sub):
        rows = slice(c * chunk, (c + 1) * chunk)
        q, f, v, g = (units(ref[:, rows, :]) for ref in (q_ref, f_ref, i_ref, g_ref))
        fg = lb + (1.0 - lb) * jax.nn.sigmoid(f)
        qh = jax.nn.silu(q)
        kh = 1.0 - fg
        bcum = _cumsum_rows(jnp.log(fg), chunk)
        btot = bcum[:, chunk - 1:chunk, :]
        q_in = (qh * jnp.exp(bcum)).astype(BF16)
        k_in = (kh * jnp.exp(-bcum)).astype(BF16)
        k_end = (kh * jnp.exp(btot - bcum)).astype(BF16)
        decay = jnp.exp(btot)
        vb = v.astype(BF16)
        att = jnp.einsum("utk,usk->uts", q_in, k_in, preferred_element_type=F32)
        att = jnp.where(causal, att, 0.0).astype(BF16)
        out = (jnp.einsum("utk,uvk->utv", q_in, st.astype(BF16), preferred_element_type=F32)
               + jnp.einsum("uts,usv->utv", att, vb, preferred_element_type=F32))
        st = st * decay + jnp.einsum("usv,usk->uvk", vb, k_end, preferred_element_type=F32)
        out = out * lax.rsqrt(jnp.mean(out * out, axis=-1, keepdims=True) + EPS) * nw
        out = (out * jax.nn.silu(g)).astype(o_ref.dtype)
        for h in range(hh):
            o_ref[:, rows, h * LANES:(h + 1) * LANES] = out[h * nb:(h + 1) * nb]
    st_scr[...] = st

    @pl.when(step == pl.num_programs(2) - 1)
    def _():
        for h in range(hh):
            for b in range(nb):
                sf_ref[b, h] = st[h * nb + b].T


def hgrn2(z3, s0, lb, norm_w, nb, hh):
    bsz, seq, _ = z3.shape
    chunk = min(HG_CHUNK, seq)
    n_sub = next(n for n in (3, 2, 1) if seq % (n * chunk) == 0)
    rows = n_sub * chunk
    assert bsz % nb == 0 and HG_HEADS % hh == 0
    wid = hh * LANES
    n_col = (HG_HEADS * HG_K) // wid

    def col(proj):
        return pl.BlockSpec((nb, rows, wid), lambda h, b, t, proj=proj: (b, t, proj * n_col + h))

    st_blk = pl.BlockSpec((nb, hh, HG_K, HG_V), lambda h, b, t: (b, h, 0, 0))
    out, sf = pl.pallas_call(
        functools.partial(_hgrn_kernel, nb=nb, hh=hh, chunk=chunk, n_sub=n_sub),
        grid=(HG_HEADS // hh, bsz // nb, seq // rows),
        in_specs=[col(1), col(2), col(3), col(4), st_blk,
                  pl.BlockSpec((1, wid), lambda h, b, t: (0, h)),
                  pl.BlockSpec((1, LANES), lambda h, b, t: (0, 0))],
        out_specs=[pl.BlockSpec((nb, rows, wid), lambda h, b, t: (b, t, h)), st_blk],
        out_shape=[jax.ShapeDtypeStruct((bsz, seq, HG_HEADS * HG_V), BF16),
                   jax.ShapeDtypeStruct((bsz, HG_HEADS, HG_K, HG_V), F32)],
        scratch_shapes=[pltpu.VMEM((hh * nb, HG_V, HG_K), F32)],
        compiler_params=_params("parallel", "parallel", "arbitrary"),
        name="hgrn2",
    )(z3, z3, z3, z3, s0, lb.reshape(1, HG_HEADS * HG_K), norm_w.reshape(1, HG_V))
    return out, sf


RW_MIXES = 6


def _norm_mix_kernel(h_ref, sh_ref, w_ref, mu_ref, *refs, tl):
    o_refs = refs[:RW_MIXES]
    last_ref, scr = refs[RW_MIXES:]
    x = h_ref[...]
    xn = x * lax.rsqrt(jnp.mean(x * x, axis=-1, keepdims=True) + EPS) * w_ref[...][None]

    @pl.when(pl.program_id(1) == 0)
    def _():
        scr[:, SUBLANES - 1:SUBLANES, :] = sh_ref[...]

    scr[:, SUBLANES:, :] = xn
    xx = scr[:, SUBLANES - 1:SUBLANES - 1 + tl, :] - xn
    for j, o_ref in enumerate(o_refs):
        o_ref[...] = (xn + xx * mu_ref[j:j + 1, :][None]).astype(o_ref.dtype)
    last = xn[:, tl - 1:tl, :]
    scr[:, SUBLANES - 1:SUBLANES, :] = last
    last_ref[...] = last


def norm_mix(h3, shift0, ln_w, mu, nb, tl):
    bsz, seq, d = h3.shape
    assert bsz % nb == 0 and seq % tl == 0 and tl % SUBLANES == 0
    blk = pl.BlockSpec((nb, tl, d), lambda b, t: (b, t, 0))
    row = pl.BlockSpec((nb, 1, d), lambda b, t: (b, 0, 0))
    outs = pl.pallas_call(
        functools.partial(_norm_mix_kernel, tl=tl),
        grid=(bsz // nb, seq // tl),
        in_specs=[blk, row, pl.BlockSpec((1, d), lambda b, t: (0, 0)), pl.BlockSpec((RW_MIXES, d), lambda b, t: (0, 0))],
        out_specs=[blk] * RW_MIXES + [row],
        out_shape=[jax.ShapeDtypeStruct((bsz, seq, d), BF16)] * RW_MIXES + [jax.ShapeDtypeStruct((bsz, 1, d), F32)],
        scratch_shapes=[pltpu.VMEM((nb, SUBLANES + tl, d), F32)],
        compiler_params=_params("parallel", "arbitrary"),
        name="norm_mix",
    )(h3, shift0.reshape(bsz, 1, d), ln_w.reshape(1, d), mu)
    return outs[:RW_MIXES], outs[RW_MIXES].reshape(bsz, d)


RW_PAIR = LANES // RW_HEAD
RW_DECAY_SCALE = math.exp(-0.5)
RW_SOLVE_BLOCK = 8


def _rwkv_kernel(r_ref, k_ref, v_ref, tw_ref, ta_ref, tg_ref, w2_ref, a2_ref, g2_ref, s0_ref,
                 w0_ref, a0_ref, kk_ref, ka_ref, rk_ref, lnw_ref, lnb_ref,
                 o_ref, sf_ref, s_scr, *, nb, hp, chunk):
    nu = hp * nb
    step = pl.program_id(2)
    lane = lax.broadcasted_iota(jnp.int32, (1, 1, LANES), 2)
    head1 = lane >= RW_HEAD

    def units(x):
        return jnp.concatenate([x[:, :, p * LANES:(p + 1) * LANES] for p in range(hp)], axis=0)

    def unit_rows(ref):
        return units(jnp.broadcast_to(ref[...][None], (nb, 1, hp * LANES)))

    w0, a0, k_k, k_a, r_k, ln_w, ln_b = (unit_rows(p) for p in
                                         (w0_ref, a0_ref, kk_ref, ka_ref, rk_ref, lnw_ref, lnb_ref))
    sq_row = lax.broadcasted_iota(jnp.int32, (LANES, LANES), 0) >= RW_HEAD
    sq_col = lax.broadcasted_iota(jnp.int32, (LANES, LANES), 1) >= RW_HEAD
    same_head = sq_row == sq_col
    ones_bd = same_head.astype(BF16)

    @pl.when(step == 0)
    def _():
        zero = jnp.zeros((nb, RW_HEAD, RW_HEAD), F32)
        for p in range(hp):
            top = jnp.concatenate([s0_ref[:, RW_PAIR * p], zero], axis=-1)
            bot = jnp.concatenate([zero, s0_ref[:, RW_PAIR * p + 1]], axis=-1)
            s_scr[p * nb:(p + 1) * nb] = jnp.concatenate([top, bot], axis=1)

    def bdot(spec, a, b):
        return jnp.einsum(spec, a.astype(BF16), b.astype(BF16), preferred_element_type=F32)

    def head_sum(x, two_pass=True):
        x2 = x.reshape(nu * chunk, LANES)
        hi = x2.astype(BF16)
        s = jnp.dot(hi, ones_bd, preferred_element_type=F32)
        if two_pass:
            lo = (x2 - hi.astype(F32)).astype(BF16)
            s = s + jnp.dot(lo, ones_bd, preferred_element_type=F32)
        return s.reshape(nu, chunk, LANES)

    def low_rank(t_ref, w_ref):
        t2 = t_ref[...].reshape(nb * chunk, t_ref.shape[-1])
        return units(jnp.dot(t2, w_ref[...], preferred_element_type=F32).reshape(nb, chunk, hp * LANES))

    srow = lax.broadcasted_iota(jnp.int32, (chunk, chunk), 0)
    scol = lax.broadcasted_iota(jnp.int32, (chunk, chunk), 1)
    tri = jnp.broadcast_to((scol <= srow).astype(BF16)[None], (nu, chunk, chunk))

    def tri_sum(x):
        hi = x.astype(BF16)
        lo = (x - hi.astype(F32)).astype(BF16)
        return (jnp.einsum("uts,usc->utc", tri, hi, preferred_element_type=F32)
                + jnp.einsum("uts,usc->utc", tri, lo, preferred_element_type=F32))

    def stack_heads(x):
        return jnp.concatenate([jnp.where(head1, 0.0, x), jnp.where(head1, x, 0.0)], axis=1).astype(BF16)

    r, k, v = (units(ref[...]) for ref in (r_ref, k_ref, v_ref))
    wl, al, g = low_rank(tw_ref, w2_ref), low_rank(ta_ref, a2_ref), low_rank(tg_ref, g2_ref)
    lw = (-RW_DECAY_SCALE) * jax.nn.sigmoid(w0 + wl)
    ag = jax.nn.sigmoid(a0 + al)
    kk = k * k_k
    kk = kk * jnp.minimum(lax.rsqrt(head_sum(kk * kk)), 1e12)
    k2 = k * (1.0 + (ag - 1.0) * k_a)
    cl = tri_sum(lw)
    e_pos = jnp.exp(cl)
    e_neg = jnp.exp(-cl)
    at = (-kk) * jnp.exp(cl - lw)
    bt = (kk * ag) * e_neg
    kt = k2 * e_neg
    rt = r * e_pos
    wc = e_pos[:, chunk - 1:chunk, :]
    trow = lax.broadcasted_iota(jnp.int32, (chunk, RW_PAIR * chunk), 0)
    tcol = lax.broadcasted_iota(jnp.int32, (chunk, RW_PAIR * chunk), 1)
    tcol = jnp.where(tcol >= chunk, tcol - chunk, tcol)
    strict = (tcol < trow)[None]
    incl = (tcol <= trow)[None]
    x2 = jnp.concatenate([at, rt], axis=1)
    pb = bdot("utc,usc->uts", x2, stack_heads(bt))
    pk = bdot("utc,usc->uts", x2, stack_heads(kt))
    lab = jnp.where(strict, pb[:, :chunk], 0.0)
    lak = jnp.where(strict, pk[:, :chunk], 0.0)
    arb = jnp.where(incl, pb[:, chunk:], 0.0)
    ark = jnp.where(incl, pk[:, chunk:], 0.0)
    v_bd = stack_heads(v)
    xa = at
    xv = bdot("uts,usc->utc", lak, v_bd)
    sub = min(RW_SOLVE_BLOCK, chunk)
    done_a, done_v = [], []
    for lo in range(0, chunk, sub):
        xa_i = xa[:, lo:lo + sub, :]
        xv_i = xv[:, lo:lo + sub, :]
        if lo:
            pad = jnp.zeros((nu, chunk - lo, LANES), F32)
            prev = jnp.concatenate([stack_heads(jnp.concatenate(done_a + [pad], axis=1)),
                                    stack_heads(jnp.concatenate(done_v + [pad], axis=1))], axis=-1)
            upd = bdot("uts,usc->utc", lab[:, lo:lo + sub, :], prev)
            xa_i = xa_i + upd[:, :, :LANES]
            xv_i = xv_i + upd[:, :, LANES:]
        l0 = lab[:, lo:lo + sub, lo:lo + sub]
        l1 = lab[:, lo:lo + sub, chunk + lo:chunk + lo + sub]
        for s in range(sub - 1):
            m = jnp.where(head1, l1[:, :, s:s + 1], l0[:, :, s:s + 1])
            xa_i = xa_i + m * xa_i[:, s:s + 1, :]
            xv_i = xv_i + m * xv_i[:, s:s + 1, :]
        done_a.append(xa_i)
        done_v.append(xv_i)
    ah = jnp.concatenate(done_a, axis=1)
    vh = jnp.concatenate(done_v, axis=1)
    both = bdot("uts,usc->utc", arb, jnp.concatenate([stack_heads(ah), stack_heads(vh)], axis=-1))
    rh = rt + both[:, :, :LANES]
    yh = both[:, :, LANES:] + bdot("uts,usc->utc", ark, v_bd)
    gp = jnp.where(same_head, bdot("utj,utk->ujk", ah, bt), 0.0)
    ht = jnp.where(same_head, bdot("utv,utk->uvk", jnp.concatenate([vh, v], axis=1),
                                   jnp.concatenate([bt, kt], axis=1)), 0.0)
    st = s_scr[...]
    y = bdot("utk,uvk->utv", rh, st) + yh
    st = (st + bdot("uvj,ujk->uvk", st, gp) + ht) * wc
    s_scr[...] = st
    inv_n = 1.0 / RW_HEAD
    yc = y - head_sum(y, two_pass=False) * inv_n
    var = head_sum(yc * yc, two_pass=False) * inv_n
    y = yc * lax.rsqrt(var + RW_GN_EPS) * ln_w + ln_b
    y = y + head_sum(r * k2 * r_k, two_pass=False) * v
    out = (y * g).astype(o_ref.dtype)
    for p in range(hp):
        o_ref[:, :, p * LANES:(p + 1) * LANES] = out[p * nb:(p + 1) * nb]

    @pl.when(step == pl.num_programs(2) - 1)
    def _():
        for p in range(hp):
            sf_ref[:, RW_PAIR * p] = st[p * nb:(p + 1) * nb, :RW_HEAD, :RW_HEAD]
            sf_ref[:, RW_PAIR * p + 1] = st[p * nb:(p + 1) * nb, RW_HEAD:, RW_HEAD:]


def _rwkv_chunk(seq):
    for c in (48, 32, 16, 8):
        if seq % c == 0:
            return c
    raise ValueError(seq)


def rwkv7(r, k, v, low, low_w, s0, w0, a0, k_k, k_a, r_k, ln_w, ln_b, nb, hp):
    bsz, seq, d = r.shape
    chunk = _rwkv_chunk(seq)
    heads = hp * RW_PAIR
    assert bsz % nb == 0 and RW_HEADS % heads == 0
    seq_blk = pl.BlockSpec((nb, chunk, hp * LANES), lambda h, b, t: (b, t, h))
    st_blk = pl.BlockSpec((nb, heads, RW_HEAD, RW_HEAD), lambda h, b, t: (b, h, 0, 0))
    vec = pl.BlockSpec((1, hp * LANES), lambda h, b, t: (0, h))
    low_blk = [pl.BlockSpec((nb, chunk, x.shape[-1]), lambda h, b, t: (b, t, 0)) for x in low]
    low_w_blk = [pl.BlockSpec((w.shape[0], hp * LANES), lambda h, b, t: (0, h)) for w in low_w]
    out, sf = pl.pallas_call(
        functools.partial(_rwkv_kernel, nb=nb, hp=hp, chunk=chunk),
        grid=(RW_HEADS // heads, bsz // nb, seq // chunk),
        in_specs=[seq_blk] * 3 + low_blk + low_w_blk + [st_blk] + [vec] * 7,
        out_specs=[seq_blk, st_blk],
        out_shape=[jax.ShapeDtypeStruct((bsz, seq, d), BF16),
                   jax.ShapeDtypeStruct((bsz, RW_HEADS, RW_HEAD, RW_HEAD), F32)],
        scratch_shapes=[pltpu.VMEM((hp * nb, LANES, LANES), F32)],
        compiler_params=_params("parallel", "parallel", "arbitrary"),
        name="rwkv7",
    )(r, k, v, *low, *low_w, s0, *(p.reshape(1, d) for p in (w0, a0, k_k, k_a, r_k, ln_w, ln_b)))
    return out, sf


def _ffn_in_kernel(x_ref, wa_ref, wv_ref, e_ref, cw_ref, cb_ref, o_ref, st_ref, scr, *wb_refs, nb, seq, sb, sr):
    tn = wa_ref.shape[1]
    if wb_refs:
        @pl.when(pl.program_id(1) == 0)
        def _():
            wb_refs[0][...] = wa_ref[...].astype(BF16)
            wb_refs[1][...] = wv_ref[...].astype(BF16)

        wa_ref, wv_ref = wb_refs
    cw = cw_ref[...]
    cb = cb_ref[...][None]
    scr[:, SUBLANES - (CONV_W - 1):SUBLANES, :] = e_ref[...]
    for b0 in range(0, nb, sb):
        for r0 in range(0, seq, sr):
            lo = b0 * seq + r0
            x = x_ref[lo:lo + sb * sr, :]
            a = jnp.dot(x, wa_ref[...], preferred_element_type=F32).reshape(sb, sr, tn)
            v = jnp.dot(x, wv_ref[...], preferred_element_type=F32).reshape(sb, sr, tn)
            scr[b0:b0 + sb, SUBLANES + r0:SUBLANES + r0 + sr, :] = a
            c = cb + cw[CONV_W - 1:CONV_W][None] * a
            for j in range(CONV_W - 1):
                first = SUBLANES + r0 - (CONV_W - 1 - j)
                c = c + cw[j:j + 1][None] * scr[b0:b0 + sb, first:first + sr, :]
            o_ref[lo:lo + sb * sr, :] = (jax.nn.gelu(c) * v).reshape(sb * sr, tn).astype(o_ref.dtype)
    st_ref[...] = scr[:, SUBLANES + seq - (CONV_W - 1):SUBLANES + seq, :]


def ffn_in(xb, conv0, w_in, layer, conv_w, conv_b, bsz, seq, nb, tn, sub):
    t, d = xb.shape
    sb, sr = sub
    assert t == bsz * seq and bsz % nb == 0 and D_FF % tn == 0
    assert nb % sb == 0 and seq % sr == 0 and sr % SUBLANES == 0 and (sr == seq or nb == sb == 1)
    n_col = D_FF // tn
    cast_w = w_in.dtype == F32
    rc = (lambda f: lambda j, i: f(i, j)) if cast_w else (lambda f: f)
    col = rc(lambda i, j: (0, j))
    out, st = pl.pallas_call(
        functools.partial(_ffn_in_kernel, nb=nb, seq=seq, sb=sb, sr=sr),
        grid=(n_col, bsz // nb) if cast_w else (bsz // nb, n_col),
        in_specs=[pl.BlockSpec((nb * seq, d), rc(lambda i, j: (i, 0))),
                  pl.BlockSpec((None, d, tn), rc(lambda i, j: (layer, 0, j))),
                  pl.BlockSpec((None, d, tn), rc(lambda i, j: (layer, 0, j + n_col))),
                  pl.BlockSpec((nb, CONV_W - 1, tn), rc(lambda i, j: (i, 0, j))),
                  pl.BlockSpec((CONV_W, tn), col), pl.BlockSpec((1, tn), col)],
        out_specs=[pl.BlockSpec((nb * seq, tn), rc(lambda i, j: (i, j))),
                   pl.BlockSpec((nb, CONV_W - 1, tn), rc(lambda i, j: (i, 0, j)))],
        out_shape=[jax.ShapeDtypeStruct((t, D_FF), BF16),
                   jax.ShapeDtypeStruct((bsz, CONV_W - 1, D_FF), F32)],
        scratch_shapes=[pltpu.VMEM((nb, SUBLANES + seq, tn), F32)] + [pltpu.VMEM((d, tn), BF16)] * (2 * cast_w),
        compiler_params=_params("parallel", "arbitrary" if cast_w else "parallel"),
        name="ffn_in",
    )(xb, w_in, w_in, conv0, conv_w, conv_b.reshape(1, D_FF))
    return out, st


FFN_DOWN_K_STEPS = 2


def _channel_mixer(h, conv0, layer, p, cfg, bsz, seq):
    (xb,) = rmsnorm(h, p["ln_ffn"][layer], (BF16,))
    gated, n_cv = ffn_in(xb, conv0, p["ffn_w_in"], layer, p["ffn_conv_w"][layer], p["ffn_conv_b"][layer],
                         bsz, seq, cfg["ffn_nb"], cfg["ffn_tn"], cfg["ffn_sub"])
    return matmul_residual_split(gated, p["ffn_w_down"], layer, h, FFN_DOWN_K_STEPS), n_cv


def _trunk(x3, s5r, s5i, hg, rw, sh, cv, p, cfg):
    bsz, seq, d = x3.shape
    seq += cfg["front"]
    t = bsz * seq

    if cfg["front"]:
        h, xb = embed_norm(x3, p["meta"], p["ln_mix"][0])
    else:
        h = x3.reshape(t, d)
        (xb,) = rmsnorm(h, p["ln_mix"][0], (BF16,))
    z = matmul([xb], [p["ev_w_in"]], F32).reshape(bsz, seq, EVEN_IN)
    if cfg["s5_tl"]:
        ys5, n_s5r, n_s5i = s5_scan_long(z, s5r[0], s5i[0], *p["s5"], tl=cfg["s5_tl"])
    else:
        ys5, n_s5r, n_s5i = s5_scan(z, s5r[0], s5i[0], *p["s5"], nb=cfg["s5_nb"])
    ys5 = ys5.reshape(t, S5_WIDTH)
    ya = matmul([ys5], [p["s5_w_glu"]], BF16, epilogue="glu", extra=ys5)
    yb, n_hg = hgrn2(z, hg[0], p["hg_lb"], p["hg_norm_w"], nb=cfg["hg_nb"], hh=cfg["hg_hh"])
    h = matmul([ya, yb.reshape(t, -1)], [(p["ev_w_out"], 0, 0), (p["ev_w_out"], 0, 1)], F32,
               epilogue="residual", extra=h)
    h, n_cv0 = _channel_mixer(h, cv[0], 0, p, cfg, bsz, seq)

    mixes, n_sh = norm_mix(h.reshape(bsz, seq, d), sh[0], p["ln_mix"][1], p["rw_mu"], *cfg["mix_blk"])
    xr, xw, xk, xv, xa, xg = (m.reshape(t, d) for m in mixes)
    r = matmul([xr], [p["rw_w_r"]], F32)
    k = matmul([xk], [p["rw_w_k"]], F32)
    v = matmul([xv], [p["rw_w_v"]], F32)
    low = (matmul([xw], [p["rw_w1"]], BF16, act="tanh"), matmul([xa], [p["rw_a1"]], BF16),
           matmul([xg], [p["rw_g1"]], BF16, act="sigmoid"))
    as3 = lambda a: a.reshape(bsz, seq, a.shape[-1])
    yo, n_rw = rwkv7(as3(r), as3(k), as3(v), [as3(x) for x in low], (p["rw_w2"], p["rw_a2"], p["rw_g2"]),
                     rw[0], *p["rw_vec"], nb=cfg["rw_nb"], hp=cfg["rw_hp"])
    h = matmul([yo.reshape(t, d)], [p["rw_w_o"]], F32, epilogue="residual", extra=h)
    h, n_cv1 = _channel_mixer(h, cv[1], 1, p, cfg, bsz, seq)

    (y,) = rmsnorm(h, p["ln_final"], (F32,), rows=(bsz, seq, cfg["front"]) if cfg["front"] else None)
    return (y.reshape(bsz, seq - cfg["front"], d), n_s5r[None], n_s5i[None], n_hg[None], n_rw[None], n_sh[None],
            jnp.stack([n_cv0, n_cv1]))


PROMPT_CFG = dict(front=N_META, s5_tl=344, s5_nb=None, hg_nb=4, hg_hh=8, mix_blk=(1, 344), rw_nb=4, rw_hp=16, ffn_nb=1, ffn_tn=512, ffn_sub=(1, 688))
SAMPLE_CFG = dict(front=0, s5_tl=None, s5_nb=128, hg_nb=8, hg_hh=8, mix_blk=(32, 8), rw_nb=16, rw_hp=4, ffn_nb=128, ffn_tn=512, ffn_sub=(32, 8))


def kernel(x_prompt, x_sample, state_s5_re, state_s5_im, state_hgrn, state_rwkv, state_shift, state_conv, meta_tokens, ln_mix, ln_ffn, ln_final, ev_w_in, ev_w_out, s5_lam_re, s5_lam_im, s5_log_step, s5_b_re, s5_b_im, s5_c_re, s5_c_im, s5_d, s5_w_glu, hg_lb, hg_norm_w, rw_mu, rw_w0, rw_w1, rw_w2, rw_a0, rw_a1, rw_a2, rw_g1, rw_g2, rw_k_k, rw_k_a, rw_r_k, rw_w_r, rw_w_k, rw_w_v, rw_w_o, rw_ln_w, rw_ln_b, ffn_w_in, ffn_conv_w, ffn_conv_b, ffn_w_down):
    bf = lambda w: w.astype(BF16)
    lb_all = hg_lower_bounds(hg_lb)
    pwr, pwi, bbr_t, bbi_t = s5_prep(s5_lam_re[0], s5_lam_im[0], s5_log_step[0], s5_b_re[0], s5_b_im[0])
    wbr, wbi, wcr, wci = _s5_block_weights(bbr_t, bbi_t, s5_c_re[0], s5_c_im[0])
    p = {
        "meta": meta_tokens, "ln_mix": ln_mix, "ln_ffn": ln_ffn, "ln_final": ln_final,
        "ev_w_in": ev_w_in[0],
        "ev_w_out": ev_w_out,
        "s5": (pwr, pwi, wbr, wbi, wcr, wci, s5_d[0].reshape(1, S5_WIDTH)),
        "s5_w_glu": s5_w_glu[0],
        "hg_lb": lb_all[0], "hg_norm_w": hg_norm_w[0],
        "rw_mu": rw_mu[0],
        "rw_w1": rw_w1[0], "rw_w2": bf(rw_w2[0]), "rw_a1": rw_a1[0], "rw_a2": bf(rw_a2[0]),
        "rw_g1": rw_g1[0], "rw_g2": bf(rw_g2[0]),
        "rw_w_r": rw_w_r[0], "rw_w_k": rw_w_k[0], "rw_w_v": rw_w_v[0], "rw_w_o": rw_w_o[0],
        "rw_vec": (rw_w0[0], rw_a0[0], rw_k_k[0], rw_k_a[0], rw_r_k[0].reshape(D_MODEL), rw_ln_w[0], rw_ln_b[0]),
        "ffn_w_in": ffn_w_in, "ffn_conv_w": ffn_conv_w, "ffn_conv_b": ffn_conv_b,
        "ffn_w_down": bf(ffn_w_down),
    }

    bsz = x_prompt.shape[0]
    zeros = lambda *s: jnp.zeros(s, F32)
    outs_p = _trunk(x_prompt,
                    zeros(1, bsz, S5_GROUPS, S5_STATE), zeros(1, bsz, S5_GROUPS, S5_STATE),
                    zeros(1, bsz, HG_HEADS, HG_K, HG_V), zeros(1, bsz, RW_HEADS, RW_HEAD, RW_HEAD),
                    zeros(1, bsz, D_MODEL), zeros(2, bsz, CONV_W - 1, D_FF), p, PROMPT_CFG)
    outs_s = _trunk(x_sample, state_s5_re, state_s5_im, state_hgrn, state_rwkv, state_shift, state_conv,
                    p, SAMPLE_CFG)
    return tuple(outs_p[:1]) + tuple(outs_s[:1]) + tuple(outs_p[1:]) + tuple(outs_s[1:])
```

```python
import functools
import math

import jax
import jax.numpy as jnp
from jax import lax
from jax.experimental import pallas as pl
from jax.experimental.pallas import tpu as pltpu

F32 = jnp.float32
BF16 = jnp.bfloat16

D_MODEL = 2048
N_META = 16
EPS = 1e-6
S5_WIDTH = 1024
S5_GROUP = 16
S5_GROUPS = 64
S5_STATE = 64
S5_CH = S5_GROUPS * S5_STATE
HG_HEADS = 8
HG_K = 128
HG_V = 128
HG_CHUNK = 16
EVEN_IN = 5120
RW_HEAD = 64
RW_HEADS = 32
RW_GN_EPS = 64e-5
D_FF = 5632
CONV_W = 3

LANES = 128
SUBLANES = 8
VMEM_LIMIT = 56 * 1024 * 1024


def _params(*sem):
    return pltpu.CompilerParams(dimension_semantics=sem, vmem_limit_bytes=VMEM_LIMIT)


def _row_tile(t, cap=1024):
    best = None
    for d in range(16, min(t, cap) + 1, 16):
        if t % d == 0:
            best = d
    assert best is not None, t
    return best


def _rms_kernel(x_ref, w_ref, *o_refs):
    x = x_ref[...]
    y = x * lax.rsqrt(jnp.mean(x * x, axis=-1, keepdims=True) + EPS) * w_ref[...]
    for o_ref in o_refs:
        o_ref[...] = y.astype(o_ref.dtype)


def rmsnorm(x, w, dtypes, rows=None):
    t, d = x.shape
    if rows is None:
        tm = _row_tile(t)
        n_out = t
        grid = (t // tm,)
        in_spec = pl.BlockSpec((tm, d), lambda i: (i, 0))
        out_spec = in_spec
        vec = pl.BlockSpec((1, d), lambda i: (0, 0))
    else:
        bsz, seq, front = rows
        keep = seq - front
        assert t == bsz * seq and front % 16 == 0
        tm = _row_tile(keep)
        per = keep // tm
        n_out = bsz * keep
        grid = (bsz, per)
        in_spec = pl.BlockSpec((pl.Element(tm), pl.Element(d)),
                               lambda b, i: (pl.multiple_of(b * seq + front + i * tm, 16), 0))
        out_spec = pl.BlockSpec((tm, d), lambda b, i: (b * per + i, 0))
        vec = pl.BlockSpec((1, d), lambda b, i: (0, 0))
    outs = pl.pallas_call(
        _rms_kernel,
        grid=grid,
        in_specs=[in_spec, vec],
        out_specs=[out_spec for _ in dtypes],
        out_shape=[jax.ShapeDtypeStruct((n_out, d), dt) for dt in dtypes],
        compiler_params=_params(*(["parallel"] * len(grid))),
        name="rmsnorm",
    )(x, w.reshape(1, d))
    return outs


def _embed_norm_kernel(x_ref, m_ref, w_ref, h_ref, xb_ref, *, front):
    x = x_ref[...]
    first = pl.program_id(1) == 0
    body = jnp.where(first, pltpu.roll(x, front, axis=0), x)
    head = jnp.where(first, m_ref[...], x[:front])
    rows = jnp.concatenate([head, body[front:]], axis=0)
    h_ref[...] = rows
    y = rows * lax.rsqrt(jnp.mean(rows * rows, axis=-1, keepdims=True) + EPS) * w_ref[...]
    xb_ref[...] = y.astype(xb_ref.dtype)


def embed_norm(x3, meta, w):
    bsz, seq, d = x3.shape
    front = meta.shape[0]
    total = seq + front
    tm = _row_tile(total)
    per = total // tm
    assert front % 16 == 0 and tm > front
    blk = pl.BlockSpec((tm, d), lambda b, j: (b * per + j, 0))
    return pl.pallas_call(
        functools.partial(_embed_norm_kernel, front=front),
        grid=(bsz, per),
        in_specs=[pl.BlockSpec((pl.Element(tm), pl.Element(d)),
                               lambda b, j: (pl.multiple_of(b * seq + jnp.maximum(j * tm - front, 0), 16), 0)),
                  pl.BlockSpec((front, d), lambda b, j: (0, 0)),
                  pl.BlockSpec((1, d), lambda b, j: (0, 0))],
        out_specs=[blk, blk],
        out_shape=[jax.ShapeDtypeStruct((bsz * total, d), F32), jax.ShapeDtypeStruct((bsz * total, d), BF16)],
        compiler_params=_params("parallel", "parallel"),
        name="embed_norm",
    )(x3.reshape(bsz * seq, d), meta, w.reshape(1, d))


def _act(x, act):
    if act == "tanh":
        return jnp.tanh(x)
    if act == "sigmoid":
        return jax.nn.sigmoid(x)
    assert act is None
    return x


def _mm_kernel(*refs, n_a, act, epilogue, cast_w):
    a_refs = refs[:n_a]
    w_refs = refs[n_a:2 * n_a]
    rest = refs[2 * n_a:]
    if cast_w:
        rest, wb_refs = rest[:-n_a], rest[-n_a:]

        @pl.when(pl.program_id(1) == 0)
        def _():
            for w_ref, wb_ref in zip(w_refs, wb_refs):
                wb_ref[...] = w_ref[...].astype(BF16)

        w_refs = wb_refs
    o_ref = rest[-1]
    acc = jnp.dot(a_refs[0][...].astype(BF16), w_refs[0][...], preferred_element_type=F32)
    for a_ref, w_ref in zip(a_refs[1:], w_refs[1:]):
        acc = acc + jnp.dot(a_ref[...].astype(BF16), w_ref[...], preferred_element_type=F32)
    acc = _act(acc, act)
    if epilogue == "residual":
        acc = rest[0][...] + acc
    elif epilogue == "glu":
        acc = rest[0][...] * jax.nn.sigmoid(acc)
    o_ref[...] = acc.astype(o_ref.dtype)


MM_VMEM_BUDGET = 40 * 1024 * 1024
MM_CAST_VMEM_BUDGET = 48 * 1024 * 1024
MM_MAX_A_BLOCK = 6 * 1024 * 1024
MXU_WIDTH = 256


def _mm_tiles(t, a_row_bytes, w_col_bytes, n, out_bytes, has_extra, budget=MM_VMEM_BUDGET):
    rows = [d for d in range(16, t + 1, 16) if t % d == 0 and d * a_row_bytes <= MM_MAX_A_BLOCK]
    cols = [d for d in range(LANES, n + 1, LANES) if n % d == 0] or [n]
    best, best_score = None, -1.0
    for tm in rows:
        for tn in cols:
            est = 2 * tm * a_row_bytes + w_col_bytes * tn + tm * tn * (2 * out_bytes + 4 + (8 if has_extra else 0))
            if est > budget:
                continue
            score = tm * tn * (1.0 if tn % MXU_WIDTH == 0 else 0.8)
            if score > best_score:
                best, best_score = (tm, tn), score
    assert best is not None, (t, a_row_bytes, w_col_bytes, n)
    return best


def matmul(a_list, w_list, out_dtype, act=None, epilogue=None, extra=None):
    t = a_list[0].shape[0]
    w0 = w_list[0][0] if isinstance(w_list[0], tuple) else w_list[0]
    n = w0.shape[-1]
    cast_w = w0.dtype == F32
    k_rows = sum(a.shape[1] for a in a_list)
    a_row_bytes = sum(a.shape[1] * a.dtype.itemsize for a in a_list)
    w_col_bytes = k_rows * (2 * 4 + 2 if cast_w else 2 * 2)
    tm, tn = _mm_tiles(t, a_row_bytes, w_col_bytes, n, jnp.dtype(out_dtype).itemsize, epilogue is not None,
                       budget=MM_CAST_VMEM_BUDGET if cast_w else MM_VMEM_BUDGET)
    rc = (lambda f: lambda j, i: f(i, j)) if cast_w else (lambda f: f)
    in_specs = [pl.BlockSpec((tm, a.shape[1]), rc(lambda i, j: (i, 0))) for a in a_list]
    args = list(a_list)
    for a, w in zip(a_list, w_list):
        if isinstance(w, tuple):
            w, layer, kblk = w
            in_specs.append(pl.BlockSpec((None, a.shape[1], tn),
                                         rc(lambda i, j, layer=layer, kblk=kblk: (layer, kblk, j))))
        else:
            in_specs.append(pl.BlockSpec((w.shape[0], tn), rc(lambda i, j: (0, j))))
        args.append(w)
    if epilogue is not None:
        in_specs.append(pl.BlockSpec((tm, tn), rc(lambda i, j: (i, j))))
        args.append(extra)
    return pl.pallas_call(
        functools.partial(_mm_kernel, n_a=len(a_list), act=act, epilogue=epilogue, cast_w=cast_w),
        grid=(n // tn, t // tm) if cast_w else (t // tm, n // tn),
        in_specs=in_specs,
        out_specs=pl.BlockSpec((tm, tn), rc(lambda i, j: (i, j))),
        out_shape=jax.ShapeDtypeStruct((t, n), out_dtype),
        scratch_shapes=[pltpu.VMEM((a.shape[1], tn), BF16) for a in a_list] if cast_w else [],
        compiler_params=_params("parallel", "arbitrary" if cast_w else "parallel"),
        name="matmul",
    )(*args)


def _mm_resid_split_kernel(a_ref, w_ref, res_ref, o_ref):
    part = jnp.dot(a_ref[...], w_ref[...], preferred_element_type=F32)

    @pl.when(pl.program_id(2) == 0)
    def _():
        o_ref[...] = res_ref[...] + part

    @pl.when(pl.program_id(2) > 0)
    def _():
        o_ref[...] += part


def matmul_residual_split(a, w, layer, res, k_steps):
    t, k = a.shape
    n = w.shape[2]
    tk = k // k_steps
    assert k % k_steps == 0 and tk % LANES == 0
    tm, tn = _mm_tiles(t, 2 * tk, 4 * tk, n, 4, True)
    return pl.pallas_call(
        _mm_resid_split_kernel,
        grid=(t // tm, n // tn, k_steps),
        in_specs=[pl.BlockSpec((tm, tk), lambda i, j, s: (i, s)),
                  pl.BlockSpec((None, tk, tn), lambda i, j, s: (layer, s, j)),
                  pl.BlockSpec((tm, tn), lambda i, j, s: (i, j))],
        out_specs=pl.BlockSpec((tm, tn), lambda i, j, s: (i, j)),
        out_shape=jax.ShapeDtypeStruct((t, n), F32),
        compiler_params=_params("parallel", "parallel", "arbitrary"),
        name="matmul_split",
    )(a, w, res)


def _s5_prep_kernel(lr_ref, li_ref, ls_ref, brt_ref, bit_ref, pwr_ref, pwi_ref, bbr_ref, bbi_ref):
    lr = jnp.minimum(lr_ref[...], -1e-4)
    li = li_ref[...]
    dt = jnp.exp(ls_ref[...])
    n = lax.broadcasted_iota(jnp.int32, (SUBLANES, S5_CH), 0).astype(F32) + 1.0
    mag = jnp.exp(n * (lr * dt))
    ang = n * (li * dt)
    pwr = mag * jnp.cos(ang)
    pwi = mag * jnp.sin(ang)
    pwr_ref[...] = pwr
    pwi_ref[...] = pwi
    ar = pwr[0:1]
    ai = pwi[0:1]
    den = lr * lr + li * li
    zr = ((ar - 1.0) * lr + ai * li) / den
    zi = (ai * lr - (ar - 1.0) * li) / den
    br = brt_ref[...]
    bi = bit_ref[...]
    bbr_ref[...] = zr * br - zi * bi
    bbi_ref[...] = zr * bi + zi * br


def s5_prep(lam_re, lam_im, log_step, b_re, b_im):
    lr = lam_re.reshape(1, S5_CH)
    li = lam_im.reshape(1, S5_CH)
    ls = jnp.broadcast_to(log_step[:, None], (S5_GROUPS, S5_STATE)).reshape(1, S5_CH)
    brt = b_re.reshape(S5_CH, S5_GROUP).T
    bit = b_im.reshape(S5_CH, S5_GROUP).T
    return pl.pallas_call(
        _s5_prep_kernel,
        out_shape=[jax.ShapeDtypeStruct((SUBLANES, S5_CH), F32)] * 2
        + [jax.ShapeDtypeStruct((S5_GROUP, S5_CH), F32)] * 2,
        name="s5_prep",
    )(lr, li, ls, brt, bit)


S5_BLK_GROUPS = LANES // S5_GROUP
S5_BLKS = S5_WIDTH // LANES
S5_BLK_CH = S5_BLK_GROUPS * S5_STATE


def _cmul_add(xr, xi, mr, mi, sr, si):
    return xr + mr * sr - mi * si, xi + mr * si + mi * sr


def _s5_kernel(u_ref, h0r_ref, h0i_ref, pwr_ref, pwi_ref, wbr_ref, wbi_ref, wcr_ref, wci_ref, d_ref,
               y_ref, hr_ref, hi_ref, xr_scr, xi_scr, *, nb, seq):
    u2 = u_ref[...].reshape(nb * seq, LANES)
    ub = u2.astype(BF16)
    xr_scr[...] = jnp.dot(ub, wbr_ref[0], preferred_element_type=F32).reshape(nb, seq, S5_BLK_CH)
    xi_scr[...] = jnp.dot(ub, wbi_ref[0], preferred_element_type=F32).reshape(nb, seq, S5_BLK_CH)

    pwr = pwr_ref[...]
    pwi = pwi_ref[...]
    row = lax.broadcasted_iota(jnp.int32, (SUBLANES, S5_BLK_CH), 0)
    steps = []
    for d in (1, 2, 4):
        keep = row >= d
        steps.append((d, jnp.where(keep, pwr[d - 1:d], 0.0)[None], jnp.where(keep, pwi[d - 1:d], 0.0)[None]))
    pr = pwr[None]
    pi = pwi[None]

    def tile(i, carry):
        cr, ci = carry
        o = pl.multiple_of(i * SUBLANES, SUBLANES)
        xr = xr_scr[:, pl.ds(o, SUBLANES), :]
        xi = xi_scr[:, pl.ds(o, SUBLANES), :]
        for d, mr, mi in steps:
            sr = pltpu.roll(xr, d, axis=1)
            si = pltpu.roll(xi, d, axis=1)
            xr, xi = _cmul_add(xr, xi, mr, mi, sr, si)
        xr, xi = _cmul_add(xr, xi, pr, pi, cr, ci)
        xr_scr[:, pl.ds(o, SUBLANES), :] = xr
        xi_scr[:, pl.ds(o, SUBLANES), :] = xi
        return xr[:, SUBLANES - 1:SUBLANES, :], xi[:, SUBLANES - 1:SUBLANES, :]

    hr, hi = lax.fori_loop(0, seq // SUBLANES, tile, (h0r_ref[...], h0i_ref[...]))
    hr_ref[...] = hr
    hi_ref[...] = hi

    xr = xr_scr[...].reshape(nb * seq, S5_BLK_CH).astype(BF16)
    xi = xi_scr[...].reshape(nb * seq, S5_BLK_CH).astype(BF16)
    y = (jnp.dot(xr, wcr_ref[0], preferred_element_type=F32)
         - jnp.dot(xi, wci_ref[0], preferred_element_type=F32)
         + d_ref[...] * u2)
    y_ref[...] = jax.nn.gelu(y).reshape(nb, seq, LANES)


def s5_scan(z3, h0r, h0i, pwr, pwi, wbr, wbi, wcr, wci, d, nb):
    bsz, seq, _ = z3.shape
    assert seq % SUBLANES == 0 and bsz % nb == 0
    seq_blk = pl.BlockSpec((nb, seq, LANES), lambda b, k: (b, 0, k))
    st_blk = pl.BlockSpec((nb, 1, S5_BLK_CH), lambda b, k: (b, 0, k))
    pw_blk = pl.BlockSpec((SUBLANES, S5_BLK_CH), lambda b, k: (0, k))
    wb_blk = pl.BlockSpec((1, LANES, S5_BLK_CH), lambda b, k: (k, 0, 0))
    wc_blk = pl.BlockSpec((1, S5_BLK_CH, LANES), lambda b, k: (k, 0, 0))
    y, hr, hi = pl.pallas_call(
        functools.partial(_s5_kernel, nb=nb, seq=seq),
        grid=(bsz // nb, S5_BLKS),
        in_specs=[seq_blk, st_blk, st_blk, pw_blk, pw_blk, wb_blk, wb_blk, wc_blk, wc_blk,
                  pl.BlockSpec((1, LANES), lambda b, k: (0, k))],
        out_specs=[seq_blk, st_blk, st_blk],
        out_shape=[jax.ShapeDtypeStruct((bsz, seq, S5_WIDTH), F32),
                   jax.ShapeDtypeStruct((bsz, 1, S5_CH), F32),
                   jax.ShapeDtypeStruct((bsz, 1, S5_CH), F32)],
        scratch_shapes=[pltpu.VMEM((nb, seq, S5_BLK_CH), F32), pltpu.VMEM((nb, seq, S5_BLK_CH), F32)],
        compiler_params=_params("parallel", "parallel"),
        name="s5_scan",
    )(z3, h0r.reshape(bsz, 1, S5_CH), h0i.reshape(bsz, 1, S5_CH), pwr, pwi, wbr, wbi, wcr, wci, d)
    return y, hr.reshape(bsz, S5_GROUPS, S5_STATE), hi.reshape(bsz, S5_GROUPS, S5_STATE)


S5_LANE_TILES = S5_BLK_CH // LANES


def _s5_long_kernel(u_ref, h0r_ref, h0i_ref, ar_ref, ai_ref, wbr_ref, wbi_ref, wcr_ref, wci_ref, d_ref,
                    y_ref, hr_ref, hi_ref, xr_scr, xi_scr, cr_scr, ci_scr, *, tl):
    step = pl.program_id(1)

    @pl.when(step == 0)
    def _():
        cr_scr[...] = h0r_ref[0]
        ci_scr[...] = h0i_ref[0]

    for k in range(S5_BLKS):
        ub = u_ref[0, :, k * LANES:(k + 1) * LANES].astype(BF16)
        bur = jnp.dot(ub, wbr_ref[k], preferred_element_type=F32)
        bui = jnp.dot(ub, wbi_ref[k], preferred_element_type=F32)
        for j in range(S5_LANE_TILES):
            xr_scr[j, k * tl:(k + 1) * tl, :] = bur[:, j * LANES:(j + 1) * LANES]
            xi_scr[j, k * tl:(k + 1) * tl, :] = bui[:, j * LANES:(j + 1) * LANES]

    ar = [ar_ref[:, j * LANES:(j + 1) * LANES] for j in range(S5_LANE_TILES)]
    ai = [ai_ref[:, j * LANES:(j + 1) * LANES] for j in range(S5_LANE_TILES)]

    def token(t, carry):
        cr, ci = carry
        nr, ni = [], []
        for j in range(S5_LANE_TILES):
            rows = pl.ds(t, S5_BLKS, stride=tl)
            xr, xi = _cmul_add(xr_scr[j, rows, :], xi_scr[j, rows, :], ar[j], ai[j], cr[j], ci[j])
            xr_scr[j, rows, :] = xr
            xi_scr[j, rows, :] = xi
            nr.append(xr)
            ni.append(xi)
        return tuple(nr), tuple(ni)

    init = (tuple(cr_scr[:, j * LANES:(j + 1) * LANES] for j in range(S5_LANE_TILES)),
            tuple(ci_scr[:, j * LANES:(j + 1) * LANES] for j in range(S5_LANE_TILES)))
    cr, ci = lax.fori_loop(0, tl, token, init, unroll=8)
    cr = jnp.concatenate(cr, axis=-1)
    ci = jnp.concatenate(ci, axis=-1)
    cr_scr[...] = cr
    ci_scr[...] = ci
    hr_ref[0] = cr
    hi_ref[0] = ci

    for k in range(S5_BLKS):
        xr = jnp.concatenate([xr_scr[j, k * tl:(k + 1) * tl, :] for j in range(S5_LANE_TILES)], axis=-1)
        xi = jnp.concatenate([xi_scr[j, k * tl:(k + 1) * tl, :] for j in range(S5_LANE_TILES)], axis=-1)
        u = u_ref[0, :, k * LANES:(k + 1) * LANES]
        y = (jnp.dot(xr.astype(BF16), wcr_ref[k], preferred_element_type=F32)
             - jnp.dot(xi.astype(BF16), wci_ref[k], preferred_element_type=F32)
             + d_ref[:, k * LANES:(k + 1) * LANES] * u)
        y_ref[0, :, k * LANES:(k + 1) * LANES] = jax.nn.gelu(y)


def s5_scan_long(z3, h0r, h0i, pwr, pwi, wbr, wbi, wcr, wci, d, tl):
    bsz, seq, _ = z3.shape
    assert seq % tl == 0 and tl % SUBLANES == 0
    seq_blk = pl.BlockSpec((1, tl, S5_WIDTH), lambda b, t: (b, t, 0))
    st_blk = pl.BlockSpec((1, S5_BLKS, S5_BLK_CH), lambda b, t: (b, 0, 0))
    lam_blk = pl.BlockSpec((S5_BLKS, S5_BLK_CH), lambda b, t: (0, 0))
    wb_blk = pl.BlockSpec((S5_BLKS, LANES, S5_BLK_CH), lambda b, t: (0, 0, 0))
    wc_blk = pl.BlockSpec((S5_BLKS, S5_BLK_CH, LANES), lambda b, t: (0, 0, 0))
    rows = pltpu.VMEM((S5_LANE_TILES, S5_BLKS * tl, LANES), F32)
    carry = pltpu.VMEM((S5_BLKS, S5_BLK_CH), F32)
    y, hr, hi = pl.pallas_call(
        functools.partial(_s5_long_kernel, tl=tl),
        grid=(bsz, seq // tl),
        in_specs=[seq_blk, st_blk, st_blk, lam_blk, lam_blk, wb_blk, wb_blk, wc_blk, wc_blk,
                  pl.BlockSpec((1, S5_WIDTH), lambda b, t: (0, 0))],
        out_specs=[seq_blk, st_blk, st_blk],
        out_shape=[jax.ShapeDtypeStruct((bsz, seq, S5_WIDTH), F32),
                   jax.ShapeDtypeStruct((bsz, S5_BLKS, S5_BLK_CH), F32),
                   jax.ShapeDtypeStruct((bsz, S5_BLKS, S5_BLK_CH), F32)],
        scratch_shapes=[rows, rows, carry, carry],
        compiler_params=_params("parallel", "arbitrary"),
        name="s5_scan_long",
    )(z3, h0r.reshape(bsz, S5_BLKS, S5_BLK_CH), h0i.reshape(bsz, S5_BLKS, S5_BLK_CH),
      pwr[0].reshape(S5_BLKS, S5_BLK_CH), pwi[0].reshape(S5_BLKS, S5_BLK_CH), wbr, wbi, wcr, wci, d)
    return y, hr.reshape(bsz, S5_GROUPS, S5_STATE), hi.reshape(bsz, S5_GROUPS, S5_STATE)


def _s5_block_weights(bbr_t, bbi_t, c_re, c_im):
    eye = jnp.eye(S5_BLK_GROUPS, dtype=F32)

    def wb(bt):
        b4 = bt.reshape(S5_GROUP, S5_BLKS, S5_BLK_GROUPS, S5_STATE)
        w = jnp.einsum("cbgp,hg->bhcgp", b4, eye)
        return w.reshape(S5_BLKS, LANES, S5_BLK_CH).astype(BF16)

    def wc(c):
        c4 = c.reshape(S5_BLKS, S5_BLK_GROUPS, S5_GROUP, S5_STATE)
        w = jnp.einsum("bgcp,hg->bhpgc", c4, eye)
        return w.reshape(S5_BLKS, S5_BLK_CH, LANES).astype(BF16)

    return wb(bbr_t), wb(bbi_t), wc(c_re), wc(c_im)


def _hg_lb_kernel(x_ref, o_ref):
    x = x_ref[...]
    e = jnp.exp(x - jnp.max(x, axis=0, keepdims=True))
    sm = e / jnp.sum(e, axis=0, keepdims=True)
    acc = sm[0:1]
    o_ref[0:1, :] = acc
    for l in range(1, x.shape[0]):
        acc = acc + sm[l:l + 1]
        o_ref[l:l + 1, :] = acc


def hg_lower_bounds(hg_lb):
    return pl.pallas_call(_hg_lb_kernel, out_shape=jax.ShapeDtypeStruct(hg_lb.shape, F32), name="hg_lb")(hg_lb)


def _cumsum_rows(x, n):
    row = lax.broadcasted_iota(jnp.int32, x.shape, 1)
    d = 1
    while d < n:
        x = x + jnp.where(row >= d, pltpu.roll(x, d, axis=1), 0.0)
        d *= 2
    return x


def _hgrn_kernel(q_ref, f_ref, i_ref, g_ref, s0_ref, lb_ref, nw_ref, o_ref, sf_ref, st_scr, *, nb, hh, chunk, n_sub):
    step = pl.program_id(2)

    def units(x):
        return jnp.concatenate([x[:, :, h * LANES:(h + 1) * LANES] for h in range(hh)], axis=0)

    lb = units(jnp.broadcast_to(lb_ref[...][None], (nb, 1, hh * LANES)))
    nw = nw_ref[...][None]

    @pl.when(step == 0)
    def _():
        for h in range(hh):
            for b in range(nb):
                st_scr[h * nb + b] = s0_ref[b, h].T

    trow = lax.broadcasted_iota(jnp.int32, (chunk, chunk), 0)
    tcol = lax.broadcasted_iota(jnp.int32, (chunk, chunk), 1)
    causal = (tcol <= trow)[None]
    st = st_scr[...]
    for c in range(n_sub):
        rows = slice(c * chunk, (c + 1) * chunk)
        q, f, v, g = (units(ref[:, rows, :]) for ref in (q_ref, f_ref, i_ref, g_ref))
        fg = lb + (1.0 - lb) * jax.nn.sigmoid(f)
        qh = jax.nn.silu(q)
        kh = 1.0 - fg
        bcum = _cumsum_rows(jnp.log(fg), chunk)
        btot = bcum[:, chunk - 1:chunk, :]
        q_in = (qh * jnp.exp(bcum)).astype(BF16)
        k_in = (kh * jnp.exp(-bcum)).astype(BF16)
        k_end = (kh * jnp.exp(btot - bcum)).astype(BF16)
        decay = jnp.exp(btot)
        vb = v.astype(BF16)
        att = jnp.einsum("utk,usk->uts", q_in, k_in, preferred_element_type=F32)
        att = jnp.where(causal, att, 0.0).astype(BF16)
        out = (jnp.einsum("utk,uvk->utv", q_in, st.astype(BF16), preferred_element_type=F32)
               + jnp.einsum("uts,usv->utv", att, vb, preferred_element_type=F32))
        st = st * decay + jnp.einsum("usv,usk->uvk", vb, k_end, preferred_element_type=F32)
        out = out * lax.rsqrt(jnp.mean(out * out, axis=-1, keepdims=True) + EPS) * nw
        out = (out * jax.nn.silu(g)).astype(o_ref.dtype)
        for h in range(hh):
            o_ref[:, rows, h * LANES:(h + 1) * LANES] = out[h * nb:(h + 1) * nb]
    st_scr[...] = st

    @pl.when(step == pl.num_programs(2) - 1)
    def _():
        for h in range(hh):
            for b in range(nb):
                sf_ref[b, h] = st[h * nb + b].T


def hgrn2(z3, s0, lb, norm_w, nb, hh):
    bsz, seq, _ = z3.shape
    chunk = min(HG_CHUNK, seq)
    n_sub = next(n for n in (3, 2, 1) if seq % (n * chunk) == 0)
    rows = n_sub * chunk
    assert bsz % nb == 0 and HG_HEADS % hh == 0
    wid = hh * LANES
    n_col = (HG_HEADS * HG_K) // wid

    def col(proj):
        return pl.BlockSpec((nb, rows, wid), lambda h, b, t, proj=proj: (b, t, proj * n_col + h))

    st_blk = pl.BlockSpec((nb, hh, HG_K, HG_V), lambda h, b, t: (b, h, 0, 0))
    out, sf = pl.pallas_call(
        functools.partial(_hgrn_kernel, nb=nb, hh=hh, chunk=chunk, n_sub=n_sub),
        grid=(HG_HEADS // hh, bsz // nb, seq // rows),
        in_specs=[col(1), col(2), col(3), col(4), st_blk,
                  pl.BlockSpec((1, wid), lambda h, b, t: (0, h)),
                  pl.BlockSpec((1, LANES), lambda h, b, t: (0, 0))],
        out_specs=[pl.BlockSpec((nb, rows, wid), lambda h, b, t: (b, t, h)), st_blk],
        out_shape=[jax.ShapeDtypeStruct((bsz, seq, HG_HEADS * HG_V), BF16),
                   jax.ShapeDtypeStruct((bsz, HG_HEADS, HG_K, HG_V), F32)],
        scratch_shapes=[pltpu.VMEM((hh * nb, HG_V, HG_K), F32)],
        compiler_params=_params("parallel", "parallel", "arbitrary"),
        name="hgrn2",
    )(z3, z3, z3, z3, s0, lb.reshape(1, HG_HEADS * HG_K), norm_w.reshape(1, HG_V))
    return out, sf


RW_MIXES = 6


def _norm_mix_kernel(h_ref, sh_ref, w_ref, mu_ref, *refs, tl):
    o_refs = refs[:RW_MIXES]
    last_ref, scr = refs[RW_MIXES:]
    x = h_ref[...]
    xn = x * lax.rsqrt(jnp.mean(x * x, axis=-1, keepdims=True) + EPS) * w_ref[...][None]

    @pl.when(pl.program_id(1) == 0)
    def _():
        scr[:, SUBLANES - 1:SUBLANES, :] = sh_ref[...]

    scr[:, SUBLANES:, :] = xn
    xx = scr[:, SUBLANES - 1:SUBLANES - 1 + tl, :] - xn
    for j, o_ref in enumerate(o_refs):
        o_ref[...] = (xn + xx * mu_ref[j:j + 1, :][None]).astype(o_ref.dtype)
    last = xn[:, tl - 1:tl, :]
    scr[:, SUBLANES - 1:SUBLANES, :] = last
    last_ref[...] = last


def norm_mix(h3, shift0, ln_w, mu, nb, tl):
    bsz, seq, d = h3.shape
    assert bsz % nb == 0 and seq % tl == 0 and tl % SUBLANES == 0
    blk = pl.BlockSpec((nb, tl, d), lambda b, t: (b, t, 0))
    row = pl.BlockSpec((nb, 1, d), lambda b, t: (b, 0, 0))
    outs = pl.pallas_call(
        functools.partial(_norm_mix_kernel, tl=tl),
        grid=(bsz // nb, seq // tl),
        in_specs=[blk, row, pl.BlockSpec((1, d), lambda b, t: (0, 0)), pl.BlockSpec((RW_MIXES, d), lambda b, t: (0, 0))],
        out_specs=[blk] * RW_MIXES + [row],
        out_shape=[jax.ShapeDtypeStruct((bsz, seq, d), BF16)] * RW_MIXES + [jax.ShapeDtypeStruct((bsz, 1, d), F32)],
        scratch_shapes=[pltpu.VMEM((nb, SUBLANES + tl, d), F32)],
        compiler_params=_params("parallel", "arbitrary"),
        name="norm_mix",
    )(h3, shift0.reshape(bsz, 1, d), ln_w.reshape(1, d), mu)
    return outs[:RW_MIXES], outs[RW_MIXES].reshape(bsz, d)


RW_PAIR = LANES // RW_HEAD
RW_DECAY_SCALE = math.exp(-0.5)
RW_SOLVE_BLOCK = 8


def _rwkv_kernel(r_ref, k_ref, v_ref, tw_ref, ta_ref, tg_ref, w2_ref, a2_ref, g2_ref, s0_ref,
                 w0_ref, a0_ref, kk_ref, ka_ref, rk_ref, lnw_ref, lnb_ref,
                 o_ref, sf_ref, s_scr, *, nb, hp, chunk):
    nu = hp * nb
    step = pl.program_id(2)
    lane = lax.broadcasted_iota(jnp.int32, (1, 1, LANES), 2)
    head1 = lane >= RW_HEAD

    def units(x):
        return jnp.concatenate([x[:, :, p * LANES:(p + 1) * LANES] for p in range(hp)], axis=0)

    def unit_rows(ref):
        return units(jnp.broadcast_to(ref[...][None], (nb, 1, hp * LANES)))

    w0, a0, k_k, k_a, r_k, ln_w, ln_b = (unit_rows(p) for p in
                                         (w0_ref, a0_ref, kk_ref, ka_ref, rk_ref, lnw_ref, lnb_ref))
    sq_row = lax.broadcasted_iota(jnp.int32, (LANES, LANES), 0) >= RW_HEAD
    sq_col = lax.broadcasted_iota(jnp.int32, (LANES, LANES), 1) >= RW_HEAD
    same_head = sq_row == sq_col
    ones_bd = same_head.astype(BF16)

    @pl.when(step == 0)
    def _():
        zero = jnp.zeros((nb, RW_HEAD, RW_HEAD), F32)
        for p in range(hp):
            top = jnp.concatenate([s0_ref[:, RW_PAIR * p], zero], axis=-1)
            bot = jnp.concatenate([zero, s0_ref[:, RW_PAIR * p + 1]], axis=-1)
            s_scr[p * nb:(p + 1) * nb] = jnp.concatenate([top, bot], axis=1)

    def bdot(spec, a, b):
        return jnp.einsum(spec, a.astype(BF16), b.astype(BF16), preferred_element_type=F32)

    def head_sum(x, two_pass=True):
        x2 = x.reshape(nu * chunk, LANES)
        hi = x2.astype(BF16)
        s = jnp.dot(hi, ones_bd, preferred_element_type=F32)
        if two_pass:
            lo = (x2 - hi.astype(F32)).astype(BF16)
            s = s + jnp.dot(lo, ones_bd, preferred_element_type=F32)
        return s.reshape(nu, chunk, LANES)

    def low_rank(t_ref, w_ref):
        t2 = t_ref[...].reshape(nb * chunk, t_ref.shape[-1])
        return units(jnp.dot(t2, w_ref[...], preferred_element_type=F32).reshape(nb, chunk, hp * LANES))

    srow = lax.broadcasted_iota(jnp.int32, (chunk, chunk), 0)
    scol = lax.broadcasted_iota(jnp.int32, (chunk, chunk), 1)
    tri = jnp.broadcast_to((scol <= srow).astype(BF16)[None], (nu, chunk, chunk))

    def tri_sum(x):
        hi = x.astype(BF16)
        lo = (x - hi.astype(F32)).astype(BF16)
        return (jnp.einsum("uts,usc->utc", tri, hi, preferred_element_type=F32)
                + jnp.einsum("uts,usc->utc", tri, lo, preferred_element_type=F32))

    def stack_heads(x):
        return jnp.concatenate([jnp.where(head1, 0.0, x), jnp.where(head1, x, 0.0)], axis=1).astype(BF16)

    r, k, v = (units(ref[...]) for ref in (r_ref, k_ref, v_ref))
    wl, al, g = low_rank(tw_ref, w2_ref), low_rank(ta_ref, a2_ref), low_rank(tg_ref, g2_ref)
    lw = (-RW_DECAY_SCALE) * jax.nn.sigmoid(w0 + wl)
    ag = jax.nn.sigmoid(a0 + al)
    kk = k * k_k
    kk = kk * jnp.minimum(lax.rsqrt(head_sum(kk * kk)), 1e12)
    k2 = k * (1.0 + (ag - 1.0) * k_a)
    cl = tri_sum(lw)
    e_pos = jnp.exp(cl)
    e_neg = jnp.exp(-cl)
    at = (-kk) * jnp.exp(cl - lw)
    bt = (kk * ag) * e_neg
    kt = k2 * e_neg
    rt = r * e_pos
    wc = e_pos[:, chunk - 1:chunk, :]
    trow = lax.broadcasted_iota(jnp.int32, (chunk, RW_PAIR * chunk), 0)
    tcol = lax.broadcasted_iota(jnp.int32, (chunk, RW_PAIR * chunk), 1)
    tcol = jnp.where(tcol >= chunk, tcol - chunk, tcol)
    strict = (tcol < trow)[None]
    incl = (tcol <= trow)[None]
    x2 = jnp.concatenate([at, rt], axis=1)
    pb = bdot("utc,usc->uts", x2, stack_heads(bt))
    pk = bdot("utc,usc->uts", x2, stack_heads(kt))
    lab = jnp.where(strict, pb[:, :chunk], 0.0)
    lak = jnp.where(strict, pk[:, :chunk], 0.0)
    arb = jnp.where(incl, pb[:, chunk:], 0.0)
    ark = jnp.where(incl, pk[:, chunk:], 0.0)
    v_bd = stack_heads(v)
    xa = at
    xv = bdot("uts,usc->utc", lak, v_bd)
    sub = min(RW_SOLVE_BLOCK, chunk)
    done_a, done_v = [], []
    for lo in range(0, chunk, sub):
        xa_i = xa[:, lo:lo + sub, :]
        xv_i = xv[:, lo:lo + sub, :]
        if lo:
            pad = jnp.zeros((nu, chunk - lo, LANES), F32)
            prev = jnp.concatenate([stack_heads(jnp.concatenate(done_a + [pad], axis=1)),
                                    stack_heads(jnp.concatenate(done_v + [pad], axis=1))], axis=-1)
            upd = bdot("uts,usc->utc", lab[:, lo:lo + sub, :], prev)
            xa_i = xa_i + upd[:, :, :LANES]
            xv_i = xv_i + upd[:, :, LANES:]
        l0 = lab[:, lo:lo + sub, lo:lo + sub]
        l1 = lab[:, lo:lo + sub, chunk + lo:chunk + lo + sub]
        for s in range(sub - 1):
            m = jnp.where(head1, l1[:, :, s:s + 1], l0[:, :, s:s + 1])
            xa_i = xa_i + m * xa_i[:, s:s + 1, :]
            xv_i = xv_i + m * xv_i[:, s:s + 1, :]
        done_a.append(xa_i)
        done_v.append(xv_i)
    ah = jnp.concatenate(done_a, axis=1)
    vh = jnp.concatenate(done_v, axis=1)
    both = bdot("uts,usc->utc", arb, jnp.concatenate([stack_heads(ah), stack_heads(vh)], axis=-1))
    rh = rt + both[:, :, :LANES]
    yh = both[:, :, LANES:] + bdot("uts,usc->utc", ark, v_bd)
    gp = jnp.where(same_head, bdot("utj,utk->ujk", ah, bt), 0.0)
    ht = jnp.where(same_head, bdot("utv,utk->uvk", jnp.concatenate([vh, v], axis=1),
                                   jnp.concatenate([bt, kt], axis=1)), 0.0)
    st = s_scr[...]
    y = bdot("utk,uvk->utv", rh, st) + yh
    st = (st + bdot("uvj,ujk->uvk", st, gp) + ht) * wc
    s_scr[...] = st
    inv_n = 1.0 / RW_HEAD
    yc = y - head_sum(y, two_pass=False) * inv_n
    var = head_sum(yc * yc, two_pass=False) * inv_n
    y = yc * lax.rsqrt(var + RW_GN_EPS) * ln_w + ln_b
    y = y + head_sum(r * k2 * r_k, two_pass=False) * v
    out = (y * g).astype(o_ref.dtype)
    for p in range(hp):
        o_ref[:, :, p * LANES:(p + 1) * LANES] = out[p * nb:(p + 1) * nb]

    @pl.when(step == pl.num_programs(2) - 1)
    def _():
        for p in range(hp):
            sf_ref[:, RW_PAIR * p] = st[p * nb:(p + 1) * nb, :RW_HEAD, :RW_HEAD]
            sf_ref[:, RW_PAIR * p + 1] = st[p * nb:(p + 1) * nb, RW_HEAD:, RW_HEAD:]


def _rwkv_chunk(seq):
    for c in (48, 32, 16, 8):
        if seq % c == 0:
            return c
    raise ValueError(seq)


def rwkv7(r, k, v, low, low_w, s0, w0, a0, k_k, k_a, r_k, ln_w, ln_b, nb, hp):
    bsz, seq, d = r.shape
    chunk = _rwkv_chunk(seq)
    heads = hp * RW_PAIR
    assert bsz % nb == 0 and RW_HEADS % heads == 0
    seq_blk = pl.BlockSpec((nb, chunk, hp * LANES), lambda h, b, t: (b, t, h))
    st_blk = pl.BlockSpec((nb, heads, RW_HEAD, RW_HEAD), lambda h, b, t: (b, h, 0, 0))
    vec = pl.BlockSpec((1, hp * LANES), lambda h, b, t: (0, h))
    low_blk = [pl.BlockSpec((nb, chunk, x.shape[-1]), lambda h, b, t: (b, t, 0)) for x in low]
    low_w_blk = [pl.BlockSpec((w.shape[0], hp * LANES), lambda h, b, t: (0, h)) for w in low_w]
    out, sf = pl.pallas_call(
        functools.partial(_rwkv_kernel, nb=nb, hp=hp, chunk=chunk),
        grid=(RW_HEADS // heads, bsz // nb, seq // chunk),
        in_specs=[seq_blk] * 3 + low_blk + low_w_blk + [st_blk] + [vec] * 7,
        out_specs=[seq_blk, st_blk],
        out_shape=[jax.ShapeDtypeStruct((bsz, seq, d), BF16),
                   jax.ShapeDtypeStruct((bsz, RW_HEADS, RW_HEAD, RW_HEAD), F32)],
        scratch_shapes=[pltpu.VMEM((hp * nb, LANES, LANES), F32)],
        compiler_params=_params("parallel", "parallel", "arbitrary"),
        name="rwkv7",
    )(r, k, v, *low, *low_w, s0, *(p.reshape(1, d) for p in (w0, a0, k_k, k_a, r_k, ln_w, ln_b)))
    return out, sf


def _ffn_in_kernel(x_ref, wa_ref, wv_ref, e_ref, cw_ref, cb_ref, o_ref, st_ref, scr, *wb_refs, nb, seq, sb, sr):
    tn = wa_ref.shape[1]
    if wb_refs:
        @pl.when(pl.program_id(1) == 0)
        def _():
            wb_refs[0][...] = wa_ref[...].astype(BF16)
            wb_refs[1][...] = wv_ref[...].astype(BF16)

        wa_ref, wv_ref = wb_refs
    cw = cw_ref[...]
    cb = cb_ref[...][None]
    scr[:, SUBLANES - (CONV_W - 1):SUBLANES, :] = e_ref[...]
    for b0 in range(0, nb, sb):
        for r0 in range(0, seq, sr):
            lo = b0 * seq + r0
            x = x_ref[lo:lo + sb * sr, :]
            a = jnp.dot(x, wa_ref[...], preferred_element_type=F32).reshape(sb, sr, tn)
            v = jnp.dot(x, wv_ref[...], preferred_element_type=F32).reshape(sb, sr, tn)
            scr[b0:b0 + sb, SUBLANES + r0:SUBLANES + r0 + sr, :] = a
            c = cb + cw[CONV_W - 1:CONV_W][None] * a
            for j in range(CONV_W - 1):
                first = SUBLANES + r0 - (CONV_W - 1 - j)
                c = c + cw[j:j + 1][None] * scr[b0:b0 + sb, first:first + sr, :]
            o_ref[lo:lo + sb * sr, :] = (jax.nn.gelu(c) * v).reshape(sb * sr, tn).astype(o_ref.dtype)
    st_ref[...] = scr[:, SUBLANES + seq - (CONV_W - 1):SUBLANES + seq, :]


def ffn_in(xb, conv0, w_in, layer, conv_w, conv_b, bsz, seq, nb, tn, sub):
    t, d = xb.shape
    sb, sr = sub
    assert t == bsz * seq and bsz % nb == 0 and D_FF % tn == 0
    assert nb % sb == 0 and seq % sr == 0 and sr % SUBLANES == 0 and (sr == seq or nb == sb == 1)
    n_col = D_FF // tn
    cast_w = w_in.dtype == F32
    rc = (lambda f: lambda j, i: f(i, j)) if cast_w else (lambda f: f)
    col = rc(lambda i, j: (0, j))
    out, st = pl.pallas_call(
        functools.partial(_ffn_in_kernel, nb=nb, seq=seq, sb=sb, sr=sr),
        grid=(n_col, bsz // nb) if cast_w else (bsz // nb, n_col),
        in_specs=[pl.BlockSpec((nb * seq, d), rc(lambda i, j: (i, 0))),
                  pl.BlockSpec((None, d, tn), rc(lambda i, j: (layer, 0, j))),
                  pl.BlockSpec((None, d, tn), rc(lambda i, j: (layer, 0, j + n_col))),
                  pl.BlockSpec((nb, CONV_W - 1, tn), rc(lambda i, j: (i, 0, j))),
                  pl.BlockSpec((CONV_W, tn), col), pl.BlockSpec((1, tn), col)],
        out_specs=[pl.BlockSpec((nb * seq, tn), rc(lambda i, j: (i, j))),
                   pl.BlockSpec((nb, CONV_W - 1, tn), rc(lambda i, j: (i, 0, j)))],
        out_shape=[jax.ShapeDtypeStruct((t, D_FF), BF16),
                   jax.ShapeDtypeStruct((bsz, CONV_W - 1, D_FF), F32)],
        scratch_shapes=[pltpu.VMEM((nb, SUBLANES + seq, tn), F32)] + [pltpu.VMEM((d, tn), BF16)] * (2 * cast_w),
        compiler_params=_params("parallel", "arbitrary" if cast_w else "parallel"),
        name="ffn_in",
    )(xb, w_in, w_in, conv0, conv_w, conv_b.reshape(1, D_FF))
    return out, st


FFN_DOWN_K_STEPS = 4


def _channel_mixer(h, conv0, layer, p, cfg, bsz, seq):
    (xb,) = rmsnorm(h, p["ln_ffn"][layer], (BF16,))
    gated, n_cv = ffn_in(xb, conv0, p["ffn_w_in"], layer, p["ffn_conv_w"][layer], p["ffn_conv_b"][layer],
                         bsz, seq, cfg["ffn_nb"], cfg["ffn_tn"], cfg["ffn_sub"])
    return matmul_residual_split(gated, p["ffn_w_down"], layer, h, FFN_DOWN_K_STEPS), n_cv


def _trunk(x3, s5r, s5i, hg, rw, sh, cv, p, cfg):
    bsz, seq, d = x3.shape
    seq += cfg["front"]
    t = bsz * seq

    if cfg["front"]:
        h, xb = embed_norm(x3, p["meta"], p["ln_mix"][0])
    else:
        h = x3.reshape(t, d)
        (xb,) = rmsnorm(h, p["ln_mix"][0], (BF16,))
    z = matmul([xb], [p["ev_w_in"]], F32).reshape(bsz, seq, EVEN_IN)
    if cfg["s5_tl"]:
        ys5, n_s5r, n_s5i = s5_scan_long(z, s5r[0], s5i[0], *p["s5"], tl=cfg["s5_tl"])
    else:
        ys5, n_s5r, n_s5i = s5_scan(z, s5r[0], s5i[0], *p["s5"], nb=cfg["s5_nb"])
    ys5 = ys5.reshape(t, S5_WIDTH)
    ya = matmul([ys5], [p["s5_w_glu"]], BF16, epilogue="glu", extra=ys5)
    yb, n_hg = hgrn2(z, hg[0], p["hg_lb"], p["hg_norm_w"], nb=cfg["hg_nb"], hh=cfg["hg_hh"])
    h = matmul([ya, yb.reshape(t, -1)], [(p["ev_w_out"], 0, 0), (p["ev_w_out"], 0, 1)], F32,
               epilogue="residual", extra=h)
    h, n_cv0 = _channel_mixer(h, cv[0], 0, p, cfg, bsz, seq)

    mixes, n_sh = norm_mix(h.reshape(bsz, seq, d), sh[0], p["ln_mix"][1], p["rw_mu"], *cfg["mix_blk"])
    xr, xw, xk, xv, xa, xg = (m.reshape(t, d) for m in mixes)
    r = matmul([xr], [p["rw_w_r"]], F32)
    k = matmul([xk], [p["rw_w_k"]], F32)
    v = matmul([xv], [p["rw_w_v"]], F32)
    low = (matmul([xw], [p["rw_w1"]], BF16, act="tanh"), matmul([xa], [p["rw_a1"]], BF16),
           matmul([xg], [p["rw_g1"]], BF16, act="sigmoid"))
    as3 = lambda a: a.reshape(bsz, seq, a.shape[-1])
    yo, n_rw = rwkv7(as3(r), as3(k), as3(v), [as3(x) for x in low], (p["rw_w2"], p["rw_a2"], p["rw_g2"]),
                     rw[0], *p["rw_vec"], nb=cfg["rw_nb"], hp=cfg["rw_hp"])
    h = matmul([yo.reshape(t, d)], [p["rw_w_o"]], F32, epilogue="residual", extra=h)
    h, n_cv1 = _channel_mixer(h, cv[1], 1, p, cfg, bsz, seq)

    (y,) = rmsnorm(h, p["ln_final"], (F32,), rows=(bsz, seq, cfg["front"]) if cfg["front"] else None)
    return (y.reshape(bsz, seq - cfg["front"], d), n_s5r[None], n_s5i[None], n_hg[None], n_rw[None], n_sh[None],
            jnp.stack([n_cv0, n_cv1]))


PROMPT_CFG = dict(front=N_META, s5_tl=344, s5_nb=None, hg_nb=4, hg_hh=8, mix_blk=(1, 344), rw_nb=4, rw_hp=16, ffn_nb=1, ffn_tn=512, ffn_sub=(1, 688))
SAMPLE_CFG = dict(front=0, s5_tl=None, s5_nb=128, hg_nb=8, hg_hh=8, mix_blk=(32, 8), rw_nb=16, rw_hp=4, ffn_nb=128, ffn_tn=512, ffn_sub=(32, 8))


def kernel(x_prompt, x_sample, state_s5_re, state_s5_im, state_hgrn, state_rwkv, state_shift, state_conv, meta_tokens, ln_mix, ln_ffn, ln_final, ev_w_in, ev_w_out, s5_lam_re, s5_lam_im, s5_log_step, s5_b_re, s5_b_im, s5_c_re, s5_c_im, s5_d, s5_w_glu, hg_lb, hg_norm_w, rw_mu, rw_w0, rw_w1, rw_w2, rw_a0, rw_a1, rw_a2, rw_g1, rw_g2, rw_k_k, rw_k_a, rw_r_k, rw_w_r, rw_w_k, rw_w_v, rw_w_o, rw_ln_w, rw_ln_b, ffn_w_in, ffn_conv_w, ffn_conv_b, ffn_w_down):
    bf = lambda w: w.astype(BF16)
    lb_all = hg_lower_bounds(hg_lb)
    pwr, pwi, bbr_t, bbi_t = s5_prep(s5_lam_re[0], s5_lam_im[0], s5_log_step[0], s5_b_re[0], s5_b_im[0])
    wbr, wbi, wcr, wci = _s5_block_weights(bbr_t, bbi_t, s5_c_re[0], s5_c_im[0])
    p = {
        "meta": meta_tokens, "ln_mix": ln_mix, "ln_ffn": ln_ffn, "ln_final": ln_final,
        "ev_w_in": ev_w_in[0],
        "ev_w_out": ev_w_out,
        "s5": (pwr, pwi, wbr, wbi, wcr, wci, s5_d[0].reshape(1, S5_WIDTH)),
        "s5_w_glu": s5_w_glu[0],
        "hg_lb": lb_all[0], "hg_norm_w": hg_norm_w[0],
        "rw_mu": rw_mu[0],
        "rw_w1": rw_w1[0], "rw_w2": bf(rw_w2[0]), "rw_a1": rw_a1[0], "rw_a2": bf(rw_a2[0]),
        "rw_g1": rw_g1[0], "rw_g2": bf(rw_g2[0]),
        "rw_w_r": rw_w_r[0], "rw_w_k": rw_w_k[0], "rw_w_v": rw_w_v[0], "rw_w_o": rw_w_o[0],
        "rw_vec": (rw_w0[0], rw_a0[0], rw_k_k[0], rw_k_a[0], rw_r_k[0].reshape(D_MODEL), rw_ln_w[0], rw_ln_b[0]),
        "ffn_w_in": ffn_w_in, "ffn_conv_w": ffn_conv_w, "ffn_conv_b": ffn_conv_b,
        "ffn_w_down": bf(ffn_w_down),
    }

    bsz = x_prompt.shape[0]
    zeros = lambda *s: jnp.zeros(s, F32)
    outs_p = _trunk(x_prompt,
                    zeros(1, bsz, S5_GROUPS, S5_STATE), zeros(1, bsz, S5_GROUPS, S5_STATE),
                    zeros(1, bsz, HG_HEADS, HG_K, HG_V), zeros(1, bsz, RW_HEADS, RW_HEAD, RW_HEAD),
                    zeros(1, bsz, D_MODEL), zeros(2, bsz, CONV_W - 1, D_FF), p, PROMPT_CFG)
    outs_s = _trunk(x_sample, state_s5_re, state_s5_im, state_hgrn, state_rwkv, state_shift, state_conv,
                    p, SAMPLE_CFG)
    return tuple(outs_p[:1]) + tuple(outs_s[:1]) + tuple(outs_p[1:]) + tuple(outs_s[1:])
```
